```python
import jax, jax.numpy as jnp
from jax import lax
import numpy as np

D_MODEL = 1024
BATCH = 16
SEQ = 2048
DEPTH = 2

MEM_LEN = 256
C_CONV = D_MODEL // 2
CONV_K = 31
C_POOL = D_MODEL // 2
POOL_WINDOWS = (2, 4, 8, 16)
N_POOL_GROUPS = len(POOL_WINDOWS)
POOL_GROUP_DIM = C_POOL // N_POOL_GROUPS
POOL_GROUP_OUT = D_MODEL // N_POOL_GROUPS
N_IN = 2 * C_CONV + C_POOL + 2 * D_MODEL
XA_HEADS = 4
XA_HEAD_DIM = D_MODEL // XA_HEADS
D_FF = 2816
FFN_CONV_K = 3
EPS = 1e-6

kernel_name = "hybrid_conformer_pool_gated_block"


def rms_norm(x, g):
    xf = x.astype(jnp.float32)
    y = xf * lax.rsqrt(jnp.mean(xf * xf, axis=-1, keepdims=True) + EPS)
    return (y * g.astype(jnp.float32)).astype(x.dtype)


def layer_norm(x, g, b):
    xf = x.astype(jnp.float32)
    mu = jnp.mean(xf, axis=-1, keepdims=True)
    xc = xf - mu
    var = jnp.mean(xc * xc, axis=-1, keepdims=True)
    y = xc * lax.rsqrt(var + EPS) * g.astype(jnp.float32) + b.astype(jnp.float32)
    return y.astype(x.dtype)


def causal_dwconv(u, w):
    k = w.shape[0]
    return lax.conv_general_dilated(
        u, w[:, None, :].astype(u.dtype), window_strides=(1,), padding=[(k - 1, 0)],
        dimension_numbers=("NWC", "WIO", "NWC"), feature_group_count=u.shape[-1])


def multiscale_pool(u):
    b, s, _ = u.shape
    uf = u.astype(jnp.float32).reshape(b, s, N_POOL_GROUPS, POOL_GROUP_DIM)
    cs = jnp.cumsum(uf, axis=1)
    t = jnp.arange(s)
    outs = []
    for g, w in enumerate(POOL_WINDOWS):
        c = cs[:, :, g]
        lag = jnp.pad(c, ((0, 0), (w, 0), (0, 0)))[:, :s]
        cnt = jnp.minimum(t + 1, w).astype(jnp.float32)[None, :, None]
        outs.append((c - lag) / cnt - uf[:, :, g])
    return jnp.stack(outs, axis=2).astype(u.dtype)


def _fwd_setup_inputs(seed: int = 0) -> dict:
    key = jax.random.key(seed)
    ks = jax.random.split(key, 24)
    f32 = jnp.float32
    nrm = lambda k, shape, scale: jax.random.normal(k, shape, f32) * scale
    gain = lambda k, shape: 1.0 + 0.05 * jax.random.normal(k, shape, f32)
    return {
        "x": jax.random.normal(ks[0], (BATCH, SEQ, D_MODEL), f32),
        "mem": jax.random.normal(ks[1], (BATCH, MEM_LEN, D_MODEL), f32),
        "mix_norm_g": gain(ks[2], (DEPTH, D_MODEL)),
        "w_in": nrm(ks[3], (DEPTH, D_MODEL, N_IN), D_MODEL ** -0.5),
        "conv_dw_w": nrm(ks[4], (DEPTH, CONV_K, C_CONV), CONV_K ** -0.5),
        "conv_dw_b": nrm(ks[5], (DEPTH, C_CONV), 0.02),
        "conv_ln_g": gain(ks[6], (DEPTH, C_CONV)),
        "conv_ln_b": nrm(ks[7], (DEPTH, C_CONV), 0.02),
        "w_conv_out": nrm(ks[8], (DEPTH, C_CONV, D_MODEL), C_CONV ** -0.5),
        "w_pool_grp": nrm(ks[9], (DEPTH, N_POOL_GROUPS, POOL_GROUP_DIM, POOL_GROUP_OUT), POOL_GROUP_DIM ** -0.5),
        "pool_scale": gain(ks[10], (DEPTH, D_MODEL)),
        "w_out": nrm(ks[11], (DEPTH, D_MODEL, D_MODEL), D_MODEL ** -0.5),
        "xattn_norm_g": gain(ks[12], (DEPTH, D_MODEL)),
        "mem_norm_g": gain(ks[13], (D_MODEL,)),
        "w_q": nrm(ks[14], (DEPTH, D_MODEL, D_MODEL), D_MODEL ** -0.5),
        "w_kv": nrm(ks[15], (DEPTH, D_MODEL, 2 * D_MODEL), D_MODEL ** -0.5),
        "w_o": nrm(ks[16], (DEPTH, D_MODEL, D_MODEL), D_MODEL ** -0.5),
        "ffn_norm_g": gain(ks[17], (DEPTH, D_MODEL)),
        "w_up": nrm(ks[18], (DEPTH, D_MODEL, 2 * D_FF), D_MODEL ** -0.5),
        "ffn_dw_w": nrm(ks[19], (DEPTH, FFN_CONV_K, 2 * D_FF), FFN_CONV_K ** -0.5),
        "w_down": nrm(ks[20], (DEPTH, D_FF, D_MODEL), D_FF ** -0.5),
        "final_norm_g": gain(ks[21], (D_MODEL,)),
    }


def _fwd_reference(x, mem, mix_norm_g, w_in, conv_dw_w, conv_dw_b, conv_ln_g, conv_ln_b, w_conv_out,
              w_pool_grp, pool_scale, w_out, xattn_norm_g, mem_norm_g, w_q, w_kv, w_o,
              ffn_norm_g, w_up, ffn_dw_w, w_down, final_norm_g):
    b, s, d = x.shape
    m_len = mem.shape[1]
    mem_n = rms_norm(mem, mem_norm_g)
    split_at = [C_CONV, 2 * C_CONV, 2 * C_CONV + C_POOL, 2 * C_CONV + C_POOL + D_MODEL]
    xa_scale = XA_HEAD_DIM ** -0.5
    for l in range(DEPTH):
        h = rms_norm(x, mix_norm_g[l])
        proj = h @ w_in[l]
        a, gl, u_pool, g_conv, g_pool = jnp.split(proj, split_at, axis=-1)
        yc = a * jax.nn.sigmoid(gl)
        yc = causal_dwconv(yc, conv_dw_w[l]) + conv_dw_b[l]
        yc = jax.nn.silu(layer_norm(yc, conv_ln_g[l], conv_ln_b[l]))
        yc = yc @ w_conv_out[l]
        zp = multiscale_pool(u_pool)
        yp = jnp.einsum("bsgc,gcd->bsgd", zp, w_pool_grp[l]).reshape(b, s, d) * pool_scale[l]
        merged = jax.nn.sigmoid(g_conv) * yc + jax.nn.sigmoid(g_pool) * yp
        x = x + merged @ w_out[l]
        hq = rms_norm(x, xattn_norm_g[l])
        q = (hq @ w_q[l]).reshape(b, s, XA_HEADS, XA_HEAD_DIM)
        kv = mem_n @ w_kv[l]
        k, v = jnp.split(kv, 2, axis=-1)
        k = k.reshape(b, m_len, XA_HEADS, XA_HEAD_DIM)
        v = v.reshape(b, m_len, XA_HEADS, XA_HEAD_DIM)
        sc = jnp.einsum("bshd,bmhd->bhsm", q, k).astype(jnp.float32) * xa_scale
        pr = jax.nn.softmax(sc, axis=-1).astype(v.dtype)
        att = jnp.einsum("bhsm,bmhd->bshd", pr, v).reshape(b, s, d)
        x = x + att @ w_o[l]
        hf = rms_norm(x, ffn_norm_g[l])
        up = causal_dwconv(hf @ w_up[l], ffn_dw_w[l])
        gate, val = jnp.split(up, 2, axis=-1)
        x = x + (jax.nn.gelu(gate) * val) @ w_down[l]
    return rms_norm(x, final_norm_g)


import jax as _jax
import jax.numpy as _jnp

TWIN_FORMAT = 'train_step'
FWD_PARAMS = ['x', 'mem', 'mix_norm_g', 'w_in', 'conv_dw_w', 'conv_dw_b', 'conv_ln_g', 'conv_ln_b', 'w_conv_out', 'w_pool_grp', 'pool_scale', 'w_out', 'xattn_norm_g', 'mem_norm_g', 'w_q', 'w_kv', 'w_o', 'ffn_norm_g', 'w_up', 'ffn_dw_w', 'w_down', 'final_norm_g']
TWIN_WEIGHTS = ['mix_norm_g', 'w_in', 'conv_dw_w', 'conv_dw_b', 'conv_ln_g', 'conv_ln_b', 'w_conv_out', 'w_pool_grp', 'pool_scale', 'w_out', 'xattn_norm_g', 'mem_norm_g', 'w_q', 'w_kv', 'w_o', 'ffn_norm_g', 'w_up', 'ffn_dw_w', 'w_down', 'final_norm_g']
TWIN_DIFF_INPUT = 'x'
TWIN_INPUTS = ['x', 'mem', 'mix_norm_g', 'w_in', 'conv_dw_w', 'conv_dw_b', 'conv_ln_g', 'conv_ln_b', 'w_conv_out', 'w_pool_grp', 'pool_scale', 'w_out', 'xattn_norm_g', 'mem_norm_g', 'w_q', 'w_kv', 'w_o', 'ffn_norm_g', 'w_up', 'ffn_dw_w', 'w_down', 'final_norm_g', 'loss_target', 'm_mix_norm_g', 'm_w_in', 'm_conv_dw_w', 'm_conv_dw_b', 'm_conv_ln_g', 'm_conv_ln_b', 'm_w_conv_out', 'm_w_pool_grp', 'm_pool_scale', 'm_w_out', 'm_xattn_norm_g', 'm_mem_norm_g', 'm_w_q', 'm_w_kv', 'm_w_o', 'm_ffn_norm_g', 'm_w_up', 'm_ffn_dw_w', 'm_w_down', 'm_final_norm_g', 'v_mix_norm_g', 'v_w_in', 'v_conv_dw_w', 'v_conv_dw_b', 'v_conv_ln_g', 'v_conv_ln_b', 'v_w_conv_out', 'v_w_pool_grp', 'v_pool_scale', 'v_w_out', 'v_xattn_norm_g', 'v_mem_norm_g', 'v_w_q', 'v_w_kv', 'v_w_o', 'v_ffn_norm_g', 'v_w_up', 'v_ffn_dw_w', 'v_w_down', 'v_final_norm_g']
TWIN_OUTPUTS = ['loss', 'grad_x', 'grad_mix_norm_g', 'grad_w_in', 'grad_conv_dw_w', 'grad_conv_dw_b', 'grad_conv_ln_g', 'grad_conv_ln_b', 'grad_w_conv_out', 'grad_w_pool_grp', 'grad_pool_scale', 'grad_w_out', 'grad_xattn_norm_g', 'grad_mem_norm_g', 'grad_w_q', 'grad_w_kv', 'grad_w_o', 'grad_ffn_norm_g', 'grad_w_up', 'grad_ffn_dw_w', 'grad_w_down', 'grad_final_norm_g', 'delta_mix_norm_g', 'delta_w_in', 'delta_conv_dw_w', 'delta_conv_dw_b', 'delta_conv_ln_g', 'delta_conv_ln_b', 'delta_w_conv_out', 'delta_w_pool_grp', 'delta_pool_scale', 'delta_w_out', 'delta_xattn_norm_g', 'delta_mem_norm_g', 'delta_w_q', 'delta_w_kv', 'delta_w_o', 'delta_ffn_norm_g', 'delta_w_up', 'delta_ffn_dw_w', 'delta_w_down', 'delta_final_norm_g', 'new_m_mix_norm_g', 'new_m_w_in', 'new_m_conv_dw_w', 'new_m_conv_dw_b', 'new_m_conv_ln_g', 'new_m_conv_ln_b', 'new_m_w_conv_out', 'new_m_w_pool_grp', 'new_m_pool_scale', 'new_m_w_out', 'new_m_xattn_norm_g', 'new_m_mem_norm_g', 'new_m_w_q', 'new_m_w_kv', 'new_m_w_o', 'new_m_ffn_norm_g', 'new_m_w_up', 'new_m_ffn_dw_w', 'new_m_w_down', 'new_m_final_norm_g', 'new_v_mix_norm_g', 'new_v_w_in', 'new_v_conv_dw_w', 'new_v_conv_dw_b', 'new_v_conv_ln_g', 'new_v_conv_ln_b', 'new_v_w_conv_out', 'new_v_w_pool_grp', 'new_v_pool_scale', 'new_v_w_out', 'new_v_xattn_norm_g', 'new_v_mem_norm_g', 'new_v_w_q', 'new_v_w_kv', 'new_v_w_o', 'new_v_ffn_norm_g', 'new_v_w_up', 'new_v_ffn_dw_w', 'new_v_w_down', 'new_v_final_norm_g']
TWIN_LEAF_KINDS = {'loss': 'loss', 'grad_x': 'grad_x', 'grad_mix_norm_g': 'grad_w', 'grad_w_in': 'grad_w', 'grad_conv_dw_w': 'grad_w', 'grad_conv_dw_b': 'grad_w', 'grad_conv_ln_g': 'grad_w', 'grad_conv_ln_b': 'grad_w', 'grad_w_conv_out': 'grad_w', 'grad_w_pool_grp': 'grad_w', 'grad_pool_scale': 'grad_w', 'grad_w_out': 'grad_w', 'grad_xattn_norm_g': 'grad_w', 'grad_mem_norm_g': 'grad_w', 'grad_w_q': 'grad_w', 'grad_w_kv': 'grad_w', 'grad_w_o': 'grad_w', 'grad_ffn_norm_g': 'grad_w', 'grad_w_up': 'grad_w', 'grad_ffn_dw_w': 'grad_w', 'grad_w_down': 'grad_w', 'grad_final_norm_g': 'grad_w', 'delta_mix_norm_g': 'delta_w', 'delta_w_in': 'delta_w', 'delta_conv_dw_w': 'delta_w', 'delta_conv_dw_b': 'delta_w', 'delta_conv_ln_g': 'delta_w', 'delta_conv_ln_b': 'delta_w', 'delta_w_conv_out': 'delta_w', 'delta_w_pool_grp': 'delta_w', 'delta_pool_scale': 'delta_w', 'delta_w_out': 'delta_w', 'delta_xattn_norm_g': 'delta_w', 'delta_mem_norm_g': 'delta_w', 'delta_w_q': 'delta_w', 'delta_w_kv': 'delta_w', 'delta_w_o': 'delta_w', 'delta_ffn_norm_g': 'delta_w', 'delta_w_up': 'delta_w', 'delta_ffn_dw_w': 'delta_w', 'delta_w_down': 'delta_w', 'delta_final_norm_g': 'delta_w', 'new_m_mix_norm_g': 'new_m', 'new_m_w_in': 'new_m', 'new_m_conv_dw_w': 'new_m', 'new_m_conv_dw_b': 'new_m', 'new_m_conv_ln_g': 'new_m', 'new_m_conv_ln_b': 'new_m', 'new_m_w_conv_out': 'new_m', 'new_m_w_pool_grp': 'new_m', 'new_m_pool_scale': 'new_m', 'new_m_w_out': 'new_m', 'new_m_xattn_norm_g': 'new_m', 'new_m_mem_norm_g': 'new_m', 'new_m_w_q': 'new_m', 'new_m_w_kv': 'new_m', 'new_m_w_o': 'new_m', 'new_m_ffn_norm_g': 'new_m', 'new_m_w_up': 'new_m', 'new_m_ffn_dw_w': 'new_m', 'new_m_w_down': 'new_m', 'new_m_final_norm_g': 'new_m', 'new_v_mix_norm_g': 'new_v', 'new_v_w_in': 'new_v', 'new_v_conv_dw_w': 'new_v', 'new_v_conv_dw_b': 'new_v', 'new_v_conv_ln_g': 'new_v', 'new_v_conv_ln_b': 'new_v', 'new_v_w_conv_out': 'new_v', 'new_v_w_pool_grp': 'new_v', 'new_v_pool_scale': 'new_v', 'new_v_w_out': 'new_v', 'new_v_xattn_norm_g': 'new_v', 'new_v_mem_norm_g': 'new_v', 'new_v_w_q': 'new_v', 'new_v_w_kv': 'new_v', 'new_v_w_o': 'new_v', 'new_v_ffn_norm_g': 'new_v', 'new_v_w_up': 'new_v', 'new_v_ffn_dw_w': 'new_v', 'new_v_w_down': 'new_v', 'new_v_final_norm_g': 'new_v'}


def _forward(args):
    return _fwd_reference(*[args[k] for k in FWD_PARAMS])


def _output_shape():
    out = _jax.eval_shape(lambda: _forward(_fwd_setup_inputs(0)))
    return out.shape, out.dtype

N_MICROBATCH = 1
ADAM_LR = 0.001
ADAM_B1 = 0.9
ADAM_B2 = 0.999
ADAM_EPS = 1e-08
ADAM_WD = 0.01
ADAM_STEP = 10
PER_EXAMPLE_BATCH_AXIS = {'x': 0, 'mem': 0, 'loss_target': 0}
SHARED_INPUTS = []
_WEIGHT_DTYPES = {'mix_norm_g': _jnp.float32, 'w_in': _jnp.float32, 'conv_dw_w': _jnp.float32, 'conv_dw_b': _jnp.float32, 'conv_ln_g': _jnp.float32, 'conv_ln_b': _jnp.float32, 'w_conv_out': _jnp.float32, 'w_pool_grp': _jnp.float32, 'pool_scale': _jnp.float32, 'w_out': _jnp.float32, 'xattn_norm_g': _jnp.float32, 'mem_norm_g': _jnp.float32, 'w_q': _jnp.float32, 'w_kv': _jnp.float32, 'w_o': _jnp.float32, 'ffn_norm_g': _jnp.float32, 'w_up': _jnp.float32, 'ffn_dw_w': _jnp.float32, 'w_down': _jnp.float32, 'final_norm_g': _jnp.float32}
MOMENT_SCALE = {'mix_norm_g': 1.059221e-01, 'w_in': 5.712408e-02, 'conv_dw_w': 7.812779e-02, 'conv_dw_b': 1.706568e-01, 'conv_ln_g': 9.217990e-02, 'conv_ln_b': 8.382759e-02, 'w_conv_out': 5.452627e-02, 'w_pool_grp': 8.046468e-02, 'pool_scale': 7.904801e-02, 'w_out': 9.686945e-02, 'xattn_norm_g': 1.753657e-02, 'mem_norm_g': 3.770548e-02, 'w_q': 1.742336e-02, 'w_kv': 1.750432e-02, 'w_o': 1.790350e-02, 'ffn_norm_g': 1.236247e-01, 'w_up': 5.332878e-02, 'ffn_dw_w': 5.300412e-02, 'w_down': 8.814386e-02, 'final_norm_g': 3.205669e+01}


def _to_microbatches(a, axis):
    t = _jnp.moveaxis(a, axis, 0)
    t = t.reshape((N_MICROBATCH, t.shape[0] // N_MICROBATCH) + t.shape[1:])
    return _jnp.moveaxis(t, 1, axis + 1)


def setup_inputs(seed: int = 0) -> dict:
    inp = _fwd_setup_inputs(seed)
    key = _jax.random.fold_in(_jax.random.key(seed), 7919)
    shape, _ = _output_shape()
    out = dict(inp)
    out["loss_target"] = _jax.random.normal(_jax.random.fold_in(key, 0), shape, _jnp.float32)
    for i, name in enumerate(TWIN_WEIGHTS):
        w = inp[name].astype(_jnp.float32)
        if MOMENT_SCALE is None:
            s = _jnp.sqrt(_jnp.mean(_jnp.square(w)) + 1e-30)
        else:
            s = MOMENT_SCALE[name]
        km, kv = _jax.random.split(_jax.random.fold_in(key, i + 1))
        out[name] = w
        out["m_" + name] = s * _jax.random.normal(km, w.shape, _jnp.float32)
        out["v_" + name] = (s * s) * _jax.random.uniform(kv, w.shape, _jnp.float32, 0.5, 1.5)
    if N_MICROBATCH > 1:
        for name, axis in PER_EXAMPLE_BATCH_AXIS.items():
            out[name] = _to_microbatches(out[name], axis)
    return {'x': out['x'], 'mem': out['mem'], 'mix_norm_g': out['mix_norm_g'], 'w_in': out['w_in'], 'conv_dw_w': out['conv_dw_w'], 'conv_dw_b': out['conv_dw_b'], 'conv_ln_g': out['conv_ln_g'], 'conv_ln_b': out['conv_ln_b'], 'w_conv_out': out['w_conv_out'], 'w_pool_grp': out['w_pool_grp'], 'pool_scale': out['pool_scale'], 'w_out': out['w_out'], 'xattn_norm_g': out['xattn_norm_g'], 'mem_norm_g': out['mem_norm_g'], 'w_q': out['w_q'], 'w_kv': out['w_kv'], 'w_o': out['w_o'], 'ffn_norm_g': out['ffn_norm_g'], 'w_up': out['w_up'], 'ffn_dw_w': out['ffn_dw_w'], 'w_down': out['w_down'], 'final_norm_g': out['final_norm_g'], 'loss_target': out['loss_target'], 'm_mix_norm_g': out['m_mix_norm_g'], 'm_w_in': out['m_w_in'], 'm_conv_dw_w': out['m_conv_dw_w'], 'm_conv_dw_b': out['m_conv_dw_b'], 'm_conv_ln_g': out['m_conv_ln_g'], 'm_conv_ln_b': out['m_conv_ln_b'], 'm_w_conv_out': out['m_w_conv_out'], 'm_w_pool_grp': out['m_w_pool_grp'], 'm_pool_scale': out['m_pool_scale'], 'm_w_out': out['m_w_out'], 'm_xattn_norm_g': out['m_xattn_norm_g'], 'm_mem_norm_g': out['m_mem_norm_g'], 'm_w_q': out['m_w_q'], 'm_w_kv': out['m_w_kv'], 'm_w_o': out['m_w_o'], 'm_ffn_norm_g': out['m_ffn_norm_g'], 'm_w_up': out['m_w_up'], 'm_ffn_dw_w': out['m_ffn_dw_w'], 'm_w_down': out['m_w_down'], 'm_final_norm_g': out['m_final_norm_g'], 'v_mix_norm_g': out['v_mix_norm_g'], 'v_w_in': out['v_w_in'], 'v_conv_dw_w': out['v_conv_dw_w'], 'v_conv_dw_b': out['v_conv_dw_b'], 'v_conv_ln_g': out['v_conv_ln_g'], 'v_conv_ln_b': out['v_conv_ln_b'], 'v_w_conv_out': out['v_w_conv_out'], 'v_w_pool_grp': out['v_w_pool_grp'], 'v_pool_scale': out['v_pool_scale'], 'v_w_out': out['v_w_out'], 'v_xattn_norm_g': out['v_xattn_norm_g'], 'v_mem_norm_g': out['v_mem_norm_g'], 'v_w_q': out['v_w_q'], 'v_w_kv': out['v_w_kv'], 'v_w_o': out['v_w_o'], 'v_ffn_norm_g': out['v_ffn_norm_g'], 'v_w_up': out['v_w_up'], 'v_ffn_dw_w': out['v_ffn_dw_w'], 'v_w_down': out['v_w_down'], 'v_final_norm_g': out['v_final_norm_g']}


def _loss(weights, diff, rest, loss_target):
    with _jax.named_scope("forward"):
        args = {**rest, TWIN_DIFF_INPUT: diff, **{k: w.astype(_WEIGHT_DTYPES[k]) for k, w in weights.items()}}
        y = _forward(args)
    with _jax.named_scope("loss_head"):
        err = _jnp.square(y.astype(_jnp.float32) - loss_target)
        return 0.5 * _jnp.sum(_jnp.mean(err, axis=-1)) if err.ndim else 0.5 * err


def _adamw(w, g, m, v):
    m = ADAM_B1 * m + (1.0 - ADAM_B1) * g
    v = ADAM_B2 * v + (1.0 - ADAM_B2) * _jnp.square(g)
    m_hat = m / (1.0 - ADAM_B1 ** ADAM_STEP)
    v_hat = v / (1.0 - ADAM_B2 ** ADAM_STEP)
    delta = -ADAM_LR * (m_hat / (_jnp.sqrt(v_hat) + ADAM_EPS) + ADAM_WD * w)
    return delta, m, v


def reference(x, mem, mix_norm_g, w_in, conv_dw_w, conv_dw_b, conv_ln_g, conv_ln_b, w_conv_out, w_pool_grp, pool_scale, w_out, xattn_norm_g, mem_norm_g, w_q, w_kv, w_o, ffn_norm_g, w_up, ffn_dw_w, w_down, final_norm_g, loss_target, m_mix_norm_g, m_w_in, m_conv_dw_w, m_conv_dw_b, m_conv_ln_g, m_conv_ln_b, m_w_conv_out, m_w_pool_grp, m_pool_scale, m_w_out, m_xattn_norm_g, m_mem_norm_g, m_w_q, m_w_kv, m_w_o, m_ffn_norm_g, m_w_up, m_ffn_dw_w, m_w_down, m_final_norm_g, v_mix_norm_g, v_w_in, v_conv_dw_w, v_conv_dw_b, v_conv_ln_g, v_conv_ln_b, v_w_conv_out, v_w_pool_grp, v_pool_scale, v_w_out, v_xattn_norm_g, v_mem_norm_g, v_w_q, v_w_kv, v_w_o, v_ffn_norm_g, v_w_up, v_ffn_dw_w, v_w_down, v_final_norm_g):
    given = dict(x=x, mem=mem, mix_norm_g=mix_norm_g, w_in=w_in, conv_dw_w=conv_dw_w, conv_dw_b=conv_dw_b, conv_ln_g=conv_ln_g, conv_ln_b=conv_ln_b, w_conv_out=w_conv_out, w_pool_grp=w_pool_grp, pool_scale=pool_scale, w_out=w_out, xattn_norm_g=xattn_norm_g, mem_norm_g=mem_norm_g, w_q=w_q, w_kv=w_kv, w_o=w_o, ffn_norm_g=ffn_norm_g, w_up=w_up, ffn_dw_w=ffn_dw_w, w_down=w_down, final_norm_g=final_norm_g, loss_target=loss_target, m_mix_norm_g=m_mix_norm_g, m_w_in=m_w_in, m_conv_dw_w=m_conv_dw_w, m_conv_dw_b=m_conv_dw_b, m_conv_ln_g=m_conv_ln_g, m_conv_ln_b=m_conv_ln_b, m_w_conv_out=m_w_conv_out, m_w_pool_grp=m_w_pool_grp, m_pool_scale=m_pool_scale, m_w_out=m_w_out, m_xattn_norm_g=m_xattn_norm_g, m_mem_norm_g=m_mem_norm_g, m_w_q=m_w_q, m_w_kv=m_w_kv, m_w_o=m_w_o, m_ffn_norm_g=m_ffn_norm_g, m_w_up=m_w_up, m_ffn_dw_w=m_ffn_dw_w, m_w_down=m_w_down, m_final_norm_g=m_final_norm_g, v_mix_norm_g=v_mix_norm_g, v_w_in=v_w_in, v_conv_dw_w=v_conv_dw_w, v_conv_dw_b=v_conv_dw_b, v_conv_ln_g=v_conv_ln_g, v_conv_ln_b=v_conv_ln_b, v_w_conv_out=v_w_conv_out, v_w_pool_grp=v_w_pool_grp, v_pool_scale=v_pool_scale, v_w_out=v_w_out, v_xattn_norm_g=v_xattn_norm_g, v_mem_norm_g=v_mem_norm_g, v_w_q=v_w_q, v_w_kv=v_w_kv, v_w_o=v_w_o, v_ffn_norm_g=v_ffn_norm_g, v_w_up=v_w_up, v_ffn_dw_w=v_ffn_dw_w, v_w_down=v_w_down, v_final_norm_g=v_final_norm_g)
    weights = {n: given[n] for n in TWIN_WEIGHTS}
    shared = {n: given[n] for n in SHARED_INPUTS}
    per_example = {n: given[n] for n in ['x', 'mem']}
    grad_fn = _jax.value_and_grad(_loss, argnums=(0, 1))

    def one_microbatch(ex, loss_target):
        ex = dict(ex)
        diff = ex.pop(TWIN_DIFF_INPUT)
        return grad_fn(weights, diff, {**shared, **ex}, loss_target)

    if N_MICROBATCH == 1:
        loss, (grad_w, grad_x) = one_microbatch(per_example, given["loss_target"])
    else:
        def body(carry, xs):
            loss_sum, grad_sum = carry
            l_k, (gw_k, gx_k) = one_microbatch(xs[0], xs[1])
            with _jax.named_scope("update"):
                return (loss_sum + l_k, _jax.tree.map(_jnp.add, grad_sum, gw_k)), gx_k

        init = (_jnp.zeros((), _jnp.float32), _jax.tree.map(_jnp.zeros_like, weights))
        (loss, grad_w), grad_x = _jax.lax.scan(body, init, (per_example, given["loss_target"]))
    with _jax.named_scope("update"):
        delta_w, new_m, new_v = {}, {}, {}
        for n in TWIN_WEIGHTS:
            delta_w[n], new_m[n], new_v[n] = _adamw(weights[n], grad_w[n], given["m_" + n], given["v_" + n])
    return (loss, grad_x, *[grad_w[n] for n in TWIN_WEIGHTS], *[delta_w[n] for n in TWIN_WEIGHTS],
            *[new_m[n] for n in TWIN_WEIGHTS], *[new_v[n] for n in TWIN_WEIGHTS])
```

```python
import functools

import jax
import jax.numpy as jnp
from jax import lax
from jax.experimental import pallas as pl
from jax.experimental.pallas import tpu as pltpu

F32 = jnp.float32
BF16 = jnp.bfloat16
MESH = pl.DeviceIdType.MESH

N_DEV = 8
EPS = 1e-6
V7X_VMEM_BYTES = 64 * 1024 * 1024
VMEM_LIMIT = (V7X_VMEM_BYTES * 3) // 4
LANE = 128
SUBLANE = 8

CONV_HALO = 32
POOL_HALO = 16
FFN_HALO = 8
POOL_WINDOW_MAX = 16
XA_HEADS = 4

ADAM_LR = 0.001
ADAM_B1 = 0.9
ADAM_B2 = 0.999
ADAM_EPS = 1e-08
ADAM_WD = 0.01
ADAM_STEP = 10

GELU_C0 = 0.7978845608028654
GELU_C1 = 0.044715


def _tile(n, cap, mult=LANE):
    if n <= cap:
        return n
    best = None
    for d in range(mult, cap + 1, mult):
        if n % d == 0:
            best = d
    assert best is not None, (n, cap, mult)
    return best


def _params(*sem):
    return pltpu.CompilerParams(dimension_semantics=sem, vmem_limit_bytes=VMEM_LIMIT)


def _sig(x):
    return 1.0 / (1.0 + jnp.exp(-x))


def _bs(shape, imap):
    return pl.BlockSpec(shape, imap)


def _matmul(a, b, mode, name, res=None, out_dtype=F32):
    if mode == "tn":
        k_dim, m_dim = a.shape
        k2, n_dim = b.shape
    elif mode == "nn":
        m_dim, k_dim = a.shape
        k2, n_dim = b.shape
    else:
        m_dim, k_dim = a.shape
        n_dim, k2 = b.shape
    assert k_dim == k2, (name, a.shape, b.shape)
    tm = _tile(m_dim, 1024)
    tn = _tile(n_dim, 1408)
    tk = _tile(k_dim, 512 if mode == "tn" else 1024)
    nk = k_dim // tk
    if mode == "tn":
        a_spec, ca = _bs((tk, tm), lambda i, j, k: (k, i)), 0
    else:
        a_spec, ca = _bs((tm, tk), lambda i, j, k: (i, k)), 1
    if mode == "nt":
        b_spec, cb = _bs((tn, tk), lambda i, j, k: (j, k)), 1
    else:
        b_spec, cb = _bs((tk, tn), lambda i, j, k: (k, j)), 0
    dims = (((ca,), (cb,)), ((), ()))
    o_spec = _bs((tm, tn), lambda i, j, k: (i, j))
    has_res = res is not None

    def body(*refs):
        if has_res:
            a_ref, b_ref, r_ref, o_ref, acc = refs
        else:
            a_ref, b_ref, o_ref, acc = refs
            r_ref = None
        k = pl.program_id(2)
        part = lax.dot_general(a_ref[...].astype(BF16), b_ref[...].astype(BF16), dims,
                               preferred_element_type=F32)

        @pl.when(k == 0)
        def _():
            acc[...] = part

        @pl.when(k > 0)
        def _():
            acc[...] += part

        @pl.when(k == nk - 1)
        def _():
            o = acc[...]
            if has_res:
                o = o + r_ref[...].astype(F32)
            o_ref[...] = o.astype(out_dtype)

    in_specs = [a_spec, b_spec] + ([o_spec] if has_res else [])
    args = (a, b) + ((res,) if has_res else ())
    return pl.pallas_call(
        body, out_shape=jax.ShapeDtypeStruct((m_dim, n_dim), out_dtype),
        grid=(m_dim // tm, n_dim // tn, nk), in_specs=in_specs, out_specs=o_spec,
        scratch_shapes=[pltpu.VMEM((tm, tn), F32)], name=name,
        compiler_params=_params("parallel", "parallel", "arbitrary"))(*args)


def _grouped(a, w, mode, name, out_dtype=F32):
    t_dim = a.shape[0]
    g_dim, r_dim, c_dim = w.shape
    ka, no = (c_dim, r_dim) if mode == "nt" else (r_dim, c_dim)
    tm = _tile(t_dim, 512)
    dims = (((1,), (1 if mode == "nt" else 0,)), ((), ()))

    def body(a_ref, w_ref, o_ref):
        o_ref[...] = lax.dot_general(a_ref[...].astype(BF16), w_ref[...].astype(BF16), dims,
                                     preferred_element_type=F32).astype(out_dtype)

    return pl.pallas_call(
        body, out_shape=jax.ShapeDtypeStruct((t_dim, g_dim * no), out_dtype),
        grid=(t_dim // tm, g_dim),
        in_specs=[_bs((tm, ka), lambda i, g: (i, g)), _bs((None, r_dim, c_dim), lambda i, g: (g, 0, 0))],
        out_specs=_bs((tm, no), lambda i, g: (i, g)), name=name,
        compiler_params=_params("parallel", "parallel"))(a, w)


def _grouped_tn(a, b, g_dim, name):
    t_dim = a.shape[0]
    ra = a.shape[1] // g_dim
    cb = b.shape[1] // g_dim
    tm = _tile(t_dim, 512)
    nt = t_dim // tm

    def body(a_ref, b_ref, o_ref):
        part = lax.dot_general(a_ref[...].astype(BF16), b_ref[...].astype(BF16), (((0,), (0,)), ((), ())),
                               preferred_element_type=F32)

        @pl.when(pl.program_id(1) == 0)
        def _():
            o_ref[...] = part

        @pl.when(pl.program_id(1) > 0)
        def _():
            o_ref[...] += part

    return pl.pallas_call(
        body, out_shape=jax.ShapeDtypeStruct((g_dim, ra, cb), F32), grid=(g_dim, nt),
        in_specs=[_bs((tm, ra), lambda g, i: (i, g)), _bs((tm, cb), lambda g, i: (i, g))],
        out_specs=_bs((None, ra, cb), lambda g, i: (g, 0, 0)), name=name,
        compiler_params=_params("parallel", "arbitrary"))(a, b)


def _rmsnorm_fwd(x, g, name):
    t_dim, d = x.shape
    tm = _tile(t_dim, 512)

    def body(x_ref, g_ref, o_ref):
        xv = x_ref[...]
        r = lax.rsqrt(jnp.mean(xv * xv, axis=-1, keepdims=True) + EPS)
        o_ref[...] = (xv * r * g_ref[...]).astype(BF16)

    return pl.pallas_call(
        body, out_shape=jax.ShapeDtypeStruct((t_dim, d), BF16), grid=(t_dim // tm,),
        in_specs=[_bs((tm, d), lambda i: (i, 0)), _bs((1, d), lambda i: (0, 0))],
        out_specs=_bs((tm, d), lambda i: (i, 0)), name=name, compiler_params=_params("parallel"))(x, g)


def _rmsnorm_bwd(x, g, dh, dx_in, name):
    t_dim, d = x.shape
    tm = _tile(t_dim, 512)
    has_in = dx_in is not None

    def body(*refs):
        if has_in:
            x_ref, g_ref, dh_ref, di_ref, dx_ref, dg_ref = refs
        else:
            x_ref, g_ref, dh_ref, dx_ref, dg_ref = refs
        xv = x_ref[...]
        r = lax.rsqrt(jnp.mean(xv * xv, axis=-1, keepdims=True) + EPS)
        xh = xv * r
        dhv = dh_ref[...].astype(F32)
        dxh = dhv * g_ref[...]
        dx = r * (dxh - xh * jnp.mean(dxh * xh, axis=-1, keepdims=True))
        if has_in:
            dx = dx + di_ref[...]
        dx_ref[...] = dx
        part = jnp.sum(dhv * xh, axis=0, keepdims=True)

        @pl.when(pl.program_id(0) == 0)
        def _():
            dg_ref[...] = part

        @pl.when(pl.program_id(0) > 0)
        def _():
            dg_ref[...] += part

    row = _bs((tm, d), lambda i: (i, 0))
    vec = _bs((1, d), lambda i: (0, 0))
    args = (x, g, dh) + ((dx_in,) if has_in else ())
    return pl.pallas_call(
        body, out_shape=(jax.ShapeDtypeStruct((t_dim, d), F32), jax.ShapeDtypeStruct((1, d), F32)),
        grid=(t_dim // tm,), in_specs=[row, vec, row] + ([row] if has_in else []),
        out_specs=(row, vec), name=name, compiler_params=_params("arbitrary"))(*args)


def _loss_head(x, g, tgt, name):
    t_dim, d = x.shape
    tm = _tile(t_dim, 512)

    def body(x_ref, g_ref, t_ref, dx_ref, dg_ref, loss_ref):
        xv = x_ref[...]
        gv = g_ref[...]
        r = lax.rsqrt(jnp.mean(xv * xv, axis=-1, keepdims=True) + EPS)
        xh = xv * r
        err = xh * gv - t_ref[...]
        dy = err * (1.0 / d)
        dxh = dy * gv
        dx_ref[...] = r * (dxh - xh * jnp.mean(dxh * xh, axis=-1, keepdims=True))
        dg_part = jnp.sum(dy * xh, axis=0, keepdims=True)
        loss_part = jnp.full((1, LANE), 0.5 * jnp.sum(jnp.mean(err * err, axis=-1, keepdims=True)), F32)

        @pl.when(pl.program_id(0) == 0)
        def _():
            dg_ref[...] = dg_part
            loss_ref[...] = loss_part

        @pl.when(pl.program_id(0) > 0)
        def _():
            dg_ref[...] += dg_part
            loss_ref[...] += loss_part

    row = _bs((tm, d), lambda i: (i, 0))
    vec = _bs((1, d), lambda i: (0, 0))
    return pl.pallas_call(
        body, out_shape=(jax.ShapeDtypeStruct((t_dim, d), F32), jax.ShapeDtypeStruct((1, d), F32),
                         jax.ShapeDtypeStruct((1, LANE), F32)),
        grid=(t_dim // tm,), in_specs=[row, vec, row],
        out_specs=(row, vec, _bs((1, LANE), lambda i: (0, 0))), name=name,
        compiler_params=_params("arbitrary"))(x, g, tgt)


def _glu_conv_fwd(proj, dw_w, dw_b, n_batch, seq, name):
    kk, cc = dw_w.shape
    nj = cc // LANE
    ch = min(256, seq)

    def body(a_ref, gl_ref, w_ref, b_ref, o_ref, pad):
        pad[0:CONV_HALO, :] = jnp.zeros((CONV_HALO, LANE), F32)
        pad[CONV_HALO:CONV_HALO + seq, :] = a_ref[...] * _sig(gl_ref[...])
        for c0 in range(0, seq, ch):
            acc = jnp.broadcast_to(b_ref[...], (ch, LANE))
            for k in range(kk):
                acc = acc + w_ref[k:k + 1, :] * pad[pl.ds(c0 + CONV_HALO - (kk - 1) + k, ch), :]
            o_ref[c0:c0 + ch, :] = acc

    return pl.pallas_call(
        body, out_shape=jax.ShapeDtypeStruct((n_batch * seq, cc), F32), grid=(n_batch, nj),
        in_specs=[_bs((seq, LANE), lambda b, j: (b, j)), _bs((seq, LANE), lambda b, j: (b, nj + j)),
                  _bs((kk, LANE), lambda b, j: (0, j)), _bs((1, LANE), lambda b, j: (0, j))],
        out_specs=_bs((seq, LANE), lambda b, j: (b, j)),
        scratch_shapes=[pltpu.VMEM((seq + CONV_HALO, LANE), F32)], name=name,
        compiler_params=_params("parallel", "parallel"))(proj, proj, dw_w, dw_b)


def _glu_conv_bwd(proj, dw_w, dy1, n_batch, seq, name):
    kk, cc = dw_w.shape
    nj = cc // LANE
    ch = min(256, seq)

    def body(a_ref, gl_ref, dy_ref, w_ref, da_ref, dgl_ref, dw_ref, db_ref, padf, padb):
        first = pl.program_id(1) == 0
        padf[0:CONV_HALO, :] = jnp.zeros((CONV_HALO, LANE), F32)
        padf[CONV_HALO:CONV_HALO + seq, :] = a_ref[...] * _sig(gl_ref[...])
        padb[0:seq, :] = dy_ref[...]
        padb[seq:seq + CONV_HALO, :] = jnp.zeros((CONV_HALO, LANE), F32)

        @pl.when(first)
        def _():
            dw_ref[...] = jnp.zeros((kk, LANE), F32)
            db_ref[...] = jnp.zeros((1, LANE), F32)

        for c0 in range(0, seq, ch):
            acc = jnp.zeros((ch, LANE), F32)
            for k in range(kk):
                acc = acc + w_ref[k:k + 1, :] * padb[pl.ds(c0 + (kk - 1) - k, ch), :]
            sg = _sig(gl_ref[c0:c0 + ch, :])
            da_ref[c0:c0 + ch, :] = (acc * sg).astype(BF16)
            dgl_ref[c0:c0 + ch, :] = (acc * a_ref[c0:c0 + ch, :] * sg * (1.0 - sg)).astype(BF16)
        for k in range(kk):
            s = jnp.zeros((1, LANE), F32)
            for c0 in range(0, seq, ch):
                s = s + jnp.sum(padb[c0:c0 + ch, :] * padf[pl.ds(c0 + CONV_HALO - (kk - 1) + k, ch), :],
                                axis=0, keepdims=True)
            dw_ref[k:k + 1, :] += s
        db_ref[...] += jnp.sum(dy_ref[...], axis=0, keepdims=True)

    tok = _bs((seq, LANE), lambda j, b: (b, j))
    t_dim = n_batch * seq
    return pl.pallas_call(
        body, out_shape=(jax.ShapeDtypeStruct((t_dim, cc), BF16), jax.ShapeDtypeStruct((t_dim, cc), BF16),
                         jax.ShapeDtypeStruct((kk, cc), F32), jax.ShapeDtypeStruct((1, cc), F32)),
        grid=(nj, n_batch),
        in_specs=[tok, _bs((seq, LANE), lambda j, b: (b, nj + j)), tok, _bs((kk, LANE), lambda j, b: (0, j))],
        out_specs=(tok, tok, _bs((kk, LANE), lambda j, b: (0, j)), _bs((1, LANE), lambda j, b: (0, j))),
        scratch_shapes=[pltpu.VMEM((seq + CONV_HALO, LANE), F32), pltpu.VMEM((seq + CONV_HALO, LANE), F32)],
        name=name, compiler_params=_params("parallel", "arbitrary"))(proj, proj, dy1, dw_w)


def _ln_silu_fwd(y1, g, b, name):
    t_dim, c = y1.shape
    tm = _tile(t_dim, 512)

    def body(y_ref, g_ref, b_ref, o_ref):
        yv = y_ref[...]
        xc = yv - jnp.mean(yv, axis=-1, keepdims=True)
        rstd = lax.rsqrt(jnp.mean(xc * xc, axis=-1, keepdims=True) + EPS)
        y2 = xc * rstd * g_ref[...] + b_ref[...]
        o_ref[...] = (y2 * _sig(y2)).astype(BF16)

    row = _bs((tm, c), lambda i: (i, 0))
    vec = _bs((1, c), lambda i: (0, 0))
    return pl.pallas_call(
        body, out_shape=jax.ShapeDtypeStruct((t_dim, c), BF16), grid=(t_dim // tm,),
        in_specs=[row, vec, vec], out_specs=row, name=name, compiler_params=_params("parallel"))(y1, g, b)


def _ln_silu_bwd(y1, g, b, dy3, name):
    t_dim, c = y1.shape
    tm = _tile(t_dim, 512)

    def body(y_ref, g_ref, b_ref, d_ref, dy_ref, dg_ref, db_ref):
        yv = y_ref[...]
        gv = g_ref[...]
        xc = yv - jnp.mean(yv, axis=-1, keepdims=True)
        rstd = lax.rsqrt(jnp.mean(xc * xc, axis=-1, keepdims=True) + EPS)
        yh = xc * rstd
        y2 = yh * gv + b_ref[...]
        s = _sig(y2)
        dy2 = d_ref[...].astype(F32) * (s * (1.0 + y2 * (1.0 - s)))
        dyh = dy2 * gv
        dy_ref[...] = rstd * (dyh - jnp.mean(dyh, axis=-1, keepdims=True)
                              - yh * jnp.mean(dyh * yh, axis=-1, keepdims=True))
        dg_part = jnp.sum(dy2 * yh, axis=0, keepdims=True)
        db_part = jnp.sum(dy2, axis=0, keepdims=True)

        @pl.when(pl.program_id(0) == 0)
        def _():
            dg_ref[...] = dg_part
            db_ref[...] = db_part

        @pl.when(pl.program_id(0) > 0)
        def _():
            dg_ref[...] += dg_part
            db_ref[...] += db_part

    row = _bs((tm, c), lambda i: (i, 0))
    vec = _bs((1, c), lambda i: (0, 0))
    return pl.pallas_call(
        body, out_shape=(jax.ShapeDtypeStruct((t_dim, c), F32), jax.ShapeDtypeStruct((1, c), F32),
                         jax.ShapeDtypeStruct((1, c), F32)),
        grid=(t_dim // tm,), in_specs=[row, vec, vec, row], out_specs=(row, vec, vec), name=name,
        compiler_params=_params("arbitrary"))(y1, g, b, dy3)


def _pool_fwd(proj, col0, n_groups, n_batch, seq, name):
    ch = min(256, seq)

    def body(u_ref, o_ref, pad):
        w = lax.shift_left(jnp.int32(2), pl.program_id(1))
        pad[0:POOL_HALO, :] = jnp.zeros((POOL_HALO, LANE), F32)
        pad[POOL_HALO:POOL_HALO + seq, :] = u_ref[...]
        for c0 in range(0, seq, ch):
            acc = jnp.zeros((ch, LANE), F32)
            for j in range(POOL_WINDOW_MAX):
                acc = acc + jnp.where(j < w, 1.0, 0.0).astype(F32) * pad[pl.ds(c0 + POOL_HALO - j, ch), :]
            t = c0 + lax.broadcasted_iota(jnp.int32, (ch, LANE), 0)
            cnt = jnp.minimum(t + 1, w).astype(F32)
            o_ref[c0:c0 + ch, :] = (acc / cnt - u_ref[c0:c0 + ch, :]).astype(BF16)

    return pl.pallas_call(
        body, out_shape=jax.ShapeDtypeStruct((n_batch * seq, n_groups * LANE), BF16), grid=(n_batch, n_groups),
        in_specs=[_bs((seq, LANE), lambda b, g: (b, col0 + g))], out_specs=_bs((seq, LANE), lambda b, g: (b, g)),
        scratch_shapes=[pltpu.VMEM((seq + POOL_HALO, LANE), F32)], name=name,
        compiler_params=_params("parallel", "parallel"))(proj)


def _pool_bwd(dzp, n_groups, n_batch, seq, name):
    ch = min(256, seq)

    def body(d_ref, o_ref, pad):
        w = lax.shift_left(jnp.int32(2), pl.program_id(1))
        for c0 in range(0, seq, ch):
            t = c0 + lax.broadcasted_iota(jnp.int32, (ch, LANE), 0)
            cnt = jnp.minimum(t + 1, w).astype(F32)
            pad[c0:c0 + ch, :] = d_ref[c0:c0 + ch, :] / cnt
        pad[seq:seq + POOL_HALO, :] = jnp.zeros((POOL_HALO, LANE), F32)
        for c0 in range(0, seq, ch):
            acc = jnp.zeros((ch, LANE), F32)
            for j in range(POOL_WINDOW_MAX):
                acc = acc + jnp.where(j < w, 1.0, 0.0).astype(F32) * pad[pl.ds(c0 + j, ch), :]
            o_ref[c0:c0 + ch, :] = (acc - d_ref[c0:c0 + ch, :]).astype(BF16)

    tok = _bs((seq, LANE), lambda b, g: (b, g))
    return pl.pallas_call(
        body, out_shape=jax.ShapeDtypeStruct((n_batch * seq, n_groups * LANE), BF16), grid=(n_batch, n_groups),
        in_specs=[tok], out_specs=tok, scratch_shapes=[pltpu.VMEM((seq + POOL_HALO, LANE), F32)], name=name,
        compiler_params=_params("parallel", "parallel"))(dzp)


def _merge_fwd(proj, col0, yc, yp, scale, name):
    t_dim, d = yc.shape
    half = d // 2
    tm = _tile(t_dim, 512)
    c0 = col0 // half

    def body(gc_ref, gp_ref, yc_ref, yp_ref, s_ref, o_ref):
        o_ref[...] = (_sig(gc_ref[...]) * yc_ref[...] + _sig(gp_ref[...]) * (yp_ref[...] * s_ref[...])).astype(BF16)

    blk = _bs((tm, half), lambda i, j: (i, j))
    return pl.pallas_call(
        body, out_shape=jax.ShapeDtypeStruct((t_dim, d), BF16), grid=(t_dim // tm, 2),
        in_specs=[_bs((tm, half), lambda i, j: (i, c0 + j)), _bs((tm, half), lambda i, j: (i, c0 + 2 + j)),
                  blk, blk, _bs((1, half), lambda i, j: (0, j))],
        out_specs=blk, name=name, compiler_params=_params("parallel", "parallel"))(proj, proj, yc, yp, scale)


def _merge_bwd(proj, col0, yc, yp, scale, dm, name):
    t_dim, d = yc.shape
    half = d // 2
    tm = _tile(t_dim, 512)
    c0 = col0 // half

    def body(gc_ref, gp_ref, yc_ref, yp_ref, s_ref, dm_ref, dgc_ref, dgp_ref, dyc_ref, dyp_ref, ds_ref):
        dmv = dm_ref[...].astype(F32)
        sgc = _sig(gc_ref[...])
        sgp = _sig(gp_ref[...])
        sv = s_ref[...]
        ypre = yp_ref[...]
        dgc_ref[...] = (dmv * yc_ref[...] * sgc * (1.0 - sgc)).astype(BF16)
        dgp_ref[...] = (dmv * (ypre * sv) * sgp * (1.0 - sgp)).astype(BF16)
        dyc_ref[...] = (dmv * sgc).astype(BF16)
        dyp = dmv * sgp
        dyp_ref[...] = (dyp * sv).astype(BF16)
        part = jnp.sum(dyp * ypre, axis=0, keepdims=True)

        @pl.when(pl.program_id(1) == 0)
        def _():
            ds_ref[...] = part

        @pl.when(pl.program_id(1) > 0)
        def _():
            ds_ref[...] += part

    blk = _bs((tm, half), lambda j, i: (i, j))
    big = jax.ShapeDtypeStruct((t_dim, d), BF16)
    return pl.pallas_call(
        body, out_shape=(big, big, big, big, jax.ShapeDtypeStruct((1, d), F32)), grid=(2, t_dim // tm),
        in_specs=[_bs((tm, half), lambda j, i: (i, c0 + j)), _bs((tm, half), lambda j, i: (i, c0 + 2 + j)),
                  blk, blk, _bs((1, half), lambda j, i: (0, j)), blk],
        out_specs=(blk, blk, blk, blk, _bs((1, half), lambda j, i: (0, j))), name=name,
        compiler_params=_params("parallel", "arbitrary"))(proj, proj, yc, yp, scale, dm)


def _attn_fwd(q, kv, n_batch, seq, m_len, name):
    d = q.shape[1]
    hd = d // XA_HEADS
    tq = _tile(seq, 1024)
    nq = seq // tq
    scale = hd ** -0.5

    def body(q_ref, k_ref, v_ref, o_ref):
        sc = lax.dot_general(q_ref[...].astype(BF16), k_ref[...].astype(BF16), (((1,), (1,)), ((), ())),
                             preferred_element_type=F32) * scale
        p = jnp.exp(sc - jnp.max(sc, axis=-1, keepdims=True))
        pr = p / jnp.sum(p, axis=-1, keepdims=True)
        o_ref[...] = jnp.dot(pr.astype(BF16), v_ref[...].astype(BF16), preferred_element_type=F32).astype(BF16)

    return pl.pallas_call(
        body, out_shape=jax.ShapeDtypeStruct((n_batch * seq, d), BF16), grid=(n_batch, XA_HEADS, nq),
        in_specs=[_bs((tq, hd), lambda b, h, i: (b * nq + i, h)), _bs((m_len, hd), lambda b, h, i: (b, h)),
                  _bs((m_len, hd), lambda b, h, i: (b, XA_HEADS + h))],
        out_specs=_bs((tq, hd), lambda b, h, i: (b * nq + i, h)), name=name,
        compiler_params=_params("parallel", "parallel", "parallel"))(q, kv, kv)


def _attn_bwd(q, kv, datt, n_batch, seq, m_len, name):
    d = q.shape[1]
    hd = d // XA_HEADS
    tq = _tile(seq, 1024)
    nq = seq // tq
    scale = hd ** -0.5

    def body(q_ref, k_ref, v_ref, do_ref, dq_ref, dk_ref, dv_ref):
        qb = q_ref[...].astype(BF16)
        kb = k_ref[...].astype(BF16)
        vb = v_ref[...].astype(BF16)
        dob = do_ref[...].astype(BF16)
        sc = lax.dot_general(qb, kb, (((1,), (1,)), ((), ())), preferred_element_type=F32) * scale
        p = jnp.exp(sc - jnp.max(sc, axis=-1, keepdims=True))
        pr = p / jnp.sum(p, axis=-1, keepdims=True)
        dpr = lax.dot_general(dob, vb, (((1,), (1,)), ((), ())), preferred_element_type=F32)
        dsc = pr * (dpr - jnp.sum(dpr * pr, axis=-1, keepdims=True)) * scale
        dsb = dsc.astype(BF16)
        dq_ref[...] = jnp.dot(dsb, kb, preferred_element_type=F32).astype(BF16)
        dv_part = lax.dot_general(pr.astype(BF16), dob, (((0,), (0,)), ((), ())), preferred_element_type=F32)
        dk_part = lax.dot_general(dsb, qb, (((0,), (0,)), ((), ())), preferred_element_type=F32)

        @pl.when(pl.program_id(2) == 0)
        def _():
            dk_ref[...] = dk_part
            dv_ref[...] = dv_part

        @pl.when(pl.program_id(2) > 0)
        def _():
            dk_ref[...] += dk_part
            dv_ref[...] += dv_part

    qs = _bs((tq, hd), lambda b, h, i: (b * nq + i, h))
    ks = _bs((m_len, hd), lambda b, h, i: (b, h))
    return pl.pallas_call(
        body, out_shape=(jax.ShapeDtypeStruct((n_batch * seq, d), BF16), jax.ShapeDtypeStruct((n_batch * m_len, d), F32),
                         jax.ShapeDtypeStruct((n_batch * m_len, d), F32)),
        grid=(n_batch, XA_HEADS, nq),
        in_specs=[qs, ks, _bs((m_len, hd), lambda b, h, i: (b, XA_HEADS + h)), qs],
        out_specs=(qs, ks, ks), name=name,
        compiler_params=_params("parallel", "parallel", "arbitrary"))(q, kv, kv, datt)


def _gelu_parts(g):
    th = jnp.tanh(GELU_C0 * (g + GELU_C1 * g * g * g))
    return th, 0.5 * g * (1.0 + th)


def _ffn_act_fwd(up0, dw_w, n_batch, seq, name):
    kk, c2 = dw_w.shape
    f_dim = c2 // 2
    wd = 2 * LANE
    nj = f_dim // wd
    ch = min(128, seq)

    def body(g_ref, v_ref, wg_ref, wv_ref, o_ref, padg, padv):
        for pad, src in ((padg, g_ref), (padv, v_ref)):
            pad[0:FFN_HALO, :] = jnp.zeros((FFN_HALO, wd), F32)
            pad[FFN_HALO:FFN_HALO + seq, :] = src[...]
        for c0 in range(0, seq, ch):
            gate = jnp.zeros((ch, wd), F32)
            val = jnp.zeros((ch, wd), F32)
            for k in range(kk):
                off = c0 + FFN_HALO - (kk - 1) + k
                gate = gate + wg_ref[k:k + 1, :] * padg[pl.ds(off, ch), :]
                val = val + wv_ref[k:k + 1, :] * padv[pl.ds(off, ch), :]
            o_ref[c0:c0 + ch, :] = (_gelu_parts(gate)[1] * val).astype(BF16)

    return pl.pallas_call(
        body, out_shape=jax.ShapeDtypeStruct((n_batch * seq, f_dim), BF16), grid=(n_batch, nj),
        in_specs=[_bs((seq, wd), lambda b, j: (b, j)), _bs((seq, wd), lambda b, j: (b, nj + j)),
                  _bs((kk, wd), lambda b, j: (0, j)), _bs((kk, wd), lambda b, j: (0, nj + j))],
        out_specs=_bs((seq, wd), lambda b, j: (b, j)),
        scratch_shapes=[pltpu.VMEM((seq + FFN_HALO, wd), F32), pltpu.VMEM((seq + FFN_HALO, wd), F32)], name=name,
        compiler_params=_params("parallel", "parallel"))(up0, up0, dw_w, dw_w)


def _ffn_act_bwd(up0, dw_w, dact, n_batch, seq, name):
    kk, c2 = dw_w.shape
    f_dim = c2 // 2
    wd = 2 * LANE
    nj = f_dim // wd
    ch = min(128, seq)

    def body(g_ref, v_ref, wg_ref, wv_ref, da_ref, dg_ref, dv_ref, dwg_ref, dwv_ref, padg, padv, pbg, pbv):
        for pad, src in ((padg, g_ref), (padv, v_ref)):
            pad[0:FFN_HALO, :] = jnp.zeros((FFN_HALO, wd), F32)
            pad[FFN_HALO:FFN_HALO + seq, :] = src[...]
        for pb in (pbg, pbv):
            pb[seq:seq + FFN_HALO, :] = jnp.zeros((FFN_HALO, wd), F32)

        @pl.when(pl.program_id(1) == 0)
        def _():
            dwg_ref[...] = jnp.zeros((kk, wd), F32)
            dwv_ref[...] = jnp.zeros((kk, wd), F32)

        for c0 in range(0, seq, ch):
            gate = jnp.zeros((ch, wd), F32)
            val = jnp.zeros((ch, wd), F32)
            for k in range(kk):
                off = c0 + FFN_HALO - (kk - 1) + k
                gate = gate + wg_ref[k:k + 1, :] * padg[pl.ds(off, ch), :]
                val = val + wv_ref[k:k + 1, :] * padv[pl.ds(off, ch), :]
            th, gelu = _gelu_parts(gate)
            dgelu = 0.5 * (1.0 + th) + 0.5 * gate * (1.0 - th * th) * GELU_C0 * (1.0 + 3.0 * GELU_C1 * gate * gate)
            dav = da_ref[c0:c0 + ch, :].astype(F32)
            pbg[c0:c0 + ch, :] = dav * val * dgelu
            pbv[c0:c0 + ch, :] = dav * gelu
        for pb, pad, w_ref, d_ref, dw_ref in ((pbg, padg, wg_ref, dg_ref, dwg_ref), (pbv, padv, wv_ref, dv_ref, dwv_ref)):
            for c0 in range(0, seq, ch):
                acc = jnp.zeros((ch, wd), F32)
                for k in range(kk):
                    acc = acc + w_ref[k:k + 1, :] * pb[pl.ds(c0 + (kk - 1) - k, ch), :]
                d_ref[c0:c0 + ch, :] = acc.astype(BF16)
            for k in range(kk):
                s = jnp.zeros((1, wd), F32)
                for c0 in range(0, seq, ch):
                    s = s + jnp.sum(pb[c0:c0 + ch, :] * pad[pl.ds(c0 + FFN_HALO - (kk - 1) + k, ch), :],
                                    axis=0, keepdims=True)
                dw_ref[k:k + 1, :] += s

    t_dim = n_batch * seq
    tok = _bs((seq, wd), lambda j, b: (b, j))
    wblk = _bs((kk, wd), lambda j, b: (0, j))
    pad_shape = pltpu.VMEM((seq + FFN_HALO, wd), F32)
    return pl.pallas_call(
        body, out_shape=(jax.ShapeDtypeStruct((t_dim, f_dim), BF16), jax.ShapeDtypeStruct((t_dim, f_dim), BF16),
                         jax.ShapeDtypeStruct((kk, f_dim), F32), jax.ShapeDtypeStruct((kk, f_dim), F32)),
        grid=(nj, n_batch),
        in_specs=[tok, _bs((seq, wd), lambda j, b: (b, nj + j)), wblk, _bs((kk, wd), lambda j, b: (0, nj + j)), tok],
        out_specs=(tok, tok, wblk, wblk), scratch_shapes=[pad_shape, pad_shape, pad_shape, pad_shape], name=name,
        compiler_params=_params("parallel", "arbitrary"))(up0, up0, dw_w, dw_w, dact)


def _sum_rows(parts, out_dtype, name):
    r_dim, c_dim = parts[0].shape
    tr = _tile(r_dim, 512, SUBLANE)
    n = len(parts)

    def body(*refs):
        acc = refs[0][...].astype(F32)
        for r in refs[1:n]:
            acc = acc + r[...].astype(F32)
        refs[n][...] = acc.astype(out_dtype)

    blk = _bs((tr, c_dim), lambda i: (i, 0))
    return pl.pallas_call(
        body, out_shape=jax.ShapeDtypeStruct((r_dim, c_dim), out_dtype), grid=(r_dim // tr,),
        in_specs=[blk] * n, out_specs=blk, name=name, compiler_params=_params("parallel"))(*parts)


def _adamw(w, g, m, v, name):
    shape = w.shape
    c_dim = shape[-1]
    r_dim = w.size // c_dim
    two_d = lambda t: t.reshape(r_dim, c_dim)
    tr = _tile(r_dim, max(SUBLANE, (256 * 1024) // max(c_dim, LANE) // SUBLANE * SUBLANE), SUBLANE)
    c1 = 1.0 - ADAM_B1 ** ADAM_STEP
    c2 = 1.0 - ADAM_B2 ** ADAM_STEP

    def body(w_ref, g_ref, m_ref, v_ref, d_ref, mo_ref, vo_ref):
        gv = g_ref[...]
        mn = ADAM_B1 * m_ref[...] + (1.0 - ADAM_B1) * gv
        vn = ADAM_B2 * v_ref[...] + (1.0 - ADAM_B2) * (gv * gv)
        mo_ref[...] = mn
        vo_ref[...] = vn
        d_ref[...] = -ADAM_LR * ((mn / c1) / (jnp.sqrt(vn / c2) + ADAM_EPS) + ADAM_WD * w_ref[...])

    blk = _bs((tr, c_dim), lambda i: (i, 0))
    out = jax.ShapeDtypeStruct((r_dim, c_dim), F32)
    d, mo, vo = pl.pallas_call(
        body, out_shape=(out, out, out), grid=(r_dim // tr,), in_specs=[blk] * 4, out_specs=(blk, blk, blk),
        name=name, compiler_params=_params("parallel"))(two_d(w), two_d(g), two_d(m), two_d(v))
    return d.reshape(shape), mo.reshape(shape), vo.reshape(shape)


HBM_SPEC = pl.BlockSpec(memory_space=pltpu.HBM)


def _position():
    return lax.axis_index("x"), lax.axis_index("y"), lax.axis_index("c")


def _all_gather(shard, name):
    def body(x_ref, out_ref, send_sems, recv_sems, local_sem):
        x, y, c = _position()
        me, sibling = (x, y, c), (x, y, 1 - c)
        chips = [(1 - x, y), (x, 1 - y), (1 - x, 1 - y)]

        def rows(px, py, pc):
            return out_ref.at[4 * px + 2 * py + pc]

        def copy(k, block, to, src=None):
            return pltpu.make_async_remote_copy(
                src_ref=rows(*block) if src is None else src, dst_ref=rows(*block),
                send_sem=send_sems.at[k], recv_sem=recv_sems.at[k], device_id=to, device_id_type=MESH)

        mine = pltpu.make_async_copy(x_ref, rows(*me), local_sem)
        mine.start()
        first = [copy(0, me, sibling, src=x_ref)]
        first += [copy(1 + j, me, (*chip, c), src=x_ref) for j, chip in enumerate(chips)]
        for cp in first:
            cp.start()
        passed = [copy(4 + j, (*chip, c), sibling) for j, chip in enumerate(chips)]
        for j, chip in enumerate(chips):
            copy(1 + j, (*chip, c), me).wait_recv()
            passed[j].start()
        copy(0, sibling, me).wait_recv()
        for j, chip in enumerate(chips):
            copy(4 + j, (*chip, 1 - c), me).wait_recv()
        for cp in first + passed:
            cp.wait_send()
        mine.wait()

    return pl.pallas_call(
        body, out_shape=jax.ShapeDtypeStruct((N_DEV,) + shard.shape, shard.dtype),
        in_specs=[HBM_SPEC], out_specs=HBM_SPEC,
        scratch_shapes=[pltpu.SemaphoreType.DMA((7,)), pltpu.SemaphoreType.DMA((7,)), pltpu.SemaphoreType.DMA(())],
        name=name)(shard)


CHIP_RELATIONS = ((0, 0), (1, 0), (0, 1), (1, 1))


def _rs_pair_exchange(g, name):
    _, r_dim, c_dim = g.shape

    def body(g_ref, own_ref, recv_ref, send_sems, recv_sems, local_sems):
        x, y, c = _position()
        sibling = (x, y, 1 - c)
        local, remote = [], []
        for k, (rx, ry) in enumerate(CHIP_RELATIONS):
            px = x + rx - 2 * x * rx
            py = y + ry - 2 * y * ry
            chip = 4 * px + 2 * py
            local.append(pltpu.make_async_copy(g_ref.at[chip + c], own_ref.at[k], local_sems.at[k]))
            remote.append(pltpu.make_async_remote_copy(
                src_ref=g_ref.at[chip + 1 - c], dst_ref=recv_ref.at[k], send_sem=send_sems.at[k],
                recv_sem=recv_sems.at[k], device_id=sibling, device_id_type=MESH))
        for cp in remote + local:
            cp.start()
        for cp in remote + local:
            cp.wait()

    out = jax.ShapeDtypeStruct((4, r_dim, c_dim), g.dtype)
    n = len(CHIP_RELATIONS)
    return pl.pallas_call(
        body, out_shape=(out, out), in_specs=[HBM_SPEC], out_specs=(HBM_SPEC, HBM_SPEC),
        scratch_shapes=[pltpu.SemaphoreType.DMA((n,)), pltpu.SemaphoreType.DMA((n,)), pltpu.SemaphoreType.DMA((n,))],
        name=name)(g)


def _rs_chip_exchange(p, name):
    _, r_dim, c_dim = p.shape

    def body(p_ref, recv_ref, send_sems, recv_sems):
        x, y, c = _position()
        copies = []
        for k, (rx, ry) in enumerate(CHIP_RELATIONS[1:]):
            px = x + rx - 2 * x * rx
            py = y + ry - 2 * y * ry
            copies.append(pltpu.make_async_remote_copy(
                src_ref=p_ref.at[k + 1], dst_ref=recv_ref.at[k], send_sem=send_sems.at[k],
                recv_sem=recv_sems.at[k], device_id=(px, py, c), device_id_type=MESH))
        for cp in copies:
            cp.start()
        for cp in copies:
            cp.wait()

    return pl.pallas_call(
        body, out_shape=jax.ShapeDtypeStruct((3, r_dim, c_dim), p.dtype), in_specs=[HBM_SPEC], out_specs=HBM_SPEC,
        scratch_shapes=[pltpu.SemaphoreType.DMA((3,)), pltpu.SemaphoreType.DMA((3,))], name=name)(p)


def _reduce_scatter(g, name):
    _, r_dim, c_dim = g.shape
    own, recv = _rs_pair_exchange(g, name + "_pair")
    pair = _sum_rows([own.reshape(4 * r_dim, c_dim), recv.reshape(4 * r_dim, c_dim)], g.dtype, name + "_pairsum")
    pair = pair.reshape(4, r_dim, c_dim)
    far = _rs_chip_exchange(pair, name + "_chips")
    return _sum_rows([pair[0], far[0], far[1], far[2]], F32, name + "_sum")


MATRICES = (("w_in", True), ("w_out", False), ("w_q", False), ("w_kv", True), ("w_o", False), ("w_up", True),
            ("w_down", False), ("w_conv_out", True), ("w_pool_grp", True))


def _to_rows(name, transposed, w, d_model):
    if name == "w_pool_grp":
        w = jnp.swapaxes(w, 1, 2)
    elif transposed:
        w = w.T
    return w.reshape(-1, d_model)


def _from_rows(name, transposed, rows, shard_shape):
    if name == "w_pool_grp":
        g, i, o = shard_shape
        return jnp.swapaxes(rows.reshape(g, o, i), 1, 2)
    if transposed:
        return rows.reshape(shard_shape[1], shard_shape[0]).T
    return rows.reshape(shard_shape)


def _gathered_matrix(name, blocks, shard_shape):
    if name == "w_pool_grp":
        g, i, o = shard_shape
        return jnp.swapaxes(blocks.reshape(N_DEV, g, o, i), 0, 1).reshape(g, N_DEV * o, i)
    if name == "w_conv_out":
        return blocks.reshape(N_DEV * shard_shape[1], shard_shape[0])
    return blocks.reshape(-1, blocks.shape[-1])


def _scatter_blocks(name, full, shard_shape, d_model):
    if name == "w_pool_grp":
        g, i, o = shard_shape
        return jnp.swapaxes(full.reshape(g, N_DEV, o, i), 0, 1).reshape(N_DEV, -1, d_model)
    return full.reshape(N_DEV, -1, d_model)


def kernel(x, mem, mix_norm_g, w_in, conv_dw_w, conv_dw_b, conv_ln_g, conv_ln_b, w_conv_out, w_pool_grp, pool_scale, w_out, xattn_norm_g, mem_norm_g, w_q, w_kv, w_o, ffn_norm_g, w_up, ffn_dw_w, w_down, final_norm_g, loss_target, m_mix_norm_g, m_w_in, m_conv_dw_w, m_conv_dw_b, m_conv_ln_g, m_conv_ln_b, m_w_conv_out, m_w_pool_grp, m_pool_scale, m_w_out, m_xattn_norm_g, m_mem_norm_g, m_w_q, m_w_kv, m_w_o, m_ffn_norm_g, m_w_up, m_ffn_dw_w, m_w_down, m_final_norm_g, v_mix_norm_g, v_w_in, v_conv_dw_w, v_conv_dw_b, v_conv_ln_g, v_conv_ln_b, v_w_conv_out, v_w_pool_grp, v_pool_scale, v_w_out, v_xattn_norm_g, v_mem_norm_g, v_w_q, v_w_kv, v_w_o, v_ffn_norm_g, v_w_up, v_ffn_dw_w, v_w_down, v_final_norm_g):
    p = dict(locals())
    weight_names = ["mix_norm_g", "w_in", "conv_dw_w", "conv_dw_b", "conv_ln_g", "conv_ln_b", "w_conv_out",
                    "w_pool_grp", "pool_scale", "w_out", "xattn_norm_g", "mem_norm_g", "w_q", "w_kv", "w_o",
                    "ffn_norm_g", "w_up", "ffn_dw_w", "w_down", "final_norm_g"]
    n_batch, seq, d_model = x.shape
    m_len = mem.shape[1]
    depth = w_in.shape[0]
    t_dim = n_batch * seq
    c_conv = conv_dw_b.shape[1]
    n_groups = w_pool_grp.shape[1]
    assert w_pool_grp.shape[2] == LANE and c_conv % LANE == 0 and n_groups * LANE == c_conv
    gate_col0 = 2 * c_conv + n_groups * LANE
    pool_col0 = (2 * c_conv) // LANE

    pieces, layout = [], []
    for l in range(depth):
        for name, tr in MATRICES:
            rows = _to_rows(name, tr, p[name][l], d_model)
            layout.append((l, name, tr, rows.shape[0]))
            pieces.append(rows.astype(BF16))
    filt = jnp.concatenate([conv_dw_w.reshape(-1), ffn_dw_w.reshape(-1)])
    filt_rows = lax.bitcast_convert_type(filt, BF16).reshape(-1, d_model)
    pieces.append(filt_rows)
    gathered = _all_gather(jnp.concatenate(pieces, axis=0), "weights_all_gather")

    full = [dict() for _ in range(depth)]
    row0 = 0
    for l, name, tr, nrows in layout:
        full[l][name] = _gathered_matrix(name, gathered[:, row0:row0 + nrows], p[name].shape[1:])
        row0 += nrows
    filt_all = lax.bitcast_convert_type(
        gathered[:, row0:row0 + filt_rows.shape[0]].reshape(N_DEV, -1, 2), F32)
    n_cw = conv_dw_w.size
    kc, cs = conv_dw_w.shape[1:]
    kf, fs = ffn_dw_w.shape[1:]
    conv_w_full = jnp.moveaxis(filt_all[:, :n_cw].reshape(N_DEV, depth, kc, cs), 0, 2).reshape(depth, kc, N_DEV * cs)
    ffn_w_full = jnp.moveaxis(filt_all[:, n_cw:].reshape(N_DEV, depth, kf, fs), 0, 2).reshape(depth, kf, N_DEV * fs)

    vec = lambda a: a.reshape(1, -1)
    x2d = x.reshape(t_dim, d_model)
    mem2d = mem.reshape(n_batch * m_len, d_model)
    mem_n = _rmsnorm_fwd(mem2d, vec(mem_norm_g), "mem_norm")

    saved = []
    xc = x2d
    for l in range(depth):
        wl = full[l]
        s = {"x0": xc}
        s["h"] = _rmsnorm_fwd(xc, vec(mix_norm_g[l]), f"mix_norm_l{l}")
        s["proj"] = _matmul(s["h"], wl["w_in"], "nt", f"in_proj_l{l}")
        s["y1"] = _glu_conv_fwd(s["proj"], conv_w_full[l], vec(conv_dw_b[l]), n_batch, seq, f"glu_conv_l{l}")
        s["y3"] = _ln_silu_fwd(s["y1"], vec(conv_ln_g[l]), vec(conv_ln_b[l]), f"ln_silu_l{l}")
        s["yc"] = _matmul(s["y3"], wl["w_conv_out"], "nt", f"conv_out_l{l}")
        s["zp"] = _pool_fwd(s["proj"], pool_col0, n_groups, n_batch, seq, f"pool_l{l}")
        s["yp"] = _grouped(s["zp"], wl["w_pool_grp"], "nt", f"pool_proj_l{l}")
        s["merged"] = _merge_fwd(s["proj"], gate_col0, s["yc"], s["yp"], vec(pool_scale[l]), f"merge_l{l}")
        s["x1"] = _matmul(s["merged"], wl["w_out"], "nn", f"mix_out_l{l}", res=xc)
        s["hq"] = _rmsnorm_fwd(s["x1"], vec(xattn_norm_g[l]), f"xattn_norm_l{l}")
        s["q"] = _matmul(s["hq"], wl["w_q"], "nn", f"q_proj_l{l}")
        s["kv"] = _matmul(mem_n, wl["w_kv"], "nt", f"kv_proj_l{l}")
        s["att"] = _attn_fwd(s["q"], s["kv"], n_batch, seq, m_len, f"attn_l{l}")
        s["x2"] = _matmul(s["att"], wl["w_o"], "nn", f"attn_out_l{l}", res=s["x1"])
        s["hf"] = _rmsnorm_fwd(s["x2"], vec(ffn_norm_g[l]), f"ffn_norm_l{l}")
        s["up0"] = _matmul(s["hf"], wl["w_up"], "nt", f"up_proj_l{l}")
        s["act"] = _ffn_act_fwd(s["up0"], ffn_w_full[l], n_batch, seq, f"ffn_act_l{l}")
        xc = _matmul(s["act"], wl["w_down"], "nn", f"down_proj_l{l}", res=s["x2"])
        saved.append(s)

    dx, dg_final, loss_part = _loss_head(xc, vec(final_norm_g), loss_target.reshape(t_dim, d_model), "loss_head")

    small = {"final_norm_g": dg_final.reshape(-1)}
    big = [dict() for _ in range(depth)]
    dmem_n = None
    for l in reversed(range(depth)):
        wl, s = full[l], saved[l]
        sm = {}
        dact = _matmul(dx, wl["w_down"], "nt", f"d_act_l{l}")
        big[l]["w_down"] = _matmul(s["act"], dx, "tn", f"d_w_down_l{l}")
        dup_g, dup_v, dwf_g, dwf_v = _ffn_act_bwd(s["up0"], ffn_w_full[l], dact, n_batch, seq, f"ffn_act_bwd_l{l}")
        sm["ffn_dw_w"] = jnp.concatenate([dwf_g, dwf_v], axis=1)
        dup0 = jnp.concatenate([dup_g, dup_v], axis=1)
        dhf = _matmul(dup0, wl["w_up"], "nn", f"d_hf_l{l}")
        big[l]["w_up"] = _matmul(dup0, s["hf"], "tn", f"d_w_up_l{l}")
        dx, dg = _rmsnorm_bwd(s["x2"], vec(ffn_norm_g[l]), dhf, dx, f"ffn_norm_bwd_l{l}")
        sm["ffn_norm_g"] = dg
        datt = _matmul(dx, wl["w_o"], "nt", f"d_att_l{l}")
        big[l]["w_o"] = _matmul(s["att"], dx, "tn", f"d_w_o_l{l}")
        dq, dk, dv = _attn_bwd(s["q"], s["kv"], datt, n_batch, seq, m_len, f"attn_bwd_l{l}")
        dkv = jnp.concatenate([dk, dv], axis=1)
        big[l]["w_kv"] = _matmul(dkv, mem_n, "tn", f"d_w_kv_l{l}")
        dmem_n = _matmul(dkv, wl["w_kv"], "nn", f"d_mem_l{l}", res=dmem_n)
        dhq = _matmul(dq, wl["w_q"], "nt", f"d_hq_l{l}")
        big[l]["w_q"] = _matmul(s["hq"], dq, "tn", f"d_w_q_l{l}")
        dx, dg = _rmsnorm_bwd(s["x1"], vec(xattn_norm_g[l]), dhq, dx, f"xattn_norm_bwd_l{l}")
        sm["xattn_norm_g"] = dg
        dmerged = _matmul(dx, wl["w_out"], "nt", f"d_merged_l{l}")
        big[l]["w_out"] = _matmul(s["merged"], dx, "tn", f"d_w_out_l{l}")
        dgc, dgp, dyc, dyp, dscale = _merge_bwd(s["proj"], gate_col0, s["yc"], s["yp"], vec(pool_scale[l]), dmerged,
                                                f"merge_bwd_l{l}")
        sm["pool_scale"] = dscale
        dzp = _grouped(dyp, wl["w_pool_grp"], "nn", f"d_zp_l{l}")
        big[l]["w_pool_grp"] = _grouped_tn(dyp, s["zp"], n_groups, f"d_w_pool_l{l}")
        du = _pool_bwd(dzp, n_groups, n_batch, seq, f"pool_bwd_l{l}")
        dy3 = _matmul(dyc, wl["w_conv_out"], "nn", f"d_y3_l{l}")
        big[l]["w_conv_out"] = _matmul(dyc, s["y3"], "tn", f"d_w_conv_out_l{l}")
        dy1, dlg, dlb = _ln_silu_bwd(s["y1"], vec(conv_ln_g[l]), vec(conv_ln_b[l]), dy3, f"ln_silu_bwd_l{l}")
        sm["conv_ln_g"], sm["conv_ln_b"] = dlg, dlb
        da, dgl, dcw, dcb = _glu_conv_bwd(s["proj"], conv_w_full[l], dy1, n_batch, seq, f"glu_conv_bwd_l{l}")
        sm["conv_dw_w"], sm["conv_dw_b"] = dcw, dcb
        dproj = jnp.concatenate([da, dgl, du, dgc, dgp], axis=1)
        dh = _matmul(dproj, wl["w_in"], "nn", f"d_h_l{l}")
        big[l]["w_in"] = _matmul(dproj, s["h"], "tn", f"d_w_in_l{l}")
        dx, dg = _rmsnorm_bwd(s["x0"], vec(mix_norm_g[l]), dh, dx, f"mix_norm_bwd_l{l}")
        sm["mix_norm_g"] = dg
        for k, val in sm.items():
            small[(l, k)] = val.reshape(-1)
    _, dg_mem = _rmsnorm_bwd(mem2d, vec(mem_norm_g), dmem_n, None, "mem_norm_bwd")
    small["mem_norm_g"] = dg_mem.reshape(-1)
    small["loss"] = loss_part.reshape(-1)

    blocks = []
    for l, name, tr, nrows in layout:
        blocks.append(_scatter_blocks(name, big[l][name], p[name].shape[1:], d_model).astype(BF16))
    mat_grads = _reduce_scatter(jnp.concatenate(blocks, axis=1), "grad_reduce_scatter")
    grads = {}
    per_layer = {name: [None] * depth for name, _ in MATRICES}
    row0 = 0
    for l, name, tr, nrows in layout:
        per_layer[name][l] = _from_rows(name, tr, mat_grads[row0:row0 + nrows], p[name].shape[1:])
        row0 += nrows
    for name, _ in MATRICES:
        grads[name] = jnp.stack(per_layer[name])

    keys = list(small.keys())
    flat = jnp.concatenate([small[k] for k in keys])
    n_small = flat.shape[0]
    rows_small = -(-n_small // (SUBLANE * d_model)) * SUBLANE
    flat = jnp.pad(flat, (0, rows_small * d_model - n_small)).reshape(rows_small, d_model)
    every = _all_gather(flat, "small_all_gather")
    total = _sum_rows([every[i] for i in range(N_DEV)], F32, "small_sum").reshape(-1)
    off = 0
    red = {}
    for k in keys:
        red[k] = total[off:off + small[k].shape[0]]
        off += small[k].shape[0]
    loss = red["loss"][0]
    dev = 4 * lax.axis_index("x") + 2 * lax.axis_index("y") + lax.axis_index("c")
    for name in ("mix_norm_g", "conv_dw_b", "conv_ln_g", "conv_ln_b", "pool_scale", "xattn_norm_g", "ffn_norm_g"):
        grads[name] = jnp.stack([red[(l, name)] for l in range(depth)])
    grads["conv_dw_w"] = jnp.stack([
        lax.dynamic_slice_in_dim(red[(l, "conv_dw_w")].reshape(kc, N_DEV * cs), dev * cs, cs, axis=1)
        for l in range(depth)])
    grads["ffn_dw_w"] = jnp.stack([
        lax.dynamic_slice_in_dim(red[(l, "ffn_dw_w")].reshape(kf, N_DEV * fs), dev * fs, fs, axis=1)
        for l in range(depth)])
    grads["mem_norm_g"] = red["mem_norm_g"]
    grads["final_norm_g"] = red["final_norm_g"]

    deltas, new_m, new_v = {}, {}, {}
    for name in weight_names:
        deltas[name], new_m[name], new_v[name] = _adamw(p[name], grads[name], p["m_" + name], p["v_" + name],
                                                        f"adamw_{name}")
    grad_x = dx.reshape(n_batch, seq, d_model)
    return (loss, grad_x, *[grads[n] for n in weight_names], *[deltas[n] for n in weight_names],
            *[new_m[n] for n in weight_names], *[new_v[n] for n in weight_names])
```

```python
import functools

import jax
import jax.numpy as jnp
from jax import lax
from jax.experimental import pallas as pl
from jax.experimental.pallas import tpu as pltpu

F32 = jnp.float32
BF16 = jnp.bfloat16
MESH = pl.DeviceIdType.MESH

N_DEV = 8
EPS = 1e-6
V7X_VMEM_BYTES = 64 * 1024 * 1024
VMEM_LIMIT = (V7X_VMEM_BYTES * 3) // 4
LANE = 128
SUBLANE = 8

CONV_HALO = 32
POOL_HALO = 16
FFN_HALO = 8
POOL_WINDOW_MAX = 16
XA_HEADS = 4

ADAM_LR = 0.001
ADAM_B1 = 0.9
ADAM_B2 = 0.999
ADAM_EPS = 1e-08
ADAM_WD = 0.01
ADAM_STEP = 10

GELU_C0 = 0.7978845608028654
GELU_C1 = 0.044715


def _tile(n, cap, mult=LANE):
    if n <= cap:
        return n
    best = None
    for d in range(mult, cap + 1, mult):
        if n % d == 0:
            best = d
    assert best is not None, (n, cap, mult)
    return best


def _params(*sem):
    return pltpu.CompilerParams(dimension_semantics=sem, vmem_limit_bytes=VMEM_LIMIT)


def _sig(x):
    return 1.0 / (1.0 + jnp.exp(-x))


def _bs(shape, imap):
    return pl.BlockSpec(shape, imap)


def _matmul(a, b, mode, name, res=None, out_dtype=F32):
    if mode == "tn":
        k_dim, m_dim = a.shape
        k2, n_dim = b.shape
    elif mode == "nn":
        m_dim, k_dim = a.shape
        k2, n_dim = b.shape
    else:
        m_dim, k_dim = a.shape
        n_dim, k2 = b.shape
    assert k_dim == k2, (name, a.shape, b.shape)
    tm = _tile(m_dim, 1408 if mode == "tn" else 1024)
    tn = _tile(n_dim, 1408)
    tk = _tile(k_dim, 512 if mode == "tn" else 1024)
    nk = k_dim // tk
    if mode == "tn":
        a_spec, ca = _bs((tk, tm), lambda i, j, k: (k, i)), 0
    else:
        a_spec, ca = _bs((tm, tk), lambda i, j, k: (i, k)), 1
    if mode == "nt":
        b_spec, cb = _bs((tn, tk), lambda i, j, k: (j, k)), 1
    else:
        b_spec, cb = _bs((tk, tn), lambda i, j, k: (k, j)), 0
    dims = (((ca,), (cb,)), ((), ()))
    o_spec = _bs((tm, tn), lambda i, j, k: (i, j))
    has_res = res is not None

    def body(*refs):
        if has_res:
            a_ref, b_ref, r_ref, o_ref, acc = refs
        else:
            a_ref, b_ref, o_ref, acc = refs
            r_ref = None
        k = pl.program_id(2)
        part = lax.dot_general(a_ref[...].astype(BF16), b_ref[...].astype(BF16), dims,
                               preferred_element_type=F32)

        @pl.when(k == 0)
        def _():
            acc[...] = part

        @pl.when(k > 0)
        def _():
            acc[...] += part

        @pl.when(k == nk - 1)
        def _():
            o = acc[...]
            if has_res:
                o = o + r_ref[...].astype(F32)
            o_ref[...] = o.astype(out_dtype)

    in_specs = [a_spec, b_spec] + ([o_spec] if has_res else [])
    args = (a, b) + ((res,) if has_res else ())
    return pl.pallas_call(
        body, out_shape=jax.ShapeDtypeStruct((m_dim, n_dim), out_dtype),
        grid=(m_dim // tm, n_dim // tn, nk), in_specs=in_specs, out_specs=o_spec,
        scratch_shapes=[pltpu.VMEM((tm, tn), F32)], name=name,
        compiler_params=_params("parallel", "parallel", "arbitrary"))(*args)


def _grouped(a, w, mode, name, out_dtype=F32):
    t_dim = a.shape[0]
    g_dim, r_dim, c_dim = w.shape
    ka, no = (c_dim, r_dim) if mode == "nt" else (r_dim, c_dim)
    tm = _tile(t_dim, 512)
    dims = (((1,), (1 if mode == "nt" else 0,)), ((), ()))

    def body(a_ref, w_ref, o_ref):
        o_ref[...] = lax.dot_general(a_ref[...].astype(BF16), w_ref[...].astype(BF16), dims,
                                     preferred_element_type=F32).astype(out_dtype)

    return pl.pallas_call(
        body, out_shape=jax.ShapeDtypeStruct((t_dim, g_dim * no), out_dtype),
        grid=(t_dim // tm, g_dim),
        in_specs=[_bs((tm, ka), lambda i, g: (i, g)), _bs((None, r_dim, c_dim), lambda i, g: (g, 0, 0))],
        out_specs=_bs((tm, no), lambda i, g: (i, g)), name=name,
        compiler_params=_params("parallel", "parallel"))(a, w)


def _grouped_tn(a, b, g_dim, name):
    t_dim = a.shape[0]
    ra = a.shape[1] // g_dim
    cb = b.shape[1] // g_dim
    tm = _tile(t_dim, 512)
    nt = t_dim // tm

    def body(a_ref, b_ref, o_ref):
        part = lax.dot_general(a_ref[...].astype(BF16), b_ref[...].astype(BF16), (((0,), (0,)), ((), ())),
                               preferred_element_type=F32)

        @pl.when(pl.program_id(1) == 0)
        def _():
            o_ref[...] = part

        @pl.when(pl.program_id(1) > 0)
        def _():
            o_ref[...] += part

    return pl.pallas_call(
        body, out_shape=jax.ShapeDtypeStruct((g_dim, ra, cb), F32), grid=(g_dim, nt),
        in_specs=[_bs((tm, ra), lambda g, i: (i, g)), _bs((tm, cb), lambda g, i: (i, g))],
        out_specs=_bs((None, ra, cb), lambda g, i: (g, 0, 0)), name=name,
        compiler_params=_params("parallel", "arbitrary"))(a, b)


def _rmsnorm_fwd(x, g, name):
    t_dim, d = x.shape
    tm = _tile(t_dim, 512)

    def body(x_ref, g_ref, o_ref):
        xv = x_ref[...]
        r = lax.rsqrt(jnp.mean(xv * xv, axis=-1, keepdims=True) + EPS)
        o_ref[...] = (xv * r * g_ref[...]).astype(BF16)

    return pl.pallas_call(
        body, out_shape=jax.ShapeDtypeStruct((t_dim, d), BF16), grid=(t_dim // tm,),
        in_specs=[_bs((tm, d), lambda i: (i, 0)), _bs((1, d), lambda i: (0, 0))],
        out_specs=_bs((tm, d), lambda i: (i, 0)), name=name, compiler_params=_params("parallel"))(x, g)


def _rmsnorm_bwd(x, g, dh, dx_in, name):
    t_dim, d = x.shape
    tm = _tile(t_dim, 512)
    has_in = dx_in is not None

    def body(*refs):
        if has_in:
            x_ref, g_ref, dh_ref, di_ref, dx_ref, dg_ref = refs
        else:
            x_ref, g_ref, dh_ref, dx_ref, dg_ref = refs
        xv = x_ref[...]
        r = lax.rsqrt(jnp.mean(xv * xv, axis=-1, keepdims=True) + EPS)
        xh = xv * r
        dhv = dh_ref[...].astype(F32)
        dxh = dhv * g_ref[...]
        dx = r * (dxh - xh * jnp.mean(dxh * xh, axis=-1, keepdims=True))
        if has_in:
            dx = dx + di_ref[...]
        dx_ref[...] = dx
        part = jnp.sum(dhv * xh, axis=0, keepdims=True)

        @pl.when(pl.program_id(0) == 0)
        def _():
            dg_ref[...] = part

        @pl.when(pl.program_id(0) > 0)
        def _():
            dg_ref[...] += part

    row = _bs((tm, d), lambda i: (i, 0))
    vec = _bs((1, d), lambda i: (0, 0))
    args = (x, g, dh) + ((dx_in,) if has_in else ())
    return pl.pallas_call(
        body, out_shape=(jax.ShapeDtypeStruct((t_dim, d), F32), jax.ShapeDtypeStruct((1, d), F32)),
        grid=(t_dim // tm,), in_specs=[row, vec, row] + ([row] if has_in else []),
        out_specs=(row, vec), name=name, compiler_params=_params("arbitrary"))(*args)


def _loss_head(x, g, tgt, name):
    t_dim, d = x.shape
    tm = _tile(t_dim, 512)

    def body(x_ref, g_ref, t_ref, dx_ref, dg_ref, loss_ref):
        xv = x_ref[...]
        gv = g_ref[...]
        r = lax.rsqrt(jnp.mean(xv * xv, axis=-1, keepdims=True) + EPS)
        xh = xv * r
        err = xh * gv - t_ref[...]
        dy = err * (1.0 / d)
        dxh = dy * gv
        dx_ref[...] = r * (dxh - xh * jnp.mean(dxh * xh, axis=-1, keepdims=True))
        dg_part = jnp.sum(dy * xh, axis=0, keepdims=True)
        loss_part = jnp.full((1, LANE), 0.5 * jnp.sum(jnp.mean(err * err, axis=-1, keepdims=True)), F32)

        @pl.when(pl.program_id(0) == 0)
        def _():
            dg_ref[...] = dg_part
            loss_ref[...] = loss_part

        @pl.when(pl.program_id(0) > 0)
        def _():
            dg_ref[...] += dg_part
            loss_ref[...] += loss_part

    row = _bs((tm, d), lambda i: (i, 0))
    vec = _bs((1, d), lambda i: (0, 0))
    return pl.pallas_call(
        body, out_shape=(jax.ShapeDtypeStruct((t_dim, d), F32), jax.ShapeDtypeStruct((1, d), F32),
                         jax.ShapeDtypeStruct((1, LANE), F32)),
        grid=(t_dim // tm,), in_specs=[row, vec, row],
        out_specs=(row, vec, _bs((1, LANE), lambda i: (0, 0))), name=name,
        compiler_params=_params("arbitrary"))(x, g, tgt)


def _glu_conv_fwd(proj, dw_w, dw_b, n_batch, seq, name):
    kk, cc = dw_w.shape
    nj = cc // LANE
    ch = min(256, seq)

    def body(a_ref, gl_ref, w_ref, b_ref, o_ref, pad):
        pad[0:CONV_HALO, :] = jnp.zeros((CONV_HALO, LANE), F32)
        pad[CONV_HALO:CONV_HALO + seq, :] = a_ref[...] * _sig(gl_ref[...])
        for c0 in range(0, seq, ch):
            acc = jnp.broadcast_to(b_ref[...], (ch, LANE))
            for k in range(kk):
                acc = acc + w_ref[k:k + 1, :] * pad[pl.ds(c0 + CONV_HALO - (kk - 1) + k, ch), :]
            o_ref[c0:c0 + ch, :] = acc

    return pl.pallas_call(
        body, out_shape=jax.ShapeDtypeStruct((n_batch * seq, cc), F32), grid=(n_batch, nj),
        in_specs=[_bs((seq, LANE), lambda b, j: (b, j)), _bs((seq, LANE), lambda b, j: (b, nj + j)),
                  _bs((kk, LANE), lambda b, j: (0, j)), _bs((1, LANE), lambda b, j: (0, j))],
        out_specs=_bs((seq, LANE), lambda b, j: (b, j)),
        scratch_shapes=[pltpu.VMEM((seq + CONV_HALO, LANE), F32)], name=name,
        compiler_params=_params("parallel", "parallel"))(proj, proj, dw_w, dw_b)


def _glu_conv_bwd(proj, dw_w, dy1, n_batch, seq, name):
    kk, cc = dw_w.shape
    nj = cc // LANE
    ch = min(256, seq)

    def body(a_ref, gl_ref, dy_ref, w_ref, da_ref, dgl_ref, dw_ref, db_ref, padf, padb):
        first = pl.program_id(1) == 0
        padf[0:CONV_HALO, :] = jnp.zeros((CONV_HALO, LANE), F32)
        padf[CONV_HALO:CONV_HALO + seq, :] = a_ref[...] * _sig(gl_ref[...])
        padb[0:seq, :] = dy_ref[...]
        padb[seq:seq + CONV_HALO, :] = jnp.zeros((CONV_HALO, LANE), F32)

        @pl.when(first)
        def _():
            dw_ref[...] = jnp.zeros((kk, LANE), F32)
            db_ref[...] = jnp.zeros((1, LANE), F32)

        for c0 in range(0, seq, ch):
            acc = jnp.zeros((ch, LANE), F32)
            for k in range(kk):
                acc = acc + w_ref[k:k + 1, :] * padb[pl.ds(c0 + (kk - 1) - k, ch), :]
            sg = _sig(gl_ref[c0:c0 + ch, :])
            da_ref[c0:c0 + ch, :] = (acc * sg).astype(BF16)
            dgl_ref[c0:c0 + ch, :] = (acc * a_ref[c0:c0 + ch, :] * sg * (1.0 - sg)).astype(BF16)
        for k in range(kk):
            s = jnp.zeros((1, LANE), F32)
            for c0 in range(0, seq, ch):
                s = s + jnp.sum(padb[c0:c0 + ch, :] * padf[pl.ds(c0 + CONV_HALO - (kk - 1) + k, ch), :],
                                axis=0, keepdims=True)
            dw_ref[k:k + 1, :] += s
        db_ref[...] += jnp.sum(dy_ref[...], axis=0, keepdims=True)

    tok = _bs((seq, LANE), lambda j, b: (b, j))
    t_dim = n_batch * seq
    return pl.pallas_call(
        body, out_shape=(jax.ShapeDtypeStruct((t_dim, cc), BF16), jax.ShapeDtypeStruct((t_dim, cc), BF16),
                         jax.ShapeDtypeStruct((kk, cc), F32), jax.ShapeDtypeStruct((1, cc), F32)),
        grid=(nj, n_batch),
        in_specs=[tok, _bs((seq, LANE), lambda j, b: (b, nj + j)), tok, _bs((kk, LANE), lambda j, b: (0, j))],
        out_specs=(tok, tok, _bs((kk, LANE), lambda j, b: (0, j)), _bs((1, LANE), lambda j, b: (0, j))),
        scratch_shapes=[pltpu.VMEM((seq + CONV_HALO, LANE), F32), pltpu.VMEM((seq + CONV_HALO, LANE), F32)],
        name=name, compiler_params=_params("parallel", "arbitrary"))(proj, proj, dy1, dw_w)


def _ln_silu_fwd(y1, g, b, name):
    t_dim, c = y1.shape
    tm = _tile(t_dim, 512)

    def body(y_ref, g_ref, b_ref, o_ref):
        yv = y_ref[...]
        xc = yv - jnp.mean(yv, axis=-1, keepdims=True)
        rstd = lax.rsqrt(jnp.mean(xc * xc, axis=-1, keepdims=True) + EPS)
        y2 = xc * rstd * g_ref[...] + b_ref[...]
        o_ref[...] = (y2 * _sig(y2)).astype(BF16)

    row = _bs((tm, c), lambda i: (i, 0))
    vec = _bs((1, c), lambda i: (0, 0))
    return pl.pallas_call(
        body, out_shape=jax.ShapeDtypeStruct((t_dim, c), BF16), grid=(t_dim // tm,),
        in_specs=[row, vec, vec], out_specs=row, name=name, compiler_params=_params("parallel"))(y1, g, b)


def _ln_silu_bwd(y1, g, b, dy3, name):
    t_dim, c = y1.shape
    tm = _tile(t_dim, 512)

    def body(y_ref, g_ref, b_ref, d_ref, dy_ref, dg_ref, db_ref):
        yv = y_ref[...]
        gv = g_ref[...]
        xc = yv - jnp.mean(yv, axis=-1, keepdims=True)
        rstd = lax.rsqrt(jnp.mean(xc * xc, axis=-1, keepdims=True) + EPS)
        yh = xc * rstd
        y2 = yh * gv + b_ref[...]
        s = _sig(y2)
        dy2 = d_ref[...].astype(F32) * (s * (1.0 + y2 * (1.0 - s)))
        dyh = dy2 * gv
        dy_ref[...] = rstd * (dyh - jnp.mean(dyh, axis=-1, keepdims=True)
                              - yh * jnp.mean(dyh * yh, axis=-1, keepdims=True))
        dg_part = jnp.sum(dy2 * yh, axis=0, keepdims=True)
        db_part = jnp.sum(dy2, axis=0, keepdims=True)

        @pl.when(pl.program_id(0) == 0)
        def _():
            dg_ref[...] = dg_part
            db_ref[...] = db_part

        @pl.when(pl.program_id(0) > 0)
        def _():
            dg_ref[...] += dg_part
            db_ref[...] += db_part

    row = _bs((tm, c), lambda i: (i, 0))
    vec = _bs((1, c), lambda i: (0, 0))
    return pl.pallas_call(
        body, out_shape=(jax.ShapeDtypeStruct((t_dim, c), F32), jax.ShapeDtypeStruct((1, c), F32),
                         jax.ShapeDtypeStruct((1, c), F32)),
        grid=(t_dim // tm,), in_specs=[row, vec, vec, row], out_specs=(row, vec, vec), name=name,
        compiler_params=_params("arbitrary"))(y1, g, b, dy3)


def _pool_fwd(proj, col0, n_groups, n_batch, seq, name):
    ch = min(256, seq)

    def body(u_ref, o_ref, pad):
        w = lax.shift_left(jnp.int32(2), pl.program_id(1))
        pad[0:POOL_HALO, :] = jnp.zeros((POOL_HALO, LANE), F32)
        pad[POOL_HALO:POOL_HALO + seq, :] = u_ref[...]
        for c0 in range(0, seq, ch):
            acc = jnp.zeros((ch, LANE), F32)
            for j in range(POOL_WINDOW_MAX):
                acc = acc + jnp.where(j < w, 1.0, 0.0).astype(F32) * pad[pl.ds(c0 + POOL_HALO - j, ch), :]
            t = c0 + lax.broadcasted_iota(jnp.int32, (ch, LANE), 0)
            cnt = jnp.minimum(t + 1, w).astype(F32)
            o_ref[c0:c0 + ch, :] = (acc / cnt - u_ref[c0:c0 + ch, :]).astype(BF16)

    return pl.pallas_call(
        body, out_shape=jax.ShapeDtypeStruct((n_batch * seq, n_groups * LANE), BF16), grid=(n_batch, n_groups),
        in_specs=[_bs((seq, LANE), lambda b, g: (b, col0 + g))], out_specs=_bs((seq, LANE), lambda b, g: (b, g)),
        scratch_shapes=[pltpu.VMEM((seq + POOL_HALO, LANE), F32)], name=name,
        compiler_params=_params("parallel", "parallel"))(proj)


def _pool_bwd(dzp, n_groups, n_batch, seq, name):
    ch = min(256, seq)

    def body(d_ref, o_ref, pad):
        w = lax.shift_left(jnp.int32(2), pl.program_id(1))
        for c0 in range(0, seq, ch):
            t = c0 + lax.broadcasted_iota(jnp.int32, (ch, LANE), 0)
            cnt = jnp.minimum(t + 1, w).astype(F32)
            pad[c0:c0 + ch, :] = d_ref[c0:c0 + ch, :] / cnt
        pad[seq:seq + POOL_HALO, :] = jnp.zeros((POOL_HALO, LANE), F32)
        for c0 in range(0, seq, ch):
            acc = jnp.zeros((ch, LANE), F32)
            for j in range(POOL_WINDOW_MAX):
                acc = acc + jnp.where(j < w, 1.0, 0.0).astype(F32) * pad[pl.ds(c0 + j, ch), :]
            o_ref[c0:c0 + ch, :] = (acc - d_ref[c0:c0 + ch, :]).astype(BF16)

    tok = _bs((seq, LANE), lambda b, g: (b, g))
    return pl.pallas_call(
        body, out_shape=jax.ShapeDtypeStruct((n_batch * seq, n_groups * LANE), BF16), grid=(n_batch, n_groups),
        in_specs=[tok], out_specs=tok, scratch_shapes=[pltpu.VMEM((seq + POOL_HALO, LANE), F32)], name=name,
        compiler_params=_params("parallel", "parallel"))(dzp)


def _merge_fwd(proj, col0, yc, yp, scale, name):
    t_dim, d = yc.shape
    half = d // 2
    tm = _tile(t_dim, 512)
    c0 = col0 // half

    def body(gc_ref, gp_ref, yc_ref, yp_ref, s_ref, o_ref):
        o_ref[...] = (_sig(gc_ref[...]) * yc_ref[...] + _sig(gp_ref[...]) * (yp_ref[...] * s_ref[...])).astype(BF16)

    blk = _bs((tm, half), lambda i, j: (i, j))
    return pl.pallas_call(
        body, out_shape=jax.ShapeDtypeStruct((t_dim, d), BF16), grid=(t_dim // tm, 2),
        in_specs=[_bs((tm, half), lambda i, j: (i, c0 + j)), _bs((tm, half), lambda i, j: (i, c0 + 2 + j)),
                  blk, blk, _bs((1, half), lambda i, j: (0, j))],
        out_specs=blk, name=name, compiler_params=_params("parallel", "parallel"))(proj, proj, yc, yp, scale)


def _merge_bwd(proj, col0, yc, yp, scale, dm, name):
    t_dim, d = yc.shape
    half = d // 2
    tm = _tile(t_dim, 512)
    c0 = col0 // half

    def body(gc_ref, gp_ref, yc_ref, yp_ref, s_ref, dm_ref, dgc_ref, dgp_ref, dyc_ref, dyp_ref, ds_ref):
        dmv = dm_ref[...].astype(F32)
        sgc = _sig(gc_ref[...])
        sgp = _sig(gp_ref[...])
        sv = s_ref[...]
        ypre = yp_ref[...]
        dgc_ref[...] = (dmv * yc_ref[...] * sgc * (1.0 - sgc)).astype(BF16)
        dgp_ref[...] = (dmv * (ypre * sv) * sgp * (1.0 - sgp)).astype(BF16)
        dyc_ref[...] = (dmv * sgc).astype(BF16)
        dyp = dmv * sgp
        dyp_ref[...] = (dyp * sv).astype(BF16)
        part = jnp.sum(dyp * ypre, axis=0, keepdims=True)

        @pl.when(pl.program_id(1) == 0)
        def _():
            ds_ref[...] = part

        @pl.when(pl.program_id(1) > 0)
        def _():
            ds_ref[...] += part

    blk = _bs((tm, half), lambda j, i: (i, j))
    big = jax.ShapeDtypeStruct((t_dim, d), BF16)
    return pl.pallas_call(
        body, out_shape=(big, big, big, big, jax.ShapeDtypeStruct((1, d), F32)), grid=(2, t_dim // tm),
        in_specs=[_bs((tm, half), lambda j, i: (i, c0 + j)), _bs((tm, half), lambda j, i: (i, c0 + 2 + j)),
                  blk, blk, _bs((1, half), lambda j, i: (0, j)), blk],
        out_specs=(blk, blk, blk, blk, _bs((1, half), lambda j, i: (0, j))), name=name,
        compiler_params=_params("parallel", "arbitrary"))(proj, proj, yc, yp, scale, dm)


def _attn_fwd(q, kv, n_batch, seq, m_len, name):
    d = q.shape[1]
    hd = d // XA_HEADS
    tq = _tile(seq, 1024)
    nq = seq // tq
    scale = hd ** -0.5

    def body(q_ref, k_ref, v_ref, o_ref):
        sc = lax.dot_general(q_ref[...].astype(BF16), k_ref[...].astype(BF16), (((1,), (1,)), ((), ())),
                             preferred_element_type=F32) * scale
        p = jnp.exp(sc - jnp.max(sc, axis=-1, keepdims=True))
        pr = p / jnp.sum(p, axis=-1, keepdims=True)
        o_ref[...] = jnp.dot(pr.astype(BF16), v_ref[...].astype(BF16), preferred_element_type=F32).astype(BF16)

    return pl.pallas_call(
        body, out_shape=jax.ShapeDtypeStruct((n_batch * seq, d), BF16), grid=(n_batch, XA_HEADS, nq),
        in_specs=[_bs((tq, hd), lambda b, h, i: (b * nq + i, h)), _bs((m_len, hd), lambda b, h, i: (b, h)),
                  _bs((m_len, hd), lambda b, h, i: (b, XA_HEADS + h))],
        out_specs=_bs((tq, hd), lambda b, h, i: (b * nq + i, h)), name=name,
        compiler_params=_params("parallel", "parallel", "parallel"))(q, kv, kv)


def _attn_bwd(q, kv, datt, n_batch, seq, m_len, name):
    d = q.shape[1]
    hd = d // XA_HEADS
    tq = _tile(seq, 1024)
    nq = seq // tq
    scale = hd ** -0.5

    def body(q_ref, k_ref, v_ref, do_ref, dq_ref, dk_ref, dv_ref):
        qb = q_ref[...].astype(BF16)
        kb = k_ref[...].astype(BF16)
        vb = v_ref[...].astype(BF16)
        dob = do_ref[...].astype(BF16)
        sc = lax.dot_general(qb, kb, (((1,), (1,)), ((), ())), preferred_element_type=F32) * scale
        p = jnp.exp(sc - jnp.max(sc, axis=-1, keepdims=True))
        pr = p / jnp.sum(p, axis=-1, keepdims=True)
        dpr = lax.dot_general(dob, vb, (((1,), (1,)), ((), ())), preferred_element_type=F32)
        dsc = pr * (dpr - jnp.sum(dpr * pr, axis=-1, keepdims=True)) * scale
        dsb = dsc.astype(BF16)
        dq_ref[...] = jnp.dot(dsb, kb, preferred_element_type=F32).astype(BF16)
        dv_part = lax.dot_general(pr.astype(BF16), dob, (((0,), (0,)), ((), ())), preferred_element_type=F32)
        dk_part = lax.dot_general(dsb, qb, (((0,), (0,)), ((), ())), preferred_element_type=F32)

        @pl.when(pl.program_id(2) == 0)
        def _():
            dk_ref[...] = dk_part
            dv_ref[...] = dv_part

        @pl.when(pl.program_id(2) > 0)
        def _():
            dk_ref[...] += dk_part
            dv_ref[...] += dv_part

    qs = _bs((tq, hd), lambda b, h, i: (b * nq + i, h))
    ks = _bs((m_len, hd), lambda b, h, i: (b, h))
    return pl.pallas_call(
        body, out_shape=(jax.ShapeDtypeStruct((n_batch * seq, d), BF16), jax.ShapeDtypeStruct((n_batch * m_len, d), F32),
                         jax.ShapeDtypeStruct((n_batch * m_len, d), F32)),
        grid=(n_batch, XA_HEADS, nq),
        in_specs=[qs, ks, _bs((m_len, hd), lambda b, h, i: (b, XA_HEADS + h)), qs],
        out_specs=(qs, ks, ks), name=name,
        compiler_params=_params("parallel", "parallel", "arbitrary"))(q, kv, kv, datt)


def _gelu_parts(g):
    th = jnp.tanh(GELU_C0 * (g + GELU_C1 * g * g * g))
    return th, 0.5 * g * (1.0 + th)


def _ffn_act_fwd(up0, dw_w, n_batch, seq, name):
    kk, c2 = dw_w.shape
    f_dim = c2 // 2
    wd = 2 * LANE
    nj = f_dim // wd
    ch = min(128, seq)

    def body(g_ref, v_ref, wg_ref, wv_ref, o_ref, padg, padv):
        for pad, src in ((padg, g_ref), (padv, v_ref)):
            pad[0:FFN_HALO, :] = jnp.zeros((FFN_HALO, wd), F32)
            pad[FFN_HALO:FFN_HALO + seq, :] = src[...]
        for c0 in range(0, seq, ch):
            gate = jnp.zeros((ch, wd), F32)
            val = jnp.zeros((ch, wd), F32)
            for k in range(kk):
                off = c0 + FFN_HALO - (kk - 1) + k
                gate = gate + wg_ref[k:k + 1, :] * padg[pl.ds(off, ch), :]
                val = val + wv_ref[k:k + 1, :] * padv[pl.ds(off, ch), :]
            o_ref[c0:c0 + ch, :] = (_gelu_parts(gate)[1] * val).astype(BF16)

    return pl.pallas_call(
        body, out_shape=jax.ShapeDtypeStruct((n_batch * seq, f_dim), BF16), grid=(n_batch, nj),
        in_specs=[_bs((seq, wd), lambda b, j: (b, j)), _bs((seq, wd), lambda b, j: (b, nj + j)),
                  _bs((kk, wd), lambda b, j: (0, j)), _bs((kk, wd), lambda b, j: (0, nj + j))],
        out_specs=_bs((seq, wd), lambda b, j: (b, j)),
        scratch_shapes=[pltpu.VMEM((seq + FFN_HALO, wd), F32), pltpu.VMEM((seq + FFN_HALO, wd), F32)], name=name,
        compiler_params=_params("parallel", "parallel"))(up0, up0, dw_w, dw_w)


def _ffn_act_bwd(up0, dw_w, dact, n_batch, seq, name):
    kk, c2 = dw_w.shape
    f_dim = c2 // 2
    wd = 2 * LANE
    nj = f_dim // wd
    ch = min(128, seq)

    def body(g_ref, v_ref, wg_ref, wv_ref, da_ref, dg_ref, dv_ref, dwg_ref, dwv_ref, padg, padv, pbg, pbv):
        for pad, src in ((padg, g_ref), (padv, v_ref)):
            pad[0:FFN_HALO, :] = jnp.zeros((FFN_HALO, wd), F32)
            pad[FFN_HALO:FFN_HALO + seq, :] = src[...]
        for pb in (pbg, pbv):
            pb[seq:seq + FFN_HALO, :] = jnp.zeros((FFN_HALO, wd), F32)

        @pl.when(pl.program_id(1) == 0)
        def _():
            dwg_ref[...] = jnp.zeros((kk, wd), F32)
            dwv_ref[...] = jnp.zeros((kk, wd), F32)

        for c0 in range(0, seq, ch):
            gate = jnp.zeros((ch, wd), F32)
            val = jnp.zeros((ch, wd), F32)
            for k in range(kk):
                off = c0 + FFN_HALO - (kk - 1) + k
                gate = gate + wg_ref[k:k + 1, :] * padg[pl.ds(off, ch), :]
                val = val + wv_ref[k:k + 1, :] * padv[pl.ds(off, ch), :]
            th, gelu = _gelu_parts(gate)
            dgelu = 0.5 * (1.0 + th) + 0.5 * gate * (1.0 - th * th) * GELU_C0 * (1.0 + 3.0 * GELU_C1 * gate * gate)
            dav = da_ref[c0:c0 + ch, :].astype(F32)
            pbg[c0:c0 + ch, :] = dav * val * dgelu
            pbv[c0:c0 + ch, :] = dav * gelu
        for pb, pad, w_ref, d_ref, dw_ref in ((pbg, padg, wg_ref, dg_ref, dwg_ref), (pbv, padv, wv_ref, dv_ref, dwv_ref)):
            for c0 in range(0, seq, ch):
                acc = jnp.zeros((ch, wd), F32)
                for k in range(kk):
                    acc = acc + w_ref[k:k + 1, :] * pb[pl.ds(c0 + (kk - 1) - k, ch), :]
                d_ref[c0:c0 + ch, :] = acc.astype(BF16)
            for k in range(kk):
                s = jnp.zeros((1, wd), F32)
                for c0 in range(0, seq, ch):
                    s = s + jnp.sum(pb[c0:c0 + ch, :] * pad[pl.ds(c0 + FFN_HALO - (kk - 1) + k, ch), :],
                                    axis=0, keepdims=True)
                dw_ref[k:k + 1, :] += s

    t_dim = n_batch * seq
    tok = _bs((seq, wd), lambda j, b: (b, j))
    wblk = _bs((kk, wd), lambda j, b: (0, j))
    pad_shape = pltpu.VMEM((seq + FFN_HALO, wd), F32)
    return pl.pallas_call(
        body, out_shape=(jax.ShapeDtypeStruct((t_dim, f_dim), BF16), jax.ShapeDtypeStruct((t_dim, f_dim), BF16),
                         jax.ShapeDtypeStruct((kk, f_dim), F32), jax.ShapeDtypeStruct((kk, f_dim), F32)),
        grid=(nj, n_batch),
        in_specs=[tok, _bs((seq, wd), lambda j, b: (b, nj + j)), wblk, _bs((kk, wd), lambda j, b: (0, nj + j)), tok],
        out_specs=(tok, tok, wblk, wblk), scratch_shapes=[pad_shape, pad_shape, pad_shape, pad_shape], name=name,
        compiler_params=_params("parallel", "arbitrary"))(up0, up0, dw_w, dw_w, dact)


def _sum_rows(parts, out_dtype, name):
    r_dim, c_dim = parts[0].shape
    tr = _tile(r_dim, 512, SUBLANE)
    n = len(parts)

    def body(*refs):
        acc = refs[0][...].astype(F32)
        for r in refs[1:n]:
            acc = acc + r[...].astype(F32)
        refs[n][...] = acc.astype(out_dtype)

    blk = _bs((tr, c_dim), lambda i: (i, 0))
    return pl.pallas_call(
        body, out_shape=jax.ShapeDtypeStruct((r_dim, c_dim), out_dtype), grid=(r_dim // tr,),
        in_specs=[blk] * n, out_specs=blk, name=name, compiler_params=_params("parallel"))(*parts)


def _adamw(w, g, m, v, name):
    shape = w.shape
    c_dim = shape[-1]
    r_dim = w.size // c_dim
    two_d = lambda t: t.reshape(r_dim, c_dim)
    tr = _tile(r_dim, max(SUBLANE, (256 * 1024) // max(c_dim, LANE) // SUBLANE * SUBLANE), SUBLANE)
    c1 = 1.0 - ADAM_B1 ** ADAM_STEP
    c2 = 1.0 - ADAM_B2 ** ADAM_STEP

    def body(w_ref, g_ref, m_ref, v_ref, d_ref, mo_ref, vo_ref):
        gv = g_ref[...]
        mn = ADAM_B1 * m_ref[...] + (1.0 - ADAM_B1) * gv
        vn = ADAM_B2 * v_ref[...] + (1.0 - ADAM_B2) * (gv * gv)
        mo_ref[...] = mn
        vo_ref[...] = vn
        d_ref[...] = -ADAM_LR * ((mn / c1) / (jnp.sqrt(vn / c2) + ADAM_EPS) + ADAM_WD * w_ref[...])

    blk = _bs((tr, c_dim), lambda i: (i, 0))
    out = jax.ShapeDtypeStruct((r_dim, c_dim), F32)
    d, mo, vo = pl.pallas_call(
        body, out_shape=(out, out, out), grid=(r_dim // tr,), in_specs=[blk] * 4, out_specs=(blk, blk, blk),
        name=name, compiler_params=_params("parallel"))(two_d(w), two_d(g), two_d(m), two_d(v))
    return d.reshape(shape), mo.reshape(shape), vo.reshape(shape)


HBM_SPEC = pl.BlockSpec(memory_space=pltpu.HBM)


def _position():
    return lax.axis_index("x"), lax.axis_index("y"), lax.axis_index("c")


def _all_gather(shard, name):
    def body(x_ref, out_ref, send_sems, recv_sems, local_sem):
        x, y, c = _position()
        me, sibling = (x, y, c), (x, y, 1 - c)
        chips = [(1 - x, y), (x, 1 - y), (1 - x, 1 - y)]

        def rows(px, py, pc):
            return out_ref.at[4 * px + 2 * py + pc]

        def copy(k, block, to, src=None):
            return pltpu.make_async_remote_copy(
                src_ref=rows(*block) if src is None else src, dst_ref=rows(*block),
                send_sem=send_sems.at[k], recv_sem=recv_sems.at[k], device_id=to, device_id_type=MESH)

        mine = pltpu.make_async_copy(x_ref, rows(*me), local_sem)
        mine.start()
        first = [copy(0, me, sibling, src=x_ref)]
        first += [copy(1 + j, me, (*chip, c), src=x_ref) for j, chip in enumerate(chips)]
        for cp in first:
            cp.start()
        passed = [copy(4 + j, (*chip, c), sibling) for j, chip in enumerate(chips)]
        for j, chip in enumerate(chips):
            copy(1 + j, (*chip, c), me).wait_recv()
            passed[j].start()
        copy(0, sibling, me).wait_recv()
        for j, chip in enumerate(chips):
            copy(4 + j, (*chip, 1 - c), me).wait_recv()
        for cp in first + passed:
            cp.wait_send()
        mine.wait()

    return pl.pallas_call(
        body, out_shape=jax.ShapeDtypeStruct((N_DEV,) + shard.shape, shard.dtype),
        in_specs=[HBM_SPEC], out_specs=HBM_SPEC,
        scratch_shapes=[pltpu.SemaphoreType.DMA((7,)), pltpu.SemaphoreType.DMA((7,)), pltpu.SemaphoreType.DMA(())],
        name=name)(shard)


CHIP_RELATIONS = ((0, 0), (1, 0), (0, 1), (1, 1))


def _rs_pair_exchange(g, name):
    _, r_dim, c_dim = g.shape
    n = len(CHIP_RELATIONS)

    def body(g_ref, recv_ref, send_sems, recv_sems):
        x, y, c = _position()
        sibling = (x, y, 1 - c)
        copies = []
        for k, (rx, ry) in enumerate(CHIP_RELATIONS):
            px = x + rx - 2 * x * rx
            py = y + ry - 2 * y * ry
            copies.append(pltpu.make_async_remote_copy(
                src_ref=g_ref.at[4 * px + 2 * py + 1 - c], dst_ref=recv_ref.at[k], send_sem=send_sems.at[k],
                recv_sem=recv_sems.at[k], device_id=sibling, device_id_type=MESH))
        for cp in copies:
            cp.start()
        for cp in copies:
            cp.wait()

    return pl.pallas_call(
        body, out_shape=jax.ShapeDtypeStruct((n, r_dim, c_dim), g.dtype), in_specs=[HBM_SPEC], out_specs=HBM_SPEC,
        scratch_shapes=[pltpu.SemaphoreType.DMA((n,)), pltpu.SemaphoreType.DMA((n,))], name=name)(g)


def _rs_pair_sum(g, recv, name):
    _, r_dim, c_dim = g.shape
    n = len(CHIP_RELATIONS)
    tr = _tile(r_dim, 512, SUBLANE)
    x, y, c = _position()
    own = jnp.stack([4 * (x + rx - 2 * x * rx) + 2 * (y + ry - 2 * y * ry) + c for rx, ry in CHIP_RELATIONS])

    def body(own_ref, g_ref, r_ref, o_ref):
        o_ref[...] = (g_ref[...].astype(F32) + r_ref[...].astype(F32)).astype(o_ref.dtype)

    blk = _bs((None, tr, c_dim), lambda k, i, own_ref: (k, i, 0))
    return pl.pallas_call(
        body, out_shape=jax.ShapeDtypeStruct((n, r_dim, c_dim), g.dtype),
        grid_spec=pltpu.PrefetchScalarGridSpec(
            num_scalar_prefetch=1, grid=(n, r_dim // tr),
            in_specs=[_bs((None, tr, c_dim), lambda k, i, own_ref: (own_ref[k], i, 0)), blk], out_specs=blk),
        name=name, compiler_params=_params("parallel", "parallel"))(own.astype(jnp.int32), g, recv)


def _rs_chip_exchange(p, name):
    _, r_dim, c_dim = p.shape

    def body(p_ref, recv_ref, send_sems, recv_sems):
        x, y, c = _position()
        copies = []
        for k, (rx, ry) in enumerate(CHIP_RELATIONS[1:]):
            px = x + rx - 2 * x * rx
            py = y + ry - 2 * y * ry
            copies.append(pltpu.make_async_remote_copy(
                src_ref=p_ref.at[k + 1], dst_ref=recv_ref.at[k], send_sem=send_sems.at[k],
                recv_sem=recv_sems.at[k], device_id=(px, py, c), device_id_type=MESH))
        for cp in copies:
            cp.start()
        for cp in copies:
            cp.wait()

    return pl.pallas_call(
        body, out_shape=jax.ShapeDtypeStruct((3, r_dim, c_dim), p.dtype), in_specs=[HBM_SPEC], out_specs=HBM_SPEC,
        scratch_shapes=[pltpu.SemaphoreType.DMA((3,)), pltpu.SemaphoreType.DMA((3,))], name=name)(p)


def _reduce_scatter(g, name):
    _, r_dim, c_dim = g.shape
    recv = _rs_pair_exchange(g, name + "_pair")
    pair = _rs_pair_sum(g, recv, name + "_pairsum")
    far = _rs_chip_exchange(pair, name + "_chips")
    return _sum_rows([pair[0], far[0], far[1], far[2]], F32, name + "_sum")


MATRICES = (("w_in", True), ("w_out", False), ("w_q", False), ("w_kv", True), ("w_o", False), ("w_up", True),
            ("w_down", False), ("w_conv_out", True), ("w_pool_grp", True))


def _to_rows(name, transposed, w, d_model):
    if name == "w_pool_grp":
        w = jnp.swapaxes(w, 1, 2)
    elif transposed:
        w = w.T
    return w.reshape(-1, d_model)


def _from_rows(name, transposed, rows, shard_shape):
    if name == "w_pool_grp":
        g, i, o = shard_shape
        return jnp.swapaxes(rows.reshape(g, o, i), 1, 2)
    if transposed:
        return rows.reshape(shard_shape[1], shard_shape[0]).T
    return rows.reshape(shard_shape)


def _gathered_matrix(name, blocks, shard_shape):
    if name == "w_pool_grp":
        g, i, o = shard_shape
        return jnp.swapaxes(blocks.reshape(N_DEV, g, o, i), 0, 1).reshape(g, N_DEV * o, i)
    if name == "w_conv_out":
        return blocks.reshape(N_DEV * shard_shape[1], shard_shape[0])
    return blocks.reshape(-1, blocks.shape[-1])


def _scatter_blocks(name, full, shard_shape, d_model):
    if name == "w_pool_grp":
        g, i, o = shard_shape
        return jnp.swapaxes(full.reshape(g, N_DEV, o, i), 0, 1).reshape(N_DEV, -1, d_model)
    return full.reshape(N_DEV, -1, d_model)


def kernel(x, mem, mix_norm_g, w_in, conv_dw_w, conv_dw_b, conv_ln_g, conv_ln_b, w_conv_out, w_pool_grp, pool_scale, w_out, xattn_norm_g, mem_norm_g, w_q, w_kv, w_o, ffn_norm_g, w_up, ffn_dw_w, w_down, final_norm_g, loss_target, m_mix_norm_g, m_w_in, m_conv_dw_w, m_conv_dw_b, m_conv_ln_g, m_conv_ln_b, m_w_conv_out, m_w_pool_grp, m_pool_scale, m_w_out, m_xattn_norm_g, m_mem_norm_g, m_w_q, m_w_kv, m_w_o, m_ffn_norm_g, m_w_up, m_ffn_dw_w, m_w_down, m_final_norm_g, v_mix_norm_g, v_w_in, v_conv_dw_w, v_conv_dw_b, v_conv_ln_g, v_conv_ln_b, v_w_conv_out, v_w_pool_grp, v_pool_scale, v_w_out, v_xattn_norm_g, v_mem_norm_g, v_w_q, v_w_kv, v_w_o, v_ffn_norm_g, v_w_up, v_ffn_dw_w, v_w_down, v_final_norm_g):
    p = dict(locals())
    weight_names = ["mix_norm_g", "w_in", "conv_dw_w", "conv_dw_b", "conv_ln_g", "conv_ln_b", "w_conv_out",
                    "w_pool_grp", "pool_scale", "w_out", "xattn_norm_g", "mem_norm_g", "w_q", "w_kv", "w_o",
                    "ffn_norm_g", "w_up", "ffn_dw_w", "w_down", "final_norm_g"]
    n_batch, seq, d_model = x.shape
    m_len = mem.shape[1]
    depth = w_in.shape[0]
    t_dim = n_batch * seq
    c_conv = conv_dw_b.shape[1]
    n_groups = w_pool_grp.shape[1]
    assert w_pool_grp.shape[2] == LANE and c_conv % LANE == 0 and n_groups * LANE == c_conv
    gate_col0 = 2 * c_conv + n_groups * LANE
    pool_col0 = (2 * c_conv) // LANE

    pieces, layout = [], []
    for l in range(depth):
        for name, tr in MATRICES:
            rows = _to_rows(name, tr, p[name][l], d_model)
            layout.append((l, name, tr, rows.shape[0]))
            pieces.append(rows.astype(BF16))
    filt = jnp.concatenate([conv_dw_w.reshape(-1), ffn_dw_w.reshape(-1)])
    filt_rows = lax.bitcast_convert_type(filt, BF16).reshape(-1, d_model)
    pieces.append(filt_rows)
    gathered = _all_gather(jnp.concatenate(pieces, axis=0), "weights_all_gather")

    full = [dict() for _ in range(depth)]
    row0 = 0
    for l, name, tr, nrows in layout:
        full[l][name] = _gathered_matrix(name, gathered[:, row0:row0 + nrows], p[name].shape[1:])
        row0 += nrows
    filt_all = lax.bitcast_convert_type(
        gathered[:, row0:row0 + filt_rows.shape[0]].reshape(N_DEV, -1, 2), F32)
    n_cw = conv_dw_w.size
    kc, cs = conv_dw_w.shape[1:]
    kf, fs = ffn_dw_w.shape[1:]
    conv_w_full = jnp.moveaxis(filt_all[:, :n_cw].reshape(N_DEV, depth, kc, cs), 0, 2).reshape(depth, kc, N_DEV * cs)
    ffn_w_full = jnp.moveaxis(filt_all[:, n_cw:].reshape(N_DEV, depth, kf, fs), 0, 2).reshape(depth, kf, N_DEV * fs)

    vec = lambda a: a.reshape(1, -1)
    x2d = x.reshape(t_dim, d_model)
    mem2d = mem.reshape(n_batch * m_len, d_model)
    mem_n = _rmsnorm_fwd(mem2d, vec(mem_norm_g), "mem_norm")

    saved = []
    xc = x2d
    for l in range(depth):
        wl = full[l]
        s = {"x0": xc}
        s["h"] = _rmsnorm_fwd(xc, vec(mix_norm_g[l]), f"mix_norm_l{l}")
        s["proj"] = _matmul(s["h"], wl["w_in"], "nt", f"in_proj_l{l}")
        s["y1"] = _glu_conv_fwd(s["proj"], conv_w_full[l], vec(conv_dw_b[l]), n_batch, seq, f"glu_conv_l{l}")
        s["y3"] = _ln_silu_fwd(s["y1"], vec(conv_ln_g[l]), vec(conv_ln_b[l]), f"ln_silu_l{l}")
        s["yc"] = _matmul(s["y3"], wl["w_conv_out"], "nt", f"conv_out_l{l}")
        s["zp"] = _pool_fwd(s["proj"], pool_col0, n_groups, n_batch, seq, f"pool_l{l}")
        s["yp"] = _grouped(s["zp"], wl["w_pool_grp"], "nt", f"pool_proj_l{l}")
        s["merged"] = _merge_fwd(s["proj"], gate_col0, s["yc"], s["yp"], vec(pool_scale[l]), f"merge_l{l}")
        s["x1"] = _matmul(s["merged"], wl["w_out"], "nn", f"mix_out_l{l}", res=xc)
        s["hq"] = _rmsnorm_fwd(s["x1"], vec(xattn_norm_g[l]), f"xattn_norm_l{l}")
        s["q"] = _matmul(s["hq"], wl["w_q"], "nn", f"q_proj_l{l}")
        s["kv"] = _matmul(mem_n, wl["w_kv"], "nt", f"kv_proj_l{l}")
        s["att"] = _attn_fwd(s["q"], s["kv"], n_batch, seq, m_len, f"attn_l{l}")
        s["x2"] = _matmul(s["att"], wl["w_o"], "nn", f"attn_out_l{l}", res=s["x1"])
        s["hf"] = _rmsnorm_fwd(s["x2"], vec(ffn_norm_g[l]), f"ffn_norm_l{l}")
        s["up0"] = _matmul(s["hf"], wl["w_up"], "nt", f"up_proj_l{l}")
        s["act"] = _ffn_act_fwd(s["up0"], ffn_w_full[l], n_batch, seq, f"ffn_act_l{l}")
        xc = _matmul(s["act"], wl["w_down"], "nn", f"down_proj_l{l}", res=s["x2"])
        saved.append(s)

    dx, dg_final, loss_part = _loss_head(xc, vec(final_norm_g), loss_target.reshape(t_dim, d_model), "loss_head")

    small = {"final_norm_g": dg_final.reshape(-1)}
    big = [dict() for _ in range(depth)]
    dmem_n = None
    for l in reversed(range(depth)):
        wl, s = full[l], saved[l]
        sm = {}
        dact = _matmul(dx, wl["w_down"], "nt", f"d_act_l{l}")
        big[l]["w_down"] = _matmul(s["act"], dx, "tn", f"d_w_down_l{l}")
        dup_g, dup_v, dwf_g, dwf_v = _ffn_act_bwd(s["up0"], ffn_w_full[l], dact, n_batch, seq, f"ffn_act_bwd_l{l}")
        sm["ffn_dw_w"] = jnp.concatenate([dwf_g, dwf_v], axis=1)
        dup0 = jnp.concatenate([dup_g, dup_v], axis=1)
        dhf = _matmul(dup0, wl["w_up"], "nn", f"d_hf_l{l}")
        big[l]["w_up"] = _matmul(dup0, s["hf"], "tn", f"d_w_up_l{l}")
        dx, dg = _rmsnorm_bwd(s["x2"], vec(ffn_norm_g[l]), dhf, dx, f"ffn_norm_bwd_l{l}")
        sm["ffn_norm_g"] = dg
        datt = _matmul(dx, wl["w_o"], "nt", f"d_att_l{l}")
        big[l]["w_o"] = _matmul(s["att"], dx, "tn", f"d_w_o_l{l}")
        dq, dk, dv = _attn_bwd(s["q"], s["kv"], datt, n_batch, seq, m_len, f"attn_bwd_l{l}")
        dkv = jnp.concatenate([dk, dv], axis=1)
        big[l]["w_kv"] = _matmul(dkv, mem_n, "tn", f"d_w_kv_l{l}")
        dmem_n = _matmul(dkv, wl["w_kv"], "nn", f"d_mem_l{l}", res=dmem_n)
        dhq = _matmul(dq, wl["w_q"], "nt", f"d_hq_l{l}")
        big[l]["w_q"] = _matmul(s["hq"], dq, "tn", f"d_w_q_l{l}")
        dx, dg = _rmsnorm_bwd(s["x1"], vec(xattn_norm_g[l]), dhq, dx, f"xattn_norm_bwd_l{l}")
        sm["xattn_norm_g"] = dg
        dmerged = _matmul(dx, wl["w_out"], "nt", f"d_merged_l{l}")
        big[l]["w_out"] = _matmul(s["merged"], dx, "tn", f"d_w_out_l{l}")
        dgc, dgp, dyc, dyp, dscale = _merge_bwd(s["proj"], gate_col0, s["yc"], s["yp"], vec(pool_scale[l]), dmerged,
                                                f"merge_bwd_l{l}")
        sm["pool_scale"] = dscale
        dzp = _grouped(dyp, wl["w_pool_grp"], "nn", f"d_zp_l{l}")
        big[l]["w_pool_grp"] = _grouped_tn(dyp, s["zp"], n_groups, f"d_w_pool_l{l}")
        du = _pool_bwd(dzp, n_groups, n_batch, seq, f"pool_bwd_l{l}")
        dy3 = _matmul(dyc, wl["w_conv_out"], "nn", f"d_y3_l{l}")
        big[l]["w_conv_out"] = _matmul(dyc, s["y3"], "tn", f"d_w_conv_out_l{l}")
        dy1, dlg, dlb = _ln_silu_bwd(s["y1"], vec(conv_ln_g[l]), vec(conv_ln_b[l]), dy3, f"ln_silu_bwd_l{l}")
        sm["conv_ln_g"], sm["conv_ln_b"] = dlg, dlb
        da, dgl, dcw, dcb = _glu_conv_bwd(s["proj"], conv_w_full[l], dy1, n_batch, seq, f"glu_conv_bwd_l{l}")
        sm["conv_dw_w"], sm["conv_dw_b"] = dcw, dcb
        dproj = jnp.concatenate([da, dgl, du, dgc, dgp], axis=1)
        dh = _matmul(dproj, wl["w_in"], "nn", f"d_h_l{l}")
        big[l]["w_in"] = _matmul(dproj, s["h"], "tn", f"d_w_in_l{l}")
        dx, dg = _rmsnorm_bwd(s["x0"], vec(mix_norm_g[l]), dh, dx, f"mix_norm_bwd_l{l}")
        sm["mix_norm_g"] = dg
        for k, val in sm.items():
            small[(l, k)] = val.reshape(-1)
    _, dg_mem = _rmsnorm_bwd(mem2d, vec(mem_norm_g), dmem_n, None, "mem_norm_bwd")
    small["mem_norm_g"] = dg_mem.reshape(-1)
    small["loss"] = loss_part.reshape(-1)

    blocks = []
    for l, name, tr, nrows in layout:
        blocks.append(_scatter_blocks(name, big[l][name], p[name].shape[1:], d_model).astype(BF16))
    mat_grads = _reduce_scatter(jnp.concatenate(blocks, axis=1), "grad_reduce_scatter")
    grads = {}
    per_layer = {name: [None] * depth for name, _ in MATRICES}
    row0 = 0
    for l, name, tr, nrows in layout:
        per_layer[name][l] = _from_rows(name, tr, mat_grads[row0:row0 + nrows], p[name].shape[1:])
        row0 += nrows
    for name, _ in MATRICES:
        grads[name] = jnp.stack(per_layer[name])

    keys = list(small.keys())
    flat = jnp.concatenate([small[k] for k in keys])
    n_small = flat.shape[0]
    rows_small = -(-n_small // (SUBLANE * d_model)) * SUBLANE
    flat = jnp.pad(flat, (0, rows_small * d_model - n_small)).reshape(rows_small, d_model)
    every = _all_gather(flat, "small_all_gather")
    total = _sum_rows([every[i] for i in range(N_DEV)], F32, "small_sum").reshape(-1)
    off = 0
    red = {}
    for k in keys:
        red[k] = total[off:off + small[k].shape[0]]
        off += small[k].shape[0]
    loss = red["loss"][0]
    dev = 4 * lax.axis_index("x") + 2 * lax.axis_index("y") + lax.axis_index("c")
    for name in ("mix_norm_g", "conv_dw_b", "conv_ln_g", "conv_ln_b", "pool_scale", "xattn_norm_g", "ffn_norm_g"):
        grads[name] = jnp.stack([red[(l, name)] for l in range(depth)])
    grads["conv_dw_w"] = jnp.stack([
        lax.dynamic_slice_in_dim(red[(l, "conv_dw_w")].reshape(kc, N_DEV * cs), dev * cs, cs, axis=1)
        for l in range(depth)])
    grads["ffn_dw_w"] = jnp.stack([
        lax.dynamic_slice_in_dim(red[(l, "ffn_dw_w")].reshape(kf, N_DEV * fs), dev * fs, fs, axis=1)
        for l in range(depth)])
    grads["mem_norm_g"] = red["mem_norm_g"]
    grads["final_norm_g"] = red["final_norm_g"]

    deltas, new_m, new_v = {}, {}, {}
    for name in weight_names:
        deltas[name], new_m[name], new_v[name] = _adamw(p[name], grads[name], p["m_" + name], p["v_" + name],
                                                        f"adamw_{name}")
    grad_x = dx.reshape(n_batch, seq, d_model)
    return (loss, grad_x, *[grads[n] for n in weight_names], *[deltas[n] for n in weight_names],
            *[new_m[n] for n in weight_names], *[new_v[n] for n in weight_names])
```

```python
import functools

import jax
import jax.numpy as jnp
from jax import lax
from jax.experimental import pallas as pl
from jax.experimental.pallas import tpu as pltpu

F32 = jnp.float32
BF16 = jnp.bfloat16
MESH = pl.DeviceIdType.MESH

N_DEV = 8
EPS = 1e-6
V7X_VMEM_BYTES = 64 * 1024 * 1024
VMEM_LIMIT = (V7X_VMEM_BYTES * 3) // 4
LANE = 128
SUBLANE = 8

CONV_HALO = 32
POOL_HALO = 16
FFN_HALO = 8
POOL_WINDOW_MAX = 16
XA_HEADS = 4

ADAM_LR = 0.001
ADAM_B1 = 0.9
ADAM_B2 = 0.999
ADAM_EPS = 1e-08
ADAM_WD = 0.01
ADAM_STEP = 10

GELU_C0 = 0.7978845608028654
GELU_C1 = 0.044715


def _tile(n, cap, mult=LANE):
    if n <= cap:
        return n
    best = None
    for d in range(mult, cap + 1, mult):
        if n % d == 0:
            best = d
    assert best is not None, (n, cap, mult)
    return best


def _params(*sem):
    return pltpu.CompilerParams(dimension_semantics=sem, vmem_limit_bytes=VMEM_LIMIT)


def _sig(x):
    return 1.0 / (1.0 + jnp.exp(-x))


def _bs(shape, imap):
    return pl.BlockSpec(shape, imap)


def _matmul(a, b, mode, name, res=None, out_dtype=F32):
    if mode == "tn":
        k_dim, m_dim = a.shape
        k2, n_dim = b.shape
    elif mode == "nn":
        m_dim, k_dim = a.shape
        k2, n_dim = b.shape
    else:
        m_dim, k_dim = a.shape
        n_dim, k2 = b.shape
    assert k_dim == k2, (name, a.shape, b.shape)
    tm = _tile(m_dim, 1408 if mode == "tn" else 1024)
    tn = _tile(n_dim, 1408)
    wide = a.dtype == F32 or b.dtype == F32
    tk = _tile(k_dim, 1024 if wide else 2048)
    nk = k_dim // tk
    use_acc = nk > 1 and out_dtype != F32
    if mode == "tn":
        a_spec, ca = _bs((tk, tm), lambda i, j, k: (k, i)), 0
    else:
        a_spec, ca = _bs((tm, tk), lambda i, j, k: (i, k)), 1
    if mode == "nt":
        b_spec, cb = _bs((tn, tk), lambda i, j, k: (j, k)), 1
    else:
        b_spec, cb = _bs((tk, tn), lambda i, j, k: (k, j)), 0
    dims = (((ca,), (cb,)), ((), ()))
    o_spec = _bs((tm, tn), lambda i, j, k: (i, j))
    has_res = res is not None

    def body(*refs):
        a_ref, b_ref = refs[:2]
        r_ref = refs[2] if has_res else None
        o_ref = refs[3 if has_res else 2]
        k = pl.program_id(2)
        part = lax.dot_general(a_ref[...].astype(BF16), b_ref[...].astype(BF16), dims,
                               preferred_element_type=F32)
        if nk == 1:
            if has_res:
                part = part + r_ref[...].astype(F32)
            o_ref[...] = part.astype(out_dtype)
            return
        acc = refs[-1] if use_acc else o_ref

        @pl.when(k == 0)
        def _():
            acc[...] = part + r_ref[...].astype(F32) if has_res else part

        @pl.when(k > 0)
        def _():
            acc[...] += part

        if use_acc:
            @pl.when(k == nk - 1)
            def _():
                o_ref[...] = acc[...].astype(out_dtype)

    in_specs = [a_spec, b_spec] + ([o_spec] if has_res else [])
    args = (a, b) + ((res,) if has_res else ())
    return pl.pallas_call(
        body, out_shape=jax.ShapeDtypeStruct((m_dim, n_dim), out_dtype),
        grid=(m_dim // tm, n_dim // tn, nk), in_specs=in_specs, out_specs=o_spec,
        scratch_shapes=[pltpu.VMEM((tm, tn), F32)] if use_acc else [], name=name,
        compiler_params=_params("parallel", "parallel", "arbitrary"))(*args)


def _grouped(a, w, mode, name, out_dtype=F32):
    t_dim = a.shape[0]
    g_dim, r_dim, c_dim = w.shape
    ka, no = (c_dim, r_dim) if mode == "nt" else (r_dim, c_dim)
    tm = _tile(t_dim, 512)
    dims = (((1,), (1 if mode == "nt" else 0,)), ((), ()))

    def body(a_ref, w_ref, o_ref):
        o_ref[...] = lax.dot_general(a_ref[...].astype(BF16), w_ref[...].astype(BF16), dims,
                                     preferred_element_type=F32).astype(out_dtype)

    return pl.pallas_call(
        body, out_shape=jax.ShapeDtypeStruct((t_dim, g_dim * no), out_dtype),
        grid=(t_dim // tm, g_dim),
        in_specs=[_bs((tm, ka), lambda i, g: (i, g)), _bs((None, r_dim, c_dim), lambda i, g: (g, 0, 0))],
        out_specs=_bs((tm, no), lambda i, g: (i, g)), name=name,
        compiler_params=_params("parallel", "parallel"))(a, w)


def _grouped_tn(a, b, g_dim, name):
    t_dim = a.shape[0]
    ra = a.shape[1] // g_dim
    cb = b.shape[1] // g_dim
    tm = _tile(t_dim, 512)
    nt = t_dim // tm

    def body(a_ref, b_ref, o_ref):
        part = lax.dot_general(a_ref[...].astype(BF16), b_ref[...].astype(BF16), (((0,), (0,)), ((), ())),
                               preferred_element_type=F32)

        @pl.when(pl.program_id(1) == 0)
        def _():
            o_ref[...] = part

        @pl.when(pl.program_id(1) > 0)
        def _():
            o_ref[...] += part

    return pl.pallas_call(
        body, out_shape=jax.ShapeDtypeStruct((g_dim, ra, cb), F32), grid=(g_dim, nt),
        in_specs=[_bs((tm, ra), lambda g, i: (i, g)), _bs((tm, cb), lambda g, i: (i, g))],
        out_specs=_bs((None, ra, cb), lambda g, i: (g, 0, 0)), name=name,
        compiler_params=_params("parallel", "arbitrary"))(a, b)


def _rmsnorm_fwd(x, g, name):
    t_dim, d = x.shape
    tm = _tile(t_dim, 512)

    def body(x_ref, g_ref, o_ref):
        xv = x_ref[...]
        r = lax.rsqrt(jnp.mean(xv * xv, axis=-1, keepdims=True) + EPS)
        o_ref[...] = (xv * r * g_ref[...]).astype(BF16)

    return pl.pallas_call(
        body, out_shape=jax.ShapeDtypeStruct((t_dim, d), BF16), grid=(t_dim // tm,),
        in_specs=[_bs((tm, d), lambda i: (i, 0)), _bs((1, d), lambda i: (0, 0))],
        out_specs=_bs((tm, d), lambda i: (i, 0)), name=name, compiler_params=_params("parallel"))(x, g)


def _rmsnorm_bwd(x, g, dh, dx_in, name):
    t_dim, d = x.shape
    tm = _tile(t_dim, 512)
    has_in = dx_in is not None

    def body(*refs):
        if has_in:
            x_ref, g_ref, dh_ref, di_ref, dx_ref, dxb_ref, dg_ref = refs
        else:
            x_ref, g_ref, dh_ref, dx_ref, dxb_ref, dg_ref = refs
        xv = x_ref[...]
        r = lax.rsqrt(jnp.mean(xv * xv, axis=-1, keepdims=True) + EPS)
        xh = xv * r
        dhv = dh_ref[...].astype(F32)
        dxh = dhv * g_ref[...]
        dx = r * (dxh - xh * jnp.mean(dxh * xh, axis=-1, keepdims=True))
        if has_in:
            dx = dx + di_ref[...]
        dx_ref[...] = dx
        dxb_ref[...] = dx.astype(BF16)
        part = jnp.sum(dhv * xh, axis=0, keepdims=True)

        @pl.when(pl.program_id(0) == 0)
        def _():
            dg_ref[...] = part

        @pl.when(pl.program_id(0) > 0)
        def _():
            dg_ref[...] += part

    row = _bs((tm, d), lambda i: (i, 0))
    vec = _bs((1, d), lambda i: (0, 0))
    args = (x, g, dh) + ((dx_in,) if has_in else ())
    return pl.pallas_call(
        body, out_shape=(jax.ShapeDtypeStruct((t_dim, d), F32), jax.ShapeDtypeStruct((t_dim, d), BF16),
                         jax.ShapeDtypeStruct((1, d), F32)),
        grid=(t_dim // tm,), in_specs=[row, vec, row] + ([row] if has_in else []),
        out_specs=(row, row, vec), name=name, compiler_params=_params("arbitrary"))(*args)


def _loss_head(x, g, tgt, name):
    t_dim, d = x.shape
    tm = _tile(t_dim, 512)

    def body(x_ref, g_ref, t_ref, dx_ref, dxb_ref, dg_ref, loss_ref):
        xv = x_ref[...]
        gv = g_ref[...]
        r = lax.rsqrt(jnp.mean(xv * xv, axis=-1, keepdims=True) + EPS)
        xh = xv * r
        err = xh * gv - t_ref[...]
        dy = err * (1.0 / d)
        dxh = dy * gv
        dx = r * (dxh - xh * jnp.mean(dxh * xh, axis=-1, keepdims=True))
        dx_ref[...] = dx
        dxb_ref[...] = dx.astype(BF16)
        dg_part = jnp.sum(dy * xh, axis=0, keepdims=True)
        loss_part = jnp.full((1, LANE), 0.5 * jnp.sum(jnp.mean(err * err, axis=-1, keepdims=True)), F32)

        @pl.when(pl.program_id(0) == 0)
        def _():
            dg_ref[...] = dg_part
            loss_ref[...] = loss_part

        @pl.when(pl.program_id(0) > 0)
        def _():
            dg_ref[...] += dg_part
            loss_ref[...] += loss_part

    row = _bs((tm, d), lambda i: (i, 0))
    vec = _bs((1, d), lambda i: (0, 0))
    return pl.pallas_call(
        body, out_shape=(jax.ShapeDtypeStruct((t_dim, d), F32), jax.ShapeDtypeStruct((t_dim, d), BF16),
                         jax.ShapeDtypeStruct((1, d), F32), jax.ShapeDtypeStruct((1, LANE), F32)),
        grid=(t_dim // tm,), in_specs=[row, vec, row],
        out_specs=(row, row, vec, _bs((1, LANE), lambda i: (0, 0))), name=name,
        compiler_params=_params("arbitrary"))(x, g, tgt)


def _glu_conv_fwd(proj, dw_w, dw_b, n_batch, seq, name):
    kk, cc = dw_w.shape
    nj = cc // LANE
    ch = min(256, seq)

    def body(a_ref, gl_ref, w_ref, b_ref, o_ref, pad):
        pad[0:CONV_HALO, :] = jnp.zeros((CONV_HALO, LANE), F32)
        pad[CONV_HALO:CONV_HALO + seq, :] = a_ref[...] * _sig(gl_ref[...])
        for c0 in range(0, seq, ch):
            acc = jnp.broadcast_to(b_ref[...], (ch, LANE))
            for k in range(kk):
                acc = acc + w_ref[k:k + 1, :] * pad[pl.ds(c0 + CONV_HALO - (kk - 1) + k, ch), :]
            o_ref[c0:c0 + ch, :] = acc

    return pl.pallas_call(
        body, out_shape=jax.ShapeDtypeStruct((n_batch * seq, cc), F32), grid=(n_batch, nj),
        in_specs=[_bs((seq, LANE), lambda b, j: (b, j)), _bs((seq, LANE), lambda b, j: (b, nj + j)),
                  _bs((kk, LANE), lambda b, j: (0, j)), _bs((1, LANE), lambda b, j: (0, j))],
        out_specs=_bs((seq, LANE), lambda b, j: (b, j)),
        scratch_shapes=[pltpu.VMEM((seq + CONV_HALO, LANE), F32)], name=name,
        compiler_params=_params("parallel", "parallel"))(proj, proj, dw_w, dw_b)


def _glu_conv_bwd(proj, dw_w, dy1, n_batch, seq, name):
    kk, cc = dw_w.shape
    nj = cc // LANE
    ch = min(256, seq)

    def body(a_ref, gl_ref, dy_ref, w_ref, da_ref, dgl_ref, dw_ref, db_ref, padf, padb):
        first = pl.program_id(1) == 0
        padf[0:CONV_HALO, :] = jnp.zeros((CONV_HALO, LANE), F32)
        padf[CONV_HALO:CONV_HALO + seq, :] = a_ref[...] * _sig(gl_ref[...])
        padb[0:seq, :] = dy_ref[...]
        padb[seq:seq + CONV_HALO, :] = jnp.zeros((CONV_HALO, LANE), F32)

        @pl.when(first)
        def _():
            dw_ref[...] = jnp.zeros((kk, LANE), F32)
            db_ref[...] = jnp.zeros((1, LANE), F32)

        for c0 in range(0, seq, ch):
            acc = jnp.zeros((ch, LANE), F32)
            for k in range(kk):
                acc = acc + w_ref[k:k + 1, :] * padb[pl.ds(c0 + (kk - 1) - k, ch), :]
            sg = _sig(gl_ref[c0:c0 + ch, :])
            da_ref[c0:c0 + ch, :] = (acc * sg).astype(BF16)
            dgl_ref[c0:c0 + ch, :] = (acc * a_ref[c0:c0 + ch, :] * sg * (1.0 - sg)).astype(BF16)
        for k in range(kk):
            s = jnp.zeros((1, LANE), F32)
            for c0 in range(0, seq, ch):
                s = s + jnp.sum(padb[c0:c0 + ch, :] * padf[pl.ds(c0 + CONV_HALO - (kk - 1) + k, ch), :],
                                axis=0, keepdims=True)
            dw_ref[k:k + 1, :] += s
        db_ref[...] += jnp.sum(dy_ref[...], axis=0, keepdims=True)

    tok = _bs((seq, LANE), lambda j, b: (b, j))
    t_dim = n_batch * seq
    return pl.pallas_call(
        body, out_shape=(jax.ShapeDtypeStruct((t_dim, cc), BF16), jax.ShapeDtypeStruct((t_dim, cc), BF16),
                         jax.ShapeDtypeStruct((kk, cc), F32), jax.ShapeDtypeStruct((1, cc), F32)),
        grid=(nj, n_batch),
        in_specs=[tok, _bs((seq, LANE), lambda j, b: (b, nj + j)), tok, _bs((kk, LANE), lambda j, b: (0, j))],
        out_specs=(tok, tok, _bs((kk, LANE), lambda j, b: (0, j)), _bs((1, LANE), lambda j, b: (0, j))),
        scratch_shapes=[pltpu.VMEM((seq + CONV_HALO, LANE), F32), pltpu.VMEM((seq + CONV_HALO, LANE), F32)],
        name=name, compiler_params=_params("parallel", "arbitrary"))(proj, proj, dy1, dw_w)


def _ln_silu_fwd(y1, g, b, name):
    t_dim, c = y1.shape
    tm = _tile(t_dim, 512)

    def body(y_ref, g_ref, b_ref, o_ref):
        yv = y_ref[...]
        xc = yv - jnp.mean(yv, axis=-1, keepdims=True)
        rstd = lax.rsqrt(jnp.mean(xc * xc, axis=-1, keepdims=True) + EPS)
        y2 = xc * rstd * g_ref[...] + b_ref[...]
        o_ref[...] = (y2 * _sig(y2)).astype(BF16)

    row = _bs((tm, c), lambda i: (i, 0))
    vec = _bs((1, c), lambda i: (0, 0))
    return pl.pallas_call(
        body, out_shape=jax.ShapeDtypeStruct((t_dim, c), BF16), grid=(t_dim // tm,),
        in_specs=[row, vec, vec], out_specs=row, name=name, compiler_params=_params("parallel"))(y1, g, b)


def _ln_silu_bwd(y1, g, b, dy3, name):
    t_dim, c = y1.shape
    tm = _tile(t_dim, 512)

    def body(y_ref, g_ref, b_ref, d_ref, dy_ref, dg_ref, db_ref):
        yv = y_ref[...]
        gv = g_ref[...]
        xc = yv - jnp.mean(yv, axis=-1, keepdims=True)
        rstd = lax.rsqrt(jnp.mean(xc * xc, axis=-1, keepdims=True) + EPS)
        yh = xc * rstd
        y2 = yh * gv + b_ref[...]
        s = _sig(y2)
        dy2 = d_ref[...].astype(F32) * (s * (1.0 + y2 * (1.0 - s)))
        dyh = dy2 * gv
        dy_ref[...] = rstd * (dyh - jnp.mean(dyh, axis=-1, keepdims=True)
                              - yh * jnp.mean(dyh * yh, axis=-1, keepdims=True))
        dg_part = jnp.sum(dy2 * yh, axis=0, keepdims=True)
        db_part = jnp.sum(dy2, axis=0, keepdims=True)

        @pl.when(pl.program_id(0) == 0)
        def _():
            dg_ref[...] = dg_part
            db_ref[...] = db_part

        @pl.when(pl.program_id(0) > 0)
        def _():
            dg_ref[...] += dg_part
            db_ref[...] += db_part

    row = _bs((tm, c), lambda i: (i, 0))
    vec = _bs((1, c), lambda i: (0, 0))
    return pl.pallas_call(
        body, out_shape=(jax.ShapeDtypeStruct((t_dim, c), F32), jax.ShapeDtypeStruct((1, c), F32),
                         jax.ShapeDtypeStruct((1, c), F32)),
        grid=(t_dim // tm,), in_specs=[row, vec, vec, row], out_specs=(row, vec, vec), name=name,
        compiler_params=_params("arbitrary"))(y1, g, b, dy3)


def _pool_fwd(proj, col0, n_groups, n_batch, seq, name):
    ch = min(256, seq)

    def body(u_ref, o_ref, pad):
        w = lax.shift_left(jnp.int32(2), pl.program_id(1))
        pad[0:POOL_HALO, :] = jnp.zeros((POOL_HALO, LANE), F32)
        pad[POOL_HALO:POOL_HALO + seq, :] = u_ref[...]
        for c0 in range(0, seq, ch):
            acc = jnp.zeros((ch, LANE), F32)
            for j in range(POOL_WINDOW_MAX):
                acc = acc + jnp.where(j < w, 1.0, 0.0).astype(F32) * pad[pl.ds(c0 + POOL_HALO - j, ch), :]
            t = c0 + lax.broadcasted_iota(jnp.int32, (ch, LANE), 0)
            cnt = jnp.minimum(t + 1, w).astype(F32)
            o_ref[c0:c0 + ch, :] = (acc / cnt - u_ref[c0:c0 + ch, :]).astype(BF16)

    return pl.pallas_call(
        body, out_shape=jax.ShapeDtypeStruct((n_batch * seq, n_groups * LANE), BF16), grid=(n_batch, n_groups),
        in_specs=[_bs((seq, LANE), lambda b, g: (b, col0 + g))], out_specs=_bs((seq, LANE), lambda b, g: (b, g)),
        scratch_shapes=[pltpu.VMEM((seq + POOL_HALO, LANE), F32)], name=name,
        compiler_params=_params("parallel", "parallel"))(proj)


def _pool_bwd(dzp, n_groups, n_batch, seq, name):
    ch = min(256, seq)

    def body(d_ref, o_ref, pad):
        w = lax.shift_left(jnp.int32(2), pl.program_id(1))
        for c0 in range(0, seq, ch):
            t = c0 + lax.broadcasted_iota(jnp.int32, (ch, LANE), 0)
            cnt = jnp.minimum(t + 1, w).astype(F32)
            pad[c0:c0 + ch, :] = d_ref[c0:c0 + ch, :] / cnt
        pad[seq:seq + POOL_HALO, :] = jnp.zeros((POOL_HALO, LANE), F32)
        for c0 in range(0, seq, ch):
            acc = jnp.zeros((ch, LANE), F32)
            for j in range(POOL_WINDOW_MAX):
                acc = acc + jnp.where(j < w, 1.0, 0.0).astype(F32) * pad[pl.ds(c0 + j, ch), :]
            o_ref[c0:c0 + ch, :] = (acc - d_ref[c0:c0 + ch, :]).astype(BF16)

    tok = _bs((seq, LANE), lambda b, g: (b, g))
    return pl.pallas_call(
        body, out_shape=jax.ShapeDtypeStruct((n_batch * seq, n_groups * LANE), BF16), grid=(n_batch, n_groups),
        in_specs=[tok], out_specs=tok, scratch_shapes=[pltpu.VMEM((seq + POOL_HALO, LANE), F32)], name=name,
        compiler_params=_params("parallel", "parallel"))(dzp)


def _merge_fwd(proj, col0, yc, yp, scale, name):
    t_dim, d = yc.shape
    half = d // 2
    tm = _tile(t_dim, 512)
    c0 = col0 // half

    def body(gc_ref, gp_ref, yc_ref, yp_ref, s_ref, o_ref):
        o_ref[...] = (_sig(gc_ref[...]) * yc_ref[...] + _sig(gp_ref[...]) * (yp_ref[...] * s_ref[...])).astype(BF16)

    blk = _bs((tm, half), lambda i, j: (i, j))
    return pl.pallas_call(
        body, out_shape=jax.ShapeDtypeStruct((t_dim, d), BF16), grid=(t_dim // tm, 2),
        in_specs=[_bs((tm, half), lambda i, j: (i, c0 + j)), _bs((tm, half), lambda i, j: (i, c0 + 2 + j)),
                  blk, blk, _bs((1, half), lambda i, j: (0, j))],
        out_specs=blk, name=name, compiler_params=_params("parallel", "parallel"))(proj, proj, yc, yp, scale)


def _merge_bwd(proj, col0, yc, yp, scale, dm, name):
    t_dim, d = yc.shape
    half = d // 2
    tm = _tile(t_dim, 512)
    c0 = col0 // half

    def body(gc_ref, gp_ref, yc_ref, yp_ref, s_ref, dm_ref, dgc_ref, dgp_ref, dyc_ref, dyp_ref, ds_ref):
        dmv = dm_ref[...].astype(F32)
        sgc = _sig(gc_ref[...])
        sgp = _sig(gp_ref[...])
        sv = s_ref[...]
        ypre = yp_ref[...]
        dgc_ref[...] = (dmv * yc_ref[...] * sgc * (1.0 - sgc)).astype(BF16)
        dgp_ref[...] = (dmv * (ypre * sv) * sgp * (1.0 - sgp)).astype(BF16)
        dyc_ref[...] = (dmv * sgc).astype(BF16)
        dyp = dmv * sgp
        dyp_ref[...] = (dyp * sv).astype(BF16)
        part = jnp.sum(dyp * ypre, axis=0, keepdims=True)

        @pl.when(pl.program_id(1) == 0)
        def _():
            ds_ref[...] = part

        @pl.when(pl.program_id(1) > 0)
        def _():
            ds_ref[...] += part

    blk = _bs((tm, half), lambda j, i: (i, j))
    big = jax.ShapeDtypeStruct((t_dim, d), BF16)
    return pl.pallas_call(
        body, out_shape=(big, big, big, big, jax.ShapeDtypeStruct((1, d), F32)), grid=(2, t_dim // tm),
        in_specs=[_bs((tm, half), lambda j, i: (i, c0 + j)), _bs((tm, half), lambda j, i: (i, c0 + 2 + j)),
                  blk, blk, _bs((1, half), lambda j, i: (0, j)), blk],
        out_specs=(blk, blk, blk, blk, _bs((1, half), lambda j, i: (0, j))), name=name,
        compiler_params=_params("parallel", "arbitrary"))(proj, proj, yc, yp, scale, dm)


def _attn_fwd(q, kv, n_batch, seq, m_len, name):
    d = q.shape[1]
    hd = d // XA_HEADS
    tq = _tile(seq, 1024)
    nq = seq // tq
    scale = hd ** -0.5

    def body(q_ref, k_ref, v_ref, o_ref):
        sc = lax.dot_general(q_ref[...].astype(BF16), k_ref[...].astype(BF16), (((1,), (1,)), ((), ())),
                             preferred_element_type=F32) * scale
        p = jnp.exp(sc - jnp.max(sc, axis=-1, keepdims=True))
        pr = p / jnp.sum(p, axis=-1, keepdims=True)
        o_ref[...] = jnp.dot(pr.astype(BF16), v_ref[...].astype(BF16), preferred_element_type=F32).astype(BF16)

    return pl.pallas_call(
        body, out_shape=jax.ShapeDtypeStruct((n_batch * seq, d), BF16), grid=(n_batch, XA_HEADS, nq),
        in_specs=[_bs((tq, hd), lambda b, h, i: (b * nq + i, h)), _bs((m_len, hd), lambda b, h, i: (b, h)),
                  _bs((m_len, hd), lambda b, h, i: (b, XA_HEADS + h))],
        out_specs=_bs((tq, hd), lambda b, h, i: (b * nq + i, h)), name=name,
        compiler_params=_params("parallel", "parallel", "parallel"))(q, kv, kv)


def _attn_bwd(q, kv, datt, n_batch, seq, m_len, name):
    d = q.shape[1]
    hd = d // XA_HEADS
    tq = _tile(seq, 1024)
    nq = seq // tq
    scale = hd ** -0.5

    def body(q_ref, k_ref, v_ref, do_ref, dq_ref, dk_ref, dv_ref):
        qb = q_ref[...].astype(BF16)
        kb = k_ref[...].astype(BF16)
        vb = v_ref[...].astype(BF16)
        dob = do_ref[...].astype(BF16)
        sc = lax.dot_general(qb, kb, (((1,), (1,)), ((), ())), preferred_element_type=F32) * scale
        p = jnp.exp(sc - jnp.max(sc, axis=-1, keepdims=True))
        pr = p / jnp.sum(p, axis=-1, keepdims=True)
        dpr = lax.dot_general(dob, vb, (((1,), (1,)), ((), ())), preferred_element_type=F32)
        dsc = pr * (dpr - jnp.sum(dpr * pr, axis=-1, keepdims=True)) * scale
        dsb = dsc.astype(BF16)
        dq_ref[...] = jnp.dot(dsb, kb, preferred_element_type=F32).astype(BF16)
        dv_part = lax.dot_general(pr.astype(BF16), dob, (((0,), (0,)), ((), ())), preferred_element_type=F32)
        dk_part = lax.dot_general(dsb, qb, (((0,), (0,)), ((), ())), preferred_element_type=F32)

        @pl.when(pl.program_id(2) == 0)
        def _():
            dk_ref[...] = dk_part
            dv_ref[...] = dv_part

        @pl.when(pl.program_id(2) > 0)
        def _():
            dk_ref[...] += dk_part
            dv_ref[...] += dv_part

    qs = _bs((tq, hd), lambda b, h, i: (b * nq + i, h))
    ks = _bs((m_len, hd), lambda b, h, i: (b, h))
    return pl.pallas_call(
        body, out_shape=(jax.ShapeDtypeStruct((n_batch * seq, d), BF16), jax.ShapeDtypeStruct((n_batch * m_len, d), F32),
                         jax.ShapeDtypeStruct((n_batch * m_len, d), F32)),
        grid=(n_batch, XA_HEADS, nq),
        in_specs=[qs, ks, _bs((m_len, hd), lambda b, h, i: (b, XA_HEADS + h)), qs],
        out_specs=(qs, ks, ks), name=name,
        compiler_params=_params("parallel", "parallel", "arbitrary"))(q, kv, kv, datt)


def _gelu_parts(g):
    th = jnp.tanh(GELU_C0 * (g + GELU_C1 * g * g * g))
    return th, 0.5 * g * (1.0 + th)


def _ffn_act_fwd(up0, dw_w, n_batch, seq, name):
    kk, c2 = dw_w.shape
    f_dim = c2 // 2
    wd = 2 * LANE
    nj = f_dim // wd
    ch = min(128, seq)

    def body(g_ref, v_ref, wg_ref, wv_ref, o_ref, padg, padv):
        for pad, src in ((padg, g_ref), (padv, v_ref)):
            pad[0:FFN_HALO, :] = jnp.zeros((FFN_HALO, wd), F32)
            pad[FFN_HALO:FFN_HALO + seq, :] = src[...]
        for c0 in range(0, seq, ch):
            gate = jnp.zeros((ch, wd), F32)
            val = jnp.zeros((ch, wd), F32)
            for k in range(kk):
                off = c0 + FFN_HALO - (kk - 1) + k
                gate = gate + wg_ref[k:k + 1, :] * padg[pl.ds(off, ch), :]
                val = val + wv_ref[k:k + 1, :] * padv[pl.ds(off, ch), :]
            o_ref[c0:c0 + ch, :] = (_gelu_parts(gate)[1] * val).astype(BF16)

    return pl.pallas_call(
        body, out_shape=jax.ShapeDtypeStruct((n_batch * seq, f_dim), BF16), grid=(n_batch, nj),
        in_specs=[_bs((seq, wd), lambda b, j: (b, j)), _bs((seq, wd), lambda b, j: (b, nj + j)),
                  _bs((kk, wd), lambda b, j: (0, j)), _bs((kk, wd), lambda b, j: (0, nj + j))],
        out_specs=_bs((seq, wd), lambda b, j: (b, j)),
        scratch_shapes=[pltpu.VMEM((seq + FFN_HALO, wd), F32), pltpu.VMEM((seq + FFN_HALO, wd), F32)], name=name,
        compiler_params=_params("parallel", "parallel"))(up0, up0, dw_w, dw_w)


def _ffn_act_bwd(up0, dw_w, dact, n_batch, seq, name):
    kk, c2 = dw_w.shape
    f_dim = c2 // 2
    wd = 2 * LANE
    nj = f_dim // wd
    ch = min(128, seq)

    def body(g_ref, v_ref, wg_ref, wv_ref, da_ref, dg_ref, dv_ref, dwg_ref, dwv_ref, padg, padv, pbg, pbv):
        for pad, src in ((padg, g_ref), (padv, v_ref)):
            pad[0:FFN_HALO, :] = jnp.zeros((FFN_HALO, wd), F32)
            pad[FFN_HALO:FFN_HALO + seq, :] = src[...]
        for pb in (pbg, pbv):
            pb[seq:seq + FFN_HALO, :] = jnp.zeros((FFN_HALO, wd), F32)

        @pl.when(pl.program_id(1) == 0)
        def _():
            dwg_ref[...] = jnp.zeros((kk, wd), F32)
            dwv_ref[...] = jnp.zeros((kk, wd), F32)

        for c0 in range(0, seq, ch):
            gate = jnp.zeros((ch, wd), F32)
            val = jnp.zeros((ch, wd), F32)
            for k in range(kk):
                off = c0 + FFN_HALO - (kk - 1) + k
                gate = gate + wg_ref[k:k + 1, :] * padg[pl.ds(off, ch), :]
                val = val + wv_ref[k:k + 1, :] * padv[pl.ds(off, ch), :]
            th, gelu = _gelu_parts(gate)
            dgelu = 0.5 * (1.0 + th) + 0.5 * gate * (1.0 - th * th) * GELU_C0 * (1.0 + 3.0 * GELU_C1 * gate * gate)
            dav = da_ref[c0:c0 + ch, :].astype(F32)
            pbg[c0:c0 + ch, :] = dav * val * dgelu
            pbv[c0:c0 + ch, :] = dav * gelu
        for pb, pad, w_ref, d_ref, dw_ref in ((pbg, padg, wg_ref, dg_ref, dwg_ref), (pbv, padv, wv_ref, dv_ref, dwv_ref)):
            for c0 in range(0, seq, ch):
                acc = jnp.zeros((ch, wd), F32)
                for k in range(kk):
                    acc = acc + w_ref[k:k + 1, :] * pb[pl.ds(c0 + (kk - 1) - k, ch), :]
                d_ref[c0:c0 + ch, :] = acc.astype(BF16)
            for k in range(kk):
                s = jnp.zeros((1, wd), F32)
                for c0 in range(0, seq, ch):
                    s = s + jnp.sum(pb[c0:c0 + ch, :] * pad[pl.ds(c0 + FFN_HALO - (kk - 1) + k, ch), :],
                                    axis=0, keepdims=True)
                dw_ref[k:k + 1, :] += s

    t_dim = n_batch * seq
    tok = _bs((seq, wd), lambda j, b: (b, j))
    wblk = _bs((kk, wd), lambda j, b: (0, j))
    pad_shape = pltpu.VMEM((seq + FFN_HALO, wd), F32)
    return pl.pallas_call(
        body, out_shape=(jax.ShapeDtypeStruct((t_dim, f_dim), BF16), jax.ShapeDtypeStruct((t_dim, f_dim), BF16),
                         jax.ShapeDtypeStruct((kk, f_dim), F32), jax.ShapeDtypeStruct((kk, f_dim), F32)),
        grid=(nj, n_batch),
        in_specs=[tok, _bs((seq, wd), lambda j, b: (b, nj + j)), wblk, _bs((kk, wd), lambda j, b: (0, nj + j)), tok],
        out_specs=(tok, tok, wblk, wblk), scratch_shapes=[pad_shape, pad_shape, pad_shape, pad_shape], name=name,
        compiler_params=_params("parallel", "arbitrary"))(up0, up0, dw_w, dw_w, dact)


def _sum_rows(parts, out_dtype, name):
    r_dim, c_dim = parts[0].shape
    tr = _tile(r_dim, 1200, SUBLANE)
    n = len(parts)

    def body(*refs):
        acc = refs[0][...].astype(F32)
        for r in refs[1:n]:
            acc = acc + r[...].astype(F32)
        refs[n][...] = acc.astype(out_dtype)

    blk = _bs((tr, c_dim), lambda i: (i, 0))
    return pl.pallas_call(
        body, out_shape=jax.ShapeDtypeStruct((r_dim, c_dim), out_dtype), grid=(r_dim // tr,),
        in_specs=[blk] * n, out_specs=blk, name=name, compiler_params=_params("parallel"))(*parts)


def _adamw(w, g, m, v, name):
    shape = w.shape
    c_dim = shape[-1]
    r_dim = w.size // c_dim
    two_d = lambda t: t.reshape(r_dim, c_dim)
    tr = _tile(r_dim, max(SUBLANE, (256 * 1024) // max(c_dim, LANE) // SUBLANE * SUBLANE), SUBLANE)
    c1 = 1.0 - ADAM_B1 ** ADAM_STEP
    c2 = 1.0 - ADAM_B2 ** ADAM_STEP

    def body(w_ref, g_ref, m_ref, v_ref, d_ref, mo_ref, vo_ref):
        gv = g_ref[...]
        mn = ADAM_B1 * m_ref[...] + (1.0 - ADAM_B1) * gv
        vn = ADAM_B2 * v_ref[...] + (1.0 - ADAM_B2) * (gv * gv)
        mo_ref[...] = mn
        vo_ref[...] = vn
        d_ref[...] = -ADAM_LR * ((mn / c1) / (jnp.sqrt(vn / c2) + ADAM_EPS) + ADAM_WD * w_ref[...])

    blk = _bs((tr, c_dim), lambda i: (i, 0))
    out = jax.ShapeDtypeStruct((r_dim, c_dim), F32)
    d, mo, vo = pl.pallas_call(
        body, out_shape=(out, out, out), grid=(r_dim // tr,), in_specs=[blk] * 4, out_specs=(blk, blk, blk),
        name=name, compiler_params=_params("parallel"))(two_d(w), two_d(g), two_d(m), two_d(v))
    return d.reshape(shape), mo.reshape(shape), vo.reshape(shape)


HBM_SPEC = pl.BlockSpec(memory_space=pltpu.HBM)


def _position():
    return lax.axis_index("x"), lax.axis_index("y"), lax.axis_index("c")


def _all_gather(shard, name):
    def body(x_ref, out_ref, send_sems, recv_sems, local_sem):
        x, y, c = _position()
        me, sibling = (x, y, c), (x, y, 1 - c)
        chips = [(1 - x, y), (x, 1 - y), (1 - x, 1 - y)]

        def rows(px, py, pc):
            return out_ref.at[4 * px + 2 * py + pc]

        def copy(k, block, to, src=None):
            return pltpu.make_async_remote_copy(
                src_ref=rows(*block) if src is None else src, dst_ref=rows(*block),
                send_sem=send_sems.at[k], recv_sem=recv_sems.at[k], device_id=to, device_id_type=MESH)

        mine = pltpu.make_async_copy(x_ref, rows(*me), local_sem)
        mine.start()
        first = [copy(0, me, sibling, src=x_ref)]
        first += [copy(1 + j, me, (*chip, c), src=x_ref) for j, chip in enumerate(chips)]
        for cp in first:
            cp.start()
        passed = [copy(4 + j, (*chip, c), sibling) for j, chip in enumerate(chips)]
        for j, chip in enumerate(chips):
            copy(1 + j, (*chip, c), me).wait_recv()
            passed[j].start()
        copy(0, sibling, me).wait_recv()
        for j, chip in enumerate(chips):
            copy(4 + j, (*chip, 1 - c), me).wait_recv()
        for cp in first + passed:
            cp.wait_send()
        mine.wait()

    return pl.pallas_call(
        body, out_shape=jax.ShapeDtypeStruct((N_DEV,) + shard.shape, shard.dtype),
        in_specs=[HBM_SPEC], out_specs=HBM_SPEC,
        scratch_shapes=[pltpu.SemaphoreType.DMA((7,)), pltpu.SemaphoreType.DMA((7,)), pltpu.SemaphoreType.DMA(())],
        name=name)(shard)


CHIP_RELATIONS = ((0, 0), (1, 0), (0, 1), (1, 1))


def _rs_pair_exchange(g, name):
    _, r_dim, c_dim = g.shape
    n = len(CHIP_RELATIONS)

    def body(g_ref, recv_ref, send_sems, recv_sems):
        x, y, c = _position()
        sibling = (x, y, 1 - c)
        copies = []
        for k, (rx, ry) in enumerate(CHIP_RELATIONS):
            px = x + rx - 2 * x * rx
            py = y + ry - 2 * y * ry
            copies.append(pltpu.make_async_remote_copy(
                src_ref=g_ref.at[4 * px + 2 * py + 1 - c], dst_ref=recv_ref.at[k], send_sem=send_sems.at[k],
                recv_sem=recv_sems.at[k], device_id=sibling, device_id_type=MESH))
        for cp in copies:
            cp.start()
        for cp in copies:
            cp.wait()

    return pl.pallas_call(
        body, out_shape=jax.ShapeDtypeStruct((n, r_dim, c_dim), g.dtype), in_specs=[HBM_SPEC], out_specs=HBM_SPEC,
        scratch_shapes=[pltpu.SemaphoreType.DMA((n,)), pltpu.SemaphoreType.DMA((n,))], name=name)(g)


def _rs_pair_sum(g, recv, name):
    _, r_dim, c_dim = g.shape
    n = len(CHIP_RELATIONS)
    tr = _tile(r_dim, 1200, SUBLANE)
    x, y, c = _position()
    own = jnp.stack([4 * (x + rx - 2 * x * rx) + 2 * (y + ry - 2 * y * ry) + c for rx, ry in CHIP_RELATIONS])

    def body(own_ref, g_ref, r_ref, o_ref):
        o_ref[...] = (g_ref[...].astype(F32) + r_ref[...].astype(F32)).astype(o_ref.dtype)

    blk = _bs((None, tr, c_dim), lambda k, i, own_ref: (k, i, 0))
    return pl.pallas_call(
        body, out_shape=jax.ShapeDtypeStruct((n, r_dim, c_dim), g.dtype),
        grid_spec=pltpu.PrefetchScalarGridSpec(
            num_scalar_prefetch=1, grid=(n, r_dim // tr),
            in_specs=[_bs((None, tr, c_dim), lambda k, i, own_ref: (own_ref[k], i, 0)), blk], out_specs=blk),
        name=name, compiler_params=_params("parallel", "parallel"))(own.astype(jnp.int32), g, recv)


def _rs_chip_exchange(p, name):
    _, r_dim, c_dim = p.shape

    def body(p_ref, recv_ref, send_sems, recv_sems):
        x, y, c = _position()
        copies = []
        for k, (rx, ry) in enumerate(CHIP_RELATIONS[1:]):
            px = x + rx - 2 * x * rx
            py = y + ry - 2 * y * ry
            copies.append(pltpu.make_async_remote_copy(
                src_ref=p_ref.at[k + 1], dst_ref=recv_ref.at[k], send_sem=send_sems.at[k],
                recv_sem=recv_sems.at[k], device_id=(px, py, c), device_id_type=MESH))
        for cp in copies:
            cp.start()
        for cp in copies:
            cp.wait()

    return pl.pallas_call(
        body, out_shape=jax.ShapeDtypeStruct((3, r_dim, c_dim), p.dtype), in_specs=[HBM_SPEC], out_specs=HBM_SPEC,
        scratch_shapes=[pltpu.SemaphoreType.DMA((3,)), pltpu.SemaphoreType.DMA((3,))], name=name)(p)


def _reduce_scatter(g, name):
    _, r_dim, c_dim = g.shape
    recv = _rs_pair_exchange(g, name + "_pair")
    pair = _rs_pair_sum(g, recv, name + "_pairsum")
    far = _rs_chip_exchange(pair, name + "_chips")
    return _sum_rows([pair[0], far[0], far[1], far[2]], F32, name + "_sum")


MATRICES = (("w_in", True), ("w_out", False), ("w_q", False), ("w_kv", True), ("w_o", False), ("w_up", True),
            ("w_down", False), ("w_conv_out", True), ("w_pool_grp", True))


def _to_rows(name, transposed, w, d_model):
    if name == "w_pool_grp":
        w = jnp.swapaxes(w, 1, 2)
    elif transposed:
        w = w.T
    return w.reshape(-1, d_model)


def _from_rows(name, transposed, rows, shard_shape):
    if name == "w_pool_grp":
        g, i, o = shard_shape
        return jnp.swapaxes(rows.reshape(g, o, i), 1, 2)
    if transposed:
        return rows.reshape(shard_shape[1], shard_shape[0]).T
    return rows.reshape(shard_shape)


def _gathered_matrix(name, blocks, shard_shape):
    if name == "w_pool_grp":
        g, i, o = shard_shape
        return jnp.swapaxes(blocks.reshape(N_DEV, g, o, i), 0, 1).reshape(g, N_DEV * o, i)
    if name == "w_conv_out":
        return blocks.reshape(N_DEV * shard_shape[1], shard_shape[0])
    return blocks.reshape(-1, blocks.shape[-1])


def _scatter_blocks(name, full, shard_shape, d_model):
    if name == "w_pool_grp":
        g, i, o = shard_shape
        return jnp.swapaxes(full.reshape(g, N_DEV, o, i), 0, 1).reshape(N_DEV, -1, d_model)
    return full.reshape(N_DEV, -1, d_model)


def kernel(x, mem, mix_norm_g, w_in, conv_dw_w, conv_dw_b, conv_ln_g, conv_ln_b, w_conv_out, w_pool_grp, pool_scale, w_out, xattn_norm_g, mem_norm_g, w_q, w_kv, w_o, ffn_norm_g, w_up, ffn_dw_w, w_down, final_norm_g, loss_target, m_mix_norm_g, m_w_in, m_conv_dw_w, m_conv_dw_b, m_conv_ln_g, m_conv_ln_b, m_w_conv_out, m_w_pool_grp, m_pool_scale, m_w_out, m_xattn_norm_g, m_mem_norm_g, m_w_q, m_w_kv, m_w_o, m_ffn_norm_g, m_w_up, m_ffn_dw_w, m_w_down, m_final_norm_g, v_mix_norm_g, v_w_in, v_conv_dw_w, v_conv_dw_b, v_conv_ln_g, v_conv_ln_b, v_w_conv_out, v_w_pool_grp, v_pool_scale, v_w_out, v_xattn_norm_g, v_mem_norm_g, v_w_q, v_w_kv, v_w_o, v_ffn_norm_g, v_w_up, v_ffn_dw_w, v_w_down, v_final_norm_g):
    p = dict(locals())
    weight_names = ["mix_norm_g", "w_in", "conv_dw_w", "conv_dw_b", "conv_ln_g", "conv_ln_b", "w_conv_out",
                    "w_pool_grp", "pool_scale", "w_out", "xattn_norm_g", "mem_norm_g", "w_q", "w_kv", "w_o",
                    "ffn_norm_g", "w_up", "ffn_dw_w", "w_down", "final_norm_g"]
    n_batch, seq, d_model = x.shape
    m_len = mem.shape[1]
    depth = w_in.shape[0]
    t_dim = n_batch * seq
    c_conv = conv_dw_b.shape[1]
    n_groups = w_pool_grp.shape[1]
    assert w_pool_grp.shape[2] == LANE and c_conv % LANE == 0 and n_groups * LANE == c_conv
    gate_col0 = 2 * c_conv + n_groups * LANE
    pool_col0 = (2 * c_conv) // LANE

    pieces, layout = [], []
    for l in range(depth):
        for name, tr in MATRICES:
            rows = _to_rows(name, tr, p[name][l], d_model)
            layout.append((l, name, tr, rows.shape[0]))
            pieces.append(rows.astype(BF16))
    filt = jnp.concatenate([conv_dw_w.reshape(-1), ffn_dw_w.reshape(-1)])
    filt_rows = lax.bitcast_convert_type(filt, BF16).reshape(-1, d_model)
    pieces.append(filt_rows)
    gathered = _all_gather(jnp.concatenate(pieces, axis=0), "weights_all_gather")

    full = [dict() for _ in range(depth)]
    row0 = 0
    for l, name, tr, nrows in layout:
        full[l][name] = _gathered_matrix(name, gathered[:, row0:row0 + nrows], p[name].shape[1:])
        row0 += nrows
    filt_all = lax.bitcast_convert_type(
        gathered[:, row0:row0 + filt_rows.shape[0]].reshape(N_DEV, -1, 2), F32)
    n_cw = conv_dw_w.size
    kc, cs = conv_dw_w.shape[1:]
    kf, fs = ffn_dw_w.shape[1:]
    conv_w_full = jnp.moveaxis(filt_all[:, :n_cw].reshape(N_DEV, depth, kc, cs), 0, 2).reshape(depth, kc, N_DEV * cs)
    ffn_w_full = jnp.moveaxis(filt_all[:, n_cw:].reshape(N_DEV, depth, kf, fs), 0, 2).reshape(depth, kf, N_DEV * fs)

    vec = lambda a: a.reshape(1, -1)
    x2d = x.reshape(t_dim, d_model)
    mem2d = mem.reshape(n_batch * m_len, d_model)
    mem_n = _rmsnorm_fwd(mem2d, vec(mem_norm_g), "mem_norm")

    saved = []
    xc = x2d
    for l in range(depth):
        wl = full[l]
        s = {"x0": xc}
        s["h"] = _rmsnorm_fwd(xc, vec(mix_norm_g[l]), f"mix_norm_l{l}")
        s["proj"] = _matmul(s["h"], wl["w_in"], "nt", f"in_proj_l{l}")
        s["y1"] = _glu_conv_fwd(s["proj"], conv_w_full[l], vec(conv_dw_b[l]), n_batch, seq, f"glu_conv_l{l}")
        s["y3"] = _ln_silu_fwd(s["y1"], vec(conv_ln_g[l]), vec(conv_ln_b[l]), f"ln_silu_l{l}")
        s["yc"] = _matmul(s["y3"], wl["w_conv_out"], "nt", f"conv_out_l{l}")
        s["zp"] = _pool_fwd(s["proj"], pool_col0, n_groups, n_batch, seq, f"pool_l{l}")
        s["yp"] = _grouped(s["zp"], wl["w_pool_grp"], "nt", f"pool_proj_l{l}")
        s["merged"] = _merge_fwd(s["proj"], gate_col0, s["yc"], s["yp"], vec(pool_scale[l]), f"merge_l{l}")
        s["x1"] = _matmul(s["merged"], wl["w_out"], "nn", f"mix_out_l{l}", res=xc)
        s["hq"] = _rmsnorm_fwd(s["x1"], vec(xattn_norm_g[l]), f"xattn_norm_l{l}")
        s["q"] = _matmul(s["hq"], wl["w_q"], "nn", f"q_proj_l{l}")
        s["kv"] = _matmul(mem_n, wl["w_kv"], "nt", f"kv_proj_l{l}")
        s["att"] = _attn_fwd(s["q"], s["kv"], n_batch, seq, m_len, f"attn_l{l}")
        s["x2"] = _matmul(s["att"], wl["w_o"], "nn", f"attn_out_l{l}", res=s["x1"])
        s["hf"] = _rmsnorm_fwd(s["x2"], vec(ffn_norm_g[l]), f"ffn_norm_l{l}")
        s["up0"] = _matmul(s["hf"], wl["w_up"], "nt", f"up_proj_l{l}")
        s["act"] = _ffn_act_fwd(s["up0"], ffn_w_full[l], n_batch, seq, f"ffn_act_l{l}")
        xc = _matmul(s["act"], wl["w_down"], "nn", f"down_proj_l{l}", res=s["x2"])
        saved.append(s)

    dx, dxb, dg_final, loss_part = _loss_head(xc, vec(final_norm_g), loss_target.reshape(t_dim, d_model), "loss_head")

    small = {"final_norm_g": dg_final.reshape(-1)}
    big = [dict() for _ in range(depth)]
    dmem_n = None
    for l in reversed(range(depth)):
        wl, s = full[l], saved[l]
        sm = {}
        dact = _matmul(dxb, wl["w_down"], "nt", f"d_act_l{l}")
        big[l]["w_down"] = _matmul(s["act"], dxb, "tn", f"d_w_down_l{l}", out_dtype=BF16)
        dup_g, dup_v, dwf_g, dwf_v = _ffn_act_bwd(s["up0"], ffn_w_full[l], dact, n_batch, seq, f"ffn_act_bwd_l{l}")
        sm["ffn_dw_w"] = jnp.concatenate([dwf_g, dwf_v], axis=1)
        dup0 = jnp.concatenate([dup_g, dup_v], axis=1)
        dhf = _matmul(dup0, wl["w_up"], "nn", f"d_hf_l{l}")
        big[l]["w_up"] = _matmul(dup0, s["hf"], "tn", f"d_w_up_l{l}", out_dtype=BF16)
        dx, dxb, dg = _rmsnorm_bwd(s["x2"], vec(ffn_norm_g[l]), dhf, dx, f"ffn_norm_bwd_l{l}")
        sm["ffn_norm_g"] = dg
        datt = _matmul(dxb, wl["w_o"], "nt", f"d_att_l{l}")
        big[l]["w_o"] = _matmul(s["att"], dxb, "tn", f"d_w_o_l{l}", out_dtype=BF16)
        dq, dk, dv = _attn_bwd(s["q"], s["kv"], datt, n_batch, seq, m_len, f"attn_bwd_l{l}")
        dkv = jnp.concatenate([dk, dv], axis=1)
        big[l]["w_kv"] = _matmul(dkv, mem_n, "tn", f"d_w_kv_l{l}", out_dtype=BF16)
        dmem_n = _matmul(dkv, wl["w_kv"], "nn", f"d_mem_l{l}", res=dmem_n)
        dhq = _matmul(dq, wl["w_q"], "nt", f"d_hq_l{l}")
        big[l]["w_q"] = _matmul(s["hq"], dq, "tn", f"d_w_q_l{l}", out_dtype=BF16)
        dx, dxb, dg = _rmsnorm_bwd(s["x1"], vec(xattn_norm_g[l]), dhq, dx, f"xattn_norm_bwd_l{l}")
        sm["xattn_norm_g"] = dg
        dmerged = _matmul(dxb, wl["w_out"], "nt", f"d_merged_l{l}")
        big[l]["w_out"] = _matmul(s["merged"], dxb, "tn", f"d_w_out_l{l}", out_dtype=BF16)
        dgc, dgp, dyc, dyp, dscale = _merge_bwd(s["proj"], gate_col0, s["yc"], s["yp"], vec(pool_scale[l]), dmerged,
                                                f"merge_bwd_l{l}")
        sm["pool_scale"] = dscale
        dzp = _grouped(dyp, wl["w_pool_grp"], "nn", f"d_zp_l{l}")
        big[l]["w_pool_grp"] = _grouped_tn(dyp, s["zp"], n_groups, f"d_w_pool_l{l}")
        du = _pool_bwd(dzp, n_groups, n_batch, seq, f"pool_bwd_l{l}")
        dy3 = _matmul(dyc, wl["w_conv_out"], "nn", f"d_y3_l{l}")
        big[l]["w_conv_out"] = _matmul(dyc, s["y3"], "tn", f"d_w_conv_out_l{l}", out_dtype=BF16)
        dy1, dlg, dlb = _ln_silu_bwd(s["y1"], vec(conv_ln_g[l]), vec(conv_ln_b[l]), dy3, f"ln_silu_bwd_l{l}")
        sm["conv_ln_g"], sm["conv_ln_b"] = dlg, dlb
        da, dgl, dcw, dcb = _glu_conv_bwd(s["proj"], conv_w_full[l], dy1, n_batch, seq, f"glu_conv_bwd_l{l}")
        sm["conv_dw_w"], sm["conv_dw_b"] = dcw, dcb
        dproj = jnp.concatenate([da, dgl, du, dgc, dgp], axis=1)
        dh = _matmul(dproj, wl["w_in"], "nn", f"d_h_l{l}")
        big[l]["w_in"] = _matmul(dproj, s["h"], "tn", f"d_w_in_l{l}", out_dtype=BF16)
        dx, dxb, dg = _rmsnorm_bwd(s["x0"], vec(mix_norm_g[l]), dh, dx, f"mix_norm_bwd_l{l}")
        sm["mix_norm_g"] = dg
        for k, val in sm.items():
            small[(l, k)] = val.reshape(-1)
    _, _, dg_mem = _rmsnorm_bwd(mem2d, vec(mem_norm_g), dmem_n, None, "mem_norm_bwd")
    small["mem_norm_g"] = dg_mem.reshape(-1)
    small["loss"] = loss_part.reshape(-1)

    blocks = []
    for l, name, tr, nrows in layout:
        blocks.append(_scatter_blocks(name, big[l][name], p[name].shape[1:], d_model).astype(BF16))
    mat_grads = _reduce_scatter(jnp.concatenate(blocks, axis=1), "grad_reduce_scatter")
    grads = {}
    per_layer = {name: [None] * depth for name, _ in MATRICES}
    row0 = 0
    for l, name, tr, nrows in layout:
        per_layer[name][l] = _from_rows(name, tr, mat_grads[row0:row0 + nrows], p[name].shape[1:])
        row0 += nrows
    for name, _ in MATRICES:
        grads[name] = jnp.stack(per_layer[name])

    keys = list(small.keys())
    flat = jnp.concatenate([small[k] for k in keys])
    n_small = flat.shape[0]
    rows_small = -(-n_small // (SUBLANE * d_model)) * SUBLANE
    flat = jnp.pad(flat, (0, rows_small * d_model - n_small)).reshape(rows_small, d_model)
    every = _all_gather(flat, "small_all_gather")
    total = _sum_rows([every[i] for i in range(N_DEV)], F32, "small_sum").reshape(-1)
    off = 0
    red = {}
    for k in keys:
        red[k] = total[off:off + small[k].shape[0]]
        off += small[k].shape[0]
    loss = red["loss"][0]
    dev = 4 * lax.axis_index("x") + 2 * lax.axis_index("y") + lax.axis_index("c")
    for name in ("mix_norm_g", "conv_dw_b", "conv_ln_g", "conv_ln_b", "pool_scale", "xattn_norm_g", "ffn_norm_g"):
        grads[name] = jnp.stack([red[(l, name)] for l in range(depth)])
    grads["conv_dw_w"] = jnp.stack([
        lax.dynamic_slice_in_dim(red[(l, "conv_dw_w")].reshape(kc, N_DEV * cs), dev * cs, cs, axis=1)
        for l in range(depth)])
    grads["ffn_dw_w"] = jnp.stack([
        lax.dynamic_slice_in_dim(red[(l, "ffn_dw_w")].reshape(kf, N_DEV * fs), dev * fs, fs, axis=1)
        for l in range(depth)])
    grads["mem_norm_g"] = red["mem_norm_g"]
    grads["final_norm_g"] = red["final_norm_g"]

    deltas, new_m, new_v = {}, {}, {}
    for name in weight_names:
        deltas[name], new_m[name], new_v[name] = _adamw(p[name], grads[name], p["m_" + name], p["v_" + name],
                                                        f"adamw_{name}")
    grad_x = dx.reshape(n_batch, seq, d_model)
    return (loss, grad_x, *[grads[n] for n in weight_names], *[deltas[n] for n in weight_names],
            *[new_m[n] for n in weight_names], *[new_v[n] for n in weight_names])
```

```python
import functools

import jax
import jax.numpy as jnp
from jax import lax
from jax.experimental import pallas as pl
from jax.experimental.pallas import tpu as pltpu

F32 = jnp.float32
BF16 = jnp.bfloat16
MESH = pl.DeviceIdType.MESH

N_DEV = 8
EPS = 1e-6
V7X_VMEM_BYTES = 64 * 1024 * 1024
VMEM_LIMIT = (V7X_VMEM_BYTES * 3) // 4
LANE = 128
SUBLANE = 8

CONV_HALO = 32
POOL_HALO = 16
FFN_HALO = 8
POOL_WINDOW_MAX = 16
XA_HEADS = 4

ADAM_LR = 0.001
ADAM_B1 = 0.9
ADAM_B2 = 0.999
ADAM_EPS = 1e-08
ADAM_WD = 0.01
ADAM_STEP = 10

GELU_C0 = 0.7978845608028654
GELU_C1 = 0.044715


def _tile(n, cap, mult=LANE):
    if n <= cap:
        return n
    best = None
    for d in range(mult, cap + 1, mult):
        if n % d == 0:
            best = d
    assert best is not None, (n, cap, mult)
    return best


def _params(*sem):
    return pltpu.CompilerParams(dimension_semantics=sem, vmem_limit_bytes=VMEM_LIMIT)


def _sig(x):
    return 1.0 / (1.0 + jnp.exp(-x))


def _bs(shape, imap):
    return pl.BlockSpec(shape, imap)


def _matmul(a, b, mode, name, res=None, out_dtype=F32):
    if mode == "tn":
        k_dim, m_dim = a.shape
        k2, n_dim = b.shape
    elif mode == "nn":
        m_dim, k_dim = a.shape
        k2, n_dim = b.shape
    else:
        m_dim, k_dim = a.shape
        n_dim, k2 = b.shape
    assert k_dim == k2, (name, a.shape, b.shape)
    tm = _tile(m_dim, 1408 if mode == "tn" else 1024)
    tn = _tile(n_dim, 1408)
    wide = a.dtype == F32 or b.dtype == F32
    tk = _tile(k_dim, 1024 if wide else 2048)
    nk = k_dim // tk
    use_acc = nk > 1 and out_dtype != F32
    if mode == "tn":
        a_spec, ca = _bs((tk, tm), lambda i, j, k: (k, i)), 0
    else:
        a_spec, ca = _bs((tm, tk), lambda i, j, k: (i, k)), 1
    if mode == "nt":
        b_spec, cb = _bs((tn, tk), lambda i, j, k: (j, k)), 1
    else:
        b_spec, cb = _bs((tk, tn), lambda i, j, k: (k, j)), 0
    dims = (((ca,), (cb,)), ((), ()))
    o_spec = _bs((tm, tn), lambda i, j, k: (i, j))
    has_res = res is not None

    def body(*refs):
        a_ref, b_ref = refs[:2]
        r_ref = refs[2] if has_res else None
        o_ref = refs[3 if has_res else 2]
        k = pl.program_id(2)
        part = lax.dot_general(a_ref[...].astype(BF16), b_ref[...].astype(BF16), dims,
                               preferred_element_type=F32)
        if nk == 1:
            if has_res:
                part = part + r_ref[...].astype(F32)
            o_ref[...] = part.astype(out_dtype)
            return
        acc = refs[-1] if use_acc else o_ref

        @pl.when(k == 0)
        def _():
            acc[...] = part + r_ref[...].astype(F32) if has_res else part

        @pl.when(k > 0)
        def _():
            acc[...] += part

        if use_acc:
            @pl.when(k == nk - 1)
            def _():
                o_ref[...] = acc[...].astype(out_dtype)

    in_specs = [a_spec, b_spec] + ([o_spec] if has_res else [])
    args = (a, b) + ((res,) if has_res else ())
    return pl.pallas_call(
        body, out_shape=jax.ShapeDtypeStruct((m_dim, n_dim), out_dtype),
        grid=(m_dim // tm, n_dim // tn, nk), in_specs=in_specs, out_specs=o_spec,
        scratch_shapes=[pltpu.VMEM((tm, tn), F32)] if use_acc else [], name=name,
        compiler_params=_params("parallel", "parallel", "arbitrary"))(*args)


def _grouped(a, w, mode, name, out_dtype=F32):
    t_dim = a.shape[0]
    g_dim, r_dim, c_dim = w.shape
    ka, no = (c_dim, r_dim) if mode == "nt" else (r_dim, c_dim)
    tm = _tile(t_dim, 512)
    dims = (((1,), (1 if mode == "nt" else 0,)), ((), ()))

    def body(a_ref, w_ref, o_ref):
        o_ref[...] = lax.dot_general(a_ref[...].astype(BF16), w_ref[...].astype(BF16), dims,
                                     preferred_element_type=F32).astype(out_dtype)

    return pl.pallas_call(
        body, out_shape=jax.ShapeDtypeStruct((t_dim, g_dim * no), out_dtype),
        grid=(t_dim // tm, g_dim),
        in_specs=[_bs((tm, ka), lambda i, g: (i, g)), _bs((None, r_dim, c_dim), lambda i, g: (g, 0, 0))],
        out_specs=_bs((tm, no), lambda i, g: (i, g)), name=name,
        compiler_params=_params("parallel", "parallel"))(a, w)


def _grouped_tn(a, b, g_dim, name):
    t_dim = a.shape[0]
    ra = a.shape[1] // g_dim
    cb = b.shape[1] // g_dim
    tm = _tile(t_dim, 512)
    nt = t_dim // tm

    def body(a_ref, b_ref, o_ref):
        part = lax.dot_general(a_ref[...].astype(BF16), b_ref[...].astype(BF16), (((0,), (0,)), ((), ())),
                               preferred_element_type=F32)

        @pl.when(pl.program_id(1) == 0)
        def _():
            o_ref[...] = part

        @pl.when(pl.program_id(1) > 0)
        def _():
            o_ref[...] += part

    return pl.pallas_call(
        body, out_shape=jax.ShapeDtypeStruct((g_dim, ra, cb), F32), grid=(g_dim, nt),
        in_specs=[_bs((tm, ra), lambda g, i: (i, g)), _bs((tm, cb), lambda g, i: (i, g))],
        out_specs=_bs((None, ra, cb), lambda g, i: (g, 0, 0)), name=name,
        compiler_params=_params("parallel", "arbitrary"))(a, b)


def _rmsnorm_fwd(x, g, name):
    t_dim, d = x.shape
    tm = _tile(t_dim, 512)

    def body(x_ref, g_ref, o_ref):
        xv = x_ref[...]
        r = lax.rsqrt(jnp.mean(xv * xv, axis=-1, keepdims=True) + EPS)
        o_ref[...] = (xv * r * g_ref[...]).astype(BF16)

    return pl.pallas_call(
        body, out_shape=jax.ShapeDtypeStruct((t_dim, d), BF16), grid=(t_dim // tm,),
        in_specs=[_bs((tm, d), lambda i: (i, 0)), _bs((1, d), lambda i: (0, 0))],
        out_specs=_bs((tm, d), lambda i: (i, 0)), name=name, compiler_params=_params("parallel"))(x, g)


def _rmsnorm_bwd(x, g, dh, dx_in, name):
    t_dim, d = x.shape
    tm = _tile(t_dim, 512)
    has_in = dx_in is not None

    def body(*refs):
        if has_in:
            x_ref, g_ref, dh_ref, di_ref, dx_ref, dxb_ref, dg_ref = refs
        else:
            x_ref, g_ref, dh_ref, dx_ref, dxb_ref, dg_ref = refs
        xv = x_ref[...]
        r = lax.rsqrt(jnp.mean(xv * xv, axis=-1, keepdims=True) + EPS)
        xh = xv * r
        dhv = dh_ref[...].astype(F32)
        dxh = dhv * g_ref[...]
        dx = r * (dxh - xh * jnp.mean(dxh * xh, axis=-1, keepdims=True))
        if has_in:
            dx = dx + di_ref[...]
        dx_ref[...] = dx
        dxb_ref[...] = dx.astype(BF16)
        part = jnp.sum(dhv * xh, axis=0, keepdims=True)

        @pl.when(pl.program_id(0) == 0)
        def _():
            dg_ref[...] = part

        @pl.when(pl.program_id(0) > 0)
        def _():
            dg_ref[...] += part

    row = _bs((tm, d), lambda i: (i, 0))
    vec = _bs((1, d), lambda i: (0, 0))
    args = (x, g, dh) + ((dx_in,) if has_in else ())
    return pl.pallas_call(
        body, out_shape=(jax.ShapeDtypeStruct((t_dim, d), F32), jax.ShapeDtypeStruct((t_dim, d), BF16),
                         jax.ShapeDtypeStruct((1, d), F32)),
        grid=(t_dim // tm,), in_specs=[row, vec, row] + ([row] if has_in else []),
        out_specs=(row, row, vec), name=name, compiler_params=_params("arbitrary"))(*args)


def _loss_head(x, g, tgt, name):
    t_dim, d = x.shape
    tm = _tile(t_dim, 512)

    def body(x_ref, g_ref, t_ref, dx_ref, dxb_ref, dg_ref, loss_ref):
        xv = x_ref[...]
        gv = g_ref[...]
        r = lax.rsqrt(jnp.mean(xv * xv, axis=-1, keepdims=True) + EPS)
        xh = xv * r
        err = xh * gv - t_ref[...]
        dy = err * (1.0 / d)
        dxh = dy * gv
        dx = r * (dxh - xh * jnp.mean(dxh * xh, axis=-1, keepdims=True))
        dx_ref[...] = dx
        dxb_ref[...] = dx.astype(BF16)
        dg_part = jnp.sum(dy * xh, axis=0, keepdims=True)
        loss_part = jnp.full((1, LANE), 0.5 * jnp.sum(jnp.mean(err * err, axis=-1, keepdims=True)), F32)

        @pl.when(pl.program_id(0) == 0)
        def _():
            dg_ref[...] = dg_part
            loss_ref[...] = loss_part

        @pl.when(pl.program_id(0) > 0)
        def _():
            dg_ref[...] += dg_part
            loss_ref[...] += loss_part

    row = _bs((tm, d), lambda i: (i, 0))
    vec = _bs((1, d), lambda i: (0, 0))
    return pl.pallas_call(
        body, out_shape=(jax.ShapeDtypeStruct((t_dim, d), F32), jax.ShapeDtypeStruct((t_dim, d), BF16),
                         jax.ShapeDtypeStruct((1, d), F32), jax.ShapeDtypeStruct((1, LANE), F32)),
        grid=(t_dim // tm,), in_specs=[row, vec, row],
        out_specs=(row, row, vec, _bs((1, LANE), lambda i: (0, 0))), name=name,
        compiler_params=_params("arbitrary"))(x, g, tgt)


def _glu_conv_fwd(proj, dw_w, dw_b, n_batch, seq, name):
    kk, cc = dw_w.shape
    nj = cc // LANE
    ch = min(256, seq)

    def body(a_ref, gl_ref, w_ref, b_ref, o_ref, pad):
        pad[0:CONV_HALO, :] = jnp.zeros((CONV_HALO, LANE), F32)
        pad[CONV_HALO:CONV_HALO + seq, :] = a_ref[...] * _sig(gl_ref[...])
        for c0 in range(0, seq, ch):
            acc = jnp.broadcast_to(b_ref[...], (ch, LANE))
            for k in range(kk):
                acc = acc + w_ref[k:k + 1, :] * pad[pl.ds(c0 + CONV_HALO - (kk - 1) + k, ch), :]
            o_ref[c0:c0 + ch, :] = acc

    return pl.pallas_call(
        body, out_shape=jax.ShapeDtypeStruct((n_batch * seq, cc), F32), grid=(n_batch, nj),
        in_specs=[_bs((seq, LANE), lambda b, j: (b, j)), _bs((seq, LANE), lambda b, j: (b, nj + j)),
                  _bs((kk, LANE), lambda b, j: (0, j)), _bs((1, LANE), lambda b, j: (0, j))],
        out_specs=_bs((seq, LANE), lambda b, j: (b, j)),
        scratch_shapes=[pltpu.VMEM((seq + CONV_HALO, LANE), F32)], name=name,
        compiler_params=_params("parallel", "parallel"))(proj, proj, dw_w, dw_b)


def _glu_conv_bwd(proj, dw_w, dy1, n_batch, seq, name):
    kk, cc = dw_w.shape
    nj = cc // LANE
    ch = min(256, seq)

    def body(a_ref, gl_ref, dy_ref, w_ref, da_ref, dgl_ref, dw_ref, db_ref, padf, padb):
        first = pl.program_id(1) == 0
        padf[0:CONV_HALO, :] = jnp.zeros((CONV_HALO, LANE), F32)
        padf[CONV_HALO:CONV_HALO + seq, :] = a_ref[...] * _sig(gl_ref[...])
        padb[0:seq, :] = dy_ref[...]
        padb[seq:seq + CONV_HALO, :] = jnp.zeros((CONV_HALO, LANE), F32)

        @pl.when(first)
        def _():
            dw_ref[...] = jnp.zeros((kk, LANE), F32)
            db_ref[...] = jnp.zeros((1, LANE), F32)

        for c0 in range(0, seq, ch):
            acc = jnp.zeros((ch, LANE), F32)
            for k in range(kk):
                acc = acc + w_ref[k:k + 1, :] * padb[pl.ds(c0 + (kk - 1) - k, ch), :]
            sg = _sig(gl_ref[c0:c0 + ch, :])
            da_ref[c0:c0 + ch, :] = (acc * sg).astype(BF16)
            dgl_ref[c0:c0 + ch, :] = (acc * a_ref[c0:c0 + ch, :] * sg * (1.0 - sg)).astype(BF16)
        for k in range(kk):
            s = jnp.zeros((1, LANE), F32)
            for c0 in range(0, seq, ch):
                s = s + jnp.sum(padb[c0:c0 + ch, :] * padf[pl.ds(c0 + CONV_HALO - (kk - 1) + k, ch), :],
                                axis=0, keepdims=True)
            dw_ref[k:k + 1, :] += s
        db_ref[...] += jnp.sum(dy_ref[...], axis=0, keepdims=True)

    tok = _bs((seq, LANE), lambda j, b: (b, j))
    t_dim = n_batch * seq
    return pl.pallas_call(
        body, out_shape=(jax.ShapeDtypeStruct((t_dim, cc), BF16), jax.ShapeDtypeStruct((t_dim, cc), BF16),
                         jax.ShapeDtypeStruct((kk, cc), F32), jax.ShapeDtypeStruct((1, cc), F32)),
        grid=(nj, n_batch),
        in_specs=[tok, _bs((seq, LANE), lambda j, b: (b, nj + j)), tok, _bs((kk, LANE), lambda j, b: (0, j))],
        out_specs=(tok, tok, _bs((kk, LANE), lambda j, b: (0, j)), _bs((1, LANE), lambda j, b: (0, j))),
        scratch_shapes=[pltpu.VMEM((seq + CONV_HALO, LANE), F32), pltpu.VMEM((seq + CONV_HALO, LANE), F32)],
        name=name, compiler_params=_params("parallel", "arbitrary"))(proj, proj, dy1, dw_w)


def _ln_silu_fwd(y1, g, b, name):
    t_dim, c = y1.shape
    tm = _tile(t_dim, 512)

    def body(y_ref, g_ref, b_ref, o_ref):
        yv = y_ref[...]
        xc = yv - jnp.mean(yv, axis=-1, keepdims=True)
        rstd = lax.rsqrt(jnp.mean(xc * xc, axis=-1, keepdims=True) + EPS)
        y2 = xc * rstd * g_ref[...] + b_ref[...]
        o_ref[...] = (y2 * _sig(y2)).astype(BF16)

    row = _bs((tm, c), lambda i: (i, 0))
    vec = _bs((1, c), lambda i: (0, 0))
    return pl.pallas_call(
        body, out_shape=jax.ShapeDtypeStruct((t_dim, c), BF16), grid=(t_dim // tm,),
        in_specs=[row, vec, vec], out_specs=row, name=name, compiler_params=_params("parallel"))(y1, g, b)


def _ln_silu_bwd(y1, g, b, dy3, name):
    t_dim, c = y1.shape
    tm = _tile(t_dim, 512)

    def body(y_ref, g_ref, b_ref, d_ref, dy_ref, dg_ref, db_ref):
        yv = y_ref[...]
        gv = g_ref[...]
        xc = yv - jnp.mean(yv, axis=-1, keepdims=True)
        rstd = lax.rsqrt(jnp.mean(xc * xc, axis=-1, keepdims=True) + EPS)
        yh = xc * rstd
        y2 = yh * gv + b_ref[...]
        s = _sig(y2)
        dy2 = d_ref[...].astype(F32) * (s * (1.0 + y2 * (1.0 - s)))
        dyh = dy2 * gv
        dy_ref[...] = rstd * (dyh - jnp.mean(dyh, axis=-1, keepdims=True)
                              - yh * jnp.mean(dyh * yh, axis=-1, keepdims=True))
        dg_part = jnp.sum(dy2 * yh, axis=0, keepdims=True)
        db_part = jnp.sum(dy2, axis=0, keepdims=True)

        @pl.when(pl.program_id(0) == 0)
        def _():
            dg_ref[...] = dg_part
            db_ref[...] = db_part

        @pl.when(pl.program_id(0) > 0)
        def _():
            dg_ref[...] += dg_part
            db_ref[...] += db_part

    row = _bs((tm, c), lambda i: (i, 0))
    vec = _bs((1, c), lambda i: (0, 0))
    return pl.pallas_call(
        body, out_shape=(jax.ShapeDtypeStruct((t_dim, c), F32), jax.ShapeDtypeStruct((1, c), F32),
                         jax.ShapeDtypeStruct((1, c), F32)),
        grid=(t_dim // tm,), in_specs=[row, vec, vec, row], out_specs=(row, vec, vec), name=name,
        compiler_params=_params("arbitrary"))(y1, g, b, dy3)


def _pool_fwd(proj, col0, n_groups, n_batch, seq, name):
    ch = min(256, seq)

    def body(u_ref, o_ref, pad):
        w = lax.shift_left(jnp.int32(2), pl.program_id(1))
        pad[0:POOL_HALO, :] = jnp.zeros((POOL_HALO, LANE), F32)
        pad[POOL_HALO:POOL_HALO + seq, :] = u_ref[...]
        for c0 in range(0, seq, ch):
            acc = jnp.zeros((ch, LANE), F32)
            for j in range(POOL_WINDOW_MAX):
                acc = acc + jnp.where(j < w, 1.0, 0.0).astype(F32) * pad[pl.ds(c0 + POOL_HALO - j, ch), :]
            t = c0 + lax.broadcasted_iota(jnp.int32, (ch, LANE), 0)
            cnt = jnp.minimum(t + 1, w).astype(F32)
            o_ref[c0:c0 + ch, :] = (acc / cnt - u_ref[c0:c0 + ch, :]).astype(BF16)

    return pl.pallas_call(
        body, out_shape=jax.ShapeDtypeStruct((n_batch * seq, n_groups * LANE), BF16), grid=(n_batch, n_groups),
        in_specs=[_bs((seq, LANE), lambda b, g: (b, col0 + g))], out_specs=_bs((seq, LANE), lambda b, g: (b, g)),
        scratch_shapes=[pltpu.VMEM((seq + POOL_HALO, LANE), F32)], name=name,
        compiler_params=_params("parallel", "parallel"))(proj)


def _pool_bwd(dzp, n_groups, n_batch, seq, name):
    ch = min(256, seq)

    def body(d_ref, o_ref, pad):
        w = lax.shift_left(jnp.int32(2), pl.program_id(1))
        for c0 in range(0, seq, ch):
            t = c0 + lax.broadcasted_iota(jnp.int32, (ch, LANE), 0)
            cnt = jnp.minimum(t + 1, w).astype(F32)
            pad[c0:c0 + ch, :] = d_ref[c0:c0 + ch, :] / cnt
        pad[seq:seq + POOL_HALO, :] = jnp.zeros((POOL_HALO, LANE), F32)
        for c0 in range(0, seq, ch):
            acc = jnp.zeros((ch, LANE), F32)
            for j in range(POOL_WINDOW_MAX):
                acc = acc + jnp.where(j < w, 1.0, 0.0).astype(F32) * pad[pl.ds(c0 + j, ch), :]
            o_ref[c0:c0 + ch, :] = (acc - d_ref[c0:c0 + ch, :]).astype(BF16)

    tok = _bs((seq, LANE), lambda b, g: (b, g))
    return pl.pallas_call(
        body, out_shape=jax.ShapeDtypeStruct((n_batch * seq, n_groups * LANE), BF16), grid=(n_batch, n_groups),
        in_specs=[tok], out_specs=tok, scratch_shapes=[pltpu.VMEM((seq + POOL_HALO, LANE), F32)], name=name,
        compiler_params=_params("parallel", "parallel"))(dzp)


def _merge_fwd(proj, col0, yc, yp, scale, name):
    t_dim, d = yc.shape
    half = d // 2
    tm = _tile(t_dim, 512)
    c0 = col0 // half

    def body(gc_ref, gp_ref, yc_ref, yp_ref, s_ref, o_ref):
        o_ref[...] = (_sig(gc_ref[...]) * yc_ref[...] + _sig(gp_ref[...]) * (yp_ref[...] * s_ref[...])).astype(BF16)

    blk = _bs((tm, half), lambda i, j: (i, j))
    return pl.pallas_call(
        body, out_shape=jax.ShapeDtypeStruct((t_dim, d), BF16), grid=(t_dim // tm, 2),
        in_specs=[_bs((tm, half), lambda i, j: (i, c0 + j)), _bs((tm, half), lambda i, j: (i, c0 + 2 + j)),
                  blk, blk, _bs((1, half), lambda i, j: (0, j))],
        out_specs=blk, name=name, compiler_params=_params("parallel", "parallel"))(proj, proj, yc, yp, scale)


def _merge_bwd(proj, col0, yc, yp, scale, dm, name):
    t_dim, d = yc.shape
    half = d // 2
    tm = _tile(t_dim, 512)
    c0 = col0 // half

    def body(gc_ref, gp_ref, yc_ref, yp_ref, s_ref, dm_ref, dgc_ref, dgp_ref, dyc_ref, dyp_ref, ds_ref):
        dmv = dm_ref[...].astype(F32)
        sgc = _sig(gc_ref[...])
        sgp = _sig(gp_ref[...])
        sv = s_ref[...]
        ypre = yp_ref[...]
        dgc_ref[...] = (dmv * yc_ref[...] * sgc * (1.0 - sgc)).astype(BF16)
        dgp_ref[...] = (dmv * (ypre * sv) * sgp * (1.0 - sgp)).astype(BF16)
        dyc_ref[...] = (dmv * sgc).astype(BF16)
        dyp = dmv * sgp
        dyp_ref[...] = (dyp * sv).astype(BF16)
        part = jnp.sum(dyp * ypre, axis=0, keepdims=True)

        @pl.when(pl.program_id(1) == 0)
        def _():
            ds_ref[...] = part

        @pl.when(pl.program_id(1) > 0)
        def _():
            ds_ref[...] += part

    blk = _bs((tm, half), lambda j, i: (i, j))
    big = jax.ShapeDtypeStruct((t_dim, d), BF16)
    return pl.pallas_call(
        body, out_shape=(big, big, big, big, jax.ShapeDtypeStruct((1, d), F32)), grid=(2, t_dim // tm),
        in_specs=[_bs((tm, half), lambda j, i: (i, c0 + j)), _bs((tm, half), lambda j, i: (i, c0 + 2 + j)),
                  blk, blk, _bs((1, half), lambda j, i: (0, j)), blk],
        out_specs=(blk, blk, blk, blk, _bs((1, half), lambda j, i: (0, j))), name=name,
        compiler_params=_params("parallel", "arbitrary"))(proj, proj, yc, yp, scale, dm)


def _attn_fwd(q, kv, n_batch, seq, m_len, name):
    d = q.shape[1]
    hd = d // XA_HEADS
    tq = _tile(seq, 1024)
    nq = seq // tq
    scale = hd ** -0.5

    def body(q_ref, k_ref, v_ref, o_ref):
        sc = lax.dot_general(q_ref[...].astype(BF16), k_ref[...].astype(BF16), (((1,), (1,)), ((), ())),
                             preferred_element_type=F32) * scale
        p = jnp.exp(sc - jnp.max(sc, axis=-1, keepdims=True))
        pr = p / jnp.sum(p, axis=-1, keepdims=True)
        o_ref[...] = jnp.dot(pr.astype(BF16), v_ref[...].astype(BF16), preferred_element_type=F32).astype(BF16)

    return pl.pallas_call(
        body, out_shape=jax.ShapeDtypeStruct((n_batch * seq, d), BF16), grid=(n_batch, XA_HEADS, nq),
        in_specs=[_bs((tq, hd), lambda b, h, i: (b * nq + i, h)), _bs((m_len, hd), lambda b, h, i: (b, h)),
                  _bs((m_len, hd), lambda b, h, i: (b, XA_HEADS + h))],
        out_specs=_bs((tq, hd), lambda b, h, i: (b * nq + i, h)), name=name,
        compiler_params=_params("parallel", "parallel", "parallel"))(q, kv, kv)


def _attn_bwd(q, kv, datt, n_batch, seq, m_len, name):
    d = q.shape[1]
    hd = d // XA_HEADS
    tq = _tile(seq, 1024)
    nq = seq // tq
    scale = hd ** -0.5

    def body(q_ref, k_ref, v_ref, do_ref, dq_ref, dk_ref, dv_ref):
        qb = q_ref[...].astype(BF16)
        kb = k_ref[...].astype(BF16)
        vb = v_ref[...].astype(BF16)
        dob = do_ref[...].astype(BF16)
        sc = lax.dot_general(qb, kb, (((1,), (1,)), ((), ())), preferred_element_type=F32) * scale
        p = jnp.exp(sc - jnp.max(sc, axis=-1, keepdims=True))
        pr = p / jnp.sum(p, axis=-1, keepdims=True)
        dpr = lax.dot_general(dob, vb, (((1,), (1,)), ((), ())), preferred_element_type=F32)
        dsc = pr * (dpr - jnp.sum(dpr * pr, axis=-1, keepdims=True)) * scale
        dsb = dsc.astype(BF16)
        dq_ref[...] = jnp.dot(dsb, kb, preferred_element_type=F32).astype(BF16)
        dv_part = lax.dot_general(pr.astype(BF16), dob, (((0,), (0,)), ((), ())), preferred_element_type=F32)
        dk_part = lax.dot_general(dsb, qb, (((0,), (0,)), ((), ())), preferred_element_type=F32)

        @pl.when(pl.program_id(2) == 0)
        def _():
            dk_ref[...] = dk_part
            dv_ref[...] = dv_part

        @pl.when(pl.program_id(2) > 0)
        def _():
            dk_ref[...] += dk_part
            dv_ref[...] += dv_part

    qs = _bs((tq, hd), lambda b, h, i: (b * nq + i, h))
    ks = _bs((m_len, hd), lambda b, h, i: (b, h))
    return pl.pallas_call(
        body, out_shape=(jax.ShapeDtypeStruct((n_batch * seq, d), BF16), jax.ShapeDtypeStruct((n_batch * m_len, d), F32),
                         jax.ShapeDtypeStruct((n_batch * m_len, d), F32)),
        grid=(n_batch, XA_HEADS, nq),
        in_specs=[qs, ks, _bs((m_len, hd), lambda b, h, i: (b, XA_HEADS + h)), qs],
        out_specs=(qs, ks, ks), name=name,
        compiler_params=_params("parallel", "parallel", "arbitrary"))(q, kv, kv, datt)


def _gelu_parts(g):
    th = jnp.tanh(GELU_C0 * (g + GELU_C1 * g * g * g))
    return th, 0.5 * g * (1.0 + th)


def _ffn_act_fwd(up0, dw_w, n_batch, seq, name):
    kk, c2 = dw_w.shape
    f_dim = c2 // 2
    wd = 2 * LANE
    nj = f_dim // wd
    ch = min(128, seq)

    def body(g_ref, v_ref, wg_ref, wv_ref, o_ref, padg, padv):
        for pad, src in ((padg, g_ref), (padv, v_ref)):
            pad[0:FFN_HALO, :] = jnp.zeros((FFN_HALO, wd), F32)
            pad[FFN_HALO:FFN_HALO + seq, :] = src[...]
        for c0 in range(0, seq, ch):
            gate = jnp.zeros((ch, wd), F32)
            val = jnp.zeros((ch, wd), F32)
            for k in range(kk):
                off = c0 + FFN_HALO - (kk - 1) + k
                gate = gate + wg_ref[k:k + 1, :] * padg[pl.ds(off, ch), :]
                val = val + wv_ref[k:k + 1, :] * padv[pl.ds(off, ch), :]
            o_ref[c0:c0 + ch, :] = (_gelu_parts(gate)[1] * val).astype(BF16)

    return pl.pallas_call(
        body, out_shape=jax.ShapeDtypeStruct((n_batch * seq, f_dim), BF16), grid=(n_batch, nj),
        in_specs=[_bs((seq, wd), lambda b, j: (b, j)), _bs((seq, wd), lambda b, j: (b, nj + j)),
                  _bs((kk, wd), lambda b, j: (0, j)), _bs((kk, wd), lambda b, j: (0, nj + j))],
        out_specs=_bs((seq, wd), lambda b, j: (b, j)),
        scratch_shapes=[pltpu.VMEM((seq + FFN_HALO, wd), F32), pltpu.VMEM((seq + FFN_HALO, wd), F32)], name=name,
        compiler_params=_params("parallel", "parallel"))(up0, up0, dw_w, dw_w)


def _ffn_act_bwd(up0, dw_w, dact, n_batch, seq, name):
    kk, c2 = dw_w.shape
    f_dim = c2 // 2
    wd = 2 * LANE
    nj = f_dim // wd
    ch = min(128, seq)

    def body(g_ref, v_ref, wg_ref, wv_ref, da_ref, dg_ref, dv_ref, dwg_ref, dwv_ref, padg, padv, pbg, pbv):
        for pad, src in ((padg, g_ref), (padv, v_ref)):
            pad[0:FFN_HALO, :] = jnp.zeros((FFN_HALO, wd), F32)
            pad[FFN_HALO:FFN_HALO + seq, :] = src[...]
        for pb in (pbg, pbv):
            pb[seq:seq + FFN_HALO, :] = jnp.zeros((FFN_HALO, wd), F32)

        @pl.when(pl.program_id(1) == 0)
        def _():
            dwg_ref[...] = jnp.zeros((kk, wd), F32)
            dwv_ref[...] = jnp.zeros((kk, wd), F32)

        for c0 in range(0, seq, ch):
            gate = jnp.zeros((ch, wd), F32)
            val = jnp.zeros((ch, wd), F32)
            for k in range(kk):
                off = c0 + FFN_HALO - (kk - 1) + k
                gate = gate + wg_ref[k:k + 1, :] * padg[pl.ds(off, ch), :]
                val = val + wv_ref[k:k + 1, :] * padv[pl.ds(off, ch), :]
            th, gelu = _gelu_parts(gate)
            dgelu = 0.5 * (1.0 + th) + 0.5 * gate * (1.0 - th * th) * GELU_C0 * (1.0 + 3.0 * GELU_C1 * gate * gate)
            dav = da_ref[c0:c0 + ch, :].astype(F32)
            pbg[c0:c0 + ch, :] = dav * val * dgelu
            pbv[c0:c0 + ch, :] = dav * gelu
        for pb, pad, w_ref, d_ref, dw_ref in ((pbg, padg, wg_ref, dg_ref, dwg_ref), (pbv, padv, wv_ref, dv_ref, dwv_ref)):
            for c0 in range(0, seq, ch):
                acc = jnp.zeros((ch, wd), F32)
                for k in range(kk):
                    acc = acc + w_ref[k:k + 1, :] * pb[pl.ds(c0 + (kk - 1) - k, ch), :]
                d_ref[c0:c0 + ch, :] = acc.astype(BF16)
            for k in range(kk):
                s = jnp.zeros((1, wd), F32)
                for c0 in range(0, seq, ch):
                    s = s + jnp.sum(pb[c0:c0 + ch, :] * pad[pl.ds(c0 + FFN_HALO - (kk - 1) + k, ch), :],
                                    axis=0, keepdims=True)
                dw_ref[k:k + 1, :] += s

    t_dim = n_batch * seq
    tok = _bs((seq, wd), lambda j, b: (b, j))
    wblk = _bs((kk, wd), lambda j, b: (0, j))
    pad_shape = pltpu.VMEM((seq + FFN_HALO, wd), F32)
    return pl.pallas_call(
        body, out_shape=(jax.ShapeDtypeStruct((t_dim, f_dim), BF16), jax.ShapeDtypeStruct((t_dim, f_dim), BF16),
                         jax.ShapeDtypeStruct((kk, f_dim), F32), jax.ShapeDtypeStruct((kk, f_dim), F32)),
        grid=(nj, n_batch),
        in_specs=[tok, _bs((seq, wd), lambda j, b: (b, nj + j)), wblk, _bs((kk, wd), lambda j, b: (0, nj + j)), tok],
        out_specs=(tok, tok, wblk, wblk), scratch_shapes=[pad_shape, pad_shape, pad_shape, pad_shape], name=name,
        compiler_params=_params("parallel", "arbitrary"))(up0, up0, dw_w, dw_w, dact)


def _sum_rows(parts, out_dtype, name):
    r_dim, c_dim = parts[0].shape
    tr = _tile(r_dim, 1200, SUBLANE)
    n = len(parts)

    def body(*refs):
        acc = refs[0][...].astype(F32)
        for r in refs[1:n]:
            acc = acc + r[...].astype(F32)
        refs[n][...] = acc.astype(out_dtype)

    blk = _bs((tr, c_dim), lambda i: (i, 0))
    return pl.pallas_call(
        body, out_shape=jax.ShapeDtypeStruct((r_dim, c_dim), out_dtype), grid=(r_dim // tr,),
        in_specs=[blk] * n, out_specs=blk, name=name, compiler_params=_params("parallel"))(*parts)


def _adamw(w, g, m, v, name):
    shape = w.shape
    c_dim = shape[-1]
    r_dim = w.size // c_dim
    two_d = lambda t: t.reshape(r_dim, c_dim)
    tr = _tile(r_dim, max(SUBLANE, (256 * 1024) // max(c_dim, LANE) // SUBLANE * SUBLANE), SUBLANE)
    c1 = 1.0 - ADAM_B1 ** ADAM_STEP
    c2 = 1.0 - ADAM_B2 ** ADAM_STEP

    def body(w_ref, g_ref, m_ref, v_ref, d_ref, mo_ref, vo_ref):
        gv = g_ref[...]
        mn = ADAM_B1 * m_ref[...] + (1.0 - ADAM_B1) * gv
        vn = ADAM_B2 * v_ref[...] + (1.0 - ADAM_B2) * (gv * gv)
        mo_ref[...] = mn
        vo_ref[...] = vn
        d_ref[...] = -ADAM_LR * ((mn / c1) / (jnp.sqrt(vn / c2) + ADAM_EPS) + ADAM_WD * w_ref[...])

    blk = _bs((tr, c_dim), lambda i: (i, 0))
    out = jax.ShapeDtypeStruct((r_dim, c_dim), F32)
    d, mo, vo = pl.pallas_call(
        body, out_shape=(out, out, out), grid=(r_dim // tr,), in_specs=[blk] * 4, out_specs=(blk, blk, blk),
        name=name, compiler_params=_params("parallel"))(two_d(w), two_d(g), two_d(m), two_d(v))
    return d.reshape(shape), mo.reshape(shape), vo.reshape(shape)


HBM_SPEC = pl.BlockSpec(memory_space=pltpu.HBM)


def _position():
    return lax.axis_index("x"), lax.axis_index("y"), lax.axis_index("c")


def _all_gather(shard, name):
    def body(x_ref, out_ref, send_sems, recv_sems, local_sem):
        x, y, c = _position()
        me, sibling = (x, y, c), (x, y, 1 - c)
        chips = [(1 - x, y), (x, 1 - y), (1 - x, 1 - y)]

        def rows(px, py, pc):
            return out_ref.at[4 * px + 2 * py + pc]

        def copy(k, block, to, src=None):
            return pltpu.make_async_remote_copy(
                src_ref=rows(*block) if src is None else src, dst_ref=rows(*block),
                send_sem=send_sems.at[k], recv_sem=recv_sems.at[k], device_id=to, device_id_type=MESH)

        mine = pltpu.make_async_copy(x_ref, rows(*me), local_sem)
        mine.start()
        first = [copy(0, me, sibling, src=x_ref)]
        first += [copy(1 + j, me, (*chip, c), src=x_ref) for j, chip in enumerate(chips)]
        for cp in first:
            cp.start()
        passed = [copy(4 + j, (*chip, c), sibling) for j, chip in enumerate(chips)]
        for j, chip in enumerate(chips):
            copy(1 + j, (*chip, c), me).wait_recv()
            passed[j].start()
        copy(0, sibling, me).wait_recv()
        for j, chip in enumerate(chips):
            copy(4 + j, (*chip, 1 - c), me).wait_recv()
        for cp in first + passed:
            cp.wait_send()
        mine.wait()

    return pl.pallas_call(
        body, out_shape=jax.ShapeDtypeStruct((N_DEV,) + shard.shape, shard.dtype),
        in_specs=[HBM_SPEC], out_specs=HBM_SPEC,
        scratch_shapes=[pltpu.SemaphoreType.DMA((7,)), pltpu.SemaphoreType.DMA((7,)), pltpu.SemaphoreType.DMA(())],
        name=name)(shard)


CHIP_RELATIONS = ((0, 0), (1, 0), (0, 1), (1, 1))


def _rs_pair_exchange(g, name):
    _, r_dim, c_dim = g.shape
    n = len(CHIP_RELATIONS)

    def body(g_ref, recv_ref, send_sems, recv_sems):
        x, y, c = _position()
        sibling = (x, y, 1 - c)
        copies = []
        for k, (rx, ry) in enumerate(CHIP_RELATIONS):
            px = x + rx - 2 * x * rx
            py = y + ry - 2 * y * ry
            copies.append(pltpu.make_async_remote_copy(
                src_ref=g_ref.at[4 * px + 2 * py + 1 - c], dst_ref=recv_ref.at[k], send_sem=send_sems.at[k],
                recv_sem=recv_sems.at[k], device_id=sibling, device_id_type=MESH))
        for cp in copies:
            cp.start()
        for cp in copies:
            cp.wait()

    return pl.pallas_call(
        body, out_shape=jax.ShapeDtypeStruct((n, r_dim, c_dim), g.dtype), in_specs=[HBM_SPEC], out_specs=HBM_SPEC,
        scratch_shapes=[pltpu.SemaphoreType.DMA((n,)), pltpu.SemaphoreType.DMA((n,))], name=name)(g)


def _rs_pair_sum(g, recv, name):
    _, r_dim, c_dim = g.shape
    n = len(CHIP_RELATIONS)
    tr = _tile(r_dim, 1200, SUBLANE)
    x, y, c = _position()
    own = jnp.stack([4 * (x + rx - 2 * x * rx) + 2 * (y + ry - 2 * y * ry) + c for rx, ry in CHIP_RELATIONS])

    def body(own_ref, g_ref, r_ref, o_ref):
        o_ref[...] = (g_ref[...].astype(F32) + r_ref[...].astype(F32)).astype(o_ref.dtype)

    blk = _bs((None, tr, c_dim), lambda k, i, own_ref: (k, i, 0))
    return pl.pallas_call(
        body, out_shape=jax.ShapeDtypeStruct((n, r_dim, c_dim), g.dtype),
        grid_spec=pltpu.PrefetchScalarGridSpec(
            num_scalar_prefetch=1, grid=(n, r_dim // tr),
            in_specs=[_bs((None, tr, c_dim), lambda k, i, own_ref: (own_ref[k], i, 0)), blk], out_specs=blk),
        name=name, compiler_params=_params("parallel", "parallel"))(own.astype(jnp.int32), g, recv)


SEM_SPEC = pl.BlockSpec(memory_space=pltpu.SEMAPHORE)
DATAFLOW = pltpu.SideEffectType.DATAFLOW_SIDE_EFFECTING
CHIP_FLIPS = CHIP_RELATIONS[1:]
TOKEN = jax.ShapeDtypeStruct((SUBLANE, LANE), F32)


def _flip(v, r):
    return v + r - 2 * v * r


def _chip_copies(src_ref, src_of, dst_ref, dst_of, send_sems, recv_sems):
    x, y, c = _position()
    me = 4 * x + 2 * y + c
    out = []
    for k, (rx, ry) in enumerate(CHIP_FLIPS):
        px, py = _flip(x, rx), _flip(y, ry)
        peer = 4 * px + 2 * py + c
        out.append(pltpu.make_async_remote_copy(
            src_ref=src_ref.at[src_of(k, me, peer)], dst_ref=dst_ref.at[dst_of(k, me, peer)],
            send_sem=send_sems.at[k], recv_sem=recv_sems.at[k], device_id=(px, py, c), device_id_type=MESH))
    return out


def _ag_chips_start(land, name):
    def body(land_ref, send_sems, recv_sems, land_thru, token):
        for cp in _chip_copies(land_ref, lambda k, me, peer: me, land_ref, lambda k, me, peer: me, send_sems, recv_sems):
            cp.start()
        token[...] = jnp.zeros(TOKEN.shape, TOKEN.dtype)

    n = len(CHIP_FLIPS)
    return pl.pallas_call(
        body, name=name,
        out_shape=(pltpu.SemaphoreType.DMA((n,)), pltpu.SemaphoreType.DMA((n,)), pltpu.HBM(land.shape, land.dtype), TOKEN),
        in_specs=(HBM_SPEC,), out_specs=(SEM_SPEC, SEM_SPEC, HBM_SPEC, pl.BlockSpec(memory_space=pltpu.VMEM)),
        input_output_aliases={0: 2}, compiler_params=pltpu.CompilerParams(has_side_effects=DATAFLOW),
    )(pltpu.with_memory_space_constraint(land, pltpu.HBM))


def _ag_chips_wait(send_sems, recv_sems, land, after, name):
    def body(land_ref, send_sems, recv_sems, after_ref, land_out):
        for cp in _chip_copies(land_ref, lambda k, me, peer: me, land_ref, lambda k, me, peer: peer, send_sems, recv_sems):
            cp.wait_send()
            cp.wait_recv()

    return pl.pallas_call(
        body, name=name, out_shape=pltpu.HBM(land.shape, land.dtype),
        in_specs=(HBM_SPEC, SEM_SPEC, SEM_SPEC, pl.BlockSpec(memory_space=pl.ANY)), out_specs=HBM_SPEC,
        input_output_aliases={0: 0}, compiler_params=pltpu.CompilerParams(has_side_effects=DATAFLOW),
    )(land, send_sems, recv_sems, after)


def _ag_pair_forward(land, name):
    n = len(CHIP_RELATIONS)

    def body(land_in, land_ref, send_sems, recv_sems):
        x, y, c = _position()
        copies = []
        for k, (rx, ry) in enumerate(CHIP_RELATIONS):
            chip = 4 * _flip(x, rx) + 2 * _flip(y, ry)
            mine = pltpu.make_async_remote_copy(
                src_ref=land_ref.at[chip + c], dst_ref=land_ref.at[chip + c], send_sem=send_sems.at[k],
                recv_sem=recv_sems.at[k], device_id=(x, y, 1 - c), device_id_type=MESH)
            theirs = pltpu.make_async_remote_copy(
                src_ref=land_ref.at[chip + c], dst_ref=land_ref.at[chip + 1 - c], send_sem=send_sems.at[k],
                recv_sem=recv_sems.at[k], device_id=(x, y, 1 - c), device_id_type=MESH)
            copies.append((mine, theirs))
        for mine, _ in copies:
            mine.start()
        for mine, theirs in copies:
            mine.wait_send()
            theirs.wait_recv()

    return pl.pallas_call(
        body, out_shape=jax.ShapeDtypeStruct(land.shape, land.dtype), in_specs=[HBM_SPEC], out_specs=HBM_SPEC,
        input_output_aliases={0: 0},
        scratch_shapes=[pltpu.SemaphoreType.DMA((n,)), pltpu.SemaphoreType.DMA((n,))], name=name)(land)


def _rs_chips_start(pair, name):
    _, r_dim, c_dim = pair.shape
    n = len(CHIP_FLIPS)

    def body(pair_ref, far_ref, send_sems, recv_sems, pair_thru, far_thru, token):
        for cp in _chip_copies(pair_ref, lambda k, me, peer: k + 1, far_ref, lambda k, me, peer: k, send_sems, recv_sems):
            cp.start()
        token[...] = jnp.zeros(TOKEN.shape, TOKEN.dtype)

    far = lax.empty((n, r_dim, c_dim), pair.dtype)
    return pl.pallas_call(
        body, name=name,
        out_shape=(pltpu.SemaphoreType.DMA((n,)), pltpu.SemaphoreType.DMA((n,)), pltpu.HBM(pair.shape, pair.dtype),
                   pltpu.HBM(far.shape, far.dtype), TOKEN),
        in_specs=(HBM_SPEC, HBM_SPEC),
        out_specs=(SEM_SPEC, SEM_SPEC, HBM_SPEC, HBM_SPEC, pl.BlockSpec(memory_space=pltpu.VMEM)),
        input_output_aliases={0: 2, 1: 3}, compiler_params=pltpu.CompilerParams(has_side_effects=DATAFLOW),
    )(pltpu.with_memory_space_constraint(pair, pltpu.HBM), pltpu.with_memory_space_constraint(far, pltpu.HBM))


def _rs_chips_wait(send_sems, recv_sems, pair, far, after, name):
    def body(pair_ref, far_ref, send_sems, recv_sems, after_ref, pair_out, far_out):
        for cp in _chip_copies(pair_ref, lambda k, me, peer: k + 1, far_ref, lambda k, me, peer: k, send_sems, recv_sems):
            cp.wait_send()
            cp.wait_recv()

    return pl.pallas_call(
        body, name=name, out_shape=(pltpu.HBM(pair.shape, pair.dtype), pltpu.HBM(far.shape, far.dtype)),
        in_specs=(HBM_SPEC, HBM_SPEC, SEM_SPEC, SEM_SPEC, pl.BlockSpec(memory_space=pl.ANY)),
        out_specs=(HBM_SPEC, HBM_SPEC), input_output_aliases={0: 0, 1: 1},
        compiler_params=pltpu.CompilerParams(has_side_effects=DATAFLOW),
    )(pair, far, send_sems, recv_sems, after)


def _rs_final_sum(pair, far, name):
    _, r_dim, c_dim = pair.shape
    tr = _tile(r_dim, 1200, SUBLANE)

    def body(p_ref, f0_ref, f1_ref, f2_ref, o_ref):
        o_ref[...] = ((p_ref[...].astype(F32) + f0_ref[...].astype(F32)) + f1_ref[...].astype(F32)) + f2_ref[...].astype(F32)

    def slot(k):
        return _bs((None, tr, c_dim), lambda i: (k, i, 0))

    return pl.pallas_call(
        body, out_shape=jax.ShapeDtypeStruct((r_dim, c_dim), F32), grid=(r_dim // tr,),
        in_specs=[slot(0), slot(0), slot(1), slot(2)], out_specs=_bs((tr, c_dim), lambda i: (i, 0)), name=name,
        compiler_params=_params("parallel"))(pair, far, far, far)


def _reduce_scatter_begin(g, name):
    recv = _rs_pair_exchange(g, name + "_pair")
    pair = _rs_pair_sum(g, recv, name + "_pairsum")
    return _rs_chips_start(pair, name + "_chips_start")


def _reduce_scatter_end(state, after, name):
    send_sems, recv_sems, pair, far, _ = state
    pair, far = _rs_chips_wait(send_sems, recv_sems, pair, far, after, name + "_chips_wait")
    return _rs_final_sum(pair, far, name + "_sum")


MATRICES = (("w_in", True), ("w_out", False), ("w_q", False), ("w_kv", True), ("w_o", False), ("w_up", True),
            ("w_down", False), ("w_conv_out", True), ("w_pool_grp", True))


def _to_rows(name, transposed, w, d_model):
    if name == "w_pool_grp":
        w = jnp.swapaxes(w, 1, 2)
    elif transposed:
        w = w.T
    return w.reshape(-1, d_model)


def _from_rows(name, transposed, rows, shard_shape):
    if name == "w_pool_grp":
        g, i, o = shard_shape
        return jnp.swapaxes(rows.reshape(g, o, i), 1, 2)
    if transposed:
        return rows.reshape(shard_shape[1], shard_shape[0]).T
    return rows.reshape(shard_shape)


def _gathered_matrix(name, blocks, shard_shape):
    if name == "w_pool_grp":
        g, i, o = shard_shape
        return jnp.swapaxes(blocks.reshape(N_DEV, g, o, i), 0, 1).reshape(g, N_DEV * o, i)
    if name == "w_conv_out":
        return blocks.reshape(N_DEV * shard_shape[1], shard_shape[0])
    return blocks.reshape(-1, blocks.shape[-1])


def _scatter_blocks(name, full, shard_shape, d_model):
    if name == "w_pool_grp":
        g, i, o = shard_shape
        return jnp.swapaxes(full.reshape(g, N_DEV, o, i), 0, 1).reshape(N_DEV, -1, d_model)
    return full.reshape(N_DEV, -1, d_model)


def kernel(x, mem, mix_norm_g, w_in, conv_dw_w, conv_dw_b, conv_ln_g, conv_ln_b, w_conv_out, w_pool_grp, pool_scale, w_out, xattn_norm_g, mem_norm_g, w_q, w_kv, w_o, ffn_norm_g, w_up, ffn_dw_w, w_down, final_norm_g, loss_target, m_mix_norm_g, m_w_in, m_conv_dw_w, m_conv_dw_b, m_conv_ln_g, m_conv_ln_b, m_w_conv_out, m_w_pool_grp, m_pool_scale, m_w_out, m_xattn_norm_g, m_mem_norm_g, m_w_q, m_w_kv, m_w_o, m_ffn_norm_g, m_w_up, m_ffn_dw_w, m_w_down, m_final_norm_g, v_mix_norm_g, v_w_in, v_conv_dw_w, v_conv_dw_b, v_conv_ln_g, v_conv_ln_b, v_w_conv_out, v_w_pool_grp, v_pool_scale, v_w_out, v_xattn_norm_g, v_mem_norm_g, v_w_q, v_w_kv, v_w_o, v_ffn_norm_g, v_w_up, v_ffn_dw_w, v_w_down, v_final_norm_g):
    p = dict(locals())
    weight_names = ["mix_norm_g", "w_in", "conv_dw_w", "conv_dw_b", "conv_ln_g", "conv_ln_b", "w_conv_out",
                    "w_pool_grp", "pool_scale", "w_out", "xattn_norm_g", "mem_norm_g", "w_q", "w_kv", "w_o",
                    "ffn_norm_g", "w_up", "ffn_dw_w", "w_down", "final_norm_g"]
    n_batch, seq, d_model = x.shape
    m_len = mem.shape[1]
    depth = w_in.shape[0]
    assert depth == 2, "the exchange schedule below is written for two layers"
    t_dim = n_batch * seq
    c_conv = conv_dw_b.shape[1]
    n_groups = w_pool_grp.shape[1]
    assert w_pool_grp.shape[2] == LANE and c_conv % LANE == 0 and n_groups * LANE == c_conv
    gate_col0 = 2 * c_conv + n_groups * LANE
    pool_col0 = (2 * c_conv) // LANE

    dev = 4 * lax.axis_index("x") + 2 * lax.axis_index("y") + lax.axis_index("c")
    filt = jnp.concatenate([conv_dw_w.reshape(-1), ffn_dw_w.reshape(-1)])
    filt_rows = lax.bitcast_convert_type(filt, BF16).reshape(-1, d_model)
    layout, packs = [], []
    for l in range(depth):
        pieces = []
        for name, tr in MATRICES:
            rows = _to_rows(name, tr, p[name][l], d_model)
            if l == 0:
                layout.append((name, tr, rows.shape[0]))
            pieces.append(rows.astype(BF16))
        if l == 0:
            pieces.append(filt_rows)
        packs.append(jnp.concatenate(pieces, axis=0))

    def landing(pack):
        return lax.dynamic_update_index_in_dim(lax.empty((N_DEV,) + pack.shape, pack.dtype), pack, dev, 0)

    def matrices(land):
        out, row0 = {}, 0
        for name, tr, nrows in layout:
            out[name] = _gathered_matrix(name, land[:, row0:row0 + nrows], p[name].shape[1:])
            row0 += nrows
        return out, row0

    send0, recv0, land0, token0 = _ag_chips_start(landing(packs[0]), "ag0_chips_start")
    land0 = _ag_chips_wait(send0, recv0, land0, token0, "ag0_chips_wait")
    land0, pack1 = lax.optimization_barrier((land0, packs[1]))
    send1, recv1, land1, token1 = _ag_chips_start(landing(pack1), "ag1_chips_start")
    land0, token1 = lax.optimization_barrier((land0, token1))
    land0 = _ag_pair_forward(land0, "ag0_pair_forward")
    full = [None] * depth
    full[0], row0 = matrices(land0)
    filt_all = lax.bitcast_convert_type(land0[:, row0:row0 + filt_rows.shape[0]].reshape(N_DEV, -1, 2), F32)
    n_cw = conv_dw_w.size
    kc, cs = conv_dw_w.shape[1:]
    kf, fs = ffn_dw_w.shape[1:]
    conv_w_full = jnp.moveaxis(filt_all[:, :n_cw].reshape(N_DEV, depth, kc, cs), 0, 2).reshape(depth, kc, N_DEV * cs)
    ffn_w_full = jnp.moveaxis(filt_all[:, n_cw:].reshape(N_DEV, depth, kf, fs), 0, 2).reshape(depth, kf, N_DEV * fs)

    vec = lambda a: a.reshape(1, -1)
    x2d = x.reshape(t_dim, d_model)
    mem2d = mem.reshape(n_batch * m_len, d_model)
    mem_n = _rmsnorm_fwd(mem2d, vec(mem_norm_g), "mem_norm")

    saved = []
    xc = x2d
    for l in range(depth):
        if l == 1:
            land1 = _ag_chips_wait(send1, recv1, land1, xc, "ag1_chips_wait")
            land1 = _ag_pair_forward(land1, "ag1_pair_forward")
            full[1], _ = matrices(land1)
        wl = full[l]
        s = {"x0": xc}
        s["h"] = _rmsnorm_fwd(xc, vec(mix_norm_g[l]), f"mix_norm_l{l}")
        s["proj"] = _matmul(s["h"], wl["w_in"], "nt", f"in_proj_l{l}")
        s["y1"] = _glu_conv_fwd(s["proj"], conv_w_full[l], vec(conv_dw_b[l]), n_batch, seq, f"glu_conv_l{l}")
        s["y3"] = _ln_silu_fwd(s["y1"], vec(conv_ln_g[l]), vec(conv_ln_b[l]), f"ln_silu_l{l}")
        s["yc"] = _matmul(s["y3"], wl["w_conv_out"], "nt", f"conv_out_l{l}")
        s["zp"] = _pool_fwd(s["proj"], pool_col0, n_groups, n_batch, seq, f"pool_l{l}")
        s["yp"] = _grouped(s["zp"], wl["w_pool_grp"], "nt", f"pool_proj_l{l}")
        s["merged"] = _merge_fwd(s["proj"], gate_col0, s["yc"], s["yp"], vec(pool_scale[l]), f"merge_l{l}")
        s["x1"] = _matmul(s["merged"], wl["w_out"], "nn", f"mix_out_l{l}", res=xc)
        s["hq"] = _rmsnorm_fwd(s["x1"], vec(xattn_norm_g[l]), f"xattn_norm_l{l}")
        s["q"] = _matmul(s["hq"], wl["w_q"], "nn", f"q_proj_l{l}")
        s["kv"] = _matmul(mem_n, wl["w_kv"], "nt", f"kv_proj_l{l}")
        s["att"] = _attn_fwd(s["q"], s["kv"], n_batch, seq, m_len, f"attn_l{l}")
        s["x2"] = _matmul(s["att"], wl["w_o"], "nn", f"attn_out_l{l}", res=s["x1"])
        s["hf"] = _rmsnorm_fwd(s["x2"], vec(ffn_norm_g[l]), f"ffn_norm_l{l}")
        s["up0"] = _matmul(s["hf"], wl["w_up"], "nt", f"up_proj_l{l}")
        s["act"] = _ffn_act_fwd(s["up0"], ffn_w_full[l], n_batch, seq, f"ffn_act_l{l}")
        xc = _matmul(s["act"], wl["w_down"], "nn", f"down_proj_l{l}", res=s["x2"])
        saved.append(s)

    dx, dxb, dg_final, loss_part = _loss_head(xc, vec(final_norm_g), loss_target.reshape(t_dim, d_model), "loss_head")

    small = {"final_norm_g": dg_final.reshape(-1)}
    big = [dict() for _ in range(depth)]
    rs_state = [None] * depth
    dmem_n = None
    for l in reversed(range(depth)):
        wl, s = full[l], saved[l]
        sm = {}
        dact = _matmul(dxb, wl["w_down"], "nt", f"d_act_l{l}")
        big[l]["w_down"] = _matmul(s["act"], dxb, "tn", f"d_w_down_l{l}", out_dtype=BF16)
        dup_g, dup_v, dwf_g, dwf_v = _ffn_act_bwd(s["up0"], ffn_w_full[l], dact, n_batch, seq, f"ffn_act_bwd_l{l}")
        sm["ffn_dw_w"] = jnp.concatenate([dwf_g, dwf_v], axis=1)
        dup0 = jnp.concatenate([dup_g, dup_v], axis=1)
        dhf = _matmul(dup0, wl["w_up"], "nn", f"d_hf_l{l}")
        big[l]["w_up"] = _matmul(dup0, s["hf"], "tn", f"d_w_up_l{l}", out_dtype=BF16)
        dx, dxb, dg = _rmsnorm_bwd(s["x2"], vec(ffn_norm_g[l]), dhf, dx, f"ffn_norm_bwd_l{l}")
        sm["ffn_norm_g"] = dg
        datt = _matmul(dxb, wl["w_o"], "nt", f"d_att_l{l}")
        big[l]["w_o"] = _matmul(s["att"], dxb, "tn", f"d_w_o_l{l}", out_dtype=BF16)
        dq, dk, dv = _attn_bwd(s["q"], s["kv"], datt, n_batch, seq, m_len, f"attn_bwd_l{l}")
        dkv = jnp.concatenate([dk, dv], axis=1)
        big[l]["w_kv"] = _matmul(dkv, mem_n, "tn", f"d_w_kv_l{l}", out_dtype=BF16)
        dmem_n = _matmul(dkv, wl["w_kv"], "nn", f"d_mem_l{l}", res=dmem_n)
        dhq = _matmul(dq, wl["w_q"], "nt", f"d_hq_l{l}")
        big[l]["w_q"] = _matmul(s["hq"], dq, "tn", f"d_w_q_l{l}", out_dtype=BF16)
        dx, dxb, dg = _rmsnorm_bwd(s["x1"], vec(xattn_norm_g[l]), dhq, dx, f"xattn_norm_bwd_l{l}")
        sm["xattn_norm_g"] = dg
        dmerged = _matmul(dxb, wl["w_out"], "nt", f"d_merged_l{l}")
        big[l]["w_out"] = _matmul(s["merged"], dxb, "tn", f"d_w_out_l{l}", out_dtype=BF16)
        dgc, dgp, dyc, dyp, dscale = _merge_bwd(s["proj"], gate_col0, s["yc"], s["yp"], vec(pool_scale[l]), dmerged,
                                                f"merge_bwd_l{l}")
        sm["pool_scale"] = dscale
        dzp = _grouped(dyp, wl["w_pool_grp"], "nn", f"d_zp_l{l}")
        big[l]["w_pool_grp"] = _grouped_tn(dyp, s["zp"], n_groups, f"d_w_pool_l{l}")
        du = _pool_bwd(dzp, n_groups, n_batch, seq, f"pool_bwd_l{l}")
        dy3 = _matmul(dyc, wl["w_conv_out"], "nn", f"d_y3_l{l}")
        big[l]["w_conv_out"] = _matmul(dyc, s["y3"], "tn", f"d_w_conv_out_l{l}", out_dtype=BF16)
        dy1, dlg, dlb = _ln_silu_bwd(s["y1"], vec(conv_ln_g[l]), vec(conv_ln_b[l]), dy3, f"ln_silu_bwd_l{l}")
        sm["conv_ln_g"], sm["conv_ln_b"] = dlg, dlb
        da, dgl, dcw, dcb = _glu_conv_bwd(s["proj"], conv_w_full[l], dy1, n_batch, seq, f"glu_conv_bwd_l{l}")
        sm["conv_dw_w"], sm["conv_dw_b"] = dcw, dcb
        dproj = jnp.concatenate([da, dgl, du, dgc, dgp], axis=1)
        dh = _matmul(dproj, wl["w_in"], "nn", f"d_h_l{l}")
        big[l]["w_in"] = _matmul(dproj, s["h"], "tn", f"d_w_in_l{l}", out_dtype=BF16)
        dx, dxb, dg = _rmsnorm_bwd(s["x0"], vec(mix_norm_g[l]), dh, dx, f"mix_norm_bwd_l{l}")
        sm["mix_norm_g"] = dg
        for k, val in sm.items():
            small[(l, k)] = val.reshape(-1)
        blocks = [_scatter_blocks(name, big[l][name], p[name].shape[1:], d_model).astype(BF16) for name, _, _ in layout]
        rs_state[l] = _reduce_scatter_begin(jnp.concatenate(blocks, axis=1), f"rs{l}")
        dx, dxb, _ = lax.optimization_barrier((dx, dxb, rs_state[l][4]))
    _, _, dg_mem = _rmsnorm_bwd(mem2d, vec(mem_norm_g), dmem_n, None, "mem_norm_bwd")
    small["mem_norm_g"] = dg_mem.reshape(-1)
    small["loss"] = loss_part.reshape(-1)

    grads = {}
    per_layer = {name: [None] * depth for name, _ in MATRICES}
    for l in reversed(range(depth)):
        mat_grads = _reduce_scatter_end(rs_state[l], dx, f"rs{l}")
        row0 = 0
        for name, tr, nrows in layout:
            per_layer[name][l] = _from_rows(name, tr, mat_grads[row0:row0 + nrows], p[name].shape[1:])
            row0 += nrows
    for name, _ in MATRICES:
        grads[name] = jnp.stack(per_layer[name])

    keys = list(small.keys())
    flat = jnp.concatenate([small[k] for k in keys])
    n_small = flat.shape[0]
    rows_small = -(-n_small // (SUBLANE * d_model)) * SUBLANE
    flat = jnp.pad(flat, (0, rows_small * d_model - n_small)).reshape(rows_small, d_model)
    every = _all_gather(flat, "small_all_gather")
    total = _sum_rows([every[i] for i in range(N_DEV)], F32, "small_sum").reshape(-1)
    off = 0
    red = {}
    for k in keys:
        red[k] = total[off:off + small[k].shape[0]]
        off += small[k].shape[0]
    loss = red["loss"][0]
    for name in ("mix_norm_g", "conv_dw_b", "conv_ln_g", "conv_ln_b", "pool_scale", "xattn_norm_g", "ffn_norm_g"):
        grads[name] = jnp.stack([red[(l, name)] for l in range(depth)])
    grads["conv_dw_w"] = jnp.stack([
        lax.dynamic_slice_in_dim(red[(l, "conv_dw_w")].reshape(kc, N_DEV * cs), dev * cs, cs, axis=1)
        for l in range(depth)])
    grads["ffn_dw_w"] = jnp.stack([
        lax.dynamic_slice_in_dim(red[(l, "ffn_dw_w")].reshape(kf, N_DEV * fs), dev * fs, fs, axis=1)
        for l in range(depth)])
    grads["mem_norm_g"] = red["mem_norm_g"]
    grads["final_norm_g"] = red["final_norm_g"]

    deltas, new_m, new_v = {}, {}, {}
    for name in weight_names:
        deltas[name], new_m[name], new_v[name] = _adamw(p[name], grads[name], p["m_" + name], p["v_" + name],
                                                        f"adamw_{name}")
    grad_x = dx.reshape(n_batch, seq, d_model)
    return (loss, grad_x, *[grads[n] for n in weight_names], *[deltas[n] for n in weight_names],
            *[new_m[n] for n in weight_names], *[new_v[n] for n in weight_names])
```

```python
import functools

import jax
import jax.numpy as jnp
from jax import lax
from jax.experimental import pallas as pl
from jax.experimental.pallas import tpu as pltpu

F32 = jnp.float32
BF16 = jnp.bfloat16
MESH = pl.DeviceIdType.MESH

N_DEV = 8
EPS = 1e-6
V7X_VMEM_BYTES = 64 * 1024 * 1024
VMEM_LIMIT = (V7X_VMEM_BYTES * 3) // 4
LANE = 128
SUBLANE = 8

CONV_HALO = 32
POOL_HALO = 16
FFN_HALO = 8
POOL_WINDOW_MAX = 16
XA_HEADS = 4

ADAM_LR = 0.001
ADAM_B1 = 0.9
ADAM_B2 = 0.999
ADAM_EPS = 1e-08
ADAM_WD = 0.01
ADAM_STEP = 10

GELU_C0 = 0.7978845608028654
GELU_C1 = 0.044715


ANY_SPEC = pl.BlockSpec(memory_space=pl.ANY)


def _tile(n, cap, mult=LANE):
    if n <= cap:
        return n
    best = None
    for d in range(mult, cap + 1, mult):
        if n % d == 0:
            best = d
    assert best is not None, (n, cap, mult)
    return best


def _params(*sem):
    return pltpu.CompilerParams(dimension_semantics=sem, vmem_limit_bytes=VMEM_LIMIT)


def _sig(x):
    return 1.0 / (1.0 + jnp.exp(-x))


def _bs(shape, imap):
    return pl.BlockSpec(shape, imap)


def _matmul(a, b, mode, name, res=None, out_dtype=F32, after=None):
    if mode == "tn":
        k_dim, m_dim = a.shape
        k2, n_dim = b.shape
    elif mode == "nn":
        m_dim, k_dim = a.shape
        k2, n_dim = b.shape
    else:
        m_dim, k_dim = a.shape
        n_dim, k2 = b.shape
    assert k_dim == k2, (name, a.shape, b.shape)
    tm = _tile(m_dim, 1408 if mode == "tn" else 1024)
    tn = _tile(n_dim, 1408)
    wide = a.dtype == F32 or b.dtype == F32
    tk = _tile(k_dim, 1024 if wide else 2048)
    nk = k_dim // tk
    use_acc = nk > 1 and out_dtype != F32
    if mode == "tn":
        a_spec, ca = _bs((tk, tm), lambda i, j, k: (k, i)), 0
    else:
        a_spec, ca = _bs((tm, tk), lambda i, j, k: (i, k)), 1
    if mode == "nt":
        b_spec, cb = _bs((tn, tk), lambda i, j, k: (j, k)), 1
    else:
        b_spec, cb = _bs((tk, tn), lambda i, j, k: (k, j)), 0
    dims = (((ca,), (cb,)), ((), ()))
    o_spec = _bs((tm, tn), lambda i, j, k: (i, j))
    has_res = res is not None

    def body(*refs):
        a_ref, b_ref = refs[:2]
        r_ref = refs[2] if has_res else None
        o_ref = refs[n_in]
        k = pl.program_id(2)
        part = lax.dot_general(a_ref[...].astype(BF16), b_ref[...].astype(BF16), dims,
                               preferred_element_type=F32)
        if nk == 1:
            if has_res:
                part = part + r_ref[...].astype(F32)
            o_ref[...] = part.astype(out_dtype)
            return
        acc = refs[-1] if use_acc else o_ref

        @pl.when(k == 0)
        def _():
            acc[...] = part + r_ref[...].astype(F32) if has_res else part

        @pl.when(k > 0)
        def _():
            acc[...] += part

        if use_acc:
            @pl.when(k == nk - 1)
            def _():
                o_ref[...] = acc[...].astype(out_dtype)

    in_specs = [a_spec, b_spec] + ([o_spec] if has_res else [])
    args = (a, b) + ((res,) if has_res else ())
    if after is not None:
        in_specs.append(ANY_SPEC)
        args += (after,)
    n_in = len(args)
    return pl.pallas_call(
        body, out_shape=jax.ShapeDtypeStruct((m_dim, n_dim), out_dtype),
        grid=(m_dim // tm, n_dim // tn, nk), in_specs=in_specs, out_specs=o_spec,
        scratch_shapes=[pltpu.VMEM((tm, tn), F32)] if use_acc else [], name=name,
        compiler_params=_params("parallel", "parallel", "arbitrary"))(*args)


def _grouped(a, w, mode, name, out_dtype=F32):
    t_dim = a.shape[0]
    g_dim, r_dim, c_dim = w.shape
    ka, no = (c_dim, r_dim) if mode == "nt" else (r_dim, c_dim)
    tm = _tile(t_dim, 512)
    dims = (((1,), (1 if mode == "nt" else 0,)), ((), ()))

    def body(a_ref, w_ref, o_ref):
        o_ref[...] = lax.dot_general(a_ref[...].astype(BF16), w_ref[...].astype(BF16), dims,
                                     preferred_element_type=F32).astype(out_dtype)

    return pl.pallas_call(
        body, out_shape=jax.ShapeDtypeStruct((t_dim, g_dim * no), out_dtype),
        grid=(t_dim // tm, g_dim),
        in_specs=[_bs((tm, ka), lambda i, g: (i, g)), _bs((None, r_dim, c_dim), lambda i, g: (g, 0, 0))],
        out_specs=_bs((tm, no), lambda i, g: (i, g)), name=name,
        compiler_params=_params("parallel", "parallel"))(a, w)


def _grouped_tn(a, b, g_dim, name):
    t_dim = a.shape[0]
    ra = a.shape[1] // g_dim
    cb = b.shape[1] // g_dim
    tm = _tile(t_dim, 512)
    nt = t_dim // tm

    def body(a_ref, b_ref, o_ref):
        part = lax.dot_general(a_ref[...].astype(BF16), b_ref[...].astype(BF16), (((0,), (0,)), ((), ())),
                               preferred_element_type=F32)

        @pl.when(pl.program_id(1) == 0)
        def _():
            o_ref[...] = part

        @pl.when(pl.program_id(1) > 0)
        def _():
            o_ref[...] += part

    return pl.pallas_call(
        body, out_shape=jax.ShapeDtypeStruct((g_dim, ra, cb), F32), grid=(g_dim, nt),
        in_specs=[_bs((tm, ra), lambda g, i: (i, g)), _bs((tm, cb), lambda g, i: (i, g))],
        out_specs=_bs((None, ra, cb), lambda g, i: (g, 0, 0)), name=name,
        compiler_params=_params("parallel", "arbitrary"))(a, b)


def _rmsnorm_fwd(x, g, name):
    t_dim, d = x.shape
    tm = _tile(t_dim, 512)

    def body(x_ref, g_ref, o_ref):
        xv = x_ref[...]
        r = lax.rsqrt(jnp.mean(xv * xv, axis=-1, keepdims=True) + EPS)
        o_ref[...] = (xv * r * g_ref[...]).astype(BF16)

    return pl.pallas_call(
        body, out_shape=jax.ShapeDtypeStruct((t_dim, d), BF16), grid=(t_dim // tm,),
        in_specs=[_bs((tm, d), lambda i: (i, 0)), _bs((1, d), lambda i: (0, 0))],
        out_specs=_bs((tm, d), lambda i: (i, 0)), name=name, compiler_params=_params("parallel"))(x, g)


def _rmsnorm_bwd(x, g, dh, dx_in, name):
    t_dim, d = x.shape
    tm = _tile(t_dim, 512)
    has_in = dx_in is not None

    def body(*refs):
        if has_in:
            x_ref, g_ref, dh_ref, di_ref, dx_ref, dxb_ref, dg_ref = refs
        else:
            x_ref, g_ref, dh_ref, dx_ref, dxb_ref, dg_ref = refs
        xv = x_ref[...]
        r = lax.rsqrt(jnp.mean(xv * xv, axis=-1, keepdims=True) + EPS)
        xh = xv * r
        dhv = dh_ref[...].astype(F32)
        dxh = dhv * g_ref[...]
        dx = r * (dxh - xh * jnp.mean(dxh * xh, axis=-1, keepdims=True))
        if has_in:
            dx = dx + di_ref[...]
        dx_ref[...] = dx
        dxb_ref[...] = dx.astype(BF16)
        part = jnp.sum(dhv * xh, axis=0, keepdims=True)

        @pl.when(pl.program_id(0) == 0)
        def _():
            dg_ref[...] = part

        @pl.when(pl.program_id(0) > 0)
        def _():
            dg_ref[...] += part

    row = _bs((tm, d), lambda i: (i, 0))
    vec = _bs((1, d), lambda i: (0, 0))
    args = (x, g, dh) + ((dx_in,) if has_in else ())
    return pl.pallas_call(
        body, out_shape=(jax.ShapeDtypeStruct((t_dim, d), F32), jax.ShapeDtypeStruct((t_dim, d), BF16),
                         jax.ShapeDtypeStruct((1, d), F32)),
        grid=(t_dim // tm,), in_specs=[row, vec, row] + ([row] if has_in else []),
        out_specs=(row, row, vec), name=name, compiler_params=_params("arbitrary"))(*args)


def _loss_head(x, g, tgt, name):
    t_dim, d = x.shape
    tm = _tile(t_dim, 512)

    def body(x_ref, g_ref, t_ref, dx_ref, dxb_ref, dg_ref, loss_ref):
        xv = x_ref[...]
        gv = g_ref[...]
        r = lax.rsqrt(jnp.mean(xv * xv, axis=-1, keepdims=True) + EPS)
        xh = xv * r
        err = xh * gv - t_ref[...]
        dy = err * (1.0 / d)
        dxh = dy * gv
        dx = r * (dxh - xh * jnp.mean(dxh * xh, axis=-1, keepdims=True))
        dx_ref[...] = dx
        dxb_ref[...] = dx.astype(BF16)
        dg_part = jnp.sum(dy * xh, axis=0, keepdims=True)
        loss_part = jnp.full((1, LANE), 0.5 * jnp.sum(jnp.mean(err * err, axis=-1, keepdims=True)), F32)

        @pl.when(pl.program_id(0) == 0)
        def _():
            dg_ref[...] = dg_part
            loss_ref[...] = loss_part

        @pl.when(pl.program_id(0) > 0)
        def _():
            dg_ref[...] += dg_part
            loss_ref[...] += loss_part

    row = _bs((tm, d), lambda i: (i, 0))
    vec = _bs((1, d), lambda i: (0, 0))
    return pl.pallas_call(
        body, out_shape=(jax.ShapeDtypeStruct((t_dim, d), F32), jax.ShapeDtypeStruct((t_dim, d), BF16),
                         jax.ShapeDtypeStruct((1, d), F32), jax.ShapeDtypeStruct((1, LANE), F32)),
        grid=(t_dim // tm,), in_specs=[row, vec, row],
        out_specs=(row, row, vec, _bs((1, LANE), lambda i: (0, 0))), name=name,
        compiler_params=_params("arbitrary"))(x, g, tgt)


def _glu_conv_fwd(proj, dw_w, dw_b, n_batch, seq, name):
    kk, cc = dw_w.shape
    nj = cc // LANE
    ch = min(256, seq)

    def body(a_ref, gl_ref, w_ref, b_ref, o_ref, pad):
        pad[0:CONV_HALO, :] = jnp.zeros((CONV_HALO, LANE), F32)
        pad[CONV_HALO:CONV_HALO + seq, :] = a_ref[...] * _sig(gl_ref[...])
        for c0 in range(0, seq, ch):
            acc = jnp.broadcast_to(b_ref[...], (ch, LANE))
            for k in range(kk):
                acc = acc + w_ref[k:k + 1, :] * pad[pl.ds(c0 + CONV_HALO - (kk - 1) + k, ch), :]
            o_ref[c0:c0 + ch, :] = acc

    return pl.pallas_call(
        body, out_shape=jax.ShapeDtypeStruct((n_batch * seq, cc), F32), grid=(n_batch, nj),
        in_specs=[_bs((seq, LANE), lambda b, j: (b, j)), _bs((seq, LANE), lambda b, j: (b, nj + j)),
                  _bs((kk, LANE), lambda b, j: (0, j)), _bs((1, LANE), lambda b, j: (0, j))],
        out_specs=_bs((seq, LANE), lambda b, j: (b, j)),
        scratch_shapes=[pltpu.VMEM((seq + CONV_HALO, LANE), F32)], name=name,
        compiler_params=_params("parallel", "parallel"))(proj, proj, dw_w, dw_b)


def _glu_conv_bwd(proj, dw_w, dy1, n_batch, seq, name):
    kk, cc = dw_w.shape
    nj = cc // LANE
    ch = min(256, seq)

    def body(a_ref, gl_ref, dy_ref, w_ref, da_ref, dgl_ref, dw_ref, db_ref, padf, padb):
        first = pl.program_id(1) == 0
        padf[0:CONV_HALO, :] = jnp.zeros((CONV_HALO, LANE), F32)
        padf[CONV_HALO:CONV_HALO + seq, :] = a_ref[...] * _sig(gl_ref[...])
        padb[0:seq, :] = dy_ref[...]
        padb[seq:seq + CONV_HALO, :] = jnp.zeros((CONV_HALO, LANE), F32)

        @pl.when(first)
        def _():
            dw_ref[...] = jnp.zeros((kk, LANE), F32)
            db_ref[...] = jnp.zeros((1, LANE), F32)

        for c0 in range(0, seq, ch):
            acc = jnp.zeros((ch, LANE), F32)
            for k in range(kk):
                acc = acc + w_ref[k:k + 1, :] * padb[pl.ds(c0 + (kk - 1) - k, ch), :]
            sg = _sig(gl_ref[c0:c0 + ch, :])
            da_ref[c0:c0 + ch, :] = (acc * sg).astype(BF16)
            dgl_ref[c0:c0 + ch, :] = (acc * a_ref[c0:c0 + ch, :] * sg * (1.0 - sg)).astype(BF16)
        for k in range(kk):
            s = jnp.zeros((1, LANE), F32)
            for c0 in range(0, seq, ch):
                s = s + jnp.sum(padb[c0:c0 + ch, :] * padf[pl.ds(c0 + CONV_HALO - (kk - 1) + k, ch), :],
                                axis=0, keepdims=True)
            dw_ref[k:k + 1, :] += s
        db_ref[...] += jnp.sum(dy_ref[...], axis=0, keepdims=True)

    tok = _bs((seq, LANE), lambda j, b: (b, j))
    t_dim = n_batch * seq
    return pl.pallas_call(
        body, out_shape=(jax.ShapeDtypeStruct((t_dim, cc), BF16), jax.ShapeDtypeStruct((t_dim, cc), BF16),
                         jax.ShapeDtypeStruct((kk, cc), F32), jax.ShapeDtypeStruct((1, cc), F32)),
        grid=(nj, n_batch),
        in_specs=[tok, _bs((seq, LANE), lambda j, b: (b, nj + j)), tok, _bs((kk, LANE), lambda j, b: (0, j))],
        out_specs=(tok, tok, _bs((kk, LANE), lambda j, b: (0, j)), _bs((1, LANE), lambda j, b: (0, j))),
        scratch_shapes=[pltpu.VMEM((seq + CONV_HALO, LANE), F32), pltpu.VMEM((seq + CONV_HALO, LANE), F32)],
        name=name, compiler_params=_params("parallel", "arbitrary"))(proj, proj, dy1, dw_w)


def _ln_silu_fwd(y1, g, b, name):
    t_dim, c = y1.shape
    tm = _tile(t_dim, 512)

    def body(y_ref, g_ref, b_ref, o_ref):
        yv = y_ref[...]
        xc = yv - jnp.mean(yv, axis=-1, keepdims=True)
        rstd = lax.rsqrt(jnp.mean(xc * xc, axis=-1, keepdims=True) + EPS)
        y2 = xc * rstd * g_ref[...] + b_ref[...]
        o_ref[...] = (y2 * _sig(y2)).astype(BF16)

    row = _bs((tm, c), lambda i: (i, 0))
    vec = _bs((1, c), lambda i: (0, 0))
    return pl.pallas_call(
        body, out_shape=jax.ShapeDtypeStruct((t_dim, c), BF16), grid=(t_dim // tm,),
        in_specs=[row, vec, vec], out_specs=row, name=name, compiler_params=_params("parallel"))(y1, g, b)


def _ln_silu_bwd(y1, g, b, dy3, name):
    t_dim, c = y1.shape
    tm = _tile(t_dim, 512)

    def body(y_ref, g_ref, b_ref, d_ref, dy_ref, dg_ref, db_ref):
        yv = y_ref[...]
        gv = g_ref[...]
        xc = yv - jnp.mean(yv, axis=-1, keepdims=True)
        rstd = lax.rsqrt(jnp.mean(xc * xc, axis=-1, keepdims=True) + EPS)
        yh = xc * rstd
        y2 = yh * gv + b_ref[...]
        s = _sig(y2)
        dy2 = d_ref[...].astype(F32) * (s * (1.0 + y2 * (1.0 - s)))
        dyh = dy2 * gv
        dy_ref[...] = rstd * (dyh - jnp.mean(dyh, axis=-1, keepdims=True)
                              - yh * jnp.mean(dyh * yh, axis=-1, keepdims=True))
        dg_part = jnp.sum(dy2 * yh, axis=0, keepdims=True)
        db_part = jnp.sum(dy2, axis=0, keepdims=True)

        @pl.when(pl.program_id(0) == 0)
        def _():
            dg_ref[...] = dg_part
            db_ref[...] = db_part

        @pl.when(pl.program_id(0) > 0)
        def _():
            dg_ref[...] += dg_part
            db_ref[...] += db_part

    row = _bs((tm, c), lambda i: (i, 0))
    vec = _bs((1, c), lambda i: (0, 0))
    return pl.pallas_call(
        body, out_shape=(jax.ShapeDtypeStruct((t_dim, c), F32), jax.ShapeDtypeStruct((1, c), F32),
                         jax.ShapeDtypeStruct((1, c), F32)),
        grid=(t_dim // tm,), in_specs=[row, vec, vec, row], out_specs=(row, vec, vec), name=name,
        compiler_params=_params("arbitrary"))(y1, g, b, dy3)


def _pool_fwd(proj, col0, n_groups, n_batch, seq, name):
    ch = min(256, seq)

    def body(u_ref, o_ref, pad):
        w = lax.shift_left(jnp.int32(2), pl.program_id(1))
        pad[0:POOL_HALO, :] = jnp.zeros((POOL_HALO, LANE), F32)
        pad[POOL_HALO:POOL_HALO + seq, :] = u_ref[...]
        for c0 in range(0, seq, ch):
            acc = jnp.zeros((ch, LANE), F32)
            for j in range(POOL_WINDOW_MAX):
                acc = acc + jnp.where(j < w, 1.0, 0.0).astype(F32) * pad[pl.ds(c0 + POOL_HALO - j, ch), :]
            t = c0 + lax.broadcasted_iota(jnp.int32, (ch, LANE), 0)
            cnt = jnp.minimum(t + 1, w).astype(F32)
            o_ref[c0:c0 + ch, :] = (acc / cnt - u_ref[c0:c0 + ch, :]).astype(BF16)

    return pl.pallas_call(
        body, out_shape=jax.ShapeDtypeStruct((n_batch * seq, n_groups * LANE), BF16), grid=(n_batch, n_groups),
        in_specs=[_bs((seq, LANE), lambda b, g: (b, col0 + g))], out_specs=_bs((seq, LANE), lambda b, g: (b, g)),
        scratch_shapes=[pltpu.VMEM((seq + POOL_HALO, LANE), F32)], name=name,
        compiler_params=_params("parallel", "parallel"))(proj)


def _pool_bwd(dzp, n_groups, n_batch, seq, name):
    ch = min(256, seq)

    def body(d_ref, o_ref, pad):
        w = lax.shift_left(jnp.int32(2), pl.program_id(1))
        for c0 in range(0, seq, ch):
            t = c0 + lax.broadcasted_iota(jnp.int32, (ch, LANE), 0)
            cnt = jnp.minimum(t + 1, w).astype(F32)
            pad[c0:c0 + ch, :] = d_ref[c0:c0 + ch, :] / cnt
        pad[seq:seq + POOL_HALO, :] = jnp.zeros((POOL_HALO, LANE), F32)
        for c0 in range(0, seq, ch):
            acc = jnp.zeros((ch, LANE), F32)
            for j in range(POOL_WINDOW_MAX):
                acc = acc + jnp.where(j < w, 1.0, 0.0).astype(F32) * pad[pl.ds(c0 + j, ch), :]
            o_ref[c0:c0 + ch, :] = (acc - d_ref[c0:c0 + ch, :]).astype(BF16)

    tok = _bs((seq, LANE), lambda b, g: (b, g))
    return pl.pallas_call(
        body, out_shape=jax.ShapeDtypeStruct((n_batch * seq, n_groups * LANE), BF16), grid=(n_batch, n_groups),
        in_specs=[tok], out_specs=tok, scratch_shapes=[pltpu.VMEM((seq + POOL_HALO, LANE), F32)], name=name,
        compiler_params=_params("parallel", "parallel"))(dzp)


def _merge_fwd(proj, col0, yc, yp, scale, name):
    t_dim, d = yc.shape
    half = d // 2
    tm = _tile(t_dim, 512)
    c0 = col0 // half

    def body(gc_ref, gp_ref, yc_ref, yp_ref, s_ref, o_ref):
        o_ref[...] = (_sig(gc_ref[...]) * yc_ref[...] + _sig(gp_ref[...]) * (yp_ref[...] * s_ref[...])).astype(BF16)

    blk = _bs((tm, half), lambda i, j: (i, j))
    return pl.pallas_call(
        body, out_shape=jax.ShapeDtypeStruct((t_dim, d), BF16), grid=(t_dim // tm, 2),
        in_specs=[_bs((tm, half), lambda i, j: (i, c0 + j)), _bs((tm, half), lambda i, j: (i, c0 + 2 + j)),
                  blk, blk, _bs((1, half), lambda i, j: (0, j))],
        out_specs=blk, name=name, compiler_params=_params("parallel", "parallel"))(proj, proj, yc, yp, scale)


def _merge_bwd(proj, col0, yc, yp, scale, dm, name):
    t_dim, d = yc.shape
    half = d // 2
    tm = _tile(t_dim, 512)
    c0 = col0 // half

    def body(gc_ref, gp_ref, yc_ref, yp_ref, s_ref, dm_ref, dgc_ref, dgp_ref, dyc_ref, dyp_ref, ds_ref):
        dmv = dm_ref[...].astype(F32)
        sgc = _sig(gc_ref[...])
        sgp = _sig(gp_ref[...])
        sv = s_ref[...]
        ypre = yp_ref[...]
        dgc_ref[...] = (dmv * yc_ref[...] * sgc * (1.0 - sgc)).astype(BF16)
        dgp_ref[...] = (dmv * (ypre * sv) * sgp * (1.0 - sgp)).astype(BF16)
        dyc_ref[...] = (dmv * sgc).astype(BF16)
        dyp = dmv * sgp
        dyp_ref[...] = (dyp * sv).astype(BF16)
        part = jnp.sum(dyp * ypre, axis=0, keepdims=True)

        @pl.when(pl.program_id(1) == 0)
        def _():
            ds_ref[...] = part

        @pl.when(pl.program_id(1) > 0)
        def _():
            ds_ref[...] += part

    blk = _bs((tm, half), lambda j, i: (i, j))
    big = jax.ShapeDtypeStruct((t_dim, d), BF16)
    return pl.pallas_call(
        body, out_shape=(big, big, big, big, jax.ShapeDtypeStruct((1, d), F32)), grid=(2, t_dim // tm),
        in_specs=[_bs((tm, half), lambda j, i: (i, c0 + j)), _bs((tm, half), lambda j, i: (i, c0 + 2 + j)),
                  blk, blk, _bs((1, half), lambda j, i: (0, j)), blk],
        out_specs=(blk, blk, blk, blk, _bs((1, half), lambda j, i: (0, j))), name=name,
        compiler_params=_params("parallel", "arbitrary"))(proj, proj, yc, yp, scale, dm)


def _attn_fwd(q, kv, n_batch, seq, m_len, name):
    d = q.shape[1]
    hd = d // XA_HEADS
    tq = _tile(seq, 1024)
    nq = seq // tq
    scale = hd ** -0.5

    def body(q_ref, k_ref, v_ref, o_ref):
        sc = lax.dot_general(q_ref[...].astype(BF16), k_ref[...].astype(BF16), (((1,), (1,)), ((), ())),
                             preferred_element_type=F32) * scale
        p = jnp.exp(sc - jnp.max(sc, axis=-1, keepdims=True))
        pr = p / jnp.sum(p, axis=-1, keepdims=True)
        o_ref[...] = jnp.dot(pr.astype(BF16), v_ref[...].astype(BF16), preferred_element_type=F32).astype(BF16)

    return pl.pallas_call(
        body, out_shape=jax.ShapeDtypeStruct((n_batch * seq, d), BF16), grid=(n_batch, XA_HEADS, nq),
        in_specs=[_bs((tq, hd), lambda b, h, i: (b * nq + i, h)), _bs((m_len, hd), lambda b, h, i: (b, h)),
                  _bs((m_len, hd), lambda b, h, i: (b, XA_HEADS + h))],
        out_specs=_bs((tq, hd), lambda b, h, i: (b * nq + i, h)), name=name,
        compiler_params=_params("parallel", "parallel", "parallel"))(q, kv, kv)


def _attn_bwd(q, kv, datt, n_batch, seq, m_len, name):
    d = q.shape[1]
    hd = d // XA_HEADS
    tq = _tile(seq, 1024)
    nq = seq // tq
    scale = hd ** -0.5

    def body(q_ref, k_ref, v_ref, do_ref, dq_ref, dk_ref, dv_ref):
        qb = q_ref[...].astype(BF16)
        kb = k_ref[...].astype(BF16)
        vb = v_ref[...].astype(BF16)
        dob = do_ref[...].astype(BF16)
        sc = lax.dot_general(qb, kb, (((1,), (1,)), ((), ())), preferred_element_type=F32) * scale
        p = jnp.exp(sc - jnp.max(sc, axis=-1, keepdims=True))
        pr = p / jnp.sum(p, axis=-1, keepdims=True)
        dpr = lax.dot_general(dob, vb, (((1,), (1,)), ((), ())), preferred_element_type=F32)
        dsc = pr * (dpr - jnp.sum(dpr * pr, axis=-1, keepdims=True)) * scale
        dsb = dsc.astype(BF16)
        dq_ref[...] = jnp.dot(dsb, kb, preferred_element_type=F32).astype(BF16)
        dv_part = lax.dot_general(pr.astype(BF16), dob, (((0,), (0,)), ((), ())), preferred_element_type=F32)
        dk_part = lax.dot_general(dsb, qb, (((0,), (0,)), ((), ())), preferred_element_type=F32)

        @pl.when(pl.program_id(2) == 0)
        def _():
            dk_ref[...] = dk_part
            dv_ref[...] = dv_part

        @pl.when(pl.program_id(2) > 0)
        def _():
            dk_ref[...] += dk_part
            dv_ref[...] += dv_part

    qs = _bs((tq, hd), lambda b, h, i: (b * nq + i, h))
    ks = _bs((m_len, hd), lambda b, h, i: (b, h))
    return pl.pallas_call(
        body, out_shape=(jax.ShapeDtypeStruct((n_batch * seq, d), BF16), jax.ShapeDtypeStruct((n_batch * m_len, d), F32),
                         jax.ShapeDtypeStruct((n_batch * m_len, d), F32)),
        grid=(n_batch, XA_HEADS, nq),
        in_specs=[qs, ks, _bs((m_len, hd), lambda b, h, i: (b, XA_HEADS + h)), qs],
        out_specs=(qs, ks, ks), name=name,
        compiler_params=_params("parallel", "parallel", "arbitrary"))(q, kv, kv, datt)


def _gelu_parts(g):
    th = jnp.tanh(GELU_C0 * (g + GELU_C1 * g * g * g))
    return th, 0.5 * g * (1.0 + th)


def _ffn_act_fwd(up0, dw_w, n_batch, seq, name):
    kk, c2 = dw_w.shape
    f_dim = c2 // 2
    wd = 2 * LANE
    nj = f_dim // wd
    ch = min(128, seq)

    def body(g_ref, v_ref, wg_ref, wv_ref, o_ref, padg, padv):
        for pad, src in ((padg, g_ref), (padv, v_ref)):
            pad[0:FFN_HALO, :] = jnp.zeros((FFN_HALO, wd), F32)
            pad[FFN_HALO:FFN_HALO + seq, :] = src[...]
        for c0 in range(0, seq, ch):
            gate = jnp.zeros((ch, wd), F32)
            val = jnp.zeros((ch, wd), F32)
            for k in range(kk):
                off = c0 + FFN_HALO - (kk - 1) + k
                gate = gate + wg_ref[k:k + 1, :] * padg[pl.ds(off, ch), :]
                val = val + wv_ref[k:k + 1, :] * padv[pl.ds(off, ch), :]
            o_ref[c0:c0 + ch, :] = (_gelu_parts(gate)[1] * val).astype(BF16)

    return pl.pallas_call(
        body, out_shape=jax.ShapeDtypeStruct((n_batch * seq, f_dim), BF16), grid=(n_batch, nj),
        in_specs=[_bs((seq, wd), lambda b, j: (b, j)), _bs((seq, wd), lambda b, j: (b, nj + j)),
                  _bs((kk, wd), lambda b, j: (0, j)), _bs((kk, wd), lambda b, j: (0, nj + j))],
        out_specs=_bs((seq, wd), lambda b, j: (b, j)),
        scratch_shapes=[pltpu.VMEM((seq + FFN_HALO, wd), F32), pltpu.VMEM((seq + FFN_HALO, wd), F32)], name=name,
        compiler_params=_params("parallel", "parallel"))(up0, up0, dw_w, dw_w)


def _ffn_act_bwd(up0, dw_w, dact, n_batch, seq, name):
    kk, c2 = dw_w.shape
    f_dim = c2 // 2
    wd = 2 * LANE
    nj = f_dim // wd
    ch = min(128, seq)

    def body(g_ref, v_ref, wg_ref, wv_ref, da_ref, dg_ref, dv_ref, dwg_ref, dwv_ref, padg, padv, pbg, pbv):
        for pad, src in ((padg, g_ref), (padv, v_ref)):
            pad[0:FFN_HALO, :] = jnp.zeros((FFN_HALO, wd), F32)
            pad[FFN_HALO:FFN_HALO + seq, :] = src[...]
        for pb in (pbg, pbv):
            pb[seq:seq + FFN_HALO, :] = jnp.zeros((FFN_HALO, wd), F32)

        @pl.when(pl.program_id(1) == 0)
        def _():
            dwg_ref[...] = jnp.zeros((kk, wd), F32)
            dwv_ref[...] = jnp.zeros((kk, wd), F32)

        for c0 in range(0, seq, ch):
            gate = jnp.zeros((ch, wd), F32)
            val = jnp.zeros((ch, wd), F32)
            for k in range(kk):
                off = c0 + FFN_HALO - (kk - 1) + k
                gate = gate + wg_ref[k:k + 1, :] * padg[pl.ds(off, ch), :]
                val = val + wv_ref[k:k + 1, :] * padv[pl.ds(off, ch), :]
            th, gelu = _gelu_parts(gate)
            dgelu = 0.5 * (1.0 + th) + 0.5 * gate * (1.0 - th * th) * GELU_C0 * (1.0 + 3.0 * GELU_C1 * gate * gate)
            dav = da_ref[c0:c0 + ch, :].astype(F32)
            pbg[c0:c0 + ch, :] = dav * val * dgelu
            pbv[c0:c0 + ch, :] = dav * gelu
        for pb, pad, w_ref, d_ref, dw_ref in ((pbg, padg, wg_ref, dg_ref, dwg_ref), (pbv, padv, wv_ref, dv_ref, dwv_ref)):
            for c0 in range(0, seq, ch):
                acc = jnp.zeros((ch, wd), F32)
                for k in range(kk):
                    acc = acc + w_ref[k:k + 1, :] * pb[pl.ds(c0 + (kk - 1) - k, ch), :]
                d_ref[c0:c0 + ch, :] = acc.astype(BF16)
            for k in range(kk):
                s = jnp.zeros((1, wd), F32)
                for c0 in range(0, seq, ch):
                    s = s + jnp.sum(pb[c0:c0 + ch, :] * pad[pl.ds(c0 + FFN_HALO - (kk - 1) + k, ch), :],
                                    axis=0, keepdims=True)
                dw_ref[k:k + 1, :] += s

    t_dim = n_batch * seq
    tok = _bs((seq, wd), lambda j, b: (b, j))
    wblk = _bs((kk, wd), lambda j, b: (0, j))
    pad_shape = pltpu.VMEM((seq + FFN_HALO, wd), F32)
    return pl.pallas_call(
        body, out_shape=(jax.ShapeDtypeStruct((t_dim, f_dim), BF16), jax.ShapeDtypeStruct((t_dim, f_dim), BF16),
                         jax.ShapeDtypeStruct((kk, f_dim), F32), jax.ShapeDtypeStruct((kk, f_dim), F32)),
        grid=(nj, n_batch),
        in_specs=[tok, _bs((seq, wd), lambda j, b: (b, nj + j)), wblk, _bs((kk, wd), lambda j, b: (0, nj + j)), tok],
        out_specs=(tok, tok, wblk, wblk), scratch_shapes=[pad_shape, pad_shape, pad_shape, pad_shape], name=name,
        compiler_params=_params("parallel", "arbitrary"))(up0, up0, dw_w, dw_w, dact)


def _sum_rows(parts, out_dtype, name):
    r_dim, c_dim = parts[0].shape
    tr = _tile(r_dim, 1200, SUBLANE)
    n = len(parts)

    def body(*refs):
        acc = refs[0][...].astype(F32)
        for r in refs[1:n]:
            acc = acc + r[...].astype(F32)
        refs[n][...] = acc.astype(out_dtype)

    blk = _bs((tr, c_dim), lambda i: (i, 0))
    return pl.pallas_call(
        body, out_shape=jax.ShapeDtypeStruct((r_dim, c_dim), out_dtype), grid=(r_dim // tr,),
        in_specs=[blk] * n, out_specs=blk, name=name, compiler_params=_params("parallel"))(*parts)


def _adamw(w, g, m, v, name):
    shape = w.shape
    c_dim = shape[-1]
    r_dim = w.size // c_dim
    two_d = lambda t: t.reshape(r_dim, c_dim)
    tr = _tile(r_dim, max(SUBLANE, (256 * 1024) // max(c_dim, LANE) // SUBLANE * SUBLANE), SUBLANE)
    c1 = 1.0 - ADAM_B1 ** ADAM_STEP
    c2 = 1.0 - ADAM_B2 ** ADAM_STEP

    def body(w_ref, g_ref, m_ref, v_ref, d_ref, mo_ref, vo_ref):
        gv = g_ref[...]
        mn = ADAM_B1 * m_ref[...] + (1.0 - ADAM_B1) * gv
        vn = ADAM_B2 * v_ref[...] + (1.0 - ADAM_B2) * (gv * gv)
        mo_ref[...] = mn
        vo_ref[...] = vn
        d_ref[...] = -ADAM_LR * ((mn / c1) / (jnp.sqrt(vn / c2) + ADAM_EPS) + ADAM_WD * w_ref[...])

    blk = _bs((tr, c_dim), lambda i: (i, 0))
    out = jax.ShapeDtypeStruct((r_dim, c_dim), F32)
    d, mo, vo = pl.pallas_call(
        body, out_shape=(out, out, out), grid=(r_dim // tr,), in_specs=[blk] * 4, out_specs=(blk, blk, blk),
        name=name, compiler_params=_params("parallel"))(two_d(w), two_d(g), two_d(m), two_d(v))
    return d.reshape(shape), mo.reshape(shape), vo.reshape(shape)


HBM_SPEC = pl.BlockSpec(memory_space=pltpu.HBM)


def _position():
    return lax.axis_index("x"), lax.axis_index("y"), lax.axis_index("c")


def _all_gather(shard, name):
    def body(x_ref, out_ref, send_sems, recv_sems, local_sem):
        x, y, c = _position()
        me, sibling = (x, y, c), (x, y, 1 - c)
        chips = [(1 - x, y), (x, 1 - y), (1 - x, 1 - y)]

        def rows(px, py, pc):
            return out_ref.at[4 * px + 2 * py + pc]

        def copy(k, block, to, src=None):
            return pltpu.make_async_remote_copy(
                src_ref=rows(*block) if src is None else src, dst_ref=rows(*block),
                send_sem=send_sems.at[k], recv_sem=recv_sems.at[k], device_id=to, device_id_type=MESH)

        mine = pltpu.make_async_copy(x_ref, rows(*me), local_sem)
        mine.start()
        first = [copy(0, me, sibling, src=x_ref)]
        first += [copy(1 + j, me, (*chip, c), src=x_ref) for j, chip in enumerate(chips)]
        for cp in first:
            cp.start()
        passed = [copy(4 + j, (*chip, c), sibling) for j, chip in enumerate(chips)]
        for j, chip in enumerate(chips):
            copy(1 + j, (*chip, c), me).wait_recv()
            passed[j].start()
        copy(0, sibling, me).wait_recv()
        for j, chip in enumerate(chips):
            copy(4 + j, (*chip, 1 - c), me).wait_recv()
        for cp in first + passed:
            cp.wait_send()
        mine.wait()

    return pl.pallas_call(
        body, out_shape=jax.ShapeDtypeStruct((N_DEV,) + shard.shape, shard.dtype),
        in_specs=[HBM_SPEC], out_specs=HBM_SPEC,
        scratch_shapes=[pltpu.SemaphoreType.DMA((7,)), pltpu.SemaphoreType.DMA((7,)), pltpu.SemaphoreType.DMA(())],
        name=name)(shard)


CHIP_RELATIONS = ((0, 0), (1, 0), (0, 1), (1, 1))


def _rs_pair_exchange(g, name):
    _, r_dim, c_dim = g.shape
    n = len(CHIP_RELATIONS)

    def body(g_ref, recv_ref, send_sems, recv_sems):
        x, y, c = _position()
        sibling = (x, y, 1 - c)
        copies = []
        for k, (rx, ry) in enumerate(CHIP_RELATIONS):
            px = x + rx - 2 * x * rx
            py = y + ry - 2 * y * ry
            copies.append(pltpu.make_async_remote_copy(
                src_ref=g_ref.at[4 * px + 2 * py + 1 - c], dst_ref=recv_ref.at[k], send_sem=send_sems.at[k],
                recv_sem=recv_sems.at[k], device_id=sibling, device_id_type=MESH))
        for cp in copies:
            cp.start()
        for cp in copies:
            cp.wait()

    return pl.pallas_call(
        body, out_shape=jax.ShapeDtypeStruct((n, r_dim, c_dim), g.dtype), in_specs=[HBM_SPEC], out_specs=HBM_SPEC,
        scratch_shapes=[pltpu.SemaphoreType.DMA((n,)), pltpu.SemaphoreType.DMA((n,))], name=name)(g)


def _rs_pair_sum(g, recv, name):
    _, r_dim, c_dim = g.shape
    n = len(CHIP_RELATIONS)
    tr = _tile(r_dim, 1200, SUBLANE)
    x, y, c = _position()
    own = jnp.stack([4 * (x + rx - 2 * x * rx) + 2 * (y + ry - 2 * y * ry) + c for rx, ry in CHIP_RELATIONS])

    def body(own_ref, g_ref, r_ref, o_ref):
        o_ref[...] = (g_ref[...].astype(F32) + r_ref[...].astype(F32)).astype(o_ref.dtype)

    blk = _bs((None, tr, c_dim), lambda k, i, own_ref: (k, i, 0))
    return pl.pallas_call(
        body, out_shape=jax.ShapeDtypeStruct((n, r_dim, c_dim), g.dtype),
        grid_spec=pltpu.PrefetchScalarGridSpec(
            num_scalar_prefetch=1, grid=(n, r_dim // tr),
            in_specs=[_bs((None, tr, c_dim), lambda k, i, own_ref: (own_ref[k], i, 0)), blk], out_specs=blk),
        name=name, compiler_params=_params("parallel", "parallel"))(own.astype(jnp.int32), g, recv)


SEM_SPEC = pl.BlockSpec(memory_space=pltpu.SEMAPHORE)
DATAFLOW = pltpu.SideEffectType.DATAFLOW_SIDE_EFFECTING
CHIP_FLIPS = CHIP_RELATIONS[1:]
TOKEN = jax.ShapeDtypeStruct((SUBLANE, LANE), F32)


def _flip(v, r):
    return v + r - 2 * v * r


def _chip_copies(src_ref, src_of, dst_ref, dst_of, send_sems, recv_sems):
    x, y, c = _position()
    me = 4 * x + 2 * y + c
    out = []
    for k, (rx, ry) in enumerate(CHIP_FLIPS):
        px, py = _flip(x, rx), _flip(y, ry)
        peer = 4 * px + 2 * py + c
        out.append(pltpu.make_async_remote_copy(
            src_ref=src_ref.at[src_of(k, me, peer)], dst_ref=dst_ref.at[dst_of(k, me, peer)],
            send_sem=send_sems.at[k], recv_sem=recv_sems.at[k], device_id=(px, py, c), device_id_type=MESH))
    return out


def _ag_chips_start(land, after, name):
    def body(land_ref, after_ref, send_sems, recv_sems, land_thru, token):
        for cp in _chip_copies(land_ref, lambda k, me, peer: me, land_ref, lambda k, me, peer: me, send_sems, recv_sems):
            cp.start()
        token[...] = jnp.zeros(TOKEN.shape, TOKEN.dtype)

    n = len(CHIP_FLIPS)
    return pl.pallas_call(
        body, name=name,
        out_shape=(pltpu.SemaphoreType.DMA((n,)), pltpu.SemaphoreType.DMA((n,)), pltpu.HBM(land.shape, land.dtype), TOKEN),
        in_specs=(HBM_SPEC, ANY_SPEC), out_specs=(SEM_SPEC, SEM_SPEC, HBM_SPEC, pl.BlockSpec(memory_space=pltpu.VMEM)),
        input_output_aliases={0: 2}, compiler_params=pltpu.CompilerParams(has_side_effects=DATAFLOW),
    )(pltpu.with_memory_space_constraint(land, pltpu.HBM), after)


def _ag_chips_wait(send_sems, recv_sems, land, after, name):
    def body(land_ref, send_sems, recv_sems, after_ref, land_out):
        for cp in _chip_copies(land_ref, lambda k, me, peer: me, land_ref, lambda k, me, peer: peer, send_sems, recv_sems):
            cp.wait_send()
            cp.wait_recv()

    return pl.pallas_call(
        body, name=name, out_shape=pltpu.HBM(land.shape, land.dtype),
        in_specs=(HBM_SPEC, SEM_SPEC, SEM_SPEC, ANY_SPEC), out_specs=HBM_SPEC,
        input_output_aliases={0: 0}, compiler_params=pltpu.CompilerParams(has_side_effects=DATAFLOW),
    )(land, send_sems, recv_sems, after)


def _ag_pair_forward(land, name):
    n = len(CHIP_RELATIONS)

    def body(land_in, land_ref, send_sems, recv_sems):
        x, y, c = _position()
        copies = []
        for k, (rx, ry) in enumerate(CHIP_RELATIONS):
            chip = 4 * _flip(x, rx) + 2 * _flip(y, ry)
            mine = pltpu.make_async_remote_copy(
                src_ref=land_ref.at[chip + c], dst_ref=land_ref.at[chip + c], send_sem=send_sems.at[k],
                recv_sem=recv_sems.at[k], device_id=(x, y, 1 - c), device_id_type=MESH)
            theirs = pltpu.make_async_remote_copy(
                src_ref=land_ref.at[chip + c], dst_ref=land_ref.at[chip + 1 - c], send_sem=send_sems.at[k],
                recv_sem=recv_sems.at[k], device_id=(x, y, 1 - c), device_id_type=MESH)
            copies.append((mine, theirs))
        for mine, _ in copies:
            mine.start()
        for mine, theirs in copies:
            mine.wait_send()
            theirs.wait_recv()

    return pl.pallas_call(
        body, out_shape=jax.ShapeDtypeStruct(land.shape, land.dtype), in_specs=[HBM_SPEC], out_specs=HBM_SPEC,
        input_output_aliases={0: 0},
        scratch_shapes=[pltpu.SemaphoreType.DMA((n,)), pltpu.SemaphoreType.DMA((n,))], name=name)(land)


def _rs_chips_start(pair, name):
    _, r_dim, c_dim = pair.shape
    n = len(CHIP_FLIPS)

    def body(pair_ref, far_ref, send_sems, recv_sems, pair_thru, far_thru, token):
        for cp in _chip_copies(pair_ref, lambda k, me, peer: k + 1, far_ref, lambda k, me, peer: k, send_sems, recv_sems):
            cp.start()
        token[...] = jnp.zeros(TOKEN.shape, TOKEN.dtype)

    far = lax.empty((n, r_dim, c_dim), pair.dtype)
    return pl.pallas_call(
        body, name=name,
        out_shape=(pltpu.SemaphoreType.DMA((n,)), pltpu.SemaphoreType.DMA((n,)), pltpu.HBM(pair.shape, pair.dtype),
                   pltpu.HBM(far.shape, far.dtype), TOKEN),
        in_specs=(HBM_SPEC, HBM_SPEC),
        out_specs=(SEM_SPEC, SEM_SPEC, HBM_SPEC, HBM_SPEC, pl.BlockSpec(memory_space=pltpu.VMEM)),
        input_output_aliases={0: 2, 1: 3}, compiler_params=pltpu.CompilerParams(has_side_effects=DATAFLOW),
    )(pltpu.with_memory_space_constraint(pair, pltpu.HBM), pltpu.with_memory_space_constraint(far, pltpu.HBM))


def _rs_chips_wait(send_sems, recv_sems, pair, far, after, name):
    def body(pair_ref, far_ref, send_sems, recv_sems, after_ref, pair_out, far_out):
        for cp in _chip_copies(pair_ref, lambda k, me, peer: k + 1, far_ref, lambda k, me, peer: k, send_sems, recv_sems):
            cp.wait_send()
            cp.wait_recv()

    return pl.pallas_call(
        body, name=name, out_shape=(pltpu.HBM(pair.shape, pair.dtype), pltpu.HBM(far.shape, far.dtype)),
        in_specs=(HBM_SPEC, HBM_SPEC, SEM_SPEC, SEM_SPEC, ANY_SPEC),
        out_specs=(HBM_SPEC, HBM_SPEC), input_output_aliases={0: 0, 1: 1},
        compiler_params=pltpu.CompilerParams(has_side_effects=DATAFLOW),
    )(pair, far, send_sems, recv_sems, after)


def _rs_final_sum(pair, far, name):
    _, r_dim, c_dim = pair.shape
    tr = _tile(r_dim, 1200, SUBLANE)

    def body(p_ref, f0_ref, f1_ref, f2_ref, o_ref):
        o_ref[...] = ((p_ref[...].astype(F32) + f0_ref[...].astype(F32)) + f1_ref[...].astype(F32)) + f2_ref[...].astype(F32)

    def slot(k):
        return _bs((None, tr, c_dim), lambda i: (k, i, 0))

    return pl.pallas_call(
        body, out_shape=jax.ShapeDtypeStruct((r_dim, c_dim), F32), grid=(r_dim // tr,),
        in_specs=[slot(0), slot(0), slot(1), slot(2)], out_specs=_bs((tr, c_dim), lambda i: (i, 0)), name=name,
        compiler_params=_params("parallel"))(pair, far, far, far)


def _reduce_scatter_begin(g, name):
    recv = _rs_pair_exchange(g, name + "_pair")
    pair = _rs_pair_sum(g, recv, name + "_pairsum")
    return _rs_chips_start(pair, name + "_chips_start")


def _reduce_scatter_end(state, after, name):
    send_sems, recv_sems, pair, far, _ = state
    pair, far = _rs_chips_wait(send_sems, recv_sems, pair, far, after, name + "_chips_wait")
    return _rs_final_sum(pair, far, name + "_sum")


MATRICES = (("w_in", True), ("w_out", False), ("w_q", False), ("w_kv", True), ("w_o", False), ("w_up", True),
            ("w_down", False), ("w_conv_out", True), ("w_pool_grp", True))
PARTS = (("mix", ("w_in", "w_conv_out", "w_pool_grp", "w_out")), ("rest", ("w_q", "w_kv", "w_o", "w_up", "w_down")))


def _to_rows(name, transposed, w, d_model):
    if name == "w_pool_grp":
        w = jnp.swapaxes(w, 1, 2)
    elif transposed:
        w = w.T
    return w.reshape(-1, d_model)


def _from_rows(name, transposed, rows, shard_shape):
    if name == "w_pool_grp":
        g, i, o = shard_shape
        return jnp.swapaxes(rows.reshape(g, o, i), 1, 2)
    if transposed:
        return rows.reshape(shard_shape[1], shard_shape[0]).T
    return rows.reshape(shard_shape)


def _gathered_matrix(name, blocks, shard_shape):
    if name == "w_pool_grp":
        g, i, o = shard_shape
        return jnp.swapaxes(blocks.reshape(N_DEV, g, o, i), 0, 1).reshape(g, N_DEV * o, i)
    if name == "w_conv_out":
        return blocks.reshape(N_DEV * shard_shape[1], shard_shape[0])
    return blocks.reshape(-1, blocks.shape[-1])


def _scatter_blocks(name, full, shard_shape, d_model):
    if name == "w_pool_grp":
        g, i, o = shard_shape
        return jnp.swapaxes(full.reshape(g, N_DEV, o, i), 0, 1).reshape(N_DEV, -1, d_model)
    return full.reshape(N_DEV, -1, d_model)


def kernel(x, mem, mix_norm_g, w_in, conv_dw_w, conv_dw_b, conv_ln_g, conv_ln_b, w_conv_out, w_pool_grp, pool_scale, w_out, xattn_norm_g, mem_norm_g, w_q, w_kv, w_o, ffn_norm_g, w_up, ffn_dw_w, w_down, final_norm_g, loss_target, m_mix_norm_g, m_w_in, m_conv_dw_w, m_conv_dw_b, m_conv_ln_g, m_conv_ln_b, m_w_conv_out, m_w_pool_grp, m_pool_scale, m_w_out, m_xattn_norm_g, m_mem_norm_g, m_w_q, m_w_kv, m_w_o, m_ffn_norm_g, m_w_up, m_ffn_dw_w, m_w_down, m_final_norm_g, v_mix_norm_g, v_w_in, v_conv_dw_w, v_conv_dw_b, v_conv_ln_g, v_conv_ln_b, v_w_conv_out, v_w_pool_grp, v_pool_scale, v_w_out, v_xattn_norm_g, v_mem_norm_g, v_w_q, v_w_kv, v_w_o, v_ffn_norm_g, v_w_up, v_ffn_dw_w, v_w_down, v_final_norm_g):
    p = dict(locals())
    weight_names = ["mix_norm_g", "w_in", "conv_dw_w", "conv_dw_b", "conv_ln_g", "conv_ln_b", "w_conv_out",
                    "w_pool_grp", "pool_scale", "w_out", "xattn_norm_g", "mem_norm_g", "w_q", "w_kv", "w_o",
                    "ffn_norm_g", "w_up", "ffn_dw_w", "w_down", "final_norm_g"]
    n_batch, seq, d_model = x.shape
    m_len = mem.shape[1]
    depth = w_in.shape[0]
    assert depth == 2, "the exchange schedule below is written for two layers"
    t_dim = n_batch * seq
    c_conv = conv_dw_b.shape[1]
    n_groups = w_pool_grp.shape[1]
    assert w_pool_grp.shape[2] == LANE and c_conv % LANE == 0 and n_groups * LANE == c_conv
    gate_col0 = 2 * c_conv + n_groups * LANE
    pool_col0 = (2 * c_conv) // LANE

    dev = 4 * lax.axis_index("x") + 2 * lax.axis_index("y") + lax.axis_index("c")
    filt = jnp.concatenate([conv_dw_w.reshape(-1), ffn_dw_w.reshape(-1)])
    filt_rows = lax.bitcast_convert_type(filt, BF16).reshape(-1, d_model)
    transposed = dict(MATRICES)
    layout = {part: [(name, transposed[name], _to_rows(name, transposed[name], p[name][0], d_model).shape[0])
                     for name in names] for part, names in PARTS}

    def landing(pack):
        return lax.dynamic_update_index_in_dim(lax.empty((N_DEV,) + pack.shape, pack.dtype), pack, dev, 0)

    ag_state = {}
    after = filt_rows
    for l in range(depth):
        for part, names in PARTS:
            pieces = [_to_rows(name, transposed[name], p[name][l], d_model).astype(BF16) for name in names]
            if (l, part) == (0, PARTS[0][0]):
                pieces.append(filt_rows)
            ag_state[(l, part)] = _ag_chips_start(landing(jnp.concatenate(pieces, axis=0)), after, f"ag{l}{part}_chips_start")
            after = ag_state[(l, part)][3]
    all_started = after

    def gathered(l, part, after):
        send_sems, recv_sems, land, _ = ag_state[(l, part)]
        land = _ag_chips_wait(send_sems, recv_sems, land, after, f"ag{l}{part}_chips_wait")
        land = _ag_pair_forward(land, f"ag{l}{part}_pair_forward")
        out, row0 = {}, 0
        for name, tr, nrows in layout[part]:
            out[name] = _gathered_matrix(name, land[:, row0:row0 + nrows], p[name].shape[1:])
            row0 += nrows
        return out, land, row0

    full = [dict() for _ in range(depth)]
    mats, land0, row0 = gathered(0, PARTS[0][0], all_started)
    full[0].update(mats)
    filt_all = lax.bitcast_convert_type(land0[:, row0:row0 + filt_rows.shape[0]].reshape(N_DEV, -1, 2), F32)
    n_cw = conv_dw_w.size
    kc, cs = conv_dw_w.shape[1:]
    kf, fs = ffn_dw_w.shape[1:]
    conv_w_full = jnp.moveaxis(filt_all[:, :n_cw].reshape(N_DEV, depth, kc, cs), 0, 2).reshape(depth, kc, N_DEV * cs)
    ffn_w_full = jnp.moveaxis(filt_all[:, n_cw:].reshape(N_DEV, depth, kf, fs), 0, 2).reshape(depth, kf, N_DEV * fs)

    vec = lambda a: a.reshape(1, -1)
    x2d = x.reshape(t_dim, d_model)
    mem2d = mem.reshape(n_batch * m_len, d_model)
    mem_n = _rmsnorm_fwd(mem2d, vec(mem_norm_g), "mem_norm")

    saved = []
    xc = x2d
    for l in range(depth):
        if l > 0:
            full[l].update(gathered(l, PARTS[0][0], xc)[0])
        wl = full[l]
        s = {"x0": xc}
        s["h"] = _rmsnorm_fwd(xc, vec(mix_norm_g[l]), f"mix_norm_l{l}")
        s["proj"] = _matmul(s["h"], wl["w_in"], "nt", f"in_proj_l{l}")
        s["y1"] = _glu_conv_fwd(s["proj"], conv_w_full[l], vec(conv_dw_b[l]), n_batch, seq, f"glu_conv_l{l}")
        s["y3"] = _ln_silu_fwd(s["y1"], vec(conv_ln_g[l]), vec(conv_ln_b[l]), f"ln_silu_l{l}")
        s["yc"] = _matmul(s["y3"], wl["w_conv_out"], "nt", f"conv_out_l{l}")
        s["zp"] = _pool_fwd(s["proj"], pool_col0, n_groups, n_batch, seq, f"pool_l{l}")
        s["yp"] = _grouped(s["zp"], wl["w_pool_grp"], "nt", f"pool_proj_l{l}")
        s["merged"] = _merge_fwd(s["proj"], gate_col0, s["yc"], s["yp"], vec(pool_scale[l]), f"merge_l{l}")
        s["x1"] = _matmul(s["merged"], wl["w_out"], "nn", f"mix_out_l{l}", res=xc)
        wl.update(gathered(l, PARTS[1][0], s["x1"])[0])
        s["hq"] = _rmsnorm_fwd(s["x1"], vec(xattn_norm_g[l]), f"xattn_norm_l{l}")
        s["q"] = _matmul(s["hq"], wl["w_q"], "nn", f"q_proj_l{l}")
        s["kv"] = _matmul(mem_n, wl["w_kv"], "nt", f"kv_proj_l{l}")
        s["att"] = _attn_fwd(s["q"], s["kv"], n_batch, seq, m_len, f"attn_l{l}")
        s["x2"] = _matmul(s["att"], wl["w_o"], "nn", f"attn_out_l{l}", res=s["x1"])
        s["hf"] = _rmsnorm_fwd(s["x2"], vec(ffn_norm_g[l]), f"ffn_norm_l{l}")
        s["up0"] = _matmul(s["hf"], wl["w_up"], "nt", f"up_proj_l{l}")
        s["act"] = _ffn_act_fwd(s["up0"], ffn_w_full[l], n_batch, seq, f"ffn_act_l{l}")
        xc = _matmul(s["act"], wl["w_down"], "nn", f"down_proj_l{l}", res=s["x2"])
        saved.append(s)

    dx, dxb, dg_final, loss_part = _loss_head(xc, vec(final_norm_g), loss_target.reshape(t_dim, d_model), "loss_head")

    small = {"final_norm_g": dg_final.reshape(-1)}
    big = [dict() for _ in range(depth)]
    rs_state = {}
    rs_after = loss_part

    def rs_begin(l, part):
        blocks = [_scatter_blocks(name, big[l][name], p[name].shape[1:], d_model).astype(BF16) for name, _, _ in layout[part]]
        rs_state[(l, part)] = _reduce_scatter_begin(jnp.concatenate(blocks, axis=1), f"rs{l}{part}")
        return rs_state[(l, part)][4]

    dmem_n = None
    for l in reversed(range(depth)):
        wl, s = full[l], saved[l]
        sm = {}
        dact = _matmul(dxb, wl["w_down"], "nt", f"d_act_l{l}", after=rs_after)
        big[l]["w_down"] = _matmul(s["act"], dxb, "tn", f"d_w_down_l{l}", out_dtype=BF16)
        dup_g, dup_v, dwf_g, dwf_v = _ffn_act_bwd(s["up0"], ffn_w_full[l], dact, n_batch, seq, f"ffn_act_bwd_l{l}")
        sm["ffn_dw_w"] = jnp.concatenate([dwf_g, dwf_v], axis=1)
        dup0 = jnp.concatenate([dup_g, dup_v], axis=1)
        dhf = _matmul(dup0, wl["w_up"], "nn", f"d_hf_l{l}")
        big[l]["w_up"] = _matmul(dup0, s["hf"], "tn", f"d_w_up_l{l}", out_dtype=BF16)
        dx, dxb, dg = _rmsnorm_bwd(s["x2"], vec(ffn_norm_g[l]), dhf, dx, f"ffn_norm_bwd_l{l}")
        sm["ffn_norm_g"] = dg
        datt = _matmul(dxb, wl["w_o"], "nt", f"d_att_l{l}", after=rs_after)
        big[l]["w_o"] = _matmul(s["att"], dxb, "tn", f"d_w_o_l{l}", out_dtype=BF16)
        dq, dk, dv = _attn_bwd(s["q"], s["kv"], datt, n_batch, seq, m_len, f"attn_bwd_l{l}")
        dkv = jnp.concatenate([dk, dv], axis=1)
        big[l]["w_kv"] = _matmul(dkv, mem_n, "tn", f"d_w_kv_l{l}", out_dtype=BF16)
        dmem_n = _matmul(dkv, wl["w_kv"], "nn", f"d_mem_l{l}", res=dmem_n)
        dhq = _matmul(dq, wl["w_q"], "nt", f"d_hq_l{l}")
        big[l]["w_q"] = _matmul(s["hq"], dq, "tn", f"d_w_q_l{l}", out_dtype=BF16)
        dx, dxb, dg = _rmsnorm_bwd(s["x1"], vec(xattn_norm_g[l]), dhq, dx, f"xattn_norm_bwd_l{l}")
        sm["xattn_norm_g"] = dg
        rs_after = rs_begin(l, PARTS[1][0])
        dmerged = _matmul(dxb, wl["w_out"], "nt", f"d_merged_l{l}", after=rs_after)
        big[l]["w_out"] = _matmul(s["merged"], dxb, "tn", f"d_w_out_l{l}", out_dtype=BF16)
        dgc, dgp, dyc, dyp, dscale = _merge_bwd(s["proj"], gate_col0, s["yc"], s["yp"], vec(pool_scale[l]), dmerged,
                                                f"merge_bwd_l{l}")
        sm["pool_scale"] = dscale
        dzp = _grouped(dyp, wl["w_pool_grp"], "nn", f"d_zp_l{l}")
        big[l]["w_pool_grp"] = _grouped_tn(dyp, s["zp"], n_groups, f"d_w_pool_l{l}")
        du = _pool_bwd(dzp, n_groups, n_batch, seq, f"pool_bwd_l{l}")
        dy3 = _matmul(dyc, wl["w_conv_out"], "nn", f"d_y3_l{l}")
        big[l]["w_conv_out"] = _matmul(dyc, s["y3"], "tn", f"d_w_conv_out_l{l}", out_dtype=BF16)
        dy1, dlg, dlb = _ln_silu_bwd(s["y1"], vec(conv_ln_g[l]), vec(conv_ln_b[l]), dy3, f"ln_silu_bwd_l{l}")
        sm["conv_ln_g"], sm["conv_ln_b"] = dlg, dlb
        da, dgl, dcw, dcb = _glu_conv_bwd(s["proj"], conv_w_full[l], dy1, n_batch, seq, f"glu_conv_bwd_l{l}")
        sm["conv_dw_w"], sm["conv_dw_b"] = dcw, dcb
        dproj = jnp.concatenate([da, dgl, du, dgc, dgp], axis=1)
        dh = _matmul(dproj, wl["w_in"], "nn", f"d_h_l{l}")
        big[l]["w_in"] = _matmul(dproj, s["h"], "tn", f"d_w_in_l{l}", out_dtype=BF16)
        dx, dxb, dg = _rmsnorm_bwd(s["x0"], vec(mix_norm_g[l]), dh, dx, f"mix_norm_bwd_l{l}")
        sm["mix_norm_g"] = dg
        for k, val in sm.items():
            small[(l, k)] = val.reshape(-1)
        rs_after = rs_begin(l, PARTS[0][0])
    _, _, dg_mem = _rmsnorm_bwd(mem2d, vec(mem_norm_g), dmem_n, None, "mem_norm_bwd")
    small["mem_norm_g"] = dg_mem.reshape(-1)
    small["loss"] = loss_part.reshape(-1)

    grads = {}
    per_layer = {name: [None] * depth for name, _ in MATRICES}
    for l in reversed(range(depth)):
        for part, _ in reversed(PARTS):
            mat_grads = _reduce_scatter_end(rs_state[(l, part)], rs_after, f"rs{l}{part}")
            row0 = 0
            for name, tr, nrows in layout[part]:
                per_layer[name][l] = _from_rows(name, tr, mat_grads[row0:row0 + nrows], p[name].shape[1:])
                row0 += nrows
    for name, _ in MATRICES:
        grads[name] = jnp.stack(per_layer[name])

    keys = list(small.keys())
    flat = jnp.concatenate([small[k] for k in keys])
    n_small = flat.shape[0]
    rows_small = -(-n_small // (SUBLANE * d_model)) * SUBLANE
    flat = jnp.pad(flat, (0, rows_small * d_model - n_small)).reshape(rows_small, d_model)
    every = _all_gather(flat, "small_all_gather")
    total = _sum_rows([every[i] for i in range(N_DEV)], F32, "small_sum").reshape(-1)
    off = 0
    red = {}
    for k in keys:
        red[k] = total[off:off + small[k].shape[0]]
        off += small[k].shape[0]
    loss = red["loss"][0]
    for name in ("mix_norm_g", "conv_dw_b", "conv_ln_g", "conv_ln_b", "pool_scale", "xattn_norm_g", "ffn_norm_g"):
        grads[name] = jnp.stack([red[(l, name)] for l in range(depth)])
    grads["conv_dw_w"] = jnp.stack([
        lax.dynamic_slice_in_dim(red[(l, "conv_dw_w")].reshape(kc, N_DEV * cs), dev * cs, cs, axis=1)
        for l in range(depth)])
    grads["ffn_dw_w"] = jnp.stack([
        lax.dynamic_slice_in_dim(red[(l, "ffn_dw_w")].reshape(kf, N_DEV * fs), dev * fs, fs, axis=1)
        for l in range(depth)])
    grads["mem_norm_g"] = red["mem_norm_g"]
    grads["final_norm_g"] = red["final_norm_g"]

    deltas, new_m, new_v = {}, {}, {}
    for name in weight_names:
        deltas[name], new_m[name], new_v[name] = _adamw(p[name], grads[name], p["m_" + name], p["v_" + name],
                                                        f"adamw_{name}")
    grad_x = dx.reshape(n_batch, seq, d_model)
    return (loss, grad_x, *[grads[n] for n in weight_names], *[deltas[n] for n in weight_names],
            *[new_m[n] for n in weight_names], *[new_v[n] for n in weight_names])
```

```python
import functools

import jax
import jax.numpy as jnp
from jax import lax
from jax.experimental import pallas as pl
from jax.experimental.pallas import tpu as pltpu

F32 = jnp.float32
BF16 = jnp.bfloat16
MESH = pl.DeviceIdType.MESH

N_DEV = 8
EPS = 1e-6
V7X_VMEM_BYTES = 64 * 1024 * 1024
VMEM_LIMIT = (V7X_VMEM_BYTES * 3) // 4
LANE = 128
SUBLANE = 8

CONV_HALO = 32
POOL_HALO = 16
FFN_HALO = 8
POOL_WINDOW_MAX = 16
XA_HEADS = 4

ADAM_LR = 0.001
ADAM_B1 = 0.9
ADAM_B2 = 0.999
ADAM_EPS = 1e-08
ADAM_WD = 0.01
ADAM_STEP = 10

GELU_C0 = 0.7978845608028654
GELU_C1 = 0.044715


ANY_SPEC = pl.BlockSpec(memory_space=pl.ANY)


def _tile(n, cap, mult=LANE):
    if n <= cap:
        return n
    best = None
    for d in range(mult, cap + 1, mult):
        if n % d == 0:
            best = d
    assert best is not None, (n, cap, mult)
    return best


def _params(*sem):
    return pltpu.CompilerParams(dimension_semantics=sem, vmem_limit_bytes=VMEM_LIMIT)


def _sig(x):
    return 1.0 / (1.0 + jnp.exp(-x))


def _bs(shape, imap):
    return pl.BlockSpec(shape, imap)


def _matmul(a, b, mode, name, res=None, out_dtype=F32, after=None):
    if mode == "tn":
        k_dim, m_dim = a.shape
        k2, n_dim = b.shape
    elif mode == "nn":
        m_dim, k_dim = a.shape
        k2, n_dim = b.shape
    else:
        m_dim, k_dim = a.shape
        n_dim, k2 = b.shape
    assert k_dim == k2, (name, a.shape, b.shape)
    tm = _tile(m_dim, 1408 if mode == "tn" else 1024)
    tn = _tile(n_dim, 1408)
    wide = a.dtype == F32 or b.dtype == F32
    tk = _tile(k_dim, 1024 if wide else 2048)
    nk = k_dim // tk
    use_acc = nk > 1 and out_dtype != F32
    if mode == "tn":
        a_spec, ca = _bs((tk, tm), lambda i, j, k: (k, i)), 0
    else:
        a_spec, ca = _bs((tm, tk), lambda i, j, k: (i, k)), 1
    if mode == "nt":
        b_spec, cb = _bs((tn, tk), lambda i, j, k: (j, k)), 1
    else:
        b_spec, cb = _bs((tk, tn), lambda i, j, k: (k, j)), 0
    dims = (((ca,), (cb,)), ((), ()))
    o_spec = _bs((tm, tn), lambda i, j, k: (i, j))
    has_res = res is not None

    def body(*refs):
        a_ref, b_ref = refs[:2]
        r_ref = refs[2] if has_res else None
        o_ref = refs[n_in]
        k = pl.program_id(2)
        part = lax.dot_general(a_ref[...].astype(BF16), b_ref[...].astype(BF16), dims,
                               preferred_element_type=F32)
        if nk == 1:
            if has_res:
                part = part + r_ref[...].astype(F32)
            o_ref[...] = part.astype(out_dtype)
            return
        acc = refs[-1] if use_acc else o_ref

        @pl.when(k == 0)
        def _():
            acc[...] = part + r_ref[...].astype(F32) if has_res else part

        @pl.when(k > 0)
        def _():
            acc[...] += part

        if use_acc:
            @pl.when(k == nk - 1)
            def _():
                o_ref[...] = acc[...].astype(out_dtype)

    in_specs = [a_spec, b_spec] + ([o_spec] if has_res else [])
    args = (a, b) + ((res,) if has_res else ())
    if after is not None:
        in_specs.append(ANY_SPEC)
        args += (after,)
    n_in = len(args)
    return pl.pallas_call(
        body, out_shape=jax.ShapeDtypeStruct((m_dim, n_dim), out_dtype),
        grid=(m_dim // tm, n_dim // tn, nk), in_specs=in_specs, out_specs=o_spec,
        scratch_shapes=[pltpu.VMEM((tm, tn), F32)] if use_acc else [], name=name,
        compiler_params=_params("parallel", "parallel", "arbitrary"))(*args)


def _grouped(a, w, mode, name, out_dtype=F32):
    t_dim = a.shape[0]
    g_dim, r_dim, c_dim = w.shape
    ka, no = (c_dim, r_dim) if mode == "nt" else (r_dim, c_dim)
    tm = _tile(t_dim, 512)
    dims = (((1,), (1 if mode == "nt" else 0,)), ((), ()))

    def body(a_ref, w_ref, o_ref):
        o_ref[...] = lax.dot_general(a_ref[...].astype(BF16), w_ref[...].astype(BF16), dims,
                                     preferred_element_type=F32).astype(out_dtype)

    return pl.pallas_call(
        body, out_shape=jax.ShapeDtypeStruct((t_dim, g_dim * no), out_dtype),
        grid=(t_dim // tm, g_dim),
        in_specs=[_bs((tm, ka), lambda i, g: (i, g)), _bs((None, r_dim, c_dim), lambda i, g: (g, 0, 0))],
        out_specs=_bs((tm, no), lambda i, g: (i, g)), name=name,
        compiler_params=_params("parallel", "parallel"))(a, w)


def _grouped_tn(a, b, g_dim, name):
    t_dim = a.shape[0]
    ra = a.shape[1] // g_dim
    cb = b.shape[1] // g_dim
    tm = _tile(t_dim, 512)
    nt = t_dim // tm

    def body(a_ref, b_ref, o_ref):
        part = lax.dot_general(a_ref[...].astype(BF16), b_ref[...].astype(BF16), (((0,), (0,)), ((), ())),
                               preferred_element_type=F32)

        @pl.when(pl.program_id(1) == 0)
        def _():
            o_ref[...] = part

        @pl.when(pl.program_id(1) > 0)
        def _():
            o_ref[...] += part

    return pl.pallas_call(
        body, out_shape=jax.ShapeDtypeStruct((g_dim, ra, cb), F32), grid=(g_dim, nt),
        in_specs=[_bs((tm, ra), lambda g, i: (i, g)), _bs((tm, cb), lambda g, i: (i, g))],
        out_specs=_bs((None, ra, cb), lambda g, i: (g, 0, 0)), name=name,
        compiler_params=_params("parallel", "arbitrary"))(a, b)


def _rmsnorm_fwd(x, g, name):
    t_dim, d = x.shape
    tm = _tile(t_dim, 512)

    def body(x_ref, g_ref, o_ref):
        xv = x_ref[...]
        r = lax.rsqrt(jnp.mean(xv * xv, axis=-1, keepdims=True) + EPS)
        o_ref[...] = (xv * r * g_ref[...]).astype(BF16)

    return pl.pallas_call(
        body, out_shape=jax.ShapeDtypeStruct((t_dim, d), BF16), grid=(t_dim // tm,),
        in_specs=[_bs((tm, d), lambda i: (i, 0)), _bs((1, d), lambda i: (0, 0))],
        out_specs=_bs((tm, d), lambda i: (i, 0)), name=name, compiler_params=_params("parallel"))(x, g)


def _rmsnorm_bwd(x, g, dh, dx_in, name):
    t_dim, d = x.shape
    tm = _tile(t_dim, 512)
    has_in = dx_in is not None

    def body(*refs):
        if has_in:
            x_ref, g_ref, dh_ref, di_ref, dx_ref, dxb_ref, dg_ref = refs
        else:
            x_ref, g_ref, dh_ref, dx_ref, dxb_ref, dg_ref = refs
        xv = x_ref[...]
        r = lax.rsqrt(jnp.mean(xv * xv, axis=-1, keepdims=True) + EPS)
        xh = xv * r
        dhv = dh_ref[...].astype(F32)
        dxh = dhv * g_ref[...]
        dx = r * (dxh - xh * jnp.mean(dxh * xh, axis=-1, keepdims=True))
        if has_in:
            dx = dx + di_ref[...]
        dx_ref[...] = dx
        dxb_ref[...] = dx.astype(BF16)
        part = jnp.sum(dhv * xh, axis=0, keepdims=True)

        @pl.when(pl.program_id(0) == 0)
        def _():
            dg_ref[...] = part

        @pl.when(pl.program_id(0) > 0)
        def _():
            dg_ref[...] += part

    row = _bs((tm, d), lambda i: (i, 0))
    vec = _bs((1, d), lambda i: (0, 0))
    args = (x, g, dh) + ((dx_in,) if has_in else ())
    return pl.pallas_call(
        body, out_shape=(jax.ShapeDtypeStruct((t_dim, d), F32), jax.ShapeDtypeStruct((t_dim, d), BF16),
                         jax.ShapeDtypeStruct((1, d), F32)),
        grid=(t_dim // tm,), in_specs=[row, vec, row] + ([row] if has_in else []),
        out_specs=(row, row, vec), name=name, compiler_params=_params("arbitrary"))(*args)


def _loss_head(x, g, tgt, name):
    t_dim, d = x.shape
    tm = _tile(t_dim, 512)

    def body(x_ref, g_ref, t_ref, dx_ref, dxb_ref, dg_ref, loss_ref):
        xv = x_ref[...]
        gv = g_ref[...]
        r = lax.rsqrt(jnp.mean(xv * xv, axis=-1, keepdims=True) + EPS)
        xh = xv * r
        err = xh * gv - t_ref[...]
        dy = err * (1.0 / d)
        dxh = dy * gv
        dx = r * (dxh - xh * jnp.mean(dxh * xh, axis=-1, keepdims=True))
        dx_ref[...] = dx
        dxb_ref[...] = dx.astype(BF16)
        dg_part = jnp.sum(dy * xh, axis=0, keepdims=True)
        loss_part = jnp.full((1, LANE), 0.5 * jnp.sum(jnp.mean(err * err, axis=-1, keepdims=True)), F32)

        @pl.when(pl.program_id(0) == 0)
        def _():
            dg_ref[...] = dg_part
            loss_ref[...] = loss_part

        @pl.when(pl.program_id(0) > 0)
        def _():
            dg_ref[...] += dg_part
            loss_ref[...] += loss_part

    row = _bs((tm, d), lambda i: (i, 0))
    vec = _bs((1, d), lambda i: (0, 0))
    return pl.pallas_call(
        body, out_shape=(jax.ShapeDtypeStruct((t_dim, d), F32), jax.ShapeDtypeStruct((t_dim, d), BF16),
                         jax.ShapeDtypeStruct((1, d), F32), jax.ShapeDtypeStruct((1, LANE), F32)),
        grid=(t_dim // tm,), in_specs=[row, vec, row],
        out_specs=(row, row, vec, _bs((1, LANE), lambda i: (0, 0))), name=name,
        compiler_params=_params("arbitrary"))(x, g, tgt)


def _glu_conv_fwd(proj, dw_w, dw_b, n_batch, seq, name):
    kk, cc = dw_w.shape
    nj = cc // LANE
    ch = min(256, seq)

    def body(a_ref, gl_ref, w_ref, b_ref, o_ref, pad):
        pad[0:CONV_HALO, :] = jnp.zeros((CONV_HALO, LANE), F32)
        pad[CONV_HALO:CONV_HALO + seq, :] = a_ref[...].astype(F32) * _sig(gl_ref[...].astype(F32))
        for c0 in range(0, seq, ch):
            acc = jnp.broadcast_to(b_ref[...], (ch, LANE))
            for k in range(kk):
                acc = acc + w_ref[k:k + 1, :] * pad[pl.ds(c0 + CONV_HALO - (kk - 1) + k, ch), :]
            o_ref[c0:c0 + ch, :] = acc

    return pl.pallas_call(
        body, out_shape=jax.ShapeDtypeStruct((n_batch * seq, cc), F32), grid=(n_batch, nj),
        in_specs=[_bs((seq, LANE), lambda b, j: (b, j)), _bs((seq, LANE), lambda b, j: (b, nj + j)),
                  _bs((kk, LANE), lambda b, j: (0, j)), _bs((1, LANE), lambda b, j: (0, j))],
        out_specs=_bs((seq, LANE), lambda b, j: (b, j)),
        scratch_shapes=[pltpu.VMEM((seq + CONV_HALO, LANE), F32)], name=name,
        compiler_params=_params("parallel", "parallel"))(proj, proj, dw_w, dw_b)


def _glu_conv_bwd(proj, dw_w, dy1, n_batch, seq, name):
    kk, cc = dw_w.shape
    nj = cc // LANE
    ch = min(256, seq)

    def body(a_ref, gl_ref, dy_ref, w_ref, da_ref, dgl_ref, dw_ref, db_ref, padf, padb):
        first = pl.program_id(1) == 0
        padf[0:CONV_HALO, :] = jnp.zeros((CONV_HALO, LANE), F32)
        padf[CONV_HALO:CONV_HALO + seq, :] = a_ref[...].astype(F32) * _sig(gl_ref[...].astype(F32))
        padb[0:seq, :] = dy_ref[...]
        padb[seq:seq + CONV_HALO, :] = jnp.zeros((CONV_HALO, LANE), F32)

        @pl.when(first)
        def _():
            dw_ref[...] = jnp.zeros((kk, LANE), F32)
            db_ref[...] = jnp.zeros((1, LANE), F32)

        for c0 in range(0, seq, ch):
            acc = jnp.zeros((ch, LANE), F32)
            for k in range(kk):
                acc = acc + w_ref[k:k + 1, :] * padb[pl.ds(c0 + (kk - 1) - k, ch), :]
            sg = _sig(gl_ref[c0:c0 + ch, :].astype(F32))
            da_ref[c0:c0 + ch, :] = (acc * sg).astype(BF16)
            dgl_ref[c0:c0 + ch, :] = (acc * a_ref[c0:c0 + ch, :].astype(F32) * sg * (1.0 - sg)).astype(BF16)
        for k in range(kk):
            s = jnp.zeros((1, LANE), F32)
            for c0 in range(0, seq, ch):
                s = s + jnp.sum(padb[c0:c0 + ch, :] * padf[pl.ds(c0 + CONV_HALO - (kk - 1) + k, ch), :],
                                axis=0, keepdims=True)
            dw_ref[k:k + 1, :] += s
        db_ref[...] += jnp.sum(dy_ref[...], axis=0, keepdims=True)

    tok = _bs((seq, LANE), lambda j, b: (b, j))
    t_dim = n_batch * seq
    return pl.pallas_call(
        body, out_shape=(jax.ShapeDtypeStruct((t_dim, cc), BF16), jax.ShapeDtypeStruct((t_dim, cc), BF16),
                         jax.ShapeDtypeStruct((kk, cc), F32), jax.ShapeDtypeStruct((1, cc), F32)),
        grid=(nj, n_batch),
        in_specs=[tok, _bs((seq, LANE), lambda j, b: (b, nj + j)), tok, _bs((kk, LANE), lambda j, b: (0, j))],
        out_specs=(tok, tok, _bs((kk, LANE), lambda j, b: (0, j)), _bs((1, LANE), lambda j, b: (0, j))),
        scratch_shapes=[pltpu.VMEM((seq + CONV_HALO, LANE), F32), pltpu.VMEM((seq + CONV_HALO, LANE), F32)],
        name=name, compiler_params=_params("parallel", "arbitrary"))(proj, proj, dy1, dw_w)


def _ln_silu_fwd(y1, g, b, name):
    t_dim, c = y1.shape
    tm = _tile(t_dim, 512)

    def body(y_ref, g_ref, b_ref, o_ref):
        yv = y_ref[...]
        xc = yv - jnp.mean(yv, axis=-1, keepdims=True)
        rstd = lax.rsqrt(jnp.mean(xc * xc, axis=-1, keepdims=True) + EPS)
        y2 = xc * rstd * g_ref[...] + b_ref[...]
        o_ref[...] = (y2 * _sig(y2)).astype(BF16)

    row = _bs((tm, c), lambda i: (i, 0))
    vec = _bs((1, c), lambda i: (0, 0))
    return pl.pallas_call(
        body, out_shape=jax.ShapeDtypeStruct((t_dim, c), BF16), grid=(t_dim // tm,),
        in_specs=[row, vec, vec], out_specs=row, name=name, compiler_params=_params("parallel"))(y1, g, b)


def _ln_silu_bwd(y1, g, b, dy3, name):
    t_dim, c = y1.shape
    tm = _tile(t_dim, 512)

    def body(y_ref, g_ref, b_ref, d_ref, dy_ref, dg_ref, db_ref):
        yv = y_ref[...]
        gv = g_ref[...]
        xc = yv - jnp.mean(yv, axis=-1, keepdims=True)
        rstd = lax.rsqrt(jnp.mean(xc * xc, axis=-1, keepdims=True) + EPS)
        yh = xc * rstd
        y2 = yh * gv + b_ref[...]
        s = _sig(y2)
        dy2 = d_ref[...].astype(F32) * (s * (1.0 + y2 * (1.0 - s)))
        dyh = dy2 * gv
        dy_ref[...] = rstd * (dyh - jnp.mean(dyh, axis=-1, keepdims=True)
                              - yh * jnp.mean(dyh * yh, axis=-1, keepdims=True))
        dg_part = jnp.sum(dy2 * yh, axis=0, keepdims=True)
        db_part = jnp.sum(dy2, axis=0, keepdims=True)

        @pl.when(pl.program_id(0) == 0)
        def _():
            dg_ref[...] = dg_part
            db_ref[...] = db_part

        @pl.when(pl.program_id(0) > 0)
        def _():
            dg_ref[...] += dg_part
            db_ref[...] += db_part

    row = _bs((tm, c), lambda i: (i, 0))
    vec = _bs((1, c), lambda i: (0, 0))
    return pl.pallas_call(
        body, out_shape=(jax.ShapeDtypeStruct((t_dim, c), F32), jax.ShapeDtypeStruct((1, c), F32),
                         jax.ShapeDtypeStruct((1, c), F32)),
        grid=(t_dim // tm,), in_specs=[row, vec, vec, row], out_specs=(row, vec, vec), name=name,
        compiler_params=_params("arbitrary"))(y1, g, b, dy3)


def _pool_fwd(proj, col0, n_groups, n_batch, seq, name):
    ch = min(256, seq)

    def body(u_ref, o_ref, pad):
        w = lax.shift_left(jnp.int32(2), pl.program_id(1))
        pad[0:POOL_HALO, :] = jnp.zeros((POOL_HALO, LANE), F32)
        pad[POOL_HALO:POOL_HALO + seq, :] = u_ref[...].astype(F32)
        for c0 in range(0, seq, ch):
            acc = jnp.zeros((ch, LANE), F32)
            for j in range(POOL_WINDOW_MAX):
                acc = acc + jnp.where(j < w, 1.0, 0.0).astype(F32) * pad[pl.ds(c0 + POOL_HALO - j, ch), :]
            t = c0 + lax.broadcasted_iota(jnp.int32, (ch, LANE), 0)
            cnt = jnp.minimum(t + 1, w).astype(F32)
            o_ref[c0:c0 + ch, :] = (acc / cnt - pad[POOL_HALO + c0:POOL_HALO + c0 + ch, :]).astype(BF16)

    return pl.pallas_call(
        body, out_shape=jax.ShapeDtypeStruct((n_batch * seq, n_groups * LANE), BF16), grid=(n_batch, n_groups),
        in_specs=[_bs((seq, LANE), lambda b, g: (b, col0 + g))], out_specs=_bs((seq, LANE), lambda b, g: (b, g)),
        scratch_shapes=[pltpu.VMEM((seq + POOL_HALO, LANE), F32)], name=name,
        compiler_params=_params("parallel", "parallel"))(proj)


def _pool_bwd(dzp, n_groups, n_batch, seq, name):
    ch = min(256, seq)

    def body(d_ref, o_ref, pad):
        w = lax.shift_left(jnp.int32(2), pl.program_id(1))
        for c0 in range(0, seq, ch):
            t = c0 + lax.broadcasted_iota(jnp.int32, (ch, LANE), 0)
            cnt = jnp.minimum(t + 1, w).astype(F32)
            pad[c0:c0 + ch, :] = d_ref[c0:c0 + ch, :] / cnt
        pad[seq:seq + POOL_HALO, :] = jnp.zeros((POOL_HALO, LANE), F32)
        for c0 in range(0, seq, ch):
            acc = jnp.zeros((ch, LANE), F32)
            for j in range(POOL_WINDOW_MAX):
                acc = acc + jnp.where(j < w, 1.0, 0.0).astype(F32) * pad[pl.ds(c0 + j, ch), :]
            o_ref[c0:c0 + ch, :] = (acc - d_ref[c0:c0 + ch, :]).astype(BF16)

    tok = _bs((seq, LANE), lambda b, g: (b, g))
    return pl.pallas_call(
        body, out_shape=jax.ShapeDtypeStruct((n_batch * seq, n_groups * LANE), BF16), grid=(n_batch, n_groups),
        in_specs=[tok], out_specs=tok, scratch_shapes=[pltpu.VMEM((seq + POOL_HALO, LANE), F32)], name=name,
        compiler_params=_params("parallel", "parallel"))(dzp)


def _merge_fwd(proj, col0, yc, yp, scale, name):
    t_dim, d = yc.shape
    half = d // 2
    tm = _tile(t_dim, 512)
    c0 = col0 // half

    def body(gc_ref, gp_ref, yc_ref, yp_ref, s_ref, o_ref):
        f32 = lambda r: r[...].astype(F32)
        o_ref[...] = (_sig(f32(gc_ref)) * f32(yc_ref) + _sig(f32(gp_ref)) * (f32(yp_ref) * s_ref[...])).astype(BF16)

    blk = _bs((tm, half), lambda i, j: (i, j))
    return pl.pallas_call(
        body, out_shape=jax.ShapeDtypeStruct((t_dim, d), BF16), grid=(t_dim // tm, 2),
        in_specs=[_bs((tm, half), lambda i, j: (i, c0 + j)), _bs((tm, half), lambda i, j: (i, c0 + 2 + j)),
                  blk, blk, _bs((1, half), lambda i, j: (0, j))],
        out_specs=blk, name=name, compiler_params=_params("parallel", "parallel"))(proj, proj, yc, yp, scale)


def _merge_bwd(proj, col0, yc, yp, scale, dm, name):
    t_dim, d = yc.shape
    half = d // 2
    tm = _tile(t_dim, 512)
    c0 = col0 // half

    def body(gc_ref, gp_ref, yc_ref, yp_ref, s_ref, dm_ref, dgc_ref, dgp_ref, dyc_ref, dyp_ref, ds_ref):
        dmv = dm_ref[...].astype(F32)
        sgc = _sig(gc_ref[...].astype(F32))
        sgp = _sig(gp_ref[...].astype(F32))
        sv = s_ref[...]
        ypre = yp_ref[...].astype(F32)
        dgc_ref[...] = (dmv * yc_ref[...].astype(F32) * sgc * (1.0 - sgc)).astype(BF16)
        dgp_ref[...] = (dmv * (ypre * sv) * sgp * (1.0 - sgp)).astype(BF16)
        dyc_ref[...] = (dmv * sgc).astype(BF16)
        dyp = dmv * sgp
        dyp_ref[...] = (dyp * sv).astype(BF16)
        part = jnp.sum(dyp * ypre, axis=0, keepdims=True)

        @pl.when(pl.program_id(1) == 0)
        def _():
            ds_ref[...] = part

        @pl.when(pl.program_id(1) > 0)
        def _():
            ds_ref[...] += part

    blk = _bs((tm, half), lambda j, i: (i, j))
    big = jax.ShapeDtypeStruct((t_dim, d), BF16)
    return pl.pallas_call(
        body, out_shape=(big, big, big, big, jax.ShapeDtypeStruct((1, d), F32)), grid=(2, t_dim // tm),
        in_specs=[_bs((tm, half), lambda j, i: (i, c0 + j)), _bs((tm, half), lambda j, i: (i, c0 + 2 + j)),
                  blk, blk, _bs((1, half), lambda j, i: (0, j)), blk],
        out_specs=(blk, blk, blk, blk, _bs((1, half), lambda j, i: (0, j))), name=name,
        compiler_params=_params("parallel", "arbitrary"))(proj, proj, yc, yp, scale, dm)


def _attn_fwd(q, kv, n_batch, seq, m_len, name):
    d = q.shape[1]
    hd = d // XA_HEADS
    tq = _tile(seq, 1024)
    nq = seq // tq
    scale = hd ** -0.5

    def body(q_ref, k_ref, v_ref, o_ref):
        sc = lax.dot_general(q_ref[...].astype(BF16), k_ref[...].astype(BF16), (((1,), (1,)), ((), ())),
                             preferred_element_type=F32) * scale
        p = jnp.exp(sc - jnp.max(sc, axis=-1, keepdims=True))
        pr = p / jnp.sum(p, axis=-1, keepdims=True)
        o_ref[...] = jnp.dot(pr.astype(BF16), v_ref[...].astype(BF16), preferred_element_type=F32).astype(BF16)

    return pl.pallas_call(
        body, out_shape=jax.ShapeDtypeStruct((n_batch * seq, d), BF16), grid=(n_batch, XA_HEADS, nq),
        in_specs=[_bs((tq, hd), lambda b, h, i: (b * nq + i, h)), _bs((m_len, hd), lambda b, h, i: (b, h)),
                  _bs((m_len, hd), lambda b, h, i: (b, XA_HEADS + h))],
        out_specs=_bs((tq, hd), lambda b, h, i: (b * nq + i, h)), name=name,
        compiler_params=_params("parallel", "parallel", "parallel"))(q, kv, kv)


def _attn_bwd(q, kv, datt, n_batch, seq, m_len, name):
    d = q.shape[1]
    hd = d // XA_HEADS
    tq = _tile(seq, 1024)
    nq = seq // tq
    scale = hd ** -0.5

    def body(q_ref, k_ref, v_ref, do_ref, dq_ref, dk_ref, dv_ref):
        qb = q_ref[...].astype(BF16)
        kb = k_ref[...].astype(BF16)
        vb = v_ref[...].astype(BF16)
        dob = do_ref[...].astype(BF16)
        sc = lax.dot_general(qb, kb, (((1,), (1,)), ((), ())), preferred_element_type=F32) * scale
        p = jnp.exp(sc - jnp.max(sc, axis=-1, keepdims=True))
        pr = p / jnp.sum(p, axis=-1, keepdims=True)
        dpr = lax.dot_general(dob, vb, (((1,), (1,)), ((), ())), preferred_element_type=F32)
        dsc = pr * (dpr - jnp.sum(dpr * pr, axis=-1, keepdims=True)) * scale
        dsb = dsc.astype(BF16)
        dq_ref[...] = jnp.dot(dsb, kb, preferred_element_type=F32).astype(BF16)
        dv_part = lax.dot_general(pr.astype(BF16), dob, (((0,), (0,)), ((), ())), preferred_element_type=F32)
        dk_part = lax.dot_general(dsb, qb, (((0,), (0,)), ((), ())), preferred_element_type=F32)

        @pl.when(pl.program_id(2) == 0)
        def _():
            dk_ref[...] = dk_part
            dv_ref[...] = dv_part

        @pl.when(pl.program_id(2) > 0)
        def _():
            dk_ref[...] += dk_part
            dv_ref[...] += dv_part

    qs = _bs((tq, hd), lambda b, h, i: (b * nq + i, h))
    ks = _bs((m_len, hd), lambda b, h, i: (b, h))
    return pl.pallas_call(
        body, out_shape=(jax.ShapeDtypeStruct((n_batch * seq, d), BF16), jax.ShapeDtypeStruct((n_batch * m_len, d), F32),
                         jax.ShapeDtypeStruct((n_batch * m_len, d), F32)),
        grid=(n_batch, XA_HEADS, nq),
        in_specs=[qs, ks, _bs((m_len, hd), lambda b, h, i: (b, XA_HEADS + h)), qs],
        out_specs=(qs, ks, ks), name=name,
        compiler_params=_params("parallel", "parallel", "arbitrary"))(q, kv, kv, datt)


def _gelu_parts(g):
    th = jnp.tanh(GELU_C0 * (g + GELU_C1 * g * g * g))
    return th, 0.5 * g * (1.0 + th)


def _ffn_act_fwd(up_g, up_v, dw_w, n_batch, seq, name):
    kk, c2 = dw_w.shape
    f_dim = c2 // 2
    wd = 2 * LANE
    nj = f_dim // wd
    ch = min(128, seq)

    def body(g_ref, v_ref, wg_ref, wv_ref, o_ref, padg, padv):
        for pad, src in ((padg, g_ref), (padv, v_ref)):
            pad[0:FFN_HALO, :] = jnp.zeros((FFN_HALO, wd), F32)
            pad[FFN_HALO:FFN_HALO + seq, :] = src[...].astype(F32)
        for c0 in range(0, seq, ch):
            gate = jnp.zeros((ch, wd), F32)
            val = jnp.zeros((ch, wd), F32)
            for k in range(kk):
                off = c0 + FFN_HALO - (kk - 1) + k
                gate = gate + wg_ref[k:k + 1, :] * padg[pl.ds(off, ch), :]
                val = val + wv_ref[k:k + 1, :] * padv[pl.ds(off, ch), :]
            o_ref[c0:c0 + ch, :] = (_gelu_parts(gate)[1] * val).astype(BF16)

    return pl.pallas_call(
        body, out_shape=jax.ShapeDtypeStruct((n_batch * seq, f_dim), BF16), grid=(n_batch, nj),
        in_specs=[_bs((seq, wd), lambda b, j: (b, j)), _bs((seq, wd), lambda b, j: (b, j)),
                  _bs((kk, wd), lambda b, j: (0, j)), _bs((kk, wd), lambda b, j: (0, nj + j))],
        out_specs=_bs((seq, wd), lambda b, j: (b, j)),
        scratch_shapes=[pltpu.VMEM((seq + FFN_HALO, wd), F32), pltpu.VMEM((seq + FFN_HALO, wd), F32)], name=name,
        compiler_params=_params("parallel", "parallel"))(up_g, up_v, dw_w, dw_w)


def _ffn_act_bwd(up_g, up_v, dw_w, dact, n_batch, seq, name):
    kk, c2 = dw_w.shape
    f_dim = c2 // 2
    wd = 2 * LANE
    nj = f_dim // wd
    ch = min(128, seq)

    def body(g_ref, v_ref, wg_ref, wv_ref, da_ref, dg_ref, dv_ref, dwg_ref, dwv_ref, padg, padv, pbg, pbv):
        for pad, src in ((padg, g_ref), (padv, v_ref)):
            pad[0:FFN_HALO, :] = jnp.zeros((FFN_HALO, wd), F32)
            pad[FFN_HALO:FFN_HALO + seq, :] = src[...].astype(F32)
        for pb in (pbg, pbv):
            pb[seq:seq + FFN_HALO, :] = jnp.zeros((FFN_HALO, wd), F32)

        @pl.when(pl.program_id(1) == 0)
        def _():
            dwg_ref[...] = jnp.zeros((kk, wd), F32)
            dwv_ref[...] = jnp.zeros((kk, wd), F32)

        for c0 in range(0, seq, ch):
            gate = jnp.zeros((ch, wd), F32)
            val = jnp.zeros((ch, wd), F32)
            for k in range(kk):
                off = c0 + FFN_HALO - (kk - 1) + k
                gate = gate + wg_ref[k:k + 1, :] * padg[pl.ds(off, ch), :]
                val = val + wv_ref[k:k + 1, :] * padv[pl.ds(off, ch), :]
            th, gelu = _gelu_parts(gate)
            dgelu = 0.5 * (1.0 + th) + 0.5 * gate * (1.0 - th * th) * GELU_C0 * (1.0 + 3.0 * GELU_C1 * gate * gate)
            dav = da_ref[c0:c0 + ch, :].astype(F32)
            pbg[c0:c0 + ch, :] = dav * val * dgelu
            pbv[c0:c0 + ch, :] = dav * gelu
        for pb, pad, w_ref, d_ref, dw_ref in ((pbg, padg, wg_ref, dg_ref, dwg_ref), (pbv, padv, wv_ref, dv_ref, dwv_ref)):
            for c0 in range(0, seq, ch):
                acc = jnp.zeros((ch, wd), F32)
                for k in range(kk):
                    acc = acc + w_ref[k:k + 1, :] * pb[pl.ds(c0 + (kk - 1) - k, ch), :]
                d_ref[c0:c0 + ch, :] = acc.astype(BF16)
            for k in range(kk):
                s = jnp.zeros((1, wd), F32)
                for c0 in range(0, seq, ch):
                    s = s + jnp.sum(pb[c0:c0 + ch, :] * pad[pl.ds(c0 + FFN_HALO - (kk - 1) + k, ch), :],
                                    axis=0, keepdims=True)
                dw_ref[k:k + 1, :] += s

    t_dim = n_batch * seq
    tok = _bs((seq, wd), lambda j, b: (b, j))
    wblk = _bs((kk, wd), lambda j, b: (0, j))
    pad_shape = pltpu.VMEM((seq + FFN_HALO, wd), F32)
    return pl.pallas_call(
        body, out_shape=(jax.ShapeDtypeStruct((t_dim, f_dim), BF16), jax.ShapeDtypeStruct((t_dim, f_dim), BF16),
                         jax.ShapeDtypeStruct((kk, f_dim), F32), jax.ShapeDtypeStruct((kk, f_dim), F32)),
        grid=(nj, n_batch),
        in_specs=[tok, tok, wblk, _bs((kk, wd), lambda j, b: (0, nj + j)), tok],
        out_specs=(tok, tok, wblk, wblk), scratch_shapes=[pad_shape, pad_shape, pad_shape, pad_shape], name=name,
        compiler_params=_params("parallel", "arbitrary"))(up_g, up_v, dw_w, dw_w, dact)


def _sum_rows(parts, out_dtype, name):
    r_dim, c_dim = parts[0].shape
    tr = _tile(r_dim, 1200, SUBLANE)
    n = len(parts)

    def body(*refs):
        acc = refs[0][...].astype(F32)
        for r in refs[1:n]:
            acc = acc + r[...].astype(F32)
        refs[n][...] = acc.astype(out_dtype)

    blk = _bs((tr, c_dim), lambda i: (i, 0))
    return pl.pallas_call(
        body, out_shape=jax.ShapeDtypeStruct((r_dim, c_dim), out_dtype), grid=(r_dim // tr,),
        in_specs=[blk] * n, out_specs=blk, name=name, compiler_params=_params("parallel"))(*parts)


def _adamw(w, g, m, v, name):
    shape = w.shape
    c_dim = shape[-1]
    r_dim = w.size // c_dim
    two_d = lambda t: t.reshape(r_dim, c_dim)
    tr = _tile(r_dim, max(SUBLANE, (256 * 1024) // max(c_dim, LANE) // SUBLANE * SUBLANE), SUBLANE)
    c1 = 1.0 - ADAM_B1 ** ADAM_STEP
    c2 = 1.0 - ADAM_B2 ** ADAM_STEP

    def body(w_ref, g_ref, m_ref, v_ref, d_ref, mo_ref, vo_ref):
        gv = g_ref[...]
        mn = ADAM_B1 * m_ref[...] + (1.0 - ADAM_B1) * gv
        vn = ADAM_B2 * v_ref[...] + (1.0 - ADAM_B2) * (gv * gv)
        mo_ref[...] = mn
        vo_ref[...] = vn
        d_ref[...] = -ADAM_LR * ((mn / c1) / (jnp.sqrt(vn / c2) + ADAM_EPS) + ADAM_WD * w_ref[...])

    blk = _bs((tr, c_dim), lambda i: (i, 0))
    out = jax.ShapeDtypeStruct((r_dim, c_dim), F32)
    d, mo, vo = pl.pallas_call(
        body, out_shape=(out, out, out), grid=(r_dim // tr,), in_specs=[blk] * 4, out_specs=(blk, blk, blk),
        name=name, compiler_params=_params("parallel"))(two_d(w), two_d(g), two_d(m), two_d(v))
    return d.reshape(shape), mo.reshape(shape), vo.reshape(shape)


HBM_SPEC = pl.BlockSpec(memory_space=pltpu.HBM)


def _position():
    return lax.axis_index("x"), lax.axis_index("y"), lax.axis_index("c")


def _all_gather(shard, name):
    def body(x_ref, out_ref, send_sems, recv_sems, local_sem):
        x, y, c = _position()
        me, sibling = (x, y, c), (x, y, 1 - c)
        chips = [(1 - x, y), (x, 1 - y), (1 - x, 1 - y)]

        def rows(px, py, pc):
            return out_ref.at[4 * px + 2 * py + pc]

        def copy(k, block, to, src=None):
            return pltpu.make_async_remote_copy(
                src_ref=rows(*block) if src is None else src, dst_ref=rows(*block),
                send_sem=send_sems.at[k], recv_sem=recv_sems.at[k], device_id=to, device_id_type=MESH)

        mine = pltpu.make_async_copy(x_ref, rows(*me), local_sem)
        mine.start()
        first = [copy(0, me, sibling, src=x_ref)]
        first += [copy(1 + j, me, (*chip, c), src=x_ref) for j, chip in enumerate(chips)]
        for cp in first:
            cp.start()
        passed = [copy(4 + j, (*chip, c), sibling) for j, chip in enumerate(chips)]
        for j, chip in enumerate(chips):
            copy(1 + j, (*chip, c), me).wait_recv()
            passed[j].start()
        copy(0, sibling, me).wait_recv()
        for j, chip in enumerate(chips):
            copy(4 + j, (*chip, 1 - c), me).wait_recv()
        for cp in first + passed:
            cp.wait_send()
        mine.wait()

    return pl.pallas_call(
        body, out_shape=jax.ShapeDtypeStruct((N_DEV,) + shard.shape, shard.dtype),
        in_specs=[HBM_SPEC], out_specs=HBM_SPEC,
        scratch_shapes=[pltpu.SemaphoreType.DMA((7,)), pltpu.SemaphoreType.DMA((7,)), pltpu.SemaphoreType.DMA(())],
        name=name)(shard)


CHIP_RELATIONS = ((0, 0), (1, 0), (0, 1), (1, 1))


def _rs_pair_exchange(g, name):
    _, r_dim, c_dim = g.shape
    n = len(CHIP_RELATIONS)

    def body(g_ref, recv_ref, send_sems, recv_sems):
        x, y, c = _position()
        sibling = (x, y, 1 - c)
        copies = []
        for k, (rx, ry) in enumerate(CHIP_RELATIONS):
            px = x + rx - 2 * x * rx
            py = y + ry - 2 * y * ry
            copies.append(pltpu.make_async_remote_copy(
                src_ref=g_ref.at[4 * px + 2 * py + 1 - c], dst_ref=recv_ref.at[k], send_sem=send_sems.at[k],
                recv_sem=recv_sems.at[k], device_id=sibling, device_id_type=MESH))
        for cp in copies:
            cp.start()
        for cp in copies:
            cp.wait()

    return pl.pallas_call(
        body, out_shape=jax.ShapeDtypeStruct((n, r_dim, c_dim), g.dtype), in_specs=[HBM_SPEC], out_specs=HBM_SPEC,
        scratch_shapes=[pltpu.SemaphoreType.DMA((n,)), pltpu.SemaphoreType.DMA((n,))], name=name)(g)


def _rs_pair_sum(g, recv, name):
    _, r_dim, c_dim = g.shape
    n = len(CHIP_RELATIONS)
    tr = _tile(r_dim, 1200, SUBLANE)
    x, y, c = _position()
    own = jnp.stack([4 * (x + rx - 2 * x * rx) + 2 * (y + ry - 2 * y * ry) + c for rx, ry in CHIP_RELATIONS])

    def body(own_ref, g_ref, r_ref, o_ref):
        o_ref[...] = (g_ref[...].astype(F32) + r_ref[...].astype(F32)).astype(o_ref.dtype)

    blk = _bs((None, tr, c_dim), lambda k, i, own_ref: (k, i, 0))
    return pl.pallas_call(
        body, out_shape=jax.ShapeDtypeStruct((n, r_dim, c_dim), g.dtype),
        grid_spec=pltpu.PrefetchScalarGridSpec(
            num_scalar_prefetch=1, grid=(n, r_dim // tr),
            in_specs=[_bs((None, tr, c_dim), lambda k, i, own_ref: (own_ref[k], i, 0)), blk], out_specs=blk),
        name=name, compiler_params=_params("parallel", "parallel"))(own.astype(jnp.int32), g, recv)


SEM_SPEC = pl.BlockSpec(memory_space=pltpu.SEMAPHORE)
DATAFLOW = pltpu.SideEffectType.DATAFLOW_SIDE_EFFECTING
CHIP_FLIPS = CHIP_RELATIONS[1:]
TOKEN = jax.ShapeDtypeStruct((SUBLANE, LANE), F32)


def _flip(v, r):
    return v + r - 2 * v * r


def _chip_copies(src_ref, src_of, dst_ref, dst_of, send_sems, recv_sems):
    x, y, c = _position()
    me = 4 * x + 2 * y + c
    out = []
    for k, (rx, ry) in enumerate(CHIP_FLIPS):
        px, py = _flip(x, rx), _flip(y, ry)
        peer = 4 * px + 2 * py + c
        out.append(pltpu.make_async_remote_copy(
            src_ref=src_ref.at[src_of(k, me, peer)], dst_ref=dst_ref.at[dst_of(k, me, peer)],
            send_sem=send_sems.at[k], recv_sem=recv_sems.at[k], device_id=(px, py, c), device_id_type=MESH))
    return out


def _ag_chips_start(land, after, name):
    def body(land_ref, after_ref, send_sems, recv_sems, land_thru, token):
        for cp in _chip_copies(land_ref, lambda k, me, peer: me, land_ref, lambda k, me, peer: me, send_sems, recv_sems):
            cp.start()
        token[...] = jnp.zeros(TOKEN.shape, TOKEN.dtype)

    n = len(CHIP_FLIPS)
    return pl.pallas_call(
        body, name=name,
        out_shape=(pltpu.SemaphoreType.DMA((n,)), pltpu.SemaphoreType.DMA((n,)), pltpu.HBM(land.shape, land.dtype), TOKEN),
        in_specs=(HBM_SPEC, ANY_SPEC), out_specs=(SEM_SPEC, SEM_SPEC, HBM_SPEC, pl.BlockSpec(memory_space=pltpu.VMEM)),
        input_output_aliases={0: 2}, compiler_params=pltpu.CompilerParams(has_side_effects=DATAFLOW),
    )(pltpu.with_memory_space_constraint(land, pltpu.HBM), after)


def _ag_chips_wait(send_sems, recv_sems, land, after, name):
    def body(land_ref, send_sems, recv_sems, after_ref, land_out):
        for cp in _chip_copies(land_ref, lambda k, me, peer: me, land_ref, lambda k, me, peer: peer, send_sems, recv_sems):
            cp.wait_send()
            cp.wait_recv()

    return pl.pallas_call(
        body, name=name, out_shape=pltpu.HBM(land.shape, land.dtype),
        in_specs=(HBM_SPEC, SEM_SPEC, SEM_SPEC, ANY_SPEC), out_specs=HBM_SPEC,
        input_output_aliases={0: 0}, compiler_params=pltpu.CompilerParams(has_side_effects=DATAFLOW),
    )(land, send_sems, recv_sems, after)


def _ag_pair_forward(land, name):
    n = len(CHIP_RELATIONS)

    def body(land_in, land_ref, send_sems, recv_sems):
        x, y, c = _position()
        copies = []
        for k, (rx, ry) in enumerate(CHIP_RELATIONS):
            chip = 4 * _flip(x, rx) + 2 * _flip(y, ry)
            mine = pltpu.make_async_remote_copy(
                src_ref=land_ref.at[chip + c], dst_ref=land_ref.at[chip + c], send_sem=send_sems.at[k],
                recv_sem=recv_sems.at[k], device_id=(x, y, 1 - c), device_id_type=MESH)
            theirs = pltpu.make_async_remote_copy(
                src_ref=land_ref.at[chip + c], dst_ref=land_ref.at[chip + 1 - c], send_sem=send_sems.at[k],
                recv_sem=recv_sems.at[k], device_id=(x, y, 1 - c), device_id_type=MESH)
            copies.append((mine, theirs))
        for mine, _ in copies:
            mine.start()
        for mine, theirs in copies:
            mine.wait_send()
            theirs.wait_recv()

    return pl.pallas_call(
        body, out_shape=jax.ShapeDtypeStruct(land.shape, land.dtype), in_specs=[HBM_SPEC], out_specs=HBM_SPEC,
        input_output_aliases={0: 0},
        scratch_shapes=[pltpu.SemaphoreType.DMA((n,)), pltpu.SemaphoreType.DMA((n,))], name=name)(land)


def _rs_chips_start(pair, name):
    _, r_dim, c_dim = pair.shape
    n = len(CHIP_FLIPS)

    def body(pair_ref, far_ref, send_sems, recv_sems, pair_thru, far_thru, token):
        for cp in _chip_copies(pair_ref, lambda k, me, peer: k + 1, far_ref, lambda k, me, peer: k, send_sems, recv_sems):
            cp.start()
        token[...] = jnp.zeros(TOKEN.shape, TOKEN.dtype)

    far = lax.empty((n, r_dim, c_dim), pair.dtype)
    return pl.pallas_call(
        body, name=name,
        out_shape=(pltpu.SemaphoreType.DMA((n,)), pltpu.SemaphoreType.DMA((n,)), pltpu.HBM(pair.shape, pair.dtype),
                   pltpu.HBM(far.shape, far.dtype), TOKEN),
        in_specs=(HBM_SPEC, HBM_SPEC),
        out_specs=(SEM_SPEC, SEM_SPEC, HBM_SPEC, HBM_SPEC, pl.BlockSpec(memory_space=pltpu.VMEM)),
        input_output_aliases={0: 2, 1: 3}, compiler_params=pltpu.CompilerParams(has_side_effects=DATAFLOW),
    )(pltpu.with_memory_space_constraint(pair, pltpu.HBM), pltpu.with_memory_space_constraint(far, pltpu.HBM))


def _rs_chips_wait(send_sems, recv_sems, pair, far, after, name):
    def body(pair_ref, far_ref, send_sems, recv_sems, after_ref, pair_out, far_out):
        for cp in _chip_copies(pair_ref, lambda k, me, peer: k + 1, far_ref, lambda k, me, peer: k, send_sems, recv_sems):
            cp.wait_send()
            cp.wait_recv()

    return pl.pallas_call(
        body, name=name, out_shape=(pltpu.HBM(pair.shape, pair.dtype), pltpu.HBM(far.shape, far.dtype)),
        in_specs=(HBM_SPEC, HBM_SPEC, SEM_SPEC, SEM_SPEC, ANY_SPEC),
        out_specs=(HBM_SPEC, HBM_SPEC), input_output_aliases={0: 0, 1: 1},
        compiler_params=pltpu.CompilerParams(has_side_effects=DATAFLOW),
    )(pair, far, send_sems, recv_sems, after)


def _rs_final_sum(pair, far, name):
    _, r_dim, c_dim = pair.shape
    tr = _tile(r_dim, 1200, SUBLANE)

    def body(p_ref, f0_ref, f1_ref, f2_ref, o_ref):
        o_ref[...] = ((p_ref[...].astype(F32) + f0_ref[...].astype(F32)) + f1_ref[...].astype(F32)) + f2_ref[...].astype(F32)

    def slot(k):
        return _bs((None, tr, c_dim), lambda i: (k, i, 0))

    return pl.pallas_call(
        body, out_shape=jax.ShapeDtypeStruct((r_dim, c_dim), F32), grid=(r_dim // tr,),
        in_specs=[slot(0), slot(0), slot(1), slot(2)], out_specs=_bs((tr, c_dim), lambda i: (i, 0)), name=name,
        compiler_params=_params("parallel"))(pair, far, far, far)


def _reduce_scatter_begin(g, name):
    recv = _rs_pair_exchange(g, name + "_pair")
    pair = _rs_pair_sum(g, recv, name + "_pairsum")
    return _rs_chips_start(pair, name + "_chips_start")


def _reduce_scatter_end(state, after, name):
    send_sems, recv_sems, pair, far, _ = state
    pair, far = _rs_chips_wait(send_sems, recv_sems, pair, far, after, name + "_chips_wait")
    return _rs_final_sum(pair, far, name + "_sum")


MATRICES = (("w_in", True), ("w_out", False), ("w_q", False), ("w_kv", True), ("w_o", False), ("w_up", True),
            ("w_down", False), ("w_conv_out", True), ("w_pool_grp", True))
PARTS = (("mix", ("w_in", "w_conv_out", "w_pool_grp", "w_out")), ("rest", ("w_q", "w_kv", "w_o", "w_up", "w_down")))


def _to_rows(name, transposed, w, d_model):
    if name == "w_pool_grp":
        w = jnp.swapaxes(w, 1, 2)
    elif transposed:
        w = w.T
    return w.reshape(-1, d_model)


def _from_rows(name, transposed, rows, shard_shape):
    if name == "w_pool_grp":
        g, i, o = shard_shape
        return jnp.swapaxes(rows.reshape(g, o, i), 1, 2)
    if transposed:
        return rows.reshape(shard_shape[1], shard_shape[0]).T
    return rows.reshape(shard_shape)


def _gathered_matrix(name, blocks, shard_shape):
    if name == "w_pool_grp":
        g, i, o = shard_shape
        return jnp.swapaxes(blocks.reshape(N_DEV, g, o, i), 0, 1).reshape(g, N_DEV * o, i)
    if name == "w_conv_out":
        return blocks.reshape(N_DEV * shard_shape[1], shard_shape[0])
    return blocks.reshape(-1, blocks.shape[-1])


def _scatter_blocks(name, full, shard_shape, d_model):
    if name == "w_pool_grp":
        g, i, o = shard_shape
        return jnp.swapaxes(full.reshape(g, N_DEV, o, i), 0, 1).reshape(N_DEV, -1, d_model)
    return full.reshape(N_DEV, -1, d_model)


def kernel(x, mem, mix_norm_g, w_in, conv_dw_w, conv_dw_b, conv_ln_g, conv_ln_b, w_conv_out, w_pool_grp, pool_scale, w_out, xattn_norm_g, mem_norm_g, w_q, w_kv, w_o, ffn_norm_g, w_up, ffn_dw_w, w_down, final_norm_g, loss_target, m_mix_norm_g, m_w_in, m_conv_dw_w, m_conv_dw_b, m_conv_ln_g, m_conv_ln_b, m_w_conv_out, m_w_pool_grp, m_pool_scale, m_w_out, m_xattn_norm_g, m_mem_norm_g, m_w_q, m_w_kv, m_w_o, m_ffn_norm_g, m_w_up, m_ffn_dw_w, m_w_down, m_final_norm_g, v_mix_norm_g, v_w_in, v_conv_dw_w, v_conv_dw_b, v_conv_ln_g, v_conv_ln_b, v_w_conv_out, v_w_pool_grp, v_pool_scale, v_w_out, v_xattn_norm_g, v_mem_norm_g, v_w_q, v_w_kv, v_w_o, v_ffn_norm_g, v_w_up, v_ffn_dw_w, v_w_down, v_final_norm_g):
    p = dict(locals())
    weight_names = ["mix_norm_g", "w_in", "conv_dw_w", "conv_dw_b", "conv_ln_g", "conv_ln_b", "w_conv_out",
                    "w_pool_grp", "pool_scale", "w_out", "xattn_norm_g", "mem_norm_g", "w_q", "w_kv", "w_o",
                    "ffn_norm_g", "w_up", "ffn_dw_w", "w_down", "final_norm_g"]
    n_batch, seq, d_model = x.shape
    m_len = mem.shape[1]
    depth = w_in.shape[0]
    assert depth == 2, "the exchange schedule below is written for two layers"
    t_dim = n_batch * seq
    c_conv = conv_dw_b.shape[1]
    n_groups = w_pool_grp.shape[1]
    assert w_pool_grp.shape[2] == LANE and c_conv % LANE == 0 and n_groups * LANE == c_conv
    gate_col0 = 2 * c_conv + n_groups * LANE
    pool_col0 = (2 * c_conv) // LANE

    dev = 4 * lax.axis_index("x") + 2 * lax.axis_index("y") + lax.axis_index("c")
    filt = jnp.concatenate([conv_dw_w.reshape(-1), ffn_dw_w.reshape(-1)])
    filt_rows = lax.bitcast_convert_type(filt, BF16).reshape(-1, d_model)
    transposed = dict(MATRICES)
    layout = {part: [(name, transposed[name], _to_rows(name, transposed[name], p[name][0], d_model).shape[0])
                     for name in names] for part, names in PARTS}

    def landing(pack):
        return lax.dynamic_update_index_in_dim(lax.empty((N_DEV,) + pack.shape, pack.dtype), pack, dev, 0)

    ag_state = {}
    after = filt_rows
    for l in range(depth):
        for part, names in PARTS:
            pieces = [_to_rows(name, transposed[name], p[name][l], d_model).astype(BF16) for name in names]
            if (l, part) == (0, PARTS[0][0]):
                pieces.append(filt_rows)
            ag_state[(l, part)] = _ag_chips_start(landing(jnp.concatenate(pieces, axis=0)), after, f"ag{l}{part}_chips_start")
            after = ag_state[(l, part)][3]
    all_started = after

    def gathered(l, part, after):
        send_sems, recv_sems, land, _ = ag_state[(l, part)]
        land = _ag_chips_wait(send_sems, recv_sems, land, after, f"ag{l}{part}_chips_wait")
        land = _ag_pair_forward(land, f"ag{l}{part}_pair_forward")
        out, row0 = {}, 0
        for name, tr, nrows in layout[part]:
            if name == "w_up":
                out["w_up_g"] = land[:N_DEV // 2, row0:row0 + nrows].reshape(-1, d_model)
                out["w_up_v"] = land[N_DEV // 2:, row0:row0 + nrows].reshape(-1, d_model)
            else:
                out[name] = _gathered_matrix(name, land[:, row0:row0 + nrows], p[name].shape[1:])
            row0 += nrows
        return out, land, row0

    full = [dict() for _ in range(depth)]
    mats, land0, row0 = gathered(0, PARTS[0][0], all_started)
    full[0].update(mats)
    filt_all = lax.bitcast_convert_type(land0[:, row0:row0 + filt_rows.shape[0]].reshape(N_DEV, -1, 2), F32)
    n_cw = conv_dw_w.size
    kc, cs = conv_dw_w.shape[1:]
    kf, fs = ffn_dw_w.shape[1:]
    conv_w_full = jnp.moveaxis(filt_all[:, :n_cw].reshape(N_DEV, depth, kc, cs), 0, 2).reshape(depth, kc, N_DEV * cs)
    ffn_w_full = jnp.moveaxis(filt_all[:, n_cw:].reshape(N_DEV, depth, kf, fs), 0, 2).reshape(depth, kf, N_DEV * fs)

    vec = lambda a: a.reshape(1, -1)
    x2d = x.reshape(t_dim, d_model)
    mem2d = mem.reshape(n_batch * m_len, d_model)
    mem_n = _rmsnorm_fwd(mem2d, vec(mem_norm_g), "mem_norm")

    saved = []
    xc = x2d
    for l in range(depth):
        if l > 0:
            full[l].update(gathered(l, PARTS[0][0], xc)[0])
        wl = full[l]
        s = {"x0": xc}
        s["h"] = _rmsnorm_fwd(xc, vec(mix_norm_g[l]), f"mix_norm_l{l}")
        s["proj"] = _matmul(s["h"], wl["w_in"], "nt", f"in_proj_l{l}", out_dtype=BF16)
        s["y1"] = _glu_conv_fwd(s["proj"], conv_w_full[l], vec(conv_dw_b[l]), n_batch, seq, f"glu_conv_l{l}")
        s["y3"] = _ln_silu_fwd(s["y1"], vec(conv_ln_g[l]), vec(conv_ln_b[l]), f"ln_silu_l{l}")
        s["yc"] = _matmul(s["y3"], wl["w_conv_out"], "nt", f"conv_out_l{l}", out_dtype=BF16)
        s["zp"] = _pool_fwd(s["proj"], pool_col0, n_groups, n_batch, seq, f"pool_l{l}")
        s["yp"] = _grouped(s["zp"], wl["w_pool_grp"], "nt", f"pool_proj_l{l}", out_dtype=BF16)
        s["merged"] = _merge_fwd(s["proj"], gate_col0, s["yc"], s["yp"], vec(pool_scale[l]), f"merge_l{l}")
        s["x1"] = _matmul(s["merged"], wl["w_out"], "nn", f"mix_out_l{l}", res=xc)
        wl.update(gathered(l, PARTS[1][0], s["x1"])[0])
        s["hq"] = _rmsnorm_fwd(s["x1"], vec(xattn_norm_g[l]), f"xattn_norm_l{l}")
        s["q"] = _matmul(s["hq"], wl["w_q"], "nn", f"q_proj_l{l}", out_dtype=BF16)
        s["kv"] = _matmul(mem_n, wl["w_kv"], "nt", f"kv_proj_l{l}", out_dtype=BF16)
        s["att"] = _attn_fwd(s["q"], s["kv"], n_batch, seq, m_len, f"attn_l{l}")
        s["x2"] = _matmul(s["att"], wl["w_o"], "nn", f"attn_out_l{l}", res=s["x1"])
        s["hf"] = _rmsnorm_fwd(s["x2"], vec(ffn_norm_g[l]), f"ffn_norm_l{l}")
        s["up_g"] = _matmul(s["hf"], wl["w_up_g"], "nt", f"up_proj_gate_l{l}", out_dtype=BF16)
        s["up_v"] = _matmul(s["hf"], wl["w_up_v"], "nt", f"up_proj_val_l{l}", out_dtype=BF16)
        s["act"] = _ffn_act_fwd(s["up_g"], s["up_v"], ffn_w_full[l], n_batch, seq, f"ffn_act_l{l}")
        xc = _matmul(s["act"], wl["w_down"], "nn", f"down_proj_l{l}", res=s["x2"])
        saved.append(s)

    dx, dxb, dg_final, loss_part = _loss_head(xc, vec(final_norm_g), loss_target.reshape(t_dim, d_model), "loss_head")

    small = {"final_norm_g": dg_final.reshape(-1)}
    big = [dict() for _ in range(depth)]
    rs_state = {}
    rs_after = loss_part

    def rs_begin(l, part):
        blocks = [_scatter_blocks(name, big[l][name], p[name].shape[1:], d_model).astype(BF16) for name, _, _ in layout[part]]
        rs_state[(l, part)] = _reduce_scatter_begin(jnp.concatenate(blocks, axis=1), f"rs{l}{part}")
        return rs_state[(l, part)][4]

    dmem_n = None
    for l in reversed(range(depth)):
        wl, s = full[l], saved[l]
        sm = {}
        dact = _matmul(dxb, wl["w_down"], "nt", f"d_act_l{l}", out_dtype=BF16, after=rs_after)
        big[l]["w_down"] = _matmul(s["act"], dxb, "tn", f"d_w_down_l{l}", out_dtype=BF16)
        dup_g, dup_v, dwf_g, dwf_v = _ffn_act_bwd(s["up_g"], s["up_v"], ffn_w_full[l], dact, n_batch, seq,
                                                  f"ffn_act_bwd_l{l}")
        sm["ffn_dw_w"] = jnp.concatenate([dwf_g, dwf_v], axis=1)
        dhf = _matmul(dup_g, wl["w_up_g"], "nn", f"d_hf_gate_l{l}")
        dhf = _matmul(dup_v, wl["w_up_v"], "nn", f"d_hf_val_l{l}", res=dhf)
        big[l]["w_up"] = jnp.concatenate([_matmul(dup_g, s["hf"], "tn", f"d_w_up_gate_l{l}", out_dtype=BF16),
                                          _matmul(dup_v, s["hf"], "tn", f"d_w_up_val_l{l}", out_dtype=BF16)], axis=0)
        dx, dxb, dg = _rmsnorm_bwd(s["x2"], vec(ffn_norm_g[l]), dhf, dx, f"ffn_norm_bwd_l{l}")
        sm["ffn_norm_g"] = dg
        datt = _matmul(dxb, wl["w_o"], "nt", f"d_att_l{l}", out_dtype=BF16, after=rs_after)
        big[l]["w_o"] = _matmul(s["att"], dxb, "tn", f"d_w_o_l{l}", out_dtype=BF16)
        dq, dk, dv = _attn_bwd(s["q"], s["kv"], datt, n_batch, seq, m_len, f"attn_bwd_l{l}")
        dkv = jnp.concatenate([dk, dv], axis=1)
        big[l]["w_kv"] = _matmul(dkv, mem_n, "tn", f"d_w_kv_l{l}", out_dtype=BF16)
        dmem_n = _matmul(dkv, wl["w_kv"], "nn", f"d_mem_l{l}", res=dmem_n)
        dhq = _matmul(dq, wl["w_q"], "nt", f"d_hq_l{l}")
        big[l]["w_q"] = _matmul(s["hq"], dq, "tn", f"d_w_q_l{l}", out_dtype=BF16)
        dx, dxb, dg = _rmsnorm_bwd(s["x1"], vec(xattn_norm_g[l]), dhq, dx, f"xattn_norm_bwd_l{l}")
        sm["xattn_norm_g"] = dg
        rs_after = rs_begin(l, PARTS[1][0])
        dmerged = _matmul(dxb, wl["w_out"], "nt", f"d_merged_l{l}", out_dtype=BF16, after=rs_after)
        big[l]["w_out"] = _matmul(s["merged"], dxb, "tn", f"d_w_out_l{l}", out_dtype=BF16)
        dgc, dgp, dyc, dyp, dscale = _merge_bwd(s["proj"], gate_col0, s["yc"], s["yp"], vec(pool_scale[l]), dmerged,
                                                f"merge_bwd_l{l}")
        sm["pool_scale"] = dscale
        dzp = _grouped(dyp, wl["w_pool_grp"], "nn", f"d_zp_l{l}")
        big[l]["w_pool_grp"] = _grouped_tn(dyp, s["zp"], n_groups, f"d_w_pool_l{l}")
        du = _pool_bwd(dzp, n_groups, n_batch, seq, f"pool_bwd_l{l}")
        dy3 = _matmul(dyc, wl["w_conv_out"], "nn", f"d_y3_l{l}")
        big[l]["w_conv_out"] = _matmul(dyc, s["y3"], "tn", f"d_w_conv_out_l{l}", out_dtype=BF16)
        dy1, dlg, dlb = _ln_silu_bwd(s["y1"], vec(conv_ln_g[l]), vec(conv_ln_b[l]), dy3, f"ln_silu_bwd_l{l}")
        sm["conv_ln_g"], sm["conv_ln_b"] = dlg, dlb
        da, dgl, dcw, dcb = _glu_conv_bwd(s["proj"], conv_w_full[l], dy1, n_batch, seq, f"glu_conv_bwd_l{l}")
        sm["conv_dw_w"], sm["conv_dw_b"] = dcw, dcb
        dproj = jnp.concatenate([da, dgl, du, dgc, dgp], axis=1)
        dh = _matmul(dproj, wl["w_in"], "nn", f"d_h_l{l}")
        big[l]["w_in"] = _matmul(dproj, s["h"], "tn", f"d_w_in_l{l}", out_dtype=BF16)
        dx, dxb, dg = _rmsnorm_bwd(s["x0"], vec(mix_norm_g[l]), dh, dx, f"mix_norm_bwd_l{l}")
        sm["mix_norm_g"] = dg
        for k, val in sm.items():
            small[(l, k)] = val.reshape(-1)
        rs_after = rs_begin(l, PARTS[0][0])
    _, _, dg_mem = _rmsnorm_bwd(mem2d, vec(mem_norm_g), dmem_n, None, "mem_norm_bwd")
    small["mem_norm_g"] = dg_mem.reshape(-1)
    small["loss"] = loss_part.reshape(-1)

    grads = {}
    per_layer = {name: [None] * depth for name, _ in MATRICES}
    for l in reversed(range(depth)):
        for part, _ in reversed(PARTS):
            mat_grads = _reduce_scatter_end(rs_state[(l, part)], rs_after, f"rs{l}{part}")
            row0 = 0
            for name, tr, nrows in layout[part]:
                per_layer[name][l] = _from_rows(name, tr, mat_grads[row0:row0 + nrows], p[name].shape[1:])
                row0 += nrows
    for name, _ in MATRICES:
        grads[name] = jnp.stack(per_layer[name])

    keys = list(small.keys())
    flat = jnp.concatenate([small[k] for k in keys])
    n_small = flat.shape[0]
    rows_small = -(-n_small // (SUBLANE * d_model)) * SUBLANE
    flat = jnp.pad(flat, (0, rows_small * d_model - n_small)).reshape(rows_small, d_model)
    every = _all_gather(flat, "small_all_gather")
    total = _sum_rows([every[i] for i in range(N_DEV)], F32, "small_sum").reshape(-1)
    off = 0
    red = {}
    for k in keys:
        red[k] = total[off:off + small[k].shape[0]]
        off += small[k].shape[0]
    loss = red["loss"][0]
    for name in ("mix_norm_g", "conv_dw_b", "conv_ln_g", "conv_ln_b", "pool_scale", "xattn_norm_g", "ffn_norm_g"):
        grads[name] = jnp.stack([red[(l, name)] for l in range(depth)])
    grads["conv_dw_w"] = jnp.stack([
        lax.dynamic_slice_in_dim(red[(l, "conv_dw_w")].reshape(kc, N_DEV * cs), dev * cs, cs, axis=1)
        for l in range(depth)])
    grads["ffn_dw_w"] = jnp.stack([
        lax.dynamic_slice_in_dim(red[(l, "ffn_dw_w")].reshape(kf, N_DEV * fs), dev * fs, fs, axis=1)
        for l in range(depth)])
    grads["mem_norm_g"] = red["mem_norm_g"]
    grads["final_norm_g"] = red["final_norm_g"]

    deltas, new_m, new_v = {}, {}, {}
    for name in weight_names:
        deltas[name], new_m[name], new_v[name] = _adamw(p[name], grads[name], p["m_" + name], p["v_" + name],
                                                        f"adamw_{name}")
    grad_x = dx.reshape(n_batch, seq, d_model)
    return (loss, grad_x, *[grads[n] for n in weight_names], *[deltas[n] for n in weight_names],
            *[new_m[n] for n in weight_names], *[new_v[n] for n in weight_names])
```

```python
import functools

import jax
import jax.numpy as jnp
from jax import lax
from jax.experimental import pallas as pl
from jax.experimental.pallas import tpu as pltpu

F32 = jnp.float32
BF16 = jnp.bfloat16
MESH = pl.DeviceIdType.MESH

N_DEV = 8
EPS = 1e-6
V7X_VMEM_BYTES = 64 * 1024 * 1024
VMEM_LIMIT = (V7X_VMEM_BYTES * 3) // 4
LANE = 128
SUBLANE = 8

CONV_HALO = 32
POOL_HALO = 16
FFN_HALO = 8
POOL_WINDOW_MAX = 16
XA_HEADS = 4

ADAM_LR = 0.001
ADAM_B1 = 0.9
ADAM_B2 = 0.999
ADAM_EPS = 1e-08
ADAM_WD = 0.01
ADAM_STEP = 10

GELU_C0 = 0.7978845608028654
GELU_C1 = 0.044715


ANY_SPEC = pl.BlockSpec(memory_space=pl.ANY)


def _tile(n, cap, mult=LANE):
    if n <= cap:
        return n
    best = None
    for d in range(mult, cap + 1, mult):
        if n % d == 0:
            best = d
    assert best is not None, (n, cap, mult)
    return best


def _params(*sem):
    return pltpu.CompilerParams(dimension_semantics=sem, vmem_limit_bytes=VMEM_LIMIT)


def _sig(x):
    return 1.0 / (1.0 + jnp.exp(-x))


def _bs(shape, imap):
    return pl.BlockSpec(shape, imap)


def _mxu_tile(n, cap):
    if n <= cap:
        return n
    best = {mult: max((d for d in range(mult, cap + 1, mult) if n % d == 0), default=0) for mult in (2 * LANE, LANE)}
    assert best[LANE] > 0, (n, cap)
    return best[2 * LANE] if 2 * best[2 * LANE] >= best[LANE] else best[LANE]


def _matmul(a, b, mode, name, res=None, out_dtype=F32, after=None, b_window=None):
    b_row0, b_rows = b_window if b_window is not None else (0, b.shape[0])
    if mode == "tn":
        k_dim, m_dim = a.shape
        k2, n_dim = b_rows, b.shape[1]
    elif mode == "nn":
        m_dim, k_dim = a.shape
        k2, n_dim = b_rows, b.shape[1]
    else:
        m_dim, k_dim = a.shape
        n_dim, k2 = b_rows, b.shape[1]
    assert k_dim == k2, (name, a.shape, b.shape)
    size = lambda t: jnp.dtype(t).itemsize
    tm = _mxu_tile(m_dim, 2816 if mode == "tn" else 1024)
    tn = _mxu_tile(n_dim, 2816)
    fixed = tm * tn * (2 * size(out_dtype) + (2 * size(res.dtype) if res is not None else 0) + 4)
    for cap in (2816, 2048, 1792, 1024, 512):
        tk = _mxu_tile(k_dim, cap)
        if fixed + 2 * tk * (tm * size(a.dtype) + tn * size(b.dtype)) <= VMEM_LIMIT - 8 * 1024 * 1024:
            break
    nk = k_dim // tk
    use_acc = nk > 1 and out_dtype != F32
    if mode == "tn":
        a_spec, ca = _bs((tk, tm), lambda i, j, k: (k, i)), 0
    else:
        a_spec, ca = _bs((tm, tk), lambda i, j, k: (i, k)), 1
    if mode == "nt":
        assert b_row0 % tn == 0
        b_spec, cb = _bs((tn, tk), lambda i, j, k: (j + b_row0 // tn, k)), 1
    else:
        assert b_row0 % tk == 0
        b_spec, cb = _bs((tk, tn), lambda i, j, k: (k + b_row0 // tk, j)), 0
    dims = (((ca,), (cb,)), ((), ()))
    o_spec = _bs((tm, tn), lambda i, j, k: (i, j))
    has_res = res is not None

    def body(*refs):
        a_ref, b_ref = refs[:2]
        r_ref = refs[2] if has_res else None
        o_ref = refs[n_in]
        k = pl.program_id(2)
        part = lax.dot_general(a_ref[...].astype(BF16), b_ref[...].astype(BF16), dims,
                               preferred_element_type=F32)
        if nk == 1:
            if has_res:
                part = part + r_ref[...].astype(F32)
            o_ref[...] = part.astype(out_dtype)
            return
        acc = refs[-1] if use_acc else o_ref

        @pl.when(k == 0)
        def _():
            acc[...] = part + r_ref[...].astype(F32) if has_res else part

        @pl.when(k > 0)
        def _():
            acc[...] += part

        if use_acc:
            @pl.when(k == nk - 1)
            def _():
                o_ref[...] = acc[...].astype(out_dtype)

    in_specs = [a_spec, b_spec] + ([o_spec] if has_res else [])
    args = (a, b) + ((res,) if has_res else ())
    if after is not None:
        in_specs.append(ANY_SPEC)
        args += (after,)
    n_in = len(args)
    return pl.pallas_call(
        body, out_shape=jax.ShapeDtypeStruct((m_dim, n_dim), out_dtype),
        grid=(m_dim // tm, n_dim // tn, nk), in_specs=in_specs, out_specs=o_spec,
        scratch_shapes=[pltpu.VMEM((tm, tn), F32)] if use_acc else [], name=name,
        compiler_params=_params("parallel", "parallel", "arbitrary"))(*args)


def _grouped(a, w, mode, name, out_dtype=F32):
    t_dim = a.shape[0]
    g_dim, r_dim, c_dim = w.shape
    ka, no = (c_dim, r_dim) if mode == "nt" else (r_dim, c_dim)
    tm = _tile(t_dim, 512)
    dims = (((1,), (1 if mode == "nt" else 0,)), ((), ()))

    def body(a_ref, w_ref, o_ref):
        o_ref[...] = lax.dot_general(a_ref[...].astype(BF16), w_ref[...].astype(BF16), dims,
                                     preferred_element_type=F32).astype(out_dtype)

    return pl.pallas_call(
        body, out_shape=jax.ShapeDtypeStruct((t_dim, g_dim * no), out_dtype),
        grid=(t_dim // tm, g_dim),
        in_specs=[_bs((tm, ka), lambda i, g: (i, g)), _bs((None, r_dim, c_dim), lambda i, g: (g, 0, 0))],
        out_specs=_bs((tm, no), lambda i, g: (i, g)), name=name,
        compiler_params=_params("parallel", "parallel"))(a, w)


def _grouped_tn(a, b, g_dim, name):
    t_dim = a.shape[0]
    ra = a.shape[1] // g_dim
    cb = b.shape[1] // g_dim
    tm = _tile(t_dim, 512)
    nt = t_dim // tm

    def body(a_ref, b_ref, o_ref):
        part = lax.dot_general(a_ref[...].astype(BF16), b_ref[...].astype(BF16), (((0,), (0,)), ((), ())),
                               preferred_element_type=F32)

        @pl.when(pl.program_id(1) == 0)
        def _():
            o_ref[...] = part

        @pl.when(pl.program_id(1) > 0)
        def _():
            o_ref[...] += part

    return pl.pallas_call(
        body, out_shape=jax.ShapeDtypeStruct((g_dim, ra, cb), F32), grid=(g_dim, nt),
        in_specs=[_bs((tm, ra), lambda g, i: (i, g)), _bs((tm, cb), lambda g, i: (i, g))],
        out_specs=_bs((None, ra, cb), lambda g, i: (g, 0, 0)), name=name,
        compiler_params=_params("parallel", "arbitrary"))(a, b)


def _rmsnorm_fwd(x, g, name):
    t_dim, d = x.shape
    tm = _tile(t_dim, 512)

    def body(x_ref, g_ref, o_ref):
        xv = x_ref[...]
        r = lax.rsqrt(jnp.mean(xv * xv, axis=-1, keepdims=True) + EPS)
        o_ref[...] = (xv * r * g_ref[...]).astype(BF16)

    return pl.pallas_call(
        body, out_shape=jax.ShapeDtypeStruct((t_dim, d), BF16), grid=(t_dim // tm,),
        in_specs=[_bs((tm, d), lambda i: (i, 0)), _bs((1, d), lambda i: (0, 0))],
        out_specs=_bs((tm, d), lambda i: (i, 0)), name=name, compiler_params=_params("parallel"))(x, g)


def _rmsnorm_bwd(x, g, dh, dx_in, name):
    t_dim, d = x.shape
    tm = _tile(t_dim, 512)
    has_in = dx_in is not None

    def body(*refs):
        if has_in:
            x_ref, g_ref, dh_ref, di_ref, dx_ref, dxb_ref, dg_ref = refs
        else:
            x_ref, g_ref, dh_ref, dx_ref, dxb_ref, dg_ref = refs
        xv = x_ref[...]
        r = lax.rsqrt(jnp.mean(xv * xv, axis=-1, keepdims=True) + EPS)
        xh = xv * r
        dhv = dh_ref[...].astype(F32)
        dxh = dhv * g_ref[...]
        dx = r * (dxh - xh * jnp.mean(dxh * xh, axis=-1, keepdims=True))
        if has_in:
            dx = dx + di_ref[...]
        dx_ref[...] = dx
        dxb_ref[...] = dx.astype(BF16)
        part = jnp.sum(dhv * xh, axis=0, keepdims=True)

        @pl.when(pl.program_id(0) == 0)
        def _():
            dg_ref[...] = part

        @pl.when(pl.program_id(0) > 0)
        def _():
            dg_ref[...] += part

    row = _bs((tm, d), lambda i: (i, 0))
    vec = _bs((1, d), lambda i: (0, 0))
    args = (x, g, dh) + ((dx_in,) if has_in else ())
    return pl.pallas_call(
        body, out_shape=(jax.ShapeDtypeStruct((t_dim, d), F32), jax.ShapeDtypeStruct((t_dim, d), BF16),
                         jax.ShapeDtypeStruct((1, d), F32)),
        grid=(t_dim // tm,), in_specs=[row, vec, row] + ([row] if has_in else []),
        out_specs=(row, row, vec), name=name, compiler_params=_params("arbitrary"))(*args)


def _loss_head(x, g, tgt, name):
    t_dim, d = x.shape
    tm = _tile(t_dim, 512)

    def body(x_ref, g_ref, t_ref, dx_ref, dxb_ref, dg_ref, loss_ref):
        xv = x_ref[...]
        gv = g_ref[...]
        r = lax.rsqrt(jnp.mean(xv * xv, axis=-1, keepdims=True) + EPS)
        xh = xv * r
        err = xh * gv - t_ref[...]
        dy = err * (1.0 / d)
        dxh = dy * gv
        dx = r * (dxh - xh * jnp.mean(dxh * xh, axis=-1, keepdims=True))
        dx_ref[...] = dx
        dxb_ref[...] = dx.astype(BF16)
        dg_part = jnp.sum(dy * xh, axis=0, keepdims=True)
        loss_part = jnp.full((1, LANE), 0.5 * jnp.sum(jnp.mean(err * err, axis=-1, keepdims=True)), F32)

        @pl.when(pl.program_id(0) == 0)
        def _():
            dg_ref[...] = dg_part
            loss_ref[...] = loss_part

        @pl.when(pl.program_id(0) > 0)
        def _():
            dg_ref[...] += dg_part
            loss_ref[...] += loss_part

    row = _bs((tm, d), lambda i: (i, 0))
    vec = _bs((1, d), lambda i: (0, 0))
    return pl.pallas_call(
        body, out_shape=(jax.ShapeDtypeStruct((t_dim, d), F32), jax.ShapeDtypeStruct((t_dim, d), BF16),
                         jax.ShapeDtypeStruct((1, d), F32), jax.ShapeDtypeStruct((1, LANE), F32)),
        grid=(t_dim // tm,), in_specs=[row, vec, row],
        out_specs=(row, row, vec, _bs((1, LANE), lambda i: (0, 0))), name=name,
        compiler_params=_params("arbitrary"))(x, g, tgt)


def _glu_conv_fwd(proj, dw_w, dw_b, n_batch, seq, name):
    kk, cc = dw_w.shape
    nj = cc // LANE
    ch = min(256, seq)

    def body(a_ref, gl_ref, w_ref, b_ref, o_ref, pad):
        pad[0:CONV_HALO, :] = jnp.zeros((CONV_HALO, LANE), F32)
        pad[CONV_HALO:CONV_HALO + seq, :] = a_ref[...].astype(F32) * _sig(gl_ref[...].astype(F32))
        for c0 in range(0, seq, ch):
            acc = jnp.broadcast_to(b_ref[...], (ch, LANE))
            for k in range(kk):
                acc = acc + w_ref[k:k + 1, :] * pad[pl.ds(c0 + CONV_HALO - (kk - 1) + k, ch), :]
            o_ref[c0:c0 + ch, :] = acc

    return pl.pallas_call(
        body, out_shape=jax.ShapeDtypeStruct((n_batch * seq, cc), F32), grid=(n_batch, nj),
        in_specs=[_bs((seq, LANE), lambda b, j: (b, j)), _bs((seq, LANE), lambda b, j: (b, nj + j)),
                  _bs((kk, LANE), lambda b, j: (0, j)), _bs((1, LANE), lambda b, j: (0, j))],
        out_specs=_bs((seq, LANE), lambda b, j: (b, j)),
        scratch_shapes=[pltpu.VMEM((seq + CONV_HALO, LANE), F32)], name=name,
        compiler_params=_params("parallel", "parallel"))(proj, proj, dw_w, dw_b)


def _glu_conv_bwd(proj, dw_w, dy1, n_batch, seq, name):
    kk, cc = dw_w.shape
    nj = cc // LANE
    ch = min(256, seq)

    def body(a_ref, gl_ref, dy_ref, w_ref, da_ref, dgl_ref, dw_ref, db_ref, padf, padb):
        first = pl.program_id(1) == 0
        padf[0:CONV_HALO, :] = jnp.zeros((CONV_HALO, LANE), F32)
        padf[CONV_HALO:CONV_HALO + seq, :] = a_ref[...].astype(F32) * _sig(gl_ref[...].astype(F32))
        padb[0:seq, :] = dy_ref[...]
        padb[seq:seq + CONV_HALO, :] = jnp.zeros((CONV_HALO, LANE), F32)

        @pl.when(first)
        def _():
            dw_ref[...] = jnp.zeros((kk, LANE), F32)
            db_ref[...] = jnp.zeros((1, LANE), F32)

        for c0 in range(0, seq, ch):
            acc = jnp.zeros((ch, LANE), F32)
            for k in range(kk):
                acc = acc + w_ref[k:k + 1, :] * padb[pl.ds(c0 + (kk - 1) - k, ch), :]
            sg = _sig(gl_ref[c0:c0 + ch, :].astype(F32))
            da_ref[c0:c0 + ch, :] = (acc * sg).astype(BF16)
            dgl_ref[c0:c0 + ch, :] = (acc * a_ref[c0:c0 + ch, :].astype(F32) * sg * (1.0 - sg)).astype(BF16)
        for k in range(kk):
            s = jnp.zeros((1, LANE), F32)
            for c0 in range(0, seq, ch):
                s = s + jnp.sum(padb[c0:c0 + ch, :] * padf[pl.ds(c0 + CONV_HALO - (kk - 1) + k, ch), :],
                                axis=0, keepdims=True)
            dw_ref[k:k + 1, :] += s
        db_ref[...] += jnp.sum(dy_ref[...], axis=0, keepdims=True)

    tok = _bs((seq, LANE), lambda j, b: (b, j))
    t_dim = n_batch * seq
    return pl.pallas_call(
        body, out_shape=(jax.ShapeDtypeStruct((t_dim, cc), BF16), jax.ShapeDtypeStruct((t_dim, cc), BF16),
                         jax.ShapeDtypeStruct((kk, cc), F32), jax.ShapeDtypeStruct((1, cc), F32)),
        grid=(nj, n_batch),
        in_specs=[tok, _bs((seq, LANE), lambda j, b: (b, nj + j)), tok, _bs((kk, LANE), lambda j, b: (0, j))],
        out_specs=(tok, tok, _bs((kk, LANE), lambda j, b: (0, j)), _bs((1, LANE), lambda j, b: (0, j))),
        scratch_shapes=[pltpu.VMEM((seq + CONV_HALO, LANE), F32), pltpu.VMEM((seq + CONV_HALO, LANE), F32)],
        name=name, compiler_params=_params("parallel", "arbitrary"))(proj, proj, dy1, dw_w)


def _ln_silu_fwd(y1, g, b, name):
    t_dim, c = y1.shape
    tm = _tile(t_dim, 512)

    def body(y_ref, g_ref, b_ref, o_ref):
        yv = y_ref[...]
        xc = yv - jnp.mean(yv, axis=-1, keepdims=True)
        rstd = lax.rsqrt(jnp.mean(xc * xc, axis=-1, keepdims=True) + EPS)
        y2 = xc * rstd * g_ref[...] + b_ref[...]
        o_ref[...] = (y2 * _sig(y2)).astype(BF16)

    row = _bs((tm, c), lambda i: (i, 0))
    vec = _bs((1, c), lambda i: (0, 0))
    return pl.pallas_call(
        body, out_shape=jax.ShapeDtypeStruct((t_dim, c), BF16), grid=(t_dim // tm,),
        in_specs=[row, vec, vec], out_specs=row, name=name, compiler_params=_params("parallel"))(y1, g, b)


def _ln_silu_bwd(y1, g, b, dy3, name):
    t_dim, c = y1.shape
    tm = _tile(t_dim, 512)

    def body(y_ref, g_ref, b_ref, d_ref, dy_ref, dg_ref, db_ref):
        yv = y_ref[...]
        gv = g_ref[...]
        xc = yv - jnp.mean(yv, axis=-1, keepdims=True)
        rstd = lax.rsqrt(jnp.mean(xc * xc, axis=-1, keepdims=True) + EPS)
        yh = xc * rstd
        y2 = yh * gv + b_ref[...]
        s = _sig(y2)
        dy2 = d_ref[...].astype(F32) * (s * (1.0 + y2 * (1.0 - s)))
        dyh = dy2 * gv
        dy_ref[...] = rstd * (dyh - jnp.mean(dyh, axis=-1, keepdims=True)
                              - yh * jnp.mean(dyh * yh, axis=-1, keepdims=True))
        dg_part = jnp.sum(dy2 * yh, axis=0, keepdims=True)
        db_part = jnp.sum(dy2, axis=0, keepdims=True)

        @pl.when(pl.program_id(0) == 0)
        def _():
            dg_ref[...] = dg_part
            db_ref[...] = db_part

        @pl.when(pl.program_id(0) > 0)
        def _():
            dg_ref[...] += dg_part
            db_ref[...] += db_part

    row = _bs((tm, c), lambda i: (i, 0))
    vec = _bs((1, c), lambda i: (0, 0))
    return pl.pallas_call(
        body, out_shape=(jax.ShapeDtypeStruct((t_dim, c), F32), jax.ShapeDtypeStruct((1, c), F32),
                         jax.ShapeDtypeStruct((1, c), F32)),
        grid=(t_dim // tm,), in_specs=[row, vec, vec, row], out_specs=(row, vec, vec), name=name,
        compiler_params=_params("arbitrary"))(y1, g, b, dy3)


def _pool_fwd(proj, col0, n_groups, n_batch, seq, name):
    ch = min(256, seq)

    def body(u_ref, o_ref, pad):
        w = lax.shift_left(jnp.int32(2), pl.program_id(1))
        pad[0:POOL_HALO, :] = jnp.zeros((POOL_HALO, LANE), F32)
        pad[POOL_HALO:POOL_HALO + seq, :] = u_ref[...].astype(F32)
        for c0 in range(0, seq, ch):
            acc = jnp.zeros((ch, LANE), F32)
            for j in range(POOL_WINDOW_MAX):
                acc = acc + jnp.where(j < w, 1.0, 0.0).astype(F32) * pad[pl.ds(c0 + POOL_HALO - j, ch), :]
            t = c0 + lax.broadcasted_iota(jnp.int32, (ch, LANE), 0)
            cnt = jnp.minimum(t + 1, w).astype(F32)
            o_ref[c0:c0 + ch, :] = (acc / cnt - pad[POOL_HALO + c0:POOL_HALO + c0 + ch, :]).astype(BF16)

    return pl.pallas_call(
        body, out_shape=jax.ShapeDtypeStruct((n_batch * seq, n_groups * LANE), BF16), grid=(n_batch, n_groups),
        in_specs=[_bs((seq, LANE), lambda b, g: (b, col0 + g))], out_specs=_bs((seq, LANE), lambda b, g: (b, g)),
        scratch_shapes=[pltpu.VMEM((seq + POOL_HALO, LANE), F32)], name=name,
        compiler_params=_params("parallel", "parallel"))(proj)


def _pool_bwd(dzp, n_groups, n_batch, seq, name):
    ch = min(256, seq)

    def body(d_ref, o_ref, pad):
        w = lax.shift_left(jnp.int32(2), pl.program_id(1))
        for c0 in range(0, seq, ch):
            t = c0 + lax.broadcasted_iota(jnp.int32, (ch, LANE), 0)
            cnt = jnp.minimum(t + 1, w).astype(F32)
            pad[c0:c0 + ch, :] = d_ref[c0:c0 + ch, :] / cnt
        pad[seq:seq + POOL_HALO, :] = jnp.zeros((POOL_HALO, LANE), F32)
        for c0 in range(0, seq, ch):
            acc = jnp.zeros((ch, LANE), F32)
            for j in range(POOL_WINDOW_MAX):
                acc = acc + jnp.where(j < w, 1.0, 0.0).astype(F32) * pad[pl.ds(c0 + j, ch), :]
            o_ref[c0:c0 + ch, :] = (acc - d_ref[c0:c0 + ch, :]).astype(BF16)

    tok = _bs((seq, LANE), lambda b, g: (b, g))
    return pl.pallas_call(
        body, out_shape=jax.ShapeDtypeStruct((n_batch * seq, n_groups * LANE), BF16), grid=(n_batch, n_groups),
        in_specs=[tok], out_specs=tok, scratch_shapes=[pltpu.VMEM((seq + POOL_HALO, LANE), F32)], name=name,
        compiler_params=_params("parallel", "parallel"))(dzp)


def _merge_fwd(proj, col0, yc, yp, scale, name):
    t_dim, d = yc.shape
    half = d // 2
    tm = _tile(t_dim, 512)
    c0 = col0 // half

    def body(gc_ref, gp_ref, yc_ref, yp_ref, s_ref, o_ref):
        f32 = lambda r: r[...].astype(F32)
        o_ref[...] = (_sig(f32(gc_ref)) * f32(yc_ref) + _sig(f32(gp_ref)) * (f32(yp_ref) * s_ref[...])).astype(BF16)

    blk = _bs((tm, half), lambda i, j: (i, j))
    return pl.pallas_call(
        body, out_shape=jax.ShapeDtypeStruct((t_dim, d), BF16), grid=(t_dim // tm, 2),
        in_specs=[_bs((tm, half), lambda i, j: (i, c0 + j)), _bs((tm, half), lambda i, j: (i, c0 + 2 + j)),
                  blk, blk, _bs((1, half), lambda i, j: (0, j))],
        out_specs=blk, name=name, compiler_params=_params("parallel", "parallel"))(proj, proj, yc, yp, scale)


def _merge_bwd(proj, col0, yc, yp, scale, dm, name):
    t_dim, d = yc.shape
    half = d // 2
    tm = _tile(t_dim, 512)
    c0 = col0 // half

    def body(gc_ref, gp_ref, yc_ref, yp_ref, s_ref, dm_ref, dgc_ref, dgp_ref, dyc_ref, dyp_ref, ds_ref):
        dmv = dm_ref[...].astype(F32)
        sgc = _sig(gc_ref[...].astype(F32))
        sgp = _sig(gp_ref[...].astype(F32))
        sv = s_ref[...]
        ypre = yp_ref[...].astype(F32)
        dgc_ref[...] = (dmv * yc_ref[...].astype(F32) * sgc * (1.0 - sgc)).astype(BF16)
        dgp_ref[...] = (dmv * (ypre * sv) * sgp * (1.0 - sgp)).astype(BF16)
        dyc_ref[...] = (dmv * sgc).astype(BF16)
        dyp = dmv * sgp
        dyp_ref[...] = (dyp * sv).astype(BF16)
        part = jnp.sum(dyp * ypre, axis=0, keepdims=True)

        @pl.when(pl.program_id(1) == 0)
        def _():
            ds_ref[...] = part

        @pl.when(pl.program_id(1) > 0)
        def _():
            ds_ref[...] += part

    blk = _bs((tm, half), lambda j, i: (i, j))
    big = jax.ShapeDtypeStruct((t_dim, d), BF16)
    return pl.pallas_call(
        body, out_shape=(big, big, big, big, jax.ShapeDtypeStruct((1, d), F32)), grid=(2, t_dim // tm),
        in_specs=[_bs((tm, half), lambda j, i: (i, c0 + j)), _bs((tm, half), lambda j, i: (i, c0 + 2 + j)),
                  blk, blk, _bs((1, half), lambda j, i: (0, j)), blk],
        out_specs=(blk, blk, blk, blk, _bs((1, half), lambda j, i: (0, j))), name=name,
        compiler_params=_params("parallel", "arbitrary"))(proj, proj, yc, yp, scale, dm)


def _attn_fwd(q, kv, n_batch, seq, m_len, name):
    d = q.shape[1]
    hd = d // XA_HEADS
    tq = _tile(seq, 1024)
    nq = seq // tq
    scale = hd ** -0.5

    def body(q_ref, k_ref, v_ref, o_ref):
        sc = lax.dot_general(q_ref[...].astype(BF16), k_ref[...].astype(BF16), (((1,), (1,)), ((), ())),
                             preferred_element_type=F32) * scale
        p = jnp.exp(sc - jnp.max(sc, axis=-1, keepdims=True))
        pr = p / jnp.sum(p, axis=-1, keepdims=True)
        o_ref[...] = jnp.dot(pr.astype(BF16), v_ref[...].astype(BF16), preferred_element_type=F32).astype(BF16)

    return pl.pallas_call(
        body, out_shape=jax.ShapeDtypeStruct((n_batch * seq, d), BF16), grid=(n_batch, XA_HEADS, nq),
        in_specs=[_bs((tq, hd), lambda b, h, i: (b * nq + i, h)), _bs((m_len, hd), lambda b, h, i: (b, h)),
                  _bs((m_len, hd), lambda b, h, i: (b, XA_HEADS + h))],
        out_specs=_bs((tq, hd), lambda b, h, i: (b * nq + i, h)), name=name,
        compiler_params=_params("parallel", "parallel", "parallel"))(q, kv, kv)


def _attn_bwd(q, kv, datt, n_batch, seq, m_len, name):
    d = q.shape[1]
    hd = d // XA_HEADS
    tq = _tile(seq, 1024)
    nq = seq // tq
    scale = hd ** -0.5

    def body(q_ref, k_ref, v_ref, do_ref, dq_ref, dk_ref, dv_ref):
        qb = q_ref[...].astype(BF16)
        kb = k_ref[...].astype(BF16)
        vb = v_ref[...].astype(BF16)
        dob = do_ref[...].astype(BF16)
        sc = lax.dot_general(qb, kb, (((1,), (1,)), ((), ())), preferred_element_type=F32) * scale
        p = jnp.exp(sc - jnp.max(sc, axis=-1, keepdims=True))
        pr = p / jnp.sum(p, axis=-1, keepdims=True)
        dpr = lax.dot_general(dob, vb, (((1,), (1,)), ((), ())), preferred_element_type=F32)
        dsc = pr * (dpr - jnp.sum(dpr * pr, axis=-1, keepdims=True)) * scale
        dsb = dsc.astype(BF16)
        dq_ref[...] = jnp.dot(dsb, kb, preferred_element_type=F32).astype(BF16)
        dv_part = lax.dot_general(pr.astype(BF16), dob, (((0,), (0,)), ((), ())), preferred_element_type=F32)
        dk_part = lax.dot_general(dsb, qb, (((0,), (0,)), ((), ())), preferred_element_type=F32)

        @pl.when(pl.program_id(2) == 0)
        def _():
            dk_ref[...] = dk_part
            dv_ref[...] = dv_part

        @pl.when(pl.program_id(2) > 0)
        def _():
            dk_ref[...] += dk_part
            dv_ref[...] += dv_part

    qs = _bs((tq, hd), lambda b, h, i: (b * nq + i, h))
    ks = _bs((m_len, hd), lambda b, h, i: (b, h))
    return pl.pallas_call(
        body, out_shape=(jax.ShapeDtypeStruct((n_batch * seq, d), BF16), jax.ShapeDtypeStruct((n_batch * m_len, d), F32),
                         jax.ShapeDtypeStruct((n_batch * m_len, d), F32)),
        grid=(n_batch, XA_HEADS, nq),
        in_specs=[qs, ks, _bs((m_len, hd), lambda b, h, i: (b, XA_HEADS + h)), qs],
        out_specs=(qs, ks, ks), name=name,
        compiler_params=_params("parallel", "parallel", "arbitrary"))(q, kv, kv, datt)


def _gelu_parts(g):
    th = jnp.tanh(GELU_C0 * (g + GELU_C1 * g * g * g))
    return th, 0.5 * g * (1.0 + th)


def _ffn_act_fwd(up_g, up_v, dw_w, n_batch, seq, name):
    kk, c2 = dw_w.shape
    f_dim = c2 // 2
    wd = 2 * LANE
    nj = f_dim // wd
    ch = min(128, seq)

    def body(g_ref, v_ref, wg_ref, wv_ref, o_ref, padg, padv):
        for pad, src in ((padg, g_ref), (padv, v_ref)):
            pad[0:FFN_HALO, :] = jnp.zeros((FFN_HALO, wd), F32)
            pad[FFN_HALO:FFN_HALO + seq, :] = src[...].astype(F32)
        for c0 in range(0, seq, ch):
            gate = jnp.zeros((ch, wd), F32)
            val = jnp.zeros((ch, wd), F32)
            for k in range(kk):
                off = c0 + FFN_HALO - (kk - 1) + k
                gate = gate + wg_ref[k:k + 1, :] * padg[pl.ds(off, ch), :]
                val = val + wv_ref[k:k + 1, :] * padv[pl.ds(off, ch), :]
            o_ref[c0:c0 + ch, :] = (_gelu_parts(gate)[1] * val).astype(BF16)

    return pl.pallas_call(
        body, out_shape=jax.ShapeDtypeStruct((n_batch * seq, f_dim), BF16), grid=(n_batch, nj),
        in_specs=[_bs((seq, wd), lambda b, j: (b, j)), _bs((seq, wd), lambda b, j: (b, j)),
                  _bs((kk, wd), lambda b, j: (0, j)), _bs((kk, wd), lambda b, j: (0, nj + j))],
        out_specs=_bs((seq, wd), lambda b, j: (b, j)),
        scratch_shapes=[pltpu.VMEM((seq + FFN_HALO, wd), F32), pltpu.VMEM((seq + FFN_HALO, wd), F32)], name=name,
        compiler_params=_params("parallel", "parallel"))(up_g, up_v, dw_w, dw_w)


def _ffn_act_bwd(up_g, up_v, dw_w, dact, n_batch, seq, name):
    kk, c2 = dw_w.shape
    f_dim = c2 // 2
    wd = 2 * LANE
    nj = f_dim // wd
    ch = min(128, seq)

    def body(g_ref, v_ref, wg_ref, wv_ref, da_ref, dg_ref, dv_ref, dwg_ref, dwv_ref, padg, padv, pbg, pbv):
        for pad, src in ((padg, g_ref), (padv, v_ref)):
            pad[0:FFN_HALO, :] = jnp.zeros((FFN_HALO, wd), F32)
            pad[FFN_HALO:FFN_HALO + seq, :] = src[...].astype(F32)
        for pb in (pbg, pbv):
            pb[seq:seq + FFN_HALO, :] = jnp.zeros((FFN_HALO, wd), F32)

        @pl.when(pl.program_id(1) == 0)
        def _():
            dwg_ref[...] = jnp.zeros((kk, wd), F32)
            dwv_ref[...] = jnp.zeros((kk, wd), F32)

        for c0 in range(0, seq, ch):
            gate = jnp.zeros((ch, wd), F32)
            val = jnp.zeros((ch, wd), F32)
            for k in range(kk):
                off = c0 + FFN_HALO - (kk - 1) + k
                gate = gate + wg_ref[k:k + 1, :] * padg[pl.ds(off, ch), :]
                val = val + wv_ref[k:k + 1, :] * padv[pl.ds(off, ch), :]
            th, gelu = _gelu_parts(gate)
            dgelu = 0.5 * (1.0 + th) + 0.5 * gate * (1.0 - th * th) * GELU_C0 * (1.0 + 3.0 * GELU_C1 * gate * gate)
            dav = da_ref[c0:c0 + ch, :].astype(F32)
            pbg[c0:c0 + ch, :] = dav * val * dgelu
            pbv[c0:c0 + ch, :] = dav * gelu
        for pb, pad, w_ref, d_ref, dw_ref in ((pbg, padg, wg_ref, dg_ref, dwg_ref), (pbv, padv, wv_ref, dv_ref, dwv_ref)):
            for c0 in range(0, seq, ch):
                acc = jnp.zeros((ch, wd), F32)
                for k in range(kk):
                    acc = acc + w_ref[k:k + 1, :] * pb[pl.ds(c0 + (kk - 1) - k, ch), :]
                d_ref[c0:c0 + ch, :] = acc.astype(BF16)
            for k in range(kk):
                s = jnp.zeros((1, wd), F32)
                for c0 in range(0, seq, ch):
                    s = s + jnp.sum(pb[c0:c0 + ch, :] * pad[pl.ds(c0 + FFN_HALO - (kk - 1) + k, ch), :],
                                    axis=0, keepdims=True)
                dw_ref[k:k + 1, :] += s

    t_dim = n_batch * seq
    tok = _bs((seq, wd), lambda j, b: (b, j))
    wblk = _bs((kk, wd), lambda j, b: (0, j))
    pad_shape = pltpu.VMEM((seq + FFN_HALO, wd), F32)
    return pl.pallas_call(
        body, out_shape=(jax.ShapeDtypeStruct((t_dim, f_dim), BF16), jax.ShapeDtypeStruct((t_dim, f_dim), BF16),
                         jax.ShapeDtypeStruct((kk, f_dim), F32), jax.ShapeDtypeStruct((kk, f_dim), F32)),
        grid=(nj, n_batch),
        in_specs=[tok, tok, wblk, _bs((kk, wd), lambda j, b: (0, nj + j)), tok],
        out_specs=(tok, tok, wblk, wblk), scratch_shapes=[pad_shape, pad_shape, pad_shape, pad_shape], name=name,
        compiler_params=_params("parallel", "arbitrary"))(up_g, up_v, dw_w, dw_w, dact)


def _sum_rows(parts, out_dtype, name):
    r_dim, c_dim = parts[0].shape
    tr = _tile(r_dim, 1200, SUBLANE)
    n = len(parts)

    def body(*refs):
        acc = refs[0][...].astype(F32)
        for r in refs[1:n]:
            acc = acc + r[...].astype(F32)
        refs[n][...] = acc.astype(out_dtype)

    blk = _bs((tr, c_dim), lambda i: (i, 0))
    return pl.pallas_call(
        body, out_shape=jax.ShapeDtypeStruct((r_dim, c_dim), out_dtype), grid=(r_dim // tr,),
        in_specs=[blk] * n, out_specs=blk, name=name, compiler_params=_params("parallel"))(*parts)


def _adamw(w, g, m, v, name):
    shape = w.shape
    c_dim = shape[-1]
    r_dim = w.size // c_dim
    two_d = lambda t: t.reshape(r_dim, c_dim)
    tr = _tile(r_dim, max(SUBLANE, (256 * 1024) // max(c_dim, LANE) // SUBLANE * SUBLANE), SUBLANE)
    c1 = 1.0 - ADAM_B1 ** ADAM_STEP
    c2 = 1.0 - ADAM_B2 ** ADAM_STEP

    def body(w_ref, g_ref, m_ref, v_ref, d_ref, mo_ref, vo_ref):
        gv = g_ref[...]
        mn = ADAM_B1 * m_ref[...] + (1.0 - ADAM_B1) * gv
        vn = ADAM_B2 * v_ref[...] + (1.0 - ADAM_B2) * (gv * gv)
        mo_ref[...] = mn
        vo_ref[...] = vn
        d_ref[...] = -ADAM_LR * ((mn / c1) / (jnp.sqrt(vn / c2) + ADAM_EPS) + ADAM_WD * w_ref[...])

    blk = _bs((tr, c_dim), lambda i: (i, 0))
    out = jax.ShapeDtypeStruct((r_dim, c_dim), F32)
    d, mo, vo = pl.pallas_call(
        body, out_shape=(out, out, out), grid=(r_dim // tr,), in_specs=[blk] * 4, out_specs=(blk, blk, blk),
        name=name, compiler_params=_params("parallel"))(two_d(w), two_d(g), two_d(m), two_d(v))
    return d.reshape(shape), mo.reshape(shape), vo.reshape(shape)


HBM_SPEC = pl.BlockSpec(memory_space=pltpu.HBM)


def _position():
    return lax.axis_index("x"), lax.axis_index("y"), lax.axis_index("c")


def _all_gather(shard, name):
    def body(x_ref, out_ref, send_sems, recv_sems, local_sem):
        x, y, c = _position()
        me, sibling = (x, y, c), (x, y, 1 - c)
        chips = [(1 - x, y), (x, 1 - y), (1 - x, 1 - y)]

        def rows(px, py, pc):
            return out_ref.at[4 * px + 2 * py + pc]

        def copy(k, block, to, src=None):
            return pltpu.make_async_remote_copy(
                src_ref=rows(*block) if src is None else src, dst_ref=rows(*block),
                send_sem=send_sems.at[k], recv_sem=recv_sems.at[k], device_id=to, device_id_type=MESH)

        mine = pltpu.make_async_copy(x_ref, rows(*me), local_sem)
        mine.start()
        first = [copy(0, me, sibling, src=x_ref)]
        first += [copy(1 + j, me, (*chip, c), src=x_ref) for j, chip in enumerate(chips)]
        for cp in first:
            cp.start()
        passed = [copy(4 + j, (*chip, c), sibling) for j, chip in enumerate(chips)]
        for j, chip in enumerate(chips):
            copy(1 + j, (*chip, c), me).wait_recv()
            passed[j].start()
        copy(0, sibling, me).wait_recv()
        for j, chip in enumerate(chips):
            copy(4 + j, (*chip, 1 - c), me).wait_recv()
        for cp in first + passed:
            cp.wait_send()
        mine.wait()

    return pl.pallas_call(
        body, out_shape=jax.ShapeDtypeStruct((N_DEV,) + shard.shape, shard.dtype),
        in_specs=[HBM_SPEC], out_specs=HBM_SPEC,
        scratch_shapes=[pltpu.SemaphoreType.DMA((7,)), pltpu.SemaphoreType.DMA((7,)), pltpu.SemaphoreType.DMA(())],
        name=name)(shard)


CHIP_RELATIONS = ((0, 0), (1, 0), (0, 1), (1, 1))


def _rs_pair_exchange(g, name):
    _, r_dim, c_dim = g.shape
    n = len(CHIP_RELATIONS)

    def body(g_ref, recv_ref, send_sems, recv_sems):
        x, y, c = _position()
        sibling = (x, y, 1 - c)
        copies = []
        for k, (rx, ry) in enumerate(CHIP_RELATIONS):
            px = x + rx - 2 * x * rx
            py = y + ry - 2 * y * ry
            copies.append(pltpu.make_async_remote_copy(
                src_ref=g_ref.at[4 * px + 2 * py + 1 - c], dst_ref=recv_ref.at[k], send_sem=send_sems.at[k],
                recv_sem=recv_sems.at[k], device_id=sibling, device_id_type=MESH))
        for cp in copies:
            cp.start()
        for cp in copies:
            cp.wait()

    return pl.pallas_call(
        body, out_shape=jax.ShapeDtypeStruct((n, r_dim, c_dim), g.dtype), in_specs=[HBM_SPEC], out_specs=HBM_SPEC,
        scratch_shapes=[pltpu.SemaphoreType.DMA((n,)), pltpu.SemaphoreType.DMA((n,))], name=name)(g)


def _rs_pair_sum(g, recv, name):
    _, r_dim, c_dim = g.shape
    n = len(CHIP_RELATIONS)
    tr = _tile(r_dim, 1200, SUBLANE)
    x, y, c = _position()
    own = jnp.stack([4 * (x + rx - 2 * x * rx) + 2 * (y + ry - 2 * y * ry) + c for rx, ry in CHIP_RELATIONS])

    def body(own_ref, g_ref, r_ref, o_ref):
        o_ref[...] = (g_ref[...].astype(F32) + r_ref[...].astype(F32)).astype(o_ref.dtype)

    blk = _bs((None, tr, c_dim), lambda k, i, own_ref: (k, i, 0))
    return pl.pallas_call(
        body, out_shape=jax.ShapeDtypeStruct((n, r_dim, c_dim), g.dtype),
        grid_spec=pltpu.PrefetchScalarGridSpec(
            num_scalar_prefetch=1, grid=(n, r_dim // tr),
            in_specs=[_bs((None, tr, c_dim), lambda k, i, own_ref: (own_ref[k], i, 0)), blk], out_specs=blk),
        name=name, compiler_params=_params("parallel", "parallel"))(own.astype(jnp.int32), g, recv)


SEM_SPEC = pl.BlockSpec(memory_space=pltpu.SEMAPHORE)
DATAFLOW = pltpu.SideEffectType.DATAFLOW_SIDE_EFFECTING
CHIP_FLIPS = CHIP_RELATIONS[1:]
TOKEN = jax.ShapeDtypeStruct((SUBLANE, LANE), F32)


def _flip(v, r):
    return v + r - 2 * v * r


def _chip_copies(src_ref, src_of, dst_ref, dst_of, send_sems, recv_sems):
    x, y, c = _position()
    me = 4 * x + 2 * y + c
    out = []
    for k, (rx, ry) in enumerate(CHIP_FLIPS):
        px, py = _flip(x, rx), _flip(y, ry)
        peer = 4 * px + 2 * py + c
        out.append(pltpu.make_async_remote_copy(
            src_ref=src_ref.at[src_of(k, me, peer)], dst_ref=dst_ref.at[dst_of(k, me, peer)],
            send_sem=send_sems.at[k], recv_sem=recv_sems.at[k], device_id=(px, py, c), device_id_type=MESH))
    return out


def _device_block(ref, spec, d):
    rows, axis = spec
    return ref.at[pl.ds(d * rows, rows)] if axis == 0 else ref.at[:, pl.ds(d * rows, rows)]


def _ag_chips_start(lands, specs, after, name):
    n = len(lands)
    nf = len(CHIP_FLIPS)

    def body(*refs):
        send_sems, recv_sems, token = refs[n + 1], refs[n + 2], refs[-1]
        x, y, c = _position()
        me = 4 * x + 2 * y + c
        for i, spec in enumerate(specs):
            blk = _device_block(refs[i], spec, me)
            for k, (rx, ry) in enumerate(CHIP_FLIPS):
                pltpu.make_async_remote_copy(
                    src_ref=blk, dst_ref=blk, send_sem=send_sems.at[nf * i + k], recv_sem=recv_sems.at[nf * i + k],
                    device_id=(_flip(x, rx), _flip(y, ry), c), device_id_type=MESH).start()
        token[...] = jnp.zeros(TOKEN.shape, TOKEN.dtype)

    sems = pltpu.SemaphoreType.DMA((nf * n,))
    return pl.pallas_call(
        body, name=name, out_shape=(sems, sems, *[pltpu.HBM(t.shape, t.dtype) for t in lands], TOKEN),
        in_specs=(HBM_SPEC,) * n + (ANY_SPEC,),
        out_specs=(SEM_SPEC, SEM_SPEC) + (HBM_SPEC,) * n + (pl.BlockSpec(memory_space=pltpu.VMEM),),
        input_output_aliases={i: 2 + i for i in range(n)}, compiler_params=pltpu.CompilerParams(has_side_effects=DATAFLOW),
    )(*[pltpu.with_memory_space_constraint(t, pltpu.HBM) for t in lands], after)


def _ag_chips_wait(send_sems, recv_sems, lands, specs, after, name):
    n = len(lands)
    nf = len(CHIP_FLIPS)

    def body(*refs):
        send_sems, recv_sems = refs[n], refs[n + 1]
        x, y, c = _position()
        me = 4 * x + 2 * y + c
        for i, spec in enumerate(specs):
            for k, (rx, ry) in enumerate(CHIP_FLIPS):
                px, py = _flip(x, rx), _flip(y, ry)
                cp = pltpu.make_async_remote_copy(
                    src_ref=_device_block(refs[i], spec, me), dst_ref=_device_block(refs[i], spec, 4 * px + 2 * py + c),
                    send_sem=send_sems.at[nf * i + k], recv_sem=recv_sems.at[nf * i + k],
                    device_id=(px, py, c), device_id_type=MESH)
                cp.wait_send()
                cp.wait_recv()

    return pl.pallas_call(
        body, name=name, out_shape=tuple(pltpu.HBM(t.shape, t.dtype) for t in lands),
        in_specs=(HBM_SPEC,) * n + (SEM_SPEC, SEM_SPEC, ANY_SPEC), out_specs=(HBM_SPEC,) * n,
        input_output_aliases={i: i for i in range(n)}, compiler_params=pltpu.CompilerParams(has_side_effects=DATAFLOW),
    )(*lands, send_sems, recv_sems, after)


def _ag_pair_forward(lands, specs, name):
    n = len(lands)
    nr = len(CHIP_RELATIONS)

    def body(*refs):
        outs, send_sems, recv_sems = refs[n:2 * n], refs[2 * n], refs[2 * n + 1]
        x, y, c = _position()
        copies = []
        for i, spec in enumerate(specs):
            for k, (rx, ry) in enumerate(CHIP_RELATIONS):
                chip = 4 * _flip(x, rx) + 2 * _flip(y, ry)
                held = _device_block(outs[i], spec, chip + c)
                sems = dict(send_sem=send_sems.at[nr * i + k], recv_sem=recv_sems.at[nr * i + k],
                            device_id=(x, y, 1 - c), device_id_type=MESH)
                mine = pltpu.make_async_remote_copy(src_ref=held, dst_ref=held, **sems)
                theirs = pltpu.make_async_remote_copy(src_ref=held, dst_ref=_device_block(outs[i], spec, chip + 1 - c), **sems)
                copies.append((mine, theirs))
        for mine, _ in copies:
            mine.start()
        for mine, theirs in copies:
            mine.wait_send()
            theirs.wait_recv()

    sems = pltpu.SemaphoreType.DMA((nr * n,))
    return pl.pallas_call(
        body, out_shape=tuple(jax.ShapeDtypeStruct(t.shape, t.dtype) for t in lands), in_specs=[HBM_SPEC] * n,
        out_specs=(HBM_SPEC,) * n, input_output_aliases={i: i for i in range(n)}, scratch_shapes=[sems, sems], name=name)(*lands)


def _rs_chips_start(pair, name):
    _, r_dim, c_dim = pair.shape
    n = len(CHIP_FLIPS)

    def body(pair_ref, far_ref, send_sems, recv_sems, pair_thru, far_thru, token):
        for cp in _chip_copies(pair_ref, lambda k, me, peer: k + 1, far_ref, lambda k, me, peer: k, send_sems, recv_sems):
            cp.start()
        token[...] = jnp.zeros(TOKEN.shape, TOKEN.dtype)

    far = lax.empty((n, r_dim, c_dim), pair.dtype)
    return pl.pallas_call(
        body, name=name,
        out_shape=(pltpu.SemaphoreType.DMA((n,)), pltpu.SemaphoreType.DMA((n,)), pltpu.HBM(pair.shape, pair.dtype),
                   pltpu.HBM(far.shape, far.dtype), TOKEN),
        in_specs=(HBM_SPEC, HBM_SPEC),
        out_specs=(SEM_SPEC, SEM_SPEC, HBM_SPEC, HBM_SPEC, pl.BlockSpec(memory_space=pltpu.VMEM)),
        input_output_aliases={0: 2, 1: 3}, compiler_params=pltpu.CompilerParams(has_side_effects=DATAFLOW),
    )(pltpu.with_memory_space_constraint(pair, pltpu.HBM), pltpu.with_memory_space_constraint(far, pltpu.HBM))


def _rs_chips_wait(send_sems, recv_sems, pair, far, after, name):
    def body(pair_ref, far_ref, send_sems, recv_sems, after_ref, pair_out, far_out):
        for cp in _chip_copies(pair_ref, lambda k, me, peer: k + 1, far_ref, lambda k, me, peer: k, send_sems, recv_sems):
            cp.wait_send()
            cp.wait_recv()

    return pl.pallas_call(
        body, name=name, out_shape=(pltpu.HBM(pair.shape, pair.dtype), pltpu.HBM(far.shape, far.dtype)),
        in_specs=(HBM_SPEC, HBM_SPEC, SEM_SPEC, SEM_SPEC, ANY_SPEC),
        out_specs=(HBM_SPEC, HBM_SPEC), input_output_aliases={0: 0, 1: 1},
        compiler_params=pltpu.CompilerParams(has_side_effects=DATAFLOW),
    )(pair, far, send_sems, recv_sems, after)


def _rs_final_sum(pair, far, name):
    _, r_dim, c_dim = pair.shape
    tr = _tile(r_dim, 1200, SUBLANE)

    def body(p_ref, f0_ref, f1_ref, f2_ref, o_ref):
        o_ref[...] = ((p_ref[...].astype(F32) + f0_ref[...].astype(F32)) + f1_ref[...].astype(F32)) + f2_ref[...].astype(F32)

    def slot(k):
        return _bs((None, tr, c_dim), lambda i: (k, i, 0))

    return pl.pallas_call(
        body, out_shape=jax.ShapeDtypeStruct((r_dim, c_dim), F32), grid=(r_dim // tr,),
        in_specs=[slot(0), slot(0), slot(1), slot(2)], out_specs=_bs((tr, c_dim), lambda i: (i, 0)), name=name,
        compiler_params=_params("parallel"))(pair, far, far, far)


def _reduce_scatter_begin(g, name):
    recv = _rs_pair_exchange(g, name + "_pair")
    pair = _rs_pair_sum(g, recv, name + "_pairsum")
    return _rs_chips_start(pair, name + "_chips_start")


def _reduce_scatter_end(state, after, name):
    send_sems, recv_sems, pair, far, _ = state
    pair, far = _rs_chips_wait(send_sems, recv_sems, pair, far, after, name + "_chips_wait")
    return _rs_final_sum(pair, far, name + "_sum")


MATRICES = (("w_in", True), ("w_out", False), ("w_q", False), ("w_kv", True), ("w_o", False), ("w_up", True),
            ("w_down", False), ("w_conv_out", True), ("w_pool_grp", True))
PARTS = (("mix", ("w_in", "w_conv_out", "w_pool_grp", "w_out")), ("rest", ("w_q", "w_kv", "w_o", "w_up", "w_down")))


def _to_rows(name, transposed, w, d_model):
    if name == "w_pool_grp":
        w = jnp.swapaxes(w, 1, 2)
    elif transposed:
        w = w.T
    return w.reshape(-1, d_model)


def _from_rows(name, transposed, rows, shard_shape):
    if name == "w_pool_grp":
        g, i, o = shard_shape
        return jnp.swapaxes(rows.reshape(g, o, i), 1, 2)
    if transposed:
        return rows.reshape(shard_shape[1], shard_shape[0]).T
    return rows.reshape(shard_shape)


def _scatter_blocks(name, full, shard_shape, d_model):
    if name == "w_pool_grp":
        g, i, o = shard_shape
        return jnp.swapaxes(full.reshape(g, N_DEV, o, i), 0, 1).reshape(N_DEV, -1, d_model)
    return full.reshape(N_DEV, -1, d_model)


def kernel(x, mem, mix_norm_g, w_in, conv_dw_w, conv_dw_b, conv_ln_g, conv_ln_b, w_conv_out, w_pool_grp, pool_scale, w_out, xattn_norm_g, mem_norm_g, w_q, w_kv, w_o, ffn_norm_g, w_up, ffn_dw_w, w_down, final_norm_g, loss_target, m_mix_norm_g, m_w_in, m_conv_dw_w, m_conv_dw_b, m_conv_ln_g, m_conv_ln_b, m_w_conv_out, m_w_pool_grp, m_pool_scale, m_w_out, m_xattn_norm_g, m_mem_norm_g, m_w_q, m_w_kv, m_w_o, m_ffn_norm_g, m_w_up, m_ffn_dw_w, m_w_down, m_final_norm_g, v_mix_norm_g, v_w_in, v_conv_dw_w, v_conv_dw_b, v_conv_ln_g, v_conv_ln_b, v_w_conv_out, v_w_pool_grp, v_pool_scale, v_w_out, v_xattn_norm_g, v_mem_norm_g, v_w_q, v_w_kv, v_w_o, v_ffn_norm_g, v_w_up, v_ffn_dw_w, v_w_down, v_final_norm_g):
    p = dict(locals())
    weight_names = ["mix_norm_g", "w_in", "conv_dw_w", "conv_dw_b", "conv_ln_g", "conv_ln_b", "w_conv_out",
                    "w_pool_grp", "pool_scale", "w_out", "xattn_norm_g", "mem_norm_g", "w_q", "w_kv", "w_o",
                    "ffn_norm_g", "w_up", "ffn_dw_w", "w_down", "final_norm_g"]
    n_batch, seq, d_model = x.shape
    m_len = mem.shape[1]
    depth = w_in.shape[0]
    assert depth == 2, "the exchange schedule below is written for two layers"
    t_dim = n_batch * seq
    c_conv = conv_dw_b.shape[1]
    n_groups = w_pool_grp.shape[1]
    assert w_pool_grp.shape[2] == LANE and c_conv % LANE == 0 and n_groups * LANE == c_conv
    gate_col0 = 2 * c_conv + n_groups * LANE
    pool_col0 = (2 * c_conv) // LANE

    dev = 4 * lax.axis_index("x") + 2 * lax.axis_index("y") + lax.axis_index("c")
    filt = jnp.concatenate([conv_dw_w.reshape(-1), ffn_dw_w.reshape(-1)])
    filt_rows = lax.bitcast_convert_type(filt, BF16).reshape(-1, d_model)
    transposed = dict(MATRICES)
    layout = {part: [(name, transposed[name], _to_rows(name, transposed[name], p[name][0], d_model).shape[0])
                     for name in names] for part, names in PARTS}

    def landing(name, shard):
        if name == "w_pool_grp":
            block, axis = jnp.swapaxes(shard, 1, 2), 1
        elif name == "filt":
            block, axis = shard, 0
        else:
            block, axis = (shard.T if transposed[name] else shard), 0
        block = block.astype(BF16)
        rows = block.shape[axis]
        shape = block.shape[:axis] + (N_DEV * rows,) + block.shape[axis + 1:]
        start = (0,) * axis + (dev * rows,) + (0,) * (block.ndim - axis - 1)
        return lax.dynamic_update_slice(lax.empty(shape, BF16), block, start), (rows, axis)

    ag_state = {}
    after = filt_rows
    for l in range(depth):
        for part, names in PARTS:
            items = [(name, p[name][l]) for name in names]
            if (l, part) == (0, PARTS[0][0]):
                items.append(("filt", filt_rows))
            lands, specs = zip(*[landing(name, shard) for name, shard in items])
            out = _ag_chips_start(lands, specs, after, f"ag{l}{part}_chips_start")
            ag_state[(l, part)] = ([name for name, _ in items], specs, out)
            after = out[-1]
    all_started = after

    def gathered(l, part, after):
        names, specs, out = ag_state[(l, part)]
        lands = _ag_chips_wait(out[0], out[1], out[2:-1], specs, after, f"ag{l}{part}_chips_wait")
        lands = _ag_pair_forward(lands, specs, f"ag{l}{part}_pair_forward")
        return dict(zip(names, lands))

    full = [dict() for _ in range(depth)]
    full[0].update(gathered(0, PARTS[0][0], all_started))
    filt_all = lax.bitcast_convert_type(full[0]["filt"].reshape(N_DEV, -1, 2), F32)
    n_cw = conv_dw_w.size
    kc, cs = conv_dw_w.shape[1:]
    kf, fs = ffn_dw_w.shape[1:]
    conv_w_full = jnp.moveaxis(filt_all[:, :n_cw].reshape(N_DEV, depth, kc, cs), 0, 2).reshape(depth, kc, N_DEV * cs)
    ffn_w_full = jnp.moveaxis(filt_all[:, n_cw:].reshape(N_DEV, depth, kf, fs), 0, 2).reshape(depth, kf, N_DEV * fs)

    vec = lambda a: a.reshape(1, -1)
    x2d = x.reshape(t_dim, d_model)
    mem2d = mem.reshape(n_batch * m_len, d_model)
    mem_n = _rmsnorm_fwd(mem2d, vec(mem_norm_g), "mem_norm")

    saved = []
    xc = x2d
    for l in range(depth):
        if l > 0:
            full[l].update(gathered(l, PARTS[0][0], xc))
        wl = full[l]
        s = {"x0": xc}
        s["h"] = _rmsnorm_fwd(xc, vec(mix_norm_g[l]), f"mix_norm_l{l}")
        s["proj"] = _matmul(s["h"], wl["w_in"], "nt", f"in_proj_l{l}", out_dtype=BF16)
        s["y1"] = _glu_conv_fwd(s["proj"], conv_w_full[l], vec(conv_dw_b[l]), n_batch, seq, f"glu_conv_l{l}")
        s["y3"] = _ln_silu_fwd(s["y1"], vec(conv_ln_g[l]), vec(conv_ln_b[l]), f"ln_silu_l{l}")
        s["yc"] = _matmul(s["y3"], wl["w_conv_out"], "nt", f"conv_out_l{l}", out_dtype=BF16)
        s["zp"] = _pool_fwd(s["proj"], pool_col0, n_groups, n_batch, seq, f"pool_l{l}")
        s["yp"] = _grouped(s["zp"], wl["w_pool_grp"], "nt", f"pool_proj_l{l}", out_dtype=BF16)
        s["merged"] = _merge_fwd(s["proj"], gate_col0, s["yc"], s["yp"], vec(pool_scale[l]), f"merge_l{l}")
        s["x1"] = _matmul(s["merged"], wl["w_out"], "nn", f"mix_out_l{l}", res=xc)
        wl.update(gathered(l, PARTS[1][0], s["x1"]))
        half_up = wl["w_up"].shape[0] // 2
        up_gate, up_val = (0, half_up), (half_up, half_up)
        s["hq"] = _rmsnorm_fwd(s["x1"], vec(xattn_norm_g[l]), f"xattn_norm_l{l}")
        s["q"] = _matmul(s["hq"], wl["w_q"], "nn", f"q_proj_l{l}", out_dtype=BF16)
        s["kv"] = _matmul(mem_n, wl["w_kv"], "nt", f"kv_proj_l{l}", out_dtype=BF16)
        s["att"] = _attn_fwd(s["q"], s["kv"], n_batch, seq, m_len, f"attn_l{l}")
        s["x2"] = _matmul(s["att"], wl["w_o"], "nn", f"attn_out_l{l}", res=s["x1"])
        s["hf"] = _rmsnorm_fwd(s["x2"], vec(ffn_norm_g[l]), f"ffn_norm_l{l}")
        s["up_g"] = _matmul(s["hf"], wl["w_up"], "nt", f"up_proj_gate_l{l}", out_dtype=BF16, b_window=up_gate)
        s["up_v"] = _matmul(s["hf"], wl["w_up"], "nt", f"up_proj_val_l{l}", out_dtype=BF16, b_window=up_val)
        s["act"] = _ffn_act_fwd(s["up_g"], s["up_v"], ffn_w_full[l], n_batch, seq, f"ffn_act_l{l}")
        xc = _matmul(s["act"], wl["w_down"], "nn", f"down_proj_l{l}", res=s["x2"])
        saved.append(s)

    dx, dxb, dg_final, loss_part = _loss_head(xc, vec(final_norm_g), loss_target.reshape(t_dim, d_model), "loss_head")

    small = {"final_norm_g": dg_final.reshape(-1)}
    big = [dict() for _ in range(depth)]
    rs_state = {}
    rs_after = loss_part

    def rs_begin(l, part):
        blocks = [_scatter_blocks(name, big[l][name], p[name].shape[1:], d_model).astype(BF16) for name, _, _ in layout[part]]
        rs_state[(l, part)] = _reduce_scatter_begin(jnp.concatenate(blocks, axis=1), f"rs{l}{part}")
        return rs_state[(l, part)][4]

    dmem_n = None
    for l in reversed(range(depth)):
        wl, s = full[l], saved[l]
        sm = {}
        dact = _matmul(dxb, wl["w_down"], "nt", f"d_act_l{l}", out_dtype=BF16, after=rs_after)
        big[l]["w_down"] = _matmul(s["act"], dxb, "tn", f"d_w_down_l{l}", out_dtype=BF16)
        dup_g, dup_v, dwf_g, dwf_v = _ffn_act_bwd(s["up_g"], s["up_v"], ffn_w_full[l], dact, n_batch, seq,
                                                  f"ffn_act_bwd_l{l}")
        sm["ffn_dw_w"] = jnp.concatenate([dwf_g, dwf_v], axis=1)
        dhf = _matmul(dup_g, wl["w_up"], "nn", f"d_hf_gate_l{l}", b_window=up_gate)
        dhf = _matmul(dup_v, wl["w_up"], "nn", f"d_hf_val_l{l}", res=dhf, b_window=up_val)
        big[l]["w_up"] = jnp.concatenate([_matmul(dup_g, s["hf"], "tn", f"d_w_up_gate_l{l}", out_dtype=BF16),
                                          _matmul(dup_v, s["hf"], "tn", f"d_w_up_val_l{l}", out_dtype=BF16)], axis=0)
        dx, dxb, dg = _rmsnorm_bwd(s["x2"], vec(ffn_norm_g[l]), dhf, dx, f"ffn_norm_bwd_l{l}")
        sm["ffn_norm_g"] = dg
        datt = _matmul(dxb, wl["w_o"], "nt", f"d_att_l{l}", out_dtype=BF16, after=rs_after)
        big[l]["w_o"] = _matmul(s["att"], dxb, "tn", f"d_w_o_l{l}", out_dtype=BF16)
        dq, dk, dv = _attn_bwd(s["q"], s["kv"], datt, n_batch, seq, m_len, f"attn_bwd_l{l}")
        dkv = jnp.concatenate([dk, dv], axis=1)
        big[l]["w_kv"] = _matmul(dkv, mem_n, "tn", f"d_w_kv_l{l}", out_dtype=BF16)
        dmem_n = _matmul(dkv, wl["w_kv"], "nn", f"d_mem_l{l}", res=dmem_n)
        dhq = _matmul(dq, wl["w_q"], "nt", f"d_hq_l{l}")
        big[l]["w_q"] = _matmul(s["hq"], dq, "tn", f"d_w_q_l{l}", out_dtype=BF16)
        dx, dxb, dg = _rmsnorm_bwd(s["x1"], vec(xattn_norm_g[l]), dhq, dx, f"xattn_norm_bwd_l{l}")
        sm["xattn_norm_g"] = dg
        rs_after = rs_begin(l, PARTS[1][0])
        dmerged = _matmul(dxb, wl["w_out"], "nt", f"d_merged_l{l}", out_dtype=BF16, after=rs_after)
        big[l]["w_out"] = _matmul(s["merged"], dxb, "tn", f"d_w_out_l{l}", out_dtype=BF16)
        dgc, dgp, dyc, dyp, dscale = _merge_bwd(s["proj"], gate_col0, s["yc"], s["yp"], vec(pool_scale[l]), dmerged,
                                                f"merge_bwd_l{l}")
        sm["pool_scale"] = dscale
        dzp = _grouped(dyp, wl["w_pool_grp"], "nn", f"d_zp_l{l}")
        big[l]["w_pool_grp"] = _grouped_tn(dyp, s["zp"], n_groups, f"d_w_pool_l{l}")
        du = _pool_bwd(dzp, n_groups, n_batch, seq, f"pool_bwd_l{l}")
        dy3 = _matmul(dyc, wl["w_conv_out"], "nn", f"d_y3_l{l}")
        big[l]["w_conv_out"] = _matmul(dyc, s["y3"], "tn", f"d_w_conv_out_l{l}", out_dtype=BF16)
        dy1, dlg, dlb = _ln_silu_bwd(s["y1"], vec(conv_ln_g[l]), vec(conv_ln_b[l]), dy3, f"ln_silu_bwd_l{l}")
        sm["conv_ln_g"], sm["conv_ln_b"] = dlg, dlb
        da, dgl, dcw, dcb = _glu_conv_bwd(s["proj"], conv_w_full[l], dy1, n_batch, seq, f"glu_conv_bwd_l{l}")
        sm["conv_dw_w"], sm["conv_dw_b"] = dcw, dcb
        dproj = jnp.concatenate([da, dgl, du, dgc, dgp], axis=1)
        dh = _matmul(dproj, wl["w_in"], "nn", f"d_h_l{l}")
        big[l]["w_in"] = _matmul(dproj, s["h"], "tn", f"d_w_in_l{l}", out_dtype=BF16)
        dx, dxb, dg = _rmsnorm_bwd(s["x0"], vec(mix_norm_g[l]), dh, dx, f"mix_norm_bwd_l{l}")
        sm["mix_norm_g"] = dg
        for k, val in sm.items():
            small[(l, k)] = val.reshape(-1)
        rs_after = rs_begin(l, PARTS[0][0])
    _, _, dg_mem = _rmsnorm_bwd(mem2d, vec(mem_norm_g), dmem_n, None, "mem_norm_bwd")
    small["mem_norm_g"] = dg_mem.reshape(-1)
    small["loss"] = loss_part.reshape(-1)

    grads = {}
    per_layer = {name: [None] * depth for name, _ in MATRICES}
    for l in reversed(range(depth)):
        for part, _ in reversed(PARTS):
            mat_grads = _reduce_scatter_end(rs_state[(l, part)], rs_after, f"rs{l}{part}")
            row0 = 0
            for name, tr, nrows in layout[part]:
                per_layer[name][l] = _from_rows(name, tr, mat_grads[row0:row0 + nrows], p[name].shape[1:])
                row0 += nrows
    for name, _ in MATRICES:
        grads[name] = jnp.stack(per_layer[name])

    keys = list(small.keys())
    flat = jnp.concatenate([small[k] for k in keys])
    n_small = flat.shape[0]
    rows_small = -(-n_small // (SUBLANE * d_model)) * SUBLANE
    flat = jnp.pad(flat, (0, rows_small * d_model - n_small)).reshape(rows_small, d_model)
    every = _all_gather(flat, "small_all_gather")
    total = _sum_rows([every[i] for i in range(N_DEV)], F32, "small_sum").reshape(-1)
    off = 0
    red = {}
    for k in keys:
        red[k] = total[off:off + small[k].shape[0]]
        off += small[k].shape[0]
    loss = red["loss"][0]
    for name in ("mix_norm_g", "conv_dw_b", "conv_ln_g", "conv_ln_b", "pool_scale", "xattn_norm_g", "ffn_norm_g"):
        grads[name] = jnp.stack([red[(l, name)] for l in range(depth)])
    grads["conv_dw_w"] = jnp.stack([
        lax.dynamic_slice_in_dim(red[(l, "conv_dw_w")].reshape(kc, N_DEV * cs), dev * cs, cs, axis=1)
        for l in range(depth)])
    grads["ffn_dw_w"] = jnp.stack([
        lax.dynamic_slice_in_dim(red[(l, "ffn_dw_w")].reshape(kf, N_DEV * fs), dev * fs, fs, axis=1)
        for l in range(depth)])
    grads["mem_norm_g"] = red["mem_norm_g"]
    grads["final_norm_g"] = red["final_norm_g"]

    deltas, new_m, new_v = {}, {}, {}
    for name in weight_names:
        deltas[name], new_m[name], new_v[name] = _adamw(p[name], grads[name], p["m_" + name], p["v_" + name],
                                                        f"adamw_{name}")
    grad_x = dx.reshape(n_batch, seq, d_model)
    return (loss, grad_x, *[grads[n] for n in weight_names], *[deltas[n] for n in weight_names],
            *[new_m[n] for n in weight_names], *[new_v[n] for n in weight_names])
```

```python
import functools

import jax
import jax.numpy as jnp
from jax import lax
from jax.experimental import pallas as pl
from jax.experimental.pallas import tpu as pltpu

F32 = jnp.float32
BF16 = jnp.bfloat16
MESH = pl.DeviceIdType.MESH

N_DEV = 8
EPS = 1e-6
V7X_VMEM_BYTES = 64 * 1024 * 1024
VMEM_LIMIT = (V7X_VMEM_BYTES * 3) // 4
LANE = 128
SUBLANE = 8

CONV_HALO = 32
POOL_HALO = 16
FFN_HALO = 8
POOL_WINDOW_MAX = 16
XA_HEADS = 4

ADAM_LR = 0.001
ADAM_B1 = 0.9
ADAM_B2 = 0.999
ADAM_EPS = 1e-08
ADAM_WD = 0.01
ADAM_STEP = 10

GELU_C0 = 0.7978845608028654
GELU_C1 = 0.044715


ANY_SPEC = pl.BlockSpec(memory_space=pl.ANY)


def _tile(n, cap, mult=LANE):
    if n <= cap:
        return n
    best = None
    for d in range(mult, cap + 1, mult):
        if n % d == 0:
            best = d
    assert best is not None, (n, cap, mult)
    return best


def _params(*sem):
    return pltpu.CompilerParams(dimension_semantics=sem, vmem_limit_bytes=VMEM_LIMIT)


def _sig(x):
    return 1.0 / (1.0 + jnp.exp(-x))


def _bs(shape, imap):
    return pl.BlockSpec(shape, imap)


def _mxu_tile(n, cap):
    if n <= cap:
        return n
    best = {mult: max((d for d in range(mult, cap + 1, mult) if n % d == 0), default=0) for mult in (2 * LANE, LANE)}
    assert best[LANE] > 0, (n, cap)
    return best[2 * LANE] if 2 * best[2 * LANE] >= best[LANE] else best[LANE]


def _matmul(a, b, mode, name, res=None, out_dtype=F32, after=None, b_window=None):
    b_row0, b_rows = b_window if b_window is not None else (0, b.shape[0])
    if mode == "tn":
        k_dim, m_dim = a.shape
        k2, n_dim = b_rows, b.shape[1]
    elif mode == "nn":
        m_dim, k_dim = a.shape
        k2, n_dim = b_rows, b.shape[1]
    else:
        m_dim, k_dim = a.shape
        n_dim, k2 = b_rows, b.shape[1]
    assert k_dim == k2, (name, a.shape, b.shape)
    size = lambda t: jnp.dtype(t).itemsize
    tm = _mxu_tile(m_dim, 2816 if mode == "tn" else 1024)
    tn = _mxu_tile(n_dim, 2816)
    fixed = tm * tn * (2 * size(out_dtype) + (2 * size(res.dtype) if res is not None else 0) + 4)
    for cap in (2816, 2048, 1792, 1024, 512):
        tk = _mxu_tile(k_dim, cap)
        if fixed + 2 * tk * (tm * size(a.dtype) + tn * size(b.dtype)) <= VMEM_LIMIT - 8 * 1024 * 1024:
            break
    nk = k_dim // tk
    use_acc = nk > 1 and out_dtype != F32
    if mode == "tn":
        a_spec, ca = _bs((tk, tm), lambda i, j, k: (k, i)), 0
    else:
        a_spec, ca = _bs((tm, tk), lambda i, j, k: (i, k)), 1
    if mode == "nt":
        assert b_row0 % tn == 0
        b_spec, cb = _bs((tn, tk), lambda i, j, k: (j + b_row0 // tn, k)), 1
    else:
        assert b_row0 % tk == 0
        b_spec, cb = _bs((tk, tn), lambda i, j, k: (k + b_row0 // tk, j)), 0
    dims = (((ca,), (cb,)), ((), ()))
    o_spec = _bs((tm, tn), lambda i, j, k: (i, j))
    has_res = res is not None

    def body(*refs):
        a_ref, b_ref = refs[:2]
        r_ref = refs[2] if has_res else None
        o_ref = refs[n_in]
        k = pl.program_id(2)
        part = lax.dot_general(a_ref[...].astype(BF16), b_ref[...].astype(BF16), dims,
                               preferred_element_type=F32)
        if nk == 1:
            if has_res:
                part = part + r_ref[...].astype(F32)
            o_ref[...] = part.astype(out_dtype)
            return
        acc = refs[-1] if use_acc else o_ref

        @pl.when(k == 0)
        def _():
            acc[...] = part + r_ref[...].astype(F32) if has_res else part

        @pl.when(k > 0)
        def _():
            acc[...] += part

        if use_acc:
            @pl.when(k == nk - 1)
            def _():
                o_ref[...] = acc[...].astype(out_dtype)

    in_specs = [a_spec, b_spec] + ([o_spec] if has_res else [])
    args = (a, b) + ((res,) if has_res else ())
    if after is not None:
        in_specs.append(ANY_SPEC)
        args += (after,)
    n_in = len(args)
    return pl.pallas_call(
        body, out_shape=jax.ShapeDtypeStruct((m_dim, n_dim), out_dtype),
        grid=(m_dim // tm, n_dim // tn, nk), in_specs=in_specs, out_specs=o_spec,
        scratch_shapes=[pltpu.VMEM((tm, tn), F32)] if use_acc else [], name=name,
        compiler_params=_params("parallel", "parallel", "arbitrary"))(*args)


def _grouped(a, w, mode, name, out_dtype=F32):
    t_dim = a.shape[0]
    g_dim, r_dim, c_dim = w.shape
    ka, no = (c_dim, r_dim) if mode == "nt" else (r_dim, c_dim)
    tm = _tile(t_dim, 512)
    dims = (((1,), (1 if mode == "nt" else 0,)), ((), ()))

    def body(a_ref, w_ref, o_ref):
        o_ref[...] = lax.dot_general(a_ref[...].astype(BF16), w_ref[...].astype(BF16), dims,
                                     preferred_element_type=F32).astype(out_dtype)

    return pl.pallas_call(
        body, out_shape=jax.ShapeDtypeStruct((t_dim, g_dim * no), out_dtype),
        grid=(t_dim // tm, g_dim),
        in_specs=[_bs((tm, ka), lambda i, g: (i, g)), _bs((None, r_dim, c_dim), lambda i, g: (g, 0, 0))],
        out_specs=_bs((tm, no), lambda i, g: (i, g)), name=name,
        compiler_params=_params("parallel", "parallel"))(a, w)


def _grouped_tn(a, b, g_dim, name):
    t_dim = a.shape[0]
    ra = a.shape[1] // g_dim
    cb = b.shape[1] // g_dim
    tm = _tile(t_dim, 512)
    nt = t_dim // tm

    def body(a_ref, b_ref, o_ref):
        part = lax.dot_general(a_ref[...].astype(BF16), b_ref[...].astype(BF16), (((0,), (0,)), ((), ())),
                               preferred_element_type=F32)

        @pl.when(pl.program_id(1) == 0)
        def _():
            o_ref[...] = part

        @pl.when(pl.program_id(1) > 0)
        def _():
            o_ref[...] += part

    return pl.pallas_call(
        body, out_shape=jax.ShapeDtypeStruct((g_dim, ra, cb), F32), grid=(g_dim, nt),
        in_specs=[_bs((tm, ra), lambda g, i: (i, g)), _bs((tm, cb), lambda g, i: (i, g))],
        out_specs=_bs((None, ra, cb), lambda g, i: (g, 0, 0)), name=name,
        compiler_params=_params("parallel", "arbitrary"))(a, b)


def _rmsnorm_fwd(x, g, name):
    t_dim, d = x.shape
    tm = _tile(t_dim, 512)

    def body(x_ref, g_ref, o_ref):
        xv = x_ref[...]
        r = lax.rsqrt(jnp.mean(xv * xv, axis=-1, keepdims=True) + EPS)
        o_ref[...] = (xv * r * g_ref[...]).astype(BF16)

    return pl.pallas_call(
        body, out_shape=jax.ShapeDtypeStruct((t_dim, d), BF16), grid=(t_dim // tm,),
        in_specs=[_bs((tm, d), lambda i: (i, 0)), _bs((1, d), lambda i: (0, 0))],
        out_specs=_bs((tm, d), lambda i: (i, 0)), name=name, compiler_params=_params("parallel"))(x, g)


def _rmsnorm_bwd(x, g, dh, dx_in, name):
    t_dim, d = x.shape
    tm = _tile(t_dim, 512)
    has_in = dx_in is not None

    def body(*refs):
        if has_in:
            x_ref, g_ref, dh_ref, di_ref, dx_ref, dxb_ref, dg_ref = refs
        else:
            x_ref, g_ref, dh_ref, dx_ref, dxb_ref, dg_ref = refs
        xv = x_ref[...]
        r = lax.rsqrt(jnp.mean(xv * xv, axis=-1, keepdims=True) + EPS)
        xh = xv * r
        dhv = dh_ref[...].astype(F32)
        dxh = dhv * g_ref[...]
        dx = r * (dxh - xh * jnp.mean(dxh * xh, axis=-1, keepdims=True))
        if has_in:
            dx = dx + di_ref[...]
        dx_ref[...] = dx
        dxb_ref[...] = dx.astype(BF16)
        part = jnp.sum(dhv * xh, axis=0, keepdims=True)

        @pl.when(pl.program_id(0) == 0)
        def _():
            dg_ref[...] = part

        @pl.when(pl.program_id(0) > 0)
        def _():
            dg_ref[...] += part

    row = _bs((tm, d), lambda i: (i, 0))
    vec = _bs((1, d), lambda i: (0, 0))
    args = (x, g, dh) + ((dx_in,) if has_in else ())
    return pl.pallas_call(
        body, out_shape=(jax.ShapeDtypeStruct((t_dim, d), F32), jax.ShapeDtypeStruct((t_dim, d), BF16),
                         jax.ShapeDtypeStruct((1, d), F32)),
        grid=(t_dim // tm,), in_specs=[row, vec, row] + ([row] if has_in else []),
        out_specs=(row, row, vec), name=name, compiler_params=_params("arbitrary"))(*args)


def _matmul_rmsnorm_bwd(a, b, mode, x, g, dx_in, name, res=None, b_window=None):
    b_row0, b_rows = b_window if b_window is not None else (0, b.shape[0])
    m_dim, k_dim = a.shape
    d = x.shape[1]
    assert (b_rows, b.shape[1]) == ((k_dim, d) if mode == "nn" else (d, k_dim)), (name, a.shape, b.shape)
    tm = _mxu_tile(m_dim, 512)
    tk = _mxu_tile(k_dim, 1792)
    nk = k_dim // tk
    has_res = res is not None
    if mode == "nt":
        assert b_row0 == 0
        b_spec, cb = _bs((d, tk), lambda i, k: (0, k)), 1
    else:
        assert b_row0 % tk == 0
        b_spec, cb = _bs((tk, d), lambda i, k: (k + b_row0 // tk, 0)), 0
    dims = (((1,), (cb,)), ((), ()))

    def body(*refs):
        a_ref, b_ref = refs[:2]
        r_ref = refs[2] if has_res else None
        x_ref, g_ref, di_ref, dx_ref, dxb_ref, dg_ref = refs[2 + has_res:8 + has_res]
        i, k = pl.program_id(0), pl.program_id(1)
        part = lax.dot_general(a_ref[...].astype(BF16), b_ref[...].astype(BF16), dims, preferred_element_type=F32)

        def finish(dhv):
            if has_res:
                dhv = dhv + r_ref[...].astype(F32)
            xv = x_ref[...]
            r = lax.rsqrt(jnp.mean(xv * xv, axis=-1, keepdims=True) + EPS)
            xh = xv * r
            dxh = dhv * g_ref[...]
            dx = r * (dxh - xh * jnp.mean(dxh * xh, axis=-1, keepdims=True)) + di_ref[...]
            dx_ref[...] = dx
            dxb_ref[...] = dx.astype(BF16)
            dg_part = jnp.sum(dhv * xh, axis=0, keepdims=True)

            @pl.when(i == 0)
            def _():
                dg_ref[...] = dg_part

            @pl.when(i > 0)
            def _():
                dg_ref[...] += dg_part

        if nk == 1:
            finish(part)
            return
        acc = refs[-1]

        @pl.when(k == 0)
        def _():
            acc[...] = part

        @pl.when(jnp.logical_and(k > 0, k < nk - 1))
        def _():
            acc[...] += part

        @pl.when(k == nk - 1)
        def _():
            finish(acc[...] + part)

    row = _bs((tm, d), lambda i, k: (i, 0))
    vec = _bs((1, d), lambda i, k: (0, 0))
    in_specs = [_bs((tm, tk), lambda i, k: (i, k)), b_spec] + ([row] if has_res else []) + [row, vec, row]
    args = (a, b) + ((res,) if has_res else ()) + (x, g, dx_in)
    return pl.pallas_call(
        body, out_shape=(jax.ShapeDtypeStruct((m_dim, d), F32), jax.ShapeDtypeStruct((m_dim, d), BF16),
                         jax.ShapeDtypeStruct((1, d), F32)),
        grid=(m_dim // tm, nk), in_specs=in_specs, out_specs=(row, row, vec),
        scratch_shapes=[pltpu.VMEM((tm, d), F32)] if nk > 1 else [], name=name,
        compiler_params=_params("arbitrary", "arbitrary"))(*args)


def _loss_head(x, g, tgt, name):
    t_dim, d = x.shape
    tm = _tile(t_dim, 512)

    def body(x_ref, g_ref, t_ref, dx_ref, dxb_ref, dg_ref, loss_ref):
        xv = x_ref[...]
        gv = g_ref[...]
        r = lax.rsqrt(jnp.mean(xv * xv, axis=-1, keepdims=True) + EPS)
        xh = xv * r
        err = xh * gv - t_ref[...]
        dy = err * (1.0 / d)
        dxh = dy * gv
        dx = r * (dxh - xh * jnp.mean(dxh * xh, axis=-1, keepdims=True))
        dx_ref[...] = dx
        dxb_ref[...] = dx.astype(BF16)
        dg_part = jnp.sum(dy * xh, axis=0, keepdims=True)
        loss_part = jnp.full((1, LANE), 0.5 * jnp.sum(jnp.mean(err * err, axis=-1, keepdims=True)), F32)

        @pl.when(pl.program_id(0) == 0)
        def _():
            dg_ref[...] = dg_part
            loss_ref[...] = loss_part

        @pl.when(pl.program_id(0) > 0)
        def _():
            dg_ref[...] += dg_part
            loss_ref[...] += loss_part

    row = _bs((tm, d), lambda i: (i, 0))
    vec = _bs((1, d), lambda i: (0, 0))
    return pl.pallas_call(
        body, out_shape=(jax.ShapeDtypeStruct((t_dim, d), F32), jax.ShapeDtypeStruct((t_dim, d), BF16),
                         jax.ShapeDtypeStruct((1, d), F32), jax.ShapeDtypeStruct((1, LANE), F32)),
        grid=(t_dim // tm,), in_specs=[row, vec, row],
        out_specs=(row, row, vec, _bs((1, LANE), lambda i: (0, 0))), name=name,
        compiler_params=_params("arbitrary"))(x, g, tgt)


def _glu_conv_fwd(proj, dw_w, dw_b, n_batch, seq, name):
    kk, cc = dw_w.shape
    nj = cc // LANE
    ch = min(256, seq)

    def body(a_ref, gl_ref, w_ref, b_ref, o_ref, pad):
        pad[0:CONV_HALO, :] = jnp.zeros((CONV_HALO, LANE), F32)
        pad[CONV_HALO:CONV_HALO + seq, :] = a_ref[...].astype(F32) * _sig(gl_ref[...].astype(F32))
        for c0 in range(0, seq, ch):
            acc = jnp.broadcast_to(b_ref[...], (ch, LANE))
            for k in range(kk):
                acc = acc + w_ref[k:k + 1, :] * pad[pl.ds(c0 + CONV_HALO - (kk - 1) + k, ch), :]
            o_ref[c0:c0 + ch, :] = acc

    return pl.pallas_call(
        body, out_shape=jax.ShapeDtypeStruct((n_batch * seq, cc), F32), grid=(n_batch, nj),
        in_specs=[_bs((seq, LANE), lambda b, j: (b, j)), _bs((seq, LANE), lambda b, j: (b, nj + j)),
                  _bs((kk, LANE), lambda b, j: (0, j)), _bs((1, LANE), lambda b, j: (0, j))],
        out_specs=_bs((seq, LANE), lambda b, j: (b, j)),
        scratch_shapes=[pltpu.VMEM((seq + CONV_HALO, LANE), F32)], name=name,
        compiler_params=_params("parallel", "parallel"))(proj, proj, dw_w, dw_b)


def _glu_conv_bwd(proj, dw_w, dy1, n_batch, seq, name):
    kk, cc = dw_w.shape
    nj = cc // LANE
    ch = min(256, seq)

    def body(a_ref, gl_ref, dy_ref, w_ref, da_ref, dgl_ref, dw_ref, db_ref, padf, padb):
        first = pl.program_id(1) == 0
        padf[0:CONV_HALO, :] = jnp.zeros((CONV_HALO, LANE), F32)
        padf[CONV_HALO:CONV_HALO + seq, :] = a_ref[...].astype(F32) * _sig(gl_ref[...].astype(F32))
        padb[0:seq, :] = dy_ref[...]
        padb[seq:seq + CONV_HALO, :] = jnp.zeros((CONV_HALO, LANE), F32)

        @pl.when(first)
        def _():
            dw_ref[...] = jnp.zeros((kk, LANE), F32)
            db_ref[...] = jnp.zeros((1, LANE), F32)

        dws = [jnp.zeros((1, LANE), F32) for _ in range(kk)]
        for c0 in range(0, seq, ch):
            acc = jnp.zeros((ch, LANE), F32)
            y0 = padf[CONV_HALO + c0:CONV_HALO + c0 + ch, :]
            for k in range(kk):
                win = padb[pl.ds(c0 + (kk - 1) - k, ch), :]
                acc = acc + w_ref[k:k + 1, :] * win
                dws[k] = dws[k] + jnp.sum(win * y0, axis=0, keepdims=True)
            sg = _sig(gl_ref[c0:c0 + ch, :].astype(F32))
            da_ref[c0:c0 + ch, :] = (acc * sg).astype(BF16)
            dgl_ref[c0:c0 + ch, :] = (acc * a_ref[c0:c0 + ch, :].astype(F32) * sg * (1.0 - sg)).astype(BF16)
        for k in range(kk):
            dw_ref[k:k + 1, :] += dws[k]
        db_ref[...] += jnp.sum(dy_ref[...], axis=0, keepdims=True)

    tok = _bs((seq, LANE), lambda j, b: (b, j))
    t_dim = n_batch * seq
    return pl.pallas_call(
        body, out_shape=(jax.ShapeDtypeStruct((t_dim, cc), BF16), jax.ShapeDtypeStruct((t_dim, cc), BF16),
                         jax.ShapeDtypeStruct((kk, cc), F32), jax.ShapeDtypeStruct((1, cc), F32)),
        grid=(nj, n_batch),
        in_specs=[tok, _bs((seq, LANE), lambda j, b: (b, nj + j)), tok, _bs((kk, LANE), lambda j, b: (0, j))],
        out_specs=(tok, tok, _bs((kk, LANE), lambda j, b: (0, j)), _bs((1, LANE), lambda j, b: (0, j))),
        scratch_shapes=[pltpu.VMEM((seq + CONV_HALO, LANE), F32), pltpu.VMEM((seq + CONV_HALO, LANE), F32)],
        name=name, compiler_params=_params("parallel", "arbitrary"))(proj, proj, dy1, dw_w)


def _ln_silu_fwd(y1, g, b, name):
    t_dim, c = y1.shape
    tm = _tile(t_dim, 512)

    def body(y_ref, g_ref, b_ref, o_ref):
        yv = y_ref[...]
        xc = yv - jnp.mean(yv, axis=-1, keepdims=True)
        rstd = lax.rsqrt(jnp.mean(xc * xc, axis=-1, keepdims=True) + EPS)
        y2 = xc * rstd * g_ref[...] + b_ref[...]
        o_ref[...] = (y2 * _sig(y2)).astype(BF16)

    row = _bs((tm, c), lambda i: (i, 0))
    vec = _bs((1, c), lambda i: (0, 0))
    return pl.pallas_call(
        body, out_shape=jax.ShapeDtypeStruct((t_dim, c), BF16), grid=(t_dim // tm,),
        in_specs=[row, vec, vec], out_specs=row, name=name, compiler_params=_params("parallel"))(y1, g, b)


def _ln_silu_bwd(y1, g, b, dy3, name):
    t_dim, c = y1.shape
    tm = _tile(t_dim, 512)

    def body(y_ref, g_ref, b_ref, d_ref, dy_ref, dg_ref, db_ref):
        yv = y_ref[...]
        gv = g_ref[...]
        xc = yv - jnp.mean(yv, axis=-1, keepdims=True)
        rstd = lax.rsqrt(jnp.mean(xc * xc, axis=-1, keepdims=True) + EPS)
        yh = xc * rstd
        y2 = yh * gv + b_ref[...]
        s = _sig(y2)
        dy2 = d_ref[...].astype(F32) * (s * (1.0 + y2 * (1.0 - s)))
        dyh = dy2 * gv
        dy_ref[...] = rstd * (dyh - jnp.mean(dyh, axis=-1, keepdims=True)
                              - yh * jnp.mean(dyh * yh, axis=-1, keepdims=True))
        dg_part = jnp.sum(dy2 * yh, axis=0, keepdims=True)
        db_part = jnp.sum(dy2, axis=0, keepdims=True)

        @pl.when(pl.program_id(0) == 0)
        def _():
            dg_ref[...] = dg_part
            db_ref[...] = db_part

        @pl.when(pl.program_id(0) > 0)
        def _():
            dg_ref[...] += dg_part
            db_ref[...] += db_part

    row = _bs((tm, c), lambda i: (i, 0))
    vec = _bs((1, c), lambda i: (0, 0))
    return pl.pallas_call(
        body, out_shape=(jax.ShapeDtypeStruct((t_dim, c), F32), jax.ShapeDtypeStruct((1, c), F32),
                         jax.ShapeDtypeStruct((1, c), F32)),
        grid=(t_dim // tm,), in_specs=[row, vec, vec, row], out_specs=(row, vec, vec), name=name,
        compiler_params=_params("arbitrary"))(y1, g, b, dy3)


def _pool_fwd(proj, col0, n_groups, n_batch, seq, name):
    ch = min(256, seq)

    def body(u_ref, o_ref, pad):
        w = lax.shift_left(jnp.int32(2), pl.program_id(1))
        pad[0:POOL_HALO, :] = jnp.zeros((POOL_HALO, LANE), F32)
        pad[POOL_HALO:POOL_HALO + seq, :] = u_ref[...].astype(F32)
        for c0 in range(0, seq, ch):
            acc = jnp.zeros((ch, LANE), F32)
            for j in range(POOL_WINDOW_MAX):
                acc = acc + jnp.where(j < w, 1.0, 0.0).astype(F32) * pad[pl.ds(c0 + POOL_HALO - j, ch), :]
            t = c0 + lax.broadcasted_iota(jnp.int32, (ch, LANE), 0)
            cnt = jnp.minimum(t + 1, w).astype(F32)
            o_ref[c0:c0 + ch, :] = (acc / cnt - pad[POOL_HALO + c0:POOL_HALO + c0 + ch, :]).astype(BF16)

    return pl.pallas_call(
        body, out_shape=jax.ShapeDtypeStruct((n_batch * seq, n_groups * LANE), BF16), grid=(n_batch, n_groups),
        in_specs=[_bs((seq, LANE), lambda b, g: (b, col0 + g))], out_specs=_bs((seq, LANE), lambda b, g: (b, g)),
        scratch_shapes=[pltpu.VMEM((seq + POOL_HALO, LANE), F32)], name=name,
        compiler_params=_params("parallel", "parallel"))(proj)


def _pool_bwd(dzp, n_groups, n_batch, seq, name):
    ch = min(256, seq)

    def body(d_ref, o_ref, pad):
        w = lax.shift_left(jnp.int32(2), pl.program_id(1))
        for c0 in range(0, seq, ch):
            t = c0 + lax.broadcasted_iota(jnp.int32, (ch, LANE), 0)
            cnt = jnp.minimum(t + 1, w).astype(F32)
            pad[c0:c0 + ch, :] = d_ref[c0:c0 + ch, :] / cnt
        pad[seq:seq + POOL_HALO, :] = jnp.zeros((POOL_HALO, LANE), F32)
        for c0 in range(0, seq, ch):
            acc = jnp.zeros((ch, LANE), F32)
            for j in range(POOL_WINDOW_MAX):
                acc = acc + jnp.where(j < w, 1.0, 0.0).astype(F32) * pad[pl.ds(c0 + j, ch), :]
            o_ref[c0:c0 + ch, :] = (acc - d_ref[c0:c0 + ch, :]).astype(BF16)

    tok = _bs((seq, LANE), lambda b, g: (b, g))
    return pl.pallas_call(
        body, out_shape=jax.ShapeDtypeStruct((n_batch * seq, n_groups * LANE), BF16), grid=(n_batch, n_groups),
        in_specs=[tok], out_specs=tok, scratch_shapes=[pltpu.VMEM((seq + POOL_HALO, LANE), F32)], name=name,
        compiler_params=_params("parallel", "parallel"))(dzp)


def _merge_fwd(proj, col0, yc, yp, scale, name):
    t_dim, d = yc.shape
    half = d // 2
    tm = _tile(t_dim, 512)
    c0 = col0 // half

    def body(gc_ref, gp_ref, yc_ref, yp_ref, s_ref, o_ref):
        f32 = lambda r: r[...].astype(F32)
        o_ref[...] = (_sig(f32(gc_ref)) * f32(yc_ref) + _sig(f32(gp_ref)) * (f32(yp_ref) * s_ref[...])).astype(BF16)

    blk = _bs((tm, half), lambda i, j: (i, j))
    return pl.pallas_call(
        body, out_shape=jax.ShapeDtypeStruct((t_dim, d), BF16), grid=(t_dim // tm, 2),
        in_specs=[_bs((tm, half), lambda i, j: (i, c0 + j)), _bs((tm, half), lambda i, j: (i, c0 + 2 + j)),
                  blk, blk, _bs((1, half), lambda i, j: (0, j))],
        out_specs=blk, name=name, compiler_params=_params("parallel", "parallel"))(proj, proj, yc, yp, scale)


def _merge_bwd(proj, col0, yc, yp, scale, dm, name):
    t_dim, d = yc.shape
    half = d // 2
    tm = _tile(t_dim, 512)
    c0 = col0 // half

    def body(gc_ref, gp_ref, yc_ref, yp_ref, s_ref, dm_ref, dgc_ref, dgp_ref, dyc_ref, dyp_ref, ds_ref):
        dmv = dm_ref[...].astype(F32)
        sgc = _sig(gc_ref[...].astype(F32))
        sgp = _sig(gp_ref[...].astype(F32))
        sv = s_ref[...]
        ypre = yp_ref[...].astype(F32)
        dgc_ref[...] = (dmv * yc_ref[...].astype(F32) * sgc * (1.0 - sgc)).astype(BF16)
        dgp_ref[...] = (dmv * (ypre * sv) * sgp * (1.0 - sgp)).astype(BF16)
        dyc_ref[...] = (dmv * sgc).astype(BF16)
        dyp = dmv * sgp
        dyp_ref[...] = (dyp * sv).astype(BF16)
        part = jnp.sum(dyp * ypre, axis=0, keepdims=True)

        @pl.when(pl.program_id(1) == 0)
        def _():
            ds_ref[...] = part

        @pl.when(pl.program_id(1) > 0)
        def _():
            ds_ref[...] += part

    blk = _bs((tm, half), lambda j, i: (i, j))
    big = jax.ShapeDtypeStruct((t_dim, d), BF16)
    return pl.pallas_call(
        body, out_shape=(big, big, big, big, jax.ShapeDtypeStruct((1, d), F32)), grid=(2, t_dim // tm),
        in_specs=[_bs((tm, half), lambda j, i: (i, c0 + j)), _bs((tm, half), lambda j, i: (i, c0 + 2 + j)),
                  blk, blk, _bs((1, half), lambda j, i: (0, j)), blk],
        out_specs=(blk, blk, blk, blk, _bs((1, half), lambda j, i: (0, j))), name=name,
        compiler_params=_params("parallel", "arbitrary"))(proj, proj, yc, yp, scale, dm)


def _attn_fwd(q, kv, n_batch, seq, m_len, name):
    d = q.shape[1]
    hd = d // XA_HEADS
    tq = _tile(seq, 1024)
    nq = seq // tq
    scale = hd ** -0.5

    def body(q_ref, k_ref, v_ref, o_ref):
        sc = lax.dot_general(q_ref[...].astype(BF16), k_ref[...].astype(BF16), (((1,), (1,)), ((), ())),
                             preferred_element_type=F32) * scale
        p = jnp.exp(sc - jnp.max(sc, axis=-1, keepdims=True))
        pr = p / jnp.sum(p, axis=-1, keepdims=True)
        o_ref[...] = jnp.dot(pr.astype(BF16), v_ref[...].astype(BF16), preferred_element_type=F32).astype(BF16)

    return pl.pallas_call(
        body, out_shape=jax.ShapeDtypeStruct((n_batch * seq, d), BF16), grid=(n_batch, XA_HEADS, nq),
        in_specs=[_bs((tq, hd), lambda b, h, i: (b * nq + i, h)), _bs((m_len, hd), lambda b, h, i: (b, h)),
                  _bs((m_len, hd), lambda b, h, i: (b, XA_HEADS + h))],
        out_specs=_bs((tq, hd), lambda b, h, i: (b * nq + i, h)), name=name,
        compiler_params=_params("parallel", "parallel", "parallel"))(q, kv, kv)


def _attn_bwd(q, kv, datt, n_batch, seq, m_len, name):
    d = q.shape[1]
    hd = d // XA_HEADS
    tq = _tile(seq, 1024)
    nq = seq // tq
    scale = hd ** -0.5

    def body(q_ref, k_ref, v_ref, do_ref, dq_ref, dk_ref, dv_ref):
        qb = q_ref[...].astype(BF16)
        kb = k_ref[...].astype(BF16)
        vb = v_ref[...].astype(BF16)
        dob = do_ref[...].astype(BF16)
        sc = lax.dot_general(qb, kb, (((1,), (1,)), ((), ())), preferred_element_type=F32) * scale
        p = jnp.exp(sc - jnp.max(sc, axis=-1, keepdims=True))
        pr = p / jnp.sum(p, axis=-1, keepdims=True)
        dpr = lax.dot_general(dob, vb, (((1,), (1,)), ((), ())), preferred_element_type=F32)
        dsc = pr * (dpr - jnp.sum(dpr * pr, axis=-1, keepdims=True)) * scale
        dsb = dsc.astype(BF16)
        dq_ref[...] = jnp.dot(dsb, kb, preferred_element_type=F32).astype(BF16)
        dv_part = lax.dot_general(pr.astype(BF16), dob, (((0,), (0,)), ((), ())), preferred_element_type=F32)
        dk_part = lax.dot_general(dsb, qb, (((0,), (0,)), ((), ())), preferred_element_type=F32)

        @pl.when(pl.program_id(2) == 0)
        def _():
            dk_ref[...] = dk_part
            dv_ref[...] = dv_part

        @pl.when(pl.program_id(2) > 0)
        def _():
            dk_ref[...] += dk_part
            dv_ref[...] += dv_part

    qs = _bs((tq, hd), lambda b, h, i: (b * nq + i, h))
    ks = _bs((m_len, hd), lambda b, h, i: (b, h))
    return pl.pallas_call(
        body, out_shape=(jax.ShapeDtypeStruct((n_batch * seq, d), BF16), jax.ShapeDtypeStruct((n_batch * m_len, d), F32),
                         jax.ShapeDtypeStruct((n_batch * m_len, d), F32)),
        grid=(n_batch, XA_HEADS, nq),
        in_specs=[qs, ks, _bs((m_len, hd), lambda b, h, i: (b, XA_HEADS + h)), qs],
        out_specs=(qs, ks, ks), name=name,
        compiler_params=_params("parallel", "parallel", "arbitrary"))(q, kv, kv, datt)


def _gelu_parts(g):
    th = jnp.tanh(GELU_C0 * (g + GELU_C1 * g * g * g))
    return th, 0.5 * g * (1.0 + th)


def _ffn_act_fwd(up_g, up_v, dw_w, n_batch, seq, name):
    kk, c2 = dw_w.shape
    f_dim = c2 // 2
    wd = 2 * LANE
    nj = f_dim // wd
    ch = min(128, seq)

    def body(g_ref, v_ref, wg_ref, wv_ref, o_ref, padg, padv):
        for pad, src in ((padg, g_ref), (padv, v_ref)):
            pad[0:FFN_HALO, :] = jnp.zeros((FFN_HALO, wd), F32)
            pad[FFN_HALO:FFN_HALO + seq, :] = src[...].astype(F32)
        for c0 in range(0, seq, ch):
            gate = jnp.zeros((ch, wd), F32)
            val = jnp.zeros((ch, wd), F32)
            for k in range(kk):
                off = c0 + FFN_HALO - (kk - 1) + k
                gate = gate + wg_ref[k:k + 1, :] * padg[pl.ds(off, ch), :]
                val = val + wv_ref[k:k + 1, :] * padv[pl.ds(off, ch), :]
            o_ref[c0:c0 + ch, :] = (_gelu_parts(gate)[1] * val).astype(BF16)

    return pl.pallas_call(
        body, out_shape=jax.ShapeDtypeStruct((n_batch * seq, f_dim), BF16), grid=(n_batch, nj),
        in_specs=[_bs((seq, wd), lambda b, j: (b, j)), _bs((seq, wd), lambda b, j: (b, j)),
                  _bs((kk, wd), lambda b, j: (0, j)), _bs((kk, wd), lambda b, j: (0, nj + j))],
        out_specs=_bs((seq, wd), lambda b, j: (b, j)),
        scratch_shapes=[pltpu.VMEM((seq + FFN_HALO, wd), F32), pltpu.VMEM((seq + FFN_HALO, wd), F32)], name=name,
        compiler_params=_params("parallel", "parallel"))(up_g, up_v, dw_w, dw_w)


def _ffn_act_bwd(up_g, up_v, dw_w, dact, n_batch, seq, name):
    kk, c2 = dw_w.shape
    f_dim = c2 // 2
    wd = 2 * LANE
    nj = f_dim // wd
    ch = min(128, seq)

    def body(g_ref, v_ref, wg_ref, wv_ref, da_ref, dg_ref, dv_ref, dwg_ref, dwv_ref, padg, padv, pbg, pbv):
        for pad, src in ((padg, g_ref), (padv, v_ref)):
            pad[0:FFN_HALO, :] = jnp.zeros((FFN_HALO, wd), F32)
            pad[FFN_HALO:FFN_HALO + seq, :] = src[...].astype(F32)
        for pb in (pbg, pbv):
            pb[seq:seq + FFN_HALO, :] = jnp.zeros((FFN_HALO, wd), F32)

        @pl.when(pl.program_id(1) == 0)
        def _():
            dwg_ref[...] = jnp.zeros((kk, wd), F32)
            dwv_ref[...] = jnp.zeros((kk, wd), F32)

        for c0 in range(0, seq, ch):
            gate = jnp.zeros((ch, wd), F32)
            val = jnp.zeros((ch, wd), F32)
            for k in range(kk):
                off = c0 + FFN_HALO - (kk - 1) + k
                gate = gate + wg_ref[k:k + 1, :] * padg[pl.ds(off, ch), :]
                val = val + wv_ref[k:k + 1, :] * padv[pl.ds(off, ch), :]
            th, gelu = _gelu_parts(gate)
            dgelu = 0.5 * (1.0 + th) + 0.5 * gate * (1.0 - th * th) * GELU_C0 * (1.0 + 3.0 * GELU_C1 * gate * gate)
            dav = da_ref[c0:c0 + ch, :].astype(F32)
            pbg[c0:c0 + ch, :] = dav * val * dgelu
            pbv[c0:c0 + ch, :] = dav * gelu
        for pb, pad, w_ref, d_ref, dw_ref in ((pbg, padg, wg_ref, dg_ref, dwg_ref), (pbv, padv, wv_ref, dv_ref, dwv_ref)):
            dws = [jnp.zeros((1, wd), F32) for _ in range(kk)]
            for c0 in range(0, seq, ch):
                acc = jnp.zeros((ch, wd), F32)
                u0 = pad[FFN_HALO + c0:FFN_HALO + c0 + ch, :]
                for k in range(kk):
                    win = pb[pl.ds(c0 + (kk - 1) - k, ch), :]
                    acc = acc + w_ref[k:k + 1, :] * win
                    dws[k] = dws[k] + jnp.sum(win * u0, axis=0, keepdims=True)
                d_ref[c0:c0 + ch, :] = acc.astype(BF16)
            for k in range(kk):
                dw_ref[k:k + 1, :] += dws[k]

    t_dim = n_batch * seq
    tok = _bs((seq, wd), lambda j, b: (b, j))
    wblk = _bs((kk, wd), lambda j, b: (0, j))
    pad_shape = pltpu.VMEM((seq + FFN_HALO, wd), F32)
    return pl.pallas_call(
        body, out_shape=(jax.ShapeDtypeStruct((t_dim, f_dim), BF16), jax.ShapeDtypeStruct((t_dim, f_dim), BF16),
                         jax.ShapeDtypeStruct((kk, f_dim), F32), jax.ShapeDtypeStruct((kk, f_dim), F32)),
        grid=(nj, n_batch),
        in_specs=[tok, tok, wblk, _bs((kk, wd), lambda j, b: (0, nj + j)), tok],
        out_specs=(tok, tok, wblk, wblk), scratch_shapes=[pad_shape, pad_shape, pad_shape, pad_shape], name=name,
        compiler_params=_params("parallel", "arbitrary"))(up_g, up_v, dw_w, dw_w, dact)


def _sum_rows(parts, out_dtype, name):
    r_dim, c_dim = parts[0].shape
    tr = _tile(r_dim, 1200, SUBLANE)
    n = len(parts)

    def body(*refs):
        acc = refs[0][...].astype(F32)
        for r in refs[1:n]:
            acc = acc + r[...].astype(F32)
        refs[n][...] = acc.astype(out_dtype)

    blk = _bs((tr, c_dim), lambda i: (i, 0))
    return pl.pallas_call(
        body, out_shape=jax.ShapeDtypeStruct((r_dim, c_dim), out_dtype), grid=(r_dim // tr,),
        in_specs=[blk] * n, out_specs=blk, name=name, compiler_params=_params("parallel"))(*parts)


def _adamw(w, g, m, v, name):
    shape = w.shape
    c_dim = shape[-1]
    r_dim = w.size // c_dim
    two_d = lambda t: t.reshape(r_dim, c_dim)
    tr = _tile(r_dim, max(SUBLANE, (256 * 1024) // max(c_dim, LANE) // SUBLANE * SUBLANE), SUBLANE)
    c1 = 1.0 - ADAM_B1 ** ADAM_STEP
    c2 = 1.0 - ADAM_B2 ** ADAM_STEP

    def body(w_ref, g_ref, m_ref, v_ref, d_ref, mo_ref, vo_ref):
        gv = g_ref[...]
        mn = ADAM_B1 * m_ref[...] + (1.0 - ADAM_B1) * gv
        vn = ADAM_B2 * v_ref[...] + (1.0 - ADAM_B2) * (gv * gv)
        mo_ref[...] = mn
        vo_ref[...] = vn
        d_ref[...] = -ADAM_LR * ((mn / c1) / (jnp.sqrt(vn / c2) + ADAM_EPS) + ADAM_WD * w_ref[...])

    blk = _bs((tr, c_dim), lambda i: (i, 0))
    out = jax.ShapeDtypeStruct((r_dim, c_dim), F32)
    d, mo, vo = pl.pallas_call(
        body, out_shape=(out, out, out), grid=(r_dim // tr,), in_specs=[blk] * 4, out_specs=(blk, blk, blk),
        name=name, compiler_params=_params("parallel"))(two_d(w), two_d(g), two_d(m), two_d(v))
    return d.reshape(shape), mo.reshape(shape), vo.reshape(shape)


HBM_SPEC = pl.BlockSpec(memory_space=pltpu.HBM)


def _position():
    return lax.axis_index("x"), lax.axis_index("y"), lax.axis_index("c")


def _all_gather(shard, name):
    def body(x_ref, out_ref, send_sems, recv_sems, local_sem):
        x, y, c = _position()
        me, sibling = (x, y, c), (x, y, 1 - c)
        chips = [(1 - x, y), (x, 1 - y), (1 - x, 1 - y)]

        def rows(px, py, pc):
            return out_ref.at[4 * px + 2 * py + pc]

        def copy(k, block, to, src=None):
            return pltpu.make_async_remote_copy(
                src_ref=rows(*block) if src is None else src, dst_ref=rows(*block),
                send_sem=send_sems.at[k], recv_sem=recv_sems.at[k], device_id=to, device_id_type=MESH)

        mine = pltpu.make_async_copy(x_ref, rows(*me), local_sem)
        mine.start()
        first = [copy(0, me, sibling, src=x_ref)]
        first += [copy(1 + j, me, (*chip, c), src=x_ref) for j, chip in enumerate(chips)]
        for cp in first:
            cp.start()
        passed = [copy(4 + j, (*chip, c), sibling) for j, chip in enumerate(chips)]
        for j, chip in enumerate(chips):
            copy(1 + j, (*chip, c), me).wait_recv()
            passed[j].start()
        copy(0, sibling, me).wait_recv()
        for j, chip in enumerate(chips):
            copy(4 + j, (*chip, 1 - c), me).wait_recv()
        for cp in first + passed:
            cp.wait_send()
        mine.wait()

    return pl.pallas_call(
        body, out_shape=jax.ShapeDtypeStruct((N_DEV,) + shard.shape, shard.dtype),
        in_specs=[HBM_SPEC], out_specs=HBM_SPEC,
        scratch_shapes=[pltpu.SemaphoreType.DMA((7,)), pltpu.SemaphoreType.DMA((7,)), pltpu.SemaphoreType.DMA(())],
        name=name)(shard)


CHIP_RELATIONS = ((0, 0), (1, 0), (0, 1), (1, 1))


def _rs_pair_exchange(g, name):
    _, r_dim, c_dim = g.shape
    n = len(CHIP_RELATIONS)

    def body(g_ref, recv_ref, send_sems, recv_sems):
        x, y, c = _position()
        sibling = (x, y, 1 - c)
        copies = []
        for k, (rx, ry) in enumerate(CHIP_RELATIONS):
            px = x + rx - 2 * x * rx
            py = y + ry - 2 * y * ry
            copies.append(pltpu.make_async_remote_copy(
                src_ref=g_ref.at[4 * px + 2 * py + 1 - c], dst_ref=recv_ref.at[k], send_sem=send_sems.at[k],
                recv_sem=recv_sems.at[k], device_id=sibling, device_id_type=MESH))
        for cp in copies:
            cp.start()
        for cp in copies:
            cp.wait()

    return pl.pallas_call(
        body, out_shape=jax.ShapeDtypeStruct((n, r_dim, c_dim), g.dtype), in_specs=[HBM_SPEC], out_specs=HBM_SPEC,
        scratch_shapes=[pltpu.SemaphoreType.DMA((n,)), pltpu.SemaphoreType.DMA((n,))], name=name)(g)


def _rs_pair_sum(g, recv, name):
    _, r_dim, c_dim = g.shape
    n = len(CHIP_RELATIONS)
    tr = _tile(r_dim, 1200, SUBLANE)
    x, y, c = _position()
    own = jnp.stack([4 * (x + rx - 2 * x * rx) + 2 * (y + ry - 2 * y * ry) + c for rx, ry in CHIP_RELATIONS])

    def body(own_ref, g_ref, r_ref, o_ref):
        o_ref[...] = (g_ref[...].astype(F32) + r_ref[...].astype(F32)).astype(o_ref.dtype)

    blk = _bs((None, tr, c_dim), lambda k, i, own_ref: (k, i, 0))
    return pl.pallas_call(
        body, out_shape=jax.ShapeDtypeStruct((n, r_dim, c_dim), g.dtype),
        grid_spec=pltpu.PrefetchScalarGridSpec(
            num_scalar_prefetch=1, grid=(n, r_dim // tr),
            in_specs=[_bs((None, tr, c_dim), lambda k, i, own_ref: (own_ref[k], i, 0)), blk], out_specs=blk),
        name=name, compiler_params=_params("parallel", "parallel"))(own.astype(jnp.int32), g, recv)


SEM_SPEC = pl.BlockSpec(memory_space=pltpu.SEMAPHORE)
DATAFLOW = pltpu.SideEffectType.DATAFLOW_SIDE_EFFECTING
CHIP_FLIPS = CHIP_RELATIONS[1:]
TOKEN = jax.ShapeDtypeStruct((SUBLANE, LANE), F32)


def _flip(v, r):
    return v + r - 2 * v * r


def _chip_copies(src_ref, src_of, dst_ref, dst_of, send_sems, recv_sems):
    x, y, c = _position()
    me = 4 * x + 2 * y + c
    out = []
    for k, (rx, ry) in enumerate(CHIP_FLIPS):
        px, py = _flip(x, rx), _flip(y, ry)
        peer = 4 * px + 2 * py + c
        out.append(pltpu.make_async_remote_copy(
            src_ref=src_ref.at[src_of(k, me, peer)], dst_ref=dst_ref.at[dst_of(k, me, peer)],
            send_sem=send_sems.at[k], recv_sem=recv_sems.at[k], device_id=(px, py, c), device_id_type=MESH))
    return out


def _device_block(ref, spec, d):
    rows, axis = spec
    return ref.at[pl.ds(d * rows, rows)] if axis == 0 else ref.at[:, pl.ds(d * rows, rows)]


def _ag_chips_start(lands, specs, after, name):
    n = len(lands)
    nf = len(CHIP_FLIPS)

    def body(*refs):
        send_sems, recv_sems, token = refs[n + 1], refs[n + 2], refs[-1]
        x, y, c = _position()
        me = 4 * x + 2 * y + c
        for i, spec in enumerate(specs):
            blk = _device_block(refs[i], spec, me)
            for k, (rx, ry) in enumerate(CHIP_FLIPS):
                pltpu.make_async_remote_copy(
                    src_ref=blk, dst_ref=blk, send_sem=send_sems.at[nf * i + k], recv_sem=recv_sems.at[nf * i + k],
                    device_id=(_flip(x, rx), _flip(y, ry), c), device_id_type=MESH).start()
        token[...] = jnp.zeros(TOKEN.shape, TOKEN.dtype)

    sems = pltpu.SemaphoreType.DMA((nf * n,))
    return pl.pallas_call(
        body, name=name, out_shape=(sems, sems, *[pltpu.HBM(t.shape, t.dtype) for t in lands], TOKEN),
        in_specs=(HBM_SPEC,) * n + (ANY_SPEC,),
        out_specs=(SEM_SPEC, SEM_SPEC) + (HBM_SPEC,) * n + (pl.BlockSpec(memory_space=pltpu.VMEM),),
        input_output_aliases={i: 2 + i for i in range(n)}, compiler_params=pltpu.CompilerParams(has_side_effects=DATAFLOW),
    )(*[pltpu.with_memory_space_constraint(t, pltpu.HBM) for t in lands], after)


def _ag_chips_wait(send_sems, recv_sems, lands, specs, after, name):
    n = len(lands)
    nf = len(CHIP_FLIPS)

    def body(*refs):
        send_sems, recv_sems = refs[n], refs[n + 1]
        x, y, c = _position()
        me = 4 * x + 2 * y + c
        for i, spec in enumerate(specs):
            for k, (rx, ry) in enumerate(CHIP_FLIPS):
                px, py = _flip(x, rx), _flip(y, ry)
                cp = pltpu.make_async_remote_copy(
                    src_ref=_device_block(refs[i], spec, me), dst_ref=_device_block(refs[i], spec, 4 * px + 2 * py + c),
                    send_sem=send_sems.at[nf * i + k], recv_sem=recv_sems.at[nf * i + k],
                    device_id=(px, py, c), device_id_type=MESH)
                cp.wait_send()
                cp.wait_recv()

    return pl.pallas_call(
        body, name=name, out_shape=tuple(pltpu.HBM(t.shape, t.dtype) for t in lands),
        in_specs=(HBM_SPEC,) * n + (SEM_SPEC, SEM_SPEC, ANY_SPEC), out_specs=(HBM_SPEC,) * n,
        input_output_aliases={i: i for i in range(n)}, compiler_params=pltpu.CompilerParams(has_side_effects=DATAFLOW),
    )(*lands, send_sems, recv_sems, after)


def _ag_pair_forward(lands, specs, name):
    n = len(lands)
    nr = len(CHIP_RELATIONS)

    def body(*refs):
        outs, send_sems, recv_sems = refs[n:2 * n], refs[2 * n], refs[2 * n + 1]
        x, y, c = _position()
        copies = []
        for i, spec in enumerate(specs):
            for k, (rx, ry) in enumerate(CHIP_RELATIONS):
                chip = 4 * _flip(x, rx) + 2 * _flip(y, ry)
                held = _device_block(outs[i], spec, chip + c)
                sems = dict(send_sem=send_sems.at[nr * i + k], recv_sem=recv_sems.at[nr * i + k],
                            device_id=(x, y, 1 - c), device_id_type=MESH)
                mine = pltpu.make_async_remote_copy(src_ref=held, dst_ref=held, **sems)
                theirs = pltpu.make_async_remote_copy(src_ref=held, dst_ref=_device_block(outs[i], spec, chip + 1 - c), **sems)
                copies.append((mine, theirs))
        for mine, _ in copies:
            mine.start()
        for mine, theirs in copies:
            mine.wait_send()
            theirs.wait_recv()

    sems = pltpu.SemaphoreType.DMA((nr * n,))
    return pl.pallas_call(
        body, out_shape=tuple(jax.ShapeDtypeStruct(t.shape, t.dtype) for t in lands), in_specs=[HBM_SPEC] * n,
        out_specs=(HBM_SPEC,) * n, input_output_aliases={i: i for i in range(n)}, scratch_shapes=[sems, sems], name=name)(*lands)


def _rs_chips_start(pair, name):
    _, r_dim, c_dim = pair.shape
    n = len(CHIP_FLIPS)

    def body(pair_ref, far_ref, send_sems, recv_sems, pair_thru, far_thru, token):
        for cp in _chip_copies(pair_ref, lambda k, me, peer: k + 1, far_ref, lambda k, me, peer: k, send_sems, recv_sems):
            cp.start()
        token[...] = jnp.zeros(TOKEN.shape, TOKEN.dtype)

    far = lax.empty((n, r_dim, c_dim), pair.dtype)
    return pl.pallas_call(
        body, name=name,
        out_shape=(pltpu.SemaphoreType.DMA((n,)), pltpu.SemaphoreType.DMA((n,)), pltpu.HBM(pair.shape, pair.dtype),
                   pltpu.HBM(far.shape, far.dtype), TOKEN),
        in_specs=(HBM_SPEC, HBM_SPEC),
        out_specs=(SEM_SPEC, SEM_SPEC, HBM_SPEC, HBM_SPEC, pl.BlockSpec(memory_space=pltpu.VMEM)),
        input_output_aliases={0: 2, 1: 3}, compiler_params=pltpu.CompilerParams(has_side_effects=DATAFLOW),
    )(pltpu.with_memory_space_constraint(pair, pltpu.HBM), pltpu.with_memory_space_constraint(far, pltpu.HBM))


def _rs_chips_wait(send_sems, recv_sems, pair, far, after, name):
    def body(pair_ref, far_ref, send_sems, recv_sems, after_ref, pair_out, far_out):
        for cp in _chip_copies(pair_ref, lambda k, me, peer: k + 1, far_ref, lambda k, me, peer: k, send_sems, recv_sems):
            cp.wait_send()
            cp.wait_recv()

    return pl.pallas_call(
        body, name=name, out_shape=(pltpu.HBM(pair.shape, pair.dtype), pltpu.HBM(far.shape, far.dtype)),
        in_specs=(HBM_SPEC, HBM_SPEC, SEM_SPEC, SEM_SPEC, ANY_SPEC),
        out_specs=(HBM_SPEC, HBM_SPEC), input_output_aliases={0: 0, 1: 1},
        compiler_params=pltpu.CompilerParams(has_side_effects=DATAFLOW),
    )(pair, far, send_sems, recv_sems, after)


def _rs_final_sum(pair, far, name):
    _, r_dim, c_dim = pair.shape
    tr = _tile(r_dim, 1200, SUBLANE)

    def body(p_ref, f0_ref, f1_ref, f2_ref, o_ref):
        o_ref[...] = ((p_ref[...].astype(F32) + f0_ref[...].astype(F32)) + f1_ref[...].astype(F32)) + f2_ref[...].astype(F32)

    def slot(k):
        return _bs((None, tr, c_dim), lambda i: (k, i, 0))

    return pl.pallas_call(
        body, out_shape=jax.ShapeDtypeStruct((r_dim, c_dim), F32), grid=(r_dim // tr,),
        in_specs=[slot(0), slot(0), slot(1), slot(2)], out_specs=_bs((tr, c_dim), lambda i: (i, 0)), name=name,
        compiler_params=_params("parallel"))(pair, far, far, far)


def _reduce_scatter_begin(g, name):
    recv = _rs_pair_exchange(g, name + "_pair")
    pair = _rs_pair_sum(g, recv, name + "_pairsum")
    return _rs_chips_start(pair, name + "_chips_start")


def _reduce_scatter_end(state, after, name):
    send_sems, recv_sems, pair, far, _ = state
    pair, far = _rs_chips_wait(send_sems, recv_sems, pair, far, after, name + "_chips_wait")
    return _rs_final_sum(pair, far, name + "_sum")


MATRICES = (("w_in", True), ("w_out", False), ("w_q", False), ("w_kv", True), ("w_o", False), ("w_up", True),
            ("w_down", False), ("w_conv_out", True), ("w_pool_grp", True))
PARTS = (("mix", ("w_in", "w_conv_out", "w_pool_grp", "w_out")), ("rest", ("w_q", "w_kv", "w_o", "w_up", "w_down")))


def _to_rows(name, transposed, w, d_model):
    if name == "w_pool_grp":
        w = jnp.swapaxes(w, 1, 2)
    elif transposed:
        w = w.T
    return w.reshape(-1, d_model)


def _from_rows(name, transposed, rows, shard_shape):
    if name == "w_pool_grp":
        g, i, o = shard_shape
        return jnp.swapaxes(rows.reshape(g, o, i), 1, 2)
    if transposed:
        return rows.reshape(shard_shape[1], shard_shape[0]).T
    return rows.reshape(shard_shape)


def _scatter_blocks(name, full, shard_shape, d_model, n_dev=N_DEV):
    if name == "w_pool_grp":
        g, i, o = shard_shape
        return jnp.swapaxes(full.reshape(g, n_dev, o, i), 0, 1).reshape(n_dev, -1, d_model)
    return full.reshape(n_dev, -1, d_model)


def kernel(x, mem, mix_norm_g, w_in, conv_dw_w, conv_dw_b, conv_ln_g, conv_ln_b, w_conv_out, w_pool_grp, pool_scale, w_out, xattn_norm_g, mem_norm_g, w_q, w_kv, w_o, ffn_norm_g, w_up, ffn_dw_w, w_down, final_norm_g, loss_target, m_mix_norm_g, m_w_in, m_conv_dw_w, m_conv_dw_b, m_conv_ln_g, m_conv_ln_b, m_w_conv_out, m_w_pool_grp, m_pool_scale, m_w_out, m_xattn_norm_g, m_mem_norm_g, m_w_q, m_w_kv, m_w_o, m_ffn_norm_g, m_w_up, m_ffn_dw_w, m_w_down, m_final_norm_g, v_mix_norm_g, v_w_in, v_conv_dw_w, v_conv_dw_b, v_conv_ln_g, v_conv_ln_b, v_w_conv_out, v_w_pool_grp, v_pool_scale, v_w_out, v_xattn_norm_g, v_mem_norm_g, v_w_q, v_w_kv, v_w_o, v_ffn_norm_g, v_w_up, v_ffn_dw_w, v_w_down, v_final_norm_g):
    p = dict(locals())
    weight_names = ["mix_norm_g", "w_in", "conv_dw_w", "conv_dw_b", "conv_ln_g", "conv_ln_b", "w_conv_out",
                    "w_pool_grp", "pool_scale", "w_out", "xattn_norm_g", "mem_norm_g", "w_q", "w_kv", "w_o",
                    "ffn_norm_g", "w_up", "ffn_dw_w", "w_down", "final_norm_g"]
    n_batch, seq, d_model = x.shape
    m_len = mem.shape[1]
    depth = w_in.shape[0]
    assert depth == 2, "the exchange schedule below is written for two layers"
    t_dim = n_batch * seq
    c_conv = conv_dw_b.shape[1]
    n_groups = w_pool_grp.shape[1]
    assert w_pool_grp.shape[2] == LANE and c_conv % LANE == 0 and n_groups * LANE == c_conv
    gate_col0 = 2 * c_conv + n_groups * LANE
    pool_col0 = (2 * c_conv) // LANE

    dev = 4 * lax.axis_index("x") + 2 * lax.axis_index("y") + lax.axis_index("c")
    filt = jnp.concatenate([conv_dw_w.reshape(-1), ffn_dw_w.reshape(-1)])
    filt_rows = lax.bitcast_convert_type(filt, BF16).reshape(-1, d_model)
    transposed = dict(MATRICES)
    layout = {part: [(name, transposed[name], _to_rows(name, transposed[name], p[name][0], d_model).shape[0])
                     for name in names] for part, names in PARTS}

    def landing(name, shard):
        if name == "w_pool_grp":
            block, axis = jnp.swapaxes(shard, 1, 2), 1
        elif name == "filt":
            block, axis = shard, 0
        else:
            block, axis = (shard.T if transposed[name] else shard), 0
        block = block.astype(BF16)
        rows = block.shape[axis]
        shape = block.shape[:axis] + (N_DEV * rows,) + block.shape[axis + 1:]
        start = (0,) * axis + (dev * rows,) + (0,) * (block.ndim - axis - 1)
        return lax.dynamic_update_slice(lax.empty(shape, BF16), block, start), (rows, axis)

    ag_state = {}
    after = filt_rows
    for l in range(depth):
        for part, names in PARTS:
            items = [(name, p[name][l]) for name in names]
            if (l, part) == (0, PARTS[0][0]):
                items.append(("filt", filt_rows))
            lands, specs = zip(*[landing(name, shard) for name, shard in items])
            out = _ag_chips_start(lands, specs, after, f"ag{l}{part}_chips_start")
            ag_state[(l, part)] = ([name for name, _ in items], specs, out)
            after = out[-1]
    all_started = after

    def gathered(l, part, after):
        names, specs, out = ag_state[(l, part)]
        lands = _ag_chips_wait(out[0], out[1], out[2:-1], specs, after, f"ag{l}{part}_chips_wait")
        lands = _ag_pair_forward(lands, specs, f"ag{l}{part}_pair_forward")
        return dict(zip(names, lands))

    full = [dict() for _ in range(depth)]
    full[0].update(gathered(0, PARTS[0][0], all_started))
    filt_all = lax.bitcast_convert_type(full[0]["filt"].reshape(N_DEV, -1, 2), F32)
    n_cw = conv_dw_w.size
    kc, cs = conv_dw_w.shape[1:]
    kf, fs = ffn_dw_w.shape[1:]
    conv_w_full = jnp.moveaxis(filt_all[:, :n_cw].reshape(N_DEV, depth, kc, cs), 0, 2).reshape(depth, kc, N_DEV * cs)
    ffn_w_full = jnp.moveaxis(filt_all[:, n_cw:].reshape(N_DEV, depth, kf, fs), 0, 2).reshape(depth, kf, N_DEV * fs)

    vec = lambda a: a.reshape(1, -1)
    x2d = x.reshape(t_dim, d_model)
    mem2d = mem.reshape(n_batch * m_len, d_model)
    mem_n = _rmsnorm_fwd(mem2d, vec(mem_norm_g), "mem_norm")

    saved = []
    xc = x2d
    for l in range(depth):
        if l > 0:
            full[l].update(gathered(l, PARTS[0][0], xc))
        wl = full[l]
        s = {"x0": xc}
        s["h"] = _rmsnorm_fwd(xc, vec(mix_norm_g[l]), f"mix_norm_l{l}")
        s["proj"] = _matmul(s["h"], wl["w_in"], "nt", f"in_proj_l{l}", out_dtype=BF16)
        s["y1"] = _glu_conv_fwd(s["proj"], conv_w_full[l], vec(conv_dw_b[l]), n_batch, seq, f"glu_conv_l{l}")
        s["y3"] = _ln_silu_fwd(s["y1"], vec(conv_ln_g[l]), vec(conv_ln_b[l]), f"ln_silu_l{l}")
        s["yc"] = _matmul(s["y3"], wl["w_conv_out"], "nt", f"conv_out_l{l}", out_dtype=BF16)
        s["zp"] = _pool_fwd(s["proj"], pool_col0, n_groups, n_batch, seq, f"pool_l{l}")
        s["yp"] = _grouped(s["zp"], wl["w_pool_grp"], "nt", f"pool_proj_l{l}", out_dtype=BF16)
        s["merged"] = _merge_fwd(s["proj"], gate_col0, s["yc"], s["yp"], vec(pool_scale[l]), f"merge_l{l}")
        s["x1"] = _matmul(s["merged"], wl["w_out"], "nn", f"mix_out_l{l}", res=xc)
        wl.update(gathered(l, PARTS[1][0], s["x1"]))
        half_up = wl["w_up"].shape[0] // 2
        up_gate, up_val = (0, half_up), (half_up, half_up)
        s["hq"] = _rmsnorm_fwd(s["x1"], vec(xattn_norm_g[l]), f"xattn_norm_l{l}")
        s["q"] = _matmul(s["hq"], wl["w_q"], "nn", f"q_proj_l{l}", out_dtype=BF16)
        s["kv"] = _matmul(mem_n, wl["w_kv"], "nt", f"kv_proj_l{l}", out_dtype=BF16)
        s["att"] = _attn_fwd(s["q"], s["kv"], n_batch, seq, m_len, f"attn_l{l}")
        s["x2"] = _matmul(s["att"], wl["w_o"], "nn", f"attn_out_l{l}", res=s["x1"])
        s["hf"] = _rmsnorm_fwd(s["x2"], vec(ffn_norm_g[l]), f"ffn_norm_l{l}")
        s["up_g"] = _matmul(s["hf"], wl["w_up"], "nt", f"up_proj_gate_l{l}", out_dtype=BF16, b_window=up_gate)
        s["up_v"] = _matmul(s["hf"], wl["w_up"], "nt", f"up_proj_val_l{l}", out_dtype=BF16, b_window=up_val)
        s["act"] = _ffn_act_fwd(s["up_g"], s["up_v"], ffn_w_full[l], n_batch, seq, f"ffn_act_l{l}")
        xc = _matmul(s["act"], wl["w_down"], "nn", f"down_proj_l{l}", res=s["x2"])
        saved.append(s)

    dx, dxb, dg_final, loss_part = _loss_head(xc, vec(final_norm_g), loss_target.reshape(t_dim, d_model), "loss_head")

    small = {"final_norm_g": dg_final.reshape(-1)}
    big = [dict() for _ in range(depth)]
    rs_state = {}
    rs_after = loss_part

    def rs_begin(l, part):
        pack = lax.empty((N_DEV, sum(nrows for _, _, nrows in layout[part]), d_model), BF16)
        row0 = 0
        for name, _, nrows in layout[part]:
            pieces = big[l][name] if isinstance(big[l][name], tuple) else (big[l][name],)
            d0 = 0
            for piece in pieces:
                blocks = _scatter_blocks(name, piece, p[name].shape[1:], d_model, N_DEV // len(pieces)).astype(BF16)
                pack = lax.dynamic_update_slice(pack, blocks, (d0, row0, 0))
                d0 += blocks.shape[0]
            row0 += nrows
        rs_state[(l, part)] = _reduce_scatter_begin(pack, f"rs{l}{part}")
        return rs_state[(l, part)][4]

    dmem_n = None
    for l in reversed(range(depth)):
        wl, s = full[l], saved[l]
        sm = {}
        dact = _matmul(dxb, wl["w_down"], "nt", f"d_act_l{l}", out_dtype=BF16, after=rs_after)
        big[l]["w_down"] = _matmul(s["act"], dxb, "tn", f"d_w_down_l{l}", out_dtype=BF16)
        dup_g, dup_v, dwf_g, dwf_v = _ffn_act_bwd(s["up_g"], s["up_v"], ffn_w_full[l], dact, n_batch, seq,
                                                  f"ffn_act_bwd_l{l}")
        sm["ffn_dw_w"] = jnp.concatenate([dwf_g, dwf_v], axis=1)
        dhf = _matmul(dup_g, wl["w_up"], "nn", f"d_hf_gate_l{l}", b_window=up_gate)
        dx, dxb, dg = _matmul_rmsnorm_bwd(dup_v, wl["w_up"], "nn", s["x2"], vec(ffn_norm_g[l]), dx,
                                          f"d_hf_val_ffn_norm_bwd_l{l}", res=dhf, b_window=up_val)
        big[l]["w_up"] = (_matmul(dup_g, s["hf"], "tn", f"d_w_up_gate_l{l}", out_dtype=BF16),
                          _matmul(dup_v, s["hf"], "tn", f"d_w_up_val_l{l}", out_dtype=BF16))
        sm["ffn_norm_g"] = dg
        datt = _matmul(dxb, wl["w_o"], "nt", f"d_att_l{l}", out_dtype=BF16, after=rs_after)
        big[l]["w_o"] = _matmul(s["att"], dxb, "tn", f"d_w_o_l{l}", out_dtype=BF16)
        dq, dk, dv = _attn_bwd(s["q"], s["kv"], datt, n_batch, seq, m_len, f"attn_bwd_l{l}")
        dkv = jnp.concatenate([dk, dv], axis=1)
        big[l]["w_kv"] = _matmul(dkv, mem_n, "tn", f"d_w_kv_l{l}", out_dtype=BF16)
        dmem_n = _matmul(dkv, wl["w_kv"], "nn", f"d_mem_l{l}", res=dmem_n)
        big[l]["w_q"] = _matmul(s["hq"], dq, "tn", f"d_w_q_l{l}", out_dtype=BF16)
        dx, dxb, dg = _matmul_rmsnorm_bwd(dq, wl["w_q"], "nt", s["x1"], vec(xattn_norm_g[l]), dx,
                                          f"d_hq_xattn_norm_bwd_l{l}")
        sm["xattn_norm_g"] = dg
        rs_after = rs_begin(l, PARTS[1][0])
        dmerged = _matmul(dxb, wl["w_out"], "nt", f"d_merged_l{l}", out_dtype=BF16, after=rs_after)
        big[l]["w_out"] = _matmul(s["merged"], dxb, "tn", f"d_w_out_l{l}", out_dtype=BF16)
        dgc, dgp, dyc, dyp, dscale = _merge_bwd(s["proj"], gate_col0, s["yc"], s["yp"], vec(pool_scale[l]), dmerged,
                                                f"merge_bwd_l{l}")
        sm["pool_scale"] = dscale
        dzp = _grouped(dyp, wl["w_pool_grp"], "nn", f"d_zp_l{l}")
        big[l]["w_pool_grp"] = _grouped_tn(dyp, s["zp"], n_groups, f"d_w_pool_l{l}")
        du = _pool_bwd(dzp, n_groups, n_batch, seq, f"pool_bwd_l{l}")
        dy3 = _matmul(dyc, wl["w_conv_out"], "nn", f"d_y3_l{l}")
        big[l]["w_conv_out"] = _matmul(dyc, s["y3"], "tn", f"d_w_conv_out_l{l}", out_dtype=BF16)
        dy1, dlg, dlb = _ln_silu_bwd(s["y1"], vec(conv_ln_g[l]), vec(conv_ln_b[l]), dy3, f"ln_silu_bwd_l{l}")
        sm["conv_ln_g"], sm["conv_ln_b"] = dlg, dlb
        da, dgl, dcw, dcb = _glu_conv_bwd(s["proj"], conv_w_full[l], dy1, n_batch, seq, f"glu_conv_bwd_l{l}")
        sm["conv_dw_w"], sm["conv_dw_b"] = dcw, dcb
        dproj = jnp.concatenate([da, dgl, du, dgc, dgp], axis=1)
        big[l]["w_in"] = _matmul(dproj, s["h"], "tn", f"d_w_in_l{l}", out_dtype=BF16)
        dx, dxb, dg = _matmul_rmsnorm_bwd(dproj, wl["w_in"], "nn", s["x0"], vec(mix_norm_g[l]), dx,
                                          f"d_h_mix_norm_bwd_l{l}")
        sm["mix_norm_g"] = dg
        for k, val in sm.items():
            small[(l, k)] = val.reshape(-1)
        rs_after = rs_begin(l, PARTS[0][0])
    _, _, dg_mem = _rmsnorm_bwd(mem2d, vec(mem_norm_g), dmem_n, None, "mem_norm_bwd")
    small["mem_norm_g"] = dg_mem.reshape(-1)
    small["loss"] = loss_part.reshape(-1)

    grads = {}
    per_layer = {name: [None] * depth for name, _ in MATRICES}
    for l in reversed(range(depth)):
        for part, _ in reversed(PARTS):
            mat_grads = _reduce_scatter_end(rs_state[(l, part)], rs_after, f"rs{l}{part}")
            row0 = 0
            for name, tr, nrows in layout[part]:
                per_layer[name][l] = _from_rows(name, tr, mat_grads[row0:row0 + nrows], p[name].shape[1:])
                row0 += nrows
    for name, _ in MATRICES:
        grads[name] = jnp.stack(per_layer[name])

    keys = list(small.keys())
    flat = jnp.concatenate([small[k] for k in keys])
    n_small = flat.shape[0]
    rows_small = -(-n_small // (SUBLANE * d_model)) * SUBLANE
    flat = jnp.pad(flat, (0, rows_small * d_model - n_small)).reshape(rows_small, d_model)
    every = _all_gather(flat, "small_all_gather")
    total = _sum_rows([every[i] for i in range(N_DEV)], F32, "small_sum").reshape(-1)
    off = 0
    red = {}
    for k in keys:
        red[k] = total[off:off + small[k].shape[0]]
        off += small[k].shape[0]
    loss = red["loss"][0]
    for name in ("mix_norm_g", "conv_dw_b", "conv_ln_g", "conv_ln_b", "pool_scale", "xattn_norm_g", "ffn_norm_g"):
        grads[name] = jnp.stack([red[(l, name)] for l in range(depth)])
    grads["conv_dw_w"] = jnp.stack([
        lax.dynamic_slice_in_dim(red[(l, "conv_dw_w")].reshape(kc, N_DEV * cs), dev * cs, cs, axis=1)
        for l in range(depth)])
    grads["ffn_dw_w"] = jnp.stack([
        lax.dynamic_slice_in_dim(red[(l, "ffn_dw_w")].reshape(kf, N_DEV * fs), dev * fs, fs, axis=1)
        for l in range(depth)])
    grads["mem_norm_g"] = red["mem_norm_g"]
    grads["final_norm_g"] = red["final_norm_g"]

    deltas, new_m, new_v = {}, {}, {}
    for name in weight_names:
        deltas[name], new_m[name], new_v[name] = _adamw(p[name], grads[name], p["m_" + name], p["v_" + name],
                                                        f"adamw_{name}")
    grad_x = dx.reshape(n_batch, seq, d_model)
    return (loss, grad_x, *[grads[n] for n in weight_names], *[deltas[n] for n in weight_names],
            *[new_m[n] for n in weight_names], *[new_v[n] for n in weight_names])
```

```python
import functools

import jax
import jax.numpy as jnp
from jax import lax
from jax.experimental import pallas as pl
from jax.experimental.pallas import tpu as pltpu

F32 = jnp.float32
BF16 = jnp.bfloat16
MESH = pl.DeviceIdType.MESH

N_DEV = 8
EPS = 1e-6
V7X_VMEM_BYTES = 64 * 1024 * 1024
VMEM_LIMIT = (V7X_VMEM_BYTES * 3) // 4
LANE = 128
SUBLANE = 8

CONV_HALO = 32
POOL_HALO = 16
FFN_HALO = 8
POOL_WINDOW_MAX = 16
XA_HEADS = 4

ADAM_LR = 0.001
ADAM_B1 = 0.9
ADAM_B2 = 0.999
ADAM_EPS = 1e-08
ADAM_WD = 0.01
ADAM_STEP = 10

GELU_C0 = 0.7978845608028654
GELU_C1 = 0.044715


ANY_SPEC = pl.BlockSpec(memory_space=pl.ANY)


def _tile(n, cap, mult=LANE):
    if n <= cap:
        return n
    best = None
    for d in range(mult, cap + 1, mult):
        if n % d == 0:
            best = d
    assert best is not None, (n, cap, mult)
    return best


def _params(*sem):
    return pltpu.CompilerParams(dimension_semantics=sem, vmem_limit_bytes=VMEM_LIMIT)


def _sig(x):
    return 1.0 / (1.0 + jnp.exp(-x))


def _bs(shape, imap):
    return pl.BlockSpec(shape, imap)


def _mxu_tile(n, cap):
    if n <= cap:
        return n
    best = {mult: max((d for d in range(mult, cap + 1, mult) if n % d == 0), default=0) for mult in (2 * LANE, LANE)}
    assert best[LANE] > 0, (n, cap)
    return best[2 * LANE] if 2 * best[2 * LANE] >= best[LANE] else best[LANE]


def _matmul(a, b, mode, name, res=None, out_dtype=F32, after=None, b_window=None):
    b_row0, b_rows = b_window if b_window is not None else (0, b.shape[0])
    if mode == "tn":
        k_dim, m_dim = a.shape
        k2, n_dim = b_rows, b.shape[1]
    elif mode == "nn":
        m_dim, k_dim = a.shape
        k2, n_dim = b_rows, b.shape[1]
    else:
        m_dim, k_dim = a.shape
        n_dim, k2 = b_rows, b.shape[1]
    assert k_dim == k2, (name, a.shape, b.shape)
    size = lambda t: jnp.dtype(t).itemsize
    tm = _mxu_tile(m_dim, 2816 if mode == "tn" else 1024)
    tn = _mxu_tile(n_dim, 2816)
    fixed = tm * tn * (2 * size(out_dtype) + (2 * size(res.dtype) if res is not None else 0) + 4)
    for cap in (2816, 2048, 1792, 1024, 512):
        tk = _mxu_tile(k_dim, cap)
        if fixed + 2 * tk * (tm * size(a.dtype) + tn * size(b.dtype)) <= VMEM_LIMIT - 8 * 1024 * 1024:
            break
    nk = k_dim // tk
    use_acc = nk > 1 and out_dtype != F32
    if mode == "tn":
        a_spec, ca = _bs((tk, tm), lambda i, j, k: (k, i)), 0
    else:
        a_spec, ca = _bs((tm, tk), lambda i, j, k: (i, k)), 1
    if mode == "nt":
        assert b_row0 % tn == 0
        b_spec, cb = _bs((tn, tk), lambda i, j, k: (j + b_row0 // tn, k)), 1
    else:
        assert b_row0 % tk == 0
        b_spec, cb = _bs((tk, tn), lambda i, j, k: (k + b_row0 // tk, j)), 0
    dims = (((ca,), (cb,)), ((), ()))
    o_spec = _bs((tm, tn), lambda i, j, k: (i, j))
    has_res = res is not None

    def body(*refs):
        a_ref, b_ref = refs[:2]
        r_ref = refs[2] if has_res else None
        o_ref = refs[n_in]
        k = pl.program_id(2)
        part = lax.dot_general(a_ref[...].astype(BF16), b_ref[...].astype(BF16), dims,
                               preferred_element_type=F32)
        if nk == 1:
            if has_res:
                part = part + r_ref[...].astype(F32)
            o_ref[...] = part.astype(out_dtype)
            return
        acc = refs[-1] if use_acc else o_ref

        @pl.when(k == 0)
        def _():
            acc[...] = part + r_ref[...].astype(F32) if has_res else part

        @pl.when(k > 0)
        def _():
            acc[...] += part

        if use_acc:
            @pl.when(k == nk - 1)
            def _():
                o_ref[...] = acc[...].astype(out_dtype)

    in_specs = [a_spec, b_spec] + ([o_spec] if has_res else [])
    args = (a, b) + ((res,) if has_res else ())
    if after is not None:
        in_specs.append(ANY_SPEC)
        args += (after,)
    n_in = len(args)
    return pl.pallas_call(
        body, out_shape=jax.ShapeDtypeStruct((m_dim, n_dim), out_dtype),
        grid=(m_dim // tm, n_dim // tn, nk), in_specs=in_specs, out_specs=o_spec,
        scratch_shapes=[pltpu.VMEM((tm, tn), F32)] if use_acc else [], name=name,
        compiler_params=_params("parallel", "parallel", "arbitrary"))(*args)


def _grouped(a, w, mode, name, out_dtype=F32):
    t_dim = a.shape[0]
    g_dim, r_dim, c_dim = w.shape
    ka, no = (c_dim, r_dim) if mode == "nt" else (r_dim, c_dim)
    tm = _tile(t_dim, 2048)
    dims = (((1,), (1 if mode == "nt" else 0,)), ((), ()))

    def body(a_ref, w_ref, o_ref):
        o_ref[...] = lax.dot_general(a_ref[...].astype(BF16), w_ref[...].astype(BF16), dims,
                                     preferred_element_type=F32).astype(out_dtype)

    return pl.pallas_call(
        body, out_shape=jax.ShapeDtypeStruct((t_dim, g_dim * no), out_dtype),
        grid=(t_dim // tm, g_dim),
        in_specs=[_bs((tm, ka), lambda i, g: (i, g)), _bs((None, r_dim, c_dim), lambda i, g: (g, 0, 0))],
        out_specs=_bs((tm, no), lambda i, g: (i, g)), name=name,
        compiler_params=_params("parallel", "parallel"))(a, w)


def _grouped_tn(a, b, g_dim, name):
    t_dim = a.shape[0]
    ra = a.shape[1] // g_dim
    cb = b.shape[1] // g_dim
    tm = _tile(t_dim, 2048)
    nt = t_dim // tm

    def body(a_ref, b_ref, o_ref):
        part = lax.dot_general(a_ref[...].astype(BF16), b_ref[...].astype(BF16), (((0,), (0,)), ((), ())),
                               preferred_element_type=F32)

        @pl.when(pl.program_id(1) == 0)
        def _():
            o_ref[...] = part

        @pl.when(pl.program_id(1) > 0)
        def _():
            o_ref[...] += part

    return pl.pallas_call(
        body, out_shape=jax.ShapeDtypeStruct((g_dim, ra, cb), F32), grid=(g_dim, nt),
        in_specs=[_bs((tm, ra), lambda g, i: (i, g)), _bs((tm, cb), lambda g, i: (i, g))],
        out_specs=_bs((None, ra, cb), lambda g, i: (g, 0, 0)), name=name,
        compiler_params=_params("parallel", "arbitrary"))(a, b)


def _rmsnorm_fwd(x, g, name):
    t_dim, d = x.shape
    tm = _tile(t_dim, 512)

    def body(x_ref, g_ref, o_ref):
        xv = x_ref[...]
        r = lax.rsqrt(jnp.mean(xv * xv, axis=-1, keepdims=True) + EPS)
        o_ref[...] = (xv * r * g_ref[...]).astype(BF16)

    return pl.pallas_call(
        body, out_shape=jax.ShapeDtypeStruct((t_dim, d), BF16), grid=(t_dim // tm,),
        in_specs=[_bs((tm, d), lambda i: (i, 0)), _bs((1, d), lambda i: (0, 0))],
        out_specs=_bs((tm, d), lambda i: (i, 0)), name=name, compiler_params=_params("parallel"))(x, g)


def _rmsnorm_bwd(x, g, dh, dx_in, name):
    t_dim, d = x.shape
    tm = _tile(t_dim, 512)
    has_in = dx_in is not None

    def body(*refs):
        if has_in:
            x_ref, g_ref, dh_ref, di_ref, dx_ref, dxb_ref, dg_ref = refs
        else:
            x_ref, g_ref, dh_ref, dx_ref, dxb_ref, dg_ref = refs
        xv = x_ref[...]
        r = lax.rsqrt(jnp.mean(xv * xv, axis=-1, keepdims=True) + EPS)
        xh = xv * r
        dhv = dh_ref[...].astype(F32)
        dxh = dhv * g_ref[...]
        dx = r * (dxh - xh * jnp.mean(dxh * xh, axis=-1, keepdims=True))
        if has_in:
            dx = dx + di_ref[...]
        dx_ref[...] = dx
        dxb_ref[...] = dx.astype(BF16)
        part = jnp.sum(dhv * xh, axis=0, keepdims=True)

        @pl.when(pl.program_id(0) == 0)
        def _():
            dg_ref[...] = part

        @pl.when(pl.program_id(0) > 0)
        def _():
            dg_ref[...] += part

    row = _bs((tm, d), lambda i: (i, 0))
    vec = _bs((1, d), lambda i: (0, 0))
    args = (x, g, dh) + ((dx_in,) if has_in else ())
    return pl.pallas_call(
        body, out_shape=(jax.ShapeDtypeStruct((t_dim, d), F32), jax.ShapeDtypeStruct((t_dim, d), BF16),
                         jax.ShapeDtypeStruct((1, d), F32)),
        grid=(t_dim // tm,), in_specs=[row, vec, row] + ([row] if has_in else []),
        out_specs=(row, row, vec), name=name, compiler_params=_params("arbitrary"))(*args)


def _matmul_rmsnorm_bwd(a, b, mode, x, g, dx_in, name, res=None, b_window=None):
    b_row0, b_rows = b_window if b_window is not None else (0, b.shape[0])
    m_dim, k_dim = a.shape
    d = x.shape[1]
    assert (b_rows, b.shape[1]) == ((k_dim, d) if mode == "nn" else (d, k_dim)), (name, a.shape, b.shape)
    tm = _mxu_tile(m_dim, 512)
    tk = _mxu_tile(k_dim, 1792)
    nk = k_dim // tk
    has_res = res is not None
    if mode == "nt":
        assert b_row0 == 0
        b_spec, cb = _bs((d, tk), lambda i, k: (0, k)), 1
    else:
        assert b_row0 % tk == 0
        b_spec, cb = _bs((tk, d), lambda i, k: (k + b_row0 // tk, 0)), 0
    dims = (((1,), (cb,)), ((), ()))

    def body(*refs):
        a_ref, b_ref = refs[:2]
        r_ref = refs[2] if has_res else None
        x_ref, g_ref, di_ref, dx_ref, dxb_ref, dg_ref = refs[2 + has_res:8 + has_res]
        i, k = pl.program_id(0), pl.program_id(1)
        part = lax.dot_general(a_ref[...].astype(BF16), b_ref[...].astype(BF16), dims, preferred_element_type=F32)

        def finish(dhv):
            if has_res:
                dhv = dhv + r_ref[...].astype(F32)
            xv = x_ref[...]
            r = lax.rsqrt(jnp.mean(xv * xv, axis=-1, keepdims=True) + EPS)
            xh = xv * r
            dxh = dhv * g_ref[...]
            dx = r * (dxh - xh * jnp.mean(dxh * xh, axis=-1, keepdims=True)) + di_ref[...]
            dx_ref[...] = dx
            dxb_ref[...] = dx.astype(BF16)
            dg_part = jnp.sum(dhv * xh, axis=0, keepdims=True)

            @pl.when(i == 0)
            def _():
                dg_ref[...] = dg_part

            @pl.when(i > 0)
            def _():
                dg_ref[...] += dg_part

        if nk == 1:
            finish(part)
            return
        acc = refs[-1]

        @pl.when(k == 0)
        def _():
            acc[...] = part

        @pl.when(jnp.logical_and(k > 0, k < nk - 1))
        def _():
            acc[...] += part

        @pl.when(k == nk - 1)
        def _():
            finish(acc[...] + part)

    row = _bs((tm, d), lambda i, k: (i, 0))
    vec = _bs((1, d), lambda i, k: (0, 0))
    in_specs = [_bs((tm, tk), lambda i, k: (i, k)), b_spec] + ([row] if has_res else []) + [row, vec, row]
    args = (a, b) + ((res,) if has_res else ()) + (x, g, dx_in)
    return pl.pallas_call(
        body, out_shape=(jax.ShapeDtypeStruct((m_dim, d), F32), jax.ShapeDtypeStruct((m_dim, d), BF16),
                         jax.ShapeDtypeStruct((1, d), F32)),
        grid=(m_dim // tm, nk), in_specs=in_specs, out_specs=(row, row, vec),
        scratch_shapes=[pltpu.VMEM((tm, d), F32)] if nk > 1 else [], name=name,
        compiler_params=_params("arbitrary", "arbitrary"))(*args)


def _loss_head(x, g, tgt, name):
    t_dim, d = x.shape
    tm = _tile(t_dim, 512)

    def body(x_ref, g_ref, t_ref, dx_ref, dxb_ref, dg_ref, loss_ref):
        xv = x_ref[...]
        gv = g_ref[...]
        r = lax.rsqrt(jnp.mean(xv * xv, axis=-1, keepdims=True) + EPS)
        xh = xv * r
        err = xh * gv - t_ref[...]
        dy = err * (1.0 / d)
        dxh = dy * gv
        dx = r * (dxh - xh * jnp.mean(dxh * xh, axis=-1, keepdims=True))
        dx_ref[...] = dx
        dxb_ref[...] = dx.astype(BF16)
        dg_part = jnp.sum(dy * xh, axis=0, keepdims=True)
        loss_part = jnp.full((1, LANE), 0.5 * jnp.sum(jnp.mean(err * err, axis=-1, keepdims=True)), F32)

        @pl.when(pl.program_id(0) == 0)
        def _():
            dg_ref[...] = dg_part
            loss_ref[...] = loss_part

        @pl.when(pl.program_id(0) > 0)
        def _():
            dg_ref[...] += dg_part
            loss_ref[...] += loss_part

    row = _bs((tm, d), lambda i: (i, 0))
    vec = _bs((1, d), lambda i: (0, 0))
    return pl.pallas_call(
        body, out_shape=(jax.ShapeDtypeStruct((t_dim, d), F32), jax.ShapeDtypeStruct((t_dim, d), BF16),
                         jax.ShapeDtypeStruct((1, d), F32), jax.ShapeDtypeStruct((1, LANE), F32)),
        grid=(t_dim // tm,), in_specs=[row, vec, row],
        out_specs=(row, row, vec, _bs((1, LANE), lambda i: (0, 0))), name=name,
        compiler_params=_params("arbitrary"))(x, g, tgt)


def _glu_conv_fwd(proj, dw_w, dw_b, n_batch, seq, name):
    kk, cc = dw_w.shape
    nj = cc // LANE
    ch = min(256, seq)

    def body(a_ref, gl_ref, w_ref, b_ref, o_ref, pad):
        pad[0:CONV_HALO, :] = jnp.zeros((CONV_HALO, LANE), F32)
        pad[CONV_HALO:CONV_HALO + seq, :] = a_ref[...].astype(F32) * _sig(gl_ref[...].astype(F32))
        for c0 in range(0, seq, ch):
            acc = jnp.broadcast_to(b_ref[...], (ch, LANE))
            for k in range(kk):
                acc = acc + w_ref[k:k + 1, :] * pad[pl.ds(c0 + CONV_HALO - (kk - 1) + k, ch), :]
            o_ref[c0:c0 + ch, :] = acc

    return pl.pallas_call(
        body, out_shape=jax.ShapeDtypeStruct((n_batch * seq, cc), F32), grid=(n_batch, nj),
        in_specs=[_bs((seq, LANE), lambda b, j: (b, j)), _bs((seq, LANE), lambda b, j: (b, nj + j)),
                  _bs((kk, LANE), lambda b, j: (0, j)), _bs((1, LANE), lambda b, j: (0, j))],
        out_specs=_bs((seq, LANE), lambda b, j: (b, j)),
        scratch_shapes=[pltpu.VMEM((seq + CONV_HALO, LANE), F32)], name=name,
        compiler_params=_params("parallel", "parallel"))(proj, proj, dw_w, dw_b)


def _glu_conv_bwd(proj, dw_w, dy1, n_batch, seq, name):
    kk, cc = dw_w.shape
    nj = cc // LANE
    ch = min(256, seq)

    def body(a_ref, gl_ref, dy_ref, w_ref, da_ref, dgl_ref, dw_ref, db_ref, padf, padb):
        first = pl.program_id(1) == 0
        padf[0:CONV_HALO, :] = jnp.zeros((CONV_HALO, LANE), F32)
        padf[CONV_HALO:CONV_HALO + seq, :] = a_ref[...].astype(F32) * _sig(gl_ref[...].astype(F32))
        padb[0:seq, :] = dy_ref[...]
        padb[seq:seq + CONV_HALO, :] = jnp.zeros((CONV_HALO, LANE), F32)

        @pl.when(first)
        def _():
            dw_ref[...] = jnp.zeros((kk, LANE), F32)
            db_ref[...] = jnp.zeros((1, LANE), F32)

        dws = [jnp.zeros((1, LANE), F32) for _ in range(kk)]
        for c0 in range(0, seq, ch):
            acc = jnp.zeros((ch, LANE), F32)
            y0 = padf[CONV_HALO + c0:CONV_HALO + c0 + ch, :]
            for k in range(kk):
                win = padb[pl.ds(c0 + (kk - 1) - k, ch), :]
                acc = acc + w_ref[k:k + 1, :] * win
                dws[k] = dws[k] + jnp.sum(win * y0, axis=0, keepdims=True)
            sg = _sig(gl_ref[c0:c0 + ch, :].astype(F32))
            da_ref[c0:c0 + ch, :] = (acc * sg).astype(BF16)
            dgl_ref[c0:c0 + ch, :] = (acc * a_ref[c0:c0 + ch, :].astype(F32) * sg * (1.0 - sg)).astype(BF16)
        for k in range(kk):
            dw_ref[k:k + 1, :] += dws[k]
        db_ref[...] += jnp.sum(dy_ref[...], axis=0, keepdims=True)

    tok = _bs((seq, LANE), lambda j, b: (b, j))
    t_dim = n_batch * seq
    return pl.pallas_call(
        body, out_shape=(jax.ShapeDtypeStruct((t_dim, cc), BF16), jax.ShapeDtypeStruct((t_dim, cc), BF16),
                         jax.ShapeDtypeStruct((kk, cc), F32), jax.ShapeDtypeStruct((1, cc), F32)),
        grid=(nj, n_batch),
        in_specs=[tok, _bs((seq, LANE), lambda j, b: (b, nj + j)), tok, _bs((kk, LANE), lambda j, b: (0, j))],
        out_specs=(tok, tok, _bs((kk, LANE), lambda j, b: (0, j)), _bs((1, LANE), lambda j, b: (0, j))),
        scratch_shapes=[pltpu.VMEM((seq + CONV_HALO, LANE), F32), pltpu.VMEM((seq + CONV_HALO, LANE), F32)],
        name=name, compiler_params=_params("parallel", "arbitrary"))(proj, proj, dy1, dw_w)


def _ln_silu_fwd(y1, g, b, name):
    t_dim, c = y1.shape
    tm = _tile(t_dim, 512)

    def body(y_ref, g_ref, b_ref, o_ref):
        yv = y_ref[...]
        xc = yv - jnp.mean(yv, axis=-1, keepdims=True)
        rstd = lax.rsqrt(jnp.mean(xc * xc, axis=-1, keepdims=True) + EPS)
        y2 = xc * rstd * g_ref[...] + b_ref[...]
        o_ref[...] = (y2 * _sig(y2)).astype(BF16)

    row = _bs((tm, c), lambda i: (i, 0))
    vec = _bs((1, c), lambda i: (0, 0))
    return pl.pallas_call(
        body, out_shape=jax.ShapeDtypeStruct((t_dim, c), BF16), grid=(t_dim // tm,),
        in_specs=[row, vec, vec], out_specs=row, name=name, compiler_params=_params("parallel"))(y1, g, b)


def _ln_silu_bwd(y1, g, b, dy3, name):
    t_dim, c = y1.shape
    tm = _tile(t_dim, 512)

    def body(y_ref, g_ref, b_ref, d_ref, dy_ref, dg_ref, db_ref):
        yv = y_ref[...]
        gv = g_ref[...]
        xc = yv - jnp.mean(yv, axis=-1, keepdims=True)
        rstd = lax.rsqrt(jnp.mean(xc * xc, axis=-1, keepdims=True) + EPS)
        yh = xc * rstd
        y2 = yh * gv + b_ref[...]
        s = _sig(y2)
        dy2 = d_ref[...].astype(F32) * (s * (1.0 + y2 * (1.0 - s)))
        dyh = dy2 * gv
        dy_ref[...] = rstd * (dyh - jnp.mean(dyh, axis=-1, keepdims=True)
                              - yh * jnp.mean(dyh * yh, axis=-1, keepdims=True))
        dg_part = jnp.sum(dy2 * yh, axis=0, keepdims=True)
        db_part = jnp.sum(dy2, axis=0, keepdims=True)

        @pl.when(pl.program_id(0) == 0)
        def _():
            dg_ref[...] = dg_part
            db_ref[...] = db_part

        @pl.when(pl.program_id(0) > 0)
        def _():
            dg_ref[...] += dg_part
            db_ref[...] += db_part

    row = _bs((tm, c), lambda i: (i, 0))
    vec = _bs((1, c), lambda i: (0, 0))
    return pl.pallas_call(
        body, out_shape=(jax.ShapeDtypeStruct((t_dim, c), F32), jax.ShapeDtypeStruct((1, c), F32),
                         jax.ShapeDtypeStruct((1, c), F32)),
        grid=(t_dim // tm,), in_specs=[row, vec, vec, row], out_specs=(row, vec, vec), name=name,
        compiler_params=_params("arbitrary"))(y1, g, b, dy3)


def _pool_fwd(proj, col0, n_groups, n_batch, seq, name):
    ch = min(256, seq)

    def body(u_ref, o_ref, pad):
        w = lax.shift_left(jnp.int32(2), pl.program_id(1))
        pad[0:POOL_HALO, :] = jnp.zeros((POOL_HALO, LANE), F32)
        pad[POOL_HALO:POOL_HALO + seq, :] = u_ref[...].astype(F32)
        for c0 in range(0, seq, ch):
            acc = jnp.zeros((ch, LANE), F32)
            for j in range(POOL_WINDOW_MAX):
                acc = acc + jnp.where(j < w, 1.0, 0.0).astype(F32) * pad[pl.ds(c0 + POOL_HALO - j, ch), :]
            t = c0 + lax.broadcasted_iota(jnp.int32, (ch, LANE), 0)
            cnt = jnp.minimum(t + 1, w).astype(F32)
            o_ref[c0:c0 + ch, :] = (acc / cnt - pad[POOL_HALO + c0:POOL_HALO + c0 + ch, :]).astype(BF16)

    return pl.pallas_call(
        body, out_shape=jax.ShapeDtypeStruct((n_batch * seq, n_groups * LANE), BF16), grid=(n_batch, n_groups),
        in_specs=[_bs((seq, LANE), lambda b, g: (b, col0 + g))], out_specs=_bs((seq, LANE), lambda b, g: (b, g)),
        scratch_shapes=[pltpu.VMEM((seq + POOL_HALO, LANE), F32)], name=name,
        compiler_params=_params("parallel", "parallel"))(proj)


def _pool_bwd(dzp, n_groups, n_batch, seq, name):
    ch = min(256, seq)

    def body(d_ref, o_ref, pad):
        w = lax.shift_left(jnp.int32(2), pl.program_id(1))
        for c0 in range(0, seq, ch):
            t = c0 + lax.broadcasted_iota(jnp.int32, (ch, LANE), 0)
            cnt = jnp.minimum(t + 1, w).astype(F32)
            pad[c0:c0 + ch, :] = d_ref[c0:c0 + ch, :].astype(F32) / cnt
        pad[seq:seq + POOL_HALO, :] = jnp.zeros((POOL_HALO, LANE), F32)
        for c0 in range(0, seq, ch):
            acc = jnp.zeros((ch, LANE), F32)
            for j in range(POOL_WINDOW_MAX):
                acc = acc + jnp.where(j < w, 1.0, 0.0).astype(F32) * pad[pl.ds(c0 + j, ch), :]
            o_ref[c0:c0 + ch, :] = (acc - d_ref[c0:c0 + ch, :].astype(F32)).astype(BF16)

    tok = _bs((seq, LANE), lambda b, g: (b, g))
    return pl.pallas_call(
        body, out_shape=jax.ShapeDtypeStruct((n_batch * seq, n_groups * LANE), BF16), grid=(n_batch, n_groups),
        in_specs=[tok], out_specs=tok, scratch_shapes=[pltpu.VMEM((seq + POOL_HALO, LANE), F32)], name=name,
        compiler_params=_params("parallel", "parallel"))(dzp)


def _merge_fwd(proj, col0, yc, yp, scale, name):
    t_dim, d = yc.shape
    half = d // 2
    tm = _tile(t_dim, 512)
    c0 = col0 // half

    def body(gc_ref, gp_ref, yc_ref, yp_ref, s_ref, o_ref):
        f32 = lambda r: r[...].astype(F32)
        o_ref[...] = (_sig(f32(gc_ref)) * f32(yc_ref) + _sig(f32(gp_ref)) * (f32(yp_ref) * s_ref[...])).astype(BF16)

    blk = _bs((tm, half), lambda i, j: (i, j))
    return pl.pallas_call(
        body, out_shape=jax.ShapeDtypeStruct((t_dim, d), BF16), grid=(t_dim // tm, 2),
        in_specs=[_bs((tm, half), lambda i, j: (i, c0 + j)), _bs((tm, half), lambda i, j: (i, c0 + 2 + j)),
                  blk, blk, _bs((1, half), lambda i, j: (0, j))],
        out_specs=blk, name=name, compiler_params=_params("parallel", "parallel"))(proj, proj, yc, yp, scale)


def _merge_bwd(proj, col0, yc, yp, scale, dm, name):
    t_dim, d = yc.shape
    half = d // 2
    tm = _tile(t_dim, 512)
    c0 = col0 // half

    def body(gc_ref, gp_ref, yc_ref, yp_ref, s_ref, dm_ref, dgc_ref, dgp_ref, dyc_ref, dyp_ref, ds_ref):
        dmv = dm_ref[...].astype(F32)
        sgc = _sig(gc_ref[...].astype(F32))
        sgp = _sig(gp_ref[...].astype(F32))
        sv = s_ref[...]
        ypre = yp_ref[...].astype(F32)
        dgc_ref[...] = (dmv * yc_ref[...].astype(F32) * sgc * (1.0 - sgc)).astype(BF16)
        dgp_ref[...] = (dmv * (ypre * sv) * sgp * (1.0 - sgp)).astype(BF16)
        dyc_ref[...] = (dmv * sgc).astype(BF16)
        dyp = dmv * sgp
        dyp_ref[...] = (dyp * sv).astype(BF16)
        part = jnp.sum(dyp * ypre, axis=0, keepdims=True)

        @pl.when(pl.program_id(1) == 0)
        def _():
            ds_ref[...] = part

        @pl.when(pl.program_id(1) > 0)
        def _():
            ds_ref[...] += part

    blk = _bs((tm, half), lambda j, i: (i, j))
    big = jax.ShapeDtypeStruct((t_dim, d), BF16)
    return pl.pallas_call(
        body, out_shape=(big, big, big, big, jax.ShapeDtypeStruct((1, d), F32)), grid=(2, t_dim // tm),
        in_specs=[_bs((tm, half), lambda j, i: (i, c0 + j)), _bs((tm, half), lambda j, i: (i, c0 + 2 + j)),
                  blk, blk, _bs((1, half), lambda j, i: (0, j)), blk],
        out_specs=(blk, blk, blk, blk, _bs((1, half), lambda j, i: (0, j))), name=name,
        compiler_params=_params("parallel", "arbitrary"))(proj, proj, yc, yp, scale, dm)


def _attn_fwd(q, kv, n_batch, seq, m_len, name):
    d = q.shape[1]
    hd = d // XA_HEADS
    tq = _tile(seq, 1024)
    nq = seq // tq
    scale = hd ** -0.5

    def body(q_ref, k_ref, v_ref, o_ref):
        sc = lax.dot_general(q_ref[...].astype(BF16), k_ref[...].astype(BF16), (((1,), (1,)), ((), ())),
                             preferred_element_type=F32) * scale
        p = jnp.exp(sc - jnp.max(sc, axis=-1, keepdims=True))
        pr = p / jnp.sum(p, axis=-1, keepdims=True)
        o_ref[...] = jnp.dot(pr.astype(BF16), v_ref[...].astype(BF16), preferred_element_type=F32).astype(BF16)

    return pl.pallas_call(
        body, out_shape=jax.ShapeDtypeStruct((n_batch * seq, d), BF16), grid=(n_batch, XA_HEADS, nq),
        in_specs=[_bs((tq, hd), lambda b, h, i: (b * nq + i, h)), _bs((m_len, hd), lambda b, h, i: (b, h)),
                  _bs((m_len, hd), lambda b, h, i: (b, XA_HEADS + h))],
        out_specs=_bs((tq, hd), lambda b, h, i: (b * nq + i, h)), name=name,
        compiler_params=_params("parallel", "parallel", "parallel"))(q, kv, kv)


def _attn_bwd(q, kv, datt, n_batch, seq, m_len, name):
    d = q.shape[1]
    hd = d // XA_HEADS
    tq = _tile(seq, 1024)
    nq = seq // tq
    scale = hd ** -0.5

    def body(q_ref, k_ref, v_ref, do_ref, dq_ref, dk_ref, dv_ref):
        qb = q_ref[...].astype(BF16)
        kb = k_ref[...].astype(BF16)
        vb = v_ref[...].astype(BF16)
        dob = do_ref[...].astype(BF16)
        sc = lax.dot_general(qb, kb, (((1,), (1,)), ((), ())), preferred_element_type=F32) * scale
        p = jnp.exp(sc - jnp.max(sc, axis=-1, keepdims=True))
        pr = p / jnp.sum(p, axis=-1, keepdims=True)
        dpr = lax.dot_general(dob, vb, (((1,), (1,)), ((), ())), preferred_element_type=F32)
        dsc = pr * (dpr - jnp.sum(dpr * pr, axis=-1, keepdims=True)) * scale
        dsb = dsc.astype(BF16)
        dq_ref[...] = jnp.dot(dsb, kb, preferred_element_type=F32).astype(BF16)
        dv_part = lax.dot_general(pr.astype(BF16), dob, (((0,), (0,)), ((), ())), preferred_element_type=F32)
        dk_part = lax.dot_general(dsb, qb, (((0,), (0,)), ((), ())), preferred_element_type=F32)

        @pl.when(pl.program_id(2) == 0)
        def _():
            dk_ref[...] = dk_part
            dv_ref[...] = dv_part

        @pl.when(pl.program_id(2) > 0)
        def _():
            dk_ref[...] += dk_part
            dv_ref[...] += dv_part

    qs = _bs((tq, hd), lambda b, h, i: (b * nq + i, h))
    ks = _bs((m_len, hd), lambda b, h, i: (b, h))
    return pl.pallas_call(
        body, out_shape=(jax.ShapeDtypeStruct((n_batch * seq, d), BF16), jax.ShapeDtypeStruct((n_batch * m_len, d), F32),
                         jax.ShapeDtypeStruct((n_batch * m_len, d), F32)),
        grid=(n_batch, XA_HEADS, nq),
        in_specs=[qs, ks, _bs((m_len, hd), lambda b, h, i: (b, XA_HEADS + h)), qs],
        out_specs=(qs, ks, ks), name=name,
        compiler_params=_params("parallel", "parallel", "arbitrary"))(q, kv, kv, datt)


def _gelu_parts(g):
    th = jnp.tanh(GELU_C0 * (g + GELU_C1 * g * g * g))
    return th, 0.5 * g * (1.0 + th)


def _ffn_act_fwd(up_g, up_v, dw_w, n_batch, seq, name):
    kk, c2 = dw_w.shape
    f_dim = c2 // 2
    wd = 2 * LANE
    nj = f_dim // wd
    ch = min(128, seq)

    def body(g_ref, v_ref, wg_ref, wv_ref, o_ref, padg, padv):
        for pad, src in ((padg, g_ref), (padv, v_ref)):
            pad[0:FFN_HALO, :] = jnp.zeros((FFN_HALO, wd), F32)
            pad[FFN_HALO:FFN_HALO + seq, :] = src[...].astype(F32)
        for c0 in range(0, seq, ch):
            gate = jnp.zeros((ch, wd), F32)
            val = jnp.zeros((ch, wd), F32)
            for k in range(kk):
                off = c0 + FFN_HALO - (kk - 1) + k
                gate = gate + wg_ref[k:k + 1, :] * padg[pl.ds(off, ch), :]
                val = val + wv_ref[k:k + 1, :] * padv[pl.ds(off, ch), :]
            o_ref[c0:c0 + ch, :] = (_gelu_parts(gate)[1] * val).astype(BF16)

    return pl.pallas_call(
        body, out_shape=jax.ShapeDtypeStruct((n_batch * seq, f_dim), BF16), grid=(n_batch, nj),
        in_specs=[_bs((seq, wd), lambda b, j: (b, j)), _bs((seq, wd), lambda b, j: (b, j)),
                  _bs((kk, wd), lambda b, j: (0, j)), _bs((kk, wd), lambda b, j: (0, nj + j))],
        out_specs=_bs((seq, wd), lambda b, j: (b, j)),
        scratch_shapes=[pltpu.VMEM((seq + FFN_HALO, wd), F32), pltpu.VMEM((seq + FFN_HALO, wd), F32)], name=name,
        compiler_params=_params("parallel", "parallel"))(up_g, up_v, dw_w, dw_w)


def _ffn_act_bwd(up_g, up_v, dw_w, dact, n_batch, seq, name):
    kk, c2 = dw_w.shape
    f_dim = c2 // 2
    wd = 2 * LANE
    nj = f_dim // wd
    ch = min(128, seq)

    def body(g_ref, v_ref, wg_ref, wv_ref, da_ref, dg_ref, dv_ref, dwg_ref, dwv_ref, padg, padv, pbg, pbv):
        for pad, src in ((padg, g_ref), (padv, v_ref)):
            pad[0:FFN_HALO, :] = jnp.zeros((FFN_HALO, wd), F32)
            pad[FFN_HALO:FFN_HALO + seq, :] = src[...].astype(F32)
        for pb in (pbg, pbv):
            pb[seq:seq + FFN_HALO, :] = jnp.zeros((FFN_HALO, wd), F32)

        @pl.when(pl.program_id(1) == 0)
        def _():
            dwg_ref[...] = jnp.zeros((kk, wd), F32)
            dwv_ref[...] = jnp.zeros((kk, wd), F32)

        for c0 in range(0, seq, ch):
            gate = jnp.zeros((ch, wd), F32)
            val = jnp.zeros((ch, wd), F32)
            for k in range(kk):
                off = c0 + FFN_HALO - (kk - 1) + k
                gate = gate + wg_ref[k:k + 1, :] * padg[pl.ds(off, ch), :]
                val = val + wv_ref[k:k + 1, :] * padv[pl.ds(off, ch), :]
            sq = gate * gate
            th = jnp.tanh(GELU_C0 * gate * (1.0 + GELU_C1 * sq))
            half = 0.5 * th + 0.5
            dgelu = half * (1.0 + gate * (GELU_C0 + 3.0 * GELU_C0 * GELU_C1 * sq) * (1.0 - th))
            dav = da_ref[c0:c0 + ch, :].astype(F32)
            pbg[c0:c0 + ch, :] = dav * val * dgelu
            pbv[c0:c0 + ch, :] = dav * (gate * half)
        for pb, pad, w_ref, d_ref, dw_ref in ((pbg, padg, wg_ref, dg_ref, dwg_ref), (pbv, padv, wv_ref, dv_ref, dwv_ref)):
            for c0 in range(0, seq, ch):
                acc = jnp.zeros((ch, wd), F32)
                for k in range(kk):
                    acc = acc + w_ref[k:k + 1, :] * pb[pl.ds(c0 + (kk - 1) - k, ch), :]
                d_ref[c0:c0 + ch, :] = acc.astype(BF16)
            for k in range(kk):
                s = jnp.zeros((1, wd), F32)
                for c0 in range(0, seq, ch):
                    s = s + jnp.sum(pb[c0:c0 + ch, :] * pad[pl.ds(c0 + FFN_HALO - (kk - 1) + k, ch), :],
                                    axis=0, keepdims=True)
                dw_ref[k:k + 1, :] += s

    t_dim = n_batch * seq
    tok = _bs((seq, wd), lambda j, b: (b, j))
    wblk = _bs((kk, wd), lambda j, b: (0, j))
    pad_shape = pltpu.VMEM((seq + FFN_HALO, wd), F32)
    return pl.pallas_call(
        body, out_shape=(jax.ShapeDtypeStruct((t_dim, f_dim), BF16), jax.ShapeDtypeStruct((t_dim, f_dim), BF16),
                         jax.ShapeDtypeStruct((kk, f_dim), F32), jax.ShapeDtypeStruct((kk, f_dim), F32)),
        grid=(nj, n_batch),
        in_specs=[tok, tok, wblk, _bs((kk, wd), lambda j, b: (0, nj + j)), tok],
        out_specs=(tok, tok, wblk, wblk), scratch_shapes=[pad_shape, pad_shape, pad_shape, pad_shape], name=name,
        compiler_params=_params("parallel", "arbitrary"))(up_g, up_v, dw_w, dw_w, dact)


def _sum_rows(parts, out_dtype, name):
    r_dim, c_dim = parts[0].shape
    tr = _tile(r_dim, 1200, SUBLANE)
    n = len(parts)

    def body(*refs):
        acc = refs[0][...].astype(F32)
        for r in refs[1:n]:
            acc = acc + r[...].astype(F32)
        refs[n][...] = acc.astype(out_dtype)

    blk = _bs((tr, c_dim), lambda i: (i, 0))
    return pl.pallas_call(
        body, out_shape=jax.ShapeDtypeStruct((r_dim, c_dim), out_dtype), grid=(r_dim // tr,),
        in_specs=[blk] * n, out_specs=blk, name=name, compiler_params=_params("parallel"))(*parts)


def _adamw(w, g, m, v, name):
    shape = w.shape
    c_dim = shape[-1]
    r_dim = w.size // c_dim
    two_d = lambda t: t.reshape(r_dim, c_dim)
    tr = _tile(r_dim, max(SUBLANE, (256 * 1024) // max(c_dim, LANE) // SUBLANE * SUBLANE), SUBLANE)
    c1 = 1.0 - ADAM_B1 ** ADAM_STEP
    c2 = 1.0 - ADAM_B2 ** ADAM_STEP

    def body(w_ref, g_ref, m_ref, v_ref, d_ref, mo_ref, vo_ref):
        gv = g_ref[...]
        mn = ADAM_B1 * m_ref[...] + (1.0 - ADAM_B1) * gv
        vn = ADAM_B2 * v_ref[...] + (1.0 - ADAM_B2) * (gv * gv)
        mo_ref[...] = mn
        vo_ref[...] = vn
        d_ref[...] = -ADAM_LR * ((mn / c1) / (jnp.sqrt(vn / c2) + ADAM_EPS) + ADAM_WD * w_ref[...])

    blk = _bs((tr, c_dim), lambda i: (i, 0))
    out = jax.ShapeDtypeStruct((r_dim, c_dim), F32)
    d, mo, vo = pl.pallas_call(
        body, out_shape=(out, out, out), grid=(r_dim // tr,), in_specs=[blk] * 4, out_specs=(blk, blk, blk),
        name=name, compiler_params=_params("parallel"))(two_d(w), two_d(g), two_d(m), two_d(v))
    return d.reshape(shape), mo.reshape(shape), vo.reshape(shape)


HBM_SPEC = pl.BlockSpec(memory_space=pltpu.HBM)


def _position():
    return lax.axis_index("x"), lax.axis_index("y"), lax.axis_index("c")


def _all_gather(shard, name):
    def body(x_ref, out_ref, send_sems, recv_sems, local_sem):
        x, y, c = _position()
        me, sibling = (x, y, c), (x, y, 1 - c)
        chips = [(1 - x, y), (x, 1 - y), (1 - x, 1 - y)]

        def rows(px, py, pc):
            return out_ref.at[4 * px + 2 * py + pc]

        def copy(k, block, to, src=None):
            return pltpu.make_async_remote_copy(
                src_ref=rows(*block) if src is None else src, dst_ref=rows(*block),
                send_sem=send_sems.at[k], recv_sem=recv_sems.at[k], device_id=to, device_id_type=MESH)

        mine = pltpu.make_async_copy(x_ref, rows(*me), local_sem)
        mine.start()
        first = [copy(0, me, sibling, src=x_ref)]
        first += [copy(1 + j, me, (*chip, c), src=x_ref) for j, chip in enumerate(chips)]
        for cp in first:
            cp.start()
        passed = [copy(4 + j, (*chip, c), sibling) for j, chip in enumerate(chips)]
        for j, chip in enumerate(chips):
            copy(1 + j, (*chip, c), me).wait_recv()
            passed[j].start()
        copy(0, sibling, me).wait_recv()
        for j, chip in enumerate(chips):
            copy(4 + j, (*chip, 1 - c), me).wait_recv()
        for cp in first + passed:
            cp.wait_send()
        mine.wait()

    return pl.pallas_call(
        body, out_shape=jax.ShapeDtypeStruct((N_DEV,) + shard.shape, shard.dtype),
        in_specs=[HBM_SPEC], out_specs=HBM_SPEC,
        scratch_shapes=[pltpu.SemaphoreType.DMA((7,)), pltpu.SemaphoreType.DMA((7,)), pltpu.SemaphoreType.DMA(())],
        name=name)(shard)


CHIP_RELATIONS = ((0, 0), (1, 0), (0, 1), (1, 1))


def _rs_pair_exchange(g, name):
    _, r_dim, c_dim = g.shape
    n = len(CHIP_RELATIONS)

    def body(g_ref, recv_ref, send_sems, recv_sems):
        x, y, c = _position()
        sibling = (x, y, 1 - c)
        copies = []
        for k, (rx, ry) in enumerate(CHIP_RELATIONS):
            px = x + rx - 2 * x * rx
            py = y + ry - 2 * y * ry
            copies.append(pltpu.make_async_remote_copy(
                src_ref=g_ref.at[4 * px + 2 * py + 1 - c], dst_ref=recv_ref.at[k], send_sem=send_sems.at[k],
                recv_sem=recv_sems.at[k], device_id=sibling, device_id_type=MESH))
        for cp in copies:
            cp.start()
        for cp in copies:
            cp.wait()

    return pl.pallas_call(
        body, out_shape=jax.ShapeDtypeStruct((n, r_dim, c_dim), g.dtype), in_specs=[HBM_SPEC], out_specs=HBM_SPEC,
        scratch_shapes=[pltpu.SemaphoreType.DMA((n,)), pltpu.SemaphoreType.DMA((n,))], name=name)(g)


def _rs_pair_sum(g, recv, name):
    _, r_dim, c_dim = g.shape
    n = len(CHIP_RELATIONS)
    tr = _tile(r_dim, 1200, SUBLANE)
    x, y, c = _position()
    own = jnp.stack([4 * (x + rx - 2 * x * rx) + 2 * (y + ry - 2 * y * ry) + c for rx, ry in CHIP_RELATIONS])

    def body(own_ref, g_ref, r_ref, o_ref):
        o_ref[...] = (g_ref[...].astype(F32) + r_ref[...].astype(F32)).astype(o_ref.dtype)

    blk = _bs((None, tr, c_dim), lambda k, i, own_ref: (k, i, 0))
    return pl.pallas_call(
        body, out_shape=jax.ShapeDtypeStruct((n, r_dim, c_dim), g.dtype),
        grid_spec=pltpu.PrefetchScalarGridSpec(
            num_scalar_prefetch=1, grid=(n, r_dim // tr),
            in_specs=[_bs((None, tr, c_dim), lambda k, i, own_ref: (own_ref[k], i, 0)), blk], out_specs=blk),
        name=name, compiler_params=_params("parallel", "parallel"))(own.astype(jnp.int32), g, recv)


SEM_SPEC = pl.BlockSpec(memory_space=pltpu.SEMAPHORE)
DATAFLOW = pltpu.SideEffectType.DATAFLOW_SIDE_EFFECTING
CHIP_FLIPS = CHIP_RELATIONS[1:]
TOKEN = jax.ShapeDtypeStruct((SUBLANE, LANE), F32)


def _flip(v, r):
    return v + r - 2 * v * r


def _chip_copies(src_ref, src_of, dst_ref, dst_of, send_sems, recv_sems):
    x, y, c = _position()
    me = 4 * x + 2 * y + c
    out = []
    for k, (rx, ry) in enumerate(CHIP_FLIPS):
        px, py = _flip(x, rx), _flip(y, ry)
        peer = 4 * px + 2 * py + c
        out.append(pltpu.make_async_remote_copy(
            src_ref=src_ref.at[src_of(k, me, peer)], dst_ref=dst_ref.at[dst_of(k, me, peer)],
            send_sem=send_sems.at[k], recv_sem=recv_sems.at[k], device_id=(px, py, c), device_id_type=MESH))
    return out


def _device_block(ref, spec, d):
    rows, axis = spec
    return ref.at[pl.ds(d * rows, rows)] if axis == 0 else ref.at[:, pl.ds(d * rows, rows)]


def _ag_chips_start(lands, specs, after, name):
    n = len(lands)
    nf = len(CHIP_FLIPS)

    def body(*refs):
        send_sems, recv_sems, token = refs[n + 1], refs[n + 2], refs[-1]
        x, y, c = _position()
        me = 4 * x + 2 * y + c
        for i, spec in enumerate(specs):
            blk = _device_block(refs[i], spec, me)
            for k, (rx, ry) in enumerate(CHIP_FLIPS):
                pltpu.make_async_remote_copy(
                    src_ref=blk, dst_ref=blk, send_sem=send_sems.at[nf * i + k], recv_sem=recv_sems.at[nf * i + k],
                    device_id=(_flip(x, rx), _flip(y, ry), c), device_id_type=MESH).start()
        token[...] = jnp.zeros(TOKEN.shape, TOKEN.dtype)

    sems = pltpu.SemaphoreType.DMA((nf * n,))
    return pl.pallas_call(
        body, name=name, out_shape=(sems, sems, *[pltpu.HBM(t.shape, t.dtype) for t in lands], TOKEN),
        in_specs=(HBM_SPEC,) * n + (ANY_SPEC,),
        out_specs=(SEM_SPEC, SEM_SPEC) + (HBM_SPEC,) * n + (pl.BlockSpec(memory_space=pltpu.VMEM),),
        input_output_aliases={i: 2 + i for i in range(n)}, compiler_params=pltpu.CompilerParams(has_side_effects=DATAFLOW),
    )(*[pltpu.with_memory_space_constraint(t, pltpu.HBM) for t in lands], after)


def _ag_chips_wait(send_sems, recv_sems, lands, specs, after, name):
    n = len(lands)
    nf = len(CHIP_FLIPS)

    def body(*refs):
        send_sems, recv_sems = refs[n], refs[n + 1]
        x, y, c = _position()
        me = 4 * x + 2 * y + c
        for i, spec in enumerate(specs):
            for k, (rx, ry) in enumerate(CHIP_FLIPS):
                px, py = _flip(x, rx), _flip(y, ry)
                cp = pltpu.make_async_remote_copy(
                    src_ref=_device_block(refs[i], spec, me), dst_ref=_device_block(refs[i], spec, 4 * px + 2 * py + c),
                    send_sem=send_sems.at[nf * i + k], recv_sem=recv_sems.at[nf * i + k],
                    device_id=(px, py, c), device_id_type=MESH)
                cp.wait_send()
                cp.wait_recv()

    return pl.pallas_call(
        body, name=name, out_shape=tuple(pltpu.HBM(t.shape, t.dtype) for t in lands),
        in_specs=(HBM_SPEC,) * n + (SEM_SPEC, SEM_SPEC, ANY_SPEC), out_specs=(HBM_SPEC,) * n,
        input_output_aliases={i: i for i in range(n)}, compiler_params=pltpu.CompilerParams(has_side_effects=DATAFLOW),
    )(*lands, send_sems, recv_sems, after)


def _ag_pair_forward(lands, specs, name):
    n = len(lands)
    nr = len(CHIP_RELATIONS)

    def body(*refs):
        outs, send_sems, recv_sems = refs[n:2 * n], refs[2 * n], refs[2 * n + 1]
        x, y, c = _position()
        copies = []
        for i, spec in enumerate(specs):
            for k, (rx, ry) in enumerate(CHIP_RELATIONS):
                chip = 4 * _flip(x, rx) + 2 * _flip(y, ry)
                held = _device_block(outs[i], spec, chip + c)
                sems = dict(send_sem=send_sems.at[nr * i + k], recv_sem=recv_sems.at[nr * i + k],
                            device_id=(x, y, 1 - c), device_id_type=MESH)
                mine = pltpu.make_async_remote_copy(src_ref=held, dst_ref=held, **sems)
                theirs = pltpu.make_async_remote_copy(src_ref=held, dst_ref=_device_block(outs[i], spec, chip + 1 - c), **sems)
                copies.append((mine, theirs))
        for mine, _ in copies:
            mine.start()
        for mine, theirs in copies:
            mine.wait_send()
            theirs.wait_recv()

    sems = pltpu.SemaphoreType.DMA((nr * n,))
    return pl.pallas_call(
        body, out_shape=tuple(jax.ShapeDtypeStruct(t.shape, t.dtype) for t in lands), in_specs=[HBM_SPEC] * n,
        out_specs=(HBM_SPEC,) * n, input_output_aliases={i: i for i in range(n)}, scratch_shapes=[sems, sems], name=name)(*lands)


def _rs_chips_start(pair, name):
    _, r_dim, c_dim = pair.shape
    n = len(CHIP_FLIPS)

    def body(pair_ref, far_ref, send_sems, recv_sems, pair_thru, far_thru, token):
        for cp in _chip_copies(pair_ref, lambda k, me, peer: k + 1, far_ref, lambda k, me, peer: k, send_sems, recv_sems):
            cp.start()
        token[...] = jnp.zeros(TOKEN.shape, TOKEN.dtype)

    far = lax.empty((n, r_dim, c_dim), pair.dtype)
    return pl.pallas_call(
        body, name=name,
        out_shape=(pltpu.SemaphoreType.DMA((n,)), pltpu.SemaphoreType.DMA((n,)), pltpu.HBM(pair.shape, pair.dtype),
                   pltpu.HBM(far.shape, far.dtype), TOKEN),
        in_specs=(HBM_SPEC, HBM_SPEC),
        out_specs=(SEM_SPEC, SEM_SPEC, HBM_SPEC, HBM_SPEC, pl.BlockSpec(memory_space=pltpu.VMEM)),
        input_output_aliases={0: 2, 1: 3}, compiler_params=pltpu.CompilerParams(has_side_effects=DATAFLOW),
    )(pltpu.with_memory_space_constraint(pair, pltpu.HBM), pltpu.with_memory_space_constraint(far, pltpu.HBM))


def _rs_chips_wait(send_sems, recv_sems, pair, far, after, name):
    def body(pair_ref, far_ref, send_sems, recv_sems, after_ref, pair_out, far_out):
        for cp in _chip_copies(pair_ref, lambda k, me, peer: k + 1, far_ref, lambda k, me, peer: k, send_sems, recv_sems):
            cp.wait_send()
            cp.wait_recv()

    return pl.pallas_call(
        body, name=name, out_shape=(pltpu.HBM(pair.shape, pair.dtype), pltpu.HBM(far.shape, far.dtype)),
        in_specs=(HBM_SPEC, HBM_SPEC, SEM_SPEC, SEM_SPEC, ANY_SPEC),
        out_specs=(HBM_SPEC, HBM_SPEC), input_output_aliases={0: 0, 1: 1},
        compiler_params=pltpu.CompilerParams(has_side_effects=DATAFLOW),
    )(pair, far, send_sems, recv_sems, after)


def _rs_final_sum(pair, far, name):
    _, r_dim, c_dim = pair.shape
    tr = _tile(r_dim, 1200, SUBLANE)

    def body(p_ref, f0_ref, f1_ref, f2_ref, o_ref):
        o_ref[...] = ((p_ref[...].astype(F32) + f0_ref[...].astype(F32)) + f1_ref[...].astype(F32)) + f2_ref[...].astype(F32)

    def slot(k):
        return _bs((None, tr, c_dim), lambda i: (k, i, 0))

    return pl.pallas_call(
        body, out_shape=jax.ShapeDtypeStruct((r_dim, c_dim), F32), grid=(r_dim // tr,),
        in_specs=[slot(0), slot(0), slot(1), slot(2)], out_specs=_bs((tr, c_dim), lambda i: (i, 0)), name=name,
        compiler_params=_params("parallel"))(pair, far, far, far)


def _reduce_scatter_begin(g, name):
    recv = _rs_pair_exchange(g, name + "_pair")
    pair = _rs_pair_sum(g, recv, name + "_pairsum")
    return _rs_chips_start(pair, name + "_chips_start")


def _reduce_scatter_end(state, after, name):
    send_sems, recv_sems, pair, far, _ = state
    pair, far = _rs_chips_wait(send_sems, recv_sems, pair, far, after, name + "_chips_wait")
    return _rs_final_sum(pair, far, name + "_sum")


MATRICES = (("w_in", True), ("w_out", False), ("w_q", False), ("w_kv", True), ("w_o", False), ("w_up", True),
            ("w_down", False), ("w_conv_out", True), ("w_pool_grp", True))
PARTS = (("mix", ("w_in", "w_conv_out", "w_pool_grp", "w_out")), ("rest", ("w_q", "w_kv", "w_o", "w_up", "w_down")))


def _to_rows(name, transposed, w, d_model):
    if name == "w_pool_grp":
        w = jnp.swapaxes(w, 1, 2)
    elif transposed:
        w = w.T
    return w.reshape(-1, d_model)


def _from_rows(name, transposed, rows, shard_shape):
    if name == "w_pool_grp":
        g, i, o = shard_shape
        return jnp.swapaxes(rows.reshape(g, o, i), 1, 2)
    if transposed:
        return rows.reshape(shard_shape[1], shard_shape[0]).T
    return rows.reshape(shard_shape)


def _scatter_blocks(name, full, shard_shape, d_model, n_dev=N_DEV):
    if name == "w_pool_grp":
        g, i, o = shard_shape
        return jnp.swapaxes(full.reshape(g, n_dev, o, i), 0, 1).reshape(n_dev, -1, d_model)
    return full.reshape(n_dev, -1, d_model)


def kernel(x, mem, mix_norm_g, w_in, conv_dw_w, conv_dw_b, conv_ln_g, conv_ln_b, w_conv_out, w_pool_grp, pool_scale, w_out, xattn_norm_g, mem_norm_g, w_q, w_kv, w_o, ffn_norm_g, w_up, ffn_dw_w, w_down, final_norm_g, loss_target, m_mix_norm_g, m_w_in, m_conv_dw_w, m_conv_dw_b, m_conv_ln_g, m_conv_ln_b, m_w_conv_out, m_w_pool_grp, m_pool_scale, m_w_out, m_xattn_norm_g, m_mem_norm_g, m_w_q, m_w_kv, m_w_o, m_ffn_norm_g, m_w_up, m_ffn_dw_w, m_w_down, m_final_norm_g, v_mix_norm_g, v_w_in, v_conv_dw_w, v_conv_dw_b, v_conv_ln_g, v_conv_ln_b, v_w_conv_out, v_w_pool_grp, v_pool_scale, v_w_out, v_xattn_norm_g, v_mem_norm_g, v_w_q, v_w_kv, v_w_o, v_ffn_norm_g, v_w_up, v_ffn_dw_w, v_w_down, v_final_norm_g):
    p = dict(locals())
    weight_names = ["mix_norm_g", "w_in", "conv_dw_w", "conv_dw_b", "conv_ln_g", "conv_ln_b", "w_conv_out",
                    "w_pool_grp", "pool_scale", "w_out", "xattn_norm_g", "mem_norm_g", "w_q", "w_kv", "w_o",
                    "ffn_norm_g", "w_up", "ffn_dw_w", "w_down", "final_norm_g"]
    n_batch, seq, d_model = x.shape
    m_len = mem.shape[1]
    depth = w_in.shape[0]
    assert depth == 2, "the exchange schedule below is written for two layers"
    t_dim = n_batch * seq
    c_conv = conv_dw_b.shape[1]
    n_groups = w_pool_grp.shape[1]
    assert w_pool_grp.shape[2] == LANE and c_conv % LANE == 0 and n_groups * LANE == c_conv
    gate_col0 = 2 * c_conv + n_groups * LANE
    pool_col0 = (2 * c_conv) // LANE

    dev = 4 * lax.axis_index("x") + 2 * lax.axis_index("y") + lax.axis_index("c")
    filt = jnp.concatenate([conv_dw_w.reshape(-1), ffn_dw_w.reshape(-1)])
    filt_rows = lax.bitcast_convert_type(filt, BF16).reshape(-1, d_model)
    transposed = dict(MATRICES)
    layout = {part: [(name, transposed[name], _to_rows(name, transposed[name], p[name][0], d_model).shape[0])
                     for name in names] for part, names in PARTS}

    def landing(name, shard):
        if name == "w_pool_grp":
            block, axis = jnp.swapaxes(shard, 1, 2), 1
        elif name == "filt":
            block, axis = shard, 0
        else:
            block, axis = (shard.T if transposed[name] else shard), 0
        block = block.astype(BF16)
        rows = block.shape[axis]
        shape = block.shape[:axis] + (N_DEV * rows,) + block.shape[axis + 1:]
        start = (0,) * axis + (dev * rows,) + (0,) * (block.ndim - axis - 1)
        return lax.dynamic_update_slice(lax.empty(shape, BF16), block, start), (rows, axis)

    ag_state = {}
    after = filt_rows
    for l in range(depth):
        for part, names in PARTS:
            items = [(name, p[name][l]) for name in names]
            if (l, part) == (0, PARTS[0][0]):
                items.append(("filt", filt_rows))
            lands, specs = zip(*[landing(name, shard) for name, shard in items])
            out = _ag_chips_start(lands, specs, after, f"ag{l}{part}_chips_start")
            ag_state[(l, part)] = ([name for name, _ in items], specs, out)
            after = out[-1]
    all_started = after

    def gathered(l, part, after):
        names, specs, out = ag_state[(l, part)]
        lands = _ag_chips_wait(out[0], out[1], out[2:-1], specs, after, f"ag{l}{part}_chips_wait")
        lands = _ag_pair_forward(lands, specs, f"ag{l}{part}_pair_forward")
        return dict(zip(names, lands))

    full = [dict() for _ in range(depth)]
    full[0].update(gathered(0, PARTS[0][0], all_started))
    filt_all = lax.bitcast_convert_type(full[0]["filt"].reshape(N_DEV, -1, 2), F32)
    n_cw = conv_dw_w.size
    kc, cs = conv_dw_w.shape[1:]
    kf, fs = ffn_dw_w.shape[1:]
    conv_w_full = jnp.moveaxis(filt_all[:, :n_cw].reshape(N_DEV, depth, kc, cs), 0, 2).reshape(depth, kc, N_DEV * cs)
    ffn_w_full = jnp.moveaxis(filt_all[:, n_cw:].reshape(N_DEV, depth, kf, fs), 0, 2).reshape(depth, kf, N_DEV * fs)

    vec = lambda a: a.reshape(1, -1)
    x2d = x.reshape(t_dim, d_model)
    mem2d = mem.reshape(n_batch * m_len, d_model)
    mem_n = _rmsnorm_fwd(mem2d, vec(mem_norm_g), "mem_norm")

    saved = []
    xc = x2d
    for l in range(depth):
        if l > 0:
            full[l].update(gathered(l, PARTS[0][0], xc))
        wl = full[l]
        s = {"x0": xc}
        s["h"] = _rmsnorm_fwd(xc, vec(mix_norm_g[l]), f"mix_norm_l{l}")
        s["proj"] = _matmul(s["h"], wl["w_in"], "nt", f"in_proj_l{l}", out_dtype=BF16)
        s["y1"] = _glu_conv_fwd(s["proj"], conv_w_full[l], vec(conv_dw_b[l]), n_batch, seq, f"glu_conv_l{l}")
        s["y3"] = _ln_silu_fwd(s["y1"], vec(conv_ln_g[l]), vec(conv_ln_b[l]), f"ln_silu_l{l}")
        s["yc"] = _matmul(s["y3"], wl["w_conv_out"], "nt", f"conv_out_l{l}", out_dtype=BF16)
        s["zp"] = _pool_fwd(s["proj"], pool_col0, n_groups, n_batch, seq, f"pool_l{l}")
        s["yp"] = _grouped(s["zp"], wl["w_pool_grp"], "nt", f"pool_proj_l{l}", out_dtype=BF16)
        s["merged"] = _merge_fwd(s["proj"], gate_col0, s["yc"], s["yp"], vec(pool_scale[l]), f"merge_l{l}")
        s["x1"] = _matmul(s["merged"], wl["w_out"], "nn", f"mix_out_l{l}", res=xc)
        wl.update(gathered(l, PARTS[1][0], s["x1"]))
        half_up = wl["w_up"].shape[0] // 2
        up_gate, up_val = (0, half_up), (half_up, half_up)
        s["hq"] = _rmsnorm_fwd(s["x1"], vec(xattn_norm_g[l]), f"xattn_norm_l{l}")
        s["q"] = _matmul(s["hq"], wl["w_q"], "nn", f"q_proj_l{l}", out_dtype=BF16)
        s["kv"] = _matmul(mem_n, wl["w_kv"], "nt", f"kv_proj_l{l}", out_dtype=BF16)
        s["att"] = _attn_fwd(s["q"], s["kv"], n_batch, seq, m_len, f"attn_l{l}")
        s["x2"] = _matmul(s["att"], wl["w_o"], "nn", f"attn_out_l{l}", res=s["x1"])
        s["hf"] = _rmsnorm_fwd(s["x2"], vec(ffn_norm_g[l]), f"ffn_norm_l{l}")
        s["up_g"] = _matmul(s["hf"], wl["w_up"], "nt", f"up_proj_gate_l{l}", out_dtype=BF16, b_window=up_gate)
        s["up_v"] = _matmul(s["hf"], wl["w_up"], "nt", f"up_proj_val_l{l}", out_dtype=BF16, b_window=up_val)
        s["act"] = _ffn_act_fwd(s["up_g"], s["up_v"], ffn_w_full[l], n_batch, seq, f"ffn_act_l{l}")
        xc = _matmul(s["act"], wl["w_down"], "nn", f"down_proj_l{l}", res=s["x2"])
        saved.append(s)

    dx, dxb, dg_final, loss_part = _loss_head(xc, vec(final_norm_g), loss_target.reshape(t_dim, d_model), "loss_head")

    small = {"final_norm_g": dg_final.reshape(-1)}
    big = [dict() for _ in range(depth)]
    rs_state = {}
    rs_after = loss_part

    def rs_begin(l, part):
        pack = lax.empty((N_DEV, sum(nrows for _, _, nrows in layout[part]), d_model), BF16)
        row0 = 0
        for name, _, nrows in layout[part]:
            pieces = big[l][name] if isinstance(big[l][name], tuple) else (big[l][name],)
            d0 = 0
            for piece in pieces:
                blocks = _scatter_blocks(name, piece, p[name].shape[1:], d_model, N_DEV // len(pieces)).astype(BF16)
                pack = lax.dynamic_update_slice(pack, blocks, (d0, row0, 0))
                d0 += blocks.shape[0]
            row0 += nrows
        rs_state[(l, part)] = _reduce_scatter_begin(pack, f"rs{l}{part}")
        return rs_state[(l, part)][4]

    dmem_n = None
    for l in reversed(range(depth)):
        wl, s = full[l], saved[l]
        sm = {}
        dact = _matmul(dxb, wl["w_down"], "nt", f"d_act_l{l}", out_dtype=BF16, after=rs_after)
        big[l]["w_down"] = _matmul(s["act"], dxb, "tn", f"d_w_down_l{l}", out_dtype=BF16)
        dup_g, dup_v, dwf_g, dwf_v = _ffn_act_bwd(s["up_g"], s["up_v"], ffn_w_full[l], dact, n_batch, seq,
                                                  f"ffn_act_bwd_l{l}")
        sm["ffn_dw_w"] = jnp.concatenate([dwf_g, dwf_v], axis=1)
        dhf = _matmul(dup_g, wl["w_up"], "nn", f"d_hf_gate_l{l}", b_window=up_gate)
        dx, dxb, dg = _matmul_rmsnorm_bwd(dup_v, wl["w_up"], "nn", s["x2"], vec(ffn_norm_g[l]), dx,
                                          f"d_hf_val_ffn_norm_bwd_l{l}", res=dhf, b_window=up_val)
        big[l]["w_up"] = (_matmul(dup_g, s["hf"], "tn", f"d_w_up_gate_l{l}", out_dtype=BF16),
                          _matmul(dup_v, s["hf"], "tn", f"d_w_up_val_l{l}", out_dtype=BF16))
        sm["ffn_norm_g"] = dg
        datt = _matmul(dxb, wl["w_o"], "nt", f"d_att_l{l}", out_dtype=BF16, after=rs_after)
        big[l]["w_o"] = _matmul(s["att"], dxb, "tn", f"d_w_o_l{l}", out_dtype=BF16)
        dq, dk, dv = _attn_bwd(s["q"], s["kv"], datt, n_batch, seq, m_len, f"attn_bwd_l{l}")
        dkv = jnp.concatenate([dk, dv], axis=1)
        big[l]["w_kv"] = _matmul(dkv, mem_n, "tn", f"d_w_kv_l{l}", out_dtype=BF16)
        dmem_n = _matmul(dkv, wl["w_kv"], "nn", f"d_mem_l{l}", res=dmem_n)
        big[l]["w_q"] = _matmul(s["hq"], dq, "tn", f"d_w_q_l{l}", out_dtype=BF16)
        dx, dxb, dg = _matmul_rmsnorm_bwd(dq, wl["w_q"], "nt", s["x1"], vec(xattn_norm_g[l]), dx,
                                          f"d_hq_xattn_norm_bwd_l{l}")
        sm["xattn_norm_g"] = dg
        rs_after = rs_begin(l, PARTS[1][0])
        dmerged = _matmul(dxb, wl["w_out"], "nt", f"d_merged_l{l}", out_dtype=BF16, after=rs_after)
        big[l]["w_out"] = _matmul(s["merged"], dxb, "tn", f"d_w_out_l{l}", out_dtype=BF16)
        dgc, dgp, dyc, dyp, dscale = _merge_bwd(s["proj"], gate_col0, s["yc"], s["yp"], vec(pool_scale[l]), dmerged,
                                                f"merge_bwd_l{l}")
        sm["pool_scale"] = dscale
        dzp = _grouped(dyp, wl["w_pool_grp"], "nn", f"d_zp_l{l}", out_dtype=BF16)
        big[l]["w_pool_grp"] = _grouped_tn(dyp, s["zp"], n_groups, f"d_w_pool_l{l}")
        du = _pool_bwd(dzp, n_groups, n_batch, seq, f"pool_bwd_l{l}")
        dy3 = _matmul(dyc, wl["w_conv_out"], "nn", f"d_y3_l{l}", out_dtype=BF16)
        big[l]["w_conv_out"] = _matmul(dyc, s["y3"], "tn", f"d_w_conv_out_l{l}", out_dtype=BF16)
        dy1, dlg, dlb = _ln_silu_bwd(s["y1"], vec(conv_ln_g[l]), vec(conv_ln_b[l]), dy3, f"ln_silu_bwd_l{l}")
        sm["conv_ln_g"], sm["conv_ln_b"] = dlg, dlb
        da, dgl, dcw, dcb = _glu_conv_bwd(s["proj"], conv_w_full[l], dy1, n_batch, seq, f"glu_conv_bwd_l{l}")
        sm["conv_dw_w"], sm["conv_dw_b"] = dcw, dcb
        dproj = jnp.concatenate([da, dgl, du, dgc, dgp], axis=1)
        big[l]["w_in"] = _matmul(dproj, s["h"], "tn", f"d_w_in_l{l}", out_dtype=BF16)
        dx, dxb, dg = _matmul_rmsnorm_bwd(dproj, wl["w_in"], "nn", s["x0"], vec(mix_norm_g[l]), dx,
                                          f"d_h_mix_norm_bwd_l{l}")
        sm["mix_norm_g"] = dg
        for k, val in sm.items():
            small[(l, k)] = val.reshape(-1)
        rs_after = rs_begin(l, PARTS[0][0])
    _, _, dg_mem = _rmsnorm_bwd(mem2d, vec(mem_norm_g), dmem_n, None, "mem_norm_bwd")
    small["mem_norm_g"] = dg_mem.reshape(-1)
    small["loss"] = loss_part.reshape(-1)

    grads = {}
    per_layer = {name: [None] * depth for name, _ in MATRICES}
    for l in reversed(range(depth)):
        for part, _ in reversed(PARTS):
            mat_grads = _reduce_scatter_end(rs_state[(l, part)], rs_after, f"rs{l}{part}")
            row0 = 0
            for name, tr, nrows in layout[part]:
                per_layer[name][l] = _from_rows(name, tr, mat_grads[row0:row0 + nrows], p[name].shape[1:])
                row0 += nrows
    for name, _ in MATRICES:
        grads[name] = jnp.stack(per_layer[name])

    keys = list(small.keys())
    flat = jnp.concatenate([small[k] for k in keys])
    n_small = flat.shape[0]
    rows_small = -(-n_small // (SUBLANE * d_model)) * SUBLANE
    flat = jnp.pad(flat, (0, rows_small * d_model - n_small)).reshape(rows_small, d_model)
    every = _all_gather(flat, "small_all_gather")
    total = _sum_rows([every[i] for i in range(N_DEV)], F32, "small_sum").reshape(-1)
    off = 0
    red = {}
    for k in keys:
        red[k] = total[off:off + small[k].shape[0]]
        off += small[k].shape[0]
    loss = red["loss"][0]
    for name in ("mix_norm_g", "conv_dw_b", "conv_ln_g", "conv_ln_b", "pool_scale", "xattn_norm_g", "ffn_norm_g"):
        grads[name] = jnp.stack([red[(l, name)] for l in range(depth)])
    grads["conv_dw_w"] = jnp.stack([
        lax.dynamic_slice_in_dim(red[(l, "conv_dw_w")].reshape(kc, N_DEV * cs), dev * cs, cs, axis=1)
        for l in range(depth)])
    grads["ffn_dw_w"] = jnp.stack([
        lax.dynamic_slice_in_dim(red[(l, "ffn_dw_w")].reshape(kf, N_DEV * fs), dev * fs, fs, axis=1)
        for l in range(depth)])
    grads["mem_norm_g"] = red["mem_norm_g"]
    grads["final_norm_g"] = red["final_norm_g"]

    deltas, new_m, new_v = {}, {}, {}
    for name in weight_names:
        deltas[name], new_m[name], new_v[name] = _adamw(p[name], grads[name], p["m_" + name], p["v_" + name],
                                                        f"adamw_{name}")
    grad_x = dx.reshape(n_batch, seq, d_model)
    return (loss, grad_x, *[grads[n] for n in weight_names], *[deltas[n] for n in weight_names],
            *[new_m[n] for n in weight_names], *[new_v[n] for n in weight_names])
```

```python
import functools

import jax
import jax.numpy as jnp
from jax import lax
from jax.experimental import pallas as pl
from jax.experimental.pallas import tpu as pltpu

F32 = jnp.float32
BF16 = jnp.bfloat16
MESH = pl.DeviceIdType.MESH

N_DEV = 8
EPS = 1e-6
V7X_VMEM_BYTES = 64 * 1024 * 1024
VMEM_LIMIT = (V7X_VMEM_BYTES * 3) // 4
LANE = 128
SUBLANE = 8

CONV_HALO = 32
POOL_HALO = 16
FFN_HALO = 8
POOL_WINDOW_MAX = 16
XA_HEADS = 4

ADAM_LR = 0.001
ADAM_B1 = 0.9
ADAM_B2 = 0.999
ADAM_EPS = 1e-08
ADAM_WD = 0.01
ADAM_STEP = 10

GELU_C0 = 0.7978845608028654
GELU_C1 = 0.044715


ANY_SPEC = pl.BlockSpec(memory_space=pl.ANY)


def _tile(n, cap, mult=LANE):
    if n <= cap:
        return n
    best = None
    for d in range(mult, cap + 1, mult):
        if n % d == 0:
            best = d
    assert best is not None, (n, cap, mult)
    return best


def _params(*sem):
    return pltpu.CompilerParams(dimension_semantics=sem, vmem_limit_bytes=VMEM_LIMIT)


def _sig(x):
    return 1.0 / (1.0 + jnp.exp(-x))


def _bs(shape, imap):
    return pl.BlockSpec(shape, imap)


def _mxu_tile(n, cap):
    if n <= cap:
        return n
    best = {mult: max((d for d in range(mult, cap + 1, mult) if n % d == 0), default=0) for mult in (2 * LANE, LANE)}
    assert best[LANE] > 0, (n, cap)
    return best[2 * LANE] if 2 * best[2 * LANE] >= best[LANE] else best[LANE]


def _matmul(a, b, mode, name, res=None, out_dtype=F32, after=None, b_window=None):
    b_row0, b_rows = b_window if b_window is not None else (0, b.shape[0])
    if mode == "tn":
        k_dim, m_dim = a.shape
        k2, n_dim = b_rows, b.shape[1]
    elif mode == "nn":
        m_dim, k_dim = a.shape
        k2, n_dim = b_rows, b.shape[1]
    else:
        m_dim, k_dim = a.shape
        n_dim, k2 = b_rows, b.shape[1]
    assert k_dim == k2, (name, a.shape, b.shape)
    size = lambda t: jnp.dtype(t).itemsize
    tm = _mxu_tile(m_dim, 2816 if mode == "tn" else 1024)
    tn = _mxu_tile(n_dim, 2816)
    fixed = tm * tn * (2 * size(out_dtype) + (2 * size(res.dtype) if res is not None else 0) + 4)
    for cap in (2816, 2048, 1792, 1024, 512):
        tk = _mxu_tile(k_dim, cap)
        if fixed + 2 * tk * (tm * size(a.dtype) + tn * size(b.dtype)) <= VMEM_LIMIT - 8 * 1024 * 1024:
            break
    nk = k_dim // tk
    use_acc = nk > 1 and out_dtype != F32
    if mode == "tn":
        a_spec, ca = _bs((tk, tm), lambda i, j, k: (k, i)), 0
    else:
        a_spec, ca = _bs((tm, tk), lambda i, j, k: (i, k)), 1
    if mode == "nt":
        assert b_row0 % tn == 0
        b_spec, cb = _bs((tn, tk), lambda i, j, k: (j + b_row0 // tn, k)), 1
    else:
        assert b_row0 % tk == 0
        b_spec, cb = _bs((tk, tn), lambda i, j, k: (k + b_row0 // tk, j)), 0
    dims = (((ca,), (cb,)), ((), ()))
    o_spec = _bs((tm, tn), lambda i, j, k: (i, j))
    has_res = res is not None

    def body(*refs):
        a_ref, b_ref = refs[:2]
        r_ref = refs[2] if has_res else None
        o_ref = refs[n_in]
        k = pl.program_id(2)
        part = lax.dot_general(a_ref[...].astype(BF16), b_ref[...].astype(BF16), dims,
                               preferred_element_type=F32)
        if nk == 1:
            if has_res:
                part = part + r_ref[...].astype(F32)
            o_ref[...] = part.astype(out_dtype)
            return
        acc = refs[-1] if use_acc else o_ref

        @pl.when(k == 0)
        def _():
            acc[...] = part + r_ref[...].astype(F32) if has_res else part

        @pl.when(k > 0)
        def _():
            acc[...] += part

        if use_acc:
            @pl.when(k == nk - 1)
            def _():
                o_ref[...] = acc[...].astype(out_dtype)

    in_specs = [a_spec, b_spec] + ([o_spec] if has_res else [])
    args = (a, b) + ((res,) if has_res else ())
    if after is not None:
        in_specs.append(ANY_SPEC)
        args += (after,)
    n_in = len(args)
    return pl.pallas_call(
        body, out_shape=jax.ShapeDtypeStruct((m_dim, n_dim), out_dtype),
        grid=(m_dim // tm, n_dim // tn, nk), in_specs=in_specs, out_specs=o_spec,
        scratch_shapes=[pltpu.VMEM((tm, tn), F32)] if use_acc else [], name=name,
        compiler_params=_params("parallel", "parallel", "arbitrary"))(*args)


def _grouped(a, w, mode, name, out_dtype=F32):
    t_dim = a.shape[0]
    g_dim, r_dim, c_dim = w.shape
    ka, no = (c_dim, r_dim) if mode == "nt" else (r_dim, c_dim)
    tm = _tile(t_dim, 2048)
    dims = (((1,), (1 if mode == "nt" else 0,)), ((), ()))

    def body(a_ref, w_ref, o_ref):
        o_ref[...] = lax.dot_general(a_ref[...].astype(BF16), w_ref[...].astype(BF16), dims,
                                     preferred_element_type=F32).astype(out_dtype)

    return pl.pallas_call(
        body, out_shape=jax.ShapeDtypeStruct((t_dim, g_dim * no), out_dtype),
        grid=(t_dim // tm, g_dim),
        in_specs=[_bs((tm, ka), lambda i, g: (i, g)), _bs((None, r_dim, c_dim), lambda i, g: (g, 0, 0))],
        out_specs=_bs((tm, no), lambda i, g: (i, g)), name=name,
        compiler_params=_params("parallel", "parallel"))(a, w)


def _grouped_tn(a, b, g_dim, name):
    t_dim = a.shape[0]
    ra = a.shape[1] // g_dim
    cb = b.shape[1] // g_dim
    tm = _tile(t_dim, 2048)
    nt = t_dim // tm

    def body(a_ref, b_ref, o_ref):
        part = lax.dot_general(a_ref[...].astype(BF16), b_ref[...].astype(BF16), (((0,), (0,)), ((), ())),
                               preferred_element_type=F32)

        @pl.when(pl.program_id(1) == 0)
        def _():
            o_ref[...] = part

        @pl.when(pl.program_id(1) > 0)
        def _():
            o_ref[...] += part

    return pl.pallas_call(
        body, out_shape=jax.ShapeDtypeStruct((g_dim, ra, cb), F32), grid=(g_dim, nt),
        in_specs=[_bs((tm, ra), lambda g, i: (i, g)), _bs((tm, cb), lambda g, i: (i, g))],
        out_specs=_bs((None, ra, cb), lambda g, i: (g, 0, 0)), name=name,
        compiler_params=_params("parallel", "arbitrary"))(a, b)


def _rmsnorm_fwd(x, g, name, after=None):
    t_dim, d = x.shape
    tm = _tile(t_dim, 512)

    def body(x_ref, g_ref, *rest):
        o_ref = rest[-1]
        xv = x_ref[...]
        r = lax.rsqrt(jnp.mean(xv * xv, axis=-1, keepdims=True) + EPS)
        o_ref[...] = (xv * r * g_ref[...]).astype(BF16)

    return pl.pallas_call(
        body, out_shape=jax.ShapeDtypeStruct((t_dim, d), BF16), grid=(t_dim // tm,),
        in_specs=[_bs((tm, d), lambda i: (i, 0)), _bs((1, d), lambda i: (0, 0))] + ([ANY_SPEC] if after is not None else []),
        out_specs=_bs((tm, d), lambda i: (i, 0)), name=name,
        compiler_params=_params("parallel"))(x, g, *([after] if after is not None else []))


def _rmsnorm_bwd(x, g, dh, dx_in, name):
    t_dim, d = x.shape
    tm = _tile(t_dim, 512)
    has_in = dx_in is not None

    def body(*refs):
        if has_in:
            x_ref, g_ref, dh_ref, di_ref, dx_ref, dxb_ref, dg_ref = refs
        else:
            x_ref, g_ref, dh_ref, dx_ref, dxb_ref, dg_ref = refs
        xv = x_ref[...]
        r = lax.rsqrt(jnp.mean(xv * xv, axis=-1, keepdims=True) + EPS)
        xh = xv * r
        dhv = dh_ref[...].astype(F32)
        dxh = dhv * g_ref[...]
        dx = r * (dxh - xh * jnp.mean(dxh * xh, axis=-1, keepdims=True))
        if has_in:
            dx = dx + di_ref[...]
        dx_ref[...] = dx
        dxb_ref[...] = dx.astype(BF16)
        part = jnp.sum(dhv * xh, axis=0, keepdims=True)

        @pl.when(pl.program_id(0) == 0)
        def _():
            dg_ref[...] = part

        @pl.when(pl.program_id(0) > 0)
        def _():
            dg_ref[...] += part

    row = _bs((tm, d), lambda i: (i, 0))
    vec = _bs((1, d), lambda i: (0, 0))
    args = (x, g, dh) + ((dx_in,) if has_in else ())
    return pl.pallas_call(
        body, out_shape=(jax.ShapeDtypeStruct((t_dim, d), F32), jax.ShapeDtypeStruct((t_dim, d), BF16),
                         jax.ShapeDtypeStruct((1, d), F32)),
        grid=(t_dim // tm,), in_specs=[row, vec, row] + ([row] if has_in else []),
        out_specs=(row, row, vec), name=name, compiler_params=_params("arbitrary"))(*args)


def _matmul_rmsnorm_bwd(a, b, mode, x, g, dx_in, name, res=None, b_window=None):
    pieces = a if isinstance(a, tuple) else (a,)
    n_p = len(pieces)
    b_row0, b_rows = b_window if b_window is not None else (0, b.shape[0])
    m_dim, k_piece = pieces[0].shape
    assert all(t.shape == pieces[0].shape for t in pieces)
    k_dim = n_p * k_piece
    d = x.shape[1]
    assert (b_rows, b.shape[1]) == ((k_dim, d) if mode == "nn" else (d, k_dim)), (name, pieces[0].shape, b.shape)
    tm = _mxu_tile(m_dim, 512)
    tk = _mxu_tile(k_piece, 1792)
    nkp = k_piece // tk
    nk = n_p * nkp
    has_res = res is not None
    if mode == "nt":
        assert b_row0 == 0
        b_spec, cb = _bs((d, tk), lambda i, k: (0, k)), 1
    else:
        assert b_row0 % tk == 0
        b_spec, cb = _bs((tk, d), lambda i, k: (k + b_row0 // tk, 0)), 0
    dims = (((1,), (cb,)), ((), ()))

    def body(*refs):
        b_ref = refs[n_p]
        r_ref = refs[n_p + 1] if has_res else None
        x_ref, g_ref, di_ref, dx_ref, dxb_ref, dg_ref = refs[n_p + 1 + has_res:n_p + 7 + has_res]
        i, k = pl.program_id(0), pl.program_id(1)

        def finish(dhv):
            if has_res:
                dhv = dhv + r_ref[...].astype(F32)
            xv = x_ref[...]
            r = lax.rsqrt(jnp.mean(xv * xv, axis=-1, keepdims=True) + EPS)
            xh = xv * r
            dxh = dhv * g_ref[...]
            dx = r * (dxh - xh * jnp.mean(dxh * xh, axis=-1, keepdims=True)) + di_ref[...]
            dx_ref[...] = dx
            dxb_ref[...] = dx.astype(BF16)
            dg_part = jnp.sum(dhv * xh, axis=0, keepdims=True)

            @pl.when(i == 0)
            def _():
                dg_ref[...] = dg_part

            @pl.when(i > 0)
            def _():
                dg_ref[...] += dg_part

        def step(a_ref):
            part = lax.dot_general(a_ref[...].astype(BF16), b_ref[...].astype(BF16), dims, preferred_element_type=F32)
            if nk == 1:
                finish(part)
                return
            acc = refs[-1]

            @pl.when(k == 0)
            def _():
                acc[...] = part

            @pl.when(jnp.logical_and(k > 0, k < nk - 1))
            def _():
                acc[...] += part

            @pl.when(k == nk - 1)
            def _():
                finish(acc[...] + part)

        if n_p == 1:
            step(refs[0])
        else:
            for q in range(n_p):
                pl.when(jnp.logical_and(k >= q * nkp, k < (q + 1) * nkp))(functools.partial(step, refs[q]))

    row = _bs((tm, d), lambda i, k: (i, 0))
    vec = _bs((1, d), lambda i, k: (0, 0))
    a_specs = [_bs((tm, tk), lambda i, k, q=q: (i, jnp.clip(k - q * nkp, 0, nkp - 1))) for q in range(n_p)]
    in_specs = a_specs + [b_spec] + ([row] if has_res else []) + [row, vec, row]
    args = pieces + (b,) + ((res,) if has_res else ()) + (x, g, dx_in)
    return pl.pallas_call(
        body, out_shape=(jax.ShapeDtypeStruct((m_dim, d), F32), jax.ShapeDtypeStruct((m_dim, d), BF16),
                         jax.ShapeDtypeStruct((1, d), F32)),
        grid=(m_dim // tm, nk), in_specs=in_specs, out_specs=(row, row, vec),
        scratch_shapes=[pltpu.VMEM((tm, d), F32)] if nk > 1 else [], name=name,
        compiler_params=_params("arbitrary", "arbitrary"))(*args)


def _loss_head(x, g, tgt, name):
    t_dim, d = x.shape
    tm = _tile(t_dim, 512)

    def body(x_ref, g_ref, t_ref, dx_ref, dxb_ref, dg_ref, loss_ref):
        xv = x_ref[...]
        gv = g_ref[...]
        r = lax.rsqrt(jnp.mean(xv * xv, axis=-1, keepdims=True) + EPS)
        xh = xv * r
        err = xh * gv - t_ref[...]
        dy = err * (1.0 / d)
        dxh = dy * gv
        dx = r * (dxh - xh * jnp.mean(dxh * xh, axis=-1, keepdims=True))
        dx_ref[...] = dx
        dxb_ref[...] = dx.astype(BF16)
        dg_part = jnp.sum(dy * xh, axis=0, keepdims=True)
        loss_part = jnp.full((1, LANE), 0.5 * jnp.sum(jnp.mean(err * err, axis=-1, keepdims=True)), F32)

        @pl.when(pl.program_id(0) == 0)
        def _():
            dg_ref[...] = dg_part
            loss_ref[...] = loss_part

        @pl.when(pl.program_id(0) > 0)
        def _():
            dg_ref[...] += dg_part
            loss_ref[...] += loss_part

    row = _bs((tm, d), lambda i: (i, 0))
    vec = _bs((1, d), lambda i: (0, 0))
    return pl.pallas_call(
        body, out_shape=(jax.ShapeDtypeStruct((t_dim, d), F32), jax.ShapeDtypeStruct((t_dim, d), BF16),
                         jax.ShapeDtypeStruct((1, d), F32), jax.ShapeDtypeStruct((1, LANE), F32)),
        grid=(t_dim // tm,), in_specs=[row, vec, row],
        out_specs=(row, row, vec, _bs((1, LANE), lambda i: (0, 0))), name=name,
        compiler_params=_params("arbitrary"))(x, g, tgt)


def _glu_conv_fwd(proj, dw_w, dw_b, n_batch, seq, name):
    kk, cc = dw_w.shape
    nj = cc // LANE
    ch = min(256, seq)

    def body(a_ref, gl_ref, w_ref, b_ref, o_ref, pad):
        pad[0:CONV_HALO, :] = jnp.zeros((CONV_HALO, LANE), F32)
        pad[CONV_HALO:CONV_HALO + seq, :] = a_ref[...].astype(F32) * _sig(gl_ref[...].astype(F32))
        for c0 in range(0, seq, ch):
            acc = jnp.broadcast_to(b_ref[...], (ch, LANE))
            for k in range(kk):
                acc = acc + w_ref[k:k + 1, :] * pad[pl.ds(c0 + CONV_HALO - (kk - 1) + k, ch), :]
            o_ref[c0:c0 + ch, :] = acc

    return pl.pallas_call(
        body, out_shape=jax.ShapeDtypeStruct((n_batch * seq, cc), F32), grid=(n_batch, nj),
        in_specs=[_bs((seq, LANE), lambda b, j: (b, j)), _bs((seq, LANE), lambda b, j: (b, nj + j)),
                  _bs((kk, LANE), lambda b, j: (0, j)), _bs((1, LANE), lambda b, j: (0, j))],
        out_specs=_bs((seq, LANE), lambda b, j: (b, j)),
        scratch_shapes=[pltpu.VMEM((seq + CONV_HALO, LANE), F32)], name=name,
        compiler_params=_params("parallel", "parallel"))(proj, proj, dw_w, dw_b)


def _glu_conv_bwd(proj, dw_w, dy1, n_batch, seq, name):
    kk, cc = dw_w.shape
    nj = cc // LANE
    ch = min(256, seq)

    def body(a_ref, gl_ref, dy_ref, w_ref, da_ref, dgl_ref, dw_ref, db_ref, padf, padb):
        first = pl.program_id(1) == 0
        padf[0:CONV_HALO, :] = jnp.zeros((CONV_HALO, LANE), F32)
        padf[CONV_HALO:CONV_HALO + seq, :] = a_ref[...].astype(F32) * _sig(gl_ref[...].astype(F32))
        padb[0:seq, :] = dy_ref[...]
        padb[seq:seq + CONV_HALO, :] = jnp.zeros((CONV_HALO, LANE), F32)

        @pl.when(first)
        def _():
            dw_ref[...] = jnp.zeros((kk, LANE), F32)
            db_ref[...] = jnp.zeros((1, LANE), F32)

        dws = [jnp.zeros((1, LANE), F32) for _ in range(kk)]
        for c0 in range(0, seq, ch):
            acc = jnp.zeros((ch, LANE), F32)
            y0 = padf[CONV_HALO + c0:CONV_HALO + c0 + ch, :]
            for k in range(kk):
                win = padb[pl.ds(c0 + (kk - 1) - k, ch), :]
                acc = acc + w_ref[k:k + 1, :] * win
                dws[k] = dws[k] + jnp.sum(win * y0, axis=0, keepdims=True)
            sg = _sig(gl_ref[c0:c0 + ch, :].astype(F32))
            da_ref[c0:c0 + ch, :] = (acc * sg).astype(BF16)
            dgl_ref[c0:c0 + ch, :] = (acc * a_ref[c0:c0 + ch, :].astype(F32) * sg * (1.0 - sg)).astype(BF16)
        for k in range(kk):
            dw_ref[k:k + 1, :] += dws[k]
        db_ref[...] += jnp.sum(dy_ref[...], axis=0, keepdims=True)

    tok = _bs((seq, LANE), lambda j, b: (b, j))
    t_dim = n_batch * seq
    return pl.pallas_call(
        body, out_shape=(jax.ShapeDtypeStruct((t_dim, cc), BF16), jax.ShapeDtypeStruct((t_dim, cc), BF16),
                         jax.ShapeDtypeStruct((kk, cc), F32), jax.ShapeDtypeStruct((1, cc), F32)),
        grid=(nj, n_batch),
        in_specs=[tok, _bs((seq, LANE), lambda j, b: (b, nj + j)), tok, _bs((kk, LANE), lambda j, b: (0, j))],
        out_specs=(tok, tok, _bs((kk, LANE), lambda j, b: (0, j)), _bs((1, LANE), lambda j, b: (0, j))),
        scratch_shapes=[pltpu.VMEM((seq + CONV_HALO, LANE), F32), pltpu.VMEM((seq + CONV_HALO, LANE), F32)],
        name=name, compiler_params=_params("parallel", "arbitrary"))(proj, proj, dy1, dw_w)


def _ln_silu_fwd(y1, g, b, name):
    t_dim, c = y1.shape
    tm = _tile(t_dim, 512)

    def body(y_ref, g_ref, b_ref, o_ref):
        yv = y_ref[...]
        xc = yv - jnp.mean(yv, axis=-1, keepdims=True)
        rstd = lax.rsqrt(jnp.mean(xc * xc, axis=-1, keepdims=True) + EPS)
        y2 = xc * rstd * g_ref[...] + b_ref[...]
        o_ref[...] = (y2 * _sig(y2)).astype(BF16)

    row = _bs((tm, c), lambda i: (i, 0))
    vec = _bs((1, c), lambda i: (0, 0))
    return pl.pallas_call(
        body, out_shape=jax.ShapeDtypeStruct((t_dim, c), BF16), grid=(t_dim // tm,),
        in_specs=[row, vec, vec], out_specs=row, name=name, compiler_params=_params("parallel"))(y1, g, b)


def _ln_silu_bwd(y1, g, b, dy3, name):
    t_dim, c = y1.shape
    tm = _tile(t_dim, 512)

    def body(y_ref, g_ref, b_ref, d_ref, dy_ref, dg_ref, db_ref):
        yv = y_ref[...]
        gv = g_ref[...]
        xc = yv - jnp.mean(yv, axis=-1, keepdims=True)
        rstd = lax.rsqrt(jnp.mean(xc * xc, axis=-1, keepdims=True) + EPS)
        yh = xc * rstd
        y2 = yh * gv + b_ref[...]
        s = _sig(y2)
        dy2 = d_ref[...].astype(F32) * (s * (1.0 + y2 * (1.0 - s)))
        dyh = dy2 * gv
        dy_ref[...] = rstd * (dyh - jnp.mean(dyh, axis=-1, keepdims=True)
                              - yh * jnp.mean(dyh * yh, axis=-1, keepdims=True))
        dg_part = jnp.sum(dy2 * yh, axis=0, keepdims=True)
        db_part = jnp.sum(dy2, axis=0, keepdims=True)

        @pl.when(pl.program_id(0) == 0)
        def _():
            dg_ref[...] = dg_part
            db_ref[...] = db_part

        @pl.when(pl.program_id(0) > 0)
        def _():
            dg_ref[...] += dg_part
            db_ref[...] += db_part

    row = _bs((tm, c), lambda i: (i, 0))
    vec = _bs((1, c), lambda i: (0, 0))
    return pl.pallas_call(
        body, out_shape=(jax.ShapeDtypeStruct((t_dim, c), F32), jax.ShapeDtypeStruct((1, c), F32),
                         jax.ShapeDtypeStruct((1, c), F32)),
        grid=(t_dim // tm,), in_specs=[row, vec, vec, row], out_specs=(row, vec, vec), name=name,
        compiler_params=_params("arbitrary"))(y1, g, b, dy3)


def _pool_fwd(proj, col0, n_groups, n_batch, seq, name):
    ch = min(256, seq)

    def body(u_ref, o_ref, pad):
        w = lax.shift_left(jnp.int32(2), pl.program_id(1))
        pad[0:POOL_HALO, :] = jnp.zeros((POOL_HALO, LANE), F32)
        pad[POOL_HALO:POOL_HALO + seq, :] = u_ref[...].astype(F32)
        for c0 in range(0, seq, ch):
            acc = jnp.zeros((ch, LANE), F32)
            for j in range(POOL_WINDOW_MAX):
                acc = acc + jnp.where(j < w, 1.0, 0.0).astype(F32) * pad[pl.ds(c0 + POOL_HALO - j, ch), :]
            t = c0 + lax.broadcasted_iota(jnp.int32, (ch, LANE), 0)
            cnt = jnp.minimum(t + 1, w).astype(F32)
            o_ref[c0:c0 + ch, :] = (acc / cnt - pad[POOL_HALO + c0:POOL_HALO + c0 + ch, :]).astype(BF16)

    return pl.pallas_call(
        body, out_shape=jax.ShapeDtypeStruct((n_batch * seq, n_groups * LANE), BF16), grid=(n_batch, n_groups),
        in_specs=[_bs((seq, LANE), lambda b, g: (b, col0 + g))], out_specs=_bs((seq, LANE), lambda b, g: (b, g)),
        scratch_shapes=[pltpu.VMEM((seq + POOL_HALO, LANE), F32)], name=name,
        compiler_params=_params("parallel", "parallel"))(proj)


def _pool_bwd(dzp, n_groups, n_batch, seq, name):
    ch = min(256, seq)

    def body(d_ref, o_ref, pad):
        w = lax.shift_left(jnp.int32(2), pl.program_id(1))
        for c0 in range(0, seq, ch):
            t = c0 + lax.broadcasted_iota(jnp.int32, (ch, LANE), 0)
            cnt = jnp.minimum(t + 1, w).astype(F32)
            pad[c0:c0 + ch, :] = d_ref[c0:c0 + ch, :].astype(F32) / cnt
        pad[seq:seq + POOL_HALO, :] = jnp.zeros((POOL_HALO, LANE), F32)
        for c0 in range(0, seq, ch):
            acc = jnp.zeros((ch, LANE), F32)
            for j in range(POOL_WINDOW_MAX):
                acc = acc + jnp.where(j < w, 1.0, 0.0).astype(F32) * pad[pl.ds(c0 + j, ch), :]
            o_ref[c0:c0 + ch, :] = (acc - d_ref[c0:c0 + ch, :].astype(F32)).astype(BF16)

    tok = _bs((seq, LANE), lambda b, g: (b, g))
    return pl.pallas_call(
        body, out_shape=jax.ShapeDtypeStruct((n_batch * seq, n_groups * LANE), BF16), grid=(n_batch, n_groups),
        in_specs=[tok], out_specs=tok, scratch_shapes=[pltpu.VMEM((seq + POOL_HALO, LANE), F32)], name=name,
        compiler_params=_params("parallel", "parallel"))(dzp)


def _merge_fwd(proj, col0, yc, yp, scale, name):
    t_dim, d = yc.shape
    half = d // 2
    tm = _tile(t_dim, 512)
    c0 = col0 // half

    def body(gc_ref, gp_ref, yc_ref, yp_ref, s_ref, o_ref):
        f32 = lambda r: r[...].astype(F32)
        o_ref[...] = (_sig(f32(gc_ref)) * f32(yc_ref) + _sig(f32(gp_ref)) * (f32(yp_ref) * s_ref[...])).astype(BF16)

    blk = _bs((tm, half), lambda i, j: (i, j))
    return pl.pallas_call(
        body, out_shape=jax.ShapeDtypeStruct((t_dim, d), BF16), grid=(t_dim // tm, 2),
        in_specs=[_bs((tm, half), lambda i, j: (i, c0 + j)), _bs((tm, half), lambda i, j: (i, c0 + 2 + j)),
                  blk, blk, _bs((1, half), lambda i, j: (0, j))],
        out_specs=blk, name=name, compiler_params=_params("parallel", "parallel"))(proj, proj, yc, yp, scale)


def _merge_bwd(proj, col0, yc, yp, scale, dm, name):
    t_dim, d = yc.shape
    half = d // 2
    tm = _tile(t_dim, 512)
    c0 = col0 // half

    def body(gc_ref, gp_ref, yc_ref, yp_ref, s_ref, dm_ref, dgc_ref, dgp_ref, dyc_ref, dyp_ref, ds_ref):
        dmv = dm_ref[...].astype(F32)
        sgc = _sig(gc_ref[...].astype(F32))
        sgp = _sig(gp_ref[...].astype(F32))
        sv = s_ref[...]
        ypre = yp_ref[...].astype(F32)
        dgc_ref[...] = (dmv * yc_ref[...].astype(F32) * sgc * (1.0 - sgc)).astype(BF16)
        dgp_ref[...] = (dmv * (ypre * sv) * sgp * (1.0 - sgp)).astype(BF16)
        dyc_ref[...] = (dmv * sgc).astype(BF16)
        dyp = dmv * sgp
        dyp_ref[...] = (dyp * sv).astype(BF16)
        part = jnp.sum(dyp * ypre, axis=0, keepdims=True)

        @pl.when(pl.program_id(1) == 0)
        def _():
            ds_ref[...] = part

        @pl.when(pl.program_id(1) > 0)
        def _():
            ds_ref[...] += part

    blk = _bs((tm, half), lambda j, i: (i, j))
    big = jax.ShapeDtypeStruct((t_dim, d), BF16)
    return pl.pallas_call(
        body, out_shape=(big, big, big, big, jax.ShapeDtypeStruct((1, d), F32)), grid=(2, t_dim // tm),
        in_specs=[_bs((tm, half), lambda j, i: (i, c0 + j)), _bs((tm, half), lambda j, i: (i, c0 + 2 + j)),
                  blk, blk, _bs((1, half), lambda j, i: (0, j)), blk],
        out_specs=(blk, blk, blk, blk, _bs((1, half), lambda j, i: (0, j))), name=name,
        compiler_params=_params("parallel", "arbitrary"))(proj, proj, yc, yp, scale, dm)


def _attn_fwd(q, kv, n_batch, seq, m_len, name):
    d = q.shape[1]
    hd = d // XA_HEADS
    tq = _tile(seq, 1024)
    nq = seq // tq
    scale = hd ** -0.5

    def body(q_ref, k_ref, v_ref, o_ref):
        sc = lax.dot_general(q_ref[...].astype(BF16), k_ref[...].astype(BF16), (((1,), (1,)), ((), ())),
                             preferred_element_type=F32) * scale
        p = jnp.exp(sc - jnp.max(sc, axis=-1, keepdims=True))
        pr = p / jnp.sum(p, axis=-1, keepdims=True)
        o_ref[...] = jnp.dot(pr.astype(BF16), v_ref[...].astype(BF16), preferred_element_type=F32).astype(BF16)

    return pl.pallas_call(
        body, out_shape=jax.ShapeDtypeStruct((n_batch * seq, d), BF16), grid=(n_batch, XA_HEADS, nq),
        in_specs=[_bs((tq, hd), lambda b, h, i: (b * nq + i, h)), _bs((m_len, hd), lambda b, h, i: (b, h)),
                  _bs((m_len, hd), lambda b, h, i: (b, XA_HEADS + h))],
        out_specs=_bs((tq, hd), lambda b, h, i: (b * nq + i, h)), name=name,
        compiler_params=_params("parallel", "parallel", "parallel"))(q, kv, kv)


def _attn_bwd(q, kv, datt, n_batch, seq, m_len, name):
    d = q.shape[1]
    hd = d // XA_HEADS
    tq = _tile(seq, 1024)
    nq = seq // tq
    scale = hd ** -0.5

    def body(q_ref, k_ref, v_ref, do_ref, dq_ref, dk_ref, dv_ref):
        qb = q_ref[...].astype(BF16)
        kb = k_ref[...].astype(BF16)
        vb = v_ref[...].astype(BF16)
        dob = do_ref[...].astype(BF16)
        sc = lax.dot_general(qb, kb, (((1,), (1,)), ((), ())), preferred_element_type=F32) * scale
        p = jnp.exp(sc - jnp.max(sc, axis=-1, keepdims=True))
        pr = p / jnp.sum(p, axis=-1, keepdims=True)
        dpr = lax.dot_general(dob, vb, (((1,), (1,)), ((), ())), preferred_element_type=F32)
        dsc = pr * (dpr - jnp.sum(dpr * pr, axis=-1, keepdims=True)) * scale
        dsb = dsc.astype(BF16)
        dq_ref[...] = jnp.dot(dsb, kb, preferred_element_type=F32).astype(BF16)
        dv_part = lax.dot_general(pr.astype(BF16), dob, (((0,), (0,)), ((), ())), preferred_element_type=F32)
        dk_part = lax.dot_general(dsb, qb, (((0,), (0,)), ((), ())), preferred_element_type=F32)

        @pl.when(pl.program_id(2) == 0)
        def _():
            dk_ref[...] = dk_part
            dv_ref[...] = dv_part

        @pl.when(pl.program_id(2) > 0)
        def _():
            dk_ref[...] += dk_part
            dv_ref[...] += dv_part

    qs = _bs((tq, hd), lambda b, h, i: (b * nq + i, h))
    ks = _bs((m_len, hd), lambda b, h, i: (b, h))
    return pl.pallas_call(
        body, out_shape=(jax.ShapeDtypeStruct((n_batch * seq, d), BF16), jax.ShapeDtypeStruct((n_batch * m_len, d), F32),
                         jax.ShapeDtypeStruct((n_batch * m_len, d), F32)),
        grid=(n_batch, XA_HEADS, nq),
        in_specs=[qs, ks, _bs((m_len, hd), lambda b, h, i: (b, XA_HEADS + h)), qs],
        out_specs=(qs, ks, ks), name=name,
        compiler_params=_params("parallel", "parallel", "arbitrary"))(q, kv, kv, datt)


def _gelu_parts(g):
    th = jnp.tanh(GELU_C0 * (g + GELU_C1 * g * g * g))
    return th, 0.5 * g * (1.0 + th)


def _ffn_act_fwd(up_g, up_v, dw_w, n_batch, seq, name):
    kk, c2 = dw_w.shape
    f_dim = c2 // 2
    wd = 2 * LANE
    nj = f_dim // wd
    ch = min(128, seq)

    def body(g_ref, v_ref, wg_ref, wv_ref, o_ref, padg, padv):
        for pad, src in ((padg, g_ref), (padv, v_ref)):
            pad[0:FFN_HALO, :] = jnp.zeros((FFN_HALO, wd), F32)
            pad[FFN_HALO:FFN_HALO + seq, :] = src[...].astype(F32)
        for c0 in range(0, seq, ch):
            gate = jnp.zeros((ch, wd), F32)
            val = jnp.zeros((ch, wd), F32)
            for k in range(kk):
                off = c0 + FFN_HALO - (kk - 1) + k
                gate = gate + wg_ref[k:k + 1, :] * padg[pl.ds(off, ch), :]
                val = val + wv_ref[k:k + 1, :] * padv[pl.ds(off, ch), :]
            o_ref[c0:c0 + ch, :] = (_gelu_parts(gate)[1] * val).astype(BF16)

    return pl.pallas_call(
        body, out_shape=jax.ShapeDtypeStruct((n_batch * seq, f_dim), BF16), grid=(n_batch, nj),
        in_specs=[_bs((seq, wd), lambda b, j: (b, j)), _bs((seq, wd), lambda b, j: (b, j)),
                  _bs((kk, wd), lambda b, j: (0, j)), _bs((kk, wd), lambda b, j: (0, nj + j))],
        out_specs=_bs((seq, wd), lambda b, j: (b, j)),
        scratch_shapes=[pltpu.VMEM((seq + FFN_HALO, wd), F32), pltpu.VMEM((seq + FFN_HALO, wd), F32)], name=name,
        compiler_params=_params("parallel", "parallel"))(up_g, up_v, dw_w, dw_w)


def _ffn_act_bwd(up_g, up_v, dw_w, dact, n_batch, seq, name):
    kk, c2 = dw_w.shape
    f_dim = c2 // 2
    wd = 2 * LANE
    nj = f_dim // wd
    ch = min(128, seq)

    def body(g_ref, v_ref, wg_ref, wv_ref, da_ref, dg_ref, dv_ref, dwg_ref, dwv_ref, padg, padv, pbg, pbv):
        for pad, src in ((padg, g_ref), (padv, v_ref)):
            pad[0:FFN_HALO, :] = jnp.zeros((FFN_HALO, wd), F32)
            pad[FFN_HALO:FFN_HALO + seq, :] = src[...].astype(F32)
        for pb in (pbg, pbv):
            pb[seq:seq + FFN_HALO, :] = jnp.zeros((FFN_HALO, wd), F32)

        @pl.when(pl.program_id(1) == 0)
        def _():
            dwg_ref[...] = jnp.zeros((kk, wd), F32)
            dwv_ref[...] = jnp.zeros((kk, wd), F32)

        for c0 in range(0, seq, ch):
            gate = jnp.zeros((ch, wd), F32)
            val = jnp.zeros((ch, wd), F32)
            for k in range(kk):
                off = c0 + FFN_HALO - (kk - 1) + k
                gate = gate + wg_ref[k:k + 1, :] * padg[pl.ds(off, ch), :]
                val = val + wv_ref[k:k + 1, :] * padv[pl.ds(off, ch), :]
            sq = gate * gate
            th = jnp.tanh(GELU_C0 * gate * (1.0 + GELU_C1 * sq))
            half = 0.5 * th + 0.5
            dgelu = half * (1.0 + gate * (GELU_C0 + 3.0 * GELU_C0 * GELU_C1 * sq) * (1.0 - th))
            dav = da_ref[c0:c0 + ch, :].astype(F32)
            pbg[c0:c0 + ch, :] = dav * val * dgelu
            pbv[c0:c0 + ch, :] = dav * (gate * half)
        for pb, pad, w_ref, d_ref, dw_ref in ((pbg, padg, wg_ref, dg_ref, dwg_ref), (pbv, padv, wv_ref, dv_ref, dwv_ref)):
            for c0 in range(0, seq, ch):
                acc = jnp.zeros((ch, wd), F32)
                for k in range(kk):
                    acc = acc + w_ref[k:k + 1, :] * pb[pl.ds(c0 + (kk - 1) - k, ch), :]
                d_ref[c0:c0 + ch, :] = acc.astype(BF16)
            for k in range(kk):
                s = jnp.zeros((1, wd), F32)
                for c0 in range(0, seq, ch):
                    s = s + jnp.sum(pb[c0:c0 + ch, :] * pad[pl.ds(c0 + FFN_HALO - (kk - 1) + k, ch), :],
                                    axis=0, keepdims=True)
                dw_ref[k:k + 1, :] += s

    t_dim = n_batch * seq
    tok = _bs((seq, wd), lambda j, b: (b, j))
    wblk = _bs((kk, wd), lambda j, b: (0, j))
    pad_shape = pltpu.VMEM((seq + FFN_HALO, wd), F32)
    return pl.pallas_call(
        body, out_shape=(jax.ShapeDtypeStruct((t_dim, f_dim), BF16), jax.ShapeDtypeStruct((t_dim, f_dim), BF16),
                         jax.ShapeDtypeStruct((kk, f_dim), F32), jax.ShapeDtypeStruct((kk, f_dim), F32)),
        grid=(nj, n_batch),
        in_specs=[tok, tok, wblk, _bs((kk, wd), lambda j, b: (0, nj + j)), tok],
        out_specs=(tok, tok, wblk, wblk), scratch_shapes=[pad_shape, pad_shape, pad_shape, pad_shape], name=name,
        compiler_params=_params("parallel", "arbitrary"))(up_g, up_v, dw_w, dw_w, dact)


def _sum_rows(parts, out_dtype, name):
    r_dim, c_dim = parts[0].shape
    tr = _tile(r_dim, 1200, SUBLANE)
    n = len(parts)

    def body(*refs):
        acc = refs[0][...].astype(F32)
        for r in refs[1:n]:
            acc = acc + r[...].astype(F32)
        refs[n][...] = acc.astype(out_dtype)

    blk = _bs((tr, c_dim), lambda i: (i, 0))
    return pl.pallas_call(
        body, out_shape=jax.ShapeDtypeStruct((r_dim, c_dim), out_dtype), grid=(r_dim // tr,),
        in_specs=[blk] * n, out_specs=blk, name=name, compiler_params=_params("parallel"))(*parts)


def _adamw(w, g, m, v, name):
    shape = w.shape
    c_dim = shape[-1]
    r_dim = w.size // c_dim
    two_d = lambda t: t.reshape(r_dim, c_dim)
    tr = _tile(r_dim, max(SUBLANE, (256 * 1024) // max(c_dim, LANE) // SUBLANE * SUBLANE), SUBLANE)
    c1 = 1.0 - ADAM_B1 ** ADAM_STEP
    c2 = 1.0 - ADAM_B2 ** ADAM_STEP

    def body(w_ref, g_ref, m_ref, v_ref, d_ref, mo_ref, vo_ref):
        gv = g_ref[...]
        mn = ADAM_B1 * m_ref[...] + (1.0 - ADAM_B1) * gv
        vn = ADAM_B2 * v_ref[...] + (1.0 - ADAM_B2) * (gv * gv)
        mo_ref[...] = mn
        vo_ref[...] = vn
        d_ref[...] = -ADAM_LR * ((mn / c1) / (jnp.sqrt(vn / c2) + ADAM_EPS) + ADAM_WD * w_ref[...])

    blk = _bs((tr, c_dim), lambda i: (i, 0))
    out = jax.ShapeDtypeStruct((r_dim, c_dim), F32)
    d, mo, vo = pl.pallas_call(
        body, out_shape=(out, out, out), grid=(r_dim // tr,), in_specs=[blk] * 4, out_specs=(blk, blk, blk),
        name=name, compiler_params=_params("parallel"))(two_d(w), two_d(g), two_d(m), two_d(v))
    return d.reshape(shape), mo.reshape(shape), vo.reshape(shape)


HBM_SPEC = pl.BlockSpec(memory_space=pltpu.HBM)


def _position():
    return lax.axis_index("x"), lax.axis_index("y"), lax.axis_index("c")


def _all_gather(shard, name):
    def body(x_ref, out_ref, send_sems, recv_sems, local_sem):
        x, y, c = _position()
        me, sibling = (x, y, c), (x, y, 1 - c)
        chips = [(1 - x, y), (x, 1 - y), (1 - x, 1 - y)]

        def rows(px, py, pc):
            return out_ref.at[4 * px + 2 * py + pc]

        def copy(k, block, to, src=None):
            return pltpu.make_async_remote_copy(
                src_ref=rows(*block) if src is None else src, dst_ref=rows(*block),
                send_sem=send_sems.at[k], recv_sem=recv_sems.at[k], device_id=to, device_id_type=MESH)

        mine = pltpu.make_async_copy(x_ref, rows(*me), local_sem)
        mine.start()
        first = [copy(0, me, sibling, src=x_ref)]
        first += [copy(1 + j, me, (*chip, c), src=x_ref) for j, chip in enumerate(chips)]
        for cp in first:
            cp.start()
        passed = [copy(4 + j, (*chip, c), sibling) for j, chip in enumerate(chips)]
        for j, chip in enumerate(chips):
            copy(1 + j, (*chip, c), me).wait_recv()
            passed[j].start()
        copy(0, sibling, me).wait_recv()
        for j, chip in enumerate(chips):
            copy(4 + j, (*chip, 1 - c), me).wait_recv()
        for cp in first + passed:
            cp.wait_send()
        mine.wait()

    return pl.pallas_call(
        body, out_shape=jax.ShapeDtypeStruct((N_DEV,) + shard.shape, shard.dtype),
        in_specs=[HBM_SPEC], out_specs=HBM_SPEC,
        scratch_shapes=[pltpu.SemaphoreType.DMA((7,)), pltpu.SemaphoreType.DMA((7,)), pltpu.SemaphoreType.DMA(())],
        name=name)(shard)


CHIP_RELATIONS = ((0, 0), (1, 0), (0, 1), (1, 1))


def _rs_pair_exchange(g, name):
    _, r_dim, c_dim = g.shape
    n = len(CHIP_RELATIONS)

    def body(g_ref, recv_ref, send_sems, recv_sems):
        x, y, c = _position()
        sibling = (x, y, 1 - c)
        copies = []
        for k, (rx, ry) in enumerate(CHIP_RELATIONS):
            px = x + rx - 2 * x * rx
            py = y + ry - 2 * y * ry
            copies.append(pltpu.make_async_remote_copy(
                src_ref=g_ref.at[4 * px + 2 * py + 1 - c], dst_ref=recv_ref.at[k], send_sem=send_sems.at[k],
                recv_sem=recv_sems.at[k], device_id=sibling, device_id_type=MESH))
        for cp in copies:
            cp.start()
        for cp in copies:
            cp.wait()

    return pl.pallas_call(
        body, out_shape=jax.ShapeDtypeStruct((n, r_dim, c_dim), g.dtype), in_specs=[HBM_SPEC], out_specs=HBM_SPEC,
        scratch_shapes=[pltpu.SemaphoreType.DMA((n,)), pltpu.SemaphoreType.DMA((n,))], name=name)(g)


def _rs_pair_sum(g, recv, name):
    _, r_dim, c_dim = g.shape
    n = len(CHIP_RELATIONS)
    tr = _tile(r_dim, 1200, SUBLANE)
    x, y, c = _position()
    own = jnp.stack([4 * (x + rx - 2 * x * rx) + 2 * (y + ry - 2 * y * ry) + c for rx, ry in CHIP_RELATIONS])

    def body(own_ref, g_ref, r_ref, o_ref):
        o_ref[...] = (g_ref[...].astype(F32) + r_ref[...].astype(F32)).astype(o_ref.dtype)

    blk = _bs((None, tr, c_dim), lambda k, i, own_ref: (k, i, 0))
    return pl.pallas_call(
        body, out_shape=jax.ShapeDtypeStruct((n, r_dim, c_dim), g.dtype),
        grid_spec=pltpu.PrefetchScalarGridSpec(
            num_scalar_prefetch=1, grid=(n, r_dim // tr),
            in_specs=[_bs((None, tr, c_dim), lambda k, i, own_ref: (own_ref[k], i, 0)), blk], out_specs=blk),
        name=name, compiler_params=_params("parallel", "parallel"))(own.astype(jnp.int32), g, recv)


SEM_SPEC = pl.BlockSpec(memory_space=pltpu.SEMAPHORE)
DATAFLOW = pltpu.SideEffectType.DATAFLOW_SIDE_EFFECTING
CHIP_FLIPS = CHIP_RELATIONS[1:]
TOKEN = jax.ShapeDtypeStruct((SUBLANE, LANE), F32)


def _flip(v, r):
    return v + r - 2 * v * r


def _chip_copies(src_ref, src_of, dst_ref, dst_of, send_sems, recv_sems):
    x, y, c = _position()
    me = 4 * x + 2 * y + c
    out = []
    for k, (rx, ry) in enumerate(CHIP_FLIPS):
        px, py = _flip(x, rx), _flip(y, ry)
        peer = 4 * px + 2 * py + c
        out.append(pltpu.make_async_remote_copy(
            src_ref=src_ref.at[src_of(k, me, peer)], dst_ref=dst_ref.at[dst_of(k, me, peer)],
            send_sem=send_sems.at[k], recv_sem=recv_sems.at[k], device_id=(px, py, c), device_id_type=MESH))
    return out


def _device_block(ref, spec, d):
    rows, axis = spec
    return ref.at[pl.ds(d * rows, rows)] if axis == 0 else ref.at[:, pl.ds(d * rows, rows)]


def _ag_chips_start(lands, specs, after, name):
    n = len(lands)
    nf = len(CHIP_FLIPS)

    def body(*refs):
        send_sems, recv_sems, token = refs[n + 1], refs[n + 2], refs[-1]
        x, y, c = _position()
        me = 4 * x + 2 * y + c
        for i, spec in enumerate(specs):
            blk = _device_block(refs[i], spec, me)
            for k, (rx, ry) in enumerate(CHIP_FLIPS):
                pltpu.make_async_remote_copy(
                    src_ref=blk, dst_ref=blk, send_sem=send_sems.at[nf * i + k], recv_sem=recv_sems.at[nf * i + k],
                    device_id=(_flip(x, rx), _flip(y, ry), c), device_id_type=MESH).start()
        token[...] = jnp.zeros(TOKEN.shape, TOKEN.dtype)

    sems = pltpu.SemaphoreType.DMA((nf * n,))
    return pl.pallas_call(
        body, name=name, out_shape=(sems, sems, *[pltpu.HBM(t.shape, t.dtype) for t in lands], TOKEN),
        in_specs=(HBM_SPEC,) * n + (ANY_SPEC,),
        out_specs=(SEM_SPEC, SEM_SPEC) + (HBM_SPEC,) * n + (pl.BlockSpec(memory_space=pltpu.VMEM),),
        input_output_aliases={i: 2 + i for i in range(n)}, compiler_params=pltpu.CompilerParams(has_side_effects=DATAFLOW),
    )(*[pltpu.with_memory_space_constraint(t, pltpu.HBM) for t in lands], after)


def _ag_chips_wait(send_sems, recv_sems, lands, specs, after, name):
    n = len(lands)
    nf = len(CHIP_FLIPS)

    def body(*refs):
        send_sems, recv_sems = refs[n], refs[n + 1]
        x, y, c = _position()
        me = 4 * x + 2 * y + c
        for i, spec in enumerate(specs):
            for k, (rx, ry) in enumerate(CHIP_FLIPS):
                px, py = _flip(x, rx), _flip(y, ry)
                cp = pltpu.make_async_remote_copy(
                    src_ref=_device_block(refs[i], spec, me), dst_ref=_device_block(refs[i], spec, 4 * px + 2 * py + c),
                    send_sem=send_sems.at[nf * i + k], recv_sem=recv_sems.at[nf * i + k],
                    device_id=(px, py, c), device_id_type=MESH)
                cp.wait_send()
                cp.wait_recv()

    return pl.pallas_call(
        body, name=name, out_shape=tuple(pltpu.HBM(t.shape, t.dtype) for t in lands),
        in_specs=(HBM_SPEC,) * n + (SEM_SPEC, SEM_SPEC, ANY_SPEC), out_specs=(HBM_SPEC,) * n,
        input_output_aliases={i: i for i in range(n)}, compiler_params=pltpu.CompilerParams(has_side_effects=DATAFLOW),
    )(*lands, send_sems, recv_sems, after)


def _ag_pair_forward(lands, specs, name):
    n = len(lands)
    nr = len(CHIP_RELATIONS)

    def body(*refs):
        outs, send_sems, recv_sems = refs[n:2 * n], refs[2 * n], refs[2 * n + 1]
        x, y, c = _position()
        copies = []
        for i, spec in enumerate(specs):
            for k, (rx, ry) in enumerate(CHIP_RELATIONS):
                chip = 4 * _flip(x, rx) + 2 * _flip(y, ry)
                held = _device_block(outs[i], spec, chip + c)
                sems = dict(send_sem=send_sems.at[nr * i + k], recv_sem=recv_sems.at[nr * i + k],
                            device_id=(x, y, 1 - c), device_id_type=MESH)
                mine = pltpu.make_async_remote_copy(src_ref=held, dst_ref=held, **sems)
                theirs = pltpu.make_async_remote_copy(src_ref=held, dst_ref=_device_block(outs[i], spec, chip + 1 - c), **sems)
                copies.append((mine, theirs))
        for mine, _ in copies:
            mine.start()
        for mine, theirs in copies:
            mine.wait_send()
            theirs.wait_recv()

    sems = pltpu.SemaphoreType.DMA((nr * n,))
    return pl.pallas_call(
        body, out_shape=tuple(jax.ShapeDtypeStruct(t.shape, t.dtype) for t in lands), in_specs=[HBM_SPEC] * n,
        out_specs=(HBM_SPEC,) * n, input_output_aliases={i: i for i in range(n)}, scratch_shapes=[sems, sems], name=name)(*lands)


def _rs_chips_start(pair, name):
    _, r_dim, c_dim = pair.shape
    n = len(CHIP_FLIPS)

    def body(pair_ref, far_ref, send_sems, recv_sems, pair_thru, far_thru, token):
        for cp in _chip_copies(pair_ref, lambda k, me, peer: k + 1, far_ref, lambda k, me, peer: k, send_sems, recv_sems):
            cp.start()
        token[...] = jnp.zeros(TOKEN.shape, TOKEN.dtype)

    far = lax.empty((n, r_dim, c_dim), pair.dtype)
    return pl.pallas_call(
        body, name=name,
        out_shape=(pltpu.SemaphoreType.DMA((n,)), pltpu.SemaphoreType.DMA((n,)), pltpu.HBM(pair.shape, pair.dtype),
                   pltpu.HBM(far.shape, far.dtype), TOKEN),
        in_specs=(HBM_SPEC, HBM_SPEC),
        out_specs=(SEM_SPEC, SEM_SPEC, HBM_SPEC, HBM_SPEC, pl.BlockSpec(memory_space=pltpu.VMEM)),
        input_output_aliases={0: 2, 1: 3}, compiler_params=pltpu.CompilerParams(has_side_effects=DATAFLOW),
    )(pltpu.with_memory_space_constraint(pair, pltpu.HBM), pltpu.with_memory_space_constraint(far, pltpu.HBM))


def _rs_chips_wait(send_sems, recv_sems, pair, far, after, name):
    def body(pair_ref, far_ref, send_sems, recv_sems, after_ref, pair_out, far_out):
        for cp in _chip_copies(pair_ref, lambda k, me, peer: k + 1, far_ref, lambda k, me, peer: k, send_sems, recv_sems):
            cp.wait_send()
            cp.wait_recv()

    return pl.pallas_call(
        body, name=name, out_shape=(pltpu.HBM(pair.shape, pair.dtype), pltpu.HBM(far.shape, far.dtype)),
        in_specs=(HBM_SPEC, HBM_SPEC, SEM_SPEC, SEM_SPEC, ANY_SPEC),
        out_specs=(HBM_SPEC, HBM_SPEC), input_output_aliases={0: 0, 1: 1},
        compiler_params=pltpu.CompilerParams(has_side_effects=DATAFLOW),
    )(pair, far, send_sems, recv_sems, after)


def _rs_final_sum(pair, far, name):
    _, r_dim, c_dim = pair.shape
    tr = _tile(r_dim, 1200, SUBLANE)

    def body(p_ref, f0_ref, f1_ref, f2_ref, o_ref):
        o_ref[...] = ((p_ref[...].astype(F32) + f0_ref[...].astype(F32)) + f1_ref[...].astype(F32)) + f2_ref[...].astype(F32)

    def slot(k):
        return _bs((None, tr, c_dim), lambda i: (k, i, 0))

    return pl.pallas_call(
        body, out_shape=jax.ShapeDtypeStruct((r_dim, c_dim), F32), grid=(r_dim // tr,),
        in_specs=[slot(0), slot(0), slot(1), slot(2)], out_specs=_bs((tr, c_dim), lambda i: (i, 0)), name=name,
        compiler_params=_params("parallel"))(pair, far, far, far)


def _reduce_scatter_begin(g, name):
    recv = _rs_pair_exchange(g, name + "_pair")
    pair = _rs_pair_sum(g, recv, name + "_pairsum")
    return _rs_chips_start(pair, name + "_chips_start")


def _reduce_scatter_end(state, after, name):
    send_sems, recv_sems, pair, far, _ = state
    pair, far = _rs_chips_wait(send_sems, recv_sems, pair, far, after, name + "_chips_wait")
    return _rs_final_sum(pair, far, name + "_sum")


MATRICES = (("w_in", True), ("w_out", False), ("w_q", False), ("w_kv", True), ("w_o", False), ("w_up", True),
            ("w_down", False), ("w_conv_out", True), ("w_pool_grp", True))
MIX_NAMES = ("w_in", "w_conv_out", "w_pool_grp", "w_out")
REST_NAMES = ("w_q", "w_kv", "w_o", "w_up", "w_down")


def _parts(layer):
    return (("mix", MIX_NAMES), ("rest", REST_NAMES)) if layer == 0 else (("all", MIX_NAMES + REST_NAMES),)


def _to_rows(name, transposed, w, d_model):
    if name == "w_pool_grp":
        w = jnp.swapaxes(w, 1, 2)
    elif transposed:
        w = w.T
    return w.reshape(-1, d_model)


def _from_rows(name, transposed, rows, shard_shape):
    if name == "w_pool_grp":
        g, i, o = shard_shape
        return jnp.swapaxes(rows.reshape(g, o, i), 1, 2)
    if transposed:
        return rows.reshape(shard_shape[1], shard_shape[0]).T
    return rows.reshape(shard_shape)


def _scatter_blocks(name, full, shard_shape, d_model, n_dev=N_DEV):
    if name == "w_pool_grp":
        g, i, o = shard_shape
        return jnp.swapaxes(full.reshape(g, n_dev, o, i), 0, 1).reshape(n_dev, -1, d_model)
    return full.reshape(n_dev, -1, d_model)


def kernel(x, mem, mix_norm_g, w_in, conv_dw_w, conv_dw_b, conv_ln_g, conv_ln_b, w_conv_out, w_pool_grp, pool_scale, w_out, xattn_norm_g, mem_norm_g, w_q, w_kv, w_o, ffn_norm_g, w_up, ffn_dw_w, w_down, final_norm_g, loss_target, m_mix_norm_g, m_w_in, m_conv_dw_w, m_conv_dw_b, m_conv_ln_g, m_conv_ln_b, m_w_conv_out, m_w_pool_grp, m_pool_scale, m_w_out, m_xattn_norm_g, m_mem_norm_g, m_w_q, m_w_kv, m_w_o, m_ffn_norm_g, m_w_up, m_ffn_dw_w, m_w_down, m_final_norm_g, v_mix_norm_g, v_w_in, v_conv_dw_w, v_conv_dw_b, v_conv_ln_g, v_conv_ln_b, v_w_conv_out, v_w_pool_grp, v_pool_scale, v_w_out, v_xattn_norm_g, v_mem_norm_g, v_w_q, v_w_kv, v_w_o, v_ffn_norm_g, v_w_up, v_ffn_dw_w, v_w_down, v_final_norm_g):
    p = dict(locals())
    weight_names = ["mix_norm_g", "w_in", "conv_dw_w", "conv_dw_b", "conv_ln_g", "conv_ln_b", "w_conv_out",
                    "w_pool_grp", "pool_scale", "w_out", "xattn_norm_g", "mem_norm_g", "w_q", "w_kv", "w_o",
                    "ffn_norm_g", "w_up", "ffn_dw_w", "w_down", "final_norm_g"]
    n_batch, seq, d_model = x.shape
    m_len = mem.shape[1]
    depth = w_in.shape[0]
    assert depth == 2, "the exchange schedule below is written for two layers"
    t_dim = n_batch * seq
    c_conv = conv_dw_b.shape[1]
    n_groups = w_pool_grp.shape[1]
    assert w_pool_grp.shape[2] == LANE and c_conv % LANE == 0 and n_groups * LANE == c_conv
    gate_col0 = 2 * c_conv + n_groups * LANE
    pool_col0 = (2 * c_conv) // LANE

    dev = 4 * lax.axis_index("x") + 2 * lax.axis_index("y") + lax.axis_index("c")
    filt = jnp.concatenate([conv_dw_w.reshape(-1), ffn_dw_w.reshape(-1)])
    filt_rows = lax.bitcast_convert_type(filt, BF16).reshape(-1, d_model)
    transposed = dict(MATRICES)
    layout = {part: [(name, transposed[name], _to_rows(name, transposed[name], p[name][0], d_model).shape[0])
                     for name in names] for l in range(depth) for part, names in _parts(l)}
    part_of = {(l, name): part for l in range(depth) for part, names in _parts(l) for name in names}

    def landing(name, shard):
        if name == "w_pool_grp":
            block, axis = jnp.swapaxes(shard, 1, 2), 1
        elif name == "filt":
            block, axis = shard, 0
        else:
            block, axis = (shard.T if transposed[name] else shard), 0
        block = block.astype(BF16)
        rows = block.shape[axis]
        shape = block.shape[:axis] + (N_DEV * rows,) + block.shape[axis + 1:]
        start = (0,) * axis + (dev * rows,) + (0,) * (block.ndim - axis - 1)
        return lax.dynamic_update_slice(lax.empty(shape, BF16), block, start), (rows, axis)

    ag_state = {}
    after = filt_rows
    for l in range(depth):
        for part, names in _parts(l):
            items = [(name, p[name][l]) for name in names]
            if (l, part) == (0, part_of[(0, "w_in")]):
                items.append(("filt", filt_rows))
            lands, specs = zip(*[landing(name, shard) for name, shard in items])
            out = _ag_chips_start(lands, specs, after, f"ag{l}{part}_chips_start")
            ag_state[(l, part)] = ([name for name, _ in items], specs, out)
            after = out[-1]
    all_started = after

    full = [dict() for _ in range(depth)]

    def ensure(l, name, after):
        if name in full[l]:
            return
        part = part_of[(l, name)]
        names, specs, out = ag_state[(l, part)]
        lands = _ag_chips_wait(out[0], out[1], out[2:-1], specs, after, f"ag{l}{part}_chips_wait")
        lands = _ag_pair_forward(lands, specs, f"ag{l}{part}_pair_forward")
        full[l].update(zip(names, lands))

    vec = lambda a: a.reshape(1, -1)
    x2d = x.reshape(t_dim, d_model)
    mem2d = mem.reshape(n_batch * m_len, d_model)
    mem_n = _rmsnorm_fwd(mem2d, vec(mem_norm_g), "mem_norm", after=all_started)
    h_first = _rmsnorm_fwd(x2d, vec(mix_norm_g[0]), "mix_norm_l0", after=mem_n)
    ensure(0, "w_in", h_first)
    filt_all = lax.bitcast_convert_type(full[0]["filt"].reshape(N_DEV, -1, 2), F32)
    n_cw = conv_dw_w.size
    kc, cs = conv_dw_w.shape[1:]
    kf, fs = ffn_dw_w.shape[1:]
    conv_w_full = jnp.moveaxis(filt_all[:, :n_cw].reshape(N_DEV, depth, kc, cs), 0, 2).reshape(depth, kc, N_DEV * cs)
    ffn_w_full = jnp.moveaxis(filt_all[:, n_cw:].reshape(N_DEV, depth, kf, fs), 0, 2).reshape(depth, kf, N_DEV * fs)

    saved = []
    xc = x2d
    for l in range(depth):
        ensure(l, "w_in", xc)
        wl = full[l]
        s = {"x0": xc}
        s["h"] = h_first if l == 0 else _rmsnorm_fwd(xc, vec(mix_norm_g[l]), f"mix_norm_l{l}")
        s["proj"] = _matmul(s["h"], wl["w_in"], "nt", f"in_proj_l{l}", out_dtype=BF16)
        s["y1"] = _glu_conv_fwd(s["proj"], conv_w_full[l], vec(conv_dw_b[l]), n_batch, seq, f"glu_conv_l{l}")
        s["y3"] = _ln_silu_fwd(s["y1"], vec(conv_ln_g[l]), vec(conv_ln_b[l]), f"ln_silu_l{l}")
        s["yc"] = _matmul(s["y3"], wl["w_conv_out"], "nt", f"conv_out_l{l}", out_dtype=BF16)
        s["zp"] = _pool_fwd(s["proj"], pool_col0, n_groups, n_batch, seq, f"pool_l{l}")
        s["yp"] = _grouped(s["zp"], wl["w_pool_grp"], "nt", f"pool_proj_l{l}", out_dtype=BF16)
        s["merged"] = _merge_fwd(s["proj"], gate_col0, s["yc"], s["yp"], vec(pool_scale[l]), f"merge_l{l}")
        s["x1"] = _matmul(s["merged"], wl["w_out"], "nn", f"mix_out_l{l}", res=xc)
        ensure(l, "w_q", s["x1"])
        half_up = wl["w_up"].shape[0] // 2
        up_gate, up_val = (0, half_up), (half_up, half_up)
        s["hq"] = _rmsnorm_fwd(s["x1"], vec(xattn_norm_g[l]), f"xattn_norm_l{l}")
        s["q"] = _matmul(s["hq"], wl["w_q"], "nn", f"q_proj_l{l}", out_dtype=BF16)
        s["kv"] = _matmul(mem_n, wl["w_kv"], "nt", f"kv_proj_l{l}", out_dtype=BF16)
        s["att"] = _attn_fwd(s["q"], s["kv"], n_batch, seq, m_len, f"attn_l{l}")
        s["x2"] = _matmul(s["att"], wl["w_o"], "nn", f"attn_out_l{l}", res=s["x1"])
        s["hf"] = _rmsnorm_fwd(s["x2"], vec(ffn_norm_g[l]), f"ffn_norm_l{l}")
        s["up_g"] = _matmul(s["hf"], wl["w_up"], "nt", f"up_proj_gate_l{l}", out_dtype=BF16, b_window=up_gate)
        s["up_v"] = _matmul(s["hf"], wl["w_up"], "nt", f"up_proj_val_l{l}", out_dtype=BF16, b_window=up_val)
        s["act"] = _ffn_act_fwd(s["up_g"], s["up_v"], ffn_w_full[l], n_batch, seq, f"ffn_act_l{l}")
        xc = _matmul(s["act"], wl["w_down"], "nn", f"down_proj_l{l}", res=s["x2"])
        saved.append(s)

    dx, dxb, dg_final, loss_part = _loss_head(xc, vec(final_norm_g), loss_target.reshape(t_dim, d_model), "loss_head")

    small = {"final_norm_g": dg_final.reshape(-1)}
    big = [dict() for _ in range(depth)]
    rs_state = {}
    rs_after = loss_part

    def rs_begin(l, part):
        pack = lax.empty((N_DEV, sum(nrows for _, _, nrows in layout[part]), d_model), BF16)
        row0 = 0
        for name, _, nrows in layout[part]:
            pieces = big[l][name] if isinstance(big[l][name], tuple) else (big[l][name],)
            d0 = 0
            for piece in pieces:
                blocks = _scatter_blocks(name, piece, p[name].shape[1:], d_model, N_DEV // len(pieces)).astype(BF16)
                pack = lax.dynamic_update_slice(pack, blocks, (d0, row0, 0))
                d0 += blocks.shape[0]
            row0 += nrows
        rs_state[(l, part)] = _reduce_scatter_begin(pack, f"rs{l}{part}")
        return rs_state[(l, part)][4]

    dmem_n = None
    for l in reversed(range(depth)):
        wl, s = full[l], saved[l]
        sm = {}
        dact = _matmul(dxb, wl["w_down"], "nt", f"d_act_l{l}", out_dtype=BF16, after=rs_after)
        big[l]["w_down"] = _matmul(s["act"], dxb, "tn", f"d_w_down_l{l}", out_dtype=BF16)
        dup_g, dup_v, dwf_g, dwf_v = _ffn_act_bwd(s["up_g"], s["up_v"], ffn_w_full[l], dact, n_batch, seq,
                                                  f"ffn_act_bwd_l{l}")
        sm["ffn_dw_w"] = jnp.concatenate([dwf_g, dwf_v], axis=1)
        dx, dxb, dg = _matmul_rmsnorm_bwd((dup_g, dup_v), wl["w_up"], "nn", s["x2"], vec(ffn_norm_g[l]), dx,
                                          f"d_hf_ffn_norm_bwd_l{l}")
        big[l]["w_up"] = (_matmul(dup_g, s["hf"], "tn", f"d_w_up_gate_l{l}", out_dtype=BF16),
                          _matmul(dup_v, s["hf"], "tn", f"d_w_up_val_l{l}", out_dtype=BF16))
        sm["ffn_norm_g"] = dg
        datt = _matmul(dxb, wl["w_o"], "nt", f"d_att_l{l}", out_dtype=BF16, after=rs_after)
        big[l]["w_o"] = _matmul(s["att"], dxb, "tn", f"d_w_o_l{l}", out_dtype=BF16)
        dq, dk, dv = _attn_bwd(s["q"], s["kv"], datt, n_batch, seq, m_len, f"attn_bwd_l{l}")
        dkv = jnp.concatenate([dk, dv], axis=1)
        big[l]["w_kv"] = _matmul(dkv, mem_n, "tn", f"d_w_kv_l{l}", out_dtype=BF16)
        dmem_n = _matmul(dkv, wl["w_kv"], "nn", f"d_mem_l{l}", res=dmem_n)
        big[l]["w_q"] = _matmul(s["hq"], dq, "tn", f"d_w_q_l{l}", out_dtype=BF16)
        dx, dxb, dg = _matmul_rmsnorm_bwd(dq, wl["w_q"], "nt", s["x1"], vec(xattn_norm_g[l]), dx,
                                          f"d_hq_xattn_norm_bwd_l{l}")
        sm["xattn_norm_g"] = dg
        if part_of[(l, "w_q")] != part_of[(l, "w_in")]:
            rs_after = rs_begin(l, part_of[(l, "w_q")])
        dmerged = _matmul(dxb, wl["w_out"], "nt", f"d_merged_l{l}", out_dtype=BF16, after=rs_after)
        big[l]["w_out"] = _matmul(s["merged"], dxb, "tn", f"d_w_out_l{l}", out_dtype=BF16)
        dgc, dgp, dyc, dyp, dscale = _merge_bwd(s["proj"], gate_col0, s["yc"], s["yp"], vec(pool_scale[l]), dmerged,
                                                f"merge_bwd_l{l}")
        sm["pool_scale"] = dscale
        dzp = _grouped(dyp, wl["w_pool_grp"], "nn", f"d_zp_l{l}", out_dtype=BF16)
        big[l]["w_pool_grp"] = _grouped_tn(dyp, s["zp"], n_groups, f"d_w_pool_l{l}")
        du = _pool_bwd(dzp, n_groups, n_batch, seq, f"pool_bwd_l{l}")
        dy3 = _matmul(dyc, wl["w_conv_out"], "nn", f"d_y3_l{l}", out_dtype=BF16)
        big[l]["w_conv_out"] = _matmul(dyc, s["y3"], "tn", f"d_w_conv_out_l{l}", out_dtype=BF16)
        dy1, dlg, dlb = _ln_silu_bwd(s["y1"], vec(conv_ln_g[l]), vec(conv_ln_b[l]), dy3, f"ln_silu_bwd_l{l}")
        sm["conv_ln_g"], sm["conv_ln_b"] = dlg, dlb
        da, dgl, dcw, dcb = _glu_conv_bwd(s["proj"], conv_w_full[l], dy1, n_batch, seq, f"glu_conv_bwd_l{l}")
        sm["conv_dw_w"], sm["conv_dw_b"] = dcw, dcb
        dproj = jnp.concatenate([da, dgl, du, dgc, dgp], axis=1)
        big[l]["w_in"] = _matmul(dproj, s["h"], "tn", f"d_w_in_l{l}", out_dtype=BF16)
        dx, dxb, dg = _matmul_rmsnorm_bwd(dproj, wl["w_in"], "nn", s["x0"], vec(mix_norm_g[l]), dx,
                                          f"d_h_mix_norm_bwd_l{l}")
        sm["mix_norm_g"] = dg
        for k, val in sm.items():
            small[(l, k)] = val.reshape(-1)
        rs_after = rs_begin(l, part_of[(l, "w_in")])
    _, _, dg_mem = _rmsnorm_bwd(mem2d, vec(mem_norm_g), dmem_n, None, "mem_norm_bwd")
    small["mem_norm_g"] = dg_mem.reshape(-1)
    small["loss"] = loss_part.reshape(-1)

    grads = {}
    per_layer = {name: [None] * depth for name, _ in MATRICES}
    for l in reversed(range(depth)):
        for part, _ in reversed(_parts(l)):
            mat_grads = _reduce_scatter_end(rs_state[(l, part)], rs_after, f"rs{l}{part}")
            row0 = 0
            for name, tr, nrows in layout[part]:
                per_layer[name][l] = _from_rows(name, tr, mat_grads[row0:row0 + nrows], p[name].shape[1:])
                row0 += nrows
    for name, _ in MATRICES:
        grads[name] = jnp.stack(per_layer[name])

    keys = list(small.keys())
    flat = jnp.concatenate([small[k] for k in keys])
    n_small = flat.shape[0]
    rows_small = -(-n_small // (SUBLANE * d_model)) * SUBLANE
    flat = jnp.pad(flat, (0, rows_small * d_model - n_small)).reshape(rows_small, d_model)
    every = _all_gather(flat, "small_all_gather")
    total = _sum_rows([every[i] for i in range(N_DEV)], F32, "small_sum").reshape(-1)
    off = 0
    red = {}
    for k in keys:
        red[k] = total[off:off + small[k].shape[0]]
        off += small[k].shape[0]
    loss = red["loss"][0]
    for name in ("mix_norm_g", "conv_dw_b", "conv_ln_g", "conv_ln_b", "pool_scale", "xattn_norm_g", "ffn_norm_g"):
        grads[name] = jnp.stack([red[(l, name)] for l in range(depth)])
    grads["conv_dw_w"] = jnp.stack([
        lax.dynamic_slice_in_dim(red[(l, "conv_dw_w")].reshape(kc, N_DEV * cs), dev * cs, cs, axis=1)
        for l in range(depth)])
    grads["ffn_dw_w"] = jnp.stack([
        lax.dynamic_slice_in_dim(red[(l, "ffn_dw_w")].reshape(kf, N_DEV * fs), dev * fs, fs, axis=1)
        for l in range(depth)])
    grads["mem_norm_g"] = red["mem_norm_g"]
    grads["final_norm_g"] = red["final_norm_g"]

    deltas, new_m, new_v = {}, {}, {}
    for name in weight_names:
        deltas[name], new_m[name], new_v[name] = _adamw(p[name], grads[name], p["m_" + name], p["v_" + name],
                                                        f"adamw_{name}")
    grad_x = dx.reshape(n_batch, seq, d_model)
    return (loss, grad_x, *[grads[n] for n in weight_names], *[deltas[n] for n in weight_names],
            *[new_m[n] for n in weight_names], *[new_v[n] for n in weight_names])
```

```python
import functools

import jax
import jax.numpy as jnp
from jax import lax
from jax.experimental import pallas as pl
from jax.experimental.pallas import tpu as pltpu

F32 = jnp.float32
BF16 = jnp.bfloat16
MESH = pl.DeviceIdType.MESH

N_DEV = 8
EPS = 1e-6
V7X_VMEM_BYTES = 64 * 1024 * 1024
VMEM_LIMIT = (V7X_VMEM_BYTES * 3) // 4
LANE = 128
SUBLANE = 8

CONV_HALO = 32
POOL_HALO = 16
FFN_HALO = 8
POOL_WINDOW_MAX = 16
XA_HEADS = 4

ADAM_LR = 0.001
ADAM_B1 = 0.9
ADAM_B2 = 0.999
ADAM_EPS = 1e-08
ADAM_WD = 0.01
ADAM_STEP = 10

GELU_C0 = 0.7978845608028654
GELU_C1 = 0.044715


ANY_SPEC = pl.BlockSpec(memory_space=pl.ANY)


def _tile(n, cap, mult=LANE):
    if n <= cap:
        return n
    best = None
    for d in range(mult, cap + 1, mult):
        if n % d == 0:
            best = d
    assert best is not None, (n, cap, mult)
    return best


def _params(*sem):
    return pltpu.CompilerParams(dimension_semantics=sem, vmem_limit_bytes=VMEM_LIMIT)


def _sig(x):
    return 1.0 / (1.0 + jnp.exp(-x))


def _bs(shape, imap):
    return pl.BlockSpec(shape, imap)


def _mxu_tile(n, cap):
    if n <= cap:
        return n
    best = {mult: max((d for d in range(mult, cap + 1, mult) if n % d == 0), default=0) for mult in (2 * LANE, LANE)}
    assert best[LANE] > 0, (n, cap)
    return best[2 * LANE] if 2 * best[2 * LANE] >= best[LANE] else best[LANE]


def _matmul(a, b, mode, name, res=None, out_dtype=F32, after=None, b_window=None):
    b_row0, b_rows = b_window if b_window is not None else (0, b.shape[0])
    if mode == "tn":
        k_dim, m_dim = a.shape
        k2, n_dim = b_rows, b.shape[1]
    elif mode == "nn":
        m_dim, k_dim = a.shape
        k2, n_dim = b_rows, b.shape[1]
    else:
        m_dim, k_dim = a.shape
        n_dim, k2 = b_rows, b.shape[1]
    assert k_dim == k2, (name, a.shape, b.shape)
    size = lambda t: jnp.dtype(t).itemsize
    tm = _mxu_tile(m_dim, 2816 if mode == "tn" else 1024)
    tn = _mxu_tile(n_dim, 2816)
    fixed = tm * tn * (2 * size(out_dtype) + (2 * size(res.dtype) if res is not None else 0) + 4)
    for cap in (2816, 2048, 1792, 1024, 512):
        tk = _mxu_tile(k_dim, cap)
        if fixed + 2 * tk * (tm * size(a.dtype) + tn * size(b.dtype)) <= VMEM_LIMIT - 8 * 1024 * 1024:
            break
    nk = k_dim // tk
    use_acc = nk > 1 and out_dtype != F32
    if mode == "tn":
        a_spec, ca = _bs((tk, tm), lambda i, j, k: (k, i)), 0
    else:
        a_spec, ca = _bs((tm, tk), lambda i, j, k: (i, k)), 1
    if mode == "nt":
        assert b_row0 % tn == 0
        b_spec, cb = _bs((tn, tk), lambda i, j, k: (j + b_row0 // tn, k)), 1
    else:
        assert b_row0 % tk == 0
        b_spec, cb = _bs((tk, tn), lambda i, j, k: (k + b_row0 // tk, j)), 0
    dims = (((ca,), (cb,)), ((), ()))
    o_spec = _bs((tm, tn), lambda i, j, k: (i, j))
    has_res = res is not None

    def body(*refs):
        a_ref, b_ref = refs[:2]
        r_ref = refs[2] if has_res else None
        o_ref = refs[n_in]
        k = pl.program_id(2)
        part = lax.dot_general(a_ref[...].astype(BF16), b_ref[...].astype(BF16), dims,
                               preferred_element_type=F32)
        if nk == 1:
            if has_res:
                part = part + r_ref[...].astype(F32)
            o_ref[...] = part.astype(out_dtype)
            return
        acc = refs[-1] if use_acc else o_ref

        @pl.when(k == 0)
        def _():
            acc[...] = part + r_ref[...].astype(F32) if has_res else part

        @pl.when(k > 0)
        def _():
            acc[...] += part

        if use_acc:
            @pl.when(k == nk - 1)
            def _():
                o_ref[...] = acc[...].astype(out_dtype)

    in_specs = [a_spec, b_spec] + ([o_spec] if has_res else [])
    args = (a, b) + ((res,) if has_res else ())
    if after is not None:
        in_specs.append(ANY_SPEC)
        args += (after,)
    n_in = len(args)
    return pl.pallas_call(
        body, out_shape=jax.ShapeDtypeStruct((m_dim, n_dim), out_dtype),
        grid=(m_dim // tm, n_dim // tn, nk), in_specs=in_specs, out_specs=o_spec,
        scratch_shapes=[pltpu.VMEM((tm, tn), F32)] if use_acc else [], name=name,
        compiler_params=_params("parallel", "parallel", "arbitrary"))(*args)


def _grouped(a, w, mode, name, out_dtype=F32):
    t_dim = a.shape[0]
    g_dim, r_dim, c_dim = w.shape
    ka, no = (c_dim, r_dim) if mode == "nt" else (r_dim, c_dim)
    tm = _tile(t_dim, 2048)
    dims = (((1,), (1 if mode == "nt" else 0,)), ((), ()))

    def body(a_ref, w_ref, o_ref):
        o_ref[...] = lax.dot_general(a_ref[...].astype(BF16), w_ref[...].astype(BF16), dims,
                                     preferred_element_type=F32).astype(out_dtype)

    return pl.pallas_call(
        body, out_shape=jax.ShapeDtypeStruct((t_dim, g_dim * no), out_dtype),
        grid=(t_dim // tm, g_dim),
        in_specs=[_bs((tm, ka), lambda i, g: (i, g)), _bs((None, r_dim, c_dim), lambda i, g: (g, 0, 0))],
        out_specs=_bs((tm, no), lambda i, g: (i, g)), name=name,
        compiler_params=_params("parallel", "parallel"))(a, w)


def _grouped_tn(a, b, g_dim, name):
    t_dim = a.shape[0]
    ra = a.shape[1] // g_dim
    cb = b.shape[1] // g_dim
    tm = _tile(t_dim, 2048)
    nt = t_dim // tm

    def body(a_ref, b_ref, o_ref):
        part = lax.dot_general(a_ref[...].astype(BF16), b_ref[...].astype(BF16), (((0,), (0,)), ((), ())),
                               preferred_element_type=F32)

        @pl.when(pl.program_id(1) == 0)
        def _():
            o_ref[...] = part

        @pl.when(pl.program_id(1) > 0)
        def _():
            o_ref[...] += part

    return pl.pallas_call(
        body, out_shape=jax.ShapeDtypeStruct((g_dim, ra, cb), F32), grid=(g_dim, nt),
        in_specs=[_bs((tm, ra), lambda g, i: (i, g)), _bs((tm, cb), lambda g, i: (i, g))],
        out_specs=_bs((None, ra, cb), lambda g, i: (g, 0, 0)), name=name,
        compiler_params=_params("parallel", "arbitrary"))(a, b)


def _rmsnorm_fwd(x, g, name, after=None):
    t_dim, d = x.shape
    tm = _tile(t_dim, 512)

    def body(x_ref, g_ref, *rest):
        o_ref = rest[-1]
        xv = x_ref[...]
        r = lax.rsqrt(jnp.mean(xv * xv, axis=-1, keepdims=True) + EPS)
        o_ref[...] = (xv * r * g_ref[...]).astype(BF16)

    return pl.pallas_call(
        body, out_shape=jax.ShapeDtypeStruct((t_dim, d), BF16), grid=(t_dim // tm,),
        in_specs=[_bs((tm, d), lambda i: (i, 0)), _bs((1, d), lambda i: (0, 0))] + ([ANY_SPEC] if after is not None else []),
        out_specs=_bs((tm, d), lambda i: (i, 0)), name=name,
        compiler_params=_params("parallel"))(x, g, *([after] if after is not None else []))


def _rmsnorm_bwd(x, g, dh, dx_in, name):
    t_dim, d = x.shape
    tm = _tile(t_dim, 512)
    has_in = dx_in is not None

    def body(*refs):
        if has_in:
            x_ref, g_ref, dh_ref, di_ref, dx_ref, dxb_ref, dg_ref = refs
        else:
            x_ref, g_ref, dh_ref, dx_ref, dxb_ref, dg_ref = refs
        xv = x_ref[...]
        r = lax.rsqrt(jnp.mean(xv * xv, axis=-1, keepdims=True) + EPS)
        xh = xv * r
        dhv = dh_ref[...].astype(F32)
        dxh = dhv * g_ref[...]
        dx = r * (dxh - xh * jnp.mean(dxh * xh, axis=-1, keepdims=True))
        if has_in:
            dx = dx + di_ref[...]
        dx_ref[...] = dx
        dxb_ref[...] = dx.astype(BF16)
        part = jnp.sum(dhv * xh, axis=0, keepdims=True)

        @pl.when(pl.program_id(0) == 0)
        def _():
            dg_ref[...] = part

        @pl.when(pl.program_id(0) > 0)
        def _():
            dg_ref[...] += part

    row = _bs((tm, d), lambda i: (i, 0))
    vec = _bs((1, d), lambda i: (0, 0))
    args = (x, g, dh) + ((dx_in,) if has_in else ())
    return pl.pallas_call(
        body, out_shape=(jax.ShapeDtypeStruct((t_dim, d), F32), jax.ShapeDtypeStruct((t_dim, d), BF16),
                         jax.ShapeDtypeStruct((1, d), F32)),
        grid=(t_dim // tm,), in_specs=[row, vec, row] + ([row] if has_in else []),
        out_specs=(row, row, vec), name=name, compiler_params=_params("arbitrary"))(*args)


def _matmul_rmsnorm_bwd(a, b, mode, x, g, dx_in, name, res=None, b_window=None):
    pieces = a if isinstance(a, tuple) else (a,)
    n_p = len(pieces)
    b_row0, b_rows = b_window if b_window is not None else (0, b.shape[0])
    m_dim, k_piece = pieces[0].shape
    assert all(t.shape == pieces[0].shape for t in pieces)
    k_dim = n_p * k_piece
    d = x.shape[1]
    assert (b_rows, b.shape[1]) == ((k_dim, d) if mode == "nn" else (d, k_dim)), (name, pieces[0].shape, b.shape)
    tm = _mxu_tile(m_dim, 512)
    tk = _mxu_tile(k_piece, 1792)
    nkp = k_piece // tk
    nk = n_p * nkp
    has_res = res is not None
    if mode == "nt":
        assert b_row0 == 0
        b_spec, cb = _bs((d, tk), lambda i, k: (0, k)), 1
    else:
        assert b_row0 % tk == 0
        b_spec, cb = _bs((tk, d), lambda i, k: (k + b_row0 // tk, 0)), 0
    dims = (((1,), (cb,)), ((), ()))

    def body(*refs):
        b_ref = refs[n_p]
        r_ref = refs[n_p + 1] if has_res else None
        x_ref, g_ref, di_ref, dx_ref, dxb_ref, dg_ref = refs[n_p + 1 + has_res:n_p + 7 + has_res]
        i, k = pl.program_id(0), pl.program_id(1)

        def finish(dhv):
            if has_res:
                dhv = dhv + r_ref[...].astype(F32)
            xv = x_ref[...]
            r = lax.rsqrt(jnp.mean(xv * xv, axis=-1, keepdims=True) + EPS)
            xh = xv * r
            dxh = dhv * g_ref[...]
            dx = r * (dxh - xh * jnp.mean(dxh * xh, axis=-1, keepdims=True)) + di_ref[...]
            dx_ref[...] = dx
            dxb_ref[...] = dx.astype(BF16)
            dg_part = jnp.sum(dhv * xh, axis=0, keepdims=True)

            @pl.when(i == 0)
            def _():
                dg_ref[...] = dg_part

            @pl.when(i > 0)
            def _():
                dg_ref[...] += dg_part

        def step(a_ref):
            part = lax.dot_general(a_ref[...].astype(BF16), b_ref[...].astype(BF16), dims, preferred_element_type=F32)
            if nk == 1:
                finish(part)
                return
            acc = refs[-1]

            @pl.when(k == 0)
            def _():
                acc[...] = part

            @pl.when(jnp.logical_and(k > 0, k < nk - 1))
            def _():
                acc[...] += part

            @pl.when(k == nk - 1)
            def _():
                finish(acc[...] + part)

        if n_p == 1:
            step(refs[0])
        else:
            for q in range(n_p):
                pl.when(jnp.logical_and(k >= q * nkp, k < (q + 1) * nkp))(functools.partial(step, refs[q]))

    row = _bs((tm, d), lambda i, k: (i, 0))
    vec = _bs((1, d), lambda i, k: (0, 0))
    a_specs = [_bs((tm, tk), lambda i, k, q=q: (i, jnp.clip(k - q * nkp, 0, nkp - 1))) for q in range(n_p)]
    in_specs = a_specs + [b_spec] + ([row] if has_res else []) + [row, vec, row]
    args = pieces + (b,) + ((res,) if has_res else ()) + (x, g, dx_in)
    return pl.pallas_call(
        body, out_shape=(jax.ShapeDtypeStruct((m_dim, d), F32), jax.ShapeDtypeStruct((m_dim, d), BF16),
                         jax.ShapeDtypeStruct((1, d), F32)),
        grid=(m_dim // tm, nk), in_specs=in_specs, out_specs=(row, row, vec),
        scratch_shapes=[pltpu.VMEM((tm, d), F32)] if nk > 1 else [], name=name,
        compiler_params=_params("arbitrary", "arbitrary"))(*args)


def _loss_head(x, g, tgt, name):
    t_dim, d = x.shape
    tm = _tile(t_dim, 512)

    def body(x_ref, g_ref, t_ref, dx_ref, dxb_ref, dg_ref, loss_ref):
        xv = x_ref[...]
        gv = g_ref[...]
        r = lax.rsqrt(jnp.mean(xv * xv, axis=-1, keepdims=True) + EPS)
        xh = xv * r
        err = xh * gv - t_ref[...]
        dy = err * (1.0 / d)
        dxh = dy * gv
        dx = r * (dxh - xh * jnp.mean(dxh * xh, axis=-1, keepdims=True))
        dx_ref[...] = dx
        dxb_ref[...] = dx.astype(BF16)
        dg_part = jnp.sum(dy * xh, axis=0, keepdims=True)
        loss_part = jnp.full((1, LANE), 0.5 * jnp.sum(jnp.mean(err * err, axis=-1, keepdims=True)), F32)

        @pl.when(pl.program_id(0) == 0)
        def _():
            dg_ref[...] = dg_part
            loss_ref[...] = loss_part

        @pl.when(pl.program_id(0) > 0)
        def _():
            dg_ref[...] += dg_part
            loss_ref[...] += loss_part

    row = _bs((tm, d), lambda i: (i, 0))
    vec = _bs((1, d), lambda i: (0, 0))
    return pl.pallas_call(
        body, out_shape=(jax.ShapeDtypeStruct((t_dim, d), F32), jax.ShapeDtypeStruct((t_dim, d), BF16),
                         jax.ShapeDtypeStruct((1, d), F32), jax.ShapeDtypeStruct((1, LANE), F32)),
        grid=(t_dim // tm,), in_specs=[row, vec, row],
        out_specs=(row, row, vec, _bs((1, LANE), lambda i: (0, 0))), name=name,
        compiler_params=_params("arbitrary"))(x, g, tgt)


def _glu_conv_fwd(proj, dw_w, dw_b, n_batch, seq, name):
    kk, cc = dw_w.shape
    nj = cc // LANE
    ch = min(256, seq)

    def body(a_ref, gl_ref, w_ref, b_ref, o_ref, pad):
        pad[0:CONV_HALO, :] = jnp.zeros((CONV_HALO, LANE), F32)
        pad[CONV_HALO:CONV_HALO + seq, :] = a_ref[...].astype(F32) * _sig(gl_ref[...].astype(F32))
        for c0 in range(0, seq, ch):
            acc = jnp.broadcast_to(b_ref[...], (ch, LANE))
            for k in range(kk):
                acc = acc + w_ref[k:k + 1, :] * pad[pl.ds(c0 + CONV_HALO - (kk - 1) + k, ch), :]
            o_ref[c0:c0 + ch, :] = acc

    return pl.pallas_call(
        body, out_shape=jax.ShapeDtypeStruct((n_batch * seq, cc), F32), grid=(n_batch, nj),
        in_specs=[_bs((seq, LANE), lambda b, j: (b, j)), _bs((seq, LANE), lambda b, j: (b, nj + j)),
                  _bs((kk, LANE), lambda b, j: (0, j)), _bs((1, LANE), lambda b, j: (0, j))],
        out_specs=_bs((seq, LANE), lambda b, j: (b, j)),
        scratch_shapes=[pltpu.VMEM((seq + CONV_HALO, LANE), F32)], name=name,
        compiler_params=_params("parallel", "parallel"))(proj, proj, dw_w, dw_b)


def _glu_conv_bwd(proj, dw_w, dy1, n_batch, seq, name):
    kk, cc = dw_w.shape
    nj = cc // LANE
    ch = min(256, seq)

    def body(a_ref, gl_ref, dy_ref, w_ref, da_ref, dgl_ref, dw_ref, db_ref, padf, padb):
        first = pl.program_id(1) == 0
        padf[0:CONV_HALO, :] = jnp.zeros((CONV_HALO, LANE), F32)
        padf[CONV_HALO:CONV_HALO + seq, :] = a_ref[...].astype(F32) * _sig(gl_ref[...].astype(F32))
        padb[0:seq, :] = dy_ref[...]
        padb[seq:seq + CONV_HALO, :] = jnp.zeros((CONV_HALO, LANE), F32)

        @pl.when(first)
        def _():
            dw_ref[...] = jnp.zeros((kk, LANE), F32)
            db_ref[...] = jnp.zeros((1, LANE), F32)

        dws = [jnp.zeros((1, LANE), F32) for _ in range(kk)]
        for c0 in range(0, seq, ch):
            acc = jnp.zeros((ch, LANE), F32)
            y0 = padf[CONV_HALO + c0:CONV_HALO + c0 + ch, :]
            for k in range(kk):
                win = padb[pl.ds(c0 + (kk - 1) - k, ch), :]
                acc = acc + w_ref[k:k + 1, :] * win
                dws[k] = dws[k] + jnp.sum(win * y0, axis=0, keepdims=True)
            sg = _sig(gl_ref[c0:c0 + ch, :].astype(F32))
            da_ref[c0:c0 + ch, :] = (acc * sg).astype(BF16)
            dgl_ref[c0:c0 + ch, :] = (acc * a_ref[c0:c0 + ch, :].astype(F32) * sg * (1.0 - sg)).astype(BF16)
        for k in range(kk):
            dw_ref[k:k + 1, :] += dws[k]
        db_ref[...] += jnp.sum(dy_ref[...], axis=0, keepdims=True)

    tok = _bs((seq, LANE), lambda j, b: (b, j))
    t_dim = n_batch * seq
    return pl.pallas_call(
        body, out_shape=(jax.ShapeDtypeStruct((t_dim, cc), BF16), jax.ShapeDtypeStruct((t_dim, cc), BF16),
                         jax.ShapeDtypeStruct((kk, cc), F32), jax.ShapeDtypeStruct((1, cc), F32)),
        grid=(nj, n_batch),
        in_specs=[tok, _bs((seq, LANE), lambda j, b: (b, nj + j)), tok, _bs((kk, LANE), lambda j, b: (0, j))],
        out_specs=(tok, tok, _bs((kk, LANE), lambda j, b: (0, j)), _bs((1, LANE), lambda j, b: (0, j))),
        scratch_shapes=[pltpu.VMEM((seq + CONV_HALO, LANE), F32), pltpu.VMEM((seq + CONV_HALO, LANE), F32)],
        name=name, compiler_params=_params("parallel", "arbitrary"))(proj, proj, dy1, dw_w)


def _ln_silu_fwd(y1, g, b, name):
    t_dim, c = y1.shape
    tm = _tile(t_dim, 512)

    def body(y_ref, g_ref, b_ref, o_ref):
        yv = y_ref[...]
        xc = yv - jnp.mean(yv, axis=-1, keepdims=True)
        rstd = lax.rsqrt(jnp.mean(xc * xc, axis=-1, keepdims=True) + EPS)
        y2 = xc * rstd * g_ref[...] + b_ref[...]
        o_ref[...] = (y2 * _sig(y2)).astype(BF16)

    row = _bs((tm, c), lambda i: (i, 0))
    vec = _bs((1, c), lambda i: (0, 0))
    return pl.pallas_call(
        body, out_shape=jax.ShapeDtypeStruct((t_dim, c), BF16), grid=(t_dim // tm,),
        in_specs=[row, vec, vec], out_specs=row, name=name, compiler_params=_params("parallel"))(y1, g, b)


def _ln_silu_bwd(y1, g, b, dy3, name):
    t_dim, c = y1.shape
    tm = _tile(t_dim, 512)

    def body(y_ref, g_ref, b_ref, d_ref, dy_ref, dg_ref, db_ref):
        yv = y_ref[...]
        gv = g_ref[...]
        xc = yv - jnp.mean(yv, axis=-1, keepdims=True)
        rstd = lax.rsqrt(jnp.mean(xc * xc, axis=-1, keepdims=True) + EPS)
        yh = xc * rstd
        y2 = yh * gv + b_ref[...]
        s = _sig(y2)
        dy2 = d_ref[...].astype(F32) * (s * (1.0 + y2 * (1.0 - s)))
        dyh = dy2 * gv
        dy_ref[...] = rstd * (dyh - jnp.mean(dyh, axis=-1, keepdims=True)
                              - yh * jnp.mean(dyh * yh, axis=-1, keepdims=True))
        dg_part = jnp.sum(dy2 * yh, axis=0, keepdims=True)
        db_part = jnp.sum(dy2, axis=0, keepdims=True)

        @pl.when(pl.program_id(0) == 0)
        def _():
            dg_ref[...] = dg_part
            db_ref[...] = db_part

        @pl.when(pl.program_id(0) > 0)
        def _():
            dg_ref[...] += dg_part
            db_ref[...] += db_part

    row = _bs((tm, c), lambda i: (i, 0))
    vec = _bs((1, c), lambda i: (0, 0))
    return pl.pallas_call(
        body, out_shape=(jax.ShapeDtypeStruct((t_dim, c), F32), jax.ShapeDtypeStruct((1, c), F32),
                         jax.ShapeDtypeStruct((1, c), F32)),
        grid=(t_dim // tm,), in_specs=[row, vec, vec, row], out_specs=(row, vec, vec), name=name,
        compiler_params=_params("arbitrary"))(y1, g, b, dy3)


def _pool_fwd(proj, col0, n_groups, n_batch, seq, name):
    ch = min(256, seq)

    def body(u_ref, o_ref, pad):
        w = lax.shift_left(jnp.int32(2), pl.program_id(1))
        pad[0:POOL_HALO, :] = jnp.zeros((POOL_HALO, LANE), F32)
        pad[POOL_HALO:POOL_HALO + seq, :] = u_ref[...].astype(F32)
        for c0 in range(0, seq, ch):
            acc = jnp.zeros((ch, LANE), F32)
            for j in range(POOL_WINDOW_MAX):
                acc = acc + jnp.where(j < w, 1.0, 0.0).astype(F32) * pad[pl.ds(c0 + POOL_HALO - j, ch), :]
            t = c0 + lax.broadcasted_iota(jnp.int32, (ch, LANE), 0)
            cnt = jnp.minimum(t + 1, w).astype(F32)
            o_ref[c0:c0 + ch, :] = (acc / cnt - pad[POOL_HALO + c0:POOL_HALO + c0 + ch, :]).astype(BF16)

    return pl.pallas_call(
        body, out_shape=jax.ShapeDtypeStruct((n_batch * seq, n_groups * LANE), BF16), grid=(n_batch, n_groups),
        in_specs=[_bs((seq, LANE), lambda b, g: (b, col0 + g))], out_specs=_bs((seq, LANE), lambda b, g: (b, g)),
        scratch_shapes=[pltpu.VMEM((seq + POOL_HALO, LANE), F32)], name=name,
        compiler_params=_params("parallel", "parallel"))(proj)


def _pool_bwd(dzp, n_groups, n_batch, seq, name):
    ch = min(256, seq)

    def body(d_ref, o_ref, pad):
        w = lax.shift_left(jnp.int32(2), pl.program_id(1))
        for c0 in range(0, seq, ch):
            t = c0 + lax.broadcasted_iota(jnp.int32, (ch, LANE), 0)
            cnt = jnp.minimum(t + 1, w).astype(F32)
            pad[c0:c0 + ch, :] = d_ref[c0:c0 + ch, :].astype(F32) / cnt
        pad[seq:seq + POOL_HALO, :] = jnp.zeros((POOL_HALO, LANE), F32)
        for c0 in range(0, seq, ch):
            acc = jnp.zeros((ch, LANE), F32)
            for j in range(POOL_WINDOW_MAX):
                acc = acc + jnp.where(j < w, 1.0, 0.0).astype(F32) * pad[pl.ds(c0 + j, ch), :]
            o_ref[c0:c0 + ch, :] = (acc - d_ref[c0:c0 + ch, :].astype(F32)).astype(BF16)

    tok = _bs((seq, LANE), lambda b, g: (b, g))
    return pl.pallas_call(
        body, out_shape=jax.ShapeDtypeStruct((n_batch * seq, n_groups * LANE), BF16), grid=(n_batch, n_groups),
        in_specs=[tok], out_specs=tok, scratch_shapes=[pltpu.VMEM((seq + POOL_HALO, LANE), F32)], name=name,
        compiler_params=_params("parallel", "parallel"))(dzp)


def _merge_fwd(proj, col0, yc, yp, scale, name):
    t_dim, d = yc.shape
    half = d // 2
    tm = _tile(t_dim, 512)
    c0 = col0 // half

    def body(gc_ref, gp_ref, yc_ref, yp_ref, s_ref, o_ref):
        f32 = lambda r: r[...].astype(F32)
        o_ref[...] = (_sig(f32(gc_ref)) * f32(yc_ref) + _sig(f32(gp_ref)) * (f32(yp_ref) * s_ref[...])).astype(BF16)

    blk = _bs((tm, half), lambda i, j: (i, j))
    return pl.pallas_call(
        body, out_shape=jax.ShapeDtypeStruct((t_dim, d), BF16), grid=(t_dim // tm, 2),
        in_specs=[_bs((tm, half), lambda i, j: (i, c0 + j)), _bs((tm, half), lambda i, j: (i, c0 + 2 + j)),
                  blk, blk, _bs((1, half), lambda i, j: (0, j))],
        out_specs=blk, name=name, compiler_params=_params("parallel", "parallel"))(proj, proj, yc, yp, scale)


def _merge_bwd(proj, col0, yc, yp, scale, dm, name):
    t_dim, d = yc.shape
    half = d // 2
    tm = _tile(t_dim, 512)
    c0 = col0 // half

    def body(gc_ref, gp_ref, yc_ref, yp_ref, s_ref, dm_ref, dgc_ref, dgp_ref, dyc_ref, dyp_ref, ds_ref):
        dmv = dm_ref[...].astype(F32)
        sgc = _sig(gc_ref[...].astype(F32))
        sgp = _sig(gp_ref[...].astype(F32))
        sv = s_ref[...]
        ypre = yp_ref[...].astype(F32)
        dgc_ref[...] = (dmv * yc_ref[...].astype(F32) * sgc * (1.0 - sgc)).astype(BF16)
        dgp_ref[...] = (dmv * (ypre * sv) * sgp * (1.0 - sgp)).astype(BF16)
        dyc_ref[...] = (dmv * sgc).astype(BF16)
        dyp = dmv * sgp
        dyp_ref[...] = (dyp * sv).astype(BF16)
        part = jnp.sum(dyp * ypre, axis=0, keepdims=True)

        @pl.when(pl.program_id(1) == 0)
        def _():
            ds_ref[...] = part

        @pl.when(pl.program_id(1) > 0)
        def _():
            ds_ref[...] += part

    blk = _bs((tm, half), lambda j, i: (i, j))
    big = jax.ShapeDtypeStruct((t_dim, d), BF16)
    return pl.pallas_call(
        body, out_shape=(big, big, big, big, jax.ShapeDtypeStruct((1, d), F32)), grid=(2, t_dim // tm),
        in_specs=[_bs((tm, half), lambda j, i: (i, c0 + j)), _bs((tm, half), lambda j, i: (i, c0 + 2 + j)),
                  blk, blk, _bs((1, half), lambda j, i: (0, j)), blk],
        out_specs=(blk, blk, blk, blk, _bs((1, half), lambda j, i: (0, j))), name=name,
        compiler_params=_params("parallel", "arbitrary"))(proj, proj, yc, yp, scale, dm)


def _attn_fwd(q, kv, n_batch, seq, m_len, name):
    d = q.shape[1]
    hd = d // XA_HEADS
    tq = _tile(seq, 1024)
    nq = seq // tq
    scale = hd ** -0.5

    def body(q_ref, k_ref, v_ref, o_ref):
        sc = lax.dot_general(q_ref[...].astype(BF16), k_ref[...].astype(BF16), (((1,), (1,)), ((), ())),
                             preferred_element_type=F32) * scale
        p = jnp.exp(sc - jnp.max(sc, axis=-1, keepdims=True))
        pr = p / jnp.sum(p, axis=-1, keepdims=True)
        o_ref[...] = jnp.dot(pr.astype(BF16), v_ref[...].astype(BF16), preferred_element_type=F32).astype(BF16)

    return pl.pallas_call(
        body, out_shape=jax.ShapeDtypeStruct((n_batch * seq, d), BF16), grid=(n_batch, XA_HEADS, nq),
        in_specs=[_bs((tq, hd), lambda b, h, i: (b * nq + i, h)), _bs((m_len, hd), lambda b, h, i: (b, h)),
                  _bs((m_len, hd), lambda b, h, i: (b, XA_HEADS + h))],
        out_specs=_bs((tq, hd), lambda b, h, i: (b * nq + i, h)), name=name,
        compiler_params=_params("parallel", "parallel", "parallel"))(q, kv, kv)


def _attn_bwd(q, kv, datt, n_batch, seq, m_len, name):
    d = q.shape[1]
    hd = d // XA_HEADS
    tq = _tile(seq, 1024)
    nq = seq // tq
    scale = hd ** -0.5

    def body(q_ref, k_ref, v_ref, do_ref, dq_ref, dk_ref, dv_ref):
        qb = q_ref[...].astype(BF16)
        kb = k_ref[...].astype(BF16)
        vb = v_ref[...].astype(BF16)
        dob = do_ref[...].astype(BF16)
        sc = lax.dot_general(qb, kb, (((1,), (1,)), ((), ())), preferred_element_type=F32) * scale
        p = jnp.exp(sc - jnp.max(sc, axis=-1, keepdims=True))
        pr = p / jnp.sum(p, axis=-1, keepdims=True)
        dpr = lax.dot_general(dob, vb, (((1,), (1,)), ((), ())), preferred_element_type=F32)
        dsc = pr * (dpr - jnp.sum(dpr * pr, axis=-1, keepdims=True)) * scale
        dsb = dsc.astype(BF16)
        dq_ref[...] = jnp.dot(dsb, kb, preferred_element_type=F32).astype(BF16)
        dv_part = lax.dot_general(pr.astype(BF16), dob, (((0,), (0,)), ((), ())), preferred_element_type=F32)
        dk_part = lax.dot_general(dsb, qb, (((0,), (0,)), ((), ())), preferred_element_type=F32)

        @pl.when(pl.program_id(2) == 0)
        def _():
            dk_ref[...] = dk_part
            dv_ref[...] = dv_part

        @pl.when(pl.program_id(2) > 0)
        def _():
            dk_ref[...] += dk_part
            dv_ref[...] += dv_part

    qs = _bs((tq, hd), lambda b, h, i: (b * nq + i, h))
    ks = _bs((m_len, hd), lambda b, h, i: (b, h))
    return pl.pallas_call(
        body, out_shape=(jax.ShapeDtypeStruct((n_batch * seq, d), BF16), jax.ShapeDtypeStruct((n_batch * m_len, d), F32),
                         jax.ShapeDtypeStruct((n_batch * m_len, d), F32)),
        grid=(n_batch, XA_HEADS, nq),
        in_specs=[qs, ks, _bs((m_len, hd), lambda b, h, i: (b, XA_HEADS + h)), qs],
        out_specs=(qs, ks, ks), name=name,
        compiler_params=_params("parallel", "parallel", "arbitrary"))(q, kv, kv, datt)


def _gelu_parts(g):
    th = jnp.tanh(GELU_C0 * (g + GELU_C1 * g * g * g))
    return th, 0.5 * g * (1.0 + th)


def _ffn_act_fwd(up_g, up_v, dw_w, n_batch, seq, name):
    kk, c2 = dw_w.shape
    f_dim = c2 // 2
    wd = 2 * LANE
    nj = f_dim // wd
    ch = min(128, seq)

    def body(g_ref, v_ref, wg_ref, wv_ref, o_ref, padg, padv):
        for pad, src in ((padg, g_ref), (padv, v_ref)):
            pad[0:FFN_HALO, :] = jnp.zeros((FFN_HALO, wd), F32)
            pad[FFN_HALO:FFN_HALO + seq, :] = src[...].astype(F32)
        for c0 in range(0, seq, ch):
            gate = jnp.zeros((ch, wd), F32)
            val = jnp.zeros((ch, wd), F32)
            for k in range(kk):
                off = c0 + FFN_HALO - (kk - 1) + k
                gate = gate + wg_ref[k:k + 1, :] * padg[pl.ds(off, ch), :]
                val = val + wv_ref[k:k + 1, :] * padv[pl.ds(off, ch), :]
            o_ref[c0:c0 + ch, :] = (_gelu_parts(gate)[1] * val).astype(BF16)

    return pl.pallas_call(
        body, out_shape=jax.ShapeDtypeStruct((n_batch * seq, f_dim), BF16), grid=(n_batch, nj),
        in_specs=[_bs((seq, wd), lambda b, j: (b, j)), _bs((seq, wd), lambda b, j: (b, j)),
                  _bs((kk, wd), lambda b, j: (0, j)), _bs((kk, wd), lambda b, j: (0, nj + j))],
        out_specs=_bs((seq, wd), lambda b, j: (b, j)),
        scratch_shapes=[pltpu.VMEM((seq + FFN_HALO, wd), F32), pltpu.VMEM((seq + FFN_HALO, wd), F32)], name=name,
        compiler_params=_params("parallel", "parallel"))(up_g, up_v, dw_w, dw_w)


def _ffn_act_bwd(up_g, up_v, dw_w, dact, n_batch, seq, name):
    kk, c2 = dw_w.shape
    f_dim = c2 // 2
    wd = 2 * LANE
    nj = f_dim // wd
    ch = min(128, seq)

    def body(g_ref, v_ref, wg_ref, wv_ref, da_ref, dg_ref, dv_ref, dwg_ref, dwv_ref, padg, padv, pbg, pbv):
        for pad, src in ((padg, g_ref), (padv, v_ref)):
            pad[0:FFN_HALO, :] = jnp.zeros((FFN_HALO, wd), F32)
            pad[FFN_HALO:FFN_HALO + seq, :] = src[...].astype(F32)
        for pb in (pbg, pbv):
            pb[seq:seq + FFN_HALO, :] = jnp.zeros((FFN_HALO, wd), F32)

        @pl.when(pl.program_id(1) == 0)
        def _():
            dwg_ref[...] = jnp.zeros((kk, wd), F32)
            dwv_ref[...] = jnp.zeros((kk, wd), F32)

        for c0 in range(0, seq, ch):
            gate = jnp.zeros((ch, wd), F32)
            val = jnp.zeros((ch, wd), F32)
            for k in range(kk):
                off = c0 + FFN_HALO - (kk - 1) + k
                gate = gate + wg_ref[k:k + 1, :] * padg[pl.ds(off, ch), :]
                val = val + wv_ref[k:k + 1, :] * padv[pl.ds(off, ch), :]
            sq = gate * gate
            th = jnp.tanh(GELU_C0 * gate * (1.0 + GELU_C1 * sq))
            half = 0.5 * th + 0.5
            dgelu = half * (1.0 + gate * (GELU_C0 + 3.0 * GELU_C0 * GELU_C1 * sq) * (1.0 - th))
            dav = da_ref[c0:c0 + ch, :].astype(F32)
            pbg[c0:c0 + ch, :] = dav * val * dgelu
            pbv[c0:c0 + ch, :] = dav * (gate * half)
        for pb, pad, w_ref, d_ref, dw_ref in ((pbg, padg, wg_ref, dg_ref, dwg_ref), (pbv, padv, wv_ref, dv_ref, dwv_ref)):
            for c0 in range(0, seq, ch):
                acc = jnp.zeros((ch, wd), F32)
                for k in range(kk):
                    acc = acc + w_ref[k:k + 1, :] * pb[pl.ds(c0 + (kk - 1) - k, ch), :]
                d_ref[c0:c0 + ch, :] = acc.astype(BF16)
            for k in range(kk):
                s = jnp.zeros((1, wd), F32)
                for c0 in range(0, seq, ch):
                    s = s + jnp.sum(pb[c0:c0 + ch, :] * pad[pl.ds(c0 + FFN_HALO - (kk - 1) + k, ch), :],
                                    axis=0, keepdims=True)
                dw_ref[k:k + 1, :] += s

    t_dim = n_batch * seq
    tok = _bs((seq, wd), lambda j, b: (b, j))
    wblk = _bs((kk, wd), lambda j, b: (0, j))
    pad_shape = pltpu.VMEM((seq + FFN_HALO, wd), F32)
    return pl.pallas_call(
        body, out_shape=(jax.ShapeDtypeStruct((t_dim, f_dim), BF16), jax.ShapeDtypeStruct((t_dim, f_dim), BF16),
                         jax.ShapeDtypeStruct((kk, f_dim), F32), jax.ShapeDtypeStruct((kk, f_dim), F32)),
        grid=(nj, n_batch),
        in_specs=[tok, tok, wblk, _bs((kk, wd), lambda j, b: (0, nj + j)), tok],
        out_specs=(tok, tok, wblk, wblk), scratch_shapes=[pad_shape, pad_shape, pad_shape, pad_shape], name=name,
        compiler_params=_params("parallel", "arbitrary"))(up_g, up_v, dw_w, dw_w, dact)


def _sum_rows(parts, out_dtype, name):
    r_dim, c_dim = parts[0].shape
    tr = _tile(r_dim, 1200, SUBLANE)
    n = len(parts)

    def body(*refs):
        acc = refs[0][...].astype(F32)
        for r in refs[1:n]:
            acc = acc + r[...].astype(F32)
        refs[n][...] = acc.astype(out_dtype)

    blk = _bs((tr, c_dim), lambda i: (i, 0))
    return pl.pallas_call(
        body, out_shape=jax.ShapeDtypeStruct((r_dim, c_dim), out_dtype), grid=(r_dim // tr,),
        in_specs=[blk] * n, out_specs=blk, name=name, compiler_params=_params("parallel"))(*parts)


def _adamw(w, g, m, v, name):
    shape = w.shape
    c_dim = shape[-1]
    r_dim = w.size // c_dim
    two_d = lambda t: t.reshape(r_dim, c_dim)
    tr = _tile(r_dim, max(SUBLANE, (256 * 1024) // max(c_dim, LANE) // SUBLANE * SUBLANE), SUBLANE)
    c1 = 1.0 - ADAM_B1 ** ADAM_STEP
    c2 = 1.0 - ADAM_B2 ** ADAM_STEP

    def body(w_ref, g_ref, m_ref, v_ref, d_ref, mo_ref, vo_ref):
        gv = g_ref[...]
        mn = ADAM_B1 * m_ref[...] + (1.0 - ADAM_B1) * gv
        vn = ADAM_B2 * v_ref[...] + (1.0 - ADAM_B2) * (gv * gv)
        mo_ref[...] = mn
        vo_ref[...] = vn
        d_ref[...] = -ADAM_LR * ((mn / c1) / (jnp.sqrt(vn / c2) + ADAM_EPS) + ADAM_WD * w_ref[...])

    blk = _bs((tr, c_dim), lambda i: (i, 0))
    out = jax.ShapeDtypeStruct((r_dim, c_dim), F32)
    d, mo, vo = pl.pallas_call(
        body, out_shape=(out, out, out), grid=(r_dim // tr,), in_specs=[blk] * 4, out_specs=(blk, blk, blk),
        name=name, compiler_params=_params("parallel"))(two_d(w), two_d(g), two_d(m), two_d(v))
    return d.reshape(shape), mo.reshape(shape), vo.reshape(shape)


HBM_SPEC = pl.BlockSpec(memory_space=pltpu.HBM)


def _position():
    return lax.axis_index("x"), lax.axis_index("y"), lax.axis_index("c")


def _all_gather(shard, name):
    def body(x_ref, out_ref, send_sems, recv_sems, local_sem):
        x, y, c = _position()
        me, sibling = (x, y, c), (x, y, 1 - c)
        chips = [(1 - x, y), (x, 1 - y), (1 - x, 1 - y)]

        def rows(px, py, pc):
            return out_ref.at[4 * px + 2 * py + pc]

        def copy(k, block, to, src=None):
            return pltpu.make_async_remote_copy(
                src_ref=rows(*block) if src is None else src, dst_ref=rows(*block),
                send_sem=send_sems.at[k], recv_sem=recv_sems.at[k], device_id=to, device_id_type=MESH)

        mine = pltpu.make_async_copy(x_ref, rows(*me), local_sem)
        mine.start()
        first = [copy(0, me, sibling, src=x_ref)]
        first += [copy(1 + j, me, (*chip, c), src=x_ref) for j, chip in enumerate(chips)]
        for cp in first:
            cp.start()
        passed = [copy(4 + j, (*chip, c), sibling) for j, chip in enumerate(chips)]
        for j, chip in enumerate(chips):
            copy(1 + j, (*chip, c), me).wait_recv()
            passed[j].start()
        copy(0, sibling, me).wait_recv()
        for j, chip in enumerate(chips):
            copy(4 + j, (*chip, 1 - c), me).wait_recv()
        for cp in first + passed:
            cp.wait_send()
        mine.wait()

    return pl.pallas_call(
        body, out_shape=jax.ShapeDtypeStruct((N_DEV,) + shard.shape, shard.dtype),
        in_specs=[HBM_SPEC], out_specs=HBM_SPEC,
        scratch_shapes=[pltpu.SemaphoreType.DMA((7,)), pltpu.SemaphoreType.DMA((7,)), pltpu.SemaphoreType.DMA(())],
        name=name)(shard)


CHIP_RELATIONS = ((0, 0), (1, 0), (0, 1), (1, 1))


def _rs_pair_exchange(g, name):
    _, r_dim, c_dim = g.shape
    n = len(CHIP_RELATIONS)

    def body(g_ref, recv_ref, send_sems, recv_sems):
        x, y, c = _position()
        sibling = (x, y, 1 - c)
        copies = []
        for k, (rx, ry) in enumerate(CHIP_RELATIONS):
            px = x + rx - 2 * x * rx
            py = y + ry - 2 * y * ry
            copies.append(pltpu.make_async_remote_copy(
                src_ref=g_ref.at[4 * px + 2 * py + 1 - c], dst_ref=recv_ref.at[k], send_sem=send_sems.at[k],
                recv_sem=recv_sems.at[k], device_id=sibling, device_id_type=MESH))
        for cp in copies:
            cp.start()
        for cp in copies:
            cp.wait()

    return pl.pallas_call(
        body, out_shape=jax.ShapeDtypeStruct((n, r_dim, c_dim), g.dtype), in_specs=[HBM_SPEC], out_specs=HBM_SPEC,
        scratch_shapes=[pltpu.SemaphoreType.DMA((n,)), pltpu.SemaphoreType.DMA((n,))], name=name)(g)


def _rs_pair_sum(g, recv, name):
    _, r_dim, c_dim = g.shape
    n = len(CHIP_RELATIONS)
    tr = _tile(r_dim, 1200, SUBLANE)
    x, y, c = _position()
    own = jnp.stack([4 * (x + rx - 2 * x * rx) + 2 * (y + ry - 2 * y * ry) + c for rx, ry in CHIP_RELATIONS])

    def body(own_ref, g_ref, r_ref, o_ref):
        o_ref[...] = (g_ref[...].astype(F32) + r_ref[...].astype(F32)).astype(o_ref.dtype)

    blk = _bs((None, tr, c_dim), lambda k, i, own_ref: (k, i, 0))
    return pl.pallas_call(
        body, out_shape=jax.ShapeDtypeStruct((n, r_dim, c_dim), g.dtype),
        grid_spec=pltpu.PrefetchScalarGridSpec(
            num_scalar_prefetch=1, grid=(n, r_dim // tr),
            in_specs=[_bs((None, tr, c_dim), lambda k, i, own_ref: (own_ref[k], i, 0)), blk], out_specs=blk),
        name=name, compiler_params=_params("parallel", "parallel"))(own.astype(jnp.int32), g, recv)


SEM_SPEC = pl.BlockSpec(memory_space=pltpu.SEMAPHORE)
DATAFLOW = pltpu.SideEffectType.DATAFLOW_SIDE_EFFECTING
CHIP_FLIPS = CHIP_RELATIONS[1:]
TOKEN = jax.ShapeDtypeStruct((SUBLANE, LANE), F32)


def _flip(v, r):
    return v + r - 2 * v * r


def _chip_copies(src_ref, src_of, dst_ref, dst_of, send_sems, recv_sems):
    x, y, c = _position()
    me = 4 * x + 2 * y + c
    out = []
    for k, (rx, ry) in enumerate(CHIP_FLIPS):
        px, py = _flip(x, rx), _flip(y, ry)
        peer = 4 * px + 2 * py + c
        out.append(pltpu.make_async_remote_copy(
            src_ref=src_ref.at[src_of(k, me, peer)], dst_ref=dst_ref.at[dst_of(k, me, peer)],
            send_sem=send_sems.at[k], recv_sem=recv_sems.at[k], device_id=(px, py, c), device_id_type=MESH))
    return out


def _device_block(ref, spec, d):
    rows, axis = spec
    return ref.at[pl.ds(d * rows, rows)] if axis == 0 else ref.at[:, pl.ds(d * rows, rows)]


def _ag_chips_start(lands, specs, after, name):
    n = len(lands)
    nf = len(CHIP_FLIPS)

    def body(*refs):
        send_sems, recv_sems, token = refs[n + 1], refs[n + 2], refs[-1]
        x, y, c = _position()
        me = 4 * x + 2 * y + c
        for i, spec in enumerate(specs):
            blk = _device_block(refs[i], spec, me)
            for k, (rx, ry) in enumerate(CHIP_FLIPS):
                pltpu.make_async_remote_copy(
                    src_ref=blk, dst_ref=blk, send_sem=send_sems.at[nf * i + k], recv_sem=recv_sems.at[nf * i + k],
                    device_id=(_flip(x, rx), _flip(y, ry), c), device_id_type=MESH).start()
        token[...] = jnp.zeros(TOKEN.shape, TOKEN.dtype)

    sems = pltpu.SemaphoreType.DMA((nf * n,))
    return pl.pallas_call(
        body, name=name, out_shape=(sems, sems, *[pltpu.HBM(t.shape, t.dtype) for t in lands], TOKEN),
        in_specs=(HBM_SPEC,) * n + (ANY_SPEC,),
        out_specs=(SEM_SPEC, SEM_SPEC) + (HBM_SPEC,) * n + (pl.BlockSpec(memory_space=pltpu.VMEM),),
        input_output_aliases={i: 2 + i for i in range(n)}, compiler_params=pltpu.CompilerParams(has_side_effects=DATAFLOW),
    )(*[pltpu.with_memory_space_constraint(t, pltpu.HBM) for t in lands], after)


def _ag_chips_wait(send_sems, recv_sems, lands, specs, after, name):
    n = len(lands)
    nf = len(CHIP_FLIPS)

    def body(*refs):
        send_sems, recv_sems = refs[n], refs[n + 1]
        x, y, c = _position()
        me = 4 * x + 2 * y + c
        for i, spec in enumerate(specs):
            for k, (rx, ry) in enumerate(CHIP_FLIPS):
                px, py = _flip(x, rx), _flip(y, ry)
                cp = pltpu.make_async_remote_copy(
                    src_ref=_device_block(refs[i], spec, me), dst_ref=_device_block(refs[i], spec, 4 * px + 2 * py + c),
                    send_sem=send_sems.at[nf * i + k], recv_sem=recv_sems.at[nf * i + k],
                    device_id=(px, py, c), device_id_type=MESH)
                cp.wait_send()
                cp.wait_recv()

    return pl.pallas_call(
        body, name=name, out_shape=tuple(pltpu.HBM(t.shape, t.dtype) for t in lands),
        in_specs=(HBM_SPEC,) * n + (SEM_SPEC, SEM_SPEC, ANY_SPEC), out_specs=(HBM_SPEC,) * n,
        input_output_aliases={i: i for i in range(n)}, compiler_params=pltpu.CompilerParams(has_side_effects=DATAFLOW),
    )(*lands, send_sems, recv_sems, after)


def _ag_pair_forward(lands, specs, name):
    n = len(lands)
    nr = len(CHIP_RELATIONS)

    def body(*refs):
        outs, send_sems, recv_sems = refs[n:2 * n], refs[2 * n], refs[2 * n + 1]
        x, y, c = _position()
        copies = []
        for i, spec in enumerate(specs):
            for k, (rx, ry) in enumerate(CHIP_RELATIONS):
                chip = 4 * _flip(x, rx) + 2 * _flip(y, ry)
                held = _device_block(outs[i], spec, chip + c)
                sems = dict(send_sem=send_sems.at[nr * i + k], recv_sem=recv_sems.at[nr * i + k],
                            device_id=(x, y, 1 - c), device_id_type=MESH)
                mine = pltpu.make_async_remote_copy(src_ref=held, dst_ref=held, **sems)
                theirs = pltpu.make_async_remote_copy(src_ref=held, dst_ref=_device_block(outs[i], spec, chip + 1 - c), **sems)
                copies.append((mine, theirs))
        for mine, _ in copies:
            mine.start()
        for mine, theirs in copies:
            mine.wait_send()
            theirs.wait_recv()

    sems = pltpu.SemaphoreType.DMA((nr * n,))
    return pl.pallas_call(
        body, out_shape=tuple(jax.ShapeDtypeStruct(t.shape, t.dtype) for t in lands), in_specs=[HBM_SPEC] * n,
        out_specs=(HBM_SPEC,) * n, input_output_aliases={i: i for i in range(n)}, scratch_shapes=[sems, sems], name=name)(*lands)


def _rs_chips_start(pair, name):
    _, r_dim, c_dim = pair.shape
    n = len(CHIP_FLIPS)

    def body(pair_ref, far_ref, send_sems, recv_sems, pair_thru, far_thru, token):
        for cp in _chip_copies(pair_ref, lambda k, me, peer: k + 1, far_ref, lambda k, me, peer: k, send_sems, recv_sems):
            cp.start()
        token[...] = jnp.zeros(TOKEN.shape, TOKEN.dtype)

    far = lax.empty((n, r_dim, c_dim), pair.dtype)
    return pl.pallas_call(
        body, name=name,
        out_shape=(pltpu.SemaphoreType.DMA((n,)), pltpu.SemaphoreType.DMA((n,)), pltpu.HBM(pair.shape, pair.dtype),
                   pltpu.HBM(far.shape, far.dtype), TOKEN),
        in_specs=(HBM_SPEC, HBM_SPEC),
        out_specs=(SEM_SPEC, SEM_SPEC, HBM_SPEC, HBM_SPEC, pl.BlockSpec(memory_space=pltpu.VMEM)),
        input_output_aliases={0: 2, 1: 3}, compiler_params=pltpu.CompilerParams(has_side_effects=DATAFLOW),
    )(pltpu.with_memory_space_constraint(pair, pltpu.HBM), pltpu.with_memory_space_constraint(far, pltpu.HBM))


def _rs_chips_wait(send_sems, recv_sems, pair, far, after, name):
    def body(pair_ref, far_ref, send_sems, recv_sems, after_ref, pair_out, far_out):
        for cp in _chip_copies(pair_ref, lambda k, me, peer: k + 1, far_ref, lambda k, me, peer: k, send_sems, recv_sems):
            cp.wait_send()
            cp.wait_recv()

    return pl.pallas_call(
        body, name=name, out_shape=(pltpu.HBM(pair.shape, pair.dtype), pltpu.HBM(far.shape, far.dtype)),
        in_specs=(HBM_SPEC, HBM_SPEC, SEM_SPEC, SEM_SPEC, ANY_SPEC),
        out_specs=(HBM_SPEC, HBM_SPEC), input_output_aliases={0: 0, 1: 1},
        compiler_params=pltpu.CompilerParams(has_side_effects=DATAFLOW),
    )(pair, far, send_sems, recv_sems, after)


def _rs_final_sum(pair, far, name):
    _, r_dim, c_dim = pair.shape
    tr = _tile(r_dim, 1200, SUBLANE)

    def body(p_ref, f0_ref, f1_ref, f2_ref, o_ref):
        o_ref[...] = ((p_ref[...].astype(F32) + f0_ref[...].astype(F32)) + f1_ref[...].astype(F32)) + f2_ref[...].astype(F32)

    def slot(k):
        return _bs((None, tr, c_dim), lambda i: (k, i, 0))

    return pl.pallas_call(
        body, out_shape=jax.ShapeDtypeStruct((r_dim, c_dim), F32), grid=(r_dim // tr,),
        in_specs=[slot(0), slot(0), slot(1), slot(2)], out_specs=_bs((tr, c_dim), lambda i: (i, 0)), name=name,
        compiler_params=_params("parallel"))(pair, far, far, far)


def _reduce_scatter_begin(g, name):
    recv = _rs_pair_exchange(g, name + "_pair")
    pair = _rs_pair_sum(g, recv, name + "_pairsum")
    return _rs_chips_start(pair, name + "_chips_start")


def _reduce_scatter_end(state, after, name):
    send_sems, recv_sems, pair, far, _ = state
    pair, far = _rs_chips_wait(send_sems, recv_sems, pair, far, after, name + "_chips_wait")
    return _rs_final_sum(pair, far, name + "_sum")


MATRICES = (("w_in", True), ("w_out", False), ("w_q", False), ("w_kv", True), ("w_o", False), ("w_up", True),
            ("w_down", False), ("w_conv_out", True), ("w_pool_grp", True))
MIX_NAMES = ("w_in", "w_conv_out", "w_pool_grp", "w_out")
REST_NAMES = ("w_q", "w_kv", "w_o", "w_up", "w_down")


def _parts(layer):
    return (("mix", MIX_NAMES), ("rest", REST_NAMES)) if layer == 0 else (("all", MIX_NAMES + REST_NAMES),)


def _to_rows(name, transposed, w, d_model):
    if name == "w_pool_grp":
        w = jnp.swapaxes(w, 1, 2)
    elif transposed:
        w = w.T
    return w.reshape(-1, d_model)


def _from_rows(name, transposed, rows, shard_shape):
    if name == "w_pool_grp":
        g, i, o = shard_shape
        return jnp.swapaxes(rows.reshape(g, o, i), 1, 2)
    if transposed:
        return rows.reshape(shard_shape[1], shard_shape[0]).T
    return rows.reshape(shard_shape)


def _scatter_blocks(name, full, shard_shape, d_model, n_dev=N_DEV):
    if name == "w_pool_grp":
        g, i, o = shard_shape
        return jnp.swapaxes(full.reshape(g, n_dev, o, i), 0, 1).reshape(n_dev, -1, d_model)
    return full.reshape(n_dev, -1, d_model)


def kernel(x, mem, mix_norm_g, w_in, conv_dw_w, conv_dw_b, conv_ln_g, conv_ln_b, w_conv_out, w_pool_grp, pool_scale, w_out, xattn_norm_g, mem_norm_g, w_q, w_kv, w_o, ffn_norm_g, w_up, ffn_dw_w, w_down, final_norm_g, loss_target, m_mix_norm_g, m_w_in, m_conv_dw_w, m_conv_dw_b, m_conv_ln_g, m_conv_ln_b, m_w_conv_out, m_w_pool_grp, m_pool_scale, m_w_out, m_xattn_norm_g, m_mem_norm_g, m_w_q, m_w_kv, m_w_o, m_ffn_norm_g, m_w_up, m_ffn_dw_w, m_w_down, m_final_norm_g, v_mix_norm_g, v_w_in, v_conv_dw_w, v_conv_dw_b, v_conv_ln_g, v_conv_ln_b, v_w_conv_out, v_w_pool_grp, v_pool_scale, v_w_out, v_xattn_norm_g, v_mem_norm_g, v_w_q, v_w_kv, v_w_o, v_ffn_norm_g, v_w_up, v_ffn_dw_w, v_w_down, v_final_norm_g):
    p = dict(locals())
    weight_names = ["mix_norm_g", "w_in", "conv_dw_w", "conv_dw_b", "conv_ln_g", "conv_ln_b", "w_conv_out",
                    "w_pool_grp", "pool_scale", "w_out", "xattn_norm_g", "mem_norm_g", "w_q", "w_kv", "w_o",
                    "ffn_norm_g", "w_up", "ffn_dw_w", "w_down", "final_norm_g"]
    n_batch, seq, d_model = x.shape
    m_len = mem.shape[1]
    depth = w_in.shape[0]
    assert depth == 2, "the exchange schedule below is written for two layers"
    t_dim = n_batch * seq
    c_conv = conv_dw_b.shape[1]
    n_groups = w_pool_grp.shape[1]
    assert w_pool_grp.shape[2] == LANE and c_conv % LANE == 0 and n_groups * LANE == c_conv
    gate_col0 = 2 * c_conv + n_groups * LANE
    pool_col0 = (2 * c_conv) // LANE

    dev = 4 * lax.axis_index("x") + 2 * lax.axis_index("y") + lax.axis_index("c")
    filt = jnp.concatenate([conv_dw_w.reshape(-1), ffn_dw_w.reshape(-1)])
    filt_rows = lax.bitcast_convert_type(filt, BF16).reshape(-1, d_model)
    transposed = dict(MATRICES)
    layout = {part: [(name, transposed[name], _to_rows(name, transposed[name], p[name][0], d_model).shape[0])
                     for name in names] for l in range(depth) for part, names in _parts(l)}
    part_of = {(l, name): part for l in range(depth) for part, names in _parts(l) for name in names}

    def landing(name, shard):
        if name == "w_pool_grp":
            block, axis = jnp.swapaxes(shard, 1, 2), 1
        elif name == "filt":
            block, axis = shard, 0
        else:
            block, axis = (shard.T if transposed[name] else shard), 0
        block = block.astype(BF16)
        rows = block.shape[axis]
        shape = block.shape[:axis] + (N_DEV * rows,) + block.shape[axis + 1:]
        start = (0,) * axis + (dev * rows,) + (0,) * (block.ndim - axis - 1)
        return lax.dynamic_update_slice(lax.empty(shape, BF16), block, start), (rows, axis)

    ag_state = {}
    after = filt_rows
    for l in range(depth):
        for part, names in _parts(l):
            items = [(name, p[name][l]) for name in names]
            if (l, part) == (0, part_of[(0, "w_in")]):
                items.append(("filt", filt_rows))
            after, shards = lax.optimization_barrier((after, [shard for _, shard in items]))
            items = [(name, shard) for (name, _), shard in zip(items, shards)]
            lands, specs = zip(*[landing(name, shard) for name, shard in items])
            out = _ag_chips_start(lands, specs, after, f"ag{l}{part}_chips_start")
            ag_state[(l, part)] = ([name for name, _ in items], specs, out)
            after = out[-1]
    all_started = after

    full = [dict() for _ in range(depth)]

    def ensure(l, name, after):
        if name in full[l]:
            return
        part = part_of[(l, name)]
        names, specs, out = ag_state[(l, part)]
        lands = _ag_chips_wait(out[0], out[1], out[2:-1], specs, after, f"ag{l}{part}_chips_wait")
        lands = _ag_pair_forward(lands, specs, f"ag{l}{part}_pair_forward")
        full[l].update(zip(names, lands))

    vec = lambda a: a.reshape(1, -1)
    x2d = x.reshape(t_dim, d_model)
    mem2d = mem.reshape(n_batch * m_len, d_model)
    mem_n = _rmsnorm_fwd(mem2d, vec(mem_norm_g), "mem_norm", after=all_started)
    h_first = _rmsnorm_fwd(x2d, vec(mix_norm_g[0]), "mix_norm_l0", after=mem_n)
    ensure(0, "w_in", h_first)
    filt_all = lax.bitcast_convert_type(full[0]["filt"].reshape(N_DEV, -1, 2), F32)
    n_cw = conv_dw_w.size
    kc, cs = conv_dw_w.shape[1:]
    kf, fs = ffn_dw_w.shape[1:]
    conv_w_full = jnp.moveaxis(filt_all[:, :n_cw].reshape(N_DEV, depth, kc, cs), 0, 2).reshape(depth, kc, N_DEV * cs)
    ffn_w_full = jnp.moveaxis(filt_all[:, n_cw:].reshape(N_DEV, depth, kf, fs), 0, 2).reshape(depth, kf, N_DEV * fs)

    saved = []
    xc = x2d
    for l in range(depth):
        ensure(l, "w_in", xc)
        wl = full[l]
        s = {"x0": xc}
        s["h"] = h_first if l == 0 else _rmsnorm_fwd(xc, vec(mix_norm_g[l]), f"mix_norm_l{l}")
        s["proj"] = _matmul(s["h"], wl["w_in"], "nt", f"in_proj_l{l}", out_dtype=BF16)
        s["y1"] = _glu_conv_fwd(s["proj"], conv_w_full[l], vec(conv_dw_b[l]), n_batch, seq, f"glu_conv_l{l}")
        s["y3"] = _ln_silu_fwd(s["y1"], vec(conv_ln_g[l]), vec(conv_ln_b[l]), f"ln_silu_l{l}")
        s["yc"] = _matmul(s["y3"], wl["w_conv_out"], "nt", f"conv_out_l{l}", out_dtype=BF16)
        s["zp"] = _pool_fwd(s["proj"], pool_col0, n_groups, n_batch, seq, f"pool_l{l}")
        s["yp"] = _grouped(s["zp"], wl["w_pool_grp"], "nt", f"pool_proj_l{l}", out_dtype=BF16)
        s["merged"] = _merge_fwd(s["proj"], gate_col0, s["yc"], s["yp"], vec(pool_scale[l]), f"merge_l{l}")
        s["x1"] = _matmul(s["merged"], wl["w_out"], "nn", f"mix_out_l{l}", res=xc)
        ensure(l, "w_q", s["x1"])
        half_up = wl["w_up"].shape[0] // 2
        up_gate, up_val = (0, half_up), (half_up, half_up)
        s["hq"] = _rmsnorm_fwd(s["x1"], vec(xattn_norm_g[l]), f"xattn_norm_l{l}")
        s["q"] = _matmul(s["hq"], wl["w_q"], "nn", f"q_proj_l{l}", out_dtype=BF16)
        s["kv"] = _matmul(mem_n, wl["w_kv"], "nt", f"kv_proj_l{l}", out_dtype=BF16)
        s["att"] = _attn_fwd(s["q"], s["kv"], n_batch, seq, m_len, f"attn_l{l}")
        s["x2"] = _matmul(s["att"], wl["w_o"], "nn", f"attn_out_l{l}", res=s["x1"])
        s["hf"] = _rmsnorm_fwd(s["x2"], vec(ffn_norm_g[l]), f"ffn_norm_l{l}")
        s["up_g"] = _matmul(s["hf"], wl["w_up"], "nt", f"up_proj_gate_l{l}", out_dtype=BF16, b_window=up_gate)
        s["up_v"] = _matmul(s["hf"], wl["w_up"], "nt", f"up_proj_val_l{l}", out_dtype=BF16, b_window=up_val)
        s["act"] = _ffn_act_fwd(s["up_g"], s["up_v"], ffn_w_full[l], n_batch, seq, f"ffn_act_l{l}")
        xc = _matmul(s["act"], wl["w_down"], "nn", f"down_proj_l{l}", res=s["x2"])
        saved.append(s)

    dx, dxb, dg_final, loss_part = _loss_head(xc, vec(final_norm_g), loss_target.reshape(t_dim, d_model), "loss_head")

    small = {"final_norm_g": dg_final.reshape(-1)}
    big = [dict() for _ in range(depth)]
    rs_state = {}
    rs_after = loss_part

    def rs_begin(l, part):
        pack = lax.empty((N_DEV, sum(nrows for _, _, nrows in layout[part]), d_model), BF16)
        row0 = 0
        for name, _, nrows in layout[part]:
            pieces = big[l][name] if isinstance(big[l][name], tuple) else (big[l][name],)
            d0 = 0
            for piece in pieces:
                blocks = _scatter_blocks(name, piece, p[name].shape[1:], d_model, N_DEV // len(pieces)).astype(BF16)
                pack = lax.dynamic_update_slice(pack, blocks, (d0, row0, 0))
                d0 += blocks.shape[0]
            row0 += nrows
        rs_state[(l, part)] = _reduce_scatter_begin(pack, f"rs{l}{part}")
        return rs_state[(l, part)][4]

    dmem_n = None
    for l in reversed(range(depth)):
        wl, s = full[l], saved[l]
        sm = {}
        dact = _matmul(dxb, wl["w_down"], "nt", f"d_act_l{l}", out_dtype=BF16, after=rs_after)
        big[l]["w_down"] = _matmul(s["act"], dxb, "tn", f"d_w_down_l{l}", out_dtype=BF16)
        dup_g, dup_v, dwf_g, dwf_v = _ffn_act_bwd(s["up_g"], s["up_v"], ffn_w_full[l], dact, n_batch, seq,
                                                  f"ffn_act_bwd_l{l}")
        sm["ffn_dw_w"] = jnp.concatenate([dwf_g, dwf_v], axis=1)
        dx, dxb, dg = _matmul_rmsnorm_bwd((dup_g, dup_v), wl["w_up"], "nn", s["x2"], vec(ffn_norm_g[l]), dx,
                                          f"d_hf_ffn_norm_bwd_l{l}")
        big[l]["w_up"] = (_matmul(dup_g, s["hf"], "tn", f"d_w_up_gate_l{l}", out_dtype=BF16),
                          _matmul(dup_v, s["hf"], "tn", f"d_w_up_val_l{l}", out_dtype=BF16))
        sm["ffn_norm_g"] = dg
        datt = _matmul(dxb, wl["w_o"], "nt", f"d_att_l{l}", out_dtype=BF16, after=rs_after)
        big[l]["w_o"] = _matmul(s["att"], dxb, "tn", f"d_w_o_l{l}", out_dtype=BF16)
        dq, dk, dv = _attn_bwd(s["q"], s["kv"], datt, n_batch, seq, m_len, f"attn_bwd_l{l}")
        dkv = jnp.concatenate([dk, dv], axis=1)
        big[l]["w_kv"] = _matmul(dkv, mem_n, "tn", f"d_w_kv_l{l}", out_dtype=BF16)
        dmem_n = _matmul(dkv, wl["w_kv"], "nn", f"d_mem_l{l}", res=dmem_n)
        big[l]["w_q"] = _matmul(s["hq"], dq, "tn", f"d_w_q_l{l}", out_dtype=BF16)
        dx, dxb, dg = _matmul_rmsnorm_bwd(dq, wl["w_q"], "nt", s["x1"], vec(xattn_norm_g[l]), dx,
                                          f"d_hq_xattn_norm_bwd_l{l}")
        sm["xattn_norm_g"] = dg
        if part_of[(l, "w_q")] != part_of[(l, "w_in")]:
            rs_after = rs_begin(l, part_of[(l, "w_q")])
        dmerged = _matmul(dxb, wl["w_out"], "nt", f"d_merged_l{l}", out_dtype=BF16, after=rs_after)
        big[l]["w_out"] = _matmul(s["merged"], dxb, "tn", f"d_w_out_l{l}", out_dtype=BF16)
        dgc, dgp, dyc, dyp, dscale = _merge_bwd(s["proj"], gate_col0, s["yc"], s["yp"], vec(pool_scale[l]), dmerged,
                                                f"merge_bwd_l{l}")
        sm["pool_scale"] = dscale
        dzp = _grouped(dyp, wl["w_pool_grp"], "nn", f"d_zp_l{l}", out_dtype=BF16)
        big[l]["w_pool_grp"] = _grouped_tn(dyp, s["zp"], n_groups, f"d_w_pool_l{l}")
        du = _pool_bwd(dzp, n_groups, n_batch, seq, f"pool_bwd_l{l}")
        dy3 = _matmul(dyc, wl["w_conv_out"], "nn", f"d_y3_l{l}", out_dtype=BF16)
        big[l]["w_conv_out"] = _matmul(dyc, s["y3"], "tn", f"d_w_conv_out_l{l}", out_dtype=BF16)
        dy1, dlg, dlb = _ln_silu_bwd(s["y1"], vec(conv_ln_g[l]), vec(conv_ln_b[l]), dy3, f"ln_silu_bwd_l{l}")
        sm["conv_ln_g"], sm["conv_ln_b"] = dlg, dlb
        da, dgl, dcw, dcb = _glu_conv_bwd(s["proj"], conv_w_full[l], dy1, n_batch, seq, f"glu_conv_bwd_l{l}")
        sm["conv_dw_w"], sm["conv_dw_b"] = dcw, dcb
        dproj = jnp.concatenate([da, dgl, du, dgc, dgp], axis=1)
        big[l]["w_in"] = _matmul(dproj, s["h"], "tn", f"d_w_in_l{l}", out_dtype=BF16)
        dx, dxb, dg = _matmul_rmsnorm_bwd(dproj, wl["w_in"], "nn", s["x0"], vec(mix_norm_g[l]), dx,
                                          f"d_h_mix_norm_bwd_l{l}")
        sm["mix_norm_g"] = dg
        for k, val in sm.items():
            small[(l, k)] = val.reshape(-1)
        rs_after = rs_begin(l, part_of[(l, "w_in")])
    _, _, dg_mem = _rmsnorm_bwd(mem2d, vec(mem_norm_g), dmem_n, None, "mem_norm_bwd")
    small["mem_norm_g"] = dg_mem.reshape(-1)
    small["loss"] = loss_part.reshape(-1)

    grads = {}
    per_layer = {name: [None] * depth for name, _ in MATRICES}
    for l in reversed(range(depth)):
        for part, _ in reversed(_parts(l)):
            mat_grads = _reduce_scatter_end(rs_state[(l, part)], rs_after, f"rs{l}{part}")
            row0 = 0
            for name, tr, nrows in layout[part]:
                per_layer[name][l] = _from_rows(name, tr, mat_grads[row0:row0 + nrows], p[name].shape[1:])
                row0 += nrows
    for name, _ in MATRICES:
        grads[name] = jnp.stack(per_layer[name])

    keys = list(small.keys())
    flat = jnp.concatenate([small[k] for k in keys])
    n_small = flat.shape[0]
    rows_small = -(-n_small // (SUBLANE * d_model)) * SUBLANE
    flat = jnp.pad(flat, (0, rows_small * d_model - n_small)).reshape(rows_small, d_model)
    every = _all_gather(flat, "small_all_gather")
    total = _sum_rows([every[i] for i in range(N_DEV)], F32, "small_sum").reshape(-1)
    off = 0
    red = {}
    for k in keys:
        red[k] = total[off:off + small[k].shape[0]]
        off += small[k].shape[0]
    loss = red["loss"][0]
    for name in ("mix_norm_g", "conv_dw_b", "conv_ln_g", "conv_ln_b", "pool_scale", "xattn_norm_g", "ffn_norm_g"):
        grads[name] = jnp.stack([red[(l, name)] for l in range(depth)])
    grads["conv_dw_w"] = jnp.stack([
        lax.dynamic_slice_in_dim(red[(l, "conv_dw_w")].reshape(kc, N_DEV * cs), dev * cs, cs, axis=1)
        for l in range(depth)])
    grads["ffn_dw_w"] = jnp.stack([
        lax.dynamic_slice_in_dim(red[(l, "ffn_dw_w")].reshape(kf, N_DEV * fs), dev * fs, fs, axis=1)
        for l in range(depth)])
    grads["mem_norm_g"] = red["mem_norm_g"]
    grads["final_norm_g"] = red["final_norm_g"]

    deltas, new_m, new_v = {}, {}, {}
    for name in weight_names:
        deltas[name], new_m[name], new_v[name] = _adamw(p[name], grads[name], p["m_" + name], p["v_" + name],
                                                        f"adamw_{name}")
    grad_x = dx.reshape(n_batch, seq, d_model)
    return (loss, grad_x, *[grads[n] for n in weight_names], *[deltas[n] for n in weight_names],
            *[new_m[n] for n in weight_names], *[new_v[n] for n in weight_names])
```

```python
import functools

import jax
import jax.numpy as jnp
from jax import lax
from jax.experimental import pallas as pl
from jax.experimental.pallas import tpu as pltpu

F32 = jnp.float32
BF16 = jnp.bfloat16
MESH = pl.DeviceIdType.MESH

N_DEV = 8
EPS = 1e-6
V7X_VMEM_BYTES = 64 * 1024 * 1024
VMEM_LIMIT = (V7X_VMEM_BYTES * 3) // 4
LANE = 128
SUBLANE = 8

CONV_HALO = 32
POOL_HALO = 16
FFN_HALO = 8
POOL_WINDOW_MAX = 16
XA_HEADS = 4

ADAM_LR = 0.001
ADAM_B1 = 0.9
ADAM_B2 = 0.999
ADAM_EPS = 1e-08
ADAM_WD = 0.01
ADAM_STEP = 10

GELU_C0 = 0.7978845608028654
GELU_C1 = 0.044715


ANY_SPEC = pl.BlockSpec(memory_space=pl.ANY)


def _tile(n, cap, mult=LANE):
    if n <= cap:
        return n
    best = None
    for d in range(mult, cap + 1, mult):
        if n % d == 0:
            best = d
    assert best is not None, (n, cap, mult)
    return best


def _params(*sem):
    return pltpu.CompilerParams(dimension_semantics=sem, vmem_limit_bytes=VMEM_LIMIT)


def _delayed(x, halo, rows, n_shifts):
    for r in range(min(SUBLANE, n_shifts)):
        xr = x if r == 0 else pltpu.roll(x, r, 0)
        for s in range(r, n_shifts, SUBLANE):
            yield s, xr[halo - (s - r):halo - (s - r) + rows]


def _advanced(x, rows, n_shifts):
    for r in range(min(SUBLANE, n_shifts)):
        xr = x if r == 0 else pltpu.roll(x, x.shape[0] - r, 0)
        for s in range(r, n_shifts, SUBLANE):
            yield s, xr[s - r:s - r + rows]


def _sig(x):
    return 1.0 / (1.0 + jnp.exp(-x))


def _bs(shape, imap):
    return pl.BlockSpec(shape, imap)


def _mxu_tile(n, cap):
    if n <= cap:
        return n
    best = {mult: max((d for d in range(mult, cap + 1, mult) if n % d == 0), default=0) for mult in (2 * LANE, LANE)}
    assert best[LANE] > 0, (n, cap)
    return best[2 * LANE] if 2 * best[2 * LANE] >= best[LANE] else best[LANE]


def _matmul(a, b, mode, name, res=None, out_dtype=F32, after=None, b_window=None):
    b_row0, b_rows = b_window if b_window is not None else (0, b.shape[0])
    if mode == "tn":
        k_dim, m_dim = a.shape
        k2, n_dim = b_rows, b.shape[1]
    elif mode == "nn":
        m_dim, k_dim = a.shape
        k2, n_dim = b_rows, b.shape[1]
    else:
        m_dim, k_dim = a.shape
        n_dim, k2 = b_rows, b.shape[1]
    assert k_dim == k2, (name, a.shape, b.shape)
    size = lambda t: jnp.dtype(t).itemsize
    tm = _mxu_tile(m_dim, 2816 if mode == "tn" else 1024)
    tn = _mxu_tile(n_dim, 2816)
    fixed = tm * tn * (2 * size(out_dtype) + (2 * size(res.dtype) if res is not None else 0) + 4)
    for cap in (2816, 2048, 1792, 1024, 512):
        tk = _mxu_tile(k_dim, cap)
        if fixed + 2 * tk * (tm * size(a.dtype) + tn * size(b.dtype)) <= VMEM_LIMIT - 8 * 1024 * 1024:
            break
    nk = k_dim // tk
    use_acc = nk > 1 and out_dtype != F32
    if mode == "tn":
        a_spec, ca = _bs((tk, tm), lambda i, j, k: (k, i)), 0
    else:
        a_spec, ca = _bs((tm, tk), lambda i, j, k: (i, k)), 1
    if mode == "nt":
        assert b_row0 % tn == 0
        b_spec, cb = _bs((tn, tk), lambda i, j, k: (j + b_row0 // tn, k)), 1
    else:
        assert b_row0 % tk == 0
        b_spec, cb = _bs((tk, tn), lambda i, j, k: (k + b_row0 // tk, j)), 0
    dims = (((ca,), (cb,)), ((), ()))
    o_spec = _bs((tm, tn), lambda i, j, k: (i, j))
    has_res = res is not None

    def body(*refs):
        a_ref, b_ref = refs[:2]
        r_ref = refs[2] if has_res else None
        o_ref = refs[n_in]
        k = pl.program_id(2)
        part = lax.dot_general(a_ref[...].astype(BF16), b_ref[...].astype(BF16), dims,
                               preferred_element_type=F32)
        if nk == 1:
            if has_res:
                part = part + r_ref[...].astype(F32)
            o_ref[...] = part.astype(out_dtype)
            return
        acc = refs[-1] if use_acc else o_ref

        @pl.when(k == 0)
        def _():
            acc[...] = part + r_ref[...].astype(F32) if has_res else part

        @pl.when(k > 0)
        def _():
            acc[...] += part

        if use_acc:
            @pl.when(k == nk - 1)
            def _():
                o_ref[...] = acc[...].astype(out_dtype)

    in_specs = [a_spec, b_spec] + ([o_spec] if has_res else [])
    args = (a, b) + ((res,) if has_res else ())
    if after is not None:
        in_specs.append(ANY_SPEC)
        args += (after,)
    n_in = len(args)
    return pl.pallas_call(
        body, out_shape=jax.ShapeDtypeStruct((m_dim, n_dim), out_dtype),
        grid=(m_dim // tm, n_dim // tn, nk), in_specs=in_specs, out_specs=o_spec,
        scratch_shapes=[pltpu.VMEM((tm, tn), F32)] if use_acc else [], name=name,
        compiler_params=_params("parallel", "parallel", "arbitrary"))(*args)


def _grouped(a, w, mode, name, out_dtype=F32):
    t_dim = a.shape[0]
    g_dim, r_dim, c_dim = w.shape
    ka, no = (c_dim, r_dim) if mode == "nt" else (r_dim, c_dim)
    tm = _tile(t_dim, 2048)
    dims = (((1,), (1 if mode == "nt" else 0,)), ((), ()))

    def body(a_ref, w_ref, o_ref):
        o_ref[...] = lax.dot_general(a_ref[...].astype(BF16), w_ref[...].astype(BF16), dims,
                                     preferred_element_type=F32).astype(out_dtype)

    return pl.pallas_call(
        body, out_shape=jax.ShapeDtypeStruct((t_dim, g_dim * no), out_dtype),
        grid=(t_dim // tm, g_dim),
        in_specs=[_bs((tm, ka), lambda i, g: (i, g)), _bs((None, r_dim, c_dim), lambda i, g: (g, 0, 0))],
        out_specs=_bs((tm, no), lambda i, g: (i, g)), name=name,
        compiler_params=_params("parallel", "parallel"))(a, w)


def _grouped_tn(a, b, g_dim, name):
    t_dim = a.shape[0]
    ra = a.shape[1] // g_dim
    cb = b.shape[1] // g_dim
    tm = _tile(t_dim, 2048)
    nt = t_dim // tm

    def body(a_ref, b_ref, o_ref):
        part = lax.dot_general(a_ref[...].astype(BF16), b_ref[...].astype(BF16), (((0,), (0,)), ((), ())),
                               preferred_element_type=F32)

        @pl.when(pl.program_id(1) == 0)
        def _():
            o_ref[...] = part

        @pl.when(pl.program_id(1) > 0)
        def _():
            o_ref[...] += part

    return pl.pallas_call(
        body, out_shape=jax.ShapeDtypeStruct((g_dim, ra, cb), F32), grid=(g_dim, nt),
        in_specs=[_bs((tm, ra), lambda g, i: (i, g)), _bs((tm, cb), lambda g, i: (i, g))],
        out_specs=_bs((None, ra, cb), lambda g, i: (g, 0, 0)), name=name,
        compiler_params=_params("parallel", "arbitrary"))(a, b)


def _rmsnorm_fwd(x, g, name, after=None):
    t_dim, d = x.shape
    tm = _tile(t_dim, 512)

    def body(x_ref, g_ref, *rest):
        o_ref = rest[-1]
        xv = x_ref[...]
        r = lax.rsqrt(jnp.mean(xv * xv, axis=-1, keepdims=True) + EPS)
        o_ref[...] = (xv * r * g_ref[...]).astype(BF16)

    return pl.pallas_call(
        body, out_shape=jax.ShapeDtypeStruct((t_dim, d), BF16), grid=(t_dim // tm,),
        in_specs=[_bs((tm, d), lambda i: (i, 0)), _bs((1, d), lambda i: (0, 0))] + ([ANY_SPEC] if after is not None else []),
        out_specs=_bs((tm, d), lambda i: (i, 0)), name=name,
        compiler_params=_params("parallel"))(x, g, *([after] if after is not None else []))


def _rmsnorm_bwd(x, g, dh, dx_in, name):
    t_dim, d = x.shape
    tm = _tile(t_dim, 512)
    has_in = dx_in is not None

    def body(*refs):
        if has_in:
            x_ref, g_ref, dh_ref, di_ref, dx_ref, dxb_ref, dg_ref = refs
        else:
            x_ref, g_ref, dh_ref, dx_ref, dxb_ref, dg_ref = refs
        xv = x_ref[...]
        r = lax.rsqrt(jnp.mean(xv * xv, axis=-1, keepdims=True) + EPS)
        xh = xv * r
        dhv = dh_ref[...].astype(F32)
        dxh = dhv * g_ref[...]
        dx = r * (dxh - xh * jnp.mean(dxh * xh, axis=-1, keepdims=True))
        if has_in:
            dx = dx + di_ref[...]
        dx_ref[...] = dx
        dxb_ref[...] = dx.astype(BF16)
        part = jnp.sum(dhv * xh, axis=0, keepdims=True)

        @pl.when(pl.program_id(0) == 0)
        def _():
            dg_ref[...] = part

        @pl.when(pl.program_id(0) > 0)
        def _():
            dg_ref[...] += part

    row = _bs((tm, d), lambda i: (i, 0))
    vec = _bs((1, d), lambda i: (0, 0))
    args = (x, g, dh) + ((dx_in,) if has_in else ())
    return pl.pallas_call(
        body, out_shape=(jax.ShapeDtypeStruct((t_dim, d), F32), jax.ShapeDtypeStruct((t_dim, d), BF16),
                         jax.ShapeDtypeStruct((1, d), F32)),
        grid=(t_dim // tm,), in_specs=[row, vec, row] + ([row] if has_in else []),
        out_specs=(row, row, vec), name=name, compiler_params=_params("arbitrary"))(*args)


def _matmul_rmsnorm_bwd(a, b, mode, x, g, dx_in, name, res=None, b_window=None):
    pieces = a if isinstance(a, tuple) else (a,)
    n_p = len(pieces)
    b_row0, b_rows = b_window if b_window is not None else (0, b.shape[0])
    m_dim, k_piece = pieces[0].shape
    assert all(t.shape == pieces[0].shape for t in pieces)
    k_dim = n_p * k_piece
    d = x.shape[1]
    assert (b_rows, b.shape[1]) == ((k_dim, d) if mode == "nn" else (d, k_dim)), (name, pieces[0].shape, b.shape)
    tm = _mxu_tile(m_dim, 512)
    tk = _mxu_tile(k_piece, 1792)
    nkp = k_piece // tk
    nk = n_p * nkp
    has_res = res is not None
    if mode == "nt":
        assert b_row0 == 0
        b_spec, cb = _bs((d, tk), lambda i, k: (0, k)), 1
    else:
        assert b_row0 % tk == 0
        b_spec, cb = _bs((tk, d), lambda i, k: (k + b_row0 // tk, 0)), 0
    dims = (((1,), (cb,)), ((), ()))

    def body(*refs):
        b_ref = refs[n_p]
        r_ref = refs[n_p + 1] if has_res else None
        x_ref, g_ref, di_ref, dx_ref, dxb_ref, dg_ref = refs[n_p + 1 + has_res:n_p + 7 + has_res]
        i, k = pl.program_id(0), pl.program_id(1)

        def finish(dhv):
            if has_res:
                dhv = dhv + r_ref[...].astype(F32)
            xv = x_ref[...]
            r = lax.rsqrt(jnp.mean(xv * xv, axis=-1, keepdims=True) + EPS)
            xh = xv * r
            dxh = dhv * g_ref[...]
            dx = r * (dxh - xh * jnp.mean(dxh * xh, axis=-1, keepdims=True)) + di_ref[...]
            dx_ref[...] = dx
            dxb_ref[...] = dx.astype(BF16)
            dg_part = jnp.sum(dhv * xh, axis=0, keepdims=True)

            @pl.when(i == 0)
            def _():
                dg_ref[...] = dg_part

            @pl.when(i > 0)
            def _():
                dg_ref[...] += dg_part

        def step(a_ref):
            part = lax.dot_general(a_ref[...].astype(BF16), b_ref[...].astype(BF16), dims, preferred_element_type=F32)
            if nk == 1:
                finish(part)
                return
            acc = refs[-1]

            @pl.when(k == 0)
            def _():
                acc[...] = part

            @pl.when(jnp.logical_and(k > 0, k < nk - 1))
            def _():
                acc[...] += part

            @pl.when(k == nk - 1)
            def _():
                finish(acc[...] + part)

        if n_p == 1:
            step(refs[0])
        else:
            for q in range(n_p):
                pl.when(jnp.logical_and(k >= q * nkp, k < (q + 1) * nkp))(functools.partial(step, refs[q]))

    row = _bs((tm, d), lambda i, k: (i, 0))
    vec = _bs((1, d), lambda i, k: (0, 0))
    a_specs = [_bs((tm, tk), lambda i, k, q=q: (i, jnp.clip(k - q * nkp, 0, nkp - 1))) for q in range(n_p)]
    in_specs = a_specs + [b_spec] + ([row] if has_res else []) + [row, vec, row]
    args = pieces + (b,) + ((res,) if has_res else ()) + (x, g, dx_in)
    return pl.pallas_call(
        body, out_shape=(jax.ShapeDtypeStruct((m_dim, d), F32), jax.ShapeDtypeStruct((m_dim, d), BF16),
                         jax.ShapeDtypeStruct((1, d), F32)),
        grid=(m_dim // tm, nk), in_specs=in_specs, out_specs=(row, row, vec),
        scratch_shapes=[pltpu.VMEM((tm, d), F32)] if nk > 1 else [], name=name,
        compiler_params=_params("arbitrary", "arbitrary"))(*args)


def _loss_head(x, g, tgt, name):
    t_dim, d = x.shape
    tm = _tile(t_dim, 512)

    def body(x_ref, g_ref, t_ref, dx_ref, dxb_ref, dg_ref, loss_ref):
        xv = x_ref[...]
        gv = g_ref[...]
        r = lax.rsqrt(jnp.mean(xv * xv, axis=-1, keepdims=True) + EPS)
        xh = xv * r
        err = xh * gv - t_ref[...]
        dy = err * (1.0 / d)
        dxh = dy * gv
        dx = r * (dxh - xh * jnp.mean(dxh * xh, axis=-1, keepdims=True))
        dx_ref[...] = dx
        dxb_ref[...] = dx.astype(BF16)
        dg_part = jnp.sum(dy * xh, axis=0, keepdims=True)
        loss_part = jnp.full((1, LANE), 0.5 * jnp.sum(jnp.mean(err * err, axis=-1, keepdims=True)), F32)

        @pl.when(pl.program_id(0) == 0)
        def _():
            dg_ref[...] = dg_part
            loss_ref[...] = loss_part

        @pl.when(pl.program_id(0) > 0)
        def _():
            dg_ref[...] += dg_part
            loss_ref[...] += loss_part

    row = _bs((tm, d), lambda i: (i, 0))
    vec = _bs((1, d), lambda i: (0, 0))
    return pl.pallas_call(
        body, out_shape=(jax.ShapeDtypeStruct((t_dim, d), F32), jax.ShapeDtypeStruct((t_dim, d), BF16),
                         jax.ShapeDtypeStruct((1, d), F32), jax.ShapeDtypeStruct((1, LANE), F32)),
        grid=(t_dim // tm,), in_specs=[row, vec, row],
        out_specs=(row, row, vec, _bs((1, LANE), lambda i: (0, 0))), name=name,
        compiler_params=_params("arbitrary"))(x, g, tgt)


def _glu_conv_fwd(proj, dw_w, dw_b, n_batch, seq, name):
    kk, cc = dw_w.shape
    nj = cc // LANE
    ch = min(128, seq)

    def body(a_ref, gl_ref, w_ref, b_ref, o_ref, pad):
        pad[0:CONV_HALO, :] = jnp.zeros((CONV_HALO, LANE), F32)
        pad[CONV_HALO:CONV_HALO + seq, :] = a_ref[...].astype(F32) * _sig(gl_ref[...].astype(F32))
        for c0 in range(0, seq, ch):
            acc = jnp.broadcast_to(b_ref[...], (ch, LANE))
            for s, win in _delayed(pad[c0:c0 + CONV_HALO + ch, :], CONV_HALO, ch, kk):
                acc = acc + w_ref[kk - 1 - s:kk - s, :] * win
            o_ref[c0:c0 + ch, :] = acc

    return pl.pallas_call(
        body, out_shape=jax.ShapeDtypeStruct((n_batch * seq, cc), F32), grid=(n_batch, nj),
        in_specs=[_bs((seq, LANE), lambda b, j: (b, j)), _bs((seq, LANE), lambda b, j: (b, nj + j)),
                  _bs((kk, LANE), lambda b, j: (0, j)), _bs((1, LANE), lambda b, j: (0, j))],
        out_specs=_bs((seq, LANE), lambda b, j: (b, j)),
        scratch_shapes=[pltpu.VMEM((seq + CONV_HALO, LANE), F32)], name=name,
        compiler_params=_params("parallel", "parallel"))(proj, proj, dw_w, dw_b)


def _glu_conv_bwd(proj, dw_w, dy1, n_batch, seq, name):
    kk, cc = dw_w.shape
    nj = cc // LANE
    ch = min(128, seq)

    def body(a_ref, gl_ref, dy_ref, w_ref, da_ref, dgl_ref, dw_ref, db_ref, padf, padb):
        first = pl.program_id(1) == 0
        padf[0:CONV_HALO, :] = jnp.zeros((CONV_HALO, LANE), F32)
        padf[CONV_HALO:CONV_HALO + seq, :] = a_ref[...].astype(F32) * _sig(gl_ref[...].astype(F32))
        padb[0:seq, :] = dy_ref[...]
        padb[seq:seq + CONV_HALO, :] = jnp.zeros((CONV_HALO, LANE), F32)

        @pl.when(first)
        def _():
            dw_ref[...] = jnp.zeros((kk, LANE), F32)
            db_ref[...] = jnp.zeros((1, LANE), F32)

        dws = [jnp.zeros((1, LANE), F32) for _ in range(kk)]
        for c0 in range(0, seq, ch):
            acc = jnp.zeros((ch, LANE), F32)
            y0 = padf[CONV_HALO + c0:CONV_HALO + c0 + ch, :]
            for s, win in _advanced(padb[c0:c0 + ch + CONV_HALO, :], ch, kk):
                k = kk - 1 - s
                acc = acc + w_ref[k:k + 1, :] * win
                dws[k] = dws[k] + jnp.sum(win * y0, axis=0, keepdims=True)
            sg = _sig(gl_ref[c0:c0 + ch, :].astype(F32))
            da_ref[c0:c0 + ch, :] = (acc * sg).astype(BF16)
            dgl_ref[c0:c0 + ch, :] = (acc * a_ref[c0:c0 + ch, :].astype(F32) * sg * (1.0 - sg)).astype(BF16)
        for k in range(kk):
            dw_ref[k:k + 1, :] += dws[k]
        db_ref[...] += jnp.sum(dy_ref[...], axis=0, keepdims=True)

    tok = _bs((seq, LANE), lambda j, b: (b, j))
    t_dim = n_batch * seq
    return pl.pallas_call(
        body, out_shape=(jax.ShapeDtypeStruct((t_dim, cc), BF16), jax.ShapeDtypeStruct((t_dim, cc), BF16),
                         jax.ShapeDtypeStruct((kk, cc), F32), jax.ShapeDtypeStruct((1, cc), F32)),
        grid=(nj, n_batch),
        in_specs=[tok, _bs((seq, LANE), lambda j, b: (b, nj + j)), tok, _bs((kk, LANE), lambda j, b: (0, j))],
        out_specs=(tok, tok, _bs((kk, LANE), lambda j, b: (0, j)), _bs((1, LANE), lambda j, b: (0, j))),
        scratch_shapes=[pltpu.VMEM((seq + CONV_HALO, LANE), F32), pltpu.VMEM((seq + CONV_HALO, LANE), F32)],
        name=name, compiler_params=_params("parallel", "arbitrary"))(proj, proj, dy1, dw_w)


def _ln_silu_fwd(y1, g, b, name):
    t_dim, c = y1.shape
    tm = _tile(t_dim, 512)

    def body(y_ref, g_ref, b_ref, o_ref):
        yv = y_ref[...]
        xc = yv - jnp.mean(yv, axis=-1, keepdims=True)
        rstd = lax.rsqrt(jnp.mean(xc * xc, axis=-1, keepdims=True) + EPS)
        y2 = xc * rstd * g_ref[...] + b_ref[...]
        o_ref[...] = (y2 * _sig(y2)).astype(BF16)

    row = _bs((tm, c), lambda i: (i, 0))
    vec = _bs((1, c), lambda i: (0, 0))
    return pl.pallas_call(
        body, out_shape=jax.ShapeDtypeStruct((t_dim, c), BF16), grid=(t_dim // tm,),
        in_specs=[row, vec, vec], out_specs=row, name=name, compiler_params=_params("parallel"))(y1, g, b)


def _ln_silu_bwd(y1, g, b, dy3, name):
    t_dim, c = y1.shape
    tm = _tile(t_dim, 512)

    def body(y_ref, g_ref, b_ref, d_ref, dy_ref, dg_ref, db_ref):
        yv = y_ref[...]
        gv = g_ref[...]
        xc = yv - jnp.mean(yv, axis=-1, keepdims=True)
        rstd = lax.rsqrt(jnp.mean(xc * xc, axis=-1, keepdims=True) + EPS)
        yh = xc * rstd
        y2 = yh * gv + b_ref[...]
        s = _sig(y2)
        dy2 = d_ref[...].astype(F32) * (s * (1.0 + y2 * (1.0 - s)))
        dyh = dy2 * gv
        dy_ref[...] = rstd * (dyh - jnp.mean(dyh, axis=-1, keepdims=True)
                              - yh * jnp.mean(dyh * yh, axis=-1, keepdims=True))
        dg_part = jnp.sum(dy2 * yh, axis=0, keepdims=True)
        db_part = jnp.sum(dy2, axis=0, keepdims=True)

        @pl.when(pl.program_id(0) == 0)
        def _():
            dg_ref[...] = dg_part
            db_ref[...] = db_part

        @pl.when(pl.program_id(0) > 0)
        def _():
            dg_ref[...] += dg_part
            db_ref[...] += db_part

    row = _bs((tm, c), lambda i: (i, 0))
    vec = _bs((1, c), lambda i: (0, 0))
    return pl.pallas_call(
        body, out_shape=(jax.ShapeDtypeStruct((t_dim, c), F32), jax.ShapeDtypeStruct((1, c), F32),
                         jax.ShapeDtypeStruct((1, c), F32)),
        grid=(t_dim // tm,), in_specs=[row, vec, vec, row], out_specs=(row, vec, vec), name=name,
        compiler_params=_params("arbitrary"))(y1, g, b, dy3)


def _pool_fwd(proj, col0, n_groups, n_batch, seq, name):
    ch = min(128, seq)

    def body(u_ref, o_ref, pad):
        w = lax.shift_left(jnp.int32(2), pl.program_id(1))
        pad[0:POOL_HALO, :] = jnp.zeros((POOL_HALO, LANE), F32)
        pad[POOL_HALO:POOL_HALO + seq, :] = u_ref[...].astype(F32)
        for c0 in range(0, seq, ch):
            acc = jnp.zeros((ch, LANE), F32)
            for j, win in _delayed(pad[c0:c0 + POOL_HALO + ch, :], POOL_HALO, ch, POOL_WINDOW_MAX):
                acc = acc + jnp.where(j < w, 1.0, 0.0).astype(F32) * win
            t = c0 + lax.broadcasted_iota(jnp.int32, (ch, LANE), 0)
            cnt = jnp.minimum(t + 1, w).astype(F32)
            o_ref[c0:c0 + ch, :] = (acc / cnt - pad[POOL_HALO + c0:POOL_HALO + c0 + ch, :]).astype(BF16)

    return pl.pallas_call(
        body, out_shape=jax.ShapeDtypeStruct((n_batch * seq, n_groups * LANE), BF16), grid=(n_batch, n_groups),
        in_specs=[_bs((seq, LANE), lambda b, g: (b, col0 + g))], out_specs=_bs((seq, LANE), lambda b, g: (b, g)),
        scratch_shapes=[pltpu.VMEM((seq + POOL_HALO, LANE), F32)], name=name,
        compiler_params=_params("parallel", "parallel"))(proj)


def _pool_bwd(dzp, n_groups, n_batch, seq, name):
    ch = min(128, seq)

    def body(d_ref, o_ref, pad):
        w = lax.shift_left(jnp.int32(2), pl.program_id(1))
        for c0 in range(0, seq, ch):
            t = c0 + lax.broadcasted_iota(jnp.int32, (ch, LANE), 0)
            cnt = jnp.minimum(t + 1, w).astype(F32)
            pad[c0:c0 + ch, :] = d_ref[c0:c0 + ch, :].astype(F32) / cnt
        pad[seq:seq + POOL_HALO, :] = jnp.zeros((POOL_HALO, LANE), F32)
        for c0 in range(0, seq, ch):
            acc = jnp.zeros((ch, LANE), F32)
            for j, win in _advanced(pad[c0:c0 + ch + POOL_HALO, :], ch, POOL_WINDOW_MAX):
                acc = acc + jnp.where(j < w, 1.0, 0.0).astype(F32) * win
            o_ref[c0:c0 + ch, :] = (acc - d_ref[c0:c0 + ch, :].astype(F32)).astype(BF16)

    tok = _bs((seq, LANE), lambda b, g: (b, g))
    return pl.pallas_call(
        body, out_shape=jax.ShapeDtypeStruct((n_batch * seq, n_groups * LANE), BF16), grid=(n_batch, n_groups),
        in_specs=[tok], out_specs=tok, scratch_shapes=[pltpu.VMEM((seq + POOL_HALO, LANE), F32)], name=name,
        compiler_params=_params("parallel", "parallel"))(dzp)


def _merge_fwd(proj, col0, yc, yp, scale, name):
    t_dim, d = yc.shape
    half = d // 2
    tm = _tile(t_dim, 512)
    c0 = col0 // half

    def body(gc_ref, gp_ref, yc_ref, yp_ref, s_ref, o_ref):
        f32 = lambda r: r[...].astype(F32)
        o_ref[...] = (_sig(f32(gc_ref)) * f32(yc_ref) + _sig(f32(gp_ref)) * (f32(yp_ref) * s_ref[...])).astype(BF16)

    blk = _bs((tm, half), lambda i, j: (i, j))
    return pl.pallas_call(
        body, out_shape=jax.ShapeDtypeStruct((t_dim, d), BF16), grid=(t_dim // tm, 2),
        in_specs=[_bs((tm, half), lambda i, j: (i, c0 + j)), _bs((tm, half), lambda i, j: (i, c0 + 2 + j)),
                  blk, blk, _bs((1, half), lambda i, j: (0, j))],
        out_specs=blk, name=name, compiler_params=_params("parallel", "parallel"))(proj, proj, yc, yp, scale)


def _merge_bwd(proj, col0, yc, yp, scale, dm, name):
    t_dim, d = yc.shape
    half = d // 2
    tm = _tile(t_dim, 512)
    c0 = col0 // half

    def body(gc_ref, gp_ref, yc_ref, yp_ref, s_ref, dm_ref, dgc_ref, dgp_ref, dyc_ref, dyp_ref, ds_ref):
        dmv = dm_ref[...].astype(F32)
        sgc = _sig(gc_ref[...].astype(F32))
        sgp = _sig(gp_ref[...].astype(F32))
        sv = s_ref[...]
        ypre = yp_ref[...].astype(F32)
        dgc_ref[...] = (dmv * yc_ref[...].astype(F32) * sgc * (1.0 - sgc)).astype(BF16)
        dgp_ref[...] = (dmv * (ypre * sv) * sgp * (1.0 - sgp)).astype(BF16)
        dyc_ref[...] = (dmv * sgc).astype(BF16)
        dyp = dmv * sgp
        dyp_ref[...] = (dyp * sv).astype(BF16)
        part = jnp.sum(dyp * ypre, axis=0, keepdims=True)

        @pl.when(pl.program_id(1) == 0)
        def _():
            ds_ref[...] = part

        @pl.when(pl.program_id(1) > 0)
        def _():
            ds_ref[...] += part

    blk = _bs((tm, half), lambda j, i: (i, j))
    big = jax.ShapeDtypeStruct((t_dim, d), BF16)
    return pl.pallas_call(
        body, out_shape=(big, big, big, big, jax.ShapeDtypeStruct((1, d), F32)), grid=(2, t_dim // tm),
        in_specs=[_bs((tm, half), lambda j, i: (i, c0 + j)), _bs((tm, half), lambda j, i: (i, c0 + 2 + j)),
                  blk, blk, _bs((1, half), lambda j, i: (0, j)), blk],
        out_specs=(blk, blk, blk, blk, _bs((1, half), lambda j, i: (0, j))), name=name,
        compiler_params=_params("parallel", "arbitrary"))(proj, proj, yc, yp, scale, dm)


def _attn_fwd(q, kv, n_batch, seq, m_len, name):
    d = q.shape[1]
    hd = d // XA_HEADS
    tq = _tile(seq, 1024)
    nq = seq // tq
    scale = hd ** -0.5

    def body(q_ref, k_ref, v_ref, o_ref):
        sc = lax.dot_general(q_ref[...].astype(BF16), k_ref[...].astype(BF16), (((1,), (1,)), ((), ())),
                             preferred_element_type=F32) * scale
        p = jnp.exp(sc - jnp.max(sc, axis=-1, keepdims=True))
        pr = p / jnp.sum(p, axis=-1, keepdims=True)
        o_ref[...] = jnp.dot(pr.astype(BF16), v_ref[...].astype(BF16), preferred_element_type=F32).astype(BF16)

    return pl.pallas_call(
        body, out_shape=jax.ShapeDtypeStruct((n_batch * seq, d), BF16), grid=(n_batch, XA_HEADS, nq),
        in_specs=[_bs((tq, hd), lambda b, h, i: (b * nq + i, h)), _bs((m_len, hd), lambda b, h, i: (b, h)),
                  _bs((m_len, hd), lambda b, h, i: (b, XA_HEADS + h))],
        out_specs=_bs((tq, hd), lambda b, h, i: (b * nq + i, h)), name=name,
        compiler_params=_params("parallel", "parallel", "parallel"))(q, kv, kv)


def _attn_bwd(q, kv, datt, n_batch, seq, m_len, name):
    d = q.shape[1]
    hd = d // XA_HEADS
    tq = _tile(seq, 1024)
    nq = seq // tq
    scale = hd ** -0.5

    def body(q_ref, k_ref, v_ref, do_ref, dq_ref, dk_ref, dv_ref):
        qb = q_ref[...].astype(BF16)
        kb = k_ref[...].astype(BF16)
        vb = v_ref[...].astype(BF16)
        dob = do_ref[...].astype(BF16)
        sc = lax.dot_general(qb, kb, (((1,), (1,)), ((), ())), preferred_element_type=F32) * scale
        p = jnp.exp(sc - jnp.max(sc, axis=-1, keepdims=True))
        pr = p / jnp.sum(p, axis=-1, keepdims=True)
        dpr = lax.dot_general(dob, vb, (((1,), (1,)), ((), ())), preferred_element_type=F32)
        dsc = pr * (dpr - jnp.sum(dpr * pr, axis=-1, keepdims=True)) * scale
        dsb = dsc.astype(BF16)
        dq_ref[...] = jnp.dot(dsb, kb, preferred_element_type=F32).astype(BF16)
        dv_part = lax.dot_general(pr.astype(BF16), dob, (((0,), (0,)), ((), ())), preferred_element_type=F32)
        dk_part = lax.dot_general(dsb, qb, (((0,), (0,)), ((), ())), preferred_element_type=F32)

        @pl.when(pl.program_id(2) == 0)
        def _():
            dk_ref[...] = dk_part
            dv_ref[...] = dv_part

        @pl.when(pl.program_id(2) > 0)
        def _():
            dk_ref[...] += dk_part
            dv_ref[...] += dv_part

    qs = _bs((tq, hd), lambda b, h, i: (b * nq + i, h))
    ks = _bs((m_len, hd), lambda b, h, i: (b, h))
    return pl.pallas_call(
        body, out_shape=(jax.ShapeDtypeStruct((n_batch * seq, d), BF16), jax.ShapeDtypeStruct((n_batch * m_len, d), F32),
                         jax.ShapeDtypeStruct((n_batch * m_len, d), F32)),
        grid=(n_batch, XA_HEADS, nq),
        in_specs=[qs, ks, _bs((m_len, hd), lambda b, h, i: (b, XA_HEADS + h)), qs],
        out_specs=(qs, ks, ks), name=name,
        compiler_params=_params("parallel", "parallel", "arbitrary"))(q, kv, kv, datt)


def _gelu_parts(g):
    th = jnp.tanh(GELU_C0 * (g + GELU_C1 * g * g * g))
    return th, 0.5 * g * (1.0 + th)


def _ffn_act_fwd(up_g, up_v, dw_w, n_batch, seq, name):
    kk, c2 = dw_w.shape
    f_dim = c2 // 2
    wd = 2 * LANE
    nj = f_dim // wd
    ch = min(128, seq)

    def body(g_ref, v_ref, wg_ref, wv_ref, o_ref, padg, padv):
        for pad, src in ((padg, g_ref), (padv, v_ref)):
            pad[0:FFN_HALO, :] = jnp.zeros((FFN_HALO, wd), F32)
            pad[FFN_HALO:FFN_HALO + seq, :] = src[...].astype(F32)
        for c0 in range(0, seq, ch):
            gate = jnp.zeros((ch, wd), F32)
            val = jnp.zeros((ch, wd), F32)
            for (s, win_g), (_, win_v) in zip(_delayed(padg[c0:c0 + FFN_HALO + ch, :], FFN_HALO, ch, kk),
                                              _delayed(padv[c0:c0 + FFN_HALO + ch, :], FFN_HALO, ch, kk)):
                gate = gate + wg_ref[kk - 1 - s:kk - s, :] * win_g
                val = val + wv_ref[kk - 1 - s:kk - s, :] * win_v
            o_ref[c0:c0 + ch, :] = (_gelu_parts(gate)[1] * val).astype(BF16)

    return pl.pallas_call(
        body, out_shape=jax.ShapeDtypeStruct((n_batch * seq, f_dim), BF16), grid=(n_batch, nj),
        in_specs=[_bs((seq, wd), lambda b, j: (b, j)), _bs((seq, wd), lambda b, j: (b, j)),
                  _bs((kk, wd), lambda b, j: (0, j)), _bs((kk, wd), lambda b, j: (0, nj + j))],
        out_specs=_bs((seq, wd), lambda b, j: (b, j)),
        scratch_shapes=[pltpu.VMEM((seq + FFN_HALO, wd), F32), pltpu.VMEM((seq + FFN_HALO, wd), F32)], name=name,
        compiler_params=_params("parallel", "parallel"))(up_g, up_v, dw_w, dw_w)


def _ffn_act_bwd(up_g, up_v, dw_w, dact, n_batch, seq, name):
    kk, c2 = dw_w.shape
    f_dim = c2 // 2
    wd = 2 * LANE
    nj = f_dim // wd
    ch = min(128, seq)

    def body(g_ref, v_ref, wg_ref, wv_ref, da_ref, dg_ref, dv_ref, dwg_ref, dwv_ref, padg, padv, pbg, pbv):
        for pad, src in ((padg, g_ref), (padv, v_ref)):
            pad[0:FFN_HALO, :] = jnp.zeros((FFN_HALO, wd), F32)
            pad[FFN_HALO:FFN_HALO + seq, :] = src[...].astype(F32)
        for pb in (pbg, pbv):
            pb[seq:seq + FFN_HALO, :] = jnp.zeros((FFN_HALO, wd), F32)

        @pl.when(pl.program_id(1) == 0)
        def _():
            dwg_ref[...] = jnp.zeros((kk, wd), F32)
            dwv_ref[...] = jnp.zeros((kk, wd), F32)

        for c0 in range(0, seq, ch):
            gate = jnp.zeros((ch, wd), F32)
            val = jnp.zeros((ch, wd), F32)
            for (s, win_g), (_, win_v) in zip(_delayed(padg[c0:c0 + FFN_HALO + ch, :], FFN_HALO, ch, kk),
                                              _delayed(padv[c0:c0 + FFN_HALO + ch, :], FFN_HALO, ch, kk)):
                gate = gate + wg_ref[kk - 1 - s:kk - s, :] * win_g
                val = val + wv_ref[kk - 1 - s:kk - s, :] * win_v
            sq = gate * gate
            th = jnp.tanh(GELU_C0 * gate * (1.0 + GELU_C1 * sq))
            half = 0.5 * th + 0.5
            dgelu = half * (1.0 + gate * (GELU_C0 + 3.0 * GELU_C0 * GELU_C1 * sq) * (1.0 - th))
            dav = da_ref[c0:c0 + ch, :].astype(F32)
            pbg[c0:c0 + ch, :] = dav * val * dgelu
            pbv[c0:c0 + ch, :] = dav * (gate * half)
        for pb, pad, w_ref, d_ref, dw_ref in ((pbg, padg, wg_ref, dg_ref, dwg_ref), (pbv, padv, wv_ref, dv_ref, dwv_ref)):
            for c0 in range(0, seq, ch):
                acc = jnp.zeros((ch, wd), F32)
                for s, win in _advanced(pb[c0:c0 + ch + FFN_HALO, :], ch, kk):
                    acc = acc + w_ref[kk - 1 - s:kk - s, :] * win
                d_ref[c0:c0 + ch, :] = acc.astype(BF16)
            for k in range(kk):
                tot = jnp.zeros((1, wd), F32)
                for c0 in range(0, seq, ch):
                    x = pad[c0:c0 + FFN_HALO + ch, :]
                    win = (x if k == kk - 1 else pltpu.roll(x, kk - 1 - k, 0))[FFN_HALO:FFN_HALO + ch]
                    tot = tot + jnp.sum(pb[c0:c0 + ch, :] * win, axis=0, keepdims=True)
                dw_ref[k:k + 1, :] += tot

    t_dim = n_batch * seq
    tok = _bs((seq, wd), lambda j, b: (b, j))
    wblk = _bs((kk, wd), lambda j, b: (0, j))
    pad_shape = pltpu.VMEM((seq + FFN_HALO, wd), F32)
    return pl.pallas_call(
        body, out_shape=(jax.ShapeDtypeStruct((t_dim, f_dim), BF16), jax.ShapeDtypeStruct((t_dim, f_dim), BF16),
                         jax.ShapeDtypeStruct((kk, f_dim), F32), jax.ShapeDtypeStruct((kk, f_dim), F32)),
        grid=(nj, n_batch),
        in_specs=[tok, tok, wblk, _bs((kk, wd), lambda j, b: (0, nj + j)), tok],
        out_specs=(tok, tok, wblk, wblk), scratch_shapes=[pad_shape, pad_shape, pad_shape, pad_shape], name=name,
        compiler_params=_params("parallel", "arbitrary"))(up_g, up_v, dw_w, dw_w, dact)


def _sum_rows(parts, out_dtype, name):
    r_dim, c_dim = parts[0].shape
    tr = _tile(r_dim, 1200, SUBLANE)
    n = len(parts)

    def body(*refs):
        acc = refs[0][...].astype(F32)
        for r in refs[1:n]:
            acc = acc + r[...].astype(F32)
        refs[n][...] = acc.astype(out_dtype)

    blk = _bs((tr, c_dim), lambda i: (i, 0))
    return pl.pallas_call(
        body, out_shape=jax.ShapeDtypeStruct((r_dim, c_dim), out_dtype), grid=(r_dim // tr,),
        in_specs=[blk] * n, out_specs=blk, name=name, compiler_params=_params("parallel"))(*parts)


def _adamw(w, g, m, v, name):
    shape = w.shape
    c_dim = shape[-1]
    r_dim = w.size // c_dim
    two_d = lambda t: t.reshape(r_dim, c_dim)
    tr = _tile(r_dim, max(SUBLANE, (256 * 1024) // max(c_dim, LANE) // SUBLANE * SUBLANE), SUBLANE)
    c1 = 1.0 - ADAM_B1 ** ADAM_STEP
    c2 = 1.0 - ADAM_B2 ** ADAM_STEP

    def body(w_ref, g_ref, m_ref, v_ref, d_ref, mo_ref, vo_ref):
        gv = g_ref[...]
        mn = ADAM_B1 * m_ref[...] + (1.0 - ADAM_B1) * gv
        vn = ADAM_B2 * v_ref[...] + (1.0 - ADAM_B2) * (gv * gv)
        mo_ref[...] = mn
        vo_ref[...] = vn
        d_ref[...] = -ADAM_LR * ((mn / c1) / (jnp.sqrt(vn / c2) + ADAM_EPS) + ADAM_WD * w_ref[...])

    blk = _bs((tr, c_dim), lambda i: (i, 0))
    out = jax.ShapeDtypeStruct((r_dim, c_dim), F32)
    d, mo, vo = pl.pallas_call(
        body, out_shape=(out, out, out), grid=(r_dim // tr,), in_specs=[blk] * 4, out_specs=(blk, blk, blk),
        name=name, compiler_params=_params("parallel"))(two_d(w), two_d(g), two_d(m), two_d(v))
    return d.reshape(shape), mo.reshape(shape), vo.reshape(shape)


HBM_SPEC = pl.BlockSpec(memory_space=pltpu.HBM)


def _position():
    return lax.axis_index("x"), lax.axis_index("y"), lax.axis_index("c")


def _all_gather(shard, name):
    def body(x_ref, out_ref, send_sems, recv_sems, local_sem):
        x, y, c = _position()
        me, sibling = (x, y, c), (x, y, 1 - c)
        chips = [(1 - x, y), (x, 1 - y), (1 - x, 1 - y)]

        def rows(px, py, pc):
            return out_ref.at[4 * px + 2 * py + pc]

        def copy(k, block, to, src=None):
            return pltpu.make_async_remote_copy(
                src_ref=rows(*block) if src is None else src, dst_ref=rows(*block),
                send_sem=send_sems.at[k], recv_sem=recv_sems.at[k], device_id=to, device_id_type=MESH)

        mine = pltpu.make_async_copy(x_ref, rows(*me), local_sem)
        mine.start()
        first = [copy(0, me, sibling, src=x_ref)]
        first += [copy(1 + j, me, (*chip, c), src=x_ref) for j, chip in enumerate(chips)]
        for cp in first:
            cp.start()
        passed = [copy(4 + j, (*chip, c), sibling) for j, chip in enumerate(chips)]
        for j, chip in enumerate(chips):
            copy(1 + j, (*chip, c), me).wait_recv()
            passed[j].start()
        copy(0, sibling, me).wait_recv()
        for j, chip in enumerate(chips):
            copy(4 + j, (*chip, 1 - c), me).wait_recv()
        for cp in first + passed:
            cp.wait_send()
        mine.wait()

    return pl.pallas_call(
        body, out_shape=jax.ShapeDtypeStruct((N_DEV,) + shard.shape, shard.dtype),
        in_specs=[HBM_SPEC], out_specs=HBM_SPEC,
        scratch_shapes=[pltpu.SemaphoreType.DMA((7,)), pltpu.SemaphoreType.DMA((7,)), pltpu.SemaphoreType.DMA(())],
        name=name)(shard)


CHIP_RELATIONS = ((0, 0), (1, 0), (0, 1), (1, 1))


def _rs_pair_exchange(g, name):
    _, r_dim, c_dim = g.shape
    n = len(CHIP_RELATIONS)

    def body(g_ref, recv_ref, send_sems, recv_sems):
        x, y, c = _position()
        sibling = (x, y, 1 - c)
        copies = []
        for k, (rx, ry) in enumerate(CHIP_RELATIONS):
            px = x + rx - 2 * x * rx
            py = y + ry - 2 * y * ry
            copies.append(pltpu.make_async_remote_copy(
                src_ref=g_ref.at[4 * px + 2 * py + 1 - c], dst_ref=recv_ref.at[k], send_sem=send_sems.at[k],
                recv_sem=recv_sems.at[k], device_id=sibling, device_id_type=MESH))
        for cp in copies:
            cp.start()
        for cp in copies:
            cp.wait()

    return pl.pallas_call(
        body, out_shape=jax.ShapeDtypeStruct((n, r_dim, c_dim), g.dtype), in_specs=[HBM_SPEC], out_specs=HBM_SPEC,
        scratch_shapes=[pltpu.SemaphoreType.DMA((n,)), pltpu.SemaphoreType.DMA((n,))], name=name)(g)


def _rs_pair_sum(g, recv, name):
    _, r_dim, c_dim = g.shape
    n = len(CHIP_RELATIONS)
    tr = _tile(r_dim, 1200, SUBLANE)
    x, y, c = _position()
    own = jnp.stack([4 * (x + rx - 2 * x * rx) + 2 * (y + ry - 2 * y * ry) + c for rx, ry in CHIP_RELATIONS])

    def body(own_ref, g_ref, r_ref, o_ref):
        o_ref[...] = (g_ref[...].astype(F32) + r_ref[...].astype(F32)).astype(o_ref.dtype)

    blk = _bs((None, tr, c_dim), lambda k, i, own_ref: (k, i, 0))
    return pl.pallas_call(
        body, out_shape=jax.ShapeDtypeStruct((n, r_dim, c_dim), g.dtype),
        grid_spec=pltpu.PrefetchScalarGridSpec(
            num_scalar_prefetch=1, grid=(n, r_dim // tr),
            in_specs=[_bs((None, tr, c_dim), lambda k, i, own_ref: (own_ref[k], i, 0)), blk], out_specs=blk),
        name=name, compiler_params=_params("parallel", "parallel"))(own.astype(jnp.int32), g, recv)


SEM_SPEC = pl.BlockSpec(memory_space=pltpu.SEMAPHORE)
DATAFLOW = pltpu.SideEffectType.DATAFLOW_SIDE_EFFECTING
CHIP_FLIPS = CHIP_RELATIONS[1:]
TOKEN = jax.ShapeDtypeStruct((SUBLANE, LANE), F32)


def _flip(v, r):
    return v + r - 2 * v * r


def _chip_copies(src_ref, src_of, dst_ref, dst_of, send_sems, recv_sems):
    x, y, c = _position()
    me = 4 * x + 2 * y + c
    out = []
    for k, (rx, ry) in enumerate(CHIP_FLIPS):
        px, py = _flip(x, rx), _flip(y, ry)
        peer = 4 * px + 2 * py + c
        out.append(pltpu.make_async_remote_copy(
            src_ref=src_ref.at[src_of(k, me, peer)], dst_ref=dst_ref.at[dst_of(k, me, peer)],
            send_sem=send_sems.at[k], recv_sem=recv_sems.at[k], device_id=(px, py, c), device_id_type=MESH))
    return out


def _device_block(ref, spec, d):
    rows, axis = spec
    return ref.at[pl.ds(d * rows, rows)] if axis == 0 else ref.at[:, pl.ds(d * rows, rows)]


def _ag_chips_start(lands, specs, after, name):
    n = len(lands)
    nf = len(CHIP_FLIPS)

    def body(*refs):
        send_sems, recv_sems, token = refs[n + 1], refs[n + 2], refs[-1]
        x, y, c = _position()
        me = 4 * x + 2 * y + c
        for i, spec in enumerate(specs):
            blk = _device_block(refs[i], spec, me)
            for k, (rx, ry) in enumerate(CHIP_FLIPS):
                pltpu.make_async_remote_copy(
                    src_ref=blk, dst_ref=blk, send_sem=send_sems.at[nf * i + k], recv_sem=recv_sems.at[nf * i + k],
                    device_id=(_flip(x, rx), _flip(y, ry), c), device_id_type=MESH).start()
        token[...] = jnp.zeros(TOKEN.shape, TOKEN.dtype)

    sems = pltpu.SemaphoreType.DMA((nf * n,))
    return pl.pallas_call(
        body, name=name, out_shape=(sems, sems, *[pltpu.HBM(t.shape, t.dtype) for t in lands], TOKEN),
        in_specs=(HBM_SPEC,) * n + (ANY_SPEC,),
        out_specs=(SEM_SPEC, SEM_SPEC) + (HBM_SPEC,) * n + (pl.BlockSpec(memory_space=pltpu.VMEM),),
        input_output_aliases={i: 2 + i for i in range(n)}, compiler_params=pltpu.CompilerParams(has_side_effects=DATAFLOW),
    )(*[pltpu.with_memory_space_constraint(t, pltpu.HBM) for t in lands], after)


def _ag_chips_wait(send_sems, recv_sems, lands, specs, after, name):
    n = len(lands)
    nf = len(CHIP_FLIPS)

    def body(*refs):
        send_sems, recv_sems = refs[n], refs[n + 1]
        x, y, c = _position()
        me = 4 * x + 2 * y + c
        for i, spec in enumerate(specs):
            for k, (rx, ry) in enumerate(CHIP_FLIPS):
                px, py = _flip(x, rx), _flip(y, ry)
                cp = pltpu.make_async_remote_copy(
                    src_ref=_device_block(refs[i], spec, me), dst_ref=_device_block(refs[i], spec, 4 * px + 2 * py + c),
                    send_sem=send_sems.at[nf * i + k], recv_sem=recv_sems.at[nf * i + k],
                    device_id=(px, py, c), device_id_type=MESH)
                cp.wait_send()
                cp.wait_recv()

    return pl.pallas_call(
        body, name=name, out_shape=tuple(pltpu.HBM(t.shape, t.dtype) for t in lands),
        in_specs=(HBM_SPEC,) * n + (SEM_SPEC, SEM_SPEC, ANY_SPEC), out_specs=(HBM_SPEC,) * n,
        input_output_aliases={i: i for i in range(n)}, compiler_params=pltpu.CompilerParams(has_side_effects=DATAFLOW),
    )(*lands, send_sems, recv_sems, after)


def _ag_pair_forward(lands, specs, name):
    n = len(lands)
    nr = len(CHIP_RELATIONS)

    def body(*refs):
        outs, send_sems, recv_sems = refs[n:2 * n], refs[2 * n], refs[2 * n + 1]
        x, y, c = _position()
        copies = []
        for i, spec in enumerate(specs):
            for k, (rx, ry) in enumerate(CHIP_RELATIONS):
                chip = 4 * _flip(x, rx) + 2 * _flip(y, ry)
                held = _device_block(outs[i], spec, chip + c)
                sems = dict(send_sem=send_sems.at[nr * i + k], recv_sem=recv_sems.at[nr * i + k],
                            device_id=(x, y, 1 - c), device_id_type=MESH)
                mine = pltpu.make_async_remote_copy(src_ref=held, dst_ref=held, **sems)
                theirs = pltpu.make_async_remote_copy(src_ref=held, dst_ref=_device_block(outs[i], spec, chip + 1 - c), **sems)
                copies.append((mine, theirs))
        for mine, _ in copies:
            mine.start()
        for mine, theirs in copies:
            mine.wait_send()
            theirs.wait_recv()

    sems = pltpu.SemaphoreType.DMA((nr * n,))
    return pl.pallas_call(
        body, out_shape=tuple(jax.ShapeDtypeStruct(t.shape, t.dtype) for t in lands), in_specs=[HBM_SPEC] * n,
        out_specs=(HBM_SPEC,) * n, input_output_aliases={i: i for i in range(n)}, scratch_shapes=[sems, sems], name=name)(*lands)


def _rs_chips_start(pair, name):
    _, r_dim, c_dim = pair.shape
    n = len(CHIP_FLIPS)

    def body(pair_ref, far_ref, send_sems, recv_sems, pair_thru, far_thru, token):
        for cp in _chip_copies(pair_ref, lambda k, me, peer: k + 1, far_ref, lambda k, me, peer: k, send_sems, recv_sems):
            cp.start()
        token[...] = jnp.zeros(TOKEN.shape, TOKEN.dtype)

    far = lax.empty((n, r_dim, c_dim), pair.dtype)
    return pl.pallas_call(
        body, name=name,
        out_shape=(pltpu.SemaphoreType.DMA((n,)), pltpu.SemaphoreType.DMA((n,)), pltpu.HBM(pair.shape, pair.dtype),
                   pltpu.HBM(far.shape, far.dtype), TOKEN),
        in_specs=(HBM_SPEC, HBM_SPEC),
        out_specs=(SEM_SPEC, SEM_SPEC, HBM_SPEC, HBM_SPEC, pl.BlockSpec(memory_space=pltpu.VMEM)),
        input_output_aliases={0: 2, 1: 3}, compiler_params=pltpu.CompilerParams(has_side_effects=DATAFLOW),
    )(pltpu.with_memory_space_constraint(pair, pltpu.HBM), pltpu.with_memory_space_constraint(far, pltpu.HBM))


def _rs_chips_wait(send_sems, recv_sems, pair, far, after, name):
    def body(pair_ref, far_ref, send_sems, recv_sems, after_ref, pair_out, far_out):
        for cp in _chip_copies(pair_ref, lambda k, me, peer: k + 1, far_ref, lambda k, me, peer: k, send_sems, recv_sems):
            cp.wait_send()
            cp.wait_recv()

    return pl.pallas_call(
        body, name=name, out_shape=(pltpu.HBM(pair.shape, pair.dtype), pltpu.HBM(far.shape, far.dtype)),
        in_specs=(HBM_SPEC, HBM_SPEC, SEM_SPEC, SEM_SPEC, ANY_SPEC),
        out_specs=(HBM_SPEC, HBM_SPEC), input_output_aliases={0: 0, 1: 1},
        compiler_params=pltpu.CompilerParams(has_side_effects=DATAFLOW),
    )(pair, far, send_sems, recv_sems, after)


def _rs_final_sum(pair, far, name):
    _, r_dim, c_dim = pair.shape
    tr = _tile(r_dim, 1200, SUBLANE)

    def body(p_ref, f0_ref, f1_ref, f2_ref, o_ref):
        o_ref[...] = ((p_ref[...].astype(F32) + f0_ref[...].astype(F32)) + f1_ref[...].astype(F32)) + f2_ref[...].astype(F32)

    def slot(k):
        return _bs((None, tr, c_dim), lambda i: (k, i, 0))

    return pl.pallas_call(
        body, out_shape=jax.ShapeDtypeStruct((r_dim, c_dim), F32), grid=(r_dim // tr,),
        in_specs=[slot(0), slot(0), slot(1), slot(2)], out_specs=_bs((tr, c_dim), lambda i: (i, 0)), name=name,
        compiler_params=_params("parallel"))(pair, far, far, far)


def _reduce_scatter_begin(g, name):
    recv = _rs_pair_exchange(g, name + "_pair")
    pair = _rs_pair_sum(g, recv, name + "_pairsum")
    return _rs_chips_start(pair, name + "_chips_start")


def _reduce_scatter_end(state, after, name):
    send_sems, recv_sems, pair, far, _ = state
    pair, far = _rs_chips_wait(send_sems, recv_sems, pair, far, after, name + "_chips_wait")
    return _rs_final_sum(pair, far, name + "_sum")


MATRICES = (("w_in", True), ("w_out", False), ("w_q", False), ("w_kv", True), ("w_o", False), ("w_up", True),
            ("w_down", False), ("w_conv_out", True), ("w_pool_grp", True))
MIX_NAMES = ("w_in", "w_conv_out", "w_pool_grp", "w_out")
REST_NAMES = ("w_q", "w_kv", "w_o", "w_up", "w_down")


def _parts(layer):
    return (("mix", MIX_NAMES), ("rest", REST_NAMES)) if layer == 0 else (("all", MIX_NAMES + REST_NAMES),)


def _to_rows(name, transposed, w, d_model):
    if name == "w_pool_grp":
        w = jnp.swapaxes(w, 1, 2)
    elif transposed:
        w = w.T
    return w.reshape(-1, d_model)


def _from_rows(name, transposed, rows, shard_shape):
    if name == "w_pool_grp":
        g, i, o = shard_shape
        return jnp.swapaxes(rows.reshape(g, o, i), 1, 2)
    if transposed:
        return rows.reshape(shard_shape[1], shard_shape[0]).T
    return rows.reshape(shard_shape)


def _scatter_blocks(name, full, shard_shape, d_model, n_dev=N_DEV):
    if name == "w_pool_grp":
        g, i, o = shard_shape
        return jnp.swapaxes(full.reshape(g, n_dev, o, i), 0, 1).reshape(n_dev, -1, d_model)
    return full.reshape(n_dev, -1, d_model)


def kernel(x, mem, mix_norm_g, w_in, conv_dw_w, conv_dw_b, conv_ln_g, conv_ln_b, w_conv_out, w_pool_grp, pool_scale, w_out, xattn_norm_g, mem_norm_g, w_q, w_kv, w_o, ffn_norm_g, w_up, ffn_dw_w, w_down, final_norm_g, loss_target, m_mix_norm_g, m_w_in, m_conv_dw_w, m_conv_dw_b, m_conv_ln_g, m_conv_ln_b, m_w_conv_out, m_w_pool_grp, m_pool_scale, m_w_out, m_xattn_norm_g, m_mem_norm_g, m_w_q, m_w_kv, m_w_o, m_ffn_norm_g, m_w_up, m_ffn_dw_w, m_w_down, m_final_norm_g, v_mix_norm_g, v_w_in, v_conv_dw_w, v_conv_dw_b, v_conv_ln_g, v_conv_ln_b, v_w_conv_out, v_w_pool_grp, v_pool_scale, v_w_out, v_xattn_norm_g, v_mem_norm_g, v_w_q, v_w_kv, v_w_o, v_ffn_norm_g, v_w_up, v_ffn_dw_w, v_w_down, v_final_norm_g):
    p = dict(locals())
    weight_names = ["mix_norm_g", "w_in", "conv_dw_w", "conv_dw_b", "conv_ln_g", "conv_ln_b", "w_conv_out",
                    "w_pool_grp", "pool_scale", "w_out", "xattn_norm_g", "mem_norm_g", "w_q", "w_kv", "w_o",
                    "ffn_norm_g", "w_up", "ffn_dw_w", "w_down", "final_norm_g"]
    n_batch, seq, d_model = x.shape
    m_len = mem.shape[1]
    depth = w_in.shape[0]
    assert depth == 2, "the exchange schedule below is written for two layers"
    t_dim = n_batch * seq
    c_conv = conv_dw_b.shape[1]
    n_groups = w_pool_grp.shape[1]
    assert w_pool_grp.shape[2] == LANE and c_conv % LANE == 0 and n_groups * LANE == c_conv
    gate_col0 = 2 * c_conv + n_groups * LANE
    pool_col0 = (2 * c_conv) // LANE

    dev = 4 * lax.axis_index("x") + 2 * lax.axis_index("y") + lax.axis_index("c")
    filt = jnp.concatenate([conv_dw_w.reshape(-1), ffn_dw_w.reshape(-1)])
    filt_rows = lax.bitcast_convert_type(filt, BF16).reshape(-1, d_model)
    transposed = dict(MATRICES)
    layout = {part: [(name, transposed[name], _to_rows(name, transposed[name], p[name][0], d_model).shape[0])
                     for name in names] for l in range(depth) for part, names in _parts(l)}
    part_of = {(l, name): part for l in range(depth) for part, names in _parts(l) for name in names}

    def landing(name, shard):
        if name == "w_pool_grp":
            block, axis = jnp.swapaxes(shard, 1, 2), 1
        elif name == "filt":
            block, axis = shard, 0
        else:
            block, axis = (shard.T if transposed[name] else shard), 0
        block = block.astype(BF16)
        rows = block.shape[axis]
        shape = block.shape[:axis] + (N_DEV * rows,) + block.shape[axis + 1:]
        start = (0,) * axis + (dev * rows,) + (0,) * (block.ndim - axis - 1)
        return lax.dynamic_update_slice(lax.empty(shape, BF16), block, start), (rows, axis)

    ag_state = {}
    after = filt_rows
    for l in range(depth):
        for part, names in _parts(l):
            items = [(name, p[name][l]) for name in names]
            if (l, part) == (0, part_of[(0, "w_in")]):
                items.append(("filt", filt_rows))
            lands, specs = zip(*[landing(name, shard) for name, shard in items])
            out = _ag_chips_start(lands, specs, after, f"ag{l}{part}_chips_start")
            ag_state[(l, part)] = ([name for name, _ in items], specs, out)
            after = out[-1]
    all_started = after

    full = [dict() for _ in range(depth)]

    def ensure(l, name, after):
        if name in full[l]:
            return
        part = part_of[(l, name)]
        names, specs, out = ag_state[(l, part)]
        lands = _ag_chips_wait(out[0], out[1], out[2:-1], specs, after, f"ag{l}{part}_chips_wait")
        lands = _ag_pair_forward(lands, specs, f"ag{l}{part}_pair_forward")
        full[l].update(zip(names, lands))

    vec = lambda a: a.reshape(1, -1)
    x2d = x.reshape(t_dim, d_model)
    mem2d = mem.reshape(n_batch * m_len, d_model)
    mem_n = _rmsnorm_fwd(mem2d, vec(mem_norm_g), "mem_norm", after=all_started)
    h_first = _rmsnorm_fwd(x2d, vec(mix_norm_g[0]), "mix_norm_l0", after=mem_n)
    ensure(0, "w_in", h_first)
    filt_all = lax.bitcast_convert_type(full[0]["filt"].reshape(N_DEV, -1, 2), F32)
    n_cw = conv_dw_w.size
    kc, cs = conv_dw_w.shape[1:]
    kf, fs = ffn_dw_w.shape[1:]
    conv_w_full = jnp.moveaxis(filt_all[:, :n_cw].reshape(N_DEV, depth, kc, cs), 0, 2).reshape(depth, kc, N_DEV * cs)
    ffn_w_full = jnp.moveaxis(filt_all[:, n_cw:].reshape(N_DEV, depth, kf, fs), 0, 2).reshape(depth, kf, N_DEV * fs)

    saved = []
    xc = x2d
    for l in range(depth):
        ensure(l, "w_in", xc)
        wl = full[l]
        s = {"x0": xc}
        s["h"] = h_first if l == 0 else _rmsnorm_fwd(xc, vec(mix_norm_g[l]), f"mix_norm_l{l}")
        s["proj"] = _matmul(s["h"], wl["w_in"], "nt", f"in_proj_l{l}", out_dtype=BF16)
        s["y1"] = _glu_conv_fwd(s["proj"], conv_w_full[l], vec(conv_dw_b[l]), n_batch, seq, f"glu_conv_l{l}")
        s["y3"] = _ln_silu_fwd(s["y1"], vec(conv_ln_g[l]), vec(conv_ln_b[l]), f"ln_silu_l{l}")
        s["yc"] = _matmul(s["y3"], wl["w_conv_out"], "nt", f"conv_out_l{l}", out_dtype=BF16)
        s["zp"] = _pool_fwd(s["proj"], pool_col0, n_groups, n_batch, seq, f"pool_l{l}")
        s["yp"] = _grouped(s["zp"], wl["w_pool_grp"], "nt", f"pool_proj_l{l}", out_dtype=BF16)
        s["merged"] = _merge_fwd(s["proj"], gate_col0, s["yc"], s["yp"], vec(pool_scale[l]), f"merge_l{l}")
        s["x1"] = _matmul(s["merged"], wl["w_out"], "nn", f"mix_out_l{l}", res=xc)
        ensure(l, "w_q", s["x1"])
        half_up = wl["w_up"].shape[0] // 2
        up_gate, up_val = (0, half_up), (half_up, half_up)
        s["hq"] = _rmsnorm_fwd(s["x1"], vec(xattn_norm_g[l]), f"xattn_norm_l{l}")
        s["q"] = _matmul(s["hq"], wl["w_q"], "nn", f"q_proj_l{l}", out_dtype=BF16)
        s["kv"] = _matmul(mem_n, wl["w_kv"], "nt", f"kv_proj_l{l}", out_dtype=BF16)
        s["att"] = _attn_fwd(s["q"], s["kv"], n_batch, seq, m_len, f"attn_l{l}")
        s["x2"] = _matmul(s["att"], wl["w_o"], "nn", f"attn_out_l{l}", res=s["x1"])
        s["hf"] = _rmsnorm_fwd(s["x2"], vec(ffn_norm_g[l]), f"ffn_norm_l{l}")
        s["up_g"] = _matmul(s["hf"], wl["w_up"], "nt", f"up_proj_gate_l{l}", out_dtype=BF16, b_window=up_gate)
        s["up_v"] = _matmul(s["hf"], wl["w_up"], "nt", f"up_proj_val_l{l}", out_dtype=BF16, b_window=up_val)
        s["act"] = _ffn_act_fwd(s["up_g"], s["up_v"], ffn_w_full[l], n_batch, seq, f"ffn_act_l{l}")
        xc = _matmul(s["act"], wl["w_down"], "nn", f"down_proj_l{l}", res=s["x2"])
        saved.append(s)

    dx, dxb, dg_final, loss_part = _loss_head(xc, vec(final_norm_g), loss_target.reshape(t_dim, d_model), "loss_head")

    small = {"final_norm_g": dg_final.reshape(-1)}
    big = [dict() for _ in range(depth)]
    rs_state = {}
    rs_after = loss_part

    def rs_begin(l, part):
        pack = lax.empty((N_DEV, sum(nrows for _, _, nrows in layout[part]), d_model), BF16)
        row0 = 0
        for name, _, nrows in layout[part]:
            pieces = big[l][name] if isinstance(big[l][name], tuple) else (big[l][name],)
            d0 = 0
            for piece in pieces:
                blocks = _scatter_blocks(name, piece, p[name].shape[1:], d_model, N_DEV // len(pieces)).astype(BF16)
                pack = lax.dynamic_update_slice(pack, blocks, (d0, row0, 0))
                d0 += blocks.shape[0]
            row0 += nrows
        rs_state[(l, part)] = _reduce_scatter_begin(pack, f"rs{l}{part}")
        return rs_state[(l, part)][4]

    dmem_n = None
    for l in reversed(range(depth)):
        wl, s = full[l], saved[l]
        sm = {}
        dact = _matmul(dxb, wl["w_down"], "nt", f"d_act_l{l}", out_dtype=BF16, after=rs_after)
        big[l]["w_down"] = _matmul(s["act"], dxb, "tn", f"d_w_down_l{l}", out_dtype=BF16)
        dup_g, dup_v, dwf_g, dwf_v = _ffn_act_bwd(s["up_g"], s["up_v"], ffn_w_full[l], dact, n_batch, seq,
                                                  f"ffn_act_bwd_l{l}")
        sm["ffn_dw_w"] = jnp.concatenate([dwf_g, dwf_v], axis=1)
        dx, dxb, dg = _matmul_rmsnorm_bwd((dup_g, dup_v), wl["w_up"], "nn", s["x2"], vec(ffn_norm_g[l]), dx,
                                          f"d_hf_ffn_norm_bwd_l{l}")
        big[l]["w_up"] = (_matmul(dup_g, s["hf"], "tn", f"d_w_up_gate_l{l}", out_dtype=BF16),
                          _matmul(dup_v, s["hf"], "tn", f"d_w_up_val_l{l}", out_dtype=BF16))
        sm["ffn_norm_g"] = dg
        datt = _matmul(dxb, wl["w_o"], "nt", f"d_att_l{l}", out_dtype=BF16, after=rs_after)
        big[l]["w_o"] = _matmul(s["att"], dxb, "tn", f"d_w_o_l{l}", out_dtype=BF16)
        dq, dk, dv = _attn_bwd(s["q"], s["kv"], datt, n_batch, seq, m_len, f"attn_bwd_l{l}")
        dkv = jnp.concatenate([dk, dv], axis=1)
        big[l]["w_kv"] = _matmul(dkv, mem_n, "tn", f"d_w_kv_l{l}", out_dtype=BF16)
        dmem_n = _matmul(dkv, wl["w_kv"], "nn", f"d_mem_l{l}", res=dmem_n)
        big[l]["w_q"] = _matmul(s["hq"], dq, "tn", f"d_w_q_l{l}", out_dtype=BF16)
        dx, dxb, dg = _matmul_rmsnorm_bwd(dq, wl["w_q"], "nt", s["x1"], vec(xattn_norm_g[l]), dx,
                                          f"d_hq_xattn_norm_bwd_l{l}")
        sm["xattn_norm_g"] = dg
        if part_of[(l, "w_q")] != part_of[(l, "w_in")]:
            rs_after = rs_begin(l, part_of[(l, "w_q")])
        dmerged = _matmul(dxb, wl["w_out"], "nt", f"d_merged_l{l}", out_dtype=BF16, after=rs_after)
        big[l]["w_out"] = _matmul(s["merged"], dxb, "tn", f"d_w_out_l{l}", out_dtype=BF16)
        dgc, dgp, dyc, dyp, dscale = _merge_bwd(s["proj"], gate_col0, s["yc"], s["yp"], vec(pool_scale[l]), dmerged,
                                                f"merge_bwd_l{l}")
        sm["pool_scale"] = dscale
        dzp = _grouped(dyp, wl["w_pool_grp"], "nn", f"d_zp_l{l}", out_dtype=BF16)
        big[l]["w_pool_grp"] = _grouped_tn(dyp, s["zp"], n_groups, f"d_w_pool_l{l}")
        du = _pool_bwd(dzp, n_groups, n_batch, seq, f"pool_bwd_l{l}")
        dy3 = _matmul(dyc, wl["w_conv_out"], "nn", f"d_y3_l{l}", out_dtype=BF16)
        big[l]["w_conv_out"] = _matmul(dyc, s["y3"], "tn", f"d_w_conv_out_l{l}", out_dtype=BF16)
        dy1, dlg, dlb = _ln_silu_bwd(s["y1"], vec(conv_ln_g[l]), vec(conv_ln_b[l]), dy3, f"ln_silu_bwd_l{l}")
        sm["conv_ln_g"], sm["conv_ln_b"] = dlg, dlb
        da, dgl, dcw, dcb = _glu_conv_bwd(s["proj"], conv_w_full[l], dy1, n_batch, seq, f"glu_conv_bwd_l{l}")
        sm["conv_dw_w"], sm["conv_dw_b"] = dcw, dcb
        dproj = jnp.concatenate([da, dgl, du, dgc, dgp], axis=1)
        big[l]["w_in"] = _matmul(dproj, s["h"], "tn", f"d_w_in_l{l}", out_dtype=BF16)
        dx, dxb, dg = _matmul_rmsnorm_bwd(dproj, wl["w_in"], "nn", s["x0"], vec(mix_norm_g[l]), dx,
                                          f"d_h_mix_norm_bwd_l{l}")
        sm["mix_norm_g"] = dg
        for k, val in sm.items():
            small[(l, k)] = val.reshape(-1)
        rs_after = rs_begin(l, part_of[(l, "w_in")])
    _, _, dg_mem = _rmsnorm_bwd(mem2d, vec(mem_norm_g), dmem_n, None, "mem_norm_bwd")
    small["mem_norm_g"] = dg_mem.reshape(-1)
    small["loss"] = loss_part.reshape(-1)

    grads = {}
    per_layer = {name: [None] * depth for name, _ in MATRICES}
    for l in reversed(range(depth)):
        for part, _ in reversed(_parts(l)):
            mat_grads = _reduce_scatter_end(rs_state[(l, part)], rs_after, f"rs{l}{part}")
            row0 = 0
            for name, tr, nrows in layout[part]:
                per_layer[name][l] = _from_rows(name, tr, mat_grads[row0:row0 + nrows], p[name].shape[1:])
                row0 += nrows
    for name, _ in MATRICES:
        grads[name] = jnp.stack(per_layer[name])

    keys = list(small.keys())
    flat = jnp.concatenate([small[k] for k in keys])
    n_small = flat.shape[0]
    rows_small = -(-n_small // (SUBLANE * d_model)) * SUBLANE
    flat = jnp.pad(flat, (0, rows_small * d_model - n_small)).reshape(rows_small, d_model)
    every = _all_gather(flat, "small_all_gather")
    total = _sum_rows([every[i] for i in range(N_DEV)], F32, "small_sum").reshape(-1)
    off = 0
    red = {}
    for k in keys:
        red[k] = total[off:off + small[k].shape[0]]
        off += small[k].shape[0]
    loss = red["loss"][0]
    for name in ("mix_norm_g", "conv_dw_b", "conv_ln_g", "conv_ln_b", "pool_scale", "xattn_norm_g", "ffn_norm_g"):
        grads[name] = jnp.stack([red[(l, name)] for l in range(depth)])
    grads["conv_dw_w"] = jnp.stack([
        lax.dynamic_slice_in_dim(red[(l, "conv_dw_w")].reshape(kc, N_DEV * cs), dev * cs, cs, axis=1)
        for l in range(depth)])
    grads["ffn_dw_w"] = jnp.stack([
        lax.dynamic_slice_in_dim(red[(l, "ffn_dw_w")].reshape(kf, N_DEV * fs), dev * fs, fs, axis=1)
        for l in range(depth)])
    grads["mem_norm_g"] = red["mem_norm_g"]
    grads["final_norm_g"] = red["final_norm_g"]

    deltas, new_m, new_v = {}, {}, {}
    for name in weight_names:
        deltas[name], new_m[name], new_v[name] = _adamw(p[name], grads[name], p["m_" + name], p["v_" + name],
                                                        f"adamw_{name}")
    grad_x = dx.reshape(n_batch, seq, d_model)
    return (loss, grad_x, *[grads[n] for n in weight_names], *[deltas[n] for n in weight_names],
            *[new_m[n] for n in weight_names], *[new_v[n] for n in weight_names])
```

```python
import functools

import jax
import jax.numpy as jnp
from jax import lax
from jax.experimental import pallas as pl
from jax.experimental.pallas import tpu as pltpu

F32 = jnp.float32
BF16 = jnp.bfloat16
MESH = pl.DeviceIdType.MESH

N_DEV = 8
EPS = 1e-6
V7X_VMEM_BYTES = 64 * 1024 * 1024
VMEM_LIMIT = (V7X_VMEM_BYTES * 3) // 4
LANE = 128
SUBLANE = 8

CONV_HALO = 32
POOL_HALO = 16
FFN_HALO = 8
POOL_WINDOW_MAX = 16
XA_HEADS = 4

ADAM_LR = 0.001
ADAM_B1 = 0.9
ADAM_B2 = 0.999
ADAM_EPS = 1e-08
ADAM_WD = 0.01
ADAM_STEP = 10

GELU_C0 = 0.7978845608028654
GELU_C1 = 0.044715


ANY_SPEC = pl.BlockSpec(memory_space=pl.ANY)


def _tile(n, cap, mult=LANE):
    if n <= cap:
        return n
    best = None
    for d in range(mult, cap + 1, mult):
        if n % d == 0:
            best = d
    assert best is not None, (n, cap, mult)
    return best


def _params(*sem):
    return pltpu.CompilerParams(dimension_semantics=sem, vmem_limit_bytes=VMEM_LIMIT)


def _delayed(x, halo, rows, n_shifts):
    for r in range(min(SUBLANE, n_shifts)):
        xr = x if r == 0 else pltpu.roll(x, r, 0)
        for s in range(r, n_shifts, SUBLANE):
            yield s, xr[halo - (s - r):halo - (s - r) + rows]


def _advanced(x, rows, n_shifts):
    for r in range(min(SUBLANE, n_shifts)):
        xr = x if r == 0 else pltpu.roll(x, x.shape[0] - r, 0)
        for s in range(r, n_shifts, SUBLANE):
            yield s, xr[s - r:s - r + rows]


def _sig(x):
    return 1.0 / (1.0 + jnp.exp(-x))


def _bs(shape, imap):
    return pl.BlockSpec(shape, imap)


def _mxu_tile(n, cap):
    if n <= cap:
        return n
    best = {mult: max((d for d in range(mult, cap + 1, mult) if n % d == 0), default=0) for mult in (2 * LANE, LANE)}
    assert best[LANE] > 0, (n, cap)
    return best[2 * LANE] if 2 * best[2 * LANE] >= best[LANE] else best[LANE]


def _matmul(a, b, mode, name, res=None, out_dtype=F32, after=None, b_window=None):
    b_row0, b_rows = b_window if b_window is not None else (0, b.shape[0])
    if mode == "tn":
        k_dim, m_dim = a.shape
        k2, n_dim = b_rows, b.shape[1]
    elif mode == "nn":
        m_dim, k_dim = a.shape
        k2, n_dim = b_rows, b.shape[1]
    else:
        m_dim, k_dim = a.shape
        n_dim, k2 = b_rows, b.shape[1]
    assert k_dim == k2, (name, a.shape, b.shape)
    size = lambda t: jnp.dtype(t).itemsize
    tm = _mxu_tile(m_dim, 2816 if mode == "tn" else 1024)
    tn = _mxu_tile(n_dim, 2816)
    fixed = tm * tn * (2 * size(out_dtype) + (2 * size(res.dtype) if res is not None else 0) + 4)
    for cap in (2816, 2048, 1792, 1024, 512):
        tk = _mxu_tile(k_dim, cap)
        if fixed + 2 * tk * (tm * size(a.dtype) + tn * size(b.dtype)) <= VMEM_LIMIT - 8 * 1024 * 1024:
            break
    nk = k_dim // tk
    use_acc = nk > 1 and out_dtype != F32
    if mode == "tn":
        a_spec, ca = _bs((tk, tm), lambda i, j, k: (k, i)), 0
    else:
        a_spec, ca = _bs((tm, tk), lambda i, j, k: (i, k)), 1
    if mode == "nt":
        assert b_row0 % tn == 0
        b_spec, cb = _bs((tn, tk), lambda i, j, k: (j + b_row0 // tn, k)), 1
    else:
        assert b_row0 % tk == 0
        b_spec, cb = _bs((tk, tn), lambda i, j, k: (k + b_row0 // tk, j)), 0
    dims = (((ca,), (cb,)), ((), ()))
    o_spec = _bs((tm, tn), lambda i, j, k: (i, j))
    has_res = res is not None

    def body(*refs):
        a_ref, b_ref = refs[:2]
        r_ref = refs[2] if has_res else None
        o_ref = refs[n_in]
        k = pl.program_id(2)
        part = lax.dot_general(a_ref[...].astype(BF16), b_ref[...].astype(BF16), dims,
                               preferred_element_type=F32)
        if nk == 1:
            if has_res:
                part = part + r_ref[...].astype(F32)
            o_ref[...] = part.astype(out_dtype)
            return
        acc = refs[-1] if use_acc else o_ref

        @pl.when(k == 0)
        def _():
            acc[...] = part + r_ref[...].astype(F32) if has_res else part

        @pl.when(k > 0)
        def _():
            acc[...] += part

        if use_acc:
            @pl.when(k == nk - 1)
            def _():
                o_ref[...] = acc[...].astype(out_dtype)

    in_specs = [a_spec, b_spec] + ([o_spec] if has_res else [])
    args = (a, b) + ((res,) if has_res else ())
    if after is not None:
        in_specs.append(ANY_SPEC)
        args += (after,)
    n_in = len(args)
    return pl.pallas_call(
        body, out_shape=jax.ShapeDtypeStruct((m_dim, n_dim), out_dtype),
        grid=(m_dim // tm, n_dim // tn, nk), in_specs=in_specs, out_specs=o_spec,
        scratch_shapes=[pltpu.VMEM((tm, tn), F32)] if use_acc else [], name=name,
        compiler_params=_params("parallel", "parallel", "arbitrary"))(*args)


def _grouped(a, w, mode, name, out_dtype=F32):
    t_dim = a.shape[0]
    g_dim, r_dim, c_dim = w.shape
    ka, no = (c_dim, r_dim) if mode == "nt" else (r_dim, c_dim)
    tm = _tile(t_dim, 2048)
    dims = (((1,), (1 if mode == "nt" else 0,)), ((), ()))

    def body(a_ref, w_ref, o_ref):
        o_ref[...] = lax.dot_general(a_ref[...].astype(BF16), w_ref[...].astype(BF16), dims,
                                     preferred_element_type=F32).astype(out_dtype)

    return pl.pallas_call(
        body, out_shape=jax.ShapeDtypeStruct((t_dim, g_dim * no), out_dtype),
        grid=(t_dim // tm, g_dim),
        in_specs=[_bs((tm, ka), lambda i, g: (i, g)), _bs((None, r_dim, c_dim), lambda i, g: (g, 0, 0))],
        out_specs=_bs((tm, no), lambda i, g: (i, g)), name=name,
        compiler_params=_params("parallel", "parallel"))(a, w)


def _grouped_tn(a, b, g_dim, name):
    t_dim = a.shape[0]
    ra = a.shape[1] // g_dim
    cb = b.shape[1] // g_dim
    tm = _tile(t_dim, 2048)
    nt = t_dim // tm

    def body(a_ref, b_ref, o_ref):
        part = lax.dot_general(a_ref[...].astype(BF16), b_ref[...].astype(BF16), (((0,), (0,)), ((), ())),
                               preferred_element_type=F32)

        @pl.when(pl.program_id(1) == 0)
        def _():
            o_ref[...] = part

        @pl.when(pl.program_id(1) > 0)
        def _():
            o_ref[...] += part

    return pl.pallas_call(
        body, out_shape=jax.ShapeDtypeStruct((g_dim, ra, cb), F32), grid=(g_dim, nt),
        in_specs=[_bs((tm, ra), lambda g, i: (i, g)), _bs((tm, cb), lambda g, i: (i, g))],
        out_specs=_bs((None, ra, cb), lambda g, i: (g, 0, 0)), name=name,
        compiler_params=_params("parallel", "arbitrary"))(a, b)


def _rmsnorm_fwd(x, g, name, after=None):
    t_dim, d = x.shape
    tm = _tile(t_dim, 512)

    def body(x_ref, g_ref, *rest):
        o_ref = rest[-1]
        xv = x_ref[...]
        r = lax.rsqrt(jnp.mean(xv * xv, axis=-1, keepdims=True) + EPS)
        o_ref[...] = (xv * r * g_ref[...]).astype(BF16)

    return pl.pallas_call(
        body, out_shape=jax.ShapeDtypeStruct((t_dim, d), BF16), grid=(t_dim // tm,),
        in_specs=[_bs((tm, d), lambda i: (i, 0)), _bs((1, d), lambda i: (0, 0))] + ([ANY_SPEC] if after is not None else []),
        out_specs=_bs((tm, d), lambda i: (i, 0)), name=name,
        compiler_params=_params("parallel"))(x, g, *([after] if after is not None else []))


def _rmsnorm_bwd(x, g, dh, dx_in, name):
    t_dim, d = x.shape
    tm = _tile(t_dim, 512)
    has_in = dx_in is not None

    def body(*refs):
        if has_in:
            x_ref, g_ref, dh_ref, di_ref, dx_ref, dxb_ref, dg_ref = refs
        else:
            x_ref, g_ref, dh_ref, dx_ref, dxb_ref, dg_ref = refs
        xv = x_ref[...]
        r = lax.rsqrt(jnp.mean(xv * xv, axis=-1, keepdims=True) + EPS)
        xh = xv * r
        dhv = dh_ref[...].astype(F32)
        dxh = dhv * g_ref[...]
        dx = r * (dxh - xh * jnp.mean(dxh * xh, axis=-1, keepdims=True))
        if has_in:
            dx = dx + di_ref[...]
        dx_ref[...] = dx
        dxb_ref[...] = dx.astype(BF16)
        part = jnp.sum(dhv * xh, axis=0, keepdims=True)

        @pl.when(pl.program_id(0) == 0)
        def _():
            dg_ref[...] = part

        @pl.when(pl.program_id(0) > 0)
        def _():
            dg_ref[...] += part

    row = _bs((tm, d), lambda i: (i, 0))
    vec = _bs((1, d), lambda i: (0, 0))
    args = (x, g, dh) + ((dx_in,) if has_in else ())
    return pl.pallas_call(
        body, out_shape=(jax.ShapeDtypeStruct((t_dim, d), F32), jax.ShapeDtypeStruct((t_dim, d), BF16),
                         jax.ShapeDtypeStruct((1, d), F32)),
        grid=(t_dim // tm,), in_specs=[row, vec, row] + ([row] if has_in else []),
        out_specs=(row, row, vec), name=name, compiler_params=_params("arbitrary"))(*args)


def _matmul_rmsnorm_bwd(a, b, mode, x, g, dx_in, name, res=None, b_window=None):
    pieces = a if isinstance(a, tuple) else (a,)
    n_p = len(pieces)
    b_row0, b_rows = b_window if b_window is not None else (0, b.shape[0])
    m_dim, k_piece = pieces[0].shape
    assert all(t.shape == pieces[0].shape for t in pieces)
    k_dim = n_p * k_piece
    d = x.shape[1]
    assert (b_rows, b.shape[1]) == ((k_dim, d) if mode == "nn" else (d, k_dim)), (name, pieces[0].shape, b.shape)
    tm = _mxu_tile(m_dim, 512)
    tk = _mxu_tile(k_piece, 1792)
    nkp = k_piece // tk
    nk = n_p * nkp
    has_res = res is not None
    if mode == "nt":
        assert b_row0 == 0
        b_spec, cb = _bs((d, tk), lambda i, k: (0, k)), 1
    else:
        assert b_row0 % tk == 0
        b_spec, cb = _bs((tk, d), lambda i, k: (k + b_row0 // tk, 0)), 0
    dims = (((1,), (cb,)), ((), ()))

    def body(*refs):
        b_ref = refs[n_p]
        r_ref = refs[n_p + 1] if has_res else None
        x_ref, g_ref, di_ref, dx_ref, dxb_ref, dg_ref = refs[n_p + 1 + has_res:n_p + 7 + has_res]
        i, k = pl.program_id(0), pl.program_id(1)

        def finish(dhv):
            if has_res:
                dhv = dhv + r_ref[...].astype(F32)
            xv = x_ref[...]
            r = lax.rsqrt(jnp.mean(xv * xv, axis=-1, keepdims=True) + EPS)
            xh = xv * r
            dxh = dhv * g_ref[...]
            dx = r * (dxh - xh * jnp.mean(dxh * xh, axis=-1, keepdims=True)) + di_ref[...]
            dx_ref[...] = dx
            dxb_ref[...] = dx.astype(BF16)
            dg_part = jnp.sum(dhv * xh, axis=0, keepdims=True)

            @pl.when(i == 0)
            def _():
                dg_ref[...] = dg_part

            @pl.when(i > 0)
            def _():
                dg_ref[...] += dg_part

        def step(a_ref):
            part = lax.dot_general(a_ref[...].astype(BF16), b_ref[...].astype(BF16), dims, preferred_element_type=F32)
            if nk == 1:
                finish(part)
                return
            acc = refs[-1]

            @pl.when(k == 0)
            def _():
                acc[...] = part

            @pl.when(jnp.logical_and(k > 0, k < nk - 1))
            def _():
                acc[...] += part

            @pl.when(k == nk - 1)
            def _():
                finish(acc[...] + part)

        if n_p == 1:
            step(refs[0])
        else:
            for q in range(n_p):
                pl.when(jnp.logical_and(k >= q * nkp, k < (q + 1) * nkp))(functools.partial(step, refs[q]))

    row = _bs((tm, d), lambda i, k: (i, 0))
    vec = _bs((1, d), lambda i, k: (0, 0))
    a_specs = [_bs((tm, tk), lambda i, k, q=q: (i, jnp.clip(k - q * nkp, 0, nkp - 1))) for q in range(n_p)]
    in_specs = a_specs + [b_spec] + ([row] if has_res else []) + [row, vec, row]
    args = pieces + (b,) + ((res,) if has_res else ()) + (x, g, dx_in)
    return pl.pallas_call(
        body, out_shape=(jax.ShapeDtypeStruct((m_dim, d), F32), jax.ShapeDtypeStruct((m_dim, d), BF16),
                         jax.ShapeDtypeStruct((1, d), F32)),
        grid=(m_dim // tm, nk), in_specs=in_specs, out_specs=(row, row, vec),
        scratch_shapes=[pltpu.VMEM((tm, d), F32)] if nk > 1 else [], name=name,
        compiler_params=_params("arbitrary", "arbitrary"))(*args)


def _loss_head(x, g, tgt, name):
    t_dim, d = x.shape
    tm = _tile(t_dim, 512)

    def body(x_ref, g_ref, t_ref, dx_ref, dxb_ref, dg_ref, loss_ref):
        xv = x_ref[...]
        gv = g_ref[...]
        r = lax.rsqrt(jnp.mean(xv * xv, axis=-1, keepdims=True) + EPS)
        xh = xv * r
        err = xh * gv - t_ref[...]
        dy = err * (1.0 / d)
        dxh = dy * gv
        dx = r * (dxh - xh * jnp.mean(dxh * xh, axis=-1, keepdims=True))
        dx_ref[...] = dx
        dxb_ref[...] = dx.astype(BF16)
        dg_part = jnp.sum(dy * xh, axis=0, keepdims=True)
        loss_part = jnp.full((1, LANE), 0.5 * jnp.sum(jnp.mean(err * err, axis=-1, keepdims=True)), F32)

        @pl.when(pl.program_id(0) == 0)
        def _():
            dg_ref[...] = dg_part
            loss_ref[...] = loss_part

        @pl.when(pl.program_id(0) > 0)
        def _():
            dg_ref[...] += dg_part
            loss_ref[...] += loss_part

    row = _bs((tm, d), lambda i: (i, 0))
    vec = _bs((1, d), lambda i: (0, 0))
    return pl.pallas_call(
        body, out_shape=(jax.ShapeDtypeStruct((t_dim, d), F32), jax.ShapeDtypeStruct((t_dim, d), BF16),
                         jax.ShapeDtypeStruct((1, d), F32), jax.ShapeDtypeStruct((1, LANE), F32)),
        grid=(t_dim // tm,), in_specs=[row, vec, row],
        out_specs=(row, row, vec, _bs((1, LANE), lambda i: (0, 0))), name=name,
        compiler_params=_params("arbitrary"))(x, g, tgt)


def _glu_conv_fwd(proj, dw_w, dw_b, n_batch, seq, name):
    kk, cc = dw_w.shape
    nj = cc // LANE
    ch = min(256, seq)

    def body(a_ref, gl_ref, w_ref, b_ref, o_ref, pad):
        pad[0:CONV_HALO, :] = jnp.zeros((CONV_HALO, LANE), F32)
        pad[CONV_HALO:CONV_HALO + seq, :] = a_ref[...].astype(F32) * _sig(gl_ref[...].astype(F32))
        for c0 in range(0, seq, ch):
            acc = jnp.broadcast_to(b_ref[...], (ch, LANE))
            for k in range(kk):
                acc = acc + w_ref[k:k + 1, :] * pad[pl.ds(c0 + CONV_HALO - (kk - 1) + k, ch), :]
            o_ref[c0:c0 + ch, :] = acc

    return pl.pallas_call(
        body, out_shape=jax.ShapeDtypeStruct((n_batch * seq, cc), F32), grid=(n_batch, nj),
        in_specs=[_bs((seq, LANE), lambda b, j: (b, j)), _bs((seq, LANE), lambda b, j: (b, nj + j)),
                  _bs((kk, LANE), lambda b, j: (0, j)), _bs((1, LANE), lambda b, j: (0, j))],
        out_specs=_bs((seq, LANE), lambda b, j: (b, j)),
        scratch_shapes=[pltpu.VMEM((seq + CONV_HALO, LANE), F32)], name=name,
        compiler_params=_params("parallel", "parallel"))(proj, proj, dw_w, dw_b)


def _glu_conv_bwd(proj, dw_w, dy1, n_batch, seq, name):
    kk, cc = dw_w.shape
    nj = cc // LANE
    ch = min(256, seq)

    def body(a_ref, gl_ref, dy_ref, w_ref, da_ref, dgl_ref, dw_ref, db_ref, padf, padb):
        first = pl.program_id(1) == 0
        padf[0:CONV_HALO, :] = jnp.zeros((CONV_HALO, LANE), F32)
        padf[CONV_HALO:CONV_HALO + seq, :] = a_ref[...].astype(F32) * _sig(gl_ref[...].astype(F32))
        padb[0:seq, :] = dy_ref[...]
        padb[seq:seq + CONV_HALO, :] = jnp.zeros((CONV_HALO, LANE), F32)

        @pl.when(first)
        def _():
            dw_ref[...] = jnp.zeros((kk, LANE), F32)
            db_ref[...] = jnp.zeros((1, LANE), F32)

        dws = [jnp.zeros((1, LANE), F32) for _ in range(kk)]
        for c0 in range(0, seq, ch):
            acc = jnp.zeros((ch, LANE), F32)
            y0 = padf[CONV_HALO + c0:CONV_HALO + c0 + ch, :]
            for k in range(kk):
                win = padb[pl.ds(c0 + (kk - 1) - k, ch), :]
                acc = acc + w_ref[k:k + 1, :] * win
                dws[k] = dws[k] + jnp.sum(win * y0, axis=0, keepdims=True)
            sg = _sig(gl_ref[c0:c0 + ch, :].astype(F32))
            da_ref[c0:c0 + ch, :] = (acc * sg).astype(BF16)
            dgl_ref[c0:c0 + ch, :] = (acc * a_ref[c0:c0 + ch, :].astype(F32) * sg * (1.0 - sg)).astype(BF16)
        for k in range(kk):
            dw_ref[k:k + 1, :] += dws[k]
        db_ref[...] += jnp.sum(dy_ref[...], axis=0, keepdims=True)

    tok = _bs((seq, LANE), lambda j, b: (b, j))
    t_dim = n_batch * seq
    return pl.pallas_call(
        body, out_shape=(jax.ShapeDtypeStruct((t_dim, cc), BF16), jax.ShapeDtypeStruct((t_dim, cc), BF16),
                         jax.ShapeDtypeStruct((kk, cc), F32), jax.ShapeDtypeStruct((1, cc), F32)),
        grid=(nj, n_batch),
        in_specs=[tok, _bs((seq, LANE), lambda j, b: (b, nj + j)), tok, _bs((kk, LANE), lambda j, b: (0, j))],
        out_specs=(tok, tok, _bs((kk, LANE), lambda j, b: (0, j)), _bs((1, LANE), lambda j, b: (0, j))),
        scratch_shapes=[pltpu.VMEM((seq + CONV_HALO, LANE), F32), pltpu.VMEM((seq + CONV_HALO, LANE), F32)],
        name=name, compiler_params=_params("parallel", "arbitrary"))(proj, proj, dy1, dw_w)


def _ln_silu_fwd(y1, g, b, name):
    t_dim, c = y1.shape
    tm = _tile(t_dim, 512)

    def body(y_ref, g_ref, b_ref, o_ref):
        yv = y_ref[...]
        xc = yv - jnp.mean(yv, axis=-1, keepdims=True)
        rstd = lax.rsqrt(jnp.mean(xc * xc, axis=-1, keepdims=True) + EPS)
        y2 = xc * rstd * g_ref[...] + b_ref[...]
        o_ref[...] = (y2 * _sig(y2)).astype(BF16)

    row = _bs((tm, c), lambda i: (i, 0))
    vec = _bs((1, c), lambda i: (0, 0))
    return pl.pallas_call(
        body, out_shape=jax.ShapeDtypeStruct((t_dim, c), BF16), grid=(t_dim // tm,),
        in_specs=[row, vec, vec], out_specs=row, name=name, compiler_params=_params("parallel"))(y1, g, b)


def _ln_silu_bwd(y1, g, b, dy3, name):
    t_dim, c = y1.shape
    tm = _tile(t_dim, 512)

    def body(y_ref, g_ref, b_ref, d_ref, dy_ref, dg_ref, db_ref):
        yv = y_ref[...]
        gv = g_ref[...]
        xc = yv - jnp.mean(yv, axis=-1, keepdims=True)
        rstd = lax.rsqrt(jnp.mean(xc * xc, axis=-1, keepdims=True) + EPS)
        yh = xc * rstd
        y2 = yh * gv + b_ref[...]
        s = _sig(y2)
        dy2 = d_ref[...].astype(F32) * (s * (1.0 + y2 * (1.0 - s)))
        dyh = dy2 * gv
        dy_ref[...] = rstd * (dyh - jnp.mean(dyh, axis=-1, keepdims=True)
                              - yh * jnp.mean(dyh * yh, axis=-1, keepdims=True))
        dg_part = jnp.sum(dy2 * yh, axis=0, keepdims=True)
        db_part = jnp.sum(dy2, axis=0, keepdims=True)

        @pl.when(pl.program_id(0) == 0)
        def _():
            dg_ref[...] = dg_part
            db_ref[...] = db_part

        @pl.when(pl.program_id(0) > 0)
        def _():
            dg_ref[...] += dg_part
            db_ref[...] += db_part

    row = _bs((tm, c), lambda i: (i, 0))
    vec = _bs((1, c), lambda i: (0, 0))
    return pl.pallas_call(
        body, out_shape=(jax.ShapeDtypeStruct((t_dim, c), F32), jax.ShapeDtypeStruct((1, c), F32),
                         jax.ShapeDtypeStruct((1, c), F32)),
        grid=(t_dim // tm,), in_specs=[row, vec, vec, row], out_specs=(row, vec, vec), name=name,
        compiler_params=_params("arbitrary"))(y1, g, b, dy3)


def _pool_fwd(proj, col0, n_groups, n_batch, seq, name):
    ch = min(256, seq)

    def body(u_ref, o_ref, pad):
        w = lax.shift_left(jnp.int32(2), pl.program_id(1))
        pad[0:POOL_HALO, :] = jnp.zeros((POOL_HALO, LANE), F32)
        pad[POOL_HALO:POOL_HALO + seq, :] = u_ref[...].astype(F32)
        for c0 in range(0, seq, ch):
            acc = jnp.zeros((ch, LANE), F32)
            for j in range(POOL_WINDOW_MAX):
                acc = acc + jnp.where(j < w, 1.0, 0.0).astype(F32) * pad[pl.ds(c0 + POOL_HALO - j, ch), :]
            t = c0 + lax.broadcasted_iota(jnp.int32, (ch, LANE), 0)
            cnt = jnp.minimum(t + 1, w).astype(F32)
            o_ref[c0:c0 + ch, :] = (acc / cnt - pad[POOL_HALO + c0:POOL_HALO + c0 + ch, :]).astype(BF16)

    return pl.pallas_call(
        body, out_shape=jax.ShapeDtypeStruct((n_batch * seq, n_groups * LANE), BF16), grid=(n_batch, n_groups),
        in_specs=[_bs((seq, LANE), lambda b, g: (b, col0 + g))], out_specs=_bs((seq, LANE), lambda b, g: (b, g)),
        scratch_shapes=[pltpu.VMEM((seq + POOL_HALO, LANE), F32)], name=name,
        compiler_params=_params("parallel", "parallel"))(proj)


def _pool_bwd(dzp, n_groups, n_batch, seq, name):
    ch = min(256, seq)

    def body(d_ref, o_ref, pad):
        w = lax.shift_left(jnp.int32(2), pl.program_id(1))
        for c0 in range(0, seq, ch):
            t = c0 + lax.broadcasted_iota(jnp.int32, (ch, LANE), 0)
            cnt = jnp.minimum(t + 1, w).astype(F32)
            pad[c0:c0 + ch, :] = d_ref[c0:c0 + ch, :].astype(F32) / cnt
        pad[seq:seq + POOL_HALO, :] = jnp.zeros((POOL_HALO, LANE), F32)
        for c0 in range(0, seq, ch):
            acc = jnp.zeros((ch, LANE), F32)
            for j in range(POOL_WINDOW_MAX):
                acc = acc + jnp.where(j < w, 1.0, 0.0).astype(F32) * pad[pl.ds(c0 + j, ch), :]
            o_ref[c0:c0 + ch, :] = (acc - d_ref[c0:c0 + ch, :].astype(F32)).astype(BF16)

    tok = _bs((seq, LANE), lambda b, g: (b, g))
    return pl.pallas_call(
        body, out_shape=jax.ShapeDtypeStruct((n_batch * seq, n_groups * LANE), BF16), grid=(n_batch, n_groups),
        in_specs=[tok], out_specs=tok, scratch_shapes=[pltpu.VMEM((seq + POOL_HALO, LANE), F32)], name=name,
        compiler_params=_params("parallel", "parallel"))(dzp)


def _merge_fwd(proj, col0, yc, yp, scale, name):
    t_dim, d = yc.shape
    half = d // 2
    tm = _tile(t_dim, 512)
    c0 = col0 // half

    def body(gc_ref, gp_ref, yc_ref, yp_ref, s_ref, o_ref):
        f32 = lambda r: r[...].astype(F32)
        o_ref[...] = (_sig(f32(gc_ref)) * f32(yc_ref) + _sig(f32(gp_ref)) * (f32(yp_ref) * s_ref[...])).astype(BF16)

    blk = _bs((tm, half), lambda i, j: (i, j))
    return pl.pallas_call(
        body, out_shape=jax.ShapeDtypeStruct((t_dim, d), BF16), grid=(t_dim // tm, 2),
        in_specs=[_bs((tm, half), lambda i, j: (i, c0 + j)), _bs((tm, half), lambda i, j: (i, c0 + 2 + j)),
                  blk, blk, _bs((1, half), lambda i, j: (0, j))],
        out_specs=blk, name=name, compiler_params=_params("parallel", "parallel"))(proj, proj, yc, yp, scale)


def _merge_bwd(proj, col0, yc, yp, scale, dm, name):
    t_dim, d = yc.shape
    half = d // 2
    tm = _tile(t_dim, 512)
    c0 = col0 // half

    def body(gc_ref, gp_ref, yc_ref, yp_ref, s_ref, dm_ref, dgc_ref, dgp_ref, dyc_ref, dyp_ref, ds_ref):
        dmv = dm_ref[...].astype(F32)
        sgc = _sig(gc_ref[...].astype(F32))
        sgp = _sig(gp_ref[...].astype(F32))
        sv = s_ref[...]
        ypre = yp_ref[...].astype(F32)
        dgc_ref[...] = (dmv * yc_ref[...].astype(F32) * sgc * (1.0 - sgc)).astype(BF16)
        dgp_ref[...] = (dmv * (ypre * sv) * sgp * (1.0 - sgp)).astype(BF16)
        dyc_ref[...] = (dmv * sgc).astype(BF16)
        dyp = dmv * sgp
        dyp_ref[...] = (dyp * sv).astype(BF16)
        part = jnp.sum(dyp * ypre, axis=0, keepdims=True)

        @pl.when(pl.program_id(1) == 0)
        def _():
            ds_ref[...] = part

        @pl.when(pl.program_id(1) > 0)
        def _():
            ds_ref[...] += part

    blk = _bs((tm, half), lambda j, i: (i, j))
    big = jax.ShapeDtypeStruct((t_dim, d), BF16)
    return pl.pallas_call(
        body, out_shape=(big, big, big, big, jax.ShapeDtypeStruct((1, d), F32)), grid=(2, t_dim // tm),
        in_specs=[_bs((tm, half), lambda j, i: (i, c0 + j)), _bs((tm, half), lambda j, i: (i, c0 + 2 + j)),
                  blk, blk, _bs((1, half), lambda j, i: (0, j)), blk],
        out_specs=(blk, blk, blk, blk, _bs((1, half), lambda j, i: (0, j))), name=name,
        compiler_params=_params("parallel", "arbitrary"))(proj, proj, yc, yp, scale, dm)


def _attn_fwd(q, kv, n_batch, seq, m_len, name):
    d = q.shape[1]
    hd = d // XA_HEADS
    tq = _tile(seq, 1024)
    nq = seq // tq
    scale = hd ** -0.5

    def body(q_ref, k_ref, v_ref, o_ref):
        sc = lax.dot_general(q_ref[...].astype(BF16), k_ref[...].astype(BF16), (((1,), (1,)), ((), ())),
                             preferred_element_type=F32) * scale
        p = jnp.exp(sc - jnp.max(sc, axis=-1, keepdims=True))
        pr = p / jnp.sum(p, axis=-1, keepdims=True)
        o_ref[...] = jnp.dot(pr.astype(BF16), v_ref[...].astype(BF16), preferred_element_type=F32).astype(BF16)

    return pl.pallas_call(
        body, out_shape=jax.ShapeDtypeStruct((n_batch * seq, d), BF16), grid=(n_batch, XA_HEADS, nq),
        in_specs=[_bs((tq, hd), lambda b, h, i: (b * nq + i, h)), _bs((m_len, hd), lambda b, h, i: (b, h)),
                  _bs((m_len, hd), lambda b, h, i: (b, XA_HEADS + h))],
        out_specs=_bs((tq, hd), lambda b, h, i: (b * nq + i, h)), name=name,
        compiler_params=_params("parallel", "parallel", "parallel"))(q, kv, kv)


def _attn_bwd(q, kv, datt, n_batch, seq, m_len, name):
    d = q.shape[1]
    hd = d // XA_HEADS
    tq = _tile(seq, 1024)
    nq = seq // tq
    scale = hd ** -0.5

    def body(q_ref, k_ref, v_ref, do_ref, dq_ref, dk_ref, dv_ref):
        qb = q_ref[...].astype(BF16)
        kb = k_ref[...].astype(BF16)
        vb = v_ref[...].astype(BF16)
        dob = do_ref[...].astype(BF16)
        sc = lax.dot_general(qb, kb, (((1,), (1,)), ((), ())), preferred_element_type=F32) * scale
        p = jnp.exp(sc - jnp.max(sc, axis=-1, keepdims=True))
        pr = p / jnp.sum(p, axis=-1, keepdims=True)
        dpr = lax.dot_general(dob, vb, (((1,), (1,)), ((), ())), preferred_element_type=F32)
        dsc = pr * (dpr - jnp.sum(dpr * pr, axis=-1, keepdims=True)) * scale
        dsb = dsc.astype(BF16)
        dq_ref[...] = jnp.dot(dsb, kb, preferred_element_type=F32).astype(BF16)
        dv_part = lax.dot_general(pr.astype(BF16), dob, (((0,), (0,)), ((), ())), preferred_element_type=F32)
        dk_part = lax.dot_general(dsb, qb, (((0,), (0,)), ((), ())), preferred_element_type=F32)

        @pl.when(pl.program_id(2) == 0)
        def _():
            dk_ref[...] = dk_part
            dv_ref[...] = dv_part

        @pl.when(pl.program_id(2) > 0)
        def _():
            dk_ref[...] += dk_part
            dv_ref[...] += dv_part

    qs = _bs((tq, hd), lambda b, h, i: (b * nq + i, h))
    ks = _bs((m_len, hd), lambda b, h, i: (b, h))
    return pl.pallas_call(
        body, out_shape=(jax.ShapeDtypeStruct((n_batch * seq, d), BF16), jax.ShapeDtypeStruct((n_batch * m_len, d), F32),
                         jax.ShapeDtypeStruct((n_batch * m_len, d), F32)),
        grid=(n_batch, XA_HEADS, nq),
        in_specs=[qs, ks, _bs((m_len, hd), lambda b, h, i: (b, XA_HEADS + h)), qs],
        out_specs=(qs, ks, ks), name=name,
        compiler_params=_params("parallel", "parallel", "arbitrary"))(q, kv, kv, datt)


def _gelu_parts(g):
    th = jnp.tanh(GELU_C0 * (g + GELU_C1 * g * g * g))
    return th, 0.5 * g * (1.0 + th)


def _ffn_act_fwd(up_g, up_v, dw_w, n_batch, seq, name):
    kk, c2 = dw_w.shape
    f_dim = c2 // 2
    wd = 2 * LANE
    nj = f_dim // wd
    ch = min(128, seq)

    def body(g_ref, v_ref, wg_ref, wv_ref, o_ref, padg, padv):
        for pad, src in ((padg, g_ref), (padv, v_ref)):
            pad[0:FFN_HALO, :] = jnp.zeros((FFN_HALO, wd), F32)
            pad[FFN_HALO:FFN_HALO + seq, :] = src[...].astype(F32)
        for c0 in range(0, seq, ch):
            gate = jnp.zeros((ch, wd), F32)
            val = jnp.zeros((ch, wd), F32)
            for (s, win_g), (_, win_v) in zip(_delayed(padg[c0:c0 + FFN_HALO + ch, :], FFN_HALO, ch, kk),
                                              _delayed(padv[c0:c0 + FFN_HALO + ch, :], FFN_HALO, ch, kk)):
                gate = gate + wg_ref[kk - 1 - s:kk - s, :] * win_g
                val = val + wv_ref[kk - 1 - s:kk - s, :] * win_v
            o_ref[c0:c0 + ch, :] = (_gelu_parts(gate)[1] * val).astype(BF16)

    return pl.pallas_call(
        body, out_shape=jax.ShapeDtypeStruct((n_batch * seq, f_dim), BF16), grid=(n_batch, nj),
        in_specs=[_bs((seq, wd), lambda b, j: (b, j)), _bs((seq, wd), lambda b, j: (b, j)),
                  _bs((kk, wd), lambda b, j: (0, j)), _bs((kk, wd), lambda b, j: (0, nj + j))],
        out_specs=_bs((seq, wd), lambda b, j: (b, j)),
        scratch_shapes=[pltpu.VMEM((seq + FFN_HALO, wd), F32), pltpu.VMEM((seq + FFN_HALO, wd), F32)], name=name,
        compiler_params=_params("parallel", "parallel"))(up_g, up_v, dw_w, dw_w)


def _ffn_act_bwd(up_g, up_v, dw_w, dact, n_batch, seq, name):
    kk, c2 = dw_w.shape
    f_dim = c2 // 2
    wd = 2 * LANE
    nj = f_dim // wd
    ch = min(128, seq)

    def body(g_ref, v_ref, wg_ref, wv_ref, da_ref, dg_ref, dv_ref, dwg_ref, dwv_ref, padg, padv, pbg, pbv, shg, shv):
        for pad, src in ((padg, g_ref), (padv, v_ref)):
            pad[0:FFN_HALO, :] = jnp.zeros((FFN_HALO, wd), F32)
            pad[FFN_HALO:FFN_HALO + seq, :] = src[...].astype(F32)
        for pb in (pbg, pbv):
            pb[seq:seq + FFN_HALO, :] = jnp.zeros((FFN_HALO, wd), F32)

        @pl.when(pl.program_id(1) == 0)
        def _():
            dwg_ref[...] = jnp.zeros((kk, wd), F32)
            dwv_ref[...] = jnp.zeros((kk, wd), F32)

        for c0 in range(0, seq, ch):
            gate = jnp.zeros((ch, wd), F32)
            val = jnp.zeros((ch, wd), F32)
            for (s, win_g), (_, win_v) in zip(_delayed(padg[c0:c0 + FFN_HALO + ch, :], FFN_HALO, ch, kk),
                                              _delayed(padv[c0:c0 + FFN_HALO + ch, :], FFN_HALO, ch, kk)):
                gate = gate + wg_ref[kk - 1 - s:kk - s, :] * win_g
                val = val + wv_ref[kk - 1 - s:kk - s, :] * win_v
                if s > 0:
                    shg[s - 1, c0:c0 + ch, :] = win_g
                    shv[s - 1, c0:c0 + ch, :] = win_v
            sq = gate * gate
            th = jnp.tanh(GELU_C0 * gate * (1.0 + GELU_C1 * sq))
            half = 0.5 * th + 0.5
            dgelu = half * (1.0 + gate * (GELU_C0 + 3.0 * GELU_C0 * GELU_C1 * sq) * (1.0 - th))
            dav = da_ref[c0:c0 + ch, :].astype(F32)
            pbg[c0:c0 + ch, :] = dav * val * dgelu
            pbv[c0:c0 + ch, :] = dav * (gate * half)
        for pb, pad, sh, w_ref, d_ref, dw_ref in ((pbg, padg, shg, wg_ref, dg_ref, dwg_ref),
                                                  (pbv, padv, shv, wv_ref, dv_ref, dwv_ref)):
            for c0 in range(0, seq, ch):
                acc = jnp.zeros((ch, wd), F32)
                for s, win in _advanced(pb[c0:c0 + ch + FFN_HALO, :], ch, kk):
                    acc = acc + w_ref[kk - 1 - s:kk - s, :] * win
                d_ref[c0:c0 + ch, :] = acc.astype(BF16)
            for k in range(kk):
                tot = jnp.zeros((1, wd), F32)
                for c0 in range(0, seq, ch):
                    win = pad[FFN_HALO + c0:FFN_HALO + c0 + ch, :] if k == kk - 1 else sh[kk - 2 - k, c0:c0 + ch, :]
                    tot = tot + jnp.sum(pb[c0:c0 + ch, :] * win, axis=0, keepdims=True)
                dw_ref[k:k + 1, :] += tot

    t_dim = n_batch * seq
    tok = _bs((seq, wd), lambda j, b: (b, j))
    wblk = _bs((kk, wd), lambda j, b: (0, j))
    pad_shape = pltpu.VMEM((seq + FFN_HALO, wd), F32)
    return pl.pallas_call(
        body, out_shape=(jax.ShapeDtypeStruct((t_dim, f_dim), BF16), jax.ShapeDtypeStruct((t_dim, f_dim), BF16),
                         jax.ShapeDtypeStruct((kk, f_dim), F32), jax.ShapeDtypeStruct((kk, f_dim), F32)),
        grid=(nj, n_batch),
        in_specs=[tok, tok, wblk, _bs((kk, wd), lambda j, b: (0, nj + j)), tok],
        out_specs=(tok, tok, wblk, wblk),
        scratch_shapes=[pad_shape] * 4 + [pltpu.VMEM((kk - 1, seq, wd), F32)] * 2, name=name,
        compiler_params=_params("parallel", "arbitrary"))(up_g, up_v, dw_w, dw_w, dact)


def _sum_rows(parts, out_dtype, name):
    r_dim, c_dim = parts[0].shape
    tr = _tile(r_dim, 1200, SUBLANE)
    n = len(parts)

    def body(*refs):
        acc = refs[0][...].astype(F32)
        for r in refs[1:n]:
            acc = acc + r[...].astype(F32)
        refs[n][...] = acc.astype(out_dtype)

    blk = _bs((tr, c_dim), lambda i: (i, 0))
    return pl.pallas_call(
        body, out_shape=jax.ShapeDtypeStruct((r_dim, c_dim), out_dtype), grid=(r_dim // tr,),
        in_specs=[blk] * n, out_specs=blk, name=name, compiler_params=_params("parallel"))(*parts)


def _adamw(w, g, m, v, name):
    shape = w.shape
    c_dim = shape[-1]
    r_dim = w.size // c_dim
    two_d = lambda t: t.reshape(r_dim, c_dim)
    tr = _tile(r_dim, max(SUBLANE, (256 * 1024) // max(c_dim, LANE) // SUBLANE * SUBLANE), SUBLANE)
    c1 = 1.0 - ADAM_B1 ** ADAM_STEP
    c2 = 1.0 - ADAM_B2 ** ADAM_STEP

    def body(w_ref, g_ref, m_ref, v_ref, d_ref, mo_ref, vo_ref):
        gv = g_ref[...]
        mn = ADAM_B1 * m_ref[...] + (1.0 - ADAM_B1) * gv
        vn = ADAM_B2 * v_ref[...] + (1.0 - ADAM_B2) * (gv * gv)
        mo_ref[...] = mn
        vo_ref[...] = vn
        d_ref[...] = -ADAM_LR * ((mn / c1) / (jnp.sqrt(vn / c2) + ADAM_EPS) + ADAM_WD * w_ref[...])

    blk = _bs((tr, c_dim), lambda i: (i, 0))
    out = jax.ShapeDtypeStruct((r_dim, c_dim), F32)
    d, mo, vo = pl.pallas_call(
        body, out_shape=(out, out, out), grid=(r_dim // tr,), in_specs=[blk] * 4, out_specs=(blk, blk, blk),
        name=name, compiler_params=_params("parallel"))(two_d(w), two_d(g), two_d(m), two_d(v))
    return d.reshape(shape), mo.reshape(shape), vo.reshape(shape)


HBM_SPEC = pl.BlockSpec(memory_space=pltpu.HBM)


def _position():
    return lax.axis_index("x"), lax.axis_index("y"), lax.axis_index("c")


def _all_gather(shard, name):
    def body(x_ref, out_ref, send_sems, recv_sems, local_sem):
        x, y, c = _position()
        me, sibling = (x, y, c), (x, y, 1 - c)
        chips = [(1 - x, y), (x, 1 - y), (1 - x, 1 - y)]

        def rows(px, py, pc):
            return out_ref.at[4 * px + 2 * py + pc]

        def copy(k, block, to, src=None):
            return pltpu.make_async_remote_copy(
                src_ref=rows(*block) if src is None else src, dst_ref=rows(*block),
                send_sem=send_sems.at[k], recv_sem=recv_sems.at[k], device_id=to, device_id_type=MESH)

        mine = pltpu.make_async_copy(x_ref, rows(*me), local_sem)
        mine.start()
        first = [copy(0, me, sibling, src=x_ref)]
        first += [copy(1 + j, me, (*chip, c), src=x_ref) for j, chip in enumerate(chips)]
        for cp in first:
            cp.start()
        passed = [copy(4 + j, (*chip, c), sibling) for j, chip in enumerate(chips)]
        for j, chip in enumerate(chips):
            copy(1 + j, (*chip, c), me).wait_recv()
            passed[j].start()
        copy(0, sibling, me).wait_recv()
        for j, chip in enumerate(chips):
            copy(4 + j, (*chip, 1 - c), me).wait_recv()
        for cp in first + passed:
            cp.wait_send()
        mine.wait()

    return pl.pallas_call(
        body, out_shape=jax.ShapeDtypeStruct((N_DEV,) + shard.shape, shard.dtype),
        in_specs=[HBM_SPEC], out_specs=HBM_SPEC,
        scratch_shapes=[pltpu.SemaphoreType.DMA((7,)), pltpu.SemaphoreType.DMA((7,)), pltpu.SemaphoreType.DMA(())],
        name=name)(shard)


CHIP_RELATIONS = ((0, 0), (1, 0), (0, 1), (1, 1))


def _rs_pair_exchange(g, name):
    _, r_dim, c_dim = g.shape
    n = len(CHIP_RELATIONS)

    def body(g_ref, recv_ref, send_sems, recv_sems):
        x, y, c = _position()
        sibling = (x, y, 1 - c)
        copies = []
        for k, (rx, ry) in enumerate(CHIP_RELATIONS):
            px = x + rx - 2 * x * rx
            py = y + ry - 2 * y * ry
            copies.append(pltpu.make_async_remote_copy(
                src_ref=g_ref.at[4 * px + 2 * py + 1 - c], dst_ref=recv_ref.at[k], send_sem=send_sems.at[k],
                recv_sem=recv_sems.at[k], device_id=sibling, device_id_type=MESH))
        for cp in copies:
            cp.start()
        for cp in copies:
            cp.wait()

    return pl.pallas_call(
        body, out_shape=jax.ShapeDtypeStruct((n, r_dim, c_dim), g.dtype), in_specs=[HBM_SPEC], out_specs=HBM_SPEC,
        scratch_shapes=[pltpu.SemaphoreType.DMA((n,)), pltpu.SemaphoreType.DMA((n,))], name=name)(g)


def _rs_pair_sum(g, recv, name):
    _, r_dim, c_dim = g.shape
    n = len(CHIP_RELATIONS)
    tr = _tile(r_dim, 1200, SUBLANE)
    x, y, c = _position()
    own = jnp.stack([4 * (x + rx - 2 * x * rx) + 2 * (y + ry - 2 * y * ry) + c for rx, ry in CHIP_RELATIONS])

    def body(own_ref, g_ref, r_ref, o_ref):
        o_ref[...] = (g_ref[...].astype(F32) + r_ref[...].astype(F32)).astype(o_ref.dtype)

    blk = _bs((None, tr, c_dim), lambda k, i, own_ref: (k, i, 0))
    return pl.pallas_call(
        body, out_shape=jax.ShapeDtypeStruct((n, r_dim, c_dim), g.dtype),
        grid_spec=pltpu.PrefetchScalarGridSpec(
            num_scalar_prefetch=1, grid=(n, r_dim // tr),
            in_specs=[_bs((None, tr, c_dim), lambda k, i, own_ref: (own_ref[k], i, 0)), blk], out_specs=blk),
        name=name, compiler_params=_params("parallel", "parallel"))(own.astype(jnp.int32), g, recv)


SEM_SPEC = pl.BlockSpec(memory_space=pltpu.SEMAPHORE)
DATAFLOW = pltpu.SideEffectType.DATAFLOW_SIDE_EFFECTING
CHIP_FLIPS = CHIP_RELATIONS[1:]
TOKEN = jax.ShapeDtypeStruct((SUBLANE, LANE), F32)


def _flip(v, r):
    return v + r - 2 * v * r


def _chip_copies(src_ref, src_of, dst_ref, dst_of, send_sems, recv_sems):
    x, y, c = _position()
    me = 4 * x + 2 * y + c
    out = []
    for k, (rx, ry) in enumerate(CHIP_FLIPS):
        px, py = _flip(x, rx), _flip(y, ry)
        peer = 4 * px + 2 * py + c
        out.append(pltpu.make_async_remote_copy(
            src_ref=src_ref.at[src_of(k, me, peer)], dst_ref=dst_ref.at[dst_of(k, me, peer)],
            send_sem=send_sems.at[k], recv_sem=recv_sems.at[k], device_id=(px, py, c), device_id_type=MESH))
    return out


def _device_block(ref, spec, d):
    rows, axis = spec
    return ref.at[pl.ds(d * rows, rows)] if axis == 0 else ref.at[:, pl.ds(d * rows, rows)]


def _ag_chips_start(lands, specs, after, name):
    n = len(lands)
    nf = len(CHIP_FLIPS)

    def body(*refs):
        send_sems, recv_sems, token = refs[n + 1], refs[n + 2], refs[-1]
        x, y, c = _position()
        me = 4 * x + 2 * y + c
        for i, spec in enumerate(specs):
            blk = _device_block(refs[i], spec, me)
            for k, (rx, ry) in enumerate(CHIP_FLIPS):
                pltpu.make_async_remote_copy(
                    src_ref=blk, dst_ref=blk, send_sem=send_sems.at[nf * i + k], recv_sem=recv_sems.at[nf * i + k],
                    device_id=(_flip(x, rx), _flip(y, ry), c), device_id_type=MESH).start()
        token[...] = jnp.zeros(TOKEN.shape, TOKEN.dtype)

    sems = pltpu.SemaphoreType.DMA((nf * n,))
    return pl.pallas_call(
        body, name=name, out_shape=(sems, sems, *[pltpu.HBM(t.shape, t.dtype) for t in lands], TOKEN),
        in_specs=(HBM_SPEC,) * n + (ANY_SPEC,),
        out_specs=(SEM_SPEC, SEM_SPEC) + (HBM_SPEC,) * n + (pl.BlockSpec(memory_space=pltpu.VMEM),),
        input_output_aliases={i: 2 + i for i in range(n)}, compiler_params=pltpu.CompilerParams(has_side_effects=DATAFLOW),
    )(*[pltpu.with_memory_space_constraint(t, pltpu.HBM) for t in lands], after)


def _ag_chips_wait(send_sems, recv_sems, lands, specs, after, name):
    n = len(lands)
    nf = len(CHIP_FLIPS)

    def body(*refs):
        send_sems, recv_sems = refs[n], refs[n + 1]
        x, y, c = _position()
        me = 4 * x + 2 * y + c
        for i, spec in enumerate(specs):
            for k, (rx, ry) in enumerate(CHIP_FLIPS):
                px, py = _flip(x, rx), _flip(y, ry)
                cp = pltpu.make_async_remote_copy(
                    src_ref=_device_block(refs[i], spec, me), dst_ref=_device_block(refs[i], spec, 4 * px + 2 * py + c),
                    send_sem=send_sems.at[nf * i + k], recv_sem=recv_sems.at[nf * i + k],
                    device_id=(px, py, c), device_id_type=MESH)
                cp.wait_send()
                cp.wait_recv()

    return pl.pallas_call(
        body, name=name, out_shape=tuple(pltpu.HBM(t.shape, t.dtype) for t in lands),
        in_specs=(HBM_SPEC,) * n + (SEM_SPEC, SEM_SPEC, ANY_SPEC), out_specs=(HBM_SPEC,) * n,
        input_output_aliases={i: i for i in range(n)}, compiler_params=pltpu.CompilerParams(has_side_effects=DATAFLOW),
    )(*lands, send_sems, recv_sems, after)


def _ag_pair_forward(lands, specs, name):
    n = len(lands)
    nr = len(CHIP_RELATIONS)

    def body(*refs):
        outs, send_sems, recv_sems = refs[n:2 * n], refs[2 * n], refs[2 * n + 1]
        x, y, c = _position()
        copies = []
        for i, spec in enumerate(specs):
            for k, (rx, ry) in enumerate(CHIP_RELATIONS):
                chip = 4 * _flip(x, rx) + 2 * _flip(y, ry)
                held = _device_block(outs[i], spec, chip + c)
                sems = dict(send_sem=send_sems.at[nr * i + k], recv_sem=recv_sems.at[nr * i + k],
                            device_id=(x, y, 1 - c), device_id_type=MESH)
                mine = pltpu.make_async_remote_copy(src_ref=held, dst_ref=held, **sems)
                theirs = pltpu.make_async_remote_copy(src_ref=held, dst_ref=_device_block(outs[i], spec, chip + 1 - c), **sems)
                copies.append((mine, theirs))
        for mine, _ in copies:
            mine.start()
        for mine, theirs in copies:
            mine.wait_send()
            theirs.wait_recv()

    sems = pltpu.SemaphoreType.DMA((nr * n,))
    return pl.pallas_call(
        body, out_shape=tuple(jax.ShapeDtypeStruct(t.shape, t.dtype) for t in lands), in_specs=[HBM_SPEC] * n,
        out_specs=(HBM_SPEC,) * n, input_output_aliases={i: i for i in range(n)}, scratch_shapes=[sems, sems], name=name)(*lands)


def _rs_chips_start(pair, name):
    _, r_dim, c_dim = pair.shape
    n = len(CHIP_FLIPS)

    def body(pair_ref, far_ref, send_sems, recv_sems, pair_thru, far_thru, token):
        for cp in _chip_copies(pair_ref, lambda k, me, peer: k + 1, far_ref, lambda k, me, peer: k, send_sems, recv_sems):
            cp.start()
        token[...] = jnp.zeros(TOKEN.shape, TOKEN.dtype)

    far = lax.empty((n, r_dim, c_dim), pair.dtype)
    return pl.pallas_call(
        body, name=name,
        out_shape=(pltpu.SemaphoreType.DMA((n,)), pltpu.SemaphoreType.DMA((n,)), pltpu.HBM(pair.shape, pair.dtype),
                   pltpu.HBM(far.shape, far.dtype), TOKEN),
        in_specs=(HBM_SPEC, HBM_SPEC),
        out_specs=(SEM_SPEC, SEM_SPEC, HBM_SPEC, HBM_SPEC, pl.BlockSpec(memory_space=pltpu.VMEM)),
        input_output_aliases={0: 2, 1: 3}, compiler_params=pltpu.CompilerParams(has_side_effects=DATAFLOW),
    )(pltpu.with_memory_space_constraint(pair, pltpu.HBM), pltpu.with_memory_space_constraint(far, pltpu.HBM))


def _rs_chips_wait(send_sems, recv_sems, pair, far, after, name):
    def body(pair_ref, far_ref, send_sems, recv_sems, after_ref, pair_out, far_out):
        for cp in _chip_copies(pair_ref, lambda k, me, peer: k + 1, far_ref, lambda k, me, peer: k, send_sems, recv_sems):
            cp.wait_send()
            cp.wait_recv()

    return pl.pallas_call(
        body, name=name, out_shape=(pltpu.HBM(pair.shape, pair.dtype), pltpu.HBM(far.shape, far.dtype)),
        in_specs=(HBM_SPEC, HBM_SPEC, SEM_SPEC, SEM_SPEC, ANY_SPEC),
        out_specs=(HBM_SPEC, HBM_SPEC), input_output_aliases={0: 0, 1: 1},
        compiler_params=pltpu.CompilerParams(has_side_effects=DATAFLOW),
    )(pair, far, send_sems, recv_sems, after)


def _rs_final_sum(pair, far, name):
    _, r_dim, c_dim = pair.shape
    tr = _tile(r_dim, 1200, SUBLANE)

    def body(p_ref, f0_ref, f1_ref, f2_ref, o_ref):
        o_ref[...] = ((p_ref[...].astype(F32) + f0_ref[...].astype(F32)) + f1_ref[...].astype(F32)) + f2_ref[...].astype(F32)

    def slot(k):
        return _bs((None, tr, c_dim), lambda i: (k, i, 0))

    return pl.pallas_call(
        body, out_shape=jax.ShapeDtypeStruct((r_dim, c_dim), F32), grid=(r_dim // tr,),
        in_specs=[slot(0), slot(0), slot(1), slot(2)], out_specs=_bs((tr, c_dim), lambda i: (i, 0)), name=name,
        compiler_params=_params("parallel"))(pair, far, far, far)


def _reduce_scatter_begin(g, name):
    recv = _rs_pair_exchange(g, name + "_pair")
    pair = _rs_pair_sum(g, recv, name + "_pairsum")
    return _rs_chips_start(pair, name + "_chips_start")


def _reduce_scatter_end(state, after, name):
    send_sems, recv_sems, pair, far, _ = state
    pair, far = _rs_chips_wait(send_sems, recv_sems, pair, far, after, name + "_chips_wait")
    return _rs_final_sum(pair, far, name + "_sum")


MATRICES = (("w_in", True), ("w_out", False), ("w_q", False), ("w_kv", True), ("w_o", False), ("w_up", True),
            ("w_down", False), ("w_conv_out", True), ("w_pool_grp", True))
MIX_NAMES = ("w_in", "w_conv_out", "w_pool_grp", "w_out")
REST_NAMES = ("w_q", "w_kv", "w_o", "w_up", "w_down")


def _parts(layer):
    return (("mix", MIX_NAMES), ("rest", REST_NAMES)) if layer == 0 else (("all", MIX_NAMES + REST_NAMES),)


def _to_rows(name, transposed, w, d_model):
    if name == "w_pool_grp":
        w = jnp.swapaxes(w, 1, 2)
    elif transposed:
        w = w.T
    return w.reshape(-1, d_model)


def _from_rows(name, transposed, rows, shard_shape):
    if name == "w_pool_grp":
        g, i, o = shard_shape
        return jnp.swapaxes(rows.reshape(g, o, i), 1, 2)
    if transposed:
        return rows.reshape(shard_shape[1], shard_shape[0]).T
    return rows.reshape(shard_shape)


def _scatter_blocks(name, full, shard_shape, d_model, n_dev=N_DEV):
    if name == "w_pool_grp":
        g, i, o = shard_shape
        return jnp.swapaxes(full.reshape(g, n_dev, o, i), 0, 1).reshape(n_dev, -1, d_model)
    return full.reshape(n_dev, -1, d_model)


def kernel(x, mem, mix_norm_g, w_in, conv_dw_w, conv_dw_b, conv_ln_g, conv_ln_b, w_conv_out, w_pool_grp, pool_scale, w_out, xattn_norm_g, mem_norm_g, w_q, w_kv, w_o, ffn_norm_g, w_up, ffn_dw_w, w_down, final_norm_g, loss_target, m_mix_norm_g, m_w_in, m_conv_dw_w, m_conv_dw_b, m_conv_ln_g, m_conv_ln_b, m_w_conv_out, m_w_pool_grp, m_pool_scale, m_w_out, m_xattn_norm_g, m_mem_norm_g, m_w_q, m_w_kv, m_w_o, m_ffn_norm_g, m_w_up, m_ffn_dw_w, m_w_down, m_final_norm_g, v_mix_norm_g, v_w_in, v_conv_dw_w, v_conv_dw_b, v_conv_ln_g, v_conv_ln_b, v_w_conv_out, v_w_pool_grp, v_pool_scale, v_w_out, v_xattn_norm_g, v_mem_norm_g, v_w_q, v_w_kv, v_w_o, v_ffn_norm_g, v_w_up, v_ffn_dw_w, v_w_down, v_final_norm_g):
    p = dict(locals())
    weight_names = ["mix_norm_g", "w_in", "conv_dw_w", "conv_dw_b", "conv_ln_g", "conv_ln_b", "w_conv_out",
                    "w_pool_grp", "pool_scale", "w_out", "xattn_norm_g", "mem_norm_g", "w_q", "w_kv", "w_o",
                    "ffn_norm_g", "w_up", "ffn_dw_w", "w_down", "final_norm_g"]
    n_batch, seq, d_model = x.shape
    m_len = mem.shape[1]
    depth = w_in.shape[0]
    assert depth == 2, "the exchange schedule below is written for two layers"
    t_dim = n_batch * seq
    c_conv = conv_dw_b.shape[1]
    n_groups = w_pool_grp.shape[1]
    assert w_pool_grp.shape[2] == LANE and c_conv % LANE == 0 and n_groups * LANE == c_conv
    gate_col0 = 2 * c_conv + n_groups * LANE
    pool_col0 = (2 * c_conv) // LANE

    dev = 4 * lax.axis_index("x") + 2 * lax.axis_index("y") + lax.axis_index("c")
    filt = jnp.concatenate([conv_dw_w.reshape(-1), ffn_dw_w.reshape(-1)])
    filt_rows = lax.bitcast_convert_type(filt, BF16).reshape(-1, d_model)
    transposed = dict(MATRICES)
    layout = {part: [(name, transposed[name], _to_rows(name, transposed[name], p[name][0], d_model).shape[0])
                     for name in names] for l in range(depth) for part, names in _parts(l)}
    part_of = {(l, name): part for l in range(depth) for part, names in _parts(l) for name in names}

    def landing(name, shard):
        if name == "w_pool_grp":
            block, axis = jnp.swapaxes(shard, 1, 2), 1
        elif name == "filt":
            block, axis = shard, 0
        else:
            block, axis = (shard.T if transposed[name] else shard), 0
        block = block.astype(BF16)
        rows = block.shape[axis]
        shape = block.shape[:axis] + (N_DEV * rows,) + block.shape[axis + 1:]
        start = (0,) * axis + (dev * rows,) + (0,) * (block.ndim - axis - 1)
        return lax.dynamic_update_slice(lax.empty(shape, BF16), block, start), (rows, axis)

    ag_state = {}
    after = filt_rows
    for l in range(depth):
        for part, names in _parts(l):
            items = [(name, p[name][l]) for name in names]
            if (l, part) == (0, part_of[(0, "w_in")]):
                items.append(("filt", filt_rows))
            lands, specs = zip(*[landing(name, shard) for name, shard in items])
            out = _ag_chips_start(lands, specs, after, f"ag{l}{part}_chips_start")
            ag_state[(l, part)] = ([name for name, _ in items], specs, out)
            after = out[-1]
    all_started = after

    full = [dict() for _ in range(depth)]

    def ensure(l, name, after):
        if name in full[l]:
            return
        part = part_of[(l, name)]
        names, specs, out = ag_state[(l, part)]
        lands = _ag_chips_wait(out[0], out[1], out[2:-1], specs, after, f"ag{l}{part}_chips_wait")
        lands = _ag_pair_forward(lands, specs, f"ag{l}{part}_pair_forward")
        full[l].update(zip(names, lands))

    vec = lambda a: a.reshape(1, -1)
    x2d = x.reshape(t_dim, d_model)
    mem2d = mem.reshape(n_batch * m_len, d_model)
    mem_n = _rmsnorm_fwd(mem2d, vec(mem_norm_g), "mem_norm", after=all_started)
    h_first = _rmsnorm_fwd(x2d, vec(mix_norm_g[0]), "mix_norm_l0", after=mem_n)
    ensure(0, "w_in", h_first)
    filt_all = lax.bitcast_convert_type(full[0]["filt"].reshape(N_DEV, -1, 2), F32)
    n_cw = conv_dw_w.size
    kc, cs = conv_dw_w.shape[1:]
    kf, fs = ffn_dw_w.shape[1:]
    conv_w_full = jnp.moveaxis(filt_all[:, :n_cw].reshape(N_DEV, depth, kc, cs), 0, 2).reshape(depth, kc, N_DEV * cs)
    ffn_w_full = jnp.moveaxis(filt_all[:, n_cw:].reshape(N_DEV, depth, kf, fs), 0, 2).reshape(depth, kf, N_DEV * fs)

    saved = []
    xc = x2d
    for l in range(depth):
        ensure(l, "w_in", xc)
        wl = full[l]
        s = {"x0": xc}
        s["h"] = h_first if l == 0 else _rmsnorm_fwd(xc, vec(mix_norm_g[l]), f"mix_norm_l{l}")
        s["proj"] = _matmul(s["h"], wl["w_in"], "nt", f"in_proj_l{l}", out_dtype=BF16)
        s["y1"] = _glu_conv_fwd(s["proj"], conv_w_full[l], vec(conv_dw_b[l]), n_batch, seq, f"glu_conv_l{l}")
        s["y3"] = _ln_silu_fwd(s["y1"], vec(conv_ln_g[l]), vec(conv_ln_b[l]), f"ln_silu_l{l}")
        s["yc"] = _matmul(s["y3"], wl["w_conv_out"], "nt", f"conv_out_l{l}", out_dtype=BF16)
        s["zp"] = _pool_fwd(s["proj"], pool_col0, n_groups, n_batch, seq, f"pool_l{l}")
        s["yp"] = _grouped(s["zp"], wl["w_pool_grp"], "nt", f"pool_proj_l{l}", out_dtype=BF16)
        s["merged"] = _merge_fwd(s["proj"], gate_col0, s["yc"], s["yp"], vec(pool_scale[l]), f"merge_l{l}")
        s["x1"] = _matmul(s["merged"], wl["w_out"], "nn", f"mix_out_l{l}", res=xc)
        ensure(l, "w_q", s["x1"])
        half_up = wl["w_up"].shape[0] // 2
        up_gate, up_val = (0, half_up), (half_up, half_up)
        s["hq"] = _rmsnorm_fwd(s["x1"], vec(xattn_norm_g[l]), f"xattn_norm_l{l}")
        s["q"] = _matmul(s["hq"], wl["w_q"], "nn", f"q_proj_l{l}", out_dtype=BF16)
        s["kv"] = _matmul(mem_n, wl["w_kv"], "nt", f"kv_proj_l{l}", out_dtype=BF16)
        s["att"] = _attn_fwd(s["q"], s["kv"], n_batch, seq, m_len, f"attn_l{l}")
        s["x2"] = _matmul(s["att"], wl["w_o"], "nn", f"attn_out_l{l}", res=s["x1"])
        s["hf"] = _rmsnorm_fwd(s["x2"], vec(ffn_norm_g[l]), f"ffn_norm_l{l}")
        s["up_g"] = _matmul(s["hf"], wl["w_up"], "nt", f"up_proj_gate_l{l}", out_dtype=BF16, b_window=up_gate)
        s["up_v"] = _matmul(s["hf"], wl["w_up"], "nt", f"up_proj_val_l{l}", out_dtype=BF16, b_window=up_val)
        s["act"] = _ffn_act_fwd(s["up_g"], s["up_v"], ffn_w_full[l], n_batch, seq, f"ffn_act_l{l}")
        xc = _matmul(s["act"], wl["w_down"], "nn", f"down_proj_l{l}", res=s["x2"])
        saved.append(s)

    dx, dxb, dg_final, loss_part = _loss_head(xc, vec(final_norm_g), loss_target.reshape(t_dim, d_model), "loss_head")

    small = {"final_norm_g": dg_final.reshape(-1)}
    big = [dict() for _ in range(depth)]
    rs_state = {}
    rs_after = loss_part

    def rs_begin(l, part):
        pack = lax.empty((N_DEV, sum(nrows for _, _, nrows in layout[part]), d_model), BF16)
        row0 = 0
        for name, _, nrows in layout[part]:
            pieces = big[l][name] if isinstance(big[l][name], tuple) else (big[l][name],)
            d0 = 0
            for piece in pieces:
                blocks = _scatter_blocks(name, piece, p[name].shape[1:], d_model, N_DEV // len(pieces)).astype(BF16)
                pack = lax.dynamic_update_slice(pack, blocks, (d0, row0, 0))
                d0 += blocks.shape[0]
            row0 += nrows
        rs_state[(l, part)] = _reduce_scatter_begin(pack, f"rs{l}{part}")
        return rs_state[(l, part)][4]

    dmem_n = None
    for l in reversed(range(depth)):
        wl, s = full[l], saved[l]
        sm = {}
        dact = _matmul(dxb, wl["w_down"], "nt", f"d_act_l{l}", out_dtype=BF16, after=rs_after)
        big[l]["w_down"] = _matmul(s["act"], dxb, "tn", f"d_w_down_l{l}", out_dtype=BF16)
        dup_g, dup_v, dwf_g, dwf_v = _ffn_act_bwd(s["up_g"], s["up_v"], ffn_w_full[l], dact, n_batch, seq,
                                                  f"ffn_act_bwd_l{l}")
        sm["ffn_dw_w"] = jnp.concatenate([dwf_g, dwf_v], axis=1)
        dx, dxb, dg = _matmul_rmsnorm_bwd((dup_g, dup_v), wl["w_up"], "nn", s["x2"], vec(ffn_norm_g[l]), dx,
                                          f"d_hf_ffn_norm_bwd_l{l}")
        big[l]["w_up"] = (_matmul(dup_g, s["hf"], "tn", f"d_w_up_gate_l{l}", out_dtype=BF16),
                          _matmul(dup_v, s["hf"], "tn", f"d_w_up_val_l{l}", out_dtype=BF16))
        sm["ffn_norm_g"] = dg
        datt = _matmul(dxb, wl["w_o"], "nt", f"d_att_l{l}", out_dtype=BF16, after=rs_after)
        big[l]["w_o"] = _matmul(s["att"], dxb, "tn", f"d_w_o_l{l}", out_dtype=BF16)
        dq, dk, dv = _attn_bwd(s["q"], s["kv"], datt, n_batch, seq, m_len, f"attn_bwd_l{l}")
        dkv = jnp.concatenate([dk, dv], axis=1)
        big[l]["w_kv"] = _matmul(dkv, mem_n, "tn", f"d_w_kv_l{l}", out_dtype=BF16)
        dmem_n = _matmul(dkv, wl["w_kv"], "nn", f"d_mem_l{l}", res=dmem_n)
        big[l]["w_q"] = _matmul(s["hq"], dq, "tn", f"d_w_q_l{l}", out_dtype=BF16)
        dx, dxb, dg = _matmul_rmsnorm_bwd(dq, wl["w_q"], "nt", s["x1"], vec(xattn_norm_g[l]), dx,
                                          f"d_hq_xattn_norm_bwd_l{l}")
        sm["xattn_norm_g"] = dg
        if part_of[(l, "w_q")] != part_of[(l, "w_in")]:
            rs_after = rs_begin(l, part_of[(l, "w_q")])
        dmerged = _matmul(dxb, wl["w_out"], "nt", f"d_merged_l{l}", out_dtype=BF16, after=rs_after)
        big[l]["w_out"] = _matmul(s["merged"], dxb, "tn", f"d_w_out_l{l}", out_dtype=BF16)
        dgc, dgp, dyc, dyp, dscale = _merge_bwd(s["proj"], gate_col0, s["yc"], s["yp"], vec(pool_scale[l]), dmerged,
                                                f"merge_bwd_l{l}")
        sm["pool_scale"] = dscale
        dzp = _grouped(dyp, wl["w_pool_grp"], "nn", f"d_zp_l{l}", out_dtype=BF16)
        big[l]["w_pool_grp"] = _grouped_tn(dyp, s["zp"], n_groups, f"d_w_pool_l{l}")
        du = _pool_bwd(dzp, n_groups, n_batch, seq, f"pool_bwd_l{l}")
        dy3 = _matmul(dyc, wl["w_conv_out"], "nn", f"d_y3_l{l}", out_dtype=BF16)
        big[l]["w_conv_out"] = _matmul(dyc, s["y3"], "tn", f"d_w_conv_out_l{l}", out_dtype=BF16)
        dy1, dlg, dlb = _ln_silu_bwd(s["y1"], vec(conv_ln_g[l]), vec(conv_ln_b[l]), dy3, f"ln_silu_bwd_l{l}")
        sm["conv_ln_g"], sm["conv_ln_b"] = dlg, dlb
        da, dgl, dcw, dcb = _glu_conv_bwd(s["proj"], conv_w_full[l], dy1, n_batch, seq, f"glu_conv_bwd_l{l}")
        sm["conv_dw_w"], sm["conv_dw_b"] = dcw, dcb
        dproj = jnp.concatenate([da, dgl, du, dgc, dgp], axis=1)
        big[l]["w_in"] = _matmul(dproj, s["h"], "tn", f"d_w_in_l{l}", out_dtype=BF16)
        dx, dxb, dg = _matmul_rmsnorm_bwd(dproj, wl["w_in"], "nn", s["x0"], vec(mix_norm_g[l]), dx,
                                          f"d_h_mix_norm_bwd_l{l}")
        sm["mix_norm_g"] = dg
        for k, val in sm.items():
            small[(l, k)] = val.reshape(-1)
        rs_after = rs_begin(l, part_of[(l, "w_in")])
    _, _, dg_mem = _rmsnorm_bwd(mem2d, vec(mem_norm_g), dmem_n, None, "mem_norm_bwd")
    small["mem_norm_g"] = dg_mem.reshape(-1)
    small["loss"] = loss_part.reshape(-1)

    grads = {}
    per_layer = {name: [None] * depth for name, _ in MATRICES}
    for l in reversed(range(depth)):
        for part, _ in reversed(_parts(l)):
            mat_grads = _reduce_scatter_end(rs_state[(l, part)], rs_after, f"rs{l}{part}")
            row0 = 0
            for name, tr, nrows in layout[part]:
                per_layer[name][l] = _from_rows(name, tr, mat_grads[row0:row0 + nrows], p[name].shape[1:])
                row0 += nrows
    for name, _ in MATRICES:
        grads[name] = jnp.stack(per_layer[name])

    keys = list(small.keys())
    flat = jnp.concatenate([small[k] for k in keys])
    n_small = flat.shape[0]
    rows_small = -(-n_small // (SUBLANE * d_model)) * SUBLANE
    flat = jnp.pad(flat, (0, rows_small * d_model - n_small)).reshape(rows_small, d_model)
    every = _all_gather(flat, "small_all_gather")
    total = _sum_rows([every[i] for i in range(N_DEV)], F32, "small_sum").reshape(-1)
    off = 0
    red = {}
    for k in keys:
        red[k] = total[off:off + small[k].shape[0]]
        off += small[k].shape[0]
    loss = red["loss"][0]
    for name in ("mix_norm_g", "conv_dw_b", "conv_ln_g", "conv_ln_b", "pool_scale", "xattn_norm_g", "ffn_norm_g"):
        grads[name] = jnp.stack([red[(l, name)] for l in range(depth)])
    grads["conv_dw_w"] = jnp.stack([
        lax.dynamic_slice_in_dim(red[(l, "conv_dw_w")].reshape(kc, N_DEV * cs), dev * cs, cs, axis=1)
        for l in range(depth)])
    grads["ffn_dw_w"] = jnp.stack([
        lax.dynamic_slice_in_dim(red[(l, "ffn_dw_w")].reshape(kf, N_DEV * fs), dev * fs, fs, axis=1)
        for l in range(depth)])
    grads["mem_norm_g"] = red["mem_norm_g"]
    grads["final_norm_g"] = red["final_norm_g"]

    deltas, new_m, new_v = {}, {}, {}
    for name in weight_names:
        deltas[name], new_m[name], new_v[name] = _adamw(p[name], grads[name], p["m_" + name], p["v_" + name],
                                                        f"adamw_{name}")
    grad_x = dx.reshape(n_batch, seq, d_model)
    return (loss, grad_x, *[grads[n] for n in weight_names], *[deltas[n] for n in weight_names],
            *[new_m[n] for n in weight_names], *[new_v[n] for n in weight_names])
```

```python
import functools

import jax
import jax.numpy as jnp
from jax import lax
from jax.experimental import pallas as pl
from jax.experimental.pallas import tpu as pltpu

F32 = jnp.float32
BF16 = jnp.bfloat16
MESH = pl.DeviceIdType.MESH

N_DEV = 8
EPS = 1e-6
V7X_VMEM_BYTES = 64 * 1024 * 1024
VMEM_LIMIT = (V7X_VMEM_BYTES * 3) // 4
LANE = 128
SUBLANE = 8

CONV_HALO = 32
POOL_HALO = 16
FFN_HALO = 8
POOL_WINDOW_MAX = 16
XA_HEADS = 4

ADAM_LR = 0.001
ADAM_B1 = 0.9
ADAM_B2 = 0.999
ADAM_EPS = 1e-08
ADAM_WD = 0.01
ADAM_STEP = 10

GELU_C0 = 0.7978845608028654
GELU_C1 = 0.044715


ANY_SPEC = pl.BlockSpec(memory_space=pl.ANY)


def _tile(n, cap, mult=LANE):
    if n <= cap:
        return n
    best = None
    for d in range(mult, cap + 1, mult):
        if n % d == 0:
            best = d
    assert best is not None, (n, cap, mult)
    return best


def _params(*sem):
    return pltpu.CompilerParams(dimension_semantics=sem, vmem_limit_bytes=VMEM_LIMIT)


def _delayed(x, halo, rows, n_shifts):
    for r in range(min(SUBLANE, n_shifts)):
        xr = x if r == 0 else pltpu.roll(x, r, 0)
        for s in range(r, n_shifts, SUBLANE):
            yield s, xr[halo - (s - r):halo - (s - r) + rows]


def _advanced(x, rows, n_shifts):
    for r in range(min(SUBLANE, n_shifts)):
        xr = x if r == 0 else pltpu.roll(x, x.shape[0] - r, 0)
        for s in range(r, n_shifts, SUBLANE):
            yield s, xr[s - r:s - r + rows]


def _sig(x):
    return 1.0 / (1.0 + jnp.exp(-x))


def _bs(shape, imap):
    return pl.BlockSpec(shape, imap)


def _mxu_tile(n, cap):
    if n <= cap:
        return n
    best = {mult: max((d for d in range(mult, cap + 1, mult) if n % d == 0), default=0) for mult in (2 * LANE, LANE)}
    assert best[LANE] > 0, (n, cap)
    return best[2 * LANE] if 2 * best[2 * LANE] >= best[LANE] else best[LANE]


def _matmul(a, b, mode, name, res=None, out_dtype=F32, after=None, b_window=None):
    b_row0, b_rows = b_window if b_window is not None else (0, b.shape[0])
    if mode == "tn":
        k_dim, m_dim = a.shape
        k2, n_dim = b_rows, b.shape[1]
    elif mode == "nn":
        m_dim, k_dim = a.shape
        k2, n_dim = b_rows, b.shape[1]
    else:
        m_dim, k_dim = a.shape
        n_dim, k2 = b_rows, b.shape[1]
    assert k_dim == k2, (name, a.shape, b.shape)
    size = lambda t: jnp.dtype(t).itemsize
    tm = _mxu_tile(m_dim, 2816 if mode == "tn" else 1024)
    tn = _mxu_tile(n_dim, 2816)
    fixed = tm * tn * (2 * size(out_dtype) + (2 * size(res.dtype) if res is not None else 0) + 4)
    for cap in (2816, 2048, 1792, 1024, 512):
        tk = _mxu_tile(k_dim, cap)
        if fixed + 2 * tk * (tm * size(a.dtype) + tn * size(b.dtype)) <= VMEM_LIMIT - 8 * 1024 * 1024:
            break
    nk = k_dim // tk
    use_acc = nk > 1 and out_dtype != F32
    if mode == "tn":
        a_spec, ca = _bs((tk, tm), lambda i, j, k: (k, i)), 0
    else:
        a_spec, ca = _bs((tm, tk), lambda i, j, k: (i, k)), 1
    if mode == "nt":
        assert b_row0 % tn == 0
        b_spec, cb = _bs((tn, tk), lambda i, j, k: (j + b_row0 // tn, k)), 1
    else:
        assert b_row0 % tk == 0
        b_spec, cb = _bs((tk, tn), lambda i, j, k: (k + b_row0 // tk, j)), 0
    dims = (((ca,), (cb,)), ((), ()))
    o_spec = _bs((tm, tn), lambda i, j, k: (i, j))
    has_res = res is not None

    def body(*refs):
        a_ref, b_ref = refs[:2]
        r_ref = refs[2] if has_res else None
        o_ref = refs[n_in]
        k = pl.program_id(2)
        part = lax.dot_general(a_ref[...].astype(BF16), b_ref[...].astype(BF16), dims,
                               preferred_element_type=F32)
        if nk == 1:
            if has_res:
                part = part + r_ref[...].astype(F32)
            o_ref[...] = part.astype(out_dtype)
            return
        acc = refs[-1] if use_acc else o_ref

        @pl.when(k == 0)
        def _():
            acc[...] = part + r_ref[...].astype(F32) if has_res else part

        @pl.when(k > 0)
        def _():
            acc[...] += part

        if use_acc:
            @pl.when(k == nk - 1)
            def _():
                o_ref[...] = acc[...].astype(out_dtype)

    in_specs = [a_spec, b_spec] + ([o_spec] if has_res else [])
    args = (a, b) + ((res,) if has_res else ())
    if after is not None:
        in_specs.append(ANY_SPEC)
        args += (after,)
    n_in = len(args)
    return pl.pallas_call(
        body, out_shape=jax.ShapeDtypeStruct((m_dim, n_dim), out_dtype),
        grid=(m_dim // tm, n_dim // tn, nk), in_specs=in_specs, out_specs=o_spec,
        scratch_shapes=[pltpu.VMEM((tm, tn), F32)] if use_acc else [], name=name,
        compiler_params=_params("parallel", "parallel", "arbitrary"))(*args)


def _grouped(a, w, mode, name, out_dtype=F32):
    t_dim = a.shape[0]
    g_dim, r_dim, c_dim = w.shape
    ka, no = (c_dim, r_dim) if mode == "nt" else (r_dim, c_dim)
    tm = _tile(t_dim, 2048)
    dims = (((1,), (1 if mode == "nt" else 0,)), ((), ()))

    def body(a_ref, w_ref, o_ref):
        o_ref[...] = lax.dot_general(a_ref[...].astype(BF16), w_ref[...].astype(BF16), dims,
                                     preferred_element_type=F32).astype(out_dtype)

    return pl.pallas_call(
        body, out_shape=jax.ShapeDtypeStruct((t_dim, g_dim * no), out_dtype),
        grid=(t_dim // tm, g_dim),
        in_specs=[_bs((tm, ka), lambda i, g: (i, g)), _bs((None, r_dim, c_dim), lambda i, g: (g, 0, 0))],
        out_specs=_bs((tm, no), lambda i, g: (i, g)), name=name,
        compiler_params=_params("parallel", "parallel"))(a, w)


def _grouped_tn(a, b, g_dim, name):
    t_dim = a.shape[0]
    ra = a.shape[1] // g_dim
    cb = b.shape[1] // g_dim
    tm = _tile(t_dim, 2048)
    nt = t_dim // tm

    def body(a_ref, b_ref, o_ref):
        part = lax.dot_general(a_ref[...].astype(BF16), b_ref[...].astype(BF16), (((0,), (0,)), ((), ())),
                               preferred_element_type=F32)

        @pl.when(pl.program_id(1) == 0)
        def _():
            o_ref[...] = part

        @pl.when(pl.program_id(1) > 0)
        def _():
            o_ref[...] += part

    return pl.pallas_call(
        body, out_shape=jax.ShapeDtypeStruct((g_dim, ra, cb), F32), grid=(g_dim, nt),
        in_specs=[_bs((tm, ra), lambda g, i: (i, g)), _bs((tm, cb), lambda g, i: (i, g))],
        out_specs=_bs((None, ra, cb), lambda g, i: (g, 0, 0)), name=name,
        compiler_params=_params("parallel", "arbitrary"))(a, b)


def _rmsnorm_fwd(x, g, name, after=None):
    t_dim, d = x.shape
    tm = _tile(t_dim, 512)

    def body(x_ref, g_ref, *rest):
        o_ref = rest[-1]
        xv = x_ref[...]
        r = lax.rsqrt(jnp.mean(xv * xv, axis=-1, keepdims=True) + EPS)
        o_ref[...] = (xv * r * g_ref[...]).astype(BF16)

    return pl.pallas_call(
        body, out_shape=jax.ShapeDtypeStruct((t_dim, d), BF16), grid=(t_dim // tm,),
        in_specs=[_bs((tm, d), lambda i: (i, 0)), _bs((1, d), lambda i: (0, 0))] + ([ANY_SPEC] if after is not None else []),
        out_specs=_bs((tm, d), lambda i: (i, 0)), name=name,
        compiler_params=_params("parallel"))(x, g, *([after] if after is not None else []))


def _rmsnorm_bwd(x, g, dh, dx_in, name):
    t_dim, d = x.shape
    tm = _tile(t_dim, 512)
    has_in = dx_in is not None

    def body(*refs):
        if has_in:
            x_ref, g_ref, dh_ref, di_ref, dx_ref, dxb_ref, dg_ref = refs
        else:
            x_ref, g_ref, dh_ref, dx_ref, dxb_ref, dg_ref = refs
        xv = x_ref[...]
        r = lax.rsqrt(jnp.mean(xv * xv, axis=-1, keepdims=True) + EPS)
        xh = xv * r
        dhv = dh_ref[...].astype(F32)
        dxh = dhv * g_ref[...]
        dx = r * (dxh - xh * jnp.mean(dxh * xh, axis=-1, keepdims=True))
        if has_in:
            dx = dx + di_ref[...]
        dx_ref[...] = dx
        dxb_ref[...] = dx.astype(BF16)
        part = jnp.sum(dhv * xh, axis=0, keepdims=True)

        @pl.when(pl.program_id(0) == 0)
        def _():
            dg_ref[...] = part

        @pl.when(pl.program_id(0) > 0)
        def _():
            dg_ref[...] += part

    row = _bs((tm, d), lambda i: (i, 0))
    vec = _bs((1, d), lambda i: (0, 0))
    args = (x, g, dh) + ((dx_in,) if has_in else ())
    return pl.pallas_call(
        body, out_shape=(jax.ShapeDtypeStruct((t_dim, d), F32), jax.ShapeDtypeStruct((t_dim, d), BF16),
                         jax.ShapeDtypeStruct((1, d), F32)),
        grid=(t_dim // tm,), in_specs=[row, vec, row] + ([row] if has_in else []),
        out_specs=(row, row, vec), name=name, compiler_params=_params("arbitrary"))(*args)


def _matmul_rmsnorm_bwd(a, b, mode, x, g, dx_in, name, res=None, b_window=None):
    pieces = a if isinstance(a, tuple) else (a,)
    n_p = len(pieces)
    b_row0, b_rows = b_window if b_window is not None else (0, b.shape[0])
    m_dim, k_piece = pieces[0].shape
    assert all(t.shape == pieces[0].shape for t in pieces)
    k_dim = n_p * k_piece
    d = x.shape[1]
    assert (b_rows, b.shape[1]) == ((k_dim, d) if mode == "nn" else (d, k_dim)), (name, pieces[0].shape, b.shape)
    tm = _mxu_tile(m_dim, 512)
    tk = _mxu_tile(k_piece, 1792)
    nkp = k_piece // tk
    nk = n_p * nkp
    has_res = res is not None
    if mode == "nt":
        assert b_row0 == 0
        b_spec, cb = _bs((d, tk), lambda i, k: (0, k)), 1
    else:
        assert b_row0 % tk == 0
        b_spec, cb = _bs((tk, d), lambda i, k: (k + b_row0 // tk, 0)), 0
    dims = (((1,), (cb,)), ((), ()))

    def body(*refs):
        b_ref = refs[n_p]
        r_ref = refs[n_p + 1] if has_res else None
        x_ref, g_ref, di_ref, dx_ref, dxb_ref, dg_ref = refs[n_p + 1 + has_res:n_p + 7 + has_res]
        i, k = pl.program_id(0), pl.program_id(1)

        def finish(dhv):
            if has_res:
                dhv = dhv + r_ref[...].astype(F32)
            xv = x_ref[...]
            r = lax.rsqrt(jnp.mean(xv * xv, axis=-1, keepdims=True) + EPS)
            xh = xv * r
            dxh = dhv * g_ref[...]
            dx = r * (dxh - xh * jnp.mean(dxh * xh, axis=-1, keepdims=True)) + di_ref[...]
            dx_ref[...] = dx
            dxb_ref[...] = dx.astype(BF16)
            dg_part = jnp.sum(dhv * xh, axis=0, keepdims=True)

            @pl.when(i == 0)
            def _():
                dg_ref[...] = dg_part

            @pl.when(i > 0)
            def _():
                dg_ref[...] += dg_part

        def step(a_ref):
            part = lax.dot_general(a_ref[...].astype(BF16), b_ref[...].astype(BF16), dims, preferred_element_type=F32)
            if nk == 1:
                finish(part)
                return
            acc = refs[-1]

            @pl.when(k == 0)
            def _():
                acc[...] = part

            @pl.when(jnp.logical_and(k > 0, k < nk - 1))
            def _():
                acc[...] += part

            @pl.when(k == nk - 1)
            def _():
                finish(acc[...] + part)

        if n_p == 1:
            step(refs[0])
        else:
            for q in range(n_p):
                pl.when(jnp.logical_and(k >= q * nkp, k < (q + 1) * nkp))(functools.partial(step, refs[q]))

    row = _bs((tm, d), lambda i, k: (i, 0))
    vec = _bs((1, d), lambda i, k: (0, 0))
    a_specs = [_bs((tm, tk), lambda i, k, q=q: (i, jnp.clip(k - q * nkp, 0, nkp - 1))) for q in range(n_p)]
    in_specs = a_specs + [b_spec] + ([row] if has_res else []) + [row, vec, row]
    args = pieces + (b,) + ((res,) if has_res else ()) + (x, g, dx_in)
    return pl.pallas_call(
        body, out_shape=(jax.ShapeDtypeStruct((m_dim, d), F32), jax.ShapeDtypeStruct((m_dim, d), BF16),
                         jax.ShapeDtypeStruct((1, d), F32)),
        grid=(m_dim // tm, nk), in_specs=in_specs, out_specs=(row, row, vec),
        scratch_shapes=[pltpu.VMEM((tm, d), F32)] if nk > 1 else [], name=name,
        compiler_params=_params("arbitrary", "arbitrary"))(*args)


def _loss_head(x, g, tgt, name):
    t_dim, d = x.shape
    tm = _tile(t_dim, 512)

    def body(x_ref, g_ref, t_ref, dx_ref, dxb_ref, dg_ref, loss_ref):
        xv = x_ref[...]
        gv = g_ref[...]
        r = lax.rsqrt(jnp.mean(xv * xv, axis=-1, keepdims=True) + EPS)
        xh = xv * r
        err = xh * gv - t_ref[...]
        dy = err * (1.0 / d)
        dxh = dy * gv
        dx = r * (dxh - xh * jnp.mean(dxh * xh, axis=-1, keepdims=True))
        dx_ref[...] = dx
        dxb_ref[...] = dx.astype(BF16)
        dg_part = jnp.sum(dy * xh, axis=0, keepdims=True)
        loss_part = jnp.full((1, LANE), 0.5 * jnp.sum(jnp.mean(err * err, axis=-1, keepdims=True)), F32)

        @pl.when(pl.program_id(0) == 0)
        def _():
            dg_ref[...] = dg_part
            loss_ref[...] = loss_part

        @pl.when(pl.program_id(0) > 0)
        def _():
            dg_ref[...] += dg_part
            loss_ref[...] += loss_part

    row = _bs((tm, d), lambda i: (i, 0))
    vec = _bs((1, d), lambda i: (0, 0))
    return pl.pallas_call(
        body, out_shape=(jax.ShapeDtypeStruct((t_dim, d), F32), jax.ShapeDtypeStruct((t_dim, d), BF16),
                         jax.ShapeDtypeStruct((1, d), F32), jax.ShapeDtypeStruct((1, LANE), F32)),
        grid=(t_dim // tm,), in_specs=[row, vec, row],
        out_specs=(row, row, vec, _bs((1, LANE), lambda i: (0, 0))), name=name,
        compiler_params=_params("arbitrary"))(x, g, tgt)


def _glu_conv_fwd(proj, dw_w, dw_b, n_batch, seq, name):
    kk, cc = dw_w.shape
    nj = cc // LANE
    ch = min(256, seq)

    def body(a_ref, gl_ref, w_ref, b_ref, o_ref, pad):
        pad[0:CONV_HALO, :] = jnp.zeros((CONV_HALO, LANE), F32)
        pad[CONV_HALO:CONV_HALO + seq, :] = a_ref[...].astype(F32) * _sig(gl_ref[...].astype(F32))
        for c0 in range(0, seq, ch):
            acc = jnp.broadcast_to(b_ref[...], (ch, LANE))
            for k in range(kk):
                acc = acc + w_ref[k:k + 1, :] * pad[pl.ds(c0 + CONV_HALO - (kk - 1) + k, ch), :]
            o_ref[c0:c0 + ch, :] = acc

    return pl.pallas_call(
        body, out_shape=jax.ShapeDtypeStruct((n_batch * seq, cc), F32), grid=(n_batch, nj),
        in_specs=[_bs((seq, LANE), lambda b, j: (b, j)), _bs((seq, LANE), lambda b, j: (b, nj + j)),
                  _bs((kk, LANE), lambda b, j: (0, j)), _bs((1, LANE), lambda b, j: (0, j))],
        out_specs=_bs((seq, LANE), lambda b, j: (b, j)),
        scratch_shapes=[pltpu.VMEM((seq + CONV_HALO, LANE), F32)], name=name,
        compiler_params=_params("parallel", "parallel"))(proj, proj, dw_w, dw_b)


def _glu_conv_bwd(proj, dw_w, dy1, n_batch, seq, name):
    kk, cc = dw_w.shape
    nj = cc // LANE
    ch = min(256, seq)

    def body(a_ref, gl_ref, dy_ref, w_ref, da_ref, dgl_ref, dw_ref, db_ref, padf, padb):
        first = pl.program_id(1) == 0
        padf[0:CONV_HALO, :] = jnp.zeros((CONV_HALO, LANE), F32)
        padf[CONV_HALO:CONV_HALO + seq, :] = a_ref[...].astype(F32) * _sig(gl_ref[...].astype(F32))
        padb[0:seq, :] = dy_ref[...]
        padb[seq:seq + CONV_HALO, :] = jnp.zeros((CONV_HALO, LANE), F32)

        @pl.when(first)
        def _():
            dw_ref[...] = jnp.zeros((kk, LANE), F32)
            db_ref[...] = jnp.zeros((1, LANE), F32)

        dws = [jnp.zeros((1, LANE), F32) for _ in range(kk)]
        for c0 in range(0, seq, ch):
            acc = jnp.zeros((ch, LANE), F32)
            y0 = padf[CONV_HALO + c0:CONV_HALO + c0 + ch, :]
            for k in range(kk):
                win = padb[pl.ds(c0 + (kk - 1) - k, ch), :]
                acc = acc + w_ref[k:k + 1, :] * win
                dws[k] = dws[k] + jnp.sum(win * y0, axis=0, keepdims=True)
            sg = _sig(gl_ref[c0:c0 + ch, :].astype(F32))
            da_ref[c0:c0 + ch, :] = (acc * sg).astype(BF16)
            dgl_ref[c0:c0 + ch, :] = (acc * a_ref[c0:c0 + ch, :].astype(F32) * sg * (1.0 - sg)).astype(BF16)
        for k in range(kk):
            dw_ref[k:k + 1, :] += dws[k]
        db_ref[...] += jnp.sum(dy_ref[...], axis=0, keepdims=True)

    tok = _bs((seq, LANE), lambda j, b: (b, j))
    t_dim = n_batch * seq
    return pl.pallas_call(
        body, out_shape=(jax.ShapeDtypeStruct((t_dim, cc), BF16), jax.ShapeDtypeStruct((t_dim, cc), BF16),
                         jax.ShapeDtypeStruct((kk, cc), F32), jax.ShapeDtypeStruct((1, cc), F32)),
        grid=(nj, n_batch),
        in_specs=[tok, _bs((seq, LANE), lambda j, b: (b, nj + j)), tok, _bs((kk, LANE), lambda j, b: (0, j))],
        out_specs=(tok, tok, _bs((kk, LANE), lambda j, b: (0, j)), _bs((1, LANE), lambda j, b: (0, j))),
        scratch_shapes=[pltpu.VMEM((seq + CONV_HALO, LANE), F32), pltpu.VMEM((seq + CONV_HALO, LANE), F32)],
        name=name, compiler_params=_params("parallel", "arbitrary"))(proj, proj, dy1, dw_w)


def _ln_silu_fwd(y1, g, b, name):
    t_dim, c = y1.shape
    tm = _tile(t_dim, 512)

    def body(y_ref, g_ref, b_ref, o_ref):
        yv = y_ref[...]
        xc = yv - jnp.mean(yv, axis=-1, keepdims=True)
        rstd = lax.rsqrt(jnp.mean(xc * xc, axis=-1, keepdims=True) + EPS)
        y2 = xc * rstd * g_ref[...] + b_ref[...]
        o_ref[...] = (y2 * _sig(y2)).astype(BF16)

    row = _bs((tm, c), lambda i: (i, 0))
    vec = _bs((1, c), lambda i: (0, 0))
    return pl.pallas_call(
        body, out_shape=jax.ShapeDtypeStruct((t_dim, c), BF16), grid=(t_dim // tm,),
        in_specs=[row, vec, vec], out_specs=row, name=name, compiler_params=_params("parallel"))(y1, g, b)


def _ln_silu_bwd(y1, g, b, dy3, name):
    t_dim, c = y1.shape
    tm = _tile(t_dim, 512)

    def body(y_ref, g_ref, b_ref, d_ref, dy_ref, dg_ref, db_ref):
        yv = y_ref[...]
        gv = g_ref[...]
        xc = yv - jnp.mean(yv, axis=-1, keepdims=True)
        rstd = lax.rsqrt(jnp.mean(xc * xc, axis=-1, keepdims=True) + EPS)
        yh = xc * rstd
        y2 = yh * gv + b_ref[...]
        s = _sig(y2)
        dy2 = d_ref[...].astype(F32) * (s * (1.0 + y2 * (1.0 - s)))
        dyh = dy2 * gv
        dy_ref[...] = rstd * (dyh - jnp.mean(dyh, axis=-1, keepdims=True)
                              - yh * jnp.mean(dyh * yh, axis=-1, keepdims=True))
        dg_part = jnp.sum(dy2 * yh, axis=0, keepdims=True)
        db_part = jnp.sum(dy2, axis=0, keepdims=True)

        @pl.when(pl.program_id(0) == 0)
        def _():
            dg_ref[...] = dg_part
            db_ref[...] = db_part

        @pl.when(pl.program_id(0) > 0)
        def _():
            dg_ref[...] += dg_part
            db_ref[...] += db_part

    row = _bs((tm, c), lambda i: (i, 0))
    vec = _bs((1, c), lambda i: (0, 0))
    return pl.pallas_call(
        body, out_shape=(jax.ShapeDtypeStruct((t_dim, c), F32), jax.ShapeDtypeStruct((1, c), F32),
                         jax.ShapeDtypeStruct((1, c), F32)),
        grid=(t_dim // tm,), in_specs=[row, vec, vec, row], out_specs=(row, vec, vec), name=name,
        compiler_params=_params("arbitrary"))(y1, g, b, dy3)


def _pool_fwd(proj, col0, n_groups, n_batch, seq, name):
    ch = min(256, seq)

    def body(u_ref, o_ref, pad):
        w = lax.shift_left(jnp.int32(2), pl.program_id(1))
        pad[0:POOL_HALO, :] = jnp.zeros((POOL_HALO, LANE), F32)
        pad[POOL_HALO:POOL_HALO + seq, :] = u_ref[...].astype(F32)
        for c0 in range(0, seq, ch):
            acc = jnp.zeros((ch, LANE), F32)
            for j in range(POOL_WINDOW_MAX):
                acc = acc + jnp.where(j < w, 1.0, 0.0).astype(F32) * pad[pl.ds(c0 + POOL_HALO - j, ch), :]
            t = c0 + lax.broadcasted_iota(jnp.int32, (ch, LANE), 0)
            cnt = jnp.minimum(t + 1, w).astype(F32)
            o_ref[c0:c0 + ch, :] = (acc / cnt - pad[POOL_HALO + c0:POOL_HALO + c0 + ch, :]).astype(BF16)

    return pl.pallas_call(
        body, out_shape=jax.ShapeDtypeStruct((n_batch * seq, n_groups * LANE), BF16), grid=(n_batch, n_groups),
        in_specs=[_bs((seq, LANE), lambda b, g: (b, col0 + g))], out_specs=_bs((seq, LANE), lambda b, g: (b, g)),
        scratch_shapes=[pltpu.VMEM((seq + POOL_HALO, LANE), F32)], name=name,
        compiler_params=_params("parallel", "parallel"))(proj)


def _pool_bwd(dzp, n_groups, n_batch, seq, name):
    ch = min(256, seq)

    def body(d_ref, o_ref, pad):
        w = lax.shift_left(jnp.int32(2), pl.program_id(1))
        for c0 in range(0, seq, ch):
            t = c0 + lax.broadcasted_iota(jnp.int32, (ch, LANE), 0)
            cnt = jnp.minimum(t + 1, w).astype(F32)
            pad[c0:c0 + ch, :] = d_ref[c0:c0 + ch, :].astype(F32) / cnt
        pad[seq:seq + POOL_HALO, :] = jnp.zeros((POOL_HALO, LANE), F32)
        for c0 in range(0, seq, ch):
            acc = jnp.zeros((ch, LANE), F32)
            for j in range(POOL_WINDOW_MAX):
                acc = acc + jnp.where(j < w, 1.0, 0.0).astype(F32) * pad[pl.ds(c0 + j, ch), :]
            o_ref[c0:c0 + ch, :] = (acc - d_ref[c0:c0 + ch, :].astype(F32)).astype(BF16)

    tok = _bs((seq, LANE), lambda b, g: (b, g))
    return pl.pallas_call(
        body, out_shape=jax.ShapeDtypeStruct((n_batch * seq, n_groups * LANE), BF16), grid=(n_batch, n_groups),
        in_specs=[tok], out_specs=tok, scratch_shapes=[pltpu.VMEM((seq + POOL_HALO, LANE), F32)], name=name,
        compiler_params=_params("parallel", "parallel"))(dzp)


def _merge_fwd(proj, col0, yc, yp, scale, name):
    t_dim, d = yc.shape
    half = d // 2
    tm = _tile(t_dim, 512)
    c0 = col0 // half

    def body(gc_ref, gp_ref, yc_ref, yp_ref, s_ref, o_ref):
        f32 = lambda r: r[...].astype(F32)
        o_ref[...] = (_sig(f32(gc_ref)) * f32(yc_ref) + _sig(f32(gp_ref)) * (f32(yp_ref) * s_ref[...])).astype(BF16)

    blk = _bs((tm, half), lambda i, j: (i, j))
    return pl.pallas_call(
        body, out_shape=jax.ShapeDtypeStruct((t_dim, d), BF16), grid=(t_dim // tm, 2),
        in_specs=[_bs((tm, half), lambda i, j: (i, c0 + j)), _bs((tm, half), lambda i, j: (i, c0 + 2 + j)),
                  blk, blk, _bs((1, half), lambda i, j: (0, j))],
        out_specs=blk, name=name, compiler_params=_params("parallel", "parallel"))(proj, proj, yc, yp, scale)


def _merge_bwd(proj, col0, yc, yp, scale, dm, name):
    t_dim, d = yc.shape
    half = d // 2
    tm = _tile(t_dim, 512)
    c0 = col0 // half

    def body(gc_ref, gp_ref, yc_ref, yp_ref, s_ref, dm_ref, dgc_ref, dgp_ref, dyc_ref, dyp_ref, ds_ref):
        dmv = dm_ref[...].astype(F32)
        sgc = _sig(gc_ref[...].astype(F32))
        sgp = _sig(gp_ref[...].astype(F32))
        sv = s_ref[...]
        ypre = yp_ref[...].astype(F32)
        dgc_ref[...] = (dmv * yc_ref[...].astype(F32) * sgc * (1.0 - sgc)).astype(BF16)
        dgp_ref[...] = (dmv * (ypre * sv) * sgp * (1.0 - sgp)).astype(BF16)
        dyc_ref[...] = (dmv * sgc).astype(BF16)
        dyp = dmv * sgp
        dyp_ref[...] = (dyp * sv).astype(BF16)
        part = jnp.sum(dyp * ypre, axis=0, keepdims=True)

        @pl.when(pl.program_id(1) == 0)
        def _():
            ds_ref[...] = part

        @pl.when(pl.program_id(1) > 0)
        def _():
            ds_ref[...] += part

    blk = _bs((tm, half), lambda j, i: (i, j))
    big = jax.ShapeDtypeStruct((t_dim, d), BF16)
    return pl.pallas_call(
        body, out_shape=(big, big, big, big, jax.ShapeDtypeStruct((1, d), F32)), grid=(2, t_dim // tm),
        in_specs=[_bs((tm, half), lambda j, i: (i, c0 + j)), _bs((tm, half), lambda j, i: (i, c0 + 2 + j)),
                  blk, blk, _bs((1, half), lambda j, i: (0, j)), blk],
        out_specs=(blk, blk, blk, blk, _bs((1, half), lambda j, i: (0, j))), name=name,
        compiler_params=_params("parallel", "arbitrary"))(proj, proj, yc, yp, scale, dm)


def _attn_fwd(q, kv, n_batch, seq, m_len, name):
    d = q.shape[1]
    hd = d // XA_HEADS
    tq = _tile(seq, 1024)
    nq = seq // tq
    scale = hd ** -0.5

    def body(q_ref, k_ref, v_ref, o_ref):
        sc = lax.dot_general(q_ref[...].astype(BF16), k_ref[...].astype(BF16), (((1,), (1,)), ((), ())),
                             preferred_element_type=F32) * scale
        p = jnp.exp(sc - jnp.max(sc, axis=-1, keepdims=True))
        pr = p / jnp.sum(p, axis=-1, keepdims=True)
        o_ref[...] = jnp.dot(pr.astype(BF16), v_ref[...].astype(BF16), preferred_element_type=F32).astype(BF16)

    return pl.pallas_call(
        body, out_shape=jax.ShapeDtypeStruct((n_batch * seq, d), BF16), grid=(n_batch, XA_HEADS, nq),
        in_specs=[_bs((tq, hd), lambda b, h, i: (b * nq + i, h)), _bs((m_len, hd), lambda b, h, i: (b, h)),
                  _bs((m_len, hd), lambda b, h, i: (b, XA_HEADS + h))],
        out_specs=_bs((tq, hd), lambda b, h, i: (b * nq + i, h)), name=name,
        compiler_params=_params("parallel", "parallel", "parallel"))(q, kv, kv)


def _attn_bwd(q, kv, datt, n_batch, seq, m_len, name):
    d = q.shape[1]
    hd = d // XA_HEADS
    tq = _tile(seq, 1024)
    nq = seq // tq
    scale = hd ** -0.5

    def body(q_ref, k_ref, v_ref, do_ref, dq_ref, dk_ref, dv_ref):
        qb = q_ref[...].astype(BF16)
        kb = k_ref[...].astype(BF16)
        vb = v_ref[...].astype(BF16)
        dob = do_ref[...].astype(BF16)
        sc = lax.dot_general(qb, kb, (((1,), (1,)), ((), ())), preferred_element_type=F32) * scale
        p = jnp.exp(sc - jnp.max(sc, axis=-1, keepdims=True))
        pr = p / jnp.sum(p, axis=-1, keepdims=True)
        dpr = lax.dot_general(dob, vb, (((1,), (1,)), ((), ())), preferred_element_type=F32)
        dsc = pr * (dpr - jnp.sum(dpr * pr, axis=-1, keepdims=True)) * scale
        dsb = dsc.astype(BF16)
        dq_ref[...] = jnp.dot(dsb, kb, preferred_element_type=F32).astype(BF16)
        dv_part = lax.dot_general(pr.astype(BF16), dob, (((0,), (0,)), ((), ())), preferred_element_type=F32)
        dk_part = lax.dot_general(dsb, qb, (((0,), (0,)), ((), ())), preferred_element_type=F32)

        @pl.when(pl.program_id(2) == 0)
        def _():
            dk_ref[...] = dk_part
            dv_ref[...] = dv_part

        @pl.when(pl.program_id(2) > 0)
        def _():
            dk_ref[...] += dk_part
            dv_ref[...] += dv_part

    qs = _bs((tq, hd), lambda b, h, i: (b * nq + i, h))
    ks = _bs((m_len, hd), lambda b, h, i: (b, h))
    return pl.pallas_call(
        body, out_shape=(jax.ShapeDtypeStruct((n_batch * seq, d), BF16), jax.ShapeDtypeStruct((n_batch * m_len, d), F32),
                         jax.ShapeDtypeStruct((n_batch * m_len, d), F32)),
        grid=(n_batch, XA_HEADS, nq),
        in_specs=[qs, ks, _bs((m_len, hd), lambda b, h, i: (b, XA_HEADS + h)), qs],
        out_specs=(qs, ks, ks), name=name,
        compiler_params=_params("parallel", "parallel", "arbitrary"))(q, kv, kv, datt)


def _gelu_parts(g):
    th = jnp.tanh(GELU_C0 * (g + GELU_C1 * g * g * g))
    return th, 0.5 * g * (1.0 + th)


def _ffn_act_fwd(up_g, up_v, dw_w, n_batch, seq, name):
    kk, c2 = dw_w.shape
    f_dim = c2 // 2
    wd = LANE
    nj = f_dim // wd
    ch = min(128, seq)

    def body(g_ref, v_ref, wg_ref, wv_ref, o_ref, padg, padv):
        for pad, src in ((padg, g_ref), (padv, v_ref)):
            pad[0:FFN_HALO, :] = jnp.zeros((FFN_HALO, wd), F32)
            pad[FFN_HALO:FFN_HALO + seq, :] = src[...].astype(F32)
        for c0 in range(0, seq, ch):
            gate = jnp.zeros((ch, wd), F32)
            val = jnp.zeros((ch, wd), F32)
            for k in range(kk):
                off = c0 + FFN_HALO - (kk - 1) + k
                gate = gate + wg_ref[k:k + 1, :] * padg[pl.ds(off, ch), :]
                val = val + wv_ref[k:k + 1, :] * padv[pl.ds(off, ch), :]
            o_ref[c0:c0 + ch, :] = (_gelu_parts(gate)[1] * val).astype(BF16)

    return pl.pallas_call(
        body, out_shape=jax.ShapeDtypeStruct((n_batch * seq, f_dim), BF16), grid=(n_batch, nj),
        in_specs=[_bs((seq, wd), lambda b, j: (b, j)), _bs((seq, wd), lambda b, j: (b, j)),
                  _bs((kk, wd), lambda b, j: (0, j)), _bs((kk, wd), lambda b, j: (0, nj + j))],
        out_specs=_bs((seq, wd), lambda b, j: (b, j)),
        scratch_shapes=[pltpu.VMEM((seq + FFN_HALO, wd), F32), pltpu.VMEM((seq + FFN_HALO, wd), F32)], name=name,
        compiler_params=_params("parallel", "parallel"))(up_g, up_v, dw_w, dw_w)


def _ffn_act_bwd(up_g, up_v, dw_w, dact, n_batch, seq, name):
    kk, c2 = dw_w.shape
    f_dim = c2 // 2
    wd = LANE
    nj = f_dim // wd
    ch = min(128, seq)

    def body(g_ref, v_ref, wg_ref, wv_ref, da_ref, dg_ref, dv_ref, dwg_ref, dwv_ref, padg, padv, pbg, pbv):
        for pad, src in ((padg, g_ref), (padv, v_ref)):
            pad[0:FFN_HALO, :] = jnp.zeros((FFN_HALO, wd), F32)
            pad[FFN_HALO:FFN_HALO + seq, :] = src[...].astype(F32)
        for pb in (pbg, pbv):
            pb[seq:seq + FFN_HALO, :] = jnp.zeros((FFN_HALO, wd), F32)

        @pl.when(pl.program_id(1) == 0)
        def _():
            dwg_ref[...] = jnp.zeros((kk, wd), F32)
            dwv_ref[...] = jnp.zeros((kk, wd), F32)

        for c0 in range(0, seq, ch):
            gate = jnp.zeros((ch, wd), F32)
            val = jnp.zeros((ch, wd), F32)
            for k in range(kk):
                off = c0 + FFN_HALO - (kk - 1) + k
                gate = gate + wg_ref[k:k + 1, :] * padg[pl.ds(off, ch), :]
                val = val + wv_ref[k:k + 1, :] * padv[pl.ds(off, ch), :]
            sq = gate * gate
            th = jnp.tanh(GELU_C0 * gate * (1.0 + GELU_C1 * sq))
            half = 0.5 * th + 0.5
            dgelu = half * (1.0 + gate * (GELU_C0 + 3.0 * GELU_C0 * GELU_C1 * sq) * (1.0 - th))
            dav = da_ref[c0:c0 + ch, :].astype(F32)
            pbg[c0:c0 + ch, :] = dav * val * dgelu
            pbv[c0:c0 + ch, :] = dav * (gate * half)
        for pb, pad, w_ref, d_ref, dw_ref in ((pbg, padg, wg_ref, dg_ref, dwg_ref), (pbv, padv, wv_ref, dv_ref, dwv_ref)):
            for c0 in range(0, seq, ch):
                acc = jnp.zeros((ch, wd), F32)
                for k in range(kk):
                    acc = acc + w_ref[k:k + 1, :] * pb[pl.ds(c0 + (kk - 1) - k, ch), :]
                d_ref[c0:c0 + ch, :] = acc.astype(BF16)
            for k in range(kk):
                s = jnp.zeros((1, wd), F32)
                for c0 in range(0, seq, ch):
                    s = s + jnp.sum(pb[c0:c0 + ch, :] * pad[pl.ds(c0 + FFN_HALO - (kk - 1) + k, ch), :],
                                    axis=0, keepdims=True)
                dw_ref[k:k + 1, :] += s

    t_dim = n_batch * seq
    tok = _bs((seq, wd), lambda j, b: (b, j))
    wblk = _bs((kk, wd), lambda j, b: (0, j))
    pad_shape = pltpu.VMEM((seq + FFN_HALO, wd), F32)
    return pl.pallas_call(
        body, out_shape=(jax.ShapeDtypeStruct((t_dim, f_dim), BF16), jax.ShapeDtypeStruct((t_dim, f_dim), BF16),
                         jax.ShapeDtypeStruct((kk, f_dim), F32), jax.ShapeDtypeStruct((kk, f_dim), F32)),
        grid=(nj, n_batch),
        in_specs=[tok, tok, wblk, _bs((kk, wd), lambda j, b: (0, nj + j)), tok],
        out_specs=(tok, tok, wblk, wblk), scratch_shapes=[pad_shape, pad_shape, pad_shape, pad_shape], name=name,
        compiler_params=_params("parallel", "arbitrary"))(up_g, up_v, dw_w, dw_w, dact)


def _sum_rows(parts, out_dtype, name):
    r_dim, c_dim = parts[0].shape
    tr = _tile(r_dim, 1200, SUBLANE)
    n = len(parts)

    def body(*refs):
        acc = refs[0][...].astype(F32)
        for r in refs[1:n]:
            acc = acc + r[...].astype(F32)
        refs[n][...] = acc.astype(out_dtype)

    blk = _bs((tr, c_dim), lambda i: (i, 0))
    return pl.pallas_call(
        body, out_shape=jax.ShapeDtypeStruct((r_dim, c_dim), out_dtype), grid=(r_dim // tr,),
        in_specs=[blk] * n, out_specs=blk, name=name, compiler_params=_params("parallel"))(*parts)


def _adamw(w, g, m, v, name):
    shape = w.shape
    c_dim = shape[-1]
    r_dim = w.size // c_dim
    two_d = lambda t: t.reshape(r_dim, c_dim)
    tr = _tile(r_dim, max(SUBLANE, (256 * 1024) // max(c_dim, LANE) // SUBLANE * SUBLANE), SUBLANE)
    c1 = 1.0 - ADAM_B1 ** ADAM_STEP
    c2 = 1.0 - ADAM_B2 ** ADAM_STEP

    def body(w_ref, g_ref, m_ref, v_ref, d_ref, mo_ref, vo_ref):
        gv = g_ref[...]
        mn = ADAM_B1 * m_ref[...] + (1.0 - ADAM_B1) * gv
        vn = ADAM_B2 * v_ref[...] + (1.0 - ADAM_B2) * (gv * gv)
        mo_ref[...] = mn
        vo_ref[...] = vn
        d_ref[...] = -ADAM_LR * ((mn / c1) / (jnp.sqrt(vn / c2) + ADAM_EPS) + ADAM_WD * w_ref[...])

    blk = _bs((tr, c_dim), lambda i: (i, 0))
    out = jax.ShapeDtypeStruct((r_dim, c_dim), F32)
    d, mo, vo = pl.pallas_call(
        body, out_shape=(out, out, out), grid=(r_dim // tr,), in_specs=[blk] * 4, out_specs=(blk, blk, blk),
        name=name, compiler_params=_params("parallel"))(two_d(w), two_d(g), two_d(m), two_d(v))
    return d.reshape(shape), mo.reshape(shape), vo.reshape(shape)


HBM_SPEC = pl.BlockSpec(memory_space=pltpu.HBM)


def _position():
    return lax.axis_index("x"), lax.axis_index("y"), lax.axis_index("c")


def _all_gather(shard, name):
    def body(x_ref, out_ref, send_sems, recv_sems, local_sem):
        x, y, c = _position()
        me, sibling = (x, y, c), (x, y, 1 - c)
        chips = [(1 - x, y), (x, 1 - y), (1 - x, 1 - y)]

        def rows(px, py, pc):
            return out_ref.at[4 * px + 2 * py + pc]

        def copy(k, block, to, src=None):
            return pltpu.make_async_remote_copy(
                src_ref=rows(*block) if src is None else src, dst_ref=rows(*block),
                send_sem=send_sems.at[k], recv_sem=recv_sems.at[k], device_id=to, device_id_type=MESH)

        mine = pltpu.make_async_copy(x_ref, rows(*me), local_sem)
        mine.start()
        first = [copy(0, me, sibling, src=x_ref)]
        first += [copy(1 + j, me, (*chip, c), src=x_ref) for j, chip in enumerate(chips)]
        for cp in first:
            cp.start()
        passed = [copy(4 + j, (*chip, c), sibling) for j, chip in enumerate(chips)]
        for j, chip in enumerate(chips):
            copy(1 + j, (*chip, c), me).wait_recv()
            passed[j].start()
        copy(0, sibling, me).wait_recv()
        for j, chip in enumerate(chips):
            copy(4 + j, (*chip, 1 - c), me).wait_recv()
        for cp in first + passed:
            cp.wait_send()
        mine.wait()

    return pl.pallas_call(
        body, out_shape=jax.ShapeDtypeStruct((N_DEV,) + shard.shape, shard.dtype),
        in_specs=[HBM_SPEC], out_specs=HBM_SPEC,
        scratch_shapes=[pltpu.SemaphoreType.DMA((7,)), pltpu.SemaphoreType.DMA((7,)), pltpu.SemaphoreType.DMA(())],
        name=name)(shard)


CHIP_RELATIONS = ((0, 0), (1, 0), (0, 1), (1, 1))


def _rs_pair_exchange(g, name):
    _, r_dim, c_dim = g.shape
    n = len(CHIP_RELATIONS)

    def body(g_ref, recv_ref, send_sems, recv_sems):
        x, y, c = _position()
        sibling = (x, y, 1 - c)
        copies = []
        for k, (rx, ry) in enumerate(CHIP_RELATIONS):
            px = x + rx - 2 * x * rx
            py = y + ry - 2 * y * ry
            copies.append(pltpu.make_async_remote_copy(
                src_ref=g_ref.at[4 * px + 2 * py + 1 - c], dst_ref=recv_ref.at[k], send_sem=send_sems.at[k],
                recv_sem=recv_sems.at[k], device_id=sibling, device_id_type=MESH))
        for cp in copies:
            cp.start()
        for cp in copies:
            cp.wait()

    return pl.pallas_call(
        body, out_shape=jax.ShapeDtypeStruct((n, r_dim, c_dim), g.dtype), in_specs=[HBM_SPEC], out_specs=HBM_SPEC,
        scratch_shapes=[pltpu.SemaphoreType.DMA((n,)), pltpu.SemaphoreType.DMA((n,))], name=name)(g)


def _rs_pair_sum(g, recv, name):
    _, r_dim, c_dim = g.shape
    n = len(CHIP_RELATIONS)
    tr = _tile(r_dim, 1200, SUBLANE)
    x, y, c = _position()
    own = jnp.stack([4 * (x + rx - 2 * x * rx) + 2 * (y + ry - 2 * y * ry) + c for rx, ry in CHIP_RELATIONS])

    def body(own_ref, g_ref, r_ref, o_ref):
        o_ref[...] = (g_ref[...].astype(F32) + r_ref[...].astype(F32)).astype(o_ref.dtype)

    blk = _bs((None, tr, c_dim), lambda k, i, own_ref: (k, i, 0))
    return pl.pallas_call(
        body, out_shape=jax.ShapeDtypeStruct((n, r_dim, c_dim), g.dtype),
        grid_spec=pltpu.PrefetchScalarGridSpec(
            num_scalar_prefetch=1, grid=(n, r_dim // tr),
            in_specs=[_bs((None, tr, c_dim), lambda k, i, own_ref: (own_ref[k], i, 0)), blk], out_specs=blk),
        name=name, compiler_params=_params("parallel", "parallel"))(own.astype(jnp.int32), g, recv)


SEM_SPEC = pl.BlockSpec(memory_space=pltpu.SEMAPHORE)
DATAFLOW = pltpu.SideEffectType.DATAFLOW_SIDE_EFFECTING
CHIP_FLIPS = CHIP_RELATIONS[1:]
TOKEN = jax.ShapeDtypeStruct((SUBLANE, LANE), F32)


def _flip(v, r):
    return v + r - 2 * v * r


def _chip_copies(src_ref, src_of, dst_ref, dst_of, send_sems, recv_sems):
    x, y, c = _position()
    me = 4 * x + 2 * y + c
    out = []
    for k, (rx, ry) in enumerate(CHIP_FLIPS):
        px, py = _flip(x, rx), _flip(y, ry)
        peer = 4 * px + 2 * py + c
        out.append(pltpu.make_async_remote_copy(
            src_ref=src_ref.at[src_of(k, me, peer)], dst_ref=dst_ref.at[dst_of(k, me, peer)],
            send_sem=send_sems.at[k], recv_sem=recv_sems.at[k], device_id=(px, py, c), device_id_type=MESH))
    return out


def _device_block(ref, spec, d):
    rows, axis = spec
    return ref.at[pl.ds(d * rows, rows)] if axis == 0 else ref.at[:, pl.ds(d * rows, rows)]


def _ag_chips_start(lands, specs, after, name):
    n = len(lands)
    nf = len(CHIP_FLIPS)

    def body(*refs):
        send_sems, recv_sems, token = refs[n + 1], refs[n + 2], refs[-1]
        x, y, c = _position()
        me = 4 * x + 2 * y + c
        for i, spec in enumerate(specs):
            blk = _device_block(refs[i], spec, me)
            for k, (rx, ry) in enumerate(CHIP_FLIPS):
                pltpu.make_async_remote_copy(
                    src_ref=blk, dst_ref=blk, send_sem=send_sems.at[nf * i + k], recv_sem=recv_sems.at[nf * i + k],
                    device_id=(_flip(x, rx), _flip(y, ry), c), device_id_type=MESH).start()
        token[...] = jnp.zeros(TOKEN.shape, TOKEN.dtype)

    sems = pltpu.SemaphoreType.DMA((nf * n,))
    return pl.pallas_call(
        body, name=name, out_shape=(sems, sems, *[pltpu.HBM(t.shape, t.dtype) for t in lands], TOKEN),
        in_specs=(HBM_SPEC,) * n + (ANY_SPEC,),
        out_specs=(SEM_SPEC, SEM_SPEC) + (HBM_SPEC,) * n + (pl.BlockSpec(memory_space=pltpu.VMEM),),
        input_output_aliases={i: 2 + i for i in range(n)}, compiler_params=pltpu.CompilerParams(has_side_effects=DATAFLOW),
    )(*[pltpu.with_memory_space_constraint(t, pltpu.HBM) for t in lands], after)


def _ag_chips_wait(send_sems, recv_sems, lands, specs, after, name):
    n = len(lands)
    nf = len(CHIP_FLIPS)

    def body(*refs):
        send_sems, recv_sems = refs[n], refs[n + 1]
        x, y, c = _position()
        me = 4 * x + 2 * y + c
        for i, spec in enumerate(specs):
            for k, (rx, ry) in enumerate(CHIP_FLIPS):
                px, py = _flip(x, rx), _flip(y, ry)
                cp = pltpu.make_async_remote_copy(
                    src_ref=_device_block(refs[i], spec, me), dst_ref=_device_block(refs[i], spec, 4 * px + 2 * py + c),
                    send_sem=send_sems.at[nf * i + k], recv_sem=recv_sems.at[nf * i + k],
                    device_id=(px, py, c), device_id_type=MESH)
                cp.wait_send()
                cp.wait_recv()

    return pl.pallas_call(
        body, name=name, out_shape=tuple(pltpu.HBM(t.shape, t.dtype) for t in lands),
        in_specs=(HBM_SPEC,) * n + (SEM_SPEC, SEM_SPEC, ANY_SPEC), out_specs=(HBM_SPEC,) * n,
        input_output_aliases={i: i for i in range(n)}, compiler_params=pltpu.CompilerParams(has_side_effects=DATAFLOW),
    )(*lands, send_sems, recv_sems, after)


def _ag_pair_forward(lands, specs, name):
    n = len(lands)
    nr = len(CHIP_RELATIONS)

    def body(*refs):
        outs, send_sems, recv_sems = refs[n:2 * n], refs[2 * n], refs[2 * n + 1]
        x, y, c = _position()
        copies = []
        for i, spec in enumerate(specs):
            for k, (rx, ry) in enumerate(CHIP_RELATIONS):
                chip = 4 * _flip(x, rx) + 2 * _flip(y, ry)
                held = _device_block(outs[i], spec, chip + c)
                sems = dict(send_sem=send_sems.at[nr * i + k], recv_sem=recv_sems.at[nr * i + k],
                            device_id=(x, y, 1 - c), device_id_type=MESH)
                mine = pltpu.make_async_remote_copy(src_ref=held, dst_ref=held, **sems)
                theirs = pltpu.make_async_remote_copy(src_ref=held, dst_ref=_device_block(outs[i], spec, chip + 1 - c), **sems)
                copies.append((mine, theirs))
        for mine, _ in copies:
            mine.start()
        for mine, theirs in copies:
            mine.wait_send()
            theirs.wait_recv()

    sems = pltpu.SemaphoreType.DMA((nr * n,))
    return pl.pallas_call(
        body, out_shape=tuple(jax.ShapeDtypeStruct(t.shape, t.dtype) for t in lands), in_specs=[HBM_SPEC] * n,
        out_specs=(HBM_SPEC,) * n, input_output_aliases={i: i for i in range(n)}, scratch_shapes=[sems, sems], name=name)(*lands)


def _rs_chips_start(pair, name):
    _, r_dim, c_dim = pair.shape
    n = len(CHIP_FLIPS)

    def body(pair_ref, far_ref, send_sems, recv_sems, pair_thru, far_thru, token):
        for cp in _chip_copies(pair_ref, lambda k, me, peer: k + 1, far_ref, lambda k, me, peer: k, send_sems, recv_sems):
            cp.start()
        token[...] = jnp.zeros(TOKEN.shape, TOKEN.dtype)

    far = lax.empty((n, r_dim, c_dim), pair.dtype)
    return pl.pallas_call(
        body, name=name,
        out_shape=(pltpu.SemaphoreType.DMA((n,)), pltpu.SemaphoreType.DMA((n,)), pltpu.HBM(pair.shape, pair.dtype),
                   pltpu.HBM(far.shape, far.dtype), TOKEN),
        in_specs=(HBM_SPEC, HBM_SPEC),
        out_specs=(SEM_SPEC, SEM_SPEC, HBM_SPEC, HBM_SPEC, pl.BlockSpec(memory_space=pltpu.VMEM)),
        input_output_aliases={0: 2, 1: 3}, compiler_params=pltpu.CompilerParams(has_side_effects=DATAFLOW),
    )(pltpu.with_memory_space_constraint(pair, pltpu.HBM), pltpu.with_memory_space_constraint(far, pltpu.HBM))


def _rs_chips_wait(send_sems, recv_sems, pair, far, after, name):
    def body(pair_ref, far_ref, send_sems, recv_sems, after_ref, pair_out, far_out):
        for cp in _chip_copies(pair_ref, lambda k, me, peer: k + 1, far_ref, lambda k, me, peer: k, send_sems, recv_sems):
            cp.wait_send()
            cp.wait_recv()

    return pl.pallas_call(
        body, name=name, out_shape=(pltpu.HBM(pair.shape, pair.dtype), pltpu.HBM(far.shape, far.dtype)),
        in_specs=(HBM_SPEC, HBM_SPEC, SEM_SPEC, SEM_SPEC, ANY_SPEC),
        out_specs=(HBM_SPEC, HBM_SPEC), input_output_aliases={0: 0, 1: 1},
        compiler_params=pltpu.CompilerParams(has_side_effects=DATAFLOW),
    )(pair, far, send_sems, recv_sems, after)


def _rs_final_sum(pair, far, name):
    _, r_dim, c_dim = pair.shape
    tr = _tile(r_dim, 1200, SUBLANE)

    def body(p_ref, f0_ref, f1_ref, f2_ref, o_ref):
        o_ref[...] = ((p_ref[...].astype(F32) + f0_ref[...].astype(F32)) + f1_ref[...].astype(F32)) + f2_ref[...].astype(F32)

    def slot(k):
        return _bs((None, tr, c_dim), lambda i: (k, i, 0))

    return pl.pallas_call(
        body, out_shape=jax.ShapeDtypeStruct((r_dim, c_dim), F32), grid=(r_dim // tr,),
        in_specs=[slot(0), slot(0), slot(1), slot(2)], out_specs=_bs((tr, c_dim), lambda i: (i, 0)), name=name,
        compiler_params=_params("parallel"))(pair, far, far, far)


def _reduce_scatter_begin(g, name):
    recv = _rs_pair_exchange(g, name + "_pair")
    pair = _rs_pair_sum(g, recv, name + "_pairsum")
    return _rs_chips_start(pair, name + "_chips_start")


def _reduce_scatter_end(state, after, name):
    send_sems, recv_sems, pair, far, _ = state
    pair, far = _rs_chips_wait(send_sems, recv_sems, pair, far, after, name + "_chips_wait")
    return _rs_final_sum(pair, far, name + "_sum")


MATRICES = (("w_in", True), ("w_out", False), ("w_q", False), ("w_kv", True), ("w_o", False), ("w_up", True),
            ("w_down", False), ("w_conv_out", True), ("w_pool_grp", True))
MIX_NAMES = ("w_in", "w_conv_out", "w_pool_grp", "w_out")
REST_NAMES = ("w_q", "w_kv", "w_o", "w_up", "w_down")


def _parts(layer):
    return (("mix", MIX_NAMES), ("rest", REST_NAMES)) if layer == 0 else (("all", MIX_NAMES + REST_NAMES),)


def _to_rows(name, transposed, w, d_model):
    if name == "w_pool_grp":
        w = jnp.swapaxes(w, 1, 2)
    elif transposed:
        w = w.T
    return w.reshape(-1, d_model)


def _from_rows(name, transposed, rows, shard_shape):
    if name == "w_pool_grp":
        g, i, o = shard_shape
        return jnp.swapaxes(rows.reshape(g, o, i), 1, 2)
    if transposed:
        return rows.reshape(shard_shape[1], shard_shape[0]).T
    return rows.reshape(shard_shape)


def _scatter_blocks(name, full, shard_shape, d_model, n_dev=N_DEV):
    if name == "w_pool_grp":
        g, i, o = shard_shape
        return jnp.swapaxes(full.reshape(g, n_dev, o, i), 0, 1).reshape(n_dev, -1, d_model)
    return full.reshape(n_dev, -1, d_model)


def kernel(x, mem, mix_norm_g, w_in, conv_dw_w, conv_dw_b, conv_ln_g, conv_ln_b, w_conv_out, w_pool_grp, pool_scale, w_out, xattn_norm_g, mem_norm_g, w_q, w_kv, w_o, ffn_norm_g, w_up, ffn_dw_w, w_down, final_norm_g, loss_target, m_mix_norm_g, m_w_in, m_conv_dw_w, m_conv_dw_b, m_conv_ln_g, m_conv_ln_b, m_w_conv_out, m_w_pool_grp, m_pool_scale, m_w_out, m_xattn_norm_g, m_mem_norm_g, m_w_q, m_w_kv, m_w_o, m_ffn_norm_g, m_w_up, m_ffn_dw_w, m_w_down, m_final_norm_g, v_mix_norm_g, v_w_in, v_conv_dw_w, v_conv_dw_b, v_conv_ln_g, v_conv_ln_b, v_w_conv_out, v_w_pool_grp, v_pool_scale, v_w_out, v_xattn_norm_g, v_mem_norm_g, v_w_q, v_w_kv, v_w_o, v_ffn_norm_g, v_w_up, v_ffn_dw_w, v_w_down, v_final_norm_g):
    p = dict(locals())
    weight_names = ["mix_norm_g", "w_in", "conv_dw_w", "conv_dw_b", "conv_ln_g", "conv_ln_b", "w_conv_out",
                    "w_pool_grp", "pool_scale", "w_out", "xattn_norm_g", "mem_norm_g", "w_q", "w_kv", "w_o",
                    "ffn_norm_g", "w_up", "ffn_dw_w", "w_down", "final_norm_g"]
    n_batch, seq, d_model = x.shape
    m_len = mem.shape[1]
    depth = w_in.shape[0]
    assert depth == 2, "the exchange schedule below is written for two layers"
    t_dim = n_batch * seq
    c_conv = conv_dw_b.shape[1]
    n_groups = w_pool_grp.shape[1]
    assert w_pool_grp.shape[2] == LANE and c_conv % LANE == 0 and n_groups * LANE == c_conv
    gate_col0 = 2 * c_conv + n_groups * LANE
    pool_col0 = (2 * c_conv) // LANE

    dev = 4 * lax.axis_index("x") + 2 * lax.axis_index("y") + lax.axis_index("c")
    filt = jnp.concatenate([conv_dw_w.reshape(-1), ffn_dw_w.reshape(-1)])
    filt_rows = lax.bitcast_convert_type(filt, BF16).reshape(-1, d_model)
    transposed = dict(MATRICES)
    layout = {part: [(name, transposed[name], _to_rows(name, transposed[name], p[name][0], d_model).shape[0])
                     for name in names] for l in range(depth) for part, names in _parts(l)}
    part_of = {(l, name): part for l in range(depth) for part, names in _parts(l) for name in names}

    def landing(name, shard):
        if name == "w_pool_grp":
            block, axis = jnp.swapaxes(shard, 1, 2), 1
        elif name == "filt":
            block, axis = shard, 0
        else:
            block, axis = (shard.T if transposed[name] else shard), 0
        block = block.astype(BF16)
        rows = block.shape[axis]
        shape = block.shape[:axis] + (N_DEV * rows,) + block.shape[axis + 1:]
        start = (0,) * axis + (dev * rows,) + (0,) * (block.ndim - axis - 1)
        return lax.dynamic_update_slice(lax.empty(shape, BF16), block, start), (rows, axis)

    ag_state = {}
    after = filt_rows
    for l in range(depth):
        for part, names in _parts(l):
            items = [(name, p[name][l]) for name in names]
            if (l, part) == (0, part_of[(0, "w_in")]):
                items.append(("filt", filt_rows))
            lands, specs = zip(*[landing(name, shard) for name, shard in items])
            out = _ag_chips_start(lands, specs, after, f"ag{l}{part}_chips_start")
            ag_state[(l, part)] = ([name for name, _ in items], specs, out)
            after = out[-1]
    all_started = after

    full = [dict() for _ in range(depth)]

    def ensure(l, name, after):
        if name in full[l]:
            return
        part = part_of[(l, name)]
        names, specs, out = ag_state[(l, part)]
        lands = _ag_chips_wait(out[0], out[1], out[2:-1], specs, after, f"ag{l}{part}_chips_wait")
        lands = _ag_pair_forward(lands, specs, f"ag{l}{part}_pair_forward")
        full[l].update(zip(names, lands))

    vec = lambda a: a.reshape(1, -1)
    x2d = x.reshape(t_dim, d_model)
    mem2d = mem.reshape(n_batch * m_len, d_model)
    mem_n = _rmsnorm_fwd(mem2d, vec(mem_norm_g), "mem_norm", after=all_started)
    h_first = _rmsnorm_fwd(x2d, vec(mix_norm_g[0]), "mix_norm_l0", after=mem_n)
    ensure(0, "w_in", h_first)
    filt_all = lax.bitcast_convert_type(full[0]["filt"].reshape(N_DEV, -1, 2), F32)
    n_cw = conv_dw_w.size
    kc, cs = conv_dw_w.shape[1:]
    kf, fs = ffn_dw_w.shape[1:]
    conv_w_full = jnp.moveaxis(filt_all[:, :n_cw].reshape(N_DEV, depth, kc, cs), 0, 2).reshape(depth, kc, N_DEV * cs)
    ffn_w_full = jnp.moveaxis(filt_all[:, n_cw:].reshape(N_DEV, depth, kf, fs), 0, 2).reshape(depth, kf, N_DEV * fs)

    saved = []
    xc = x2d
    for l in range(depth):
        ensure(l, "w_in", xc)
        wl = full[l]
        s = {"x0": xc}
        s["h"] = h_first if l == 0 else _rmsnorm_fwd(xc, vec(mix_norm_g[l]), f"mix_norm_l{l}")
        s["proj"] = _matmul(s["h"], wl["w_in"], "nt", f"in_proj_l{l}", out_dtype=BF16)
        s["y1"] = _glu_conv_fwd(s["proj"], conv_w_full[l], vec(conv_dw_b[l]), n_batch, seq, f"glu_conv_l{l}")
        s["y3"] = _ln_silu_fwd(s["y1"], vec(conv_ln_g[l]), vec(conv_ln_b[l]), f"ln_silu_l{l}")
        s["yc"] = _matmul(s["y3"], wl["w_conv_out"], "nt", f"conv_out_l{l}", out_dtype=BF16)
        s["zp"] = _pool_fwd(s["proj"], pool_col0, n_groups, n_batch, seq, f"pool_l{l}")
        s["yp"] = _grouped(s["zp"], wl["w_pool_grp"], "nt", f"pool_proj_l{l}", out_dtype=BF16)
        s["merged"] = _merge_fwd(s["proj"], gate_col0, s["yc"], s["yp"], vec(pool_scale[l]), f"merge_l{l}")
        s["x1"] = _matmul(s["merged"], wl["w_out"], "nn", f"mix_out_l{l}", res=xc)
        ensure(l, "w_q", s["x1"])
        half_up = wl["w_up"].shape[0] // 2
        up_gate, up_val = (0, half_up), (half_up, half_up)
        s["hq"] = _rmsnorm_fwd(s["x1"], vec(xattn_norm_g[l]), f"xattn_norm_l{l}")
        s["q"] = _matmul(s["hq"], wl["w_q"], "nn", f"q_proj_l{l}", out_dtype=BF16)
        s["kv"] = _matmul(mem_n, wl["w_kv"], "nt", f"kv_proj_l{l}", out_dtype=BF16)
        s["att"] = _attn_fwd(s["q"], s["kv"], n_batch, seq, m_len, f"attn_l{l}")
        s["x2"] = _matmul(s["att"], wl["w_o"], "nn", f"attn_out_l{l}", res=s["x1"])
        s["hf"] = _rmsnorm_fwd(s["x2"], vec(ffn_norm_g[l]), f"ffn_norm_l{l}")
        s["up_g"] = _matmul(s["hf"], wl["w_up"], "nt", f"up_proj_gate_l{l}", out_dtype=BF16, b_window=up_gate)
        s["up_v"] = _matmul(s["hf"], wl["w_up"], "nt", f"up_proj_val_l{l}", out_dtype=BF16, b_window=up_val)
        s["act"] = _ffn_act_fwd(s["up_g"], s["up_v"], ffn_w_full[l], n_batch, seq, f"ffn_act_l{l}")
        xc = _matmul(s["act"], wl["w_down"], "nn", f"down_proj_l{l}", res=s["x2"])
        saved.append(s)

    dx, dxb, dg_final, loss_part = _loss_head(xc, vec(final_norm_g), loss_target.reshape(t_dim, d_model), "loss_head")

    small = {"final_norm_g": dg_final.reshape(-1)}
    big = [dict() for _ in range(depth)]
    rs_state = {}
    rs_after = loss_part

    def rs_begin(l, part):
        pack = lax.empty((N_DEV, sum(nrows for _, _, nrows in layout[part]), d_model), BF16)
        row0 = 0
        for name, _, nrows in layout[part]:
            pieces = big[l][name] if isinstance(big[l][name], tuple) else (big[l][name],)
            d0 = 0
            for piece in pieces:
                blocks = _scatter_blocks(name, piece, p[name].shape[1:], d_model, N_DEV // len(pieces)).astype(BF16)
                pack = lax.dynamic_update_slice(pack, blocks, (d0, row0, 0))
                d0 += blocks.shape[0]
            row0 += nrows
        rs_state[(l, part)] = _reduce_scatter_begin(pack, f"rs{l}{part}")
        return rs_state[(l, part)][4]

    dmem_n = None
    for l in reversed(range(depth)):
        wl, s = full[l], saved[l]
        sm = {}
        dact = _matmul(dxb, wl["w_down"], "nt", f"d_act_l{l}", out_dtype=BF16, after=rs_after)
        big[l]["w_down"] = _matmul(s["act"], dxb, "tn", f"d_w_down_l{l}", out_dtype=BF16)
        dup_g, dup_v, dwf_g, dwf_v = _ffn_act_bwd(s["up_g"], s["up_v"], ffn_w_full[l], dact, n_batch, seq,
                                                  f"ffn_act_bwd_l{l}")
        sm["ffn_dw_w"] = jnp.concatenate([dwf_g, dwf_v], axis=1)
        dx, dxb, dg = _matmul_rmsnorm_bwd((dup_g, dup_v), wl["w_up"], "nn", s["x2"], vec(ffn_norm_g[l]), dx,
                                          f"d_hf_ffn_norm_bwd_l{l}")
        big[l]["w_up"] = (_matmul(dup_g, s["hf"], "tn", f"d_w_up_gate_l{l}", out_dtype=BF16),
                          _matmul(dup_v, s["hf"], "tn", f"d_w_up_val_l{l}", out_dtype=BF16))
        sm["ffn_norm_g"] = dg
        datt = _matmul(dxb, wl["w_o"], "nt", f"d_att_l{l}", out_dtype=BF16, after=rs_after)
        big[l]["w_o"] = _matmul(s["att"], dxb, "tn", f"d_w_o_l{l}", out_dtype=BF16)
        dq, dk, dv = _attn_bwd(s["q"], s["kv"], datt, n_batch, seq, m_len, f"attn_bwd_l{l}")
        dkv = jnp.concatenate([dk, dv], axis=1)
        big[l]["w_kv"] = _matmul(dkv, mem_n, "tn", f"d_w_kv_l{l}", out_dtype=BF16)
        dmem_n = _matmul(dkv, wl["w_kv"], "nn", f"d_mem_l{l}", res=dmem_n)
        big[l]["w_q"] = _matmul(s["hq"], dq, "tn", f"d_w_q_l{l}", out_dtype=BF16)
        dx, dxb, dg = _matmul_rmsnorm_bwd(dq, wl["w_q"], "nt", s["x1"], vec(xattn_norm_g[l]), dx,
                                          f"d_hq_xattn_norm_bwd_l{l}")
        sm["xattn_norm_g"] = dg
        if part_of[(l, "w_q")] != part_of[(l, "w_in")]:
            rs_after = rs_begin(l, part_of[(l, "w_q")])
        dmerged = _matmul(dxb, wl["w_out"], "nt", f"d_merged_l{l}", out_dtype=BF16, after=rs_after)
        big[l]["w_out"] = _matmul(s["merged"], dxb, "tn", f"d_w_out_l{l}", out_dtype=BF16)
        dgc, dgp, dyc, dyp, dscale = _merge_bwd(s["proj"], gate_col0, s["yc"], s["yp"], vec(pool_scale[l]), dmerged,
                                                f"merge_bwd_l{l}")
        sm["pool_scale"] = dscale
        dzp = _grouped(dyp, wl["w_pool_grp"], "nn", f"d_zp_l{l}", out_dtype=BF16)
        big[l]["w_pool_grp"] = _grouped_tn(dyp, s["zp"], n_groups, f"d_w_pool_l{l}")
        du = _pool_bwd(dzp, n_groups, n_batch, seq, f"pool_bwd_l{l}")
        dy3 = _matmul(dyc, wl["w_conv_out"], "nn", f"d_y3_l{l}", out_dtype=BF16)
        big[l]["w_conv_out"] = _matmul(dyc, s["y3"], "tn", f"d_w_conv_out_l{l}", out_dtype=BF16)
        dy1, dlg, dlb = _ln_silu_bwd(s["y1"], vec(conv_ln_g[l]), vec(conv_ln_b[l]), dy3, f"ln_silu_bwd_l{l}")
        sm["conv_ln_g"], sm["conv_ln_b"] = dlg, dlb
        da, dgl, dcw, dcb = _glu_conv_bwd(s["proj"], conv_w_full[l], dy1, n_batch, seq, f"glu_conv_bwd_l{l}")
        sm["conv_dw_w"], sm["conv_dw_b"] = dcw, dcb
        dproj = jnp.concatenate([da, dgl, du, dgc, dgp], axis=1)
        big[l]["w_in"] = _matmul(dproj, s["h"], "tn", f"d_w_in_l{l}", out_dtype=BF16)
        dx, dxb, dg = _matmul_rmsnorm_bwd(dproj, wl["w_in"], "nn", s["x0"], vec(mix_norm_g[l]), dx,
                                          f"d_h_mix_norm_bwd_l{l}")
        sm["mix_norm_g"] = dg
        for k, val in sm.items():
            small[(l, k)] = val.reshape(-1)
        rs_after = rs_begin(l, part_of[(l, "w_in")])
    _, _, dg_mem = _rmsnorm_bwd(mem2d, vec(mem_norm_g), dmem_n, None, "mem_norm_bwd")
    small["mem_norm_g"] = dg_mem.reshape(-1)
    small["loss"] = loss_part.reshape(-1)

    grads = {}
    per_layer = {name: [None] * depth for name, _ in MATRICES}
    for l in reversed(range(depth)):
        for part, _ in reversed(_parts(l)):
            mat_grads = _reduce_scatter_end(rs_state[(l, part)], rs_after, f"rs{l}{part}")
            row0 = 0
            for name, tr, nrows in layout[part]:
                per_layer[name][l] = _from_rows(name, tr, mat_grads[row0:row0 + nrows], p[name].shape[1:])
                row0 += nrows
    for name, _ in MATRICES:
        grads[name] = jnp.stack(per_layer[name])

    keys = list(small.keys())
    flat = jnp.concatenate([small[k] for k in keys])
    n_small = flat.shape[0]
    rows_small = -(-n_small // (SUBLANE * d_model)) * SUBLANE
    flat = jnp.pad(flat, (0, rows_small * d_model - n_small)).reshape(rows_small, d_model)
    every = _all_gather(flat, "small_all_gather")
    total = _sum_rows([every[i] for i in range(N_DEV)], F32, "small_sum").reshape(-1)
    off = 0
    red = {}
    for k in keys:
        red[k] = total[off:off + small[k].shape[0]]
        off += small[k].shape[0]
    loss = red["loss"][0]
    for name in ("mix_norm_g", "conv_dw_b", "conv_ln_g", "conv_ln_b", "pool_scale", "xattn_norm_g", "ffn_norm_g"):
        grads[name] = jnp.stack([red[(l, name)] for l in range(depth)])
    grads["conv_dw_w"] = jnp.stack([
        lax.dynamic_slice_in_dim(red[(l, "conv_dw_w")].reshape(kc, N_DEV * cs), dev * cs, cs, axis=1)
        for l in range(depth)])
    grads["ffn_dw_w"] = jnp.stack([
        lax.dynamic_slice_in_dim(red[(l, "ffn_dw_w")].reshape(kf, N_DEV * fs), dev * fs, fs, axis=1)
        for l in range(depth)])
    grads["mem_norm_g"] = red["mem_norm_g"]
    grads["final_norm_g"] = red["final_norm_g"]

    deltas, new_m, new_v = {}, {}, {}
    for name in weight_names:
        deltas[name], new_m[name], new_v[name] = _adamw(p[name], grads[name], p["m_" + name], p["v_" + name],
                                                        f"adamw_{name}")
    grad_x = dx.reshape(n_batch, seq, d_model)
    return (loss, grad_x, *[grads[n] for n in weight_names], *[deltas[n] for n in weight_names],
            *[new_m[n] for n in weight_names], *[new_v[n] for n in weight_names])
```

```python
import functools

import jax
import jax.numpy as jnp
from jax import lax
from jax.experimental import pallas as pl
from jax.experimental.pallas import tpu as pltpu

F32 = jnp.float32
BF16 = jnp.bfloat16
MESH = pl.DeviceIdType.MESH

N_DEV = 8
EPS = 1e-6
V7X_VMEM_BYTES = 64 * 1024 * 1024
VMEM_LIMIT = (V7X_VMEM_BYTES * 3) // 4
LANE = 128
SUBLANE = 8

CONV_HALO = 32
POOL_HALO = 16
FFN_HALO = 8
FFN_LANE_GROUPS = 2
POOL_WINDOW_MAX = 16
XA_HEADS = 4

ADAM_LR = 0.001
ADAM_B1 = 0.9
ADAM_B2 = 0.999
ADAM_EPS = 1e-08
ADAM_WD = 0.01
ADAM_STEP = 10

GELU_C0 = 0.7978845608028654
GELU_C1 = 0.044715


ANY_SPEC = pl.BlockSpec(memory_space=pl.ANY)


def _tile(n, cap, mult=LANE):
    if n <= cap:
        return n
    best = None
    for d in range(mult, cap + 1, mult):
        if n % d == 0:
            best = d
    assert best is not None, (n, cap, mult)
    return best


def _params(*sem):
    return pltpu.CompilerParams(dimension_semantics=sem, vmem_limit_bytes=VMEM_LIMIT)


def _delayed(x, halo, rows, n_shifts):
    for r in range(min(SUBLANE, n_shifts)):
        xr = x if r == 0 else pltpu.roll(x, r, 0)
        for s in range(r, n_shifts, SUBLANE):
            yield s, xr[halo - (s - r):halo - (s - r) + rows]


def _advanced(x, rows, n_shifts):
    for r in range(min(SUBLANE, n_shifts)):
        xr = x if r == 0 else pltpu.roll(x, x.shape[0] - r, 0)
        for s in range(r, n_shifts, SUBLANE):
            yield s, xr[s - r:s - r + rows]


def _sig(x):
    return 1.0 / (1.0 + jnp.exp(-x))


def _bs(shape, imap):
    return pl.BlockSpec(shape, imap)


def _mxu_tile(n, cap):
    if n <= cap:
        return n
    best = {mult: max((d for d in range(mult, cap + 1, mult) if n % d == 0), default=0) for mult in (2 * LANE, LANE)}
    assert best[LANE] > 0, (n, cap)
    return best[2 * LANE] if 2 * best[2 * LANE] >= best[LANE] else best[LANE]


def _matmul(a, b, mode, name, res=None, out_dtype=F32, after=None, b_window=None):
    b_row0, b_rows = b_window if b_window is not None else (0, b.shape[0])
    if mode == "tn":
        k_dim, m_dim = a.shape
        k2, n_dim = b_rows, b.shape[1]
    elif mode == "nn":
        m_dim, k_dim = a.shape
        k2, n_dim = b_rows, b.shape[1]
    else:
        m_dim, k_dim = a.shape
        n_dim, k2 = b_rows, b.shape[1]
    assert k_dim == k2, (name, a.shape, b.shape)
    size = lambda t: jnp.dtype(t).itemsize
    tm = _mxu_tile(m_dim, 2816 if mode == "tn" else 1024)
    tn = _mxu_tile(n_dim, 2816)
    fixed = tm * tn * (2 * size(out_dtype) + (2 * size(res.dtype) if res is not None else 0) + 4)
    for cap in (2816, 2048, 1792, 1024, 512):
        tk = _mxu_tile(k_dim, cap)
        if fixed + 2 * tk * (tm * size(a.dtype) + tn * size(b.dtype)) <= VMEM_LIMIT - 8 * 1024 * 1024:
            break
    nk = k_dim // tk
    use_acc = nk > 1 and out_dtype != F32
    if mode == "tn":
        a_spec, ca = _bs((tk, tm), lambda i, j, k: (k, i)), 0
    else:
        a_spec, ca = _bs((tm, tk), lambda i, j, k: (i, k)), 1
    if mode == "nt":
        assert b_row0 % tn == 0
        b_spec, cb = _bs((tn, tk), lambda i, j, k: (j + b_row0 // tn, k)), 1
    else:
        assert b_row0 % tk == 0
        b_spec, cb = _bs((tk, tn), lambda i, j, k: (k + b_row0 // tk, j)), 0
    dims = (((ca,), (cb,)), ((), ()))
    o_spec = _bs((tm, tn), lambda i, j, k: (i, j))
    has_res = res is not None

    def body(*refs):
        a_ref, b_ref = refs[:2]
        r_ref = refs[2] if has_res else None
        o_ref = refs[n_in]
        k = pl.program_id(2)
        part = lax.dot_general(a_ref[...].astype(BF16), b_ref[...].astype(BF16), dims,
                               preferred_element_type=F32)
        if nk == 1:
            if has_res:
                part = part + r_ref[...].astype(F32)
            o_ref[...] = part.astype(out_dtype)
            return
        acc = refs[-1] if use_acc else o_ref

        @pl.when(k == 0)
        def _():
            acc[...] = part + r_ref[...].astype(F32) if has_res else part

        @pl.when(k > 0)
        def _():
            acc[...] += part

        if use_acc:
            @pl.when(k == nk - 1)
            def _():
                o_ref[...] = acc[...].astype(out_dtype)

    in_specs = [a_spec, b_spec] + ([o_spec] if has_res else [])
    args = (a, b) + ((res,) if has_res else ())
    if after is not None:
        in_specs.append(ANY_SPEC)
        args += (after,)
    n_in = len(args)
    return pl.pallas_call(
        body, out_shape=jax.ShapeDtypeStruct((m_dim, n_dim), out_dtype),
        grid=(m_dim // tm, n_dim // tn, nk), in_specs=in_specs, out_specs=o_spec,
        scratch_shapes=[pltpu.VMEM((tm, tn), F32)] if use_acc else [], name=name,
        compiler_params=_params("parallel", "parallel", "arbitrary"))(*args)


def _grouped(a, w, mode, name, out_dtype=F32):
    t_dim = a.shape[0]
    g_dim, r_dim, c_dim = w.shape
    ka, no = (c_dim, r_dim) if mode == "nt" else (r_dim, c_dim)
    tm = _tile(t_dim, 2048)
    dims = (((1,), (1 if mode == "nt" else 0,)), ((), ()))

    def body(a_ref, w_ref, o_ref):
        o_ref[...] = lax.dot_general(a_ref[...].astype(BF16), w_ref[...].astype(BF16), dims,
                                     preferred_element_type=F32).astype(out_dtype)

    return pl.pallas_call(
        body, out_shape=jax.ShapeDtypeStruct((t_dim, g_dim * no), out_dtype),
        grid=(t_dim // tm, g_dim),
        in_specs=[_bs((tm, ka), lambda i, g: (i, g)), _bs((None, r_dim, c_dim), lambda i, g: (g, 0, 0))],
        out_specs=_bs((tm, no), lambda i, g: (i, g)), name=name,
        compiler_params=_params("parallel", "parallel"))(a, w)


def _grouped_tn(a, b, g_dim, name):
    t_dim = a.shape[0]
    ra = a.shape[1] // g_dim
    cb = b.shape[1] // g_dim
    tm = _tile(t_dim, 2048)
    nt = t_dim // tm

    def body(a_ref, b_ref, o_ref):
        part = lax.dot_general(a_ref[...].astype(BF16), b_ref[...].astype(BF16), (((0,), (0,)), ((), ())),
                               preferred_element_type=F32)

        @pl.when(pl.program_id(1) == 0)
        def _():
            o_ref[...] = part

        @pl.when(pl.program_id(1) > 0)
        def _():
            o_ref[...] += part

    return pl.pallas_call(
        body, out_shape=jax.ShapeDtypeStruct((g_dim, ra, cb), F32), grid=(g_dim, nt),
        in_specs=[_bs((tm, ra), lambda g, i: (i, g)), _bs((tm, cb), lambda g, i: (i, g))],
        out_specs=_bs((None, ra, cb), lambda g, i: (g, 0, 0)), name=name,
        compiler_params=_params("parallel", "arbitrary"))(a, b)


def _rmsnorm_fwd(x, g, name, after=None):
    t_dim, d = x.shape
    tm = _tile(t_dim, 512)

    def body(x_ref, g_ref, *rest):
        o_ref = rest[-1]
        xv = x_ref[...]
        r = lax.rsqrt(jnp.mean(xv * xv, axis=-1, keepdims=True) + EPS)
        o_ref[...] = (xv * r * g_ref[...]).astype(BF16)

    return pl.pallas_call(
        body, out_shape=jax.ShapeDtypeStruct((t_dim, d), BF16), grid=(t_dim // tm,),
        in_specs=[_bs((tm, d), lambda i: (i, 0)), _bs((1, d), lambda i: (0, 0))] + ([ANY_SPEC] if after is not None else []),
        out_specs=_bs((tm, d), lambda i: (i, 0)), name=name,
        compiler_params=_params("parallel"))(x, g, *([after] if after is not None else []))


def _rmsnorm_bwd(x, g, dh, dx_in, name):
    t_dim, d = x.shape
    tm = _tile(t_dim, 512)
    has_in = dx_in is not None

    def body(*refs):
        if has_in:
            x_ref, g_ref, dh_ref, di_ref, dx_ref, dxb_ref, dg_ref = refs
        else:
            x_ref, g_ref, dh_ref, dx_ref, dxb_ref, dg_ref = refs
        xv = x_ref[...]
        r = lax.rsqrt(jnp.mean(xv * xv, axis=-1, keepdims=True) + EPS)
        xh = xv * r
        dhv = dh_ref[...].astype(F32)
        dxh = dhv * g_ref[...]
        dx = r * (dxh - xh * jnp.mean(dxh * xh, axis=-1, keepdims=True))
        if has_in:
            dx = dx + di_ref[...]
        dx_ref[...] = dx
        dxb_ref[...] = dx.astype(BF16)
        part = jnp.sum(dhv * xh, axis=0, keepdims=True)

        @pl.when(pl.program_id(0) == 0)
        def _():
            dg_ref[...] = part

        @pl.when(pl.program_id(0) > 0)
        def _():
            dg_ref[...] += part

    row = _bs((tm, d), lambda i: (i, 0))
    vec = _bs((1, d), lambda i: (0, 0))
    args = (x, g, dh) + ((dx_in,) if has_in else ())
    return pl.pallas_call(
        body, out_shape=(jax.ShapeDtypeStruct((t_dim, d), F32), jax.ShapeDtypeStruct((t_dim, d), BF16),
                         jax.ShapeDtypeStruct((1, d), F32)),
        grid=(t_dim // tm,), in_specs=[row, vec, row] + ([row] if has_in else []),
        out_specs=(row, row, vec), name=name, compiler_params=_params("arbitrary"))(*args)


def _matmul_rmsnorm_bwd(a, b, mode, x, g, dx_in, name, res=None, b_window=None):
    pieces = a if isinstance(a, tuple) else (a,)
    n_p = len(pieces)
    b_row0, b_rows = b_window if b_window is not None else (0, b.shape[0])
    m_dim, k_piece = pieces[0].shape
    assert all(t.shape == pieces[0].shape for t in pieces)
    k_dim = n_p * k_piece
    d = x.shape[1]
    assert (b_rows, b.shape[1]) == ((k_dim, d) if mode == "nn" else (d, k_dim)), (name, pieces[0].shape, b.shape)
    tm = _mxu_tile(m_dim, 512)
    tk = _mxu_tile(k_piece, 1792)
    nkp = k_piece // tk
    nk = n_p * nkp
    has_res = res is not None
    if mode == "nt":
        assert b_row0 == 0
        b_spec, cb = _bs((d, tk), lambda i, k: (0, k)), 1
    else:
        assert b_row0 % tk == 0
        b_spec, cb = _bs((tk, d), lambda i, k: (k + b_row0 // tk, 0)), 0
    dims = (((1,), (cb,)), ((), ()))

    def body(*refs):
        b_ref = refs[n_p]
        r_ref = refs[n_p + 1] if has_res else None
        x_ref, g_ref, di_ref, dx_ref, dxb_ref, dg_ref = refs[n_p + 1 + has_res:n_p + 7 + has_res]
        i, k = pl.program_id(0), pl.program_id(1)

        def finish(dhv):
            if has_res:
                dhv = dhv + r_ref[...].astype(F32)
            xv = x_ref[...]
            r = lax.rsqrt(jnp.mean(xv * xv, axis=-1, keepdims=True) + EPS)
            xh = xv * r
            dxh = dhv * g_ref[...]
            dx = r * (dxh - xh * jnp.mean(dxh * xh, axis=-1, keepdims=True)) + di_ref[...]
            dx_ref[...] = dx
            dxb_ref[...] = dx.astype(BF16)
            dg_part = jnp.sum(dhv * xh, axis=0, keepdims=True)

            @pl.when(i == 0)
            def _():
                dg_ref[...] = dg_part

            @pl.when(i > 0)
            def _():
                dg_ref[...] += dg_part

        def step(a_ref):
            part = lax.dot_general(a_ref[...].astype(BF16), b_ref[...].astype(BF16), dims, preferred_element_type=F32)
            if nk == 1:
                finish(part)
                return
            acc = refs[-1]

            @pl.when(k == 0)
            def _():
                acc[...] = part

            @pl.when(jnp.logical_and(k > 0, k < nk - 1))
            def _():
                acc[...] += part

            @pl.when(k == nk - 1)
            def _():
                finish(acc[...] + part)

        if n_p == 1:
            step(refs[0])
        else:
            for q in range(n_p):
                pl.when(jnp.logical_and(k >= q * nkp, k < (q + 1) * nkp))(functools.partial(step, refs[q]))

    row = _bs((tm, d), lambda i, k: (i, 0))
    vec = _bs((1, d), lambda i, k: (0, 0))
    a_specs = [_bs((tm, tk), lambda i, k, q=q: (i, jnp.clip(k - q * nkp, 0, nkp - 1))) for q in range(n_p)]
    in_specs = a_specs + [b_spec] + ([row] if has_res else []) + [row, vec, row]
    args = pieces + (b,) + ((res,) if has_res else ()) + (x, g, dx_in)
    return pl.pallas_call(
        body, out_shape=(jax.ShapeDtypeStruct((m_dim, d), F32), jax.ShapeDtypeStruct((m_dim, d), BF16),
                         jax.ShapeDtypeStruct((1, d), F32)),
        grid=(m_dim // tm, nk), in_specs=in_specs, out_specs=(row, row, vec),
        scratch_shapes=[pltpu.VMEM((tm, d), F32)] if nk > 1 else [], name=name,
        compiler_params=_params("arbitrary", "arbitrary"))(*args)


def _loss_head(x, g, tgt, name):
    t_dim, d = x.shape
    tm = _tile(t_dim, 512)

    def body(x_ref, g_ref, t_ref, dx_ref, dxb_ref, dg_ref, loss_ref):
        xv = x_ref[...]
        gv = g_ref[...]
        r = lax.rsqrt(jnp.mean(xv * xv, axis=-1, keepdims=True) + EPS)
        xh = xv * r
        err = xh * gv - t_ref[...]
        dy = err * (1.0 / d)
        dxh = dy * gv
        dx = r * (dxh - xh * jnp.mean(dxh * xh, axis=-1, keepdims=True))
        dx_ref[...] = dx
        dxb_ref[...] = dx.astype(BF16)
        dg_part = jnp.sum(dy * xh, axis=0, keepdims=True)
        loss_part = jnp.full((1, LANE), 0.5 * jnp.sum(jnp.mean(err * err, axis=-1, keepdims=True)), F32)

        @pl.when(pl.program_id(0) == 0)
        def _():
            dg_ref[...] = dg_part
            loss_ref[...] = loss_part

        @pl.when(pl.program_id(0) > 0)
        def _():
            dg_ref[...] += dg_part
            loss_ref[...] += loss_part

    row = _bs((tm, d), lambda i: (i, 0))
    vec = _bs((1, d), lambda i: (0, 0))
    return pl.pallas_call(
        body, out_shape=(jax.ShapeDtypeStruct((t_dim, d), F32), jax.ShapeDtypeStruct((t_dim, d), BF16),
                         jax.ShapeDtypeStruct((1, d), F32), jax.ShapeDtypeStruct((1, LANE), F32)),
        grid=(t_dim // tm,), in_specs=[row, vec, row],
        out_specs=(row, row, vec, _bs((1, LANE), lambda i: (0, 0))), name=name,
        compiler_params=_params("arbitrary"))(x, g, tgt)


def _glu_conv_fwd(proj, dw_w, dw_b, n_batch, seq, name):
    kk, cc = dw_w.shape
    nj = cc // LANE
    ch = min(256, seq)

    def body(a_ref, gl_ref, w_ref, b_ref, o_ref, pad):
        pad[0:CONV_HALO, :] = jnp.zeros((CONV_HALO, LANE), F32)
        pad[CONV_HALO:CONV_HALO + seq, :] = a_ref[...].astype(F32) * _sig(gl_ref[...].astype(F32))
        for c0 in range(0, seq, ch):
            acc = jnp.broadcast_to(b_ref[...], (ch, LANE))
            for k in range(kk):
                acc = acc + w_ref[k:k + 1, :] * pad[pl.ds(c0 + CONV_HALO - (kk - 1) + k, ch), :]
            o_ref[c0:c0 + ch, :] = acc

    return pl.pallas_call(
        body, out_shape=jax.ShapeDtypeStruct((n_batch * seq, cc), F32), grid=(n_batch, nj),
        in_specs=[_bs((seq, LANE), lambda b, j: (b, j)), _bs((seq, LANE), lambda b, j: (b, nj + j)),
                  _bs((kk, LANE), lambda b, j: (0, j)), _bs((1, LANE), lambda b, j: (0, j))],
        out_specs=_bs((seq, LANE), lambda b, j: (b, j)),
        scratch_shapes=[pltpu.VMEM((seq + CONV_HALO, LANE), F32)], name=name,
        compiler_params=_params("parallel", "parallel"))(proj, proj, dw_w, dw_b)


def _glu_conv_bwd(proj, dw_w, dy1, n_batch, seq, name):
    kk, cc = dw_w.shape
    nj = cc // LANE
    ch = min(256, seq)

    def body(a_ref, gl_ref, dy_ref, w_ref, da_ref, dgl_ref, dw_ref, db_ref, padf, padb):
        first = pl.program_id(1) == 0
        padf[0:CONV_HALO, :] = jnp.zeros((CONV_HALO, LANE), F32)
        padf[CONV_HALO:CONV_HALO + seq, :] = a_ref[...].astype(F32) * _sig(gl_ref[...].astype(F32))
        padb[0:seq, :] = dy_ref[...]
        padb[seq:seq + CONV_HALO, :] = jnp.zeros((CONV_HALO, LANE), F32)

        @pl.when(first)
        def _():
            dw_ref[...] = jnp.zeros((kk, LANE), F32)
            db_ref[...] = jnp.zeros((1, LANE), F32)

        dws = [jnp.zeros((1, LANE), F32) for _ in range(kk)]
        for c0 in range(0, seq, ch):
            acc = jnp.zeros((ch, LANE), F32)
            y0 = padf[CONV_HALO + c0:CONV_HALO + c0 + ch, :]
            for k in range(kk):
                win = padb[pl.ds(c0 + (kk - 1) - k, ch), :]
                acc = acc + w_ref[k:k + 1, :] * win
                dws[k] = dws[k] + jnp.sum(win * y0, axis=0, keepdims=True)
            sg = _sig(gl_ref[c0:c0 + ch, :].astype(F32))
            da_ref[c0:c0 + ch, :] = (acc * sg).astype(BF16)
            dgl_ref[c0:c0 + ch, :] = (acc * a_ref[c0:c0 + ch, :].astype(F32) * sg * (1.0 - sg)).astype(BF16)
        for k in range(kk):
            dw_ref[k:k + 1, :] += dws[k]
        db_ref[...] += jnp.sum(dy_ref[...], axis=0, keepdims=True)

    tok = _bs((seq, LANE), lambda j, b: (b, j))
    t_dim = n_batch * seq
    return pl.pallas_call(
        body, out_shape=(jax.ShapeDtypeStruct((t_dim, cc), BF16), jax.ShapeDtypeStruct((t_dim, cc), BF16),
                         jax.ShapeDtypeStruct((kk, cc), F32), jax.ShapeDtypeStruct((1, cc), F32)),
        grid=(nj, n_batch),
        in_specs=[tok, _bs((seq, LANE), lambda j, b: (b, nj + j)), tok, _bs((kk, LANE), lambda j, b: (0, j))],
        out_specs=(tok, tok, _bs((kk, LANE), lambda j, b: (0, j)), _bs((1, LANE), lambda j, b: (0, j))),
        scratch_shapes=[pltpu.VMEM((seq + CONV_HALO, LANE), F32), pltpu.VMEM((seq + CONV_HALO, LANE), F32)],
        name=name, compiler_params=_params("parallel", "arbitrary"))(proj, proj, dy1, dw_w)


def _ln_silu_fwd(y1, g, b, name):
    t_dim, c = y1.shape
    tm = _tile(t_dim, 512)

    def body(y_ref, g_ref, b_ref, o_ref):
        yv = y_ref[...]
        xc = yv - jnp.mean(yv, axis=-1, keepdims=True)
        rstd = lax.rsqrt(jnp.mean(xc * xc, axis=-1, keepdims=True) + EPS)
        y2 = xc * rstd * g_ref[...] + b_ref[...]
        o_ref[...] = (y2 * _sig(y2)).astype(BF16)

    row = _bs((tm, c), lambda i: (i, 0))
    vec = _bs((1, c), lambda i: (0, 0))
    return pl.pallas_call(
        body, out_shape=jax.ShapeDtypeStruct((t_dim, c), BF16), grid=(t_dim // tm,),
        in_specs=[row, vec, vec], out_specs=row, name=name, compiler_params=_params("parallel"))(y1, g, b)


def _ln_silu_bwd(y1, g, b, dy3, name):
    t_dim, c = y1.shape
    tm = _tile(t_dim, 512)

    def body(y_ref, g_ref, b_ref, d_ref, dy_ref, dg_ref, db_ref):
        yv = y_ref[...]
        gv = g_ref[...]
        xc = yv - jnp.mean(yv, axis=-1, keepdims=True)
        rstd = lax.rsqrt(jnp.mean(xc * xc, axis=-1, keepdims=True) + EPS)
        yh = xc * rstd
        y2 = yh * gv + b_ref[...]
        s = _sig(y2)
        dy2 = d_ref[...].astype(F32) * (s * (1.0 + y2 * (1.0 - s)))
        dyh = dy2 * gv
        dy_ref[...] = rstd * (dyh - jnp.mean(dyh, axis=-1, keepdims=True)
                              - yh * jnp.mean(dyh * yh, axis=-1, keepdims=True))
        dg_part = jnp.sum(dy2 * yh, axis=0, keepdims=True)
        db_part = jnp.sum(dy2, axis=0, keepdims=True)

        @pl.when(pl.program_id(0) == 0)
        def _():
            dg_ref[...] = dg_part
            db_ref[...] = db_part

        @pl.when(pl.program_id(0) > 0)
        def _():
            dg_ref[...] += dg_part
            db_ref[...] += db_part

    row = _bs((tm, c), lambda i: (i, 0))
    vec = _bs((1, c), lambda i: (0, 0))
    return pl.pallas_call(
        body, out_shape=(jax.ShapeDtypeStruct((t_dim, c), F32), jax.ShapeDtypeStruct((1, c), F32),
                         jax.ShapeDtypeStruct((1, c), F32)),
        grid=(t_dim // tm,), in_specs=[row, vec, vec, row], out_specs=(row, vec, vec), name=name,
        compiler_params=_params("arbitrary"))(y1, g, b, dy3)


def _pool_fwd(proj, col0, n_groups, n_batch, seq, name):
    ch = min(256, seq)

    def body(u_ref, o_ref, pad):
        w = lax.shift_left(jnp.int32(2), pl.program_id(1))
        pad[0:POOL_HALO, :] = jnp.zeros((POOL_HALO, LANE), F32)
        pad[POOL_HALO:POOL_HALO + seq, :] = u_ref[...].astype(F32)
        for c0 in range(0, seq, ch):
            acc = jnp.zeros((ch, LANE), F32)
            for j in range(POOL_WINDOW_MAX):
                acc = acc + jnp.where(j < w, 1.0, 0.0).astype(F32) * pad[pl.ds(c0 + POOL_HALO - j, ch), :]
            t = c0 + lax.broadcasted_iota(jnp.int32, (ch, LANE), 0)
            cnt = jnp.minimum(t + 1, w).astype(F32)
            o_ref[c0:c0 + ch, :] = (acc / cnt - pad[POOL_HALO + c0:POOL_HALO + c0 + ch, :]).astype(BF16)

    return pl.pallas_call(
        body, out_shape=jax.ShapeDtypeStruct((n_batch * seq, n_groups * LANE), BF16), grid=(n_batch, n_groups),
        in_specs=[_bs((seq, LANE), lambda b, g: (b, col0 + g))], out_specs=_bs((seq, LANE), lambda b, g: (b, g)),
        scratch_shapes=[pltpu.VMEM((seq + POOL_HALO, LANE), F32)], name=name,
        compiler_params=_params("parallel", "parallel"))(proj)


def _pool_bwd(dzp, n_groups, n_batch, seq, name):
    ch = min(256, seq)

    def body(d_ref, o_ref, pad):
        w = lax.shift_left(jnp.int32(2), pl.program_id(1))
        for c0 in range(0, seq, ch):
            t = c0 + lax.broadcasted_iota(jnp.int32, (ch, LANE), 0)
            cnt = jnp.minimum(t + 1, w).astype(F32)
            pad[c0:c0 + ch, :] = d_ref[c0:c0 + ch, :].astype(F32) / cnt
        pad[seq:seq + POOL_HALO, :] = jnp.zeros((POOL_HALO, LANE), F32)
        for c0 in range(0, seq, ch):
            acc = jnp.zeros((ch, LANE), F32)
            for j in range(POOL_WINDOW_MAX):
                acc = acc + jnp.where(j < w, 1.0, 0.0).astype(F32) * pad[pl.ds(c0 + j, ch), :]
            o_ref[c0:c0 + ch, :] = (acc - d_ref[c0:c0 + ch, :].astype(F32)).astype(BF16)

    tok = _bs((seq, LANE), lambda b, g: (b, g))
    return pl.pallas_call(
        body, out_shape=jax.ShapeDtypeStruct((n_batch * seq, n_groups * LANE), BF16), grid=(n_batch, n_groups),
        in_specs=[tok], out_specs=tok, scratch_shapes=[pltpu.VMEM((seq + POOL_HALO, LANE), F32)], name=name,
        compiler_params=_params("parallel", "parallel"))(dzp)


def _merge_fwd(proj, col0, yc, yp, scale, name):
    t_dim, d = yc.shape
    half = d // 2
    tm = _tile(t_dim, 512)
    c0 = col0 // half

    def body(gc_ref, gp_ref, yc_ref, yp_ref, s_ref, o_ref):
        f32 = lambda r: r[...].astype(F32)
        o_ref[...] = (_sig(f32(gc_ref)) * f32(yc_ref) + _sig(f32(gp_ref)) * (f32(yp_ref) * s_ref[...])).astype(BF16)

    blk = _bs((tm, half), lambda i, j: (i, j))
    return pl.pallas_call(
        body, out_shape=jax.ShapeDtypeStruct((t_dim, d), BF16), grid=(t_dim // tm, 2),
        in_specs=[_bs((tm, half), lambda i, j: (i, c0 + j)), _bs((tm, half), lambda i, j: (i, c0 + 2 + j)),
                  blk, blk, _bs((1, half), lambda i, j: (0, j))],
        out_specs=blk, name=name, compiler_params=_params("parallel", "parallel"))(proj, proj, yc, yp, scale)


def _merge_bwd(proj, col0, yc, yp, scale, dm, name):
    t_dim, d = yc.shape
    half = d // 2
    tm = _tile(t_dim, 512)
    c0 = col0 // half

    def body(gc_ref, gp_ref, yc_ref, yp_ref, s_ref, dm_ref, dgc_ref, dgp_ref, dyc_ref, dyp_ref, ds_ref):
        dmv = dm_ref[...].astype(F32)
        sgc = _sig(gc_ref[...].astype(F32))
        sgp = _sig(gp_ref[...].astype(F32))
        sv = s_ref[...]
        ypre = yp_ref[...].astype(F32)
        dgc_ref[...] = (dmv * yc_ref[...].astype(F32) * sgc * (1.0 - sgc)).astype(BF16)
        dgp_ref[...] = (dmv * (ypre * sv) * sgp * (1.0 - sgp)).astype(BF16)
        dyc_ref[...] = (dmv * sgc).astype(BF16)
        dyp = dmv * sgp
        dyp_ref[...] = (dyp * sv).astype(BF16)
        part = jnp.sum(dyp * ypre, axis=0, keepdims=True)

        @pl.when(pl.program_id(1) == 0)
        def _():
            ds_ref[...] = part

        @pl.when(pl.program_id(1) > 0)
        def _():
            ds_ref[...] += part

    blk = _bs((tm, half), lambda j, i: (i, j))
    big = jax.ShapeDtypeStruct((t_dim, d), BF16)
    return pl.pallas_call(
        body, out_shape=(big, big, big, big, jax.ShapeDtypeStruct((1, d), F32)), grid=(2, t_dim // tm),
        in_specs=[_bs((tm, half), lambda j, i: (i, c0 + j)), _bs((tm, half), lambda j, i: (i, c0 + 2 + j)),
                  blk, blk, _bs((1, half), lambda j, i: (0, j)), blk],
        out_specs=(blk, blk, blk, blk, _bs((1, half), lambda j, i: (0, j))), name=name,
        compiler_params=_params("parallel", "arbitrary"))(proj, proj, yc, yp, scale, dm)


def _attn_fwd(q, kv, n_batch, seq, m_len, name):
    d = q.shape[1]
    hd = d // XA_HEADS
    tq = _tile(seq, 1024)
    nq = seq // tq
    scale = hd ** -0.5

    def body(q_ref, k_ref, v_ref, o_ref):
        sc = lax.dot_general(q_ref[...].astype(BF16), k_ref[...].astype(BF16), (((1,), (1,)), ((), ())),
                             preferred_element_type=F32) * scale
        p = jnp.exp(sc - jnp.max(sc, axis=-1, keepdims=True))
        pr = p / jnp.sum(p, axis=-1, keepdims=True)
        o_ref[...] = jnp.dot(pr.astype(BF16), v_ref[...].astype(BF16), preferred_element_type=F32).astype(BF16)

    return pl.pallas_call(
        body, out_shape=jax.ShapeDtypeStruct((n_batch * seq, d), BF16), grid=(n_batch, XA_HEADS, nq),
        in_specs=[_bs((tq, hd), lambda b, h, i: (b * nq + i, h)), _bs((m_len, hd), lambda b, h, i: (b, h)),
                  _bs((m_len, hd), lambda b, h, i: (b, XA_HEADS + h))],
        out_specs=_bs((tq, hd), lambda b, h, i: (b * nq + i, h)), name=name,
        compiler_params=_params("parallel", "parallel", "parallel"))(q, kv, kv)


def _attn_bwd(q, kv, datt, n_batch, seq, m_len, name):
    d = q.shape[1]
    hd = d // XA_HEADS
    tq = _tile(seq, 1024)
    nq = seq // tq
    scale = hd ** -0.5

    def body(q_ref, k_ref, v_ref, do_ref, dq_ref, dk_ref, dv_ref):
        qb = q_ref[...].astype(BF16)
        kb = k_ref[...].astype(BF16)
        vb = v_ref[...].astype(BF16)
        dob = do_ref[...].astype(BF16)
        sc = lax.dot_general(qb, kb, (((1,), (1,)), ((), ())), preferred_element_type=F32) * scale
        p = jnp.exp(sc - jnp.max(sc, axis=-1, keepdims=True))
        pr = p / jnp.sum(p, axis=-1, keepdims=True)
        dpr = lax.dot_general(dob, vb, (((1,), (1,)), ((), ())), preferred_element_type=F32)
        dsc = pr * (dpr - jnp.sum(dpr * pr, axis=-1, keepdims=True)) * scale
        dsb = dsc.astype(BF16)
        dq_ref[...] = jnp.dot(dsb, kb, preferred_element_type=F32).astype(BF16)
        dv_part = lax.dot_general(pr.astype(BF16), dob, (((0,), (0,)), ((), ())), preferred_element_type=F32)
        dk_part = lax.dot_general(dsb, qb, (((0,), (0,)), ((), ())), preferred_element_type=F32)

        @pl.when(pl.program_id(2) == 0)
        def _():
            dk_ref[...] = dk_part
            dv_ref[...] = dv_part

        @pl.when(pl.program_id(2) > 0)
        def _():
            dk_ref[...] += dk_part
            dv_ref[...] += dv_part

    qs = _bs((tq, hd), lambda b, h, i: (b * nq + i, h))
    ks = _bs((m_len, hd), lambda b, h, i: (b, h))
    return pl.pallas_call(
        body, out_shape=(jax.ShapeDtypeStruct((n_batch * seq, d), BF16), jax.ShapeDtypeStruct((n_batch * m_len, d), F32),
                         jax.ShapeDtypeStruct((n_batch * m_len, d), F32)),
        grid=(n_batch, XA_HEADS, nq),
        in_specs=[qs, ks, _bs((m_len, hd), lambda b, h, i: (b, XA_HEADS + h)), qs],
        out_specs=(qs, ks, ks), name=name,
        compiler_params=_params("parallel", "parallel", "arbitrary"))(q, kv, kv, datt)


def _gelu_parts(g):
    th = jnp.tanh(GELU_C0 * (g + GELU_C1 * g * g * g))
    return th, 0.5 * g * (1.0 + th)


def _ffn_act_fwd(up_g, up_v, dw_w, n_batch, seq, name):
    kk, c2 = dw_w.shape
    f_dim = c2 // 2
    wd = FFN_LANE_GROUPS * LANE
    nj = f_dim // wd
    ch = min(128, seq)

    def body(g_ref, v_ref, wg_ref, wv_ref, o_ref, padg, padv):
        for h in range(FFN_LANE_GROUPS):
            lanes = slice(h * LANE, (h + 1) * LANE)
            for pad, src in ((padg, g_ref), (padv, v_ref)):
                pad[h, 0:FFN_HALO, :] = jnp.zeros((FFN_HALO, LANE), F32)
                pad[h, FFN_HALO:FFN_HALO + seq, :] = src[:, lanes].astype(F32)
            for c0 in range(0, seq, ch):
                gate = jnp.zeros((ch, LANE), F32)
                val = jnp.zeros((ch, LANE), F32)
                for k in range(kk):
                    off = c0 + FFN_HALO - (kk - 1) + k
                    gate = gate + wg_ref[k:k + 1, lanes] * padg[h, pl.ds(off, ch), :]
                    val = val + wv_ref[k:k + 1, lanes] * padv[h, pl.ds(off, ch), :]
                o_ref[c0:c0 + ch, lanes] = (_gelu_parts(gate)[1] * val).astype(BF16)

    pad_shape = pltpu.VMEM((FFN_LANE_GROUPS, seq + FFN_HALO, LANE), F32)
    return pl.pallas_call(
        body, out_shape=jax.ShapeDtypeStruct((n_batch * seq, f_dim), BF16), grid=(n_batch, nj),
        in_specs=[_bs((seq, wd), lambda b, j: (b, j)), _bs((seq, wd), lambda b, j: (b, j)),
                  _bs((kk, wd), lambda b, j: (0, j)), _bs((kk, wd), lambda b, j: (0, nj + j))],
        out_specs=_bs((seq, wd), lambda b, j: (b, j)), scratch_shapes=[pad_shape, pad_shape], name=name,
        compiler_params=_params("parallel", "parallel"))(up_g, up_v, dw_w, dw_w)


def _ffn_act_bwd(up_g, up_v, dw_w, dact, n_batch, seq, name):
    kk, c2 = dw_w.shape
    f_dim = c2 // 2
    wd = FFN_LANE_GROUPS * LANE
    nj = f_dim // wd
    ch = min(128, seq)

    def body(g_ref, v_ref, wg_ref, wv_ref, da_ref, dg_ref, dv_ref, dwg_ref, dwv_ref, padg, padv, pbg, pbv):
        @pl.when(pl.program_id(1) == 0)
        def _():
            dwg_ref[...] = jnp.zeros((kk, wd), F32)
            dwv_ref[...] = jnp.zeros((kk, wd), F32)

        for h in range(FFN_LANE_GROUPS):
            lanes = slice(h * LANE, (h + 1) * LANE)
            for pad, src in ((padg, g_ref), (padv, v_ref)):
                pad[h, 0:FFN_HALO, :] = jnp.zeros((FFN_HALO, LANE), F32)
                pad[h, FFN_HALO:FFN_HALO + seq, :] = src[:, lanes].astype(F32)
            for pb in (pbg, pbv):
                pb[h, seq:seq + FFN_HALO, :] = jnp.zeros((FFN_HALO, LANE), F32)
            for c0 in range(0, seq, ch):
                gate = jnp.zeros((ch, LANE), F32)
                val = jnp.zeros((ch, LANE), F32)
                for k in range(kk):
                    off = c0 + FFN_HALO - (kk - 1) + k
                    gate = gate + wg_ref[k:k + 1, lanes] * padg[h, pl.ds(off, ch), :]
                    val = val + wv_ref[k:k + 1, lanes] * padv[h, pl.ds(off, ch), :]
                sq = gate * gate
                th = jnp.tanh(GELU_C0 * gate * (1.0 + GELU_C1 * sq))
                half = 0.5 * th + 0.5
                dgelu = half * (1.0 + gate * (GELU_C0 + 3.0 * GELU_C0 * GELU_C1 * sq) * (1.0 - th))
                dav = da_ref[c0:c0 + ch, lanes].astype(F32)
                pbg[h, c0:c0 + ch, :] = dav * val * dgelu
                pbv[h, c0:c0 + ch, :] = dav * (gate * half)
            for pb, pad, w_ref, d_ref, dw_ref in ((pbg, padg, wg_ref, dg_ref, dwg_ref), (pbv, padv, wv_ref, dv_ref, dwv_ref)):
                for c0 in range(0, seq, ch):
                    acc = jnp.zeros((ch, LANE), F32)
                    for k in range(kk):
                        acc = acc + w_ref[k:k + 1, lanes] * pb[h, pl.ds(c0 + (kk - 1) - k, ch), :]
                    d_ref[c0:c0 + ch, lanes] = acc.astype(BF16)
                for k in range(kk):
                    s = jnp.zeros((1, LANE), F32)
                    for c0 in range(0, seq, ch):
                        s = s + jnp.sum(pb[h, c0:c0 + ch, :] * pad[h, pl.ds(c0 + FFN_HALO - (kk - 1) + k, ch), :],
                                        axis=0, keepdims=True)
                    dw_ref[k:k + 1, lanes] += s

    t_dim = n_batch * seq
    tok = _bs((seq, wd), lambda j, b: (b, j))
    wblk = _bs((kk, wd), lambda j, b: (0, j))
    pad_shape = pltpu.VMEM((FFN_LANE_GROUPS, seq + FFN_HALO, LANE), F32)
    return pl.pallas_call(
        body, out_shape=(jax.ShapeDtypeStruct((t_dim, f_dim), BF16), jax.ShapeDtypeStruct((t_dim, f_dim), BF16),
                         jax.ShapeDtypeStruct((kk, f_dim), F32), jax.ShapeDtypeStruct((kk, f_dim), F32)),
        grid=(nj, n_batch),
        in_specs=[tok, tok, wblk, _bs((kk, wd), lambda j, b: (0, nj + j)), tok],
        out_specs=(tok, tok, wblk, wblk), scratch_shapes=[pad_shape, pad_shape, pad_shape, pad_shape], name=name,
        compiler_params=_params("parallel", "arbitrary"))(up_g, up_v, dw_w, dw_w, dact)


def _sum_rows(parts, out_dtype, name):
    r_dim, c_dim = parts[0].shape
    tr = _tile(r_dim, 1200, SUBLANE)
    n = len(parts)

    def body(*refs):
        acc = refs[0][...].astype(F32)
        for r in refs[1:n]:
            acc = acc + r[...].astype(F32)
        refs[n][...] = acc.astype(out_dtype)

    blk = _bs((tr, c_dim), lambda i: (i, 0))
    return pl.pallas_call(
        body, out_shape=jax.ShapeDtypeStruct((r_dim, c_dim), out_dtype), grid=(r_dim // tr,),
        in_specs=[blk] * n, out_specs=blk, name=name, compiler_params=_params("parallel"))(*parts)


def _adamw(w, g, m, v, name):
    shape = w.shape
    c_dim = shape[-1]
    r_dim = w.size // c_dim
    two_d = lambda t: t.reshape(r_dim, c_dim)
    tr = _tile(r_dim, max(SUBLANE, (256 * 1024) // max(c_dim, LANE) // SUBLANE * SUBLANE), SUBLANE)
    c1 = 1.0 - ADAM_B1 ** ADAM_STEP
    c2 = 1.0 - ADAM_B2 ** ADAM_STEP

    def body(w_ref, g_ref, m_ref, v_ref, d_ref, mo_ref, vo_ref):
        gv = g_ref[...]
        mn = ADAM_B1 * m_ref[...] + (1.0 - ADAM_B1) * gv
        vn = ADAM_B2 * v_ref[...] + (1.0 - ADAM_B2) * (gv * gv)
        mo_ref[...] = mn
        vo_ref[...] = vn
        d_ref[...] = -ADAM_LR * ((mn / c1) / (jnp.sqrt(vn / c2) + ADAM_EPS) + ADAM_WD * w_ref[...])

    blk = _bs((tr, c_dim), lambda i: (i, 0))
    out = jax.ShapeDtypeStruct((r_dim, c_dim), F32)
    d, mo, vo = pl.pallas_call(
        body, out_shape=(out, out, out), grid=(r_dim // tr,), in_specs=[blk] * 4, out_specs=(blk, blk, blk),
        name=name, compiler_params=_params("parallel"))(two_d(w), two_d(g), two_d(m), two_d(v))
    return d.reshape(shape), mo.reshape(shape), vo.reshape(shape)


HBM_SPEC = pl.BlockSpec(memory_space=pltpu.HBM)


def _position():
    return lax.axis_index("x"), lax.axis_index("y"), lax.axis_index("c")


def _all_gather(shard, name):
    def body(x_ref, out_ref, send_sems, recv_sems, local_sem):
        x, y, c = _position()
        me, sibling = (x, y, c), (x, y, 1 - c)
        chips = [(1 - x, y), (x, 1 - y), (1 - x, 1 - y)]

        def rows(px, py, pc):
            return out_ref.at[4 * px + 2 * py + pc]

        def copy(k, block, to, src=None):
            return pltpu.make_async_remote_copy(
                src_ref=rows(*block) if src is None else src, dst_ref=rows(*block),
                send_sem=send_sems.at[k], recv_sem=recv_sems.at[k], device_id=to, device_id_type=MESH)

        mine = pltpu.make_async_copy(x_ref, rows(*me), local_sem)
        mine.start()
        first = [copy(0, me, sibling, src=x_ref)]
        first += [copy(1 + j, me, (*chip, c), src=x_ref) for j, chip in enumerate(chips)]
        for cp in first:
            cp.start()
        passed = [copy(4 + j, (*chip, c), sibling) for j, chip in enumerate(chips)]
        for j, chip in enumerate(chips):
            copy(1 + j, (*chip, c), me).wait_recv()
            passed[j].start()
        copy(0, sibling, me).wait_recv()
        for j, chip in enumerate(chips):
            copy(4 + j, (*chip, 1 - c), me).wait_recv()
        for cp in first + passed:
            cp.wait_send()
        mine.wait()

    return pl.pallas_call(
        body, out_shape=jax.ShapeDtypeStruct((N_DEV,) + shard.shape, shard.dtype),
        in_specs=[HBM_SPEC], out_specs=HBM_SPEC,
        scratch_shapes=[pltpu.SemaphoreType.DMA((7,)), pltpu.SemaphoreType.DMA((7,)), pltpu.SemaphoreType.DMA(())],
        name=name)(shard)


CHIP_RELATIONS = ((0, 0), (1, 0), (0, 1), (1, 1))


def _rs_pair_exchange(g, name):
    _, r_dim, c_dim = g.shape
    n = len(CHIP_RELATIONS)

    def body(g_ref, recv_ref, send_sems, recv_sems):
        x, y, c = _position()
        sibling = (x, y, 1 - c)
        copies = []
        for k, (rx, ry) in enumerate(CHIP_RELATIONS):
            px = x + rx - 2 * x * rx
            py = y + ry - 2 * y * ry
            copies.append(pltpu.make_async_remote_copy(
                src_ref=g_ref.at[4 * px + 2 * py + 1 - c], dst_ref=recv_ref.at[k], send_sem=send_sems.at[k],
                recv_sem=recv_sems.at[k], device_id=sibling, device_id_type=MESH))
        for cp in copies:
            cp.start()
        for cp in copies:
            cp.wait()

    return pl.pallas_call(
        body, out_shape=jax.ShapeDtypeStruct((n, r_dim, c_dim), g.dtype), in_specs=[HBM_SPEC], out_specs=HBM_SPEC,
        scratch_shapes=[pltpu.SemaphoreType.DMA((n,)), pltpu.SemaphoreType.DMA((n,))], name=name)(g)


def _rs_pair_sum(g, recv, name):
    _, r_dim, c_dim = g.shape
    n = len(CHIP_RELATIONS)
    tr = _tile(r_dim, 1200, SUBLANE)
    x, y, c = _position()
    own = jnp.stack([4 * (x + rx - 2 * x * rx) + 2 * (y + ry - 2 * y * ry) + c for rx, ry in CHIP_RELATIONS])

    def body(own_ref, g_ref, r_ref, o_ref):
        o_ref[...] = (g_ref[...].astype(F32) + r_ref[...].astype(F32)).astype(o_ref.dtype)

    blk = _bs((None, tr, c_dim), lambda k, i, own_ref: (k, i, 0))
    return pl.pallas_call(
        body, out_shape=jax.ShapeDtypeStruct((n, r_dim, c_dim), g.dtype),
        grid_spec=pltpu.PrefetchScalarGridSpec(
            num_scalar_prefetch=1, grid=(n, r_dim // tr),
            in_specs=[_bs((None, tr, c_dim), lambda k, i, own_ref: (own_ref[k], i, 0)), blk], out_specs=blk),
        name=name, compiler_params=_params("parallel", "parallel"))(own.astype(jnp.int32), g, recv)


SEM_SPEC = pl.BlockSpec(memory_space=pltpu.SEMAPHORE)
DATAFLOW = pltpu.SideEffectType.DATAFLOW_SIDE_EFFECTING
CHIP_FLIPS = CHIP_RELATIONS[1:]
TOKEN = jax.ShapeDtypeStruct((SUBLANE, LANE), F32)


def _flip(v, r):
    return v + r - 2 * v * r


def _chip_copies(src_ref, src_of, dst_ref, dst_of, send_sems, recv_sems):
    x, y, c = _position()
    me = 4 * x + 2 * y + c
    out = []
    for k, (rx, ry) in enumerate(CHIP_FLIPS):
        px, py = _flip(x, rx), _flip(y, ry)
        peer = 4 * px + 2 * py + c
        out.append(pltpu.make_async_remote_copy(
            src_ref=src_ref.at[src_of(k, me, peer)], dst_ref=dst_ref.at[dst_of(k, me, peer)],
            send_sem=send_sems.at[k], recv_sem=recv_sems.at[k], device_id=(px, py, c), device_id_type=MESH))
    return out


def _device_block(ref, spec, d):
    rows, axis = spec
    return ref.at[pl.ds(d * rows, rows)] if axis == 0 else ref.at[:, pl.ds(d * rows, rows)]


def _ag_chips_start(lands, specs, after, name):
    n = len(lands)
    nf = len(CHIP_FLIPS)

    def body(*refs):
        send_sems, recv_sems, token = refs[n + 1], refs[n + 2], refs[-1]
        x, y, c = _position()
        me = 4 * x + 2 * y + c
        for i, spec in enumerate(specs):
            blk = _device_block(refs[i], spec, me)
            for k, (rx, ry) in enumerate(CHIP_FLIPS):
                pltpu.make_async_remote_copy(
                    src_ref=blk, dst_ref=blk, send_sem=send_sems.at[nf * i + k], recv_sem=recv_sems.at[nf * i + k],
                    device_id=(_flip(x, rx), _flip(y, ry), c), device_id_type=MESH).start()
        token[...] = jnp.zeros(TOKEN.shape, TOKEN.dtype)

    sems = pltpu.SemaphoreType.DMA((nf * n,))
    return pl.pallas_call(
        body, name=name, out_shape=(sems, sems, *[pltpu.HBM(t.shape, t.dtype) for t in lands], TOKEN),
        in_specs=(HBM_SPEC,) * n + (ANY_SPEC,),
        out_specs=(SEM_SPEC, SEM_SPEC) + (HBM_SPEC,) * n + (pl.BlockSpec(memory_space=pltpu.VMEM),),
        input_output_aliases={i: 2 + i for i in range(n)}, compiler_params=pltpu.CompilerParams(has_side_effects=DATAFLOW),
    )(*[pltpu.with_memory_space_constraint(t, pltpu.HBM) for t in lands], after)


def _ag_chips_wait(send_sems, recv_sems, lands, specs, after, name):
    n = len(lands)
    nf = len(CHIP_FLIPS)

    def body(*refs):
        send_sems, recv_sems = refs[n], refs[n + 1]
        x, y, c = _position()
        me = 4 * x + 2 * y + c
        for i, spec in enumerate(specs):
            for k, (rx, ry) in enumerate(CHIP_FLIPS):
                px, py = _flip(x, rx), _flip(y, ry)
                cp = pltpu.make_async_remote_copy(
                    src_ref=_device_block(refs[i], spec, me), dst_ref=_device_block(refs[i], spec, 4 * px + 2 * py + c),
                    send_sem=send_sems.at[nf * i + k], recv_sem=recv_sems.at[nf * i + k],
                    device_id=(px, py, c), device_id_type=MESH)
                cp.wait_send()
                cp.wait_recv()

    return pl.pallas_call(
        body, name=name, out_shape=tuple(pltpu.HBM(t.shape, t.dtype) for t in lands),
        in_specs=(HBM_SPEC,) * n + (SEM_SPEC, SEM_SPEC, ANY_SPEC), out_specs=(HBM_SPEC,) * n,
        input_output_aliases={i: i for i in range(n)}, compiler_params=pltpu.CompilerParams(has_side_effects=DATAFLOW),
    )(*lands, send_sems, recv_sems, after)


def _ag_pair_forward(lands, specs, name):
    n = len(lands)
    nr = len(CHIP_RELATIONS)

    def body(*refs):
        outs, send_sems, recv_sems = refs[n:2 * n], refs[2 * n], refs[2 * n + 1]
        x, y, c = _position()
        copies = []
        for i, spec in enumerate(specs):
            for k, (rx, ry) in enumerate(CHIP_RELATIONS):
                chip = 4 * _flip(x, rx) + 2 * _flip(y, ry)
                held = _device_block(outs[i], spec, chip + c)
                sems = dict(send_sem=send_sems.at[nr * i + k], recv_sem=recv_sems.at[nr * i + k],
                            device_id=(x, y, 1 - c), device_id_type=MESH)
                mine = pltpu.make_async_remote_copy(src_ref=held, dst_ref=held, **sems)
                theirs = pltpu.make_async_remote_copy(src_ref=held, dst_ref=_device_block(outs[i], spec, chip + 1 - c), **sems)
                copies.append((mine, theirs))
        for mine, _ in copies:
            mine.start()
        for mine, theirs in copies:
            mine.wait_send()
            theirs.wait_recv()

    sems = pltpu.SemaphoreType.DMA((nr * n,))
    return pl.pallas_call(
        body, out_shape=tuple(jax.ShapeDtypeStruct(t.shape, t.dtype) for t in lands), in_specs=[HBM_SPEC] * n,
        out_specs=(HBM_SPEC,) * n, input_output_aliases={i: i for i in range(n)}, scratch_shapes=[sems, sems], name=name)(*lands)


def _rs_chips_start(pair, name):
    _, r_dim, c_dim = pair.shape
    n = len(CHIP_FLIPS)

    def body(pair_ref, far_ref, send_sems, recv_sems, pair_thru, far_thru, token):
        for cp in _chip_copies(pair_ref, lambda k, me, peer: k + 1, far_ref, lambda k, me, peer: k, send_sems, recv_sems):
            cp.start()
        token[...] = jnp.zeros(TOKEN.shape, TOKEN.dtype)

    far = lax.empty((n, r_dim, c_dim), pair.dtype)
    return pl.pallas_call(
        body, name=name,
        out_shape=(pltpu.SemaphoreType.DMA((n,)), pltpu.SemaphoreType.DMA((n,)), pltpu.HBM(pair.shape, pair.dtype),
                   pltpu.HBM(far.shape, far.dtype), TOKEN),
        in_specs=(HBM_SPEC, HBM_SPEC),
        out_specs=(SEM_SPEC, SEM_SPEC, HBM_SPEC, HBM_SPEC, pl.BlockSpec(memory_space=pltpu.VMEM)),
        input_output_aliases={0: 2, 1: 3}, compiler_params=pltpu.CompilerParams(has_side_effects=DATAFLOW),
    )(pltpu.with_memory_space_constraint(pair, pltpu.HBM), pltpu.with_memory_space_constraint(far, pltpu.HBM))


def _rs_chips_wait(send_sems, recv_sems, pair, far, after, name):
    def body(pair_ref, far_ref, send_sems, recv_sems, after_ref, pair_out, far_out):
        for cp in _chip_copies(pair_ref, lambda k, me, peer: k + 1, far_ref, lambda k, me, peer: k, send_sems, recv_sems):
            cp.wait_send()
            cp.wait_recv()

    return pl.pallas_call(
        body, name=name, out_shape=(pltpu.HBM(pair.shape, pair.dtype), pltpu.HBM(far.shape, far.dtype)),
        in_specs=(HBM_SPEC, HBM_SPEC, SEM_SPEC, SEM_SPEC, ANY_SPEC),
        out_specs=(HBM_SPEC, HBM_SPEC), input_output_aliases={0: 0, 1: 1},
        compiler_params=pltpu.CompilerParams(has_side_effects=DATAFLOW),
    )(pair, far, send_sems, recv_sems, after)


def _rs_final_sum(pair, far, name):
    _, r_dim, c_dim = pair.shape
    tr = _tile(r_dim, 1200, SUBLANE)

    def body(p_ref, f0_ref, f1_ref, f2_ref, o_ref):
        o_ref[...] = ((p_ref[...].astype(F32) + f0_ref[...].astype(F32)) + f1_ref[...].astype(F32)) + f2_ref[...].astype(F32)

    def slot(k):
        return _bs((None, tr, c_dim), lambda i: (k, i, 0))

    return pl.pallas_call(
        body, out_shape=jax.ShapeDtypeStruct((r_dim, c_dim), F32), grid=(r_dim // tr,),
        in_specs=[slot(0), slot(0), slot(1), slot(2)], out_specs=_bs((tr, c_dim), lambda i: (i, 0)), name=name,
        compiler_params=_params("parallel"))(pair, far, far, far)


def _reduce_scatter_begin(g, name):
    recv = _rs_pair_exchange(g, name + "_pair")
    pair = _rs_pair_sum(g, recv, name + "_pairsum")
    return _rs_chips_start(pair, name + "_chips_start")


def _reduce_scatter_end(state, after, name):
    send_sems, recv_sems, pair, far, _ = state
    pair, far = _rs_chips_wait(send_sems, recv_sems, pair, far, after, name + "_chips_wait")
    return _rs_final_sum(pair, far, name + "_sum")


MATRICES = (("w_in", True), ("w_out", False), ("w_q", False), ("w_kv", True), ("w_o", False), ("w_up", True),
            ("w_down", False), ("w_conv_out", True), ("w_pool_grp", True))
MIX_NAMES = ("w_in", "w_conv_out", "w_pool_grp", "w_out")
REST_NAMES = ("w_q", "w_kv", "w_o", "w_up", "w_down")


def _parts(layer):
    return (("mix", MIX_NAMES), ("rest", REST_NAMES)) if layer == 0 else (("all", MIX_NAMES + REST_NAMES),)


def _to_rows(name, transposed, w, d_model):
    if name == "w_pool_grp":
        w = jnp.swapaxes(w, 1, 2)
    elif transposed:
        w = w.T
    return w.reshape(-1, d_model)


def _from_rows(name, transposed, rows, shard_shape):
    if name == "w_pool_grp":
        g, i, o = shard_shape
        return jnp.swapaxes(rows.reshape(g, o, i), 1, 2)
    if transposed:
        return rows.reshape(shard_shape[1], shard_shape[0]).T
    return rows.reshape(shard_shape)


def _scatter_blocks(name, full, shard_shape, d_model, n_dev=N_DEV):
    if name == "w_pool_grp":
        g, i, o = shard_shape
        return jnp.swapaxes(full.reshape(g, n_dev, o, i), 0, 1).reshape(n_dev, -1, d_model)
    return full.reshape(n_dev, -1, d_model)


def kernel(x, mem, mix_norm_g, w_in, conv_dw_w, conv_dw_b, conv_ln_g, conv_ln_b, w_conv_out, w_pool_grp, pool_scale, w_out, xattn_norm_g, mem_norm_g, w_q, w_kv, w_o, ffn_norm_g, w_up, ffn_dw_w, w_down, final_norm_g, loss_target, m_mix_norm_g, m_w_in, m_conv_dw_w, m_conv_dw_b, m_conv_ln_g, m_conv_ln_b, m_w_conv_out, m_w_pool_grp, m_pool_scale, m_w_out, m_xattn_norm_g, m_mem_norm_g, m_w_q, m_w_kv, m_w_o, m_ffn_norm_g, m_w_up, m_ffn_dw_w, m_w_down, m_final_norm_g, v_mix_norm_g, v_w_in, v_conv_dw_w, v_conv_dw_b, v_conv_ln_g, v_conv_ln_b, v_w_conv_out, v_w_pool_grp, v_pool_scale, v_w_out, v_xattn_norm_g, v_mem_norm_g, v_w_q, v_w_kv, v_w_o, v_ffn_norm_g, v_w_up, v_ffn_dw_w, v_w_down, v_final_norm_g):
    p = dict(locals())
    weight_names = ["mix_norm_g", "w_in", "conv_dw_w", "conv_dw_b", "conv_ln_g", "conv_ln_b", "w_conv_out",
                    "w_pool_grp", "pool_scale", "w_out", "xattn_norm_g", "mem_norm_g", "w_q", "w_kv", "w_o",
                    "ffn_norm_g", "w_up", "ffn_dw_w", "w_down", "final_norm_g"]
    n_batch, seq, d_model = x.shape
    m_len = mem.shape[1]
    depth = w_in.shape[0]
    assert depth == 2, "the exchange schedule below is written for two layers"
    t_dim = n_batch * seq
    c_conv = conv_dw_b.shape[1]
    n_groups = w_pool_grp.shape[1]
    assert w_pool_grp.shape[2] == LANE and c_conv % LANE == 0 and n_groups * LANE == c_conv
    gate_col0 = 2 * c_conv + n_groups * LANE
    pool_col0 = (2 * c_conv) // LANE

    dev = 4 * lax.axis_index("x") + 2 * lax.axis_index("y") + lax.axis_index("c")
    filt = jnp.concatenate([conv_dw_w.reshape(-1), ffn_dw_w.reshape(-1)])
    filt_rows = lax.bitcast_convert_type(filt, BF16).reshape(-1, d_model)
    transposed = dict(MATRICES)
    layout = {part: [(name, transposed[name], _to_rows(name, transposed[name], p[name][0], d_model).shape[0])
                     for name in names] for l in range(depth) for part, names in _parts(l)}
    part_of = {(l, name): part for l in range(depth) for part, names in _parts(l) for name in names}

    def landing(name, shard):
        if name == "w_pool_grp":
            block, axis = jnp.swapaxes(shard, 1, 2), 1
        elif name == "filt":
            block, axis = shard, 0
        else:
            block, axis = (shard.T if transposed[name] else shard), 0
        block = block.astype(BF16)
        rows = block.shape[axis]
        shape = block.shape[:axis] + (N_DEV * rows,) + block.shape[axis + 1:]
        start = (0,) * axis + (dev * rows,) + (0,) * (block.ndim - axis - 1)
        return lax.dynamic_update_slice(lax.empty(shape, BF16), block, start), (rows, axis)

    ag_state = {}
    after = filt_rows
    for l in range(depth):
        for part, names in _parts(l):
            items = [(name, p[name][l]) for name in names]
            if (l, part) == (0, part_of[(0, "w_in")]):
                items.append(("filt", filt_rows))
            lands, specs = zip(*[landing(name, shard) for name, shard in items])
            out = _ag_chips_start(lands, specs, after, f"ag{l}{part}_chips_start")
            ag_state[(l, part)] = ([name for name, _ in items], specs, out)
            after = out[-1]
    all_started = after

    full = [dict() for _ in range(depth)]

    def ensure(l, name, after):
        if name in full[l]:
            return
        part = part_of[(l, name)]
        names, specs, out = ag_state[(l, part)]
        lands = _ag_chips_wait(out[0], out[1], out[2:-1], specs, after, f"ag{l}{part}_chips_wait")
        lands = _ag_pair_forward(lands, specs, f"ag{l}{part}_pair_forward")
        full[l].update(zip(names, lands))

    vec = lambda a: a.reshape(1, -1)
    x2d = x.reshape(t_dim, d_model)
    mem2d = mem.reshape(n_batch * m_len, d_model)
    mem_n = _rmsnorm_fwd(mem2d, vec(mem_norm_g), "mem_norm", after=all_started)
    h_first = _rmsnorm_fwd(x2d, vec(mix_norm_g[0]), "mix_norm_l0", after=mem_n)
    ensure(0, "w_in", h_first)
    filt_all = lax.bitcast_convert_type(full[0]["filt"].reshape(N_DEV, -1, 2), F32)
    n_cw = conv_dw_w.size
    kc, cs = conv_dw_w.shape[1:]
    kf, fs = ffn_dw_w.shape[1:]
    conv_w_full = jnp.moveaxis(filt_all[:, :n_cw].reshape(N_DEV, depth, kc, cs), 0, 2).reshape(depth, kc, N_DEV * cs)
    ffn_w_full = jnp.moveaxis(filt_all[:, n_cw:].reshape(N_DEV, depth, kf, fs), 0, 2).reshape(depth, kf, N_DEV * fs)

    saved = []
    xc = x2d
    for l in range(depth):
        ensure(l, "w_in", xc)
        wl = full[l]
        s = {"x0": xc}
        s["h"] = h_first if l == 0 else _rmsnorm_fwd(xc, vec(mix_norm_g[l]), f"mix_norm_l{l}")
        s["proj"] = _matmul(s["h"], wl["w_in"], "nt", f"in_proj_l{l}", out_dtype=BF16)
        s["y1"] = _glu_conv_fwd(s["proj"], conv_w_full[l], vec(conv_dw_b[l]), n_batch, seq, f"glu_conv_l{l}")
        s["y3"] = _ln_silu_fwd(s["y1"], vec(conv_ln_g[l]), vec(conv_ln_b[l]), f"ln_silu_l{l}")
        s["yc"] = _matmul(s["y3"], wl["w_conv_out"], "nt", f"conv_out_l{l}", out_dtype=BF16)
        s["zp"] = _pool_fwd(s["proj"], pool_col0, n_groups, n_batch, seq, f"pool_l{l}")
        s["yp"] = _grouped(s["zp"], wl["w_pool_grp"], "nt", f"pool_proj_l{l}", out_dtype=BF16)
        s["merged"] = _merge_fwd(s["proj"], gate_col0, s["yc"], s["yp"], vec(pool_scale[l]), f"merge_l{l}")
        s["x1"] = _matmul(s["merged"], wl["w_out"], "nn", f"mix_out_l{l}", res=xc)
        ensure(l, "w_q", s["x1"])
        half_up = wl["w_up"].shape[0] // 2
        up_gate, up_val = (0, half_up), (half_up, half_up)
        s["hq"] = _rmsnorm_fwd(s["x1"], vec(xattn_norm_g[l]), f"xattn_norm_l{l}")
        s["q"] = _matmul(s["hq"], wl["w_q"], "nn", f"q_proj_l{l}", out_dtype=BF16)
        s["kv"] = _matmul(mem_n, wl["w_kv"], "nt", f"kv_proj_l{l}", out_dtype=BF16)
        s["att"] = _attn_fwd(s["q"], s["kv"], n_batch, seq, m_len, f"attn_l{l}")
        s["x2"] = _matmul(s["att"], wl["w_o"], "nn", f"attn_out_l{l}", res=s["x1"])
        s["hf"] = _rmsnorm_fwd(s["x2"], vec(ffn_norm_g[l]), f"ffn_norm_l{l}")
        s["up_g"] = _matmul(s["hf"], wl["w_up"], "nt", f"up_proj_gate_l{l}", out_dtype=BF16, b_window=up_gate)
        s["up_v"] = _matmul(s["hf"], wl["w_up"], "nt", f"up_proj_val_l{l}", out_dtype=BF16, b_window=up_val)
        s["act"] = _ffn_act_fwd(s["up_g"], s["up_v"], ffn_w_full[l], n_batch, seq, f"ffn_act_l{l}")
        xc = _matmul(s["act"], wl["w_down"], "nn", f"down_proj_l{l}", res=s["x2"])
        saved.append(s)

    dx, dxb, dg_final, loss_part = _loss_head(xc, vec(final_norm_g), loss_target.reshape(t_dim, d_model), "loss_head")

    small = {"final_norm_g": dg_final.reshape(-1)}
    big = [dict() for _ in range(depth)]
    rs_state = {}
    rs_after = loss_part

    def rs_begin(l, part):
        pack = lax.empty((N_DEV, sum(nrows for _, _, nrows in layout[part]), d_model), BF16)
        row0 = 0
        for name, _, nrows in layout[part]:
            pieces = big[l][name] if isinstance(big[l][name], tuple) else (big[l][name],)
            d0 = 0
            for piece in pieces:
                blocks = _scatter_blocks(name, piece, p[name].shape[1:], d_model, N_DEV // len(pieces)).astype(BF16)
                pack = lax.dynamic_update_slice(pack, blocks, (d0, row0, 0))
                d0 += blocks.shape[0]
            row0 += nrows
        rs_state[(l, part)] = _reduce_scatter_begin(pack, f"rs{l}{part}")
        return rs_state[(l, part)][4]

    dmem_n = None
    for l in reversed(range(depth)):
        wl, s = full[l], saved[l]
        sm = {}
        dact = _matmul(dxb, wl["w_down"], "nt", f"d_act_l{l}", out_dtype=BF16, after=rs_after)
        big[l]["w_down"] = _matmul(s["act"], dxb, "tn", f"d_w_down_l{l}", out_dtype=BF16)
        dup_g, dup_v, dwf_g, dwf_v = _ffn_act_bwd(s["up_g"], s["up_v"], ffn_w_full[l], dact, n_batch, seq,
                                                  f"ffn_act_bwd_l{l}")
        sm["ffn_dw_w"] = jnp.concatenate([dwf_g, dwf_v], axis=1)
        dx, dxb, dg = _matmul_rmsnorm_bwd((dup_g, dup_v), wl["w_up"], "nn", s["x2"], vec(ffn_norm_g[l]), dx,
                                          f"d_hf_ffn_norm_bwd_l{l}")
        big[l]["w_up"] = (_matmul(dup_g, s["hf"], "tn", f"d_w_up_gate_l{l}", out_dtype=BF16),
                          _matmul(dup_v, s["hf"], "tn", f"d_w_up_val_l{l}", out_dtype=BF16))
        sm["ffn_norm_g"] = dg
        datt = _matmul(dxb, wl["w_o"], "nt", f"d_att_l{l}", out_dtype=BF16, after=rs_after)
        big[l]["w_o"] = _matmul(s["att"], dxb, "tn", f"d_w_o_l{l}", out_dtype=BF16)
        dq, dk, dv = _attn_bwd(s["q"], s["kv"], datt, n_batch, seq, m_len, f"attn_bwd_l{l}")
        dkv = jnp.concatenate([dk, dv], axis=1)
        big[l]["w_kv"] = _matmul(dkv, mem_n, "tn", f"d_w_kv_l{l}", out_dtype=BF16)
        dmem_n = _matmul(dkv, wl["w_kv"], "nn", f"d_mem_l{l}", res=dmem_n)
        big[l]["w_q"] = _matmul(s["hq"], dq, "tn", f"d_w_q_l{l}", out_dtype=BF16)
        dx, dxb, dg = _matmul_rmsnorm_bwd(dq, wl["w_q"], "nt", s["x1"], vec(xattn_norm_g[l]), dx,
                                          f"d_hq_xattn_norm_bwd_l{l}")
        sm["xattn_norm_g"] = dg
        if part_of[(l, "w_q")] != part_of[(l, "w_in")]:
            rs_after = rs_begin(l, part_of[(l, "w_q")])
        dmerged = _matmul(dxb, wl["w_out"], "nt", f"d_merged_l{l}", out_dtype=BF16, after=rs_after)
        big[l]["w_out"] = _matmul(s["merged"], dxb, "tn", f"d_w_out_l{l}", out_dtype=BF16)
        dgc, dgp, dyc, dyp, dscale = _merge_bwd(s["proj"], gate_col0, s["yc"], s["yp"], vec(pool_scale[l]), dmerged,
                                                f"merge_bwd_l{l}")
        sm["pool_scale"] = dscale
        dzp = _grouped(dyp, wl["w_pool_grp"], "nn", f"d_zp_l{l}", out_dtype=BF16)
        big[l]["w_pool_grp"] = _grouped_tn(dyp, s["zp"], n_groups, f"d_w_pool_l{l}")
        du = _pool_bwd(dzp, n_groups, n_batch, seq, f"pool_bwd_l{l}")
        dy3 = _matmul(dyc, wl["w_conv_out"], "nn", f"d_y3_l{l}", out_dtype=BF16)
        big[l]["w_conv_out"] = _matmul(dyc, s["y3"], "tn", f"d_w_conv_out_l{l}", out_dtype=BF16)
        dy1, dlg, dlb = _ln_silu_bwd(s["y1"], vec(conv_ln_g[l]), vec(conv_ln_b[l]), dy3, f"ln_silu_bwd_l{l}")
        sm["conv_ln_g"], sm["conv_ln_b"] = dlg, dlb
        da, dgl, dcw, dcb = _glu_conv_bwd(s["proj"], conv_w_full[l], dy1, n_batch, seq, f"glu_conv_bwd_l{l}")
        sm["conv_dw_w"], sm["conv_dw_b"] = dcw, dcb
        dproj = jnp.concatenate([da, dgl, du, dgc, dgp], axis=1)
        big[l]["w_in"] = _matmul(dproj, s["h"], "tn", f"d_w_in_l{l}", out_dtype=BF16)
        dx, dxb, dg = _matmul_rmsnorm_bwd(dproj, wl["w_in"], "nn", s["x0"], vec(mix_norm_g[l]), dx,
                                          f"d_h_mix_norm_bwd_l{l}")
        sm["mix_norm_g"] = dg
        for k, val in sm.items():
            small[(l, k)] = val.reshape(-1)
        rs_after = rs_begin(l, part_of[(l, "w_in")])
    _, _, dg_mem = _rmsnorm_bwd(mem2d, vec(mem_norm_g), dmem_n, None, "mem_norm_bwd")
    small["mem_norm_g"] = dg_mem.reshape(-1)
    small["loss"] = loss_part.reshape(-1)

    grads = {}
    per_layer = {name: [None] * depth for name, _ in MATRICES}
    for l in reversed(range(depth)):
        for part, _ in reversed(_parts(l)):
            mat_grads = _reduce_scatter_end(rs_state[(l, part)], rs_after, f"rs{l}{part}")
            row0 = 0
            for name, tr, nrows in layout[part]:
                per_layer[name][l] = _from_rows(name, tr, mat_grads[row0:row0 + nrows], p[name].shape[1:])
                row0 += nrows
    for name, _ in MATRICES:
        grads[name] = jnp.stack(per_layer[name])

    keys = list(small.keys())
    flat = jnp.concatenate([small[k] for k in keys])
    n_small = flat.shape[0]
    rows_small = -(-n_small // (SUBLANE * d_model)) * SUBLANE
    flat = jnp.pad(flat, (0, rows_small * d_model - n_small)).reshape(rows_small, d_model)
    every = _all_gather(flat, "small_all_gather")
    total = _sum_rows([every[i] for i in range(N_DEV)], F32, "small_sum").reshape(-1)
    off = 0
    red = {}
    for k in keys:
        red[k] = total[off:off + small[k].shape[0]]
        off += small[k].shape[0]
    loss = red["loss"][0]
    for name in ("mix_norm_g", "conv_dw_b", "conv_ln_g", "conv_ln_b", "pool_scale", "xattn_norm_g", "ffn_norm_g"):
        grads[name] = jnp.stack([red[(l, name)] for l in range(depth)])
    grads["conv_dw_w"] = jnp.stack([
        lax.dynamic_slice_in_dim(red[(l, "conv_dw_w")].reshape(kc, N_DEV * cs), dev * cs, cs, axis=1)
        for l in range(depth)])
    grads["ffn_dw_w"] = jnp.stack([
        lax.dynamic_slice_in_dim(red[(l, "ffn_dw_w")].reshape(kf, N_DEV * fs), dev * fs, fs, axis=1)
        for l in range(depth)])
    grads["mem_norm_g"] = red["mem_norm_g"]
    grads["final_norm_g"] = red["final_norm_g"]

    deltas, new_m, new_v = {}, {}, {}
    for name in weight_names:
        deltas[name], new_m[name], new_v[name] = _adamw(p[name], grads[name], p["m_" + name], p["v_" + name],
                                                        f"adamw_{name}")
    grad_x = dx.reshape(n_batch, seq, d_model)
    return (loss, grad_x, *[grads[n] for n in weight_names], *[deltas[n] for n in weight_names],
            *[new_m[n] for n in weight_names], *[new_v[n] for n in weight_names])
```

```python
import functools

import jax
import jax.numpy as jnp
from jax import lax
from jax.experimental import pallas as pl
from jax.experimental.pallas import tpu as pltpu

F32 = jnp.float32
BF16 = jnp.bfloat16
MESH = pl.DeviceIdType.MESH

N_DEV = 8
EPS = 1e-6
V7X_VMEM_BYTES = 64 * 1024 * 1024
VMEM_LIMIT = (V7X_VMEM_BYTES * 3) // 4
LANE = 128
SUBLANE = 8

CONV_HALO = 32
POOL_HALO = 16
FFN_HALO = 8
FFN_LANE_GROUPS = 2
POOL_WINDOW_MAX = 16
XA_HEADS = 4

ADAM_LR = 0.001
ADAM_B1 = 0.9
ADAM_B2 = 0.999
ADAM_EPS = 1e-08
ADAM_WD = 0.01
ADAM_STEP = 10

GELU_C0 = 0.7978845608028654
GELU_C1 = 0.044715


ANY_SPEC = pl.BlockSpec(memory_space=pl.ANY)


def _tile(n, cap, mult=LANE):
    if n <= cap:
        return n
    best = None
    for d in range(mult, cap + 1, mult):
        if n % d == 0:
            best = d
    assert best is not None, (n, cap, mult)
    return best


def _params(*sem):
    return pltpu.CompilerParams(dimension_semantics=sem, vmem_limit_bytes=VMEM_LIMIT)


def _delayed(x, halo, rows, n_shifts):
    for r in range(min(SUBLANE, n_shifts)):
        xr = x if r == 0 else pltpu.roll(x, r, 0)
        for s in range(r, n_shifts, SUBLANE):
            yield s, xr[halo - (s - r):halo - (s - r) + rows]


def _advanced(x, rows, n_shifts):
    for r in range(min(SUBLANE, n_shifts)):
        xr = x if r == 0 else pltpu.roll(x, x.shape[0] - r, 0)
        for s in range(r, n_shifts, SUBLANE):
            yield s, xr[s - r:s - r + rows]


def _sig(x):
    return 1.0 / (1.0 + jnp.exp(-x))


def _bs(shape, imap):
    return pl.BlockSpec(shape, imap)


def _mxu_tile(n, cap):
    if n <= cap:
        return n
    best = {mult: max((d for d in range(mult, cap + 1, mult) if n % d == 0), default=0) for mult in (2 * LANE, LANE)}
    assert best[LANE] > 0, (n, cap)
    return best[2 * LANE] if 2 * best[2 * LANE] >= best[LANE] else best[LANE]


def _matmul(a, b, mode, name, res=None, out_dtype=F32, after=None, b_window=None):
    b_row0, b_rows = b_window if b_window is not None else (0, b.shape[0])
    if mode == "tn":
        k_dim, m_dim = a.shape
        k2, n_dim = b_rows, b.shape[1]
    elif mode == "nn":
        m_dim, k_dim = a.shape
        k2, n_dim = b_rows, b.shape[1]
    else:
        m_dim, k_dim = a.shape
        n_dim, k2 = b_rows, b.shape[1]
    assert k_dim == k2, (name, a.shape, b.shape)
    size = lambda t: jnp.dtype(t).itemsize
    tm = _mxu_tile(m_dim, 2816 if mode == "tn" else 1024)
    tn = _mxu_tile(n_dim, 2816)
    fixed = tm * tn * (2 * size(out_dtype) + (2 * size(res.dtype) if res is not None else 0) + 4)
    for cap in (2816, 2048, 1792, 1024, 512):
        tk = _mxu_tile(k_dim, cap)
        if fixed + 2 * tk * (tm * size(a.dtype) + tn * size(b.dtype)) <= VMEM_LIMIT - 8 * 1024 * 1024:
            break
    nk = k_dim // tk
    use_acc = nk > 1 and out_dtype != F32
    if mode == "tn":
        a_spec, ca = _bs((tk, tm), lambda i, j, k: (k, i)), 0
    else:
        a_spec, ca = _bs((tm, tk), lambda i, j, k: (i, k)), 1
    if mode == "nt":
        assert b_row0 % tn == 0
        b_spec, cb = _bs((tn, tk), lambda i, j, k: (j + b_row0 // tn, k)), 1
    else:
        assert b_row0 % tk == 0
        b_spec, cb = _bs((tk, tn), lambda i, j, k: (k + b_row0 // tk, j)), 0
    dims = (((ca,), (cb,)), ((), ()))
    o_spec = _bs((tm, tn), lambda i, j, k: (i, j))
    has_res = res is not None

    def body(*refs):
        a_ref, b_ref = refs[:2]
        r_ref = refs[2] if has_res else None
        o_ref = refs[n_in]
        k = pl.program_id(2)
        part = lax.dot_general(a_ref[...].astype(BF16), b_ref[...].astype(BF16), dims,
                               preferred_element_type=F32)
        if nk == 1:
            if has_res:
                part = part + r_ref[...].astype(F32)
            o_ref[...] = part.astype(out_dtype)
            return
        acc = refs[-1] if use_acc else o_ref

        @pl.when(k == 0)
        def _():
            acc[...] = part + r_ref[...].astype(F32) if has_res else part

        @pl.when(k > 0)
        def _():
            acc[...] += part

        if use_acc:
            @pl.when(k == nk - 1)
            def _():
                o_ref[...] = acc[...].astype(out_dtype)

    in_specs = [a_spec, b_spec] + ([o_spec] if has_res else [])
    args = (a, b) + ((res,) if has_res else ())
    if after is not None:
        in_specs.append(ANY_SPEC)
        args += (after,)
    n_in = len(args)
    return pl.pallas_call(
        body, out_shape=jax.ShapeDtypeStruct((m_dim, n_dim), out_dtype),
        grid=(m_dim // tm, n_dim // tn, nk), in_specs=in_specs, out_specs=o_spec,
        scratch_shapes=[pltpu.VMEM((tm, tn), F32)] if use_acc else [], name=name,
        compiler_params=_params("parallel", "parallel", "arbitrary"))(*args)


def _grouped(a, w, mode, name, out_dtype=F32):
    t_dim = a.shape[0]
    g_dim, r_dim, c_dim = w.shape
    ka, no = (c_dim, r_dim) if mode == "nt" else (r_dim, c_dim)
    tm = _tile(t_dim, 2048)
    dims = (((1,), (1 if mode == "nt" else 0,)), ((), ()))

    def body(a_ref, w_ref, o_ref):
        o_ref[...] = lax.dot_general(a_ref[...].astype(BF16), w_ref[...].astype(BF16), dims,
                                     preferred_element_type=F32).astype(out_dtype)

    return pl.pallas_call(
        body, out_shape=jax.ShapeDtypeStruct((t_dim, g_dim * no), out_dtype),
        grid=(t_dim // tm, g_dim),
        in_specs=[_bs((tm, ka), lambda i, g: (i, g)), _bs((None, r_dim, c_dim), lambda i, g: (g, 0, 0))],
        out_specs=_bs((tm, no), lambda i, g: (i, g)), name=name,
        compiler_params=_params("parallel", "parallel"))(a, w)


def _grouped_tn(a, b, g_dim, name):
    t_dim = a.shape[0]
    ra = a.shape[1] // g_dim
    cb = b.shape[1] // g_dim
    tm = _tile(t_dim, 2048)
    nt = t_dim // tm

    def body(a_ref, b_ref, o_ref):
        part = lax.dot_general(a_ref[...].astype(BF16), b_ref[...].astype(BF16), (((0,), (0,)), ((), ())),
                               preferred_element_type=F32)

        @pl.when(pl.program_id(1) == 0)
        def _():
            o_ref[...] = part

        @pl.when(pl.program_id(1) > 0)
        def _():
            o_ref[...] += part

    return pl.pallas_call(
        body, out_shape=jax.ShapeDtypeStruct((g_dim, ra, cb), F32), grid=(g_dim, nt),
        in_specs=[_bs((tm, ra), lambda g, i: (i, g)), _bs((tm, cb), lambda g, i: (i, g))],
        out_specs=_bs((None, ra, cb), lambda g, i: (g, 0, 0)), name=name,
        compiler_params=_params("parallel", "arbitrary"))(a, b)


def _rmsnorm_fwd(x, g, name, after=None):
    t_dim, d = x.shape
    tm = _tile(t_dim, 512)

    def body(x_ref, g_ref, *rest):
        o_ref = rest[-1]
        xv = x_ref[...]
        r = lax.rsqrt(jnp.mean(xv * xv, axis=-1, keepdims=True) + EPS)
        o_ref[...] = (xv * r * g_ref[...]).astype(BF16)

    return pl.pallas_call(
        body, out_shape=jax.ShapeDtypeStruct((t_dim, d), BF16), grid=(t_dim // tm,),
        in_specs=[_bs((tm, d), lambda i: (i, 0)), _bs((1, d), lambda i: (0, 0))] + ([ANY_SPEC] if after is not None else []),
        out_specs=_bs((tm, d), lambda i: (i, 0)), name=name,
        compiler_params=_params("parallel"))(x, g, *([after] if after is not None else []))


def _rmsnorm_bwd(x, g, dh, dx_in, name):
    t_dim, d = x.shape
    tm = _tile(t_dim, 512)
    has_in = dx_in is not None

    def body(*refs):
        if has_in:
            x_ref, g_ref, dh_ref, di_ref, dx_ref, dxb_ref, dg_ref = refs
        else:
            x_ref, g_ref, dh_ref, dx_ref, dxb_ref, dg_ref = refs
        xv = x_ref[...]
        r = lax.rsqrt(jnp.mean(xv * xv, axis=-1, keepdims=True) + EPS)
        xh = xv * r
        dhv = dh_ref[...].astype(F32)
        dxh = dhv * g_ref[...]
        dx = r * (dxh - xh * jnp.mean(dxh * xh, axis=-1, keepdims=True))
        if has_in:
            dx = dx + di_ref[...]
        dx_ref[...] = dx
        dxb_ref[...] = dx.astype(BF16)
        part = jnp.sum(dhv * xh, axis=0, keepdims=True)

        @pl.when(pl.program_id(0) == 0)
        def _():
            dg_ref[...] = part

        @pl.when(pl.program_id(0) > 0)
        def _():
            dg_ref[...] += part

    row = _bs((tm, d), lambda i: (i, 0))
    vec = _bs((1, d), lambda i: (0, 0))
    args = (x, g, dh) + ((dx_in,) if has_in else ())
    return pl.pallas_call(
        body, out_shape=(jax.ShapeDtypeStruct((t_dim, d), F32), jax.ShapeDtypeStruct((t_dim, d), BF16),
                         jax.ShapeDtypeStruct((1, d), F32)),
        grid=(t_dim // tm,), in_specs=[row, vec, row] + ([row] if has_in else []),
        out_specs=(row, row, vec), name=name, compiler_params=_params("arbitrary"))(*args)


def _matmul_rmsnorm_bwd(a, b, mode, x, g, dx_in, name, res=None, b_window=None):
    pieces = a if isinstance(a, tuple) else (a,)
    n_p = len(pieces)
    b_row0, b_rows = b_window if b_window is not None else (0, b.shape[0])
    m_dim, k_piece = pieces[0].shape
    assert all(t.shape == pieces[0].shape for t in pieces)
    k_dim = n_p * k_piece
    d = x.shape[1]
    assert (b_rows, b.shape[1]) == ((k_dim, d) if mode == "nn" else (d, k_dim)), (name, pieces[0].shape, b.shape)
    tm = _mxu_tile(m_dim, 512)
    tk = _mxu_tile(k_piece, 1792)
    nkp = k_piece // tk
    nk = n_p * nkp
    has_res = res is not None
    if mode == "nt":
        assert b_row0 == 0
        b_spec, cb = _bs((d, tk), lambda i, k: (0, k)), 1
    else:
        assert b_row0 % tk == 0
        b_spec, cb = _bs((tk, d), lambda i, k: (k + b_row0 // tk, 0)), 0
    dims = (((1,), (cb,)), ((), ()))

    def body(*refs):
        b_ref = refs[n_p]
        r_ref = refs[n_p + 1] if has_res else None
        x_ref, g_ref, di_ref, dx_ref, dxb_ref, dg_ref = refs[n_p + 1 + has_res:n_p + 7 + has_res]
        i, k = pl.program_id(0), pl.program_id(1)

        def finish(dhv):
            if has_res:
                dhv = dhv + r_ref[...].astype(F32)
            xv = x_ref[...]
            r = lax.rsqrt(jnp.mean(xv * xv, axis=-1, keepdims=True) + EPS)
            xh = xv * r
            dxh = dhv * g_ref[...]
            dx = r * (dxh - xh * jnp.mean(dxh * xh, axis=-1, keepdims=True)) + di_ref[...]
            dx_ref[...] = dx
            dxb_ref[...] = dx.astype(BF16)
            dg_part = jnp.sum(dhv * xh, axis=0, keepdims=True)

            @pl.when(i == 0)
            def _():
                dg_ref[...] = dg_part

            @pl.when(i > 0)
            def _():
                dg_ref[...] += dg_part

        def step(a_ref):
            part = lax.dot_general(a_ref[...].astype(BF16), b_ref[...].astype(BF16), dims, preferred_element_type=F32)
            if nk == 1:
                finish(part)
                return
            acc = refs[-1]

            @pl.when(k == 0)
            def _():
                acc[...] = part

            @pl.when(jnp.logical_and(k > 0, k < nk - 1))
            def _():
                acc[...] += part

            @pl.when(k == nk - 1)
            def _():
                finish(acc[...] + part)

        if n_p == 1:
            step(refs[0])
        else:
            for q in range(n_p):
                pl.when(jnp.logical_and(k >= q * nkp, k < (q + 1) * nkp))(functools.partial(step, refs[q]))

    row = _bs((tm, d), lambda i, k: (i, 0))
    vec = _bs((1, d), lambda i, k: (0, 0))
    a_specs = [_bs((tm, tk), lambda i, k, q=q: (i, jnp.clip(k - q * nkp, 0, nkp - 1))) for q in range(n_p)]
    in_specs = a_specs + [b_spec] + ([row] if has_res else []) + [row, vec, row]
    args = pieces + (b,) + ((res,) if has_res else ()) + (x, g, dx_in)
    return pl.pallas_call(
        body, out_shape=(jax.ShapeDtypeStruct((m_dim, d), F32), jax.ShapeDtypeStruct((m_dim, d), BF16),
                         jax.ShapeDtypeStruct((1, d), F32)),
        grid=(m_dim // tm, nk), in_specs=in_specs, out_specs=(row, row, vec),
        scratch_shapes=[pltpu.VMEM((tm, d), F32)] if nk > 1 else [], name=name,
        compiler_params=_params("arbitrary", "arbitrary"))(*args)


def _loss_head(x, g, tgt, name):
    t_dim, d = x.shape
    tm = _tile(t_dim, 512)

    def body(x_ref, g_ref, t_ref, dx_ref, dxb_ref, dg_ref, loss_ref):
        xv = x_ref[...]
        gv = g_ref[...]
        r = lax.rsqrt(jnp.mean(xv * xv, axis=-1, keepdims=True) + EPS)
        xh = xv * r
        err = xh * gv - t_ref[...]
        dy = err * (1.0 / d)
        dxh = dy * gv
        dx = r * (dxh - xh * jnp.mean(dxh * xh, axis=-1, keepdims=True))
        dx_ref[...] = dx
        dxb_ref[...] = dx.astype(BF16)
        dg_part = jnp.sum(dy * xh, axis=0, keepdims=True)
        loss_part = jnp.full((1, LANE), 0.5 * jnp.sum(jnp.mean(err * err, axis=-1, keepdims=True)), F32)

        @pl.when(pl.program_id(0) == 0)
        def _():
            dg_ref[...] = dg_part
            loss_ref[...] = loss_part

        @pl.when(pl.program_id(0) > 0)
        def _():
            dg_ref[...] += dg_part
            loss_ref[...] += loss_part

    row = _bs((tm, d), lambda i: (i, 0))
    vec = _bs((1, d), lambda i: (0, 0))
    return pl.pallas_call(
        body, out_shape=(jax.ShapeDtypeStruct((t_dim, d), F32), jax.ShapeDtypeStruct((t_dim, d), BF16),
                         jax.ShapeDtypeStruct((1, d), F32), jax.ShapeDtypeStruct((1, LANE), F32)),
        grid=(t_dim // tm,), in_specs=[row, vec, row],
        out_specs=(row, row, vec, _bs((1, LANE), lambda i: (0, 0))), name=name,
        compiler_params=_params("arbitrary"))(x, g, tgt)


def _glu_conv_fwd(proj, dw_w, dw_b, n_batch, seq, name):
    kk, cc = dw_w.shape
    nj = cc // LANE
    ch = min(256, seq)

    def body(a_ref, gl_ref, w_ref, b_ref, o_ref, pad):
        pad[0:CONV_HALO, :] = jnp.zeros((CONV_HALO, LANE), F32)
        pad[CONV_HALO:CONV_HALO + seq, :] = a_ref[...].astype(F32) * _sig(gl_ref[...].astype(F32))
        for c0 in range(0, seq, ch):
            acc = jnp.broadcast_to(b_ref[...], (ch, LANE))
            for k in range(kk):
                acc = acc + w_ref[k:k + 1, :] * pad[pl.ds(c0 + CONV_HALO - (kk - 1) + k, ch), :]
            o_ref[c0:c0 + ch, :] = acc

    return pl.pallas_call(
        body, out_shape=jax.ShapeDtypeStruct((n_batch * seq, cc), F32), grid=(n_batch, nj),
        in_specs=[_bs((seq, LANE), lambda b, j: (b, j)), _bs((seq, LANE), lambda b, j: (b, nj + j)),
                  _bs((kk, LANE), lambda b, j: (0, j)), _bs((1, LANE), lambda b, j: (0, j))],
        out_specs=_bs((seq, LANE), lambda b, j: (b, j)),
        scratch_shapes=[pltpu.VMEM((seq + CONV_HALO, LANE), F32)], name=name,
        compiler_params=_params("parallel", "parallel"))(proj, proj, dw_w, dw_b)


def _glu_conv_bwd(proj, dw_w, dy1, n_batch, seq, name):
    kk, cc = dw_w.shape
    nj = cc // LANE
    ch = min(256, seq)

    def body(a_ref, gl_ref, dy_ref, w_ref, da_ref, dgl_ref, dw_ref, db_ref, padf, padb):
        first = pl.program_id(1) == 0
        padf[0:CONV_HALO, :] = jnp.zeros((CONV_HALO, LANE), F32)
        padf[CONV_HALO:CONV_HALO + seq, :] = a_ref[...].astype(F32) * _sig(gl_ref[...].astype(F32))
        padb[0:seq, :] = dy_ref[...]
        padb[seq:seq + CONV_HALO, :] = jnp.zeros((CONV_HALO, LANE), F32)

        @pl.when(first)
        def _():
            dw_ref[...] = jnp.zeros((kk, LANE), F32)
            db_ref[...] = jnp.zeros((1, LANE), F32)

        dws = [jnp.zeros((1, LANE), F32) for _ in range(kk)]
        for c0 in range(0, seq, ch):
            acc = jnp.zeros((ch, LANE), F32)
            y0 = padf[CONV_HALO + c0:CONV_HALO + c0 + ch, :]
            for k in range(kk):
                win = padb[pl.ds(c0 + (kk - 1) - k, ch), :]
                acc = acc + w_ref[k:k + 1, :] * win
                dws[k] = dws[k] + jnp.sum(win * y0, axis=0, keepdims=True)
            sg = _sig(gl_ref[c0:c0 + ch, :].astype(F32))
            da_ref[c0:c0 + ch, :] = (acc * sg).astype(BF16)
            dgl_ref[c0:c0 + ch, :] = (acc * a_ref[c0:c0 + ch, :].astype(F32) * sg * (1.0 - sg)).astype(BF16)
        for k in range(kk):
            dw_ref[k:k + 1, :] += dws[k]
        db_ref[...] += jnp.sum(dy_ref[...], axis=0, keepdims=True)

    tok = _bs((seq, LANE), lambda j, b: (b, j))
    t_dim = n_batch * seq
    return pl.pallas_call(
        body, out_shape=(jax.ShapeDtypeStruct((t_dim, cc), BF16), jax.ShapeDtypeStruct((t_dim, cc), BF16),
                         jax.ShapeDtypeStruct((kk, cc), F32), jax.ShapeDtypeStruct((1, cc), F32)),
        grid=(nj, n_batch),
        in_specs=[tok, _bs((seq, LANE), lambda j, b: (b, nj + j)), tok, _bs((kk, LANE), lambda j, b: (0, j))],
        out_specs=(tok, tok, _bs((kk, LANE), lambda j, b: (0, j)), _bs((1, LANE), lambda j, b: (0, j))),
        scratch_shapes=[pltpu.VMEM((seq + CONV_HALO, LANE), F32), pltpu.VMEM((seq + CONV_HALO, LANE), F32)],
        name=name, compiler_params=_params("parallel", "arbitrary"))(proj, proj, dy1, dw_w)


def _ln_silu_fwd(y1, g, b, name):
    t_dim, c = y1.shape
    tm = _tile(t_dim, 512)

    def body(y_ref, g_ref, b_ref, o_ref):
        yv = y_ref[...]
        xc = yv - jnp.mean(yv, axis=-1, keepdims=True)
        rstd = lax.rsqrt(jnp.mean(xc * xc, axis=-1, keepdims=True) + EPS)
        y2 = xc * rstd * g_ref[...] + b_ref[...]
        o_ref[...] = (y2 * _sig(y2)).astype(BF16)

    row = _bs((tm, c), lambda i: (i, 0))
    vec = _bs((1, c), lambda i: (0, 0))
    return pl.pallas_call(
        body, out_shape=jax.ShapeDtypeStruct((t_dim, c), BF16), grid=(t_dim // tm,),
        in_specs=[row, vec, vec], out_specs=row, name=name, compiler_params=_params("parallel"))(y1, g, b)


def _ln_silu_bwd(y1, g, b, dy3, name):
    t_dim, c = y1.shape
    tm = _tile(t_dim, 512)

    def body(y_ref, g_ref, b_ref, d_ref, dy_ref, dg_ref, db_ref):
        yv = y_ref[...]
        gv = g_ref[...]
        xc = yv - jnp.mean(yv, axis=-1, keepdims=True)
        rstd = lax.rsqrt(jnp.mean(xc * xc, axis=-1, keepdims=True) + EPS)
        yh = xc * rstd
        y2 = yh * gv + b_ref[...]
        s = _sig(y2)
        dy2 = d_ref[...].astype(F32) * (s * (1.0 + y2 * (1.0 - s)))
        dyh = dy2 * gv
        dy_ref[...] = rstd * (dyh - jnp.mean(dyh, axis=-1, keepdims=True)
                              - yh * jnp.mean(dyh * yh, axis=-1, keepdims=True))
        dg_part = jnp.sum(dy2 * yh, axis=0, keepdims=True)
        db_part = jnp.sum(dy2, axis=0, keepdims=True)

        @pl.when(pl.program_id(0) == 0)
        def _():
            dg_ref[...] = dg_part
            db_ref[...] = db_part

        @pl.when(pl.program_id(0) > 0)
        def _():
            dg_ref[...] += dg_part
            db_ref[...] += db_part

    row = _bs((tm, c), lambda i: (i, 0))
    vec = _bs((1, c), lambda i: (0, 0))
    return pl.pallas_call(
        body, out_shape=(jax.ShapeDtypeStruct((t_dim, c), F32), jax.ShapeDtypeStruct((1, c), F32),
                         jax.ShapeDtypeStruct((1, c), F32)),
        grid=(t_dim // tm,), in_specs=[row, vec, vec, row], out_specs=(row, vec, vec), name=name,
        compiler_params=_params("arbitrary"))(y1, g, b, dy3)


def _pool_fwd(proj, col0, n_groups, n_batch, seq, name):
    ch = min(256, seq)

    def body(u_ref, o_ref, pad):
        w = lax.shift_left(jnp.int32(2), pl.program_id(1))
        pad[0:POOL_HALO, :] = jnp.zeros((POOL_HALO, LANE), F32)
        pad[POOL_HALO:POOL_HALO + seq, :] = u_ref[...].astype(F32)
        for c0 in range(0, seq, ch):
            acc = jnp.zeros((ch, LANE), F32)
            for j in range(POOL_WINDOW_MAX):
                acc = acc + jnp.where(j < w, 1.0, 0.0).astype(F32) * pad[pl.ds(c0 + POOL_HALO - j, ch), :]
            t = c0 + lax.broadcasted_iota(jnp.int32, (ch, LANE), 0)
            cnt = jnp.minimum(t + 1, w).astype(F32)
            o_ref[c0:c0 + ch, :] = (acc / cnt - pad[POOL_HALO + c0:POOL_HALO + c0 + ch, :]).astype(BF16)

    return pl.pallas_call(
        body, out_shape=jax.ShapeDtypeStruct((n_batch * seq, n_groups * LANE), BF16), grid=(n_batch, n_groups),
        in_specs=[_bs((seq, LANE), lambda b, g: (b, col0 + g))], out_specs=_bs((seq, LANE), lambda b, g: (b, g)),
        scratch_shapes=[pltpu.VMEM((seq + POOL_HALO, LANE), F32)], name=name,
        compiler_params=_params("parallel", "parallel"))(proj)


def _pool_bwd(dzp, n_groups, n_batch, seq, name):
    ch = min(256, seq)

    def body(d_ref, o_ref, pad):
        w = lax.shift_left(jnp.int32(2), pl.program_id(1))
        for c0 in range(0, seq, ch):
            t = c0 + lax.broadcasted_iota(jnp.int32, (ch, LANE), 0)
            cnt = jnp.minimum(t + 1, w).astype(F32)
            pad[c0:c0 + ch, :] = d_ref[c0:c0 + ch, :].astype(F32) / cnt
        pad[seq:seq + POOL_HALO, :] = jnp.zeros((POOL_HALO, LANE), F32)
        for c0 in range(0, seq, ch):
            acc = jnp.zeros((ch, LANE), F32)
            for j in range(POOL_WINDOW_MAX):
                acc = acc + jnp.where(j < w, 1.0, 0.0).astype(F32) * pad[pl.ds(c0 + j, ch), :]
            o_ref[c0:c0 + ch, :] = (acc - d_ref[c0:c0 + ch, :].astype(F32)).astype(BF16)

    tok = _bs((seq, LANE), lambda b, g: (b, g))
    return pl.pallas_call(
        body, out_shape=jax.ShapeDtypeStruct((n_batch * seq, n_groups * LANE), BF16), grid=(n_batch, n_groups),
        in_specs=[tok], out_specs=tok, scratch_shapes=[pltpu.VMEM((seq + POOL_HALO, LANE), F32)], name=name,
        compiler_params=_params("parallel", "parallel"))(dzp)


def _merge_fwd(proj, col0, yc, yp, scale, name):
    t_dim, d = yc.shape
    half = d // 2
    tm = _tile(t_dim, 512)
    c0 = col0 // half

    def body(gc_ref, gp_ref, yc_ref, yp_ref, s_ref, o_ref):
        f32 = lambda r: r[...].astype(F32)
        o_ref[...] = (_sig(f32(gc_ref)) * f32(yc_ref) + _sig(f32(gp_ref)) * (f32(yp_ref) * s_ref[...])).astype(BF16)

    blk = _bs((tm, half), lambda i, j: (i, j))
    return pl.pallas_call(
        body, out_shape=jax.ShapeDtypeStruct((t_dim, d), BF16), grid=(t_dim // tm, 2),
        in_specs=[_bs((tm, half), lambda i, j: (i, c0 + j)), _bs((tm, half), lambda i, j: (i, c0 + 2 + j)),
                  blk, blk, _bs((1, half), lambda i, j: (0, j))],
        out_specs=blk, name=name, compiler_params=_params("parallel", "parallel"))(proj, proj, yc, yp, scale)


def _merge_bwd(proj, col0, yc, yp, scale, dm, name):
    t_dim, d = yc.shape
    half = d // 2
    tm = _tile(t_dim, 512)
    c0 = col0 // half

    def body(gc_ref, gp_ref, yc_ref, yp_ref, s_ref, dm_ref, dgc_ref, dgp_ref, dyc_ref, dyp_ref, ds_ref):
        dmv = dm_ref[...].astype(F32)
        sgc = _sig(gc_ref[...].astype(F32))
        sgp = _sig(gp_ref[...].astype(F32))
        sv = s_ref[...]
        ypre = yp_ref[...].astype(F32)
        dgc_ref[...] = (dmv * yc_ref[...].astype(F32) * sgc * (1.0 - sgc)).astype(BF16)
        dgp_ref[...] = (dmv * (ypre * sv) * sgp * (1.0 - sgp)).astype(BF16)
        dyc_ref[...] = (dmv * sgc).astype(BF16)
        dyp = dmv * sgp
        dyp_ref[...] = (dyp * sv).astype(BF16)
        part = jnp.sum(dyp * ypre, axis=0, keepdims=True)

        @pl.when(pl.program_id(1) == 0)
        def _():
            ds_ref[...] = part

        @pl.when(pl.program_id(1) > 0)
        def _():
            ds_ref[...] += part

    blk = _bs((tm, half), lambda j, i: (i, j))
    big = jax.ShapeDtypeStruct((t_dim, d), BF16)
    return pl.pallas_call(
        body, out_shape=(big, big, big, big, jax.ShapeDtypeStruct((1, d), F32)), grid=(2, t_dim // tm),
        in_specs=[_bs((tm, half), lambda j, i: (i, c0 + j)), _bs((tm, half), lambda j, i: (i, c0 + 2 + j)),
                  blk, blk, _bs((1, half), lambda j, i: (0, j)), blk],
        out_specs=(blk, blk, blk, blk, _bs((1, half), lambda j, i: (0, j))), name=name,
        compiler_params=_params("parallel", "arbitrary"))(proj, proj, yc, yp, scale, dm)


def _attn_fwd(q, kv, n_batch, seq, m_len, name):
    d = q.shape[1]
    hd = d // XA_HEADS
    tq = _tile(seq, 1024)
    nq = seq // tq
    scale = hd ** -0.5

    def body(q_ref, k_ref, v_ref, o_ref):
        sc = lax.dot_general(q_ref[...].astype(BF16), k_ref[...].astype(BF16), (((1,), (1,)), ((), ())),
                             preferred_element_type=F32) * scale
        p = jnp.exp(sc - jnp.max(sc, axis=-1, keepdims=True))
        pr = p / jnp.sum(p, axis=-1, keepdims=True)
        o_ref[...] = jnp.dot(pr.astype(BF16), v_ref[...].astype(BF16), preferred_element_type=F32).astype(BF16)

    return pl.pallas_call(
        body, out_shape=jax.ShapeDtypeStruct((n_batch * seq, d), BF16), grid=(n_batch, XA_HEADS, nq),
        in_specs=[_bs((tq, hd), lambda b, h, i: (b * nq + i, h)), _bs((m_len, hd), lambda b, h, i: (b, h)),
                  _bs((m_len, hd), lambda b, h, i: (b, XA_HEADS + h))],
        out_specs=_bs((tq, hd), lambda b, h, i: (b * nq + i, h)), name=name,
        compiler_params=_params("parallel", "parallel", "parallel"))(q, kv, kv)


def _attn_bwd(q, kv, datt, n_batch, seq, m_len, name):
    d = q.shape[1]
    hd = d // XA_HEADS
    tq = _tile(seq, 1024)
    nq = seq // tq
    scale = hd ** -0.5

    def body(q_ref, k_ref, v_ref, do_ref, dq_ref, dk_ref, dv_ref):
        qb = q_ref[...].astype(BF16)
        kb = k_ref[...].astype(BF16)
        vb = v_ref[...].astype(BF16)
        dob = do_ref[...].astype(BF16)
        sc = lax.dot_general(qb, kb, (((1,), (1,)), ((), ())), preferred_element_type=F32) * scale
        p = jnp.exp(sc - jnp.max(sc, axis=-1, keepdims=True))
        pr = p / jnp.sum(p, axis=-1, keepdims=True)
        dpr = lax.dot_general(dob, vb, (((1,), (1,)), ((), ())), preferred_element_type=F32)
        dsc = pr * (dpr - jnp.sum(dpr * pr, axis=-1, keepdims=True)) * scale
        dsb = dsc.astype(BF16)
        dq_ref[...] = jnp.dot(dsb, kb, preferred_element_type=F32).astype(BF16)
        dv_part = lax.dot_general(pr.astype(BF16), dob, (((0,), (0,)), ((), ())), preferred_element_type=F32)
        dk_part = lax.dot_general(dsb, qb, (((0,), (0,)), ((), ())), preferred_element_type=F32)

        @pl.when(pl.program_id(2) == 0)
        def _():
            dk_ref[...] = dk_part
            dv_ref[...] = dv_part

        @pl.when(pl.program_id(2) > 0)
        def _():
            dk_ref[...] += dk_part
            dv_ref[...] += dv_part

    qs = _bs((tq, hd), lambda b, h, i: (b * nq + i, h))
    ks = _bs((m_len, hd), lambda b, h, i: (b, h))
    return pl.pallas_call(
        body, out_shape=(jax.ShapeDtypeStruct((n_batch * seq, d), BF16), jax.ShapeDtypeStruct((n_batch * m_len, d), F32),
                         jax.ShapeDtypeStruct((n_batch * m_len, d), F32)),
        grid=(n_batch, XA_HEADS, nq),
        in_specs=[qs, ks, _bs((m_len, hd), lambda b, h, i: (b, XA_HEADS + h)), qs],
        out_specs=(qs, ks, ks), name=name,
        compiler_params=_params("parallel", "parallel", "arbitrary"))(q, kv, kv, datt)


def _gelu_parts(g):
    th = jnp.tanh(GELU_C0 * (g + GELU_C1 * g * g * g))
    return th, 0.5 * g * (1.0 + th)


def _ffn_act_fwd(up_g, up_v, dw_w, n_batch, seq, name):
    kk, c2 = dw_w.shape
    f_dim = c2 // 2
    wd = FFN_LANE_GROUPS * LANE
    nj = f_dim // wd
    ch = min(128, seq)

    def body(g_ref, v_ref, wg_ref, wv_ref, o_ref, padg, padv):
        for h in range(FFN_LANE_GROUPS):
            lanes = slice(h * LANE, (h + 1) * LANE)
            for pad, src in ((padg, g_ref), (padv, v_ref)):
                pad[h, 0:FFN_HALO, :] = jnp.zeros((FFN_HALO, LANE), F32)
                pad[h, FFN_HALO:FFN_HALO + seq, :] = src[:, lanes].astype(F32)
            for c0 in range(0, seq, ch):
                gate = jnp.zeros((ch, LANE), F32)
                val = jnp.zeros((ch, LANE), F32)
                for k in range(kk):
                    off = c0 + FFN_HALO - (kk - 1) + k
                    gate = gate + wg_ref[k:k + 1, lanes] * padg[h, pl.ds(off, ch), :]
                    val = val + wv_ref[k:k + 1, lanes] * padv[h, pl.ds(off, ch), :]
                o_ref[c0:c0 + ch, lanes] = (_gelu_parts(gate)[1] * val).astype(BF16)

    pad_shape = pltpu.VMEM((FFN_LANE_GROUPS, seq + FFN_HALO, LANE), F32)
    return pl.pallas_call(
        body, out_shape=jax.ShapeDtypeStruct((n_batch * seq, f_dim), BF16), grid=(n_batch, nj),
        in_specs=[_bs((seq, wd), lambda b, j: (b, j)), _bs((seq, wd), lambda b, j: (b, j)),
                  _bs((kk, wd), lambda b, j: (0, j)), _bs((kk, wd), lambda b, j: (0, nj + j))],
        out_specs=_bs((seq, wd), lambda b, j: (b, j)), scratch_shapes=[pad_shape, pad_shape], name=name,
        compiler_params=_params("parallel", "parallel"))(up_g, up_v, dw_w, dw_w)


def _ffn_act_bwd(up_g, up_v, dw_w, dact, n_batch, seq, name):
    kk, c2 = dw_w.shape
    f_dim = c2 // 2
    wd = FFN_LANE_GROUPS * LANE
    nj = f_dim // wd
    ch = min(128, seq)

    def body(g_ref, v_ref, wg_ref, wv_ref, da_ref, dg_ref, dv_ref, dwg_ref, dwv_ref, padg, padv, pbg, pbv):
        @pl.when(pl.program_id(1) == 0)
        def _():
            dwg_ref[...] = jnp.zeros((kk, wd), F32)
            dwv_ref[...] = jnp.zeros((kk, wd), F32)

        for h in range(FFN_LANE_GROUPS):
            lanes = slice(h * LANE, (h + 1) * LANE)
            for pad, src in ((padg, g_ref), (padv, v_ref)):
                pad[h, 0:FFN_HALO, :] = jnp.zeros((FFN_HALO, LANE), F32)
                pad[h, FFN_HALO:FFN_HALO + seq, :] = src[:, lanes].astype(F32)
            for pb in (pbg, pbv):
                pb[h, seq:seq + FFN_HALO, :] = jnp.zeros((FFN_HALO, LANE), F32)
            for c0 in range(0, seq, ch):
                gate = jnp.zeros((ch, LANE), F32)
                val = jnp.zeros((ch, LANE), F32)
                for k in range(kk):
                    off = c0 + FFN_HALO - (kk - 1) + k
                    gate = gate + wg_ref[k:k + 1, lanes] * padg[h, pl.ds(off, ch), :]
                    val = val + wv_ref[k:k + 1, lanes] * padv[h, pl.ds(off, ch), :]
                sq = gate * gate
                th = jnp.tanh(GELU_C0 * gate * (1.0 + GELU_C1 * sq))
                half = 0.5 * th + 0.5
                dgelu = half * (1.0 + gate * (GELU_C0 + 3.0 * GELU_C0 * GELU_C1 * sq) * (1.0 - th))
                dav = da_ref[c0:c0 + ch, lanes].astype(F32)
                pbg[h, c0:c0 + ch, :] = dav * val * dgelu
                pbv[h, c0:c0 + ch, :] = dav * (gate * half)
            for pb, pad, w_ref, d_ref, dw_ref in ((pbg, padg, wg_ref, dg_ref, dwg_ref), (pbv, padv, wv_ref, dv_ref, dwv_ref)):
                for c0 in range(0, seq, ch):
                    acc = jnp.zeros((ch, LANE), F32)
                    for k in range(kk):
                        acc = acc + w_ref[k:k + 1, lanes] * pb[h, pl.ds(c0 + (kk - 1) - k, ch), :]
                    d_ref[c0:c0 + ch, lanes] = acc.astype(BF16)
                for k in range(kk):
                    s = jnp.zeros((1, LANE), F32)
                    for c0 in range(0, seq, ch):
                        s = s + jnp.sum(pb[h, c0:c0 + ch, :] * pad[h, pl.ds(c0 + FFN_HALO - (kk - 1) + k, ch), :],
                                        axis=0, keepdims=True)
                    dw_ref[k:k + 1, lanes] += s

    t_dim = n_batch * seq
    tok = _bs((seq, wd), lambda j, b: (b, j))
    wblk = _bs((kk, wd), lambda j, b: (0, j))
    pad_shape = pltpu.VMEM((FFN_LANE_GROUPS, seq + FFN_HALO, LANE), F32)
    return pl.pallas_call(
        body, out_shape=(jax.ShapeDtypeStruct((t_dim, f_dim), BF16), jax.ShapeDtypeStruct((t_dim, f_dim), BF16),
                         jax.ShapeDtypeStruct((kk, f_dim), F32), jax.ShapeDtypeStruct((kk, f_dim), F32)),
        grid=(nj, n_batch),
        in_specs=[tok, tok, wblk, _bs((kk, wd), lambda j, b: (0, nj + j)), tok],
        out_specs=(tok, tok, wblk, wblk), scratch_shapes=[pad_shape, pad_shape, pad_shape, pad_shape], name=name,
        compiler_params=_params("parallel", "arbitrary"))(up_g, up_v, dw_w, dw_w, dact)


def _sum_rows(parts, out_dtype, name):
    r_dim, c_dim = parts[0].shape
    tr = _tile(r_dim, 1200, SUBLANE)
    n = len(parts)

    def body(*refs):
        acc = refs[0][...].astype(F32)
        for r in refs[1:n]:
            acc = acc + r[...].astype(F32)
        refs[n][...] = acc.astype(out_dtype)

    blk = _bs((tr, c_dim), lambda i: (i, 0))
    return pl.pallas_call(
        body, out_shape=jax.ShapeDtypeStruct((r_dim, c_dim), out_dtype), grid=(r_dim // tr,),
        in_specs=[blk] * n, out_specs=blk, name=name, compiler_params=_params("parallel"))(*parts)


def _adamw(w, g, m, v, name):
    shape = w.shape
    c_dim = shape[-1]
    r_dim = w.size // c_dim
    two_d = lambda t: t.reshape(r_dim, c_dim)
    tr = _tile(r_dim, max(SUBLANE, (256 * 1024) // max(c_dim, LANE) // SUBLANE * SUBLANE), SUBLANE)
    c1 = 1.0 - ADAM_B1 ** ADAM_STEP
    c2 = 1.0 - ADAM_B2 ** ADAM_STEP

    def body(w_ref, g_ref, m_ref, v_ref, d_ref, mo_ref, vo_ref):
        gv = g_ref[...]
        mn = ADAM_B1 * m_ref[...] + (1.0 - ADAM_B1) * gv
        vn = ADAM_B2 * v_ref[...] + (1.0 - ADAM_B2) * (gv * gv)
        mo_ref[...] = mn
        vo_ref[...] = vn
        d_ref[...] = -ADAM_LR * ((mn / c1) / (jnp.sqrt(vn / c2) + ADAM_EPS) + ADAM_WD * w_ref[...])

    blk = _bs((tr, c_dim), lambda i: (i, 0))
    out = jax.ShapeDtypeStruct((r_dim, c_dim), F32)
    d, mo, vo = pl.pallas_call(
        body, out_shape=(out, out, out), grid=(r_dim // tr,), in_specs=[blk] * 4, out_specs=(blk, blk, blk),
        name=name, compiler_params=_params("parallel"))(two_d(w), two_d(g), two_d(m), two_d(v))
    return d.reshape(shape), mo.reshape(shape), vo.reshape(shape)


HBM_SPEC = pl.BlockSpec(memory_space=pltpu.HBM)


def _position():
    return lax.axis_index("x"), lax.axis_index("y"), lax.axis_index("c")


def _all_gather(shard, name):
    def body(x_ref, out_ref, send_sems, recv_sems, local_sem):
        x, y, c = _position()
        me, sibling = (x, y, c), (x, y, 1 - c)
        chips = [(1 - x, y), (x, 1 - y), (1 - x, 1 - y)]

        def rows(px, py, pc):
            return out_ref.at[4 * px + 2 * py + pc]

        def copy(k, block, to, src=None):
            return pltpu.make_async_remote_copy(
                src_ref=rows(*block) if src is None else src, dst_ref=rows(*block),
                send_sem=send_sems.at[k], recv_sem=recv_sems.at[k], device_id=to, device_id_type=MESH)

        mine = pltpu.make_async_copy(x_ref, rows(*me), local_sem)
        mine.start()
        first = [copy(0, me, sibling, src=x_ref)]
        first += [copy(1 + j, me, (*chip, c), src=x_ref) for j, chip in enumerate(chips)]
        for cp in first:
            cp.start()
        passed = [copy(4 + j, (*chip, c), sibling) for j, chip in enumerate(chips)]
        for j, chip in enumerate(chips):
            copy(1 + j, (*chip, c), me).wait_recv()
            passed[j].start()
        copy(0, sibling, me).wait_recv()
        for j, chip in enumerate(chips):
            copy(4 + j, (*chip, 1 - c), me).wait_recv()
        for cp in first + passed:
            cp.wait_send()
        mine.wait()

    return pl.pallas_call(
        body, out_shape=jax.ShapeDtypeStruct((N_DEV,) + shard.shape, shard.dtype),
        in_specs=[HBM_SPEC], out_specs=HBM_SPEC,
        scratch_shapes=[pltpu.SemaphoreType.DMA((7,)), pltpu.SemaphoreType.DMA((7,)), pltpu.SemaphoreType.DMA(())],
        name=name)(shard)


CHIP_RELATIONS = ((0, 0), (1, 0), (0, 1), (1, 1))


def _rs_pair_exchange(g, name):
    _, r_dim, c_dim = g.shape
    n = len(CHIP_RELATIONS)

    def body(g_ref, recv_ref, send_sems, recv_sems):
        x, y, c = _position()
        sibling = (x, y, 1 - c)
        copies = []
        for k, (rx, ry) in enumerate(CHIP_RELATIONS):
            px = x + rx - 2 * x * rx
            py = y + ry - 2 * y * ry
            copies.append(pltpu.make_async_remote_copy(
                src_ref=g_ref.at[4 * px + 2 * py + 1 - c], dst_ref=recv_ref.at[k], send_sem=send_sems.at[k],
                recv_sem=recv_sems.at[k], device_id=sibling, device_id_type=MESH))
        for cp in copies:
            cp.start()
        for cp in copies:
            cp.wait()

    return pl.pallas_call(
        body, out_shape=jax.ShapeDtypeStruct((n, r_dim, c_dim), g.dtype), in_specs=[HBM_SPEC], out_specs=HBM_SPEC,
        scratch_shapes=[pltpu.SemaphoreType.DMA((n,)), pltpu.SemaphoreType.DMA((n,))], name=name)(g)


def _rs_pair_sum(g, recv, name):
    _, r_dim, c_dim = g.shape
    n = len(CHIP_RELATIONS)
    tr = _tile(r_dim, 1200, SUBLANE)
    x, y, c = _position()
    own = jnp.stack([4 * (x + rx - 2 * x * rx) + 2 * (y + ry - 2 * y * ry) + c for rx, ry in CHIP_RELATIONS])

    def body(own_ref, g_ref, r_ref, o_ref):
        o_ref[...] = (g_ref[...].astype(F32) + r_ref[...].astype(F32)).astype(o_ref.dtype)

    blk = _bs((None, tr, c_dim), lambda k, i, own_ref: (k, i, 0))
    return pl.pallas_call(
        body, out_shape=jax.ShapeDtypeStruct((n, r_dim, c_dim), g.dtype),
        grid_spec=pltpu.PrefetchScalarGridSpec(
            num_scalar_prefetch=1, grid=(n, r_dim // tr),
            in_specs=[_bs((None, tr, c_dim), lambda k, i, own_ref: (own_ref[k], i, 0)), blk], out_specs=blk),
        name=name, compiler_params=_params("parallel", "parallel"))(own.astype(jnp.int32), g, recv)


SEM_SPEC = pl.BlockSpec(memory_space=pltpu.SEMAPHORE)
DATAFLOW = pltpu.SideEffectType.DATAFLOW_SIDE_EFFECTING
CHIP_FLIPS = CHIP_RELATIONS[1:]
TOKEN = jax.ShapeDtypeStruct((SUBLANE, LANE), F32)


def _flip(v, r):
    return v + r - 2 * v * r


def _chip_copies(src_ref, src_of, dst_ref, dst_of, send_sems, recv_sems):
    x, y, c = _position()
    me = 4 * x + 2 * y + c
    out = []
    for k, (rx, ry) in enumerate(CHIP_FLIPS):
        px, py = _flip(x, rx), _flip(y, ry)
        peer = 4 * px + 2 * py + c
        out.append(pltpu.make_async_remote_copy(
            src_ref=src_ref.at[src_of(k, me, peer)], dst_ref=dst_ref.at[dst_of(k, me, peer)],
            send_sem=send_sems.at[k], recv_sem=recv_sems.at[k], device_id=(px, py, c), device_id_type=MESH))
    return out


def _device_block(ref, spec, d):
    rows, axis = spec
    return ref.at[pl.ds(d * rows, rows)] if axis == 0 else ref.at[:, pl.ds(d * rows, rows)]


def _ag_chips_start(lands, specs, after, name):
    n = len(lands)
    nf = len(CHIP_FLIPS)

    def body(*refs):
        send_sems, recv_sems, token = refs[n + 1], refs[n + 2], refs[-1]
        x, y, c = _position()
        me = 4 * x + 2 * y + c
        for i, spec in enumerate(specs):
            blk = _device_block(refs[i], spec, me)
            for k, (rx, ry) in enumerate(CHIP_FLIPS):
                pltpu.make_async_remote_copy(
                    src_ref=blk, dst_ref=blk, send_sem=send_sems.at[nf * i + k], recv_sem=recv_sems.at[nf * i + k],
                    device_id=(_flip(x, rx), _flip(y, ry), c), device_id_type=MESH).start()
        token[...] = jnp.zeros(TOKEN.shape, TOKEN.dtype)

    sems = pltpu.SemaphoreType.DMA((nf * n,))
    return pl.pallas_call(
        body, name=name, out_shape=(sems, sems, *[pltpu.HBM(t.shape, t.dtype) for t in lands], TOKEN),
        in_specs=(HBM_SPEC,) * n + (ANY_SPEC,),
        out_specs=(SEM_SPEC, SEM_SPEC) + (HBM_SPEC,) * n + (pl.BlockSpec(memory_space=pltpu.VMEM),),
        input_output_aliases={i: 2 + i for i in range(n)}, compiler_params=pltpu.CompilerParams(has_side_effects=DATAFLOW),
    )(*[pltpu.with_memory_space_constraint(t, pltpu.HBM) for t in lands], after)


def _ag_chips_wait(send_sems, recv_sems, lands, specs, after, name):
    n = len(lands)
    nf = len(CHIP_FLIPS)

    def body(*refs):
        send_sems, recv_sems = refs[n], refs[n + 1]
        x, y, c = _position()
        me = 4 * x + 2 * y + c
        for i, spec in enumerate(specs):
            for k, (rx, ry) in enumerate(CHIP_FLIPS):
                px, py = _flip(x, rx), _flip(y, ry)
                cp = pltpu.make_async_remote_copy(
                    src_ref=_device_block(refs[i], spec, me), dst_ref=_device_block(refs[i], spec, 4 * px + 2 * py + c),
                    send_sem=send_sems.at[nf * i + k], recv_sem=recv_sems.at[nf * i + k],
                    device_id=(px, py, c), device_id_type=MESH)
                cp.wait_send()
                cp.wait_recv()

    return pl.pallas_call(
        body, name=name, out_shape=tuple(pltpu.HBM(t.shape, t.dtype) for t in lands),
        in_specs=(HBM_SPEC,) * n + (SEM_SPEC, SEM_SPEC, ANY_SPEC), out_specs=(HBM_SPEC,) * n,
        input_output_aliases={i: i for i in range(n)}, compiler_params=pltpu.CompilerParams(has_side_effects=DATAFLOW),
    )(*lands, send_sems, recv_sems, after)


def _ag_pair_forward(lands, specs, name):
    n = len(lands)
    nr = len(CHIP_RELATIONS)

    def body(*refs):
        outs, send_sems, recv_sems = refs[n:2 * n], refs[2 * n], refs[2 * n + 1]
        x, y, c = _position()
        copies = []
        for i, spec in enumerate(specs):
            for k, (rx, ry) in enumerate(CHIP_RELATIONS):
                chip = 4 * _flip(x, rx) + 2 * _flip(y, ry)
                held = _device_block(outs[i], spec, chip + c)
                sems = dict(send_sem=send_sems.at[nr * i + k], recv_sem=recv_sems.at[nr * i + k],
                            device_id=(x, y, 1 - c), device_id_type=MESH)
                mine = pltpu.make_async_remote_copy(src_ref=held, dst_ref=held, **sems)
                theirs = pltpu.make_async_remote_copy(src_ref=held, dst_ref=_device_block(outs[i], spec, chip + 1 - c), **sems)
                copies.append((mine, theirs))
        for mine, _ in copies:
            mine.start()
        for mine, theirs in copies:
            mine.wait_send()
            theirs.wait_recv()

    sems = pltpu.SemaphoreType.DMA((nr * n,))
    return pl.pallas_call(
        body, out_shape=tuple(jax.ShapeDtypeStruct(t.shape, t.dtype) for t in lands), in_specs=[HBM_SPEC] * n,
        out_specs=(HBM_SPEC,) * n, input_output_aliases={i: i for i in range(n)}, scratch_shapes=[sems, sems], name=name)(*lands)


def _rs_chips_start(pair, name):
    _, r_dim, c_dim = pair.shape
    n = len(CHIP_FLIPS)

    def body(pair_ref, far_ref, send_sems, recv_sems, pair_thru, far_thru, token):
        for cp in _chip_copies(pair_ref, lambda k, me, peer: k + 1, far_ref, lambda k, me, peer: k, send_sems, recv_sems):
            cp.start()
        token[...] = jnp.zeros(TOKEN.shape, TOKEN.dtype)

    far = lax.empty((n, r_dim, c_dim), pair.dtype)
    return pl.pallas_call(
        body, name=name,
        out_shape=(pltpu.SemaphoreType.DMA((n,)), pltpu.SemaphoreType.DMA((n,)), pltpu.HBM(pair.shape, pair.dtype),
                   pltpu.HBM(far.shape, far.dtype), TOKEN),
        in_specs=(HBM_SPEC, HBM_SPEC),
        out_specs=(SEM_SPEC, SEM_SPEC, HBM_SPEC, HBM_SPEC, pl.BlockSpec(memory_space=pltpu.VMEM)),
        input_output_aliases={0: 2, 1: 3}, compiler_params=pltpu.CompilerParams(has_side_effects=DATAFLOW),
    )(pltpu.with_memory_space_constraint(pair, pltpu.HBM), pltpu.with_memory_space_constraint(far, pltpu.HBM))


def _rs_chips_wait(send_sems, recv_sems, pair, far, after, name):
    def body(pair_ref, far_ref, send_sems, recv_sems, after_ref, pair_out, far_out):
        for cp in _chip_copies(pair_ref, lambda k, me, peer: k + 1, far_ref, lambda k, me, peer: k, send_sems, recv_sems):
            cp.wait_send()
            cp.wait_recv()

    return pl.pallas_call(
        body, name=name, out_shape=(pltpu.HBM(pair.shape, pair.dtype), pltpu.HBM(far.shape, far.dtype)),
        in_specs=(HBM_SPEC, HBM_SPEC, SEM_SPEC, SEM_SPEC, ANY_SPEC),
        out_specs=(HBM_SPEC, HBM_SPEC), input_output_aliases={0: 0, 1: 1},
        compiler_params=pltpu.CompilerParams(has_side_effects=DATAFLOW),
    )(pair, far, send_sems, recv_sems, after)


def _rs_final_sum(pair, far, name):
    _, r_dim, c_dim = pair.shape
    tr = _tile(r_dim, 1200, SUBLANE)

    def body(p_ref, f0_ref, f1_ref, f2_ref, o_ref):
        o_ref[...] = ((p_ref[...].astype(F32) + f0_ref[...].astype(F32)) + f1_ref[...].astype(F32)) + f2_ref[...].astype(F32)

    def slot(k):
        return _bs((None, tr, c_dim), lambda i: (k, i, 0))

    return pl.pallas_call(
        body, out_shape=jax.ShapeDtypeStruct((r_dim, c_dim), F32), grid=(r_dim // tr,),
        in_specs=[slot(0), slot(0), slot(1), slot(2)], out_specs=_bs((tr, c_dim), lambda i: (i, 0)), name=name,
        compiler_params=_params("parallel"))(pair, far, far, far)


def _reduce_scatter_begin(g, name):
    recv = _rs_pair_exchange(g, name + "_pair")
    pair = _rs_pair_sum(g, recv, name + "_pairsum")
    return _rs_chips_start(pair, name + "_chips_start")


def _reduce_scatter_end(state, after, name):
    send_sems, recv_sems, pair, far, _ = state
    pair, far = _rs_chips_wait(send_sems, recv_sems, pair, far, after, name + "_chips_wait")
    return _rs_final_sum(pair, far, name + "_sum")


MATRICES = (("w_in", True), ("w_out", False), ("w_q", False), ("w_kv", True), ("w_o", False), ("w_up", True),
            ("w_down", False), ("w_conv_out", True), ("w_pool_grp", True))
MIX_NAMES = ("w_in", "w_conv_out", "w_pool_grp", "w_out")
REST_NAMES = ("w_q", "w_kv", "w_o", "w_up", "w_down")


def _parts(layer):
    return (("mix", MIX_NAMES), ("rest", REST_NAMES)) if layer == 0 else (("all", MIX_NAMES + REST_NAMES),)


def _to_rows(name, transposed, w, d_model):
    if name == "w_pool_grp":
        w = jnp.swapaxes(w, 1, 2)
    elif transposed:
        w = w.T
    return w.reshape(-1, d_model)


def _stored_form(name, transposed, rows, shard_shape):
    if name == "w_pool_grp":
        g, i, o = shard_shape
        return rows.reshape(g, o, i)
    if transposed:
        return rows.reshape(shard_shape[1], shard_shape[0])
    return rows.reshape(shard_shape)


def _scatter_blocks(name, full, shard_shape, d_model, n_dev=N_DEV):
    if name == "w_pool_grp":
        g, i, o = shard_shape
        return jnp.swapaxes(full.reshape(g, n_dev, o, i), 0, 1).reshape(n_dev, -1, d_model)
    return full.reshape(n_dev, -1, d_model)


def kernel(x, mem, mix_norm_g, w_in, conv_dw_w, conv_dw_b, conv_ln_g, conv_ln_b, w_conv_out, w_pool_grp, pool_scale, w_out, xattn_norm_g, mem_norm_g, w_q, w_kv, w_o, ffn_norm_g, w_up, ffn_dw_w, w_down, final_norm_g, loss_target, m_mix_norm_g, m_w_in, m_conv_dw_w, m_conv_dw_b, m_conv_ln_g, m_conv_ln_b, m_w_conv_out, m_w_pool_grp, m_pool_scale, m_w_out, m_xattn_norm_g, m_mem_norm_g, m_w_q, m_w_kv, m_w_o, m_ffn_norm_g, m_w_up, m_ffn_dw_w, m_w_down, m_final_norm_g, v_mix_norm_g, v_w_in, v_conv_dw_w, v_conv_dw_b, v_conv_ln_g, v_conv_ln_b, v_w_conv_out, v_w_pool_grp, v_pool_scale, v_w_out, v_xattn_norm_g, v_mem_norm_g, v_w_q, v_w_kv, v_w_o, v_ffn_norm_g, v_w_up, v_ffn_dw_w, v_w_down, v_final_norm_g):
    p = dict(locals())
    weight_names = ["mix_norm_g", "w_in", "conv_dw_w", "conv_dw_b", "conv_ln_g", "conv_ln_b", "w_conv_out",
                    "w_pool_grp", "pool_scale", "w_out", "xattn_norm_g", "mem_norm_g", "w_q", "w_kv", "w_o",
                    "ffn_norm_g", "w_up", "ffn_dw_w", "w_down", "final_norm_g"]
    n_batch, seq, d_model = x.shape
    m_len = mem.shape[1]
    depth = w_in.shape[0]
    assert depth == 2, "the exchange schedule below is written for two layers"
    t_dim = n_batch * seq
    c_conv = conv_dw_b.shape[1]
    n_groups = w_pool_grp.shape[1]
    assert w_pool_grp.shape[2] == LANE and c_conv % LANE == 0 and n_groups * LANE == c_conv
    gate_col0 = 2 * c_conv + n_groups * LANE
    pool_col0 = (2 * c_conv) // LANE

    dev = 4 * lax.axis_index("x") + 2 * lax.axis_index("y") + lax.axis_index("c")
    filt = jnp.concatenate([conv_dw_w.reshape(-1), ffn_dw_w.reshape(-1)])
    filt_rows = lax.bitcast_convert_type(filt, BF16).reshape(-1, d_model)
    transposed = dict(MATRICES)
    layout = {part: [(name, transposed[name], _to_rows(name, transposed[name], p[name][0], d_model).shape[0])
                     for name in names] for l in range(depth) for part, names in _parts(l)}
    part_of = {(l, name): part for l in range(depth) for part, names in _parts(l) for name in names}

    def landing(name, shard):
        if name == "w_pool_grp":
            block, axis = jnp.swapaxes(shard, 1, 2), 1
        elif name == "filt":
            block, axis = shard, 0
        else:
            block, axis = (shard.T if transposed[name] else shard), 0
        block = block.astype(BF16)
        rows = block.shape[axis]
        shape = block.shape[:axis] + (N_DEV * rows,) + block.shape[axis + 1:]
        start = (0,) * axis + (dev * rows,) + (0,) * (block.ndim - axis - 1)
        return lax.dynamic_update_slice(lax.empty(shape, BF16), block, start), (rows, axis)

    ag_state = {}
    after = filt_rows
    for l in range(depth):
        for part, names in _parts(l):
            items = [(name, p[name][l]) for name in names]
            if (l, part) == (0, part_of[(0, "w_in")]):
                items.append(("filt", filt_rows))
            lands, specs = zip(*[landing(name, shard) for name, shard in items])
            out = _ag_chips_start(lands, specs, after, f"ag{l}{part}_chips_start")
            ag_state[(l, part)] = ([name for name, _ in items], specs, out)
            after = out[-1]
    all_started = after

    full = [dict() for _ in range(depth)]

    def ensure(l, name, after):
        if name in full[l]:
            return
        part = part_of[(l, name)]
        names, specs, out = ag_state[(l, part)]
        lands = _ag_chips_wait(out[0], out[1], out[2:-1], specs, after, f"ag{l}{part}_chips_wait")
        lands = _ag_pair_forward(lands, specs, f"ag{l}{part}_pair_forward")
        full[l].update(zip(names, lands))

    vec = lambda a: a.reshape(1, -1)
    x2d = x.reshape(t_dim, d_model)
    mem2d = mem.reshape(n_batch * m_len, d_model)
    mem_n = _rmsnorm_fwd(mem2d, vec(mem_norm_g), "mem_norm", after=all_started)
    h_first = _rmsnorm_fwd(x2d, vec(mix_norm_g[0]), "mix_norm_l0", after=mem_n)
    ensure(0, "w_in", h_first)
    filt_all = lax.bitcast_convert_type(full[0]["filt"].reshape(N_DEV, -1, 2), F32)
    n_cw = conv_dw_w.size
    kc, cs = conv_dw_w.shape[1:]
    kf, fs = ffn_dw_w.shape[1:]
    conv_w_full = jnp.moveaxis(filt_all[:, :n_cw].reshape(N_DEV, depth, kc, cs), 0, 2).reshape(depth, kc, N_DEV * cs)
    ffn_w_full = jnp.moveaxis(filt_all[:, n_cw:].reshape(N_DEV, depth, kf, fs), 0, 2).reshape(depth, kf, N_DEV * fs)

    saved = []
    xc = x2d
    for l in range(depth):
        ensure(l, "w_in", xc)
        wl = full[l]
        s = {"x0": xc}
        s["h"] = h_first if l == 0 else _rmsnorm_fwd(xc, vec(mix_norm_g[l]), f"mix_norm_l{l}")
        s["proj"] = _matmul(s["h"], wl["w_in"], "nt", f"in_proj_l{l}", out_dtype=BF16)
        s["y1"] = _glu_conv_fwd(s["proj"], conv_w_full[l], vec(conv_dw_b[l]), n_batch, seq, f"glu_conv_l{l}")
        s["y3"] = _ln_silu_fwd(s["y1"], vec(conv_ln_g[l]), vec(conv_ln_b[l]), f"ln_silu_l{l}")
        s["yc"] = _matmul(s["y3"], wl["w_conv_out"], "nt", f"conv_out_l{l}", out_dtype=BF16)
        s["zp"] = _pool_fwd(s["proj"], pool_col0, n_groups, n_batch, seq, f"pool_l{l}")
        s["yp"] = _grouped(s["zp"], wl["w_pool_grp"], "nt", f"pool_proj_l{l}", out_dtype=BF16)
        s["merged"] = _merge_fwd(s["proj"], gate_col0, s["yc"], s["yp"], vec(pool_scale[l]), f"merge_l{l}")
        s["x1"] = _matmul(s["merged"], wl["w_out"], "nn", f"mix_out_l{l}", res=xc)
        ensure(l, "w_q", s["x1"])
        half_up = wl["w_up"].shape[0] // 2
        up_gate, up_val = (0, half_up), (half_up, half_up)
        s["hq"] = _rmsnorm_fwd(s["x1"], vec(xattn_norm_g[l]), f"xattn_norm_l{l}")
        s["q"] = _matmul(s["hq"], wl["w_q"], "nn", f"q_proj_l{l}", out_dtype=BF16)
        s["kv"] = _matmul(mem_n, wl["w_kv"], "nt", f"kv_proj_l{l}", out_dtype=BF16)
        s["att"] = _attn_fwd(s["q"], s["kv"], n_batch, seq, m_len, f"attn_l{l}")
        s["x2"] = _matmul(s["att"], wl["w_o"], "nn", f"attn_out_l{l}", res=s["x1"])
        s["hf"] = _rmsnorm_fwd(s["x2"], vec(ffn_norm_g[l]), f"ffn_norm_l{l}")
        s["up_g"] = _matmul(s["hf"], wl["w_up"], "nt", f"up_proj_gate_l{l}", out_dtype=BF16, b_window=up_gate)
        s["up_v"] = _matmul(s["hf"], wl["w_up"], "nt", f"up_proj_val_l{l}", out_dtype=BF16, b_window=up_val)
        s["act"] = _ffn_act_fwd(s["up_g"], s["up_v"], ffn_w_full[l], n_batch, seq, f"ffn_act_l{l}")
        xc = _matmul(s["act"], wl["w_down"], "nn", f"down_proj_l{l}", res=s["x2"])
        saved.append(s)

    dx, dxb, dg_final, loss_part = _loss_head(xc, vec(final_norm_g), loss_target.reshape(t_dim, d_model), "loss_head")

    small = {"final_norm_g": dg_final.reshape(-1)}
    big = [dict() for _ in range(depth)]
    rs_state = {}
    rs_after = loss_part

    def rs_begin(l, part):
        pack = lax.empty((N_DEV, sum(nrows for _, _, nrows in layout[part]), d_model), BF16)
        row0 = 0
        for name, _, nrows in layout[part]:
            pieces = big[l][name] if isinstance(big[l][name], tuple) else (big[l][name],)
            d0 = 0
            for piece in pieces:
                blocks = _scatter_blocks(name, piece, p[name].shape[1:], d_model, N_DEV // len(pieces)).astype(BF16)
                pack = lax.dynamic_update_slice(pack, blocks, (d0, row0, 0))
                d0 += blocks.shape[0]
            row0 += nrows
        rs_state[(l, part)] = _reduce_scatter_begin(pack, f"rs{l}{part}")
        return rs_state[(l, part)][4]

    dmem_n = None
    for l in reversed(range(depth)):
        wl, s = full[l], saved[l]
        sm = {}
        dact = _matmul(dxb, wl["w_down"], "nt", f"d_act_l{l}", out_dtype=BF16, after=rs_after)
        big[l]["w_down"] = _matmul(s["act"], dxb, "tn", f"d_w_down_l{l}", out_dtype=BF16)
        dup_g, dup_v, dwf_g, dwf_v = _ffn_act_bwd(s["up_g"], s["up_v"], ffn_w_full[l], dact, n_batch, seq,
                                                  f"ffn_act_bwd_l{l}")
        sm["ffn_dw_w"] = jnp.concatenate([dwf_g, dwf_v], axis=1)
        dx, dxb, dg = _matmul_rmsnorm_bwd((dup_g, dup_v), wl["w_up"], "nn", s["x2"], vec(ffn_norm_g[l]), dx,
                                          f"d_hf_ffn_norm_bwd_l{l}")
        big[l]["w_up"] = (_matmul(dup_g, s["hf"], "tn", f"d_w_up_gate_l{l}", out_dtype=BF16),
                          _matmul(dup_v, s["hf"], "tn", f"d_w_up_val_l{l}", out_dtype=BF16))
        sm["ffn_norm_g"] = dg
        datt = _matmul(dxb, wl["w_o"], "nt", f"d_att_l{l}", out_dtype=BF16, after=rs_after)
        big[l]["w_o"] = _matmul(s["att"], dxb, "tn", f"d_w_o_l{l}", out_dtype=BF16)
        dq, dk, dv = _attn_bwd(s["q"], s["kv"], datt, n_batch, seq, m_len, f"attn_bwd_l{l}")
        dkv = jnp.concatenate([dk, dv], axis=1)
        big[l]["w_kv"] = _matmul(dkv, mem_n, "tn", f"d_w_kv_l{l}", out_dtype=BF16)
        dmem_n = _matmul(dkv, wl["w_kv"], "nn", f"d_mem_l{l}", res=dmem_n)
        big[l]["w_q"] = _matmul(s["hq"], dq, "tn", f"d_w_q_l{l}", out_dtype=BF16)
        dx, dxb, dg = _matmul_rmsnorm_bwd(dq, wl["w_q"], "nt", s["x1"], vec(xattn_norm_g[l]), dx,
                                          f"d_hq_xattn_norm_bwd_l{l}")
        sm["xattn_norm_g"] = dg
        if part_of[(l, "w_q")] != part_of[(l, "w_in")]:
            rs_after = rs_begin(l, part_of[(l, "w_q")])
        dmerged = _matmul(dxb, wl["w_out"], "nt", f"d_merged_l{l}", out_dtype=BF16, after=rs_after)
        big[l]["w_out"] = _matmul(s["merged"], dxb, "tn", f"d_w_out_l{l}", out_dtype=BF16)
        dgc, dgp, dyc, dyp, dscale = _merge_bwd(s["proj"], gate_col0, s["yc"], s["yp"], vec(pool_scale[l]), dmerged,
                                                f"merge_bwd_l{l}")
        sm["pool_scale"] = dscale
        dzp = _grouped(dyp, wl["w_pool_grp"], "nn", f"d_zp_l{l}", out_dtype=BF16)
        big[l]["w_pool_grp"] = _grouped_tn(dyp, s["zp"], n_groups, f"d_w_pool_l{l}")
        du = _pool_bwd(dzp, n_groups, n_batch, seq, f"pool_bwd_l{l}")
        dy3 = _matmul(dyc, wl["w_conv_out"], "nn", f"d_y3_l{l}", out_dtype=BF16)
        big[l]["w_conv_out"] = _matmul(dyc, s["y3"], "tn", f"d_w_conv_out_l{l}", out_dtype=BF16)
        dy1, dlg, dlb = _ln_silu_bwd(s["y1"], vec(conv_ln_g[l]), vec(conv_ln_b[l]), dy3, f"ln_silu_bwd_l{l}")
        sm["conv_ln_g"], sm["conv_ln_b"] = dlg, dlb
        da, dgl, dcw, dcb = _glu_conv_bwd(s["proj"], conv_w_full[l], dy1, n_batch, seq, f"glu_conv_bwd_l{l}")
        sm["conv_dw_w"], sm["conv_dw_b"] = dcw, dcb
        dproj = jnp.concatenate([da, dgl, du, dgc, dgp], axis=1)
        big[l]["w_in"] = _matmul(dproj, s["h"], "tn", f"d_w_in_l{l}", out_dtype=BF16)
        dx, dxb, dg = _matmul_rmsnorm_bwd(dproj, wl["w_in"], "nn", s["x0"], vec(mix_norm_g[l]), dx,
                                          f"d_h_mix_norm_bwd_l{l}")
        sm["mix_norm_g"] = dg
        for k, val in sm.items():
            small[(l, k)] = val.reshape(-1)
        rs_after = rs_begin(l, part_of[(l, "w_in")])
    _, _, dg_mem = _rmsnorm_bwd(mem2d, vec(mem_norm_g), dmem_n, None, "mem_norm_bwd")
    small["mem_norm_g"] = dg_mem.reshape(-1)
    small["loss"] = loss_part.reshape(-1)

    grads = {}
    per_layer = {name: [None] * depth for name, _ in MATRICES}
    for l in reversed(range(depth)):
        for part, _ in reversed(_parts(l)):
            mat_grads = _reduce_scatter_end(rs_state[(l, part)], rs_after, f"rs{l}{part}")
            row0 = 0
            for name, tr, nrows in layout[part]:
                per_layer[name][l] = _stored_form(name, tr, mat_grads[row0:row0 + nrows], p[name].shape[1:])
                row0 += nrows
    flip = lambda t: jnp.swapaxes(t, -1, -2)
    stored_grads = {name: jnp.stack(per_layer[name]) for name, _ in MATRICES}
    for name, tr in MATRICES:
        grads[name] = flip(stored_grads[name]) if tr else stored_grads[name]

    keys = list(small.keys())
    flat = jnp.concatenate([small[k] for k in keys])
    n_small = flat.shape[0]
    rows_small = -(-n_small // (SUBLANE * d_model)) * SUBLANE
    flat = jnp.pad(flat, (0, rows_small * d_model - n_small)).reshape(rows_small, d_model)
    every = _all_gather(flat, "small_all_gather")
    total = _sum_rows([every[i] for i in range(N_DEV)], F32, "small_sum").reshape(-1)
    off = 0
    red = {}
    for k in keys:
        red[k] = total[off:off + small[k].shape[0]]
        off += small[k].shape[0]
    loss = red["loss"][0]
    for name in ("mix_norm_g", "conv_dw_b", "conv_ln_g", "conv_ln_b", "pool_scale", "xattn_norm_g", "ffn_norm_g"):
        grads[name] = jnp.stack([red[(l, name)] for l in range(depth)])
    grads["conv_dw_w"] = jnp.stack([
        lax.dynamic_slice_in_dim(red[(l, "conv_dw_w")].reshape(kc, N_DEV * cs), dev * cs, cs, axis=1)
        for l in range(depth)])
    grads["ffn_dw_w"] = jnp.stack([
        lax.dynamic_slice_in_dim(red[(l, "ffn_dw_w")].reshape(kf, N_DEV * fs), dev * fs, fs, axis=1)
        for l in range(depth)])
    grads["mem_norm_g"] = red["mem_norm_g"]
    grads["final_norm_g"] = red["final_norm_g"]

    deltas, new_m, new_v = {}, {}, {}
    for name in weight_names:
        if transposed.get(name, False):
            out = _adamw(flip(p[name]), stored_grads[name], flip(p["m_" + name]), flip(p["v_" + name]), f"adamw_{name}")
            deltas[name], new_m[name], new_v[name] = (flip(t) for t in out)
        else:
            deltas[name], new_m[name], new_v[name] = _adamw(p[name], grads[name], p["m_" + name], p["v_" + name],
                                                            f"adamw_{name}")
    grad_x = dx.reshape(n_batch, seq, d_model)
    return (loss, grad_x, *[grads[n] for n in weight_names], *[deltas[n] for n in weight_names],
            *[new_m[n] for n in weight_names], *[new_v[n] for n in weight_names])
```

```python
import functools

import jax
import jax.numpy as jnp
from jax import lax
from jax.experimental import pallas as pl
from jax.experimental.pallas import tpu as pltpu

F32 = jnp.float32
BF16 = jnp.bfloat16
MESH = pl.DeviceIdType.MESH

N_DEV = 8
EPS = 1e-6
V7X_VMEM_BYTES = 64 * 1024 * 1024
VMEM_LIMIT = (V7X_VMEM_BYTES * 3) // 4
LANE = 128
SUBLANE = 8

CONV_HALO = 32
POOL_HALO = 16
FFN_HALO = 8
FFN_LANE_GROUPS = 2
POOL_WINDOW_MAX = 16
XA_HEADS = 4

ADAM_LR = 0.001
ADAM_B1 = 0.9
ADAM_B2 = 0.999
ADAM_EPS = 1e-08
ADAM_WD = 0.01
ADAM_STEP = 10

GELU_C0 = 0.7978845608028654
GELU_C1 = 0.044715


ANY_SPEC = pl.BlockSpec(memory_space=pl.ANY)


def _tile(n, cap, mult=LANE):
    if n <= cap:
        return n
    best = None
    for d in range(mult, cap + 1, mult):
        if n % d == 0:
            best = d
    assert best is not None, (n, cap, mult)
    return best


def _params(*sem):
    return pltpu.CompilerParams(dimension_semantics=sem, vmem_limit_bytes=VMEM_LIMIT)


def _delayed(x, halo, rows, n_shifts):
    for r in range(min(SUBLANE, n_shifts)):
        xr = x if r == 0 else pltpu.roll(x, r, 0)
        for s in range(r, n_shifts, SUBLANE):
            yield s, xr[halo - (s - r):halo - (s - r) + rows]


def _advanced(x, rows, n_shifts):
    for r in range(min(SUBLANE, n_shifts)):
        xr = x if r == 0 else pltpu.roll(x, x.shape[0] - r, 0)
        for s in range(r, n_shifts, SUBLANE):
            yield s, xr[s - r:s - r + rows]


def _sig(x):
    return 1.0 / (1.0 + jnp.exp(-x))


def _bs(shape, imap):
    return pl.BlockSpec(shape, imap)


def _mxu_tile(n, cap):
    if n <= cap:
        return n
    best = {mult: max((d for d in range(mult, cap + 1, mult) if n % d == 0), default=0) for mult in (2 * LANE, LANE)}
    assert best[LANE] > 0, (n, cap)
    return best[2 * LANE] if 2 * best[2 * LANE] >= best[LANE] else best[LANE]


def _matmul(a, b, mode, name, res=None, out_dtype=F32, after=None, b_window=None):
    b_row0, b_rows = b_window if b_window is not None else (0, b.shape[0])
    if mode == "tn":
        k_dim, m_dim = a.shape
        k2, n_dim = b_rows, b.shape[1]
    elif mode == "nn":
        m_dim, k_dim = a.shape
        k2, n_dim = b_rows, b.shape[1]
    else:
        m_dim, k_dim = a.shape
        n_dim, k2 = b_rows, b.shape[1]
    assert k_dim == k2, (name, a.shape, b.shape)
    size = lambda t: jnp.dtype(t).itemsize
    tm = _mxu_tile(m_dim, 2816 if mode == "tn" else 1024)
    tn = _mxu_tile(n_dim, 2816)
    fixed = tm * tn * (2 * size(out_dtype) + (2 * size(res.dtype) if res is not None else 0) + 4)
    for cap in (2816, 2048, 1792, 1024, 512):
        tk = _mxu_tile(k_dim, cap)
        if fixed + 2 * tk * (tm * size(a.dtype) + tn * size(b.dtype)) <= VMEM_LIMIT - 8 * 1024 * 1024:
            break
    nk = k_dim // tk
    use_acc = nk > 1 and out_dtype != F32
    if mode == "tn":
        a_spec, ca = _bs((tk, tm), lambda i, j, k: (k, i)), 0
    else:
        a_spec, ca = _bs((tm, tk), lambda i, j, k: (i, k)), 1
    if mode == "nt":
        assert b_row0 % tn == 0
        b_spec, cb = _bs((tn, tk), lambda i, j, k: (j + b_row0 // tn, k)), 1
    else:
        assert b_row0 % tk == 0
        b_spec, cb = _bs((tk, tn), lambda i, j, k: (k + b_row0 // tk, j)), 0
    dims = (((ca,), (cb,)), ((), ()))
    o_spec = _bs((tm, tn), lambda i, j, k: (i, j))
    has_res = res is not None

    def body(*refs):
        a_ref, b_ref = refs[:2]
        r_ref = refs[2] if has_res else None
        o_ref = refs[n_in]
        k = pl.program_id(2)
        part = lax.dot_general(a_ref[...].astype(BF16), b_ref[...].astype(BF16), dims,
                               preferred_element_type=F32)
        if nk == 1:
            if has_res:
                part = part + r_ref[...].astype(F32)
            o_ref[...] = part.astype(out_dtype)
            return
        acc = refs[-1] if use_acc else o_ref

        @pl.when(k == 0)
        def _():
            acc[...] = part + r_ref[...].astype(F32) if has_res else part

        @pl.when(k > 0)
        def _():
            acc[...] += part

        if use_acc:
            @pl.when(k == nk - 1)
            def _():
                o_ref[...] = acc[...].astype(out_dtype)

    in_specs = [a_spec, b_spec] + ([o_spec] if has_res else [])
    args = (a, b) + ((res,) if has_res else ())
    if after is not None:
        in_specs.append(ANY_SPEC)
        args += (after,)
    n_in = len(args)
    return pl.pallas_call(
        body, out_shape=jax.ShapeDtypeStruct((m_dim, n_dim), out_dtype),
        grid=(m_dim // tm, n_dim // tn, nk), in_specs=in_specs, out_specs=o_spec,
        scratch_shapes=[pltpu.VMEM((tm, tn), F32)] if use_acc else [], name=name,
        compiler_params=_params("parallel", "parallel", "arbitrary"))(*args)


def _grouped(a, w, mode, name, out_dtype=F32):
    t_dim = a.shape[0]
    g_dim, r_dim, c_dim = w.shape
    ka, no = (c_dim, r_dim) if mode == "nt" else (r_dim, c_dim)
    tm = _tile(t_dim, 2048)
    dims = (((1,), (1 if mode == "nt" else 0,)), ((), ()))

    def body(a_ref, w_ref, o_ref):
        o_ref[...] = lax.dot_general(a_ref[...].astype(BF16), w_ref[...].astype(BF16), dims,
                                     preferred_element_type=F32).astype(out_dtype)

    return pl.pallas_call(
        body, out_shape=jax.ShapeDtypeStruct((t_dim, g_dim * no), out_dtype),
        grid=(t_dim // tm, g_dim),
        in_specs=[_bs((tm, ka), lambda i, g: (i, g)), _bs((None, r_dim, c_dim), lambda i, g: (g, 0, 0))],
        out_specs=_bs((tm, no), lambda i, g: (i, g)), name=name,
        compiler_params=_params("parallel", "parallel"))(a, w)


def _grouped_tn(a, b, g_dim, name):
    t_dim = a.shape[0]
    ra = a.shape[1] // g_dim
    cb = b.shape[1] // g_dim
    tm = _tile(t_dim, 2048)
    nt = t_dim // tm

    def body(a_ref, b_ref, o_ref):
        part = lax.dot_general(a_ref[...].astype(BF16), b_ref[...].astype(BF16), (((0,), (0,)), ((), ())),
                               preferred_element_type=F32)

        @pl.when(pl.program_id(1) == 0)
        def _():
            o_ref[...] = part

        @pl.when(pl.program_id(1) > 0)
        def _():
            o_ref[...] += part

    return pl.pallas_call(
        body, out_shape=jax.ShapeDtypeStruct((g_dim, ra, cb), F32), grid=(g_dim, nt),
        in_specs=[_bs((tm, ra), lambda g, i: (i, g)), _bs((tm, cb), lambda g, i: (i, g))],
        out_specs=_bs((None, ra, cb), lambda g, i: (g, 0, 0)), name=name,
        compiler_params=_params("parallel", "arbitrary"))(a, b)


def _rmsnorm_fwd(x, g, name, after=None):
    t_dim, d = x.shape
    tm = _tile(t_dim, 512)

    def body(x_ref, g_ref, *rest):
        o_ref = rest[-1]
        xv = x_ref[...]
        r = lax.rsqrt(jnp.mean(xv * xv, axis=-1, keepdims=True) + EPS)
        o_ref[...] = (xv * r * g_ref[...]).astype(BF16)

    return pl.pallas_call(
        body, out_shape=jax.ShapeDtypeStruct((t_dim, d), BF16), grid=(t_dim // tm,),
        in_specs=[_bs((tm, d), lambda i: (i, 0)), _bs((1, d), lambda i: (0, 0))] + ([ANY_SPEC] if after is not None else []),
        out_specs=_bs((tm, d), lambda i: (i, 0)), name=name,
        compiler_params=_params("parallel"))(x, g, *([after] if after is not None else []))


def _rmsnorm_bwd(x, g, dh, dx_in, name):
    t_dim, d = x.shape
    tm = _tile(t_dim, 512)
    has_in = dx_in is not None

    def body(*refs):
        if has_in:
            x_ref, g_ref, dh_ref, di_ref, dx_ref, dxb_ref, dg_ref = refs
        else:
            x_ref, g_ref, dh_ref, dx_ref, dxb_ref, dg_ref = refs
        xv = x_ref[...]
        r = lax.rsqrt(jnp.mean(xv * xv, axis=-1, keepdims=True) + EPS)
        xh = xv * r
        dhv = dh_ref[...].astype(F32)
        dxh = dhv * g_ref[...]
        dx = r * (dxh - xh * jnp.mean(dxh * xh, axis=-1, keepdims=True))
        if has_in:
            dx = dx + di_ref[...]
        dx_ref[...] = dx
        dxb_ref[...] = dx.astype(BF16)
        part = jnp.sum(dhv * xh, axis=0, keepdims=True)

        @pl.when(pl.program_id(0) == 0)
        def _():
            dg_ref[...] = part

        @pl.when(pl.program_id(0) > 0)
        def _():
            dg_ref[...] += part

    row = _bs((tm, d), lambda i: (i, 0))
    vec = _bs((1, d), lambda i: (0, 0))
    args = (x, g, dh) + ((dx_in,) if has_in else ())
    return pl.pallas_call(
        body, out_shape=(jax.ShapeDtypeStruct((t_dim, d), F32), jax.ShapeDtypeStruct((t_dim, d), BF16),
                         jax.ShapeDtypeStruct((1, d), F32)),
        grid=(t_dim // tm,), in_specs=[row, vec, row] + ([row] if has_in else []),
        out_specs=(row, row, vec), name=name, compiler_params=_params("arbitrary"))(*args)


def _matmul_rmsnorm_bwd(a, b, mode, x, g, dx_in, name, res=None, b_window=None):
    pieces = a if isinstance(a, tuple) else (a,)
    n_p = len(pieces)
    b_row0, b_rows = b_window if b_window is not None else (0, b.shape[0])
    m_dim, k_piece = pieces[0].shape
    assert all(t.shape == pieces[0].shape for t in pieces)
    k_dim = n_p * k_piece
    d = x.shape[1]
    assert (b_rows, b.shape[1]) == ((k_dim, d) if mode == "nn" else (d, k_dim)), (name, pieces[0].shape, b.shape)
    tm = _mxu_tile(m_dim, 512)
    tk = _mxu_tile(k_piece, 1792)
    nkp = k_piece // tk
    nk = n_p * nkp
    has_res = res is not None
    if mode == "nt":
        assert b_row0 == 0
        b_spec, cb = _bs((d, tk), lambda i, k: (0, k)), 1
    else:
        assert b_row0 % tk == 0
        b_spec, cb = _bs((tk, d), lambda i, k: (k + b_row0 // tk, 0)), 0
    dims = (((1,), (cb,)), ((), ()))

    def body(*refs):
        b_ref = refs[n_p]
        r_ref = refs[n_p + 1] if has_res else None
        x_ref, g_ref, di_ref, dx_ref, dxb_ref, dg_ref = refs[n_p + 1 + has_res:n_p + 7 + has_res]
        i, k = pl.program_id(0), pl.program_id(1)

        def finish(dhv):
            if has_res:
                dhv = dhv + r_ref[...].astype(F32)
            xv = x_ref[...]
            r = lax.rsqrt(jnp.mean(xv * xv, axis=-1, keepdims=True) + EPS)
            xh = xv * r
            dxh = dhv * g_ref[...]
            dx = r * (dxh - xh * jnp.mean(dxh * xh, axis=-1, keepdims=True)) + di_ref[...]
            dx_ref[...] = dx
            dxb_ref[...] = dx.astype(BF16)
            dg_part = jnp.sum(dhv * xh, axis=0, keepdims=True)

            @pl.when(i == 0)
            def _():
                dg_ref[...] = dg_part

            @pl.when(i > 0)
            def _():
                dg_ref[...] += dg_part

        def step(a_ref):
            part = lax.dot_general(a_ref[...].astype(BF16), b_ref[...].astype(BF16), dims, preferred_element_type=F32)
            if nk == 1:
                finish(part)
                return
            acc = refs[-1]

            @pl.when(k == 0)
            def _():
                acc[...] = part

            @pl.when(jnp.logical_and(k > 0, k < nk - 1))
            def _():
                acc[...] += part

            @pl.when(k == nk - 1)
            def _():
                finish(acc[...] + part)

        if n_p == 1:
            step(refs[0])
        else:
            for q in range(n_p):
                pl.when(jnp.logical_and(k >= q * nkp, k < (q + 1) * nkp))(functools.partial(step, refs[q]))

    row = _bs((tm, d), lambda i, k: (i, 0))
    vec = _bs((1, d), lambda i, k: (0, 0))
    a_specs = [_bs((tm, tk), lambda i, k, q=q: (i, jnp.clip(k - q * nkp, 0, nkp - 1))) for q in range(n_p)]
    in_specs = a_specs + [b_spec] + ([row] if has_res else []) + [row, vec, row]
    args = pieces + (b,) + ((res,) if has_res else ()) + (x, g, dx_in)
    return pl.pallas_call(
        body, out_shape=(jax.ShapeDtypeStruct((m_dim, d), F32), jax.ShapeDtypeStruct((m_dim, d), BF16),
                         jax.ShapeDtypeStruct((1, d), F32)),
        grid=(m_dim // tm, nk), in_specs=in_specs, out_specs=(row, row, vec),
        scratch_shapes=[pltpu.VMEM((tm, d), F32)] if nk > 1 else [], name=name,
        compiler_params=_params("arbitrary", "arbitrary"))(*args)


def _loss_head(x, g, tgt, name):
    t_dim, d = x.shape
    tm = _tile(t_dim, 512)

    def body(x_ref, g_ref, t_ref, dx_ref, dxb_ref, dg_ref, loss_ref):
        xv = x_ref[...]
        gv = g_ref[...]
        r = lax.rsqrt(jnp.mean(xv * xv, axis=-1, keepdims=True) + EPS)
        xh = xv * r
        err = xh * gv - t_ref[...]
        dy = err * (1.0 / d)
        dxh = dy * gv
        dx = r * (dxh - xh * jnp.mean(dxh * xh, axis=-1, keepdims=True))
        dx_ref[...] = dx
        dxb_ref[...] = dx.astype(BF16)
        dg_part = jnp.sum(dy * xh, axis=0, keepdims=True)
        loss_part = jnp.full((1, LANE), 0.5 * jnp.sum(jnp.mean(err * err, axis=-1, keepdims=True)), F32)

        @pl.when(pl.program_id(0) == 0)
        def _():
            dg_ref[...] = dg_part
            loss_ref[...] = loss_part

        @pl.when(pl.program_id(0) > 0)
        def _():
            dg_ref[...] += dg_part
            loss_ref[...] += loss_part

    row = _bs((tm, d), lambda i: (i, 0))
    vec = _bs((1, d), lambda i: (0, 0))
    return pl.pallas_call(
        body, out_shape=(jax.ShapeDtypeStruct((t_dim, d), F32), jax.ShapeDtypeStruct((t_dim, d), BF16),
                         jax.ShapeDtypeStruct((1, d), F32), jax.ShapeDtypeStruct((1, LANE), F32)),
        grid=(t_dim // tm,), in_specs=[row, vec, row],
        out_specs=(row, row, vec, _bs((1, LANE), lambda i: (0, 0))), name=name,
        compiler_params=_params("arbitrary"))(x, g, tgt)


def _glu_conv_fwd(proj, dw_w, dw_b, n_batch, seq, name):
    kk, cc = dw_w.shape
    nj = cc // LANE
    ch = min(256, seq)

    def body(a_ref, gl_ref, w_ref, b_ref, o_ref, pad):
        pad[0:CONV_HALO, :] = jnp.zeros((CONV_HALO, LANE), F32)
        pad[CONV_HALO:CONV_HALO + seq, :] = a_ref[...].astype(F32) * _sig(gl_ref[...].astype(F32))
        for c0 in range(0, seq, ch):
            acc = jnp.broadcast_to(b_ref[...], (ch, LANE))
            for k in range(kk):
                acc = acc + w_ref[k:k + 1, :] * pad[pl.ds(c0 + CONV_HALO - (kk - 1) + k, ch), :]
            o_ref[c0:c0 + ch, :] = acc

    return pl.pallas_call(
        body, out_shape=jax.ShapeDtypeStruct((n_batch * seq, cc), F32), grid=(n_batch, nj),
        in_specs=[_bs((seq, LANE), lambda b, j: (b, j)), _bs((seq, LANE), lambda b, j: (b, nj + j)),
                  _bs((kk, LANE), lambda b, j: (0, j)), _bs((1, LANE), lambda b, j: (0, j))],
        out_specs=_bs((seq, LANE), lambda b, j: (b, j)),
        scratch_shapes=[pltpu.VMEM((seq + CONV_HALO, LANE), F32)], name=name,
        compiler_params=_params("parallel", "parallel"))(proj, proj, dw_w, dw_b)


def _glu_conv_bwd(proj, dw_w, dy1, n_batch, seq, name):
    kk, cc = dw_w.shape
    nj = cc // LANE
    ch = min(256, seq)

    def body(a_ref, gl_ref, dy_ref, w_ref, da_ref, dgl_ref, dw_ref, db_ref, padf, padb):
        first = pl.program_id(1) == 0
        padf[0:CONV_HALO, :] = jnp.zeros((CONV_HALO, LANE), F32)
        padf[CONV_HALO:CONV_HALO + seq, :] = a_ref[...].astype(F32) * _sig(gl_ref[...].astype(F32))
        padb[0:seq, :] = dy_ref[...]
        padb[seq:seq + CONV_HALO, :] = jnp.zeros((CONV_HALO, LANE), F32)

        @pl.when(first)
        def _():
            dw_ref[...] = jnp.zeros((kk, LANE), F32)
            db_ref[...] = jnp.zeros((1, LANE), F32)

        dws = [jnp.zeros((1, LANE), F32) for _ in range(kk)]
        for c0 in range(0, seq, ch):
            acc = jnp.zeros((ch, LANE), F32)
            y0 = padf[CONV_HALO + c0:CONV_HALO + c0 + ch, :]
            for k in range(kk):
                win = padb[pl.ds(c0 + (kk - 1) - k, ch), :]
                acc = acc + w_ref[k:k + 1, :] * win
                dws[k] = dws[k] + jnp.sum(win * y0, axis=0, keepdims=True)
            sg = _sig(gl_ref[c0:c0 + ch, :].astype(F32))
            da_ref[c0:c0 + ch, :] = (acc * sg).astype(BF16)
            dgl_ref[c0:c0 + ch, :] = (acc * a_ref[c0:c0 + ch, :].astype(F32) * sg * (1.0 - sg)).astype(BF16)
        for k in range(kk):
            dw_ref[k:k + 1, :] += dws[k]
        db_ref[...] += jnp.sum(dy_ref[...], axis=0, keepdims=True)

    tok = _bs((seq, LANE), lambda j, b: (b, j))
    t_dim = n_batch * seq
    return pl.pallas_call(
        body, out_shape=(jax.ShapeDtypeStruct((t_dim, cc), BF16), jax.ShapeDtypeStruct((t_dim, cc), BF16),
                         jax.ShapeDtypeStruct((kk, cc), F32), jax.ShapeDtypeStruct((1, cc), F32)),
        grid=(nj, n_batch),
        in_specs=[tok, _bs((seq, LANE), lambda j, b: (b, nj + j)), tok, _bs((kk, LANE), lambda j, b: (0, j))],
        out_specs=(tok, tok, _bs((kk, LANE), lambda j, b: (0, j)), _bs((1, LANE), lambda j, b: (0, j))),
        scratch_shapes=[pltpu.VMEM((seq + CONV_HALO, LANE), F32), pltpu.VMEM((seq + CONV_HALO, LANE), F32)],
        name=name, compiler_params=_params("parallel", "arbitrary"))(proj, proj, dy1, dw_w)


def _ln_silu_fwd(y1, g, b, name):
    t_dim, c = y1.shape
    tm = _tile(t_dim, 512)

    def body(y_ref, g_ref, b_ref, o_ref):
        yv = y_ref[...]
        xc = yv - jnp.mean(yv, axis=-1, keepdims=True)
        rstd = lax.rsqrt(jnp.mean(xc * xc, axis=-1, keepdims=True) + EPS)
        y2 = xc * rstd * g_ref[...] + b_ref[...]
        o_ref[...] = (y2 * _sig(y2)).astype(BF16)

    row = _bs((tm, c), lambda i: (i, 0))
    vec = _bs((1, c), lambda i: (0, 0))
    return pl.pallas_call(
        body, out_shape=jax.ShapeDtypeStruct((t_dim, c), BF16), grid=(t_dim // tm,),
        in_specs=[row, vec, vec], out_specs=row, name=name, compiler_params=_params("parallel"))(y1, g, b)


def _ln_silu_bwd(y1, g, b, dy3, name):
    t_dim, c = y1.shape
    tm = _tile(t_dim, 512)

    def body(y_ref, g_ref, b_ref, d_ref, dy_ref, dg_ref, db_ref):
        yv = y_ref[...]
        gv = g_ref[...]
        xc = yv - jnp.mean(yv, axis=-1, keepdims=True)
        rstd = lax.rsqrt(jnp.mean(xc * xc, axis=-1, keepdims=True) + EPS)
        yh = xc * rstd
        y2 = yh * gv + b_ref[...]
        s = _sig(y2)
        dy2 = d_ref[...].astype(F32) * (s * (1.0 + y2 * (1.0 - s)))
        dyh = dy2 * gv
        dy_ref[...] = rstd * (dyh - jnp.mean(dyh, axis=-1, keepdims=True)
                              - yh * jnp.mean(dyh * yh, axis=-1, keepdims=True))
        dg_part = jnp.sum(dy2 * yh, axis=0, keepdims=True)
        db_part = jnp.sum(dy2, axis=0, keepdims=True)

        @pl.when(pl.program_id(0) == 0)
        def _():
            dg_ref[...] = dg_part
            db_ref[...] = db_part

        @pl.when(pl.program_id(0) > 0)
        def _():
            dg_ref[...] += dg_part
            db_ref[...] += db_part

    row = _bs((tm, c), lambda i: (i, 0))
    vec = _bs((1, c), lambda i: (0, 0))
    return pl.pallas_call(
        body, out_shape=(jax.ShapeDtypeStruct((t_dim, c), F32), jax.ShapeDtypeStruct((1, c), F32),
                         jax.ShapeDtypeStruct((1, c), F32)),
        grid=(t_dim // tm,), in_specs=[row, vec, vec, row], out_specs=(row, vec, vec), name=name,
        compiler_params=_params("arbitrary"))(y1, g, b, dy3)


def _pool_fwd(proj, col0, n_groups, n_batch, seq, name):
    ch = min(256, seq)

    def body(u_ref, o_ref, pad):
        w = lax.shift_left(jnp.int32(2), pl.program_id(1))
        pad[0:POOL_HALO, :] = jnp.zeros((POOL_HALO, LANE), F32)
        pad[POOL_HALO:POOL_HALO + seq, :] = u_ref[...].astype(F32)
        for c0 in range(0, seq, ch):
            acc = jnp.zeros((ch, LANE), F32)
            for j in range(POOL_WINDOW_MAX):
                acc = acc + jnp.where(j < w, 1.0, 0.0).astype(F32) * pad[pl.ds(c0 + POOL_HALO - j, ch), :]
            t = c0 + lax.broadcasted_iota(jnp.int32, (ch, LANE), 0)
            cnt = jnp.minimum(t + 1, w).astype(F32)
            o_ref[c0:c0 + ch, :] = (acc / cnt - pad[POOL_HALO + c0:POOL_HALO + c0 + ch, :]).astype(BF16)

    return pl.pallas_call(
        body, out_shape=jax.ShapeDtypeStruct((n_batch * seq, n_groups * LANE), BF16), grid=(n_batch, n_groups),
        in_specs=[_bs((seq, LANE), lambda b, g: (b, col0 + g))], out_specs=_bs((seq, LANE), lambda b, g: (b, g)),
        scratch_shapes=[pltpu.VMEM((seq + POOL_HALO, LANE), F32)], name=name,
        compiler_params=_params("parallel", "parallel"))(proj)


def _pool_bwd(dzp, n_groups, n_batch, seq, name):
    ch = min(256, seq)

    def body(d_ref, o_ref, pad):
        w = lax.shift_left(jnp.int32(2), pl.program_id(1))
        for c0 in range(0, seq, ch):
            t = c0 + lax.broadcasted_iota(jnp.int32, (ch, LANE), 0)
            cnt = jnp.minimum(t + 1, w).astype(F32)
            pad[c0:c0 + ch, :] = d_ref[c0:c0 + ch, :].astype(F32) / cnt
        pad[seq:seq + POOL_HALO, :] = jnp.zeros((POOL_HALO, LANE), F32)
        for c0 in range(0, seq, ch):
            acc = jnp.zeros((ch, LANE), F32)
            for j in range(POOL_WINDOW_MAX):
                acc = acc + jnp.where(j < w, 1.0, 0.0).astype(F32) * pad[pl.ds(c0 + j, ch), :]
            o_ref[c0:c0 + ch, :] = (acc - d_ref[c0:c0 + ch, :].astype(F32)).astype(BF16)

    tok = _bs((seq, LANE), lambda b, g: (b, g))
    return pl.pallas_call(
        body, out_shape=jax.ShapeDtypeStruct((n_batch * seq, n_groups * LANE), BF16), grid=(n_batch, n_groups),
        in_specs=[tok], out_specs=tok, scratch_shapes=[pltpu.VMEM((seq + POOL_HALO, LANE), F32)], name=name,
        compiler_params=_params("parallel", "parallel"))(dzp)


def _merge_fwd(proj, col0, yc, yp, scale, name):
    t_dim, d = yc.shape
    half = d // 2
    tm = _tile(t_dim, 512)
    c0 = col0 // half

    def body(gc_ref, gp_ref, yc_ref, yp_ref, s_ref, o_ref):
        f32 = lambda r: r[...].astype(F32)
        o_ref[...] = (_sig(f32(gc_ref)) * f32(yc_ref) + _sig(f32(gp_ref)) * (f32(yp_ref) * s_ref[...])).astype(BF16)

    blk = _bs((tm, half), lambda i, j: (i, j))
    return pl.pallas_call(
        body, out_shape=jax.ShapeDtypeStruct((t_dim, d), BF16), grid=(t_dim // tm, 2),
        in_specs=[_bs((tm, half), lambda i, j: (i, c0 + j)), _bs((tm, half), lambda i, j: (i, c0 + 2 + j)),
                  blk, blk, _bs((1, half), lambda i, j: (0, j))],
        out_specs=blk, name=name, compiler_params=_params("parallel", "parallel"))(proj, proj, yc, yp, scale)


def _merge_bwd(proj, col0, yc, yp, scale, dm, name):
    t_dim, d = yc.shape
    half = d // 2
    tm = _tile(t_dim, 512)
    c0 = col0 // half

    def body(gc_ref, gp_ref, yc_ref, yp_ref, s_ref, dm_ref, dgc_ref, dgp_ref, dyc_ref, dyp_ref, ds_ref):
        dmv = dm_ref[...].astype(F32)
        sgc = _sig(gc_ref[...].astype(F32))
        sgp = _sig(gp_ref[...].astype(F32))
        sv = s_ref[...]
        ypre = yp_ref[...].astype(F32)
        dgc_ref[...] = (dmv * yc_ref[...].astype(F32) * sgc * (1.0 - sgc)).astype(BF16)
        dgp_ref[...] = (dmv * (ypre * sv) * sgp * (1.0 - sgp)).astype(BF16)
        dyc_ref[...] = (dmv * sgc).astype(BF16)
        dyp = dmv * sgp
        dyp_ref[...] = (dyp * sv).astype(BF16)
        part = jnp.sum(dyp * ypre, axis=0, keepdims=True)

        @pl.when(pl.program_id(1) == 0)
        def _():
            ds_ref[...] = part

        @pl.when(pl.program_id(1) > 0)
        def _():
            ds_ref[...] += part

    blk = _bs((tm, half), lambda j, i: (i, j))
    big = jax.ShapeDtypeStruct((t_dim, d), BF16)
    return pl.pallas_call(
        body, out_shape=(big, big, big, big, jax.ShapeDtypeStruct((1, d), F32)), grid=(2, t_dim // tm),
        in_specs=[_bs((tm, half), lambda j, i: (i, c0 + j)), _bs((tm, half), lambda j, i: (i, c0 + 2 + j)),
                  blk, blk, _bs((1, half), lambda j, i: (0, j)), blk],
        out_specs=(blk, blk, blk, blk, _bs((1, half), lambda j, i: (0, j))), name=name,
        compiler_params=_params("parallel", "arbitrary"))(proj, proj, yc, yp, scale, dm)


def _attn_fwd(q, kv, n_batch, seq, m_len, name):
    d = q.shape[1]
    hd = d // XA_HEADS
    tq = _tile(seq, 1024)
    nq = seq // tq
    scale = hd ** -0.5

    def body(q_ref, k_ref, v_ref, o_ref):
        sc = lax.dot_general(q_ref[...].astype(BF16), k_ref[...].astype(BF16), (((1,), (1,)), ((), ())),
                             preferred_element_type=F32) * scale
        p = jnp.exp(sc - jnp.max(sc, axis=-1, keepdims=True))
        pr = p / jnp.sum(p, axis=-1, keepdims=True)
        o_ref[...] = jnp.dot(pr.astype(BF16), v_ref[...].astype(BF16), preferred_element_type=F32).astype(BF16)

    return pl.pallas_call(
        body, out_shape=jax.ShapeDtypeStruct((n_batch * seq, d), BF16), grid=(n_batch, XA_HEADS, nq),
        in_specs=[_bs((tq, hd), lambda b, h, i: (b * nq + i, h)), _bs((m_len, hd), lambda b, h, i: (b, h)),
                  _bs((m_len, hd), lambda b, h, i: (b, XA_HEADS + h))],
        out_specs=_bs((tq, hd), lambda b, h, i: (b * nq + i, h)), name=name,
        compiler_params=_params("parallel", "parallel", "parallel"))(q, kv, kv)


def _attn_bwd(q, kv, datt, n_batch, seq, m_len, name):
    d = q.shape[1]
    hd = d // XA_HEADS
    tq = _tile(seq, 1024)
    nq = seq // tq
    scale = hd ** -0.5

    def body(q_ref, k_ref, v_ref, do_ref, dq_ref, dk_ref, dv_ref):
        qb = q_ref[...].astype(BF16)
        kb = k_ref[...].astype(BF16)
        vb = v_ref[...].astype(BF16)
        dob = do_ref[...].astype(BF16)
        sc = lax.dot_general(qb, kb, (((1,), (1,)), ((), ())), preferred_element_type=F32) * scale
        p = jnp.exp(sc - jnp.max(sc, axis=-1, keepdims=True))
        pr = p / jnp.sum(p, axis=-1, keepdims=True)
        dpr = lax.dot_general(dob, vb, (((1,), (1,)), ((), ())), preferred_element_type=F32)
        dsc = pr * (dpr - jnp.sum(dpr * pr, axis=-1, keepdims=True)) * scale
        dsb = dsc.astype(BF16)
        dq_ref[...] = jnp.dot(dsb, kb, preferred_element_type=F32).astype(BF16)
        dv_part = lax.dot_general(pr.astype(BF16), dob, (((0,), (0,)), ((), ())), preferred_element_type=F32)
        dk_part = lax.dot_general(dsb, qb, (((0,), (0,)), ((), ())), preferred_element_type=F32)

        @pl.when(pl.program_id(2) == 0)
        def _():
            dk_ref[...] = dk_part
            dv_ref[...] = dv_part

        @pl.when(pl.program_id(2) > 0)
        def _():
            dk_ref[...] += dk_part
            dv_ref[...] += dv_part

    qs = _bs((tq, hd), lambda b, h, i: (b * nq + i, h))
    ks = _bs((m_len, hd), lambda b, h, i: (b, h))
    return pl.pallas_call(
        body, out_shape=(jax.ShapeDtypeStruct((n_batch * seq, d), BF16), jax.ShapeDtypeStruct((n_batch * m_len, d), F32),
                         jax.ShapeDtypeStruct((n_batch * m_len, d), F32)),
        grid=(n_batch, XA_HEADS, nq),
        in_specs=[qs, ks, _bs((m_len, hd), lambda b, h, i: (b, XA_HEADS + h)), qs],
        out_specs=(qs, ks, ks), name=name,
        compiler_params=_params("parallel", "parallel", "arbitrary"))(q, kv, kv, datt)


def _gelu_parts(g):
    th = jnp.tanh(GELU_C0 * (g + GELU_C1 * g * g * g))
    return th, 0.5 * g * (1.0 + th)


def _ffn_act_fwd(up_g, up_v, dw_w, n_batch, seq, name):
    kk, c2 = dw_w.shape
    f_dim = c2 // 2
    wd = FFN_LANE_GROUPS * LANE
    nj = f_dim // wd
    ch = min(128, seq)

    def body(g_ref, v_ref, wg_ref, wv_ref, o_ref, padg, padv):
        for h in range(FFN_LANE_GROUPS):
            lanes = slice(h * LANE, (h + 1) * LANE)
            for pad, src in ((padg, g_ref), (padv, v_ref)):
                pad[h, 0:FFN_HALO, :] = jnp.zeros((FFN_HALO, LANE), F32)
                pad[h, FFN_HALO:FFN_HALO + seq, :] = src[:, lanes].astype(F32)
            for c0 in range(0, seq, ch):
                gate = jnp.zeros((ch, LANE), F32)
                val = jnp.zeros((ch, LANE), F32)
                for k in range(kk):
                    off = c0 + FFN_HALO - (kk - 1) + k
                    gate = gate + wg_ref[k:k + 1, lanes] * padg[h, pl.ds(off, ch), :]
                    val = val + wv_ref[k:k + 1, lanes] * padv[h, pl.ds(off, ch), :]
                o_ref[c0:c0 + ch, lanes] = (_gelu_parts(gate)[1] * val).astype(BF16)

    pad_shape = pltpu.VMEM((FFN_LANE_GROUPS, seq + FFN_HALO, LANE), F32)
    return pl.pallas_call(
        body, out_shape=jax.ShapeDtypeStruct((n_batch * seq, f_dim), BF16), grid=(n_batch, nj),
        in_specs=[_bs((seq, wd), lambda b, j: (b, j)), _bs((seq, wd), lambda b, j: (b, j)),
                  _bs((kk, wd), lambda b, j: (0, j)), _bs((kk, wd), lambda b, j: (0, nj + j))],
        out_specs=_bs((seq, wd), lambda b, j: (b, j)), scratch_shapes=[pad_shape, pad_shape], name=name,
        compiler_params=_params("parallel", "parallel"))(up_g, up_v, dw_w, dw_w)


def _ffn_act_bwd(up_g, up_v, dw_w, dact, n_batch, seq, name):
    kk, c2 = dw_w.shape
    f_dim = c2 // 2
    wd = FFN_LANE_GROUPS * LANE
    nj = f_dim // wd
    ch = min(128, seq)

    def body(g_ref, v_ref, wg_ref, wv_ref, da_ref, dg_ref, dv_ref, dwg_ref, dwv_ref, padg, padv, pbg, pbv):
        @pl.when(pl.program_id(1) == 0)
        def _():
            dwg_ref[...] = jnp.zeros((kk, wd), F32)
            dwv_ref[...] = jnp.zeros((kk, wd), F32)

        for h in range(FFN_LANE_GROUPS):
            lanes = slice(h * LANE, (h + 1) * LANE)
            for pad, src in ((padg, g_ref), (padv, v_ref)):
                pad[h, 0:FFN_HALO, :] = jnp.zeros((FFN_HALO, LANE), F32)
                pad[h, FFN_HALO:FFN_HALO + seq, :] = src[:, lanes].astype(F32)
            for pb in (pbg, pbv):
                pb[h, seq:seq + FFN_HALO, :] = jnp.zeros((FFN_HALO, LANE), F32)
            for c0 in range(0, seq, ch):
                gate = jnp.zeros((ch, LANE), F32)
                val = jnp.zeros((ch, LANE), F32)
                for k in range(kk):
                    off = c0 + FFN_HALO - (kk - 1) + k
                    gate = gate + wg_ref[k:k + 1, lanes] * padg[h, pl.ds(off, ch), :]
                    val = val + wv_ref[k:k + 1, lanes] * padv[h, pl.ds(off, ch), :]
                sq = gate * gate
                th = jnp.tanh(GELU_C0 * gate * (1.0 + GELU_C1 * sq))
                half = 0.5 * th + 0.5
                dgelu = half * (1.0 + gate * (GELU_C0 + 3.0 * GELU_C0 * GELU_C1 * sq) * (1.0 - th))
                dav = da_ref[c0:c0 + ch, lanes].astype(F32)
                pbg[h, c0:c0 + ch, :] = dav * val * dgelu
                pbv[h, c0:c0 + ch, :] = dav * (gate * half)
            for pb, pad, w_ref, d_ref, dw_ref in ((pbg, padg, wg_ref, dg_ref, dwg_ref), (pbv, padv, wv_ref, dv_ref, dwv_ref)):
                for c0 in range(0, seq, ch):
                    acc = jnp.zeros((ch, LANE), F32)
                    for k in range(kk):
                        acc = acc + w_ref[k:k + 1, lanes] * pb[h, pl.ds(c0 + (kk - 1) - k, ch), :]
                    d_ref[c0:c0 + ch, lanes] = acc.astype(BF16)
                for k in range(kk):
                    s = jnp.zeros((1, LANE), F32)
                    for c0 in range(0, seq, ch):
                        s = s + jnp.sum(pb[h, c0:c0 + ch, :] * pad[h, pl.ds(c0 + FFN_HALO - (kk - 1) + k, ch), :],
                                        axis=0, keepdims=True)
                    dw_ref[k:k + 1, lanes] += s

    t_dim = n_batch * seq
    tok = _bs((seq, wd), lambda j, b: (b, j))
    wblk = _bs((kk, wd), lambda j, b: (0, j))
    pad_shape = pltpu.VMEM((FFN_LANE_GROUPS, seq + FFN_HALO, LANE), F32)
    return pl.pallas_call(
        body, out_shape=(jax.ShapeDtypeStruct((t_dim, f_dim), BF16), jax.ShapeDtypeStruct((t_dim, f_dim), BF16),
                         jax.ShapeDtypeStruct((kk, f_dim), F32), jax.ShapeDtypeStruct((kk, f_dim), F32)),
        grid=(nj, n_batch),
        in_specs=[tok, tok, wblk, _bs((kk, wd), lambda j, b: (0, nj + j)), tok],
        out_specs=(tok, tok, wblk, wblk), scratch_shapes=[pad_shape, pad_shape, pad_shape, pad_shape], name=name,
        compiler_params=_params("parallel", "arbitrary"))(up_g, up_v, dw_w, dw_w, dact)


def _sum_rows(parts, out_dtype, name):
    r_dim, c_dim = parts[0].shape
    tr = _tile(r_dim, 1200, SUBLANE)
    n = len(parts)

    def body(*refs):
        acc = refs[0][...].astype(F32)
        for r in refs[1:n]:
            acc = acc + r[...].astype(F32)
        refs[n][...] = acc.astype(out_dtype)

    blk = _bs((tr, c_dim), lambda i: (i, 0))
    return pl.pallas_call(
        body, out_shape=jax.ShapeDtypeStruct((r_dim, c_dim), out_dtype), grid=(r_dim // tr,),
        in_specs=[blk] * n, out_specs=blk, name=name, compiler_params=_params("parallel"))(*parts)


def _adamw(w, g, m, v, name):
    shape = w.shape
    c_dim = shape[-1]
    r_dim = w.size // c_dim
    two_d = lambda t: t.reshape(r_dim, c_dim)
    tr = _tile(r_dim, max(SUBLANE, (256 * 1024) // max(c_dim, LANE) // SUBLANE * SUBLANE), SUBLANE)
    c1 = 1.0 - ADAM_B1 ** ADAM_STEP
    c2 = 1.0 - ADAM_B2 ** ADAM_STEP

    def body(w_ref, g_ref, m_ref, v_ref, d_ref, mo_ref, vo_ref):
        gv = g_ref[...]
        mn = ADAM_B1 * m_ref[...] + (1.0 - ADAM_B1) * gv
        vn = ADAM_B2 * v_ref[...] + (1.0 - ADAM_B2) * (gv * gv)
        mo_ref[...] = mn
        vo_ref[...] = vn
        d_ref[...] = -ADAM_LR * ((mn / c1) / (jnp.sqrt(vn / c2) + ADAM_EPS) + ADAM_WD * w_ref[...])

    blk = _bs((tr, c_dim), lambda i: (i, 0))
    out = jax.ShapeDtypeStruct((r_dim, c_dim), F32)
    d, mo, vo = pl.pallas_call(
        body, out_shape=(out, out, out), grid=(r_dim // tr,), in_specs=[blk] * 4, out_specs=(blk, blk, blk),
        name=name, compiler_params=_params("parallel"))(two_d(w), two_d(g), two_d(m), two_d(v))
    return d.reshape(shape), mo.reshape(shape), vo.reshape(shape)


HBM_SPEC = pl.BlockSpec(memory_space=pltpu.HBM)


def _position():
    return lax.axis_index("x"), lax.axis_index("y"), lax.axis_index("c")


def _all_gather(shard, name):
    def body(x_ref, out_ref, send_sems, recv_sems, local_sem):
        x, y, c = _position()
        me, sibling = (x, y, c), (x, y, 1 - c)
        chips = [(1 - x, y), (x, 1 - y), (1 - x, 1 - y)]

        def rows(px, py, pc):
            return out_ref.at[4 * px + 2 * py + pc]

        def copy(k, block, to, src=None):
            return pltpu.make_async_remote_copy(
                src_ref=rows(*block) if src is None else src, dst_ref=rows(*block),
                send_sem=send_sems.at[k], recv_sem=recv_sems.at[k], device_id=to, device_id_type=MESH)

        mine = pltpu.make_async_copy(x_ref, rows(*me), local_sem)
        mine.start()
        first = [copy(0, me, sibling, src=x_ref)]
        first += [copy(1 + j, me, (*chip, c), src=x_ref) for j, chip in enumerate(chips)]
        for cp in first:
            cp.start()
        passed = [copy(4 + j, (*chip, c), sibling) for j, chip in enumerate(chips)]
        for j, chip in enumerate(chips):
            copy(1 + j, (*chip, c), me).wait_recv()
            passed[j].start()
        copy(0, sibling, me).wait_recv()
        for j, chip in enumerate(chips):
            copy(4 + j, (*chip, 1 - c), me).wait_recv()
        for cp in first + passed:
            cp.wait_send()
        mine.wait()

    return pl.pallas_call(
        body, out_shape=jax.ShapeDtypeStruct((N_DEV,) + shard.shape, shard.dtype),
        in_specs=[HBM_SPEC], out_specs=HBM_SPEC,
        scratch_shapes=[pltpu.SemaphoreType.DMA((7,)), pltpu.SemaphoreType.DMA((7,)), pltpu.SemaphoreType.DMA(())],
        name=name)(shard)


CHIP_RELATIONS = ((0, 0), (1, 0), (0, 1), (1, 1))


def _rs_pair_exchange(g, name):
    _, r_dim, c_dim = g.shape
    n = len(CHIP_RELATIONS)

    def body(g_ref, recv_ref, send_sems, recv_sems):
        x, y, c = _position()
        sibling = (x, y, 1 - c)
        copies = []
        for k, (rx, ry) in enumerate(CHIP_RELATIONS):
            px = x + rx - 2 * x * rx
            py = y + ry - 2 * y * ry
            copies.append(pltpu.make_async_remote_copy(
                src_ref=g_ref.at[4 * px + 2 * py + 1 - c], dst_ref=recv_ref.at[k], send_sem=send_sems.at[k],
                recv_sem=recv_sems.at[k], device_id=sibling, device_id_type=MESH))
        for cp in copies:
            cp.start()
        for cp in copies:
            cp.wait()

    return pl.pallas_call(
        body, out_shape=jax.ShapeDtypeStruct((n, r_dim, c_dim), g.dtype), in_specs=[HBM_SPEC], out_specs=HBM_SPEC,
        scratch_shapes=[pltpu.SemaphoreType.DMA((n,)), pltpu.SemaphoreType.DMA((n,))], name=name)(g)


def _rs_pair_sum(g, recv, name):
    _, r_dim, c_dim = g.shape
    n = len(CHIP_RELATIONS)
    tr = _tile(r_dim, 1200, SUBLANE)
    x, y, c = _position()
    own = jnp.stack([4 * (x + rx - 2 * x * rx) + 2 * (y + ry - 2 * y * ry) + c for rx, ry in CHIP_RELATIONS])

    def body(own_ref, g_ref, r_ref, o_ref):
        o_ref[...] = (g_ref[...].astype(F32) + r_ref[...].astype(F32)).astype(o_ref.dtype)

    blk = _bs((None, tr, c_dim), lambda k, i, own_ref: (k, i, 0))
    return pl.pallas_call(
        body, out_shape=jax.ShapeDtypeStruct((n, r_dim, c_dim), g.dtype),
        grid_spec=pltpu.PrefetchScalarGridSpec(
            num_scalar_prefetch=1, grid=(n, r_dim // tr),
            in_specs=[_bs((None, tr, c_dim), lambda k, i, own_ref: (own_ref[k], i, 0)), blk], out_specs=blk),
        name=name, compiler_params=_params("parallel", "parallel"))(own.astype(jnp.int32), g, recv)


SEM_SPEC = pl.BlockSpec(memory_space=pltpu.SEMAPHORE)
DATAFLOW = pltpu.SideEffectType.DATAFLOW_SIDE_EFFECTING
CHIP_FLIPS = CHIP_RELATIONS[1:]
TOKEN = jax.ShapeDtypeStruct((SUBLANE, LANE), F32)


def _flip(v, r):
    return v + r - 2 * v * r


def _chip_copies(src_ref, src_of, dst_ref, dst_of, send_sems, recv_sems):
    x, y, c = _position()
    me = 4 * x + 2 * y + c
    out = []
    for k, (rx, ry) in enumerate(CHIP_FLIPS):
        px, py = _flip(x, rx), _flip(y, ry)
        peer = 4 * px + 2 * py + c
        out.append(pltpu.make_async_remote_copy(
            src_ref=src_ref.at[src_of(k, me, peer)], dst_ref=dst_ref.at[dst_of(k, me, peer)],
            send_sem=send_sems.at[k], recv_sem=recv_sems.at[k], device_id=(px, py, c), device_id_type=MESH))
    return out


def _device_block(ref, spec, d):
    rows, axis = spec
    return ref.at[pl.ds(d * rows, rows)] if axis == 0 else ref.at[:, pl.ds(d * rows, rows)]


def _ag_chips_start(lands, specs, after, name):
    n = len(lands)
    nf = len(CHIP_FLIPS)

    def body(*refs):
        send_sems, recv_sems, token = refs[n + 1], refs[n + 2], refs[-1]
        x, y, c = _position()
        me = 4 * x + 2 * y + c
        for i, spec in enumerate(specs):
            blk = _device_block(refs[i], spec, me)
            for k, (rx, ry) in enumerate(CHIP_FLIPS):
                pltpu.make_async_remote_copy(
                    src_ref=blk, dst_ref=blk, send_sem=send_sems.at[nf * i + k], recv_sem=recv_sems.at[nf * i + k],
                    device_id=(_flip(x, rx), _flip(y, ry), c), device_id_type=MESH).start()
        token[...] = jnp.zeros(TOKEN.shape, TOKEN.dtype)

    sems = pltpu.SemaphoreType.DMA((nf * n,))
    return pl.pallas_call(
        body, name=name, out_shape=(sems, sems, *[pltpu.HBM(t.shape, t.dtype) for t in lands], TOKEN),
        in_specs=(HBM_SPEC,) * n + (ANY_SPEC,),
        out_specs=(SEM_SPEC, SEM_SPEC) + (HBM_SPEC,) * n + (pl.BlockSpec(memory_space=pltpu.VMEM),),
        input_output_aliases={i: 2 + i for i in range(n)}, compiler_params=pltpu.CompilerParams(has_side_effects=DATAFLOW),
    )(*[pltpu.with_memory_space_constraint(t, pltpu.HBM) for t in lands], after)


def _ag_chips_wait(send_sems, recv_sems, lands, specs, after, name):
    n = len(lands)
    nf = len(CHIP_FLIPS)

    def body(*refs):
        send_sems, recv_sems = refs[n], refs[n + 1]
        x, y, c = _position()
        me = 4 * x + 2 * y + c
        for i, spec in enumerate(specs):
            for k, (rx, ry) in enumerate(CHIP_FLIPS):
                px, py = _flip(x, rx), _flip(y, ry)
                cp = pltpu.make_async_remote_copy(
                    src_ref=_device_block(refs[i], spec, me), dst_ref=_device_block(refs[i], spec, 4 * px + 2 * py + c),
                    send_sem=send_sems.at[nf * i + k], recv_sem=recv_sems.at[nf * i + k],
                    device_id=(px, py, c), device_id_type=MESH)
                cp.wait_send()
                cp.wait_recv()

    return pl.pallas_call(
        body, name=name, out_shape=tuple(pltpu.HBM(t.shape, t.dtype) for t in lands),
        in_specs=(HBM_SPEC,) * n + (SEM_SPEC, SEM_SPEC, ANY_SPEC), out_specs=(HBM_SPEC,) * n,
        input_output_aliases={i: i for i in range(n)}, compiler_params=pltpu.CompilerParams(has_side_effects=DATAFLOW),
    )(*lands, send_sems, recv_sems, after)


def _ag_pair_forward(lands, specs, name):
    n = len(lands)
    nr = len(CHIP_RELATIONS)

    def body(*refs):
        outs, send_sems, recv_sems = refs[n:2 * n], refs[2 * n], refs[2 * n + 1]
        x, y, c = _position()
        copies = []
        for i, spec in enumerate(specs):
            for k, (rx, ry) in enumerate(CHIP_RELATIONS):
                chip = 4 * _flip(x, rx) + 2 * _flip(y, ry)
                held = _device_block(outs[i], spec, chip + c)
                sems = dict(send_sem=send_sems.at[nr * i + k], recv_sem=recv_sems.at[nr * i + k],
                            device_id=(x, y, 1 - c), device_id_type=MESH)
                mine = pltpu.make_async_remote_copy(src_ref=held, dst_ref=held, **sems)
                theirs = pltpu.make_async_remote_copy(src_ref=held, dst_ref=_device_block(outs[i], spec, chip + 1 - c), **sems)
                copies.append((mine, theirs))
        for mine, _ in copies:
            mine.start()
        for mine, theirs in copies:
            mine.wait_send()
            theirs.wait_recv()

    sems = pltpu.SemaphoreType.DMA((nr * n,))
    return pl.pallas_call(
        body, out_shape=tuple(jax.ShapeDtypeStruct(t.shape, t.dtype) for t in lands), in_specs=[HBM_SPEC] * n,
        out_specs=(HBM_SPEC,) * n, input_output_aliases={i: i for i in range(n)}, scratch_shapes=[sems, sems], name=name)(*lands)


def _rs_chips_start(pair, name):
    _, r_dim, c_dim = pair.shape
    n = len(CHIP_FLIPS)

    def body(pair_ref, far_ref, send_sems, recv_sems, pair_thru, far_thru, token):
        for cp in _chip_copies(pair_ref, lambda k, me, peer: k + 1, far_ref, lambda k, me, peer: k, send_sems, recv_sems):
            cp.start()
        token[...] = jnp.zeros(TOKEN.shape, TOKEN.dtype)

    far = lax.empty((n, r_dim, c_dim), pair.dtype)
    return pl.pallas_call(
        body, name=name,
        out_shape=(pltpu.SemaphoreType.DMA((n,)), pltpu.SemaphoreType.DMA((n,)), pltpu.HBM(pair.shape, pair.dtype),
                   pltpu.HBM(far.shape, far.dtype), TOKEN),
        in_specs=(HBM_SPEC, HBM_SPEC),
        out_specs=(SEM_SPEC, SEM_SPEC, HBM_SPEC, HBM_SPEC, pl.BlockSpec(memory_space=pltpu.VMEM)),
        input_output_aliases={0: 2, 1: 3}, compiler_params=pltpu.CompilerParams(has_side_effects=DATAFLOW),
    )(pltpu.with_memory_space_constraint(pair, pltpu.HBM), pltpu.with_memory_space_constraint(far, pltpu.HBM))


def _rs_chips_wait(send_sems, recv_sems, pair, far, after, name):
    def body(pair_ref, far_ref, send_sems, recv_sems, after_ref, pair_out, far_out):
        for cp in _chip_copies(pair_ref, lambda k, me, peer: k + 1, far_ref, lambda k, me, peer: k, send_sems, recv_sems):
            cp.wait_send()
            cp.wait_recv()

    return pl.pallas_call(
        body, name=name, out_shape=(pltpu.HBM(pair.shape, pair.dtype), pltpu.HBM(far.shape, far.dtype)),
        in_specs=(HBM_SPEC, HBM_SPEC, SEM_SPEC, SEM_SPEC, ANY_SPEC),
        out_specs=(HBM_SPEC, HBM_SPEC), input_output_aliases={0: 0, 1: 1},
        compiler_params=pltpu.CompilerParams(has_side_effects=DATAFLOW),
    )(pair, far, send_sems, recv_sems, after)


def _rs_final_sum(pair, far, name):
    _, r_dim, c_dim = pair.shape
    tr = _tile(r_dim, 1200, SUBLANE)

    def body(p_ref, f0_ref, f1_ref, f2_ref, o_ref):
        o_ref[...] = ((p_ref[...].astype(F32) + f0_ref[...].astype(F32)) + f1_ref[...].astype(F32)) + f2_ref[...].astype(F32)

    def slot(k):
        return _bs((None, tr, c_dim), lambda i: (k, i, 0))

    return pl.pallas_call(
        body, out_shape=jax.ShapeDtypeStruct((r_dim, c_dim), F32), grid=(r_dim // tr,),
        in_specs=[slot(0), slot(0), slot(1), slot(2)], out_specs=_bs((tr, c_dim), lambda i: (i, 0)), name=name,
        compiler_params=_params("parallel"))(pair, far, far, far)


def _reduce_scatter_begin(g, name):
    recv = _rs_pair_exchange(g, name + "_pair")
    pair = _rs_pair_sum(g, recv, name + "_pairsum")
    return _rs_chips_start(pair, name + "_chips_start")


def _reduce_scatter_end(state, after, name):
    send_sems, recv_sems, pair, far, _ = state
    pair, far = _rs_chips_wait(send_sems, recv_sems, pair, far, after, name + "_chips_wait")
    return _rs_final_sum(pair, far, name + "_sum")


MATRICES = (("w_in", True), ("w_out", False), ("w_q", False), ("w_kv", True), ("w_o", False), ("w_up", True),
            ("w_down", False), ("w_conv_out", True), ("w_pool_grp", True))
MIX_NAMES = ("w_in", "w_conv_out", "w_pool_grp", "w_out")
REST_NAMES = ("w_q", "w_kv", "w_o", "w_up", "w_down")


def _parts(layer):
    return (("mix", MIX_NAMES), ("rest", REST_NAMES)) if layer == 0 else (("all", MIX_NAMES + REST_NAMES),)


def _to_rows(name, transposed, w, d_model):
    if name == "w_pool_grp":
        w = jnp.swapaxes(w, 1, 2)
    elif transposed:
        w = w.T
    return w.reshape(-1, d_model)


def _stored_form(name, transposed, rows, shard_shape):
    if name == "w_pool_grp":
        g, i, o = shard_shape
        return rows.reshape(g, o, i)
    if transposed:
        return rows.reshape(shard_shape[1], shard_shape[0])
    return rows.reshape(shard_shape)


def _scatter_blocks(name, full, shard_shape, d_model, n_dev=N_DEV):
    if name == "w_pool_grp":
        g, i, o = shard_shape
        return jnp.swapaxes(full.reshape(g, n_dev, o, i), 0, 1).reshape(n_dev, -1, d_model)
    return full.reshape(n_dev, -1, d_model)


def kernel(x, mem, mix_norm_g, w_in, conv_dw_w, conv_dw_b, conv_ln_g, conv_ln_b, w_conv_out, w_pool_grp, pool_scale, w_out, xattn_norm_g, mem_norm_g, w_q, w_kv, w_o, ffn_norm_g, w_up, ffn_dw_w, w_down, final_norm_g, loss_target, m_mix_norm_g, m_w_in, m_conv_dw_w, m_conv_dw_b, m_conv_ln_g, m_conv_ln_b, m_w_conv_out, m_w_pool_grp, m_pool_scale, m_w_out, m_xattn_norm_g, m_mem_norm_g, m_w_q, m_w_kv, m_w_o, m_ffn_norm_g, m_w_up, m_ffn_dw_w, m_w_down, m_final_norm_g, v_mix_norm_g, v_w_in, v_conv_dw_w, v_conv_dw_b, v_conv_ln_g, v_conv_ln_b, v_w_conv_out, v_w_pool_grp, v_pool_scale, v_w_out, v_xattn_norm_g, v_mem_norm_g, v_w_q, v_w_kv, v_w_o, v_ffn_norm_g, v_w_up, v_ffn_dw_w, v_w_down, v_final_norm_g):
    p = dict(locals())
    weight_names = ["mix_norm_g", "w_in", "conv_dw_w", "conv_dw_b", "conv_ln_g", "conv_ln_b", "w_conv_out",
                    "w_pool_grp", "pool_scale", "w_out", "xattn_norm_g", "mem_norm_g", "w_q", "w_kv", "w_o",
                    "ffn_norm_g", "w_up", "ffn_dw_w", "w_down", "final_norm_g"]
    n_batch, seq, d_model = x.shape
    m_len = mem.shape[1]
    depth = w_in.shape[0]
    assert depth == 2, "the exchange schedule below is written for two layers"
    t_dim = n_batch * seq
    c_conv = conv_dw_b.shape[1]
    n_groups = w_pool_grp.shape[1]
    assert w_pool_grp.shape[2] == LANE and c_conv % LANE == 0 and n_groups * LANE == c_conv
    gate_col0 = 2 * c_conv + n_groups * LANE
    pool_col0 = (2 * c_conv) // LANE

    dev = 4 * lax.axis_index("x") + 2 * lax.axis_index("y") + lax.axis_index("c")
    filt = jnp.concatenate([conv_dw_w.reshape(-1), ffn_dw_w.reshape(-1)])
    filt_rows = filt.reshape(-1, d_model)
    transposed = dict(MATRICES)
    layout = {part: [(name, transposed[name], _to_rows(name, transposed[name], p[name][0], d_model).shape[0])
                     for name in names] for l in range(depth) for part, names in _parts(l)}
    part_of = {(l, name): part for l in range(depth) for part, names in _parts(l) for name in names}

    def landing(name, shard):
        if name == "w_pool_grp":
            block, axis = jnp.swapaxes(shard, 1, 2), 1
        elif name == "filt":
            block, axis = shard, 0
        else:
            block, axis = (shard.T if transposed[name] else shard), 0
        block = block if name == "filt" else block.astype(BF16)
        rows = block.shape[axis]
        shape = block.shape[:axis] + (N_DEV * rows,) + block.shape[axis + 1:]
        start = (0,) * axis + (dev * rows,) + (0,) * (block.ndim - axis - 1)
        return lax.dynamic_update_slice(lax.empty(shape, block.dtype), block, start), (rows, axis)

    ag_state = {}
    after = filt_rows
    for l in range(depth):
        for part, names in _parts(l):
            items = [(name, p[name][l]) for name in names]
            if (l, part) == (0, part_of[(0, "w_in")]):
                items.append(("filt", filt_rows))
            lands, specs = zip(*[landing(name, shard) for name, shard in items])
            out = _ag_chips_start(lands, specs, after, f"ag{l}{part}_chips_start")
            ag_state[(l, part)] = ([name for name, _ in items], specs, out)
            after = out[-1]
    all_started = after

    full = [dict() for _ in range(depth)]

    def ensure(l, name, after):
        if name in full[l]:
            return
        part = part_of[(l, name)]
        names, specs, out = ag_state[(l, part)]
        lands = _ag_chips_wait(out[0], out[1], out[2:-1], specs, after, f"ag{l}{part}_chips_wait")
        lands = _ag_pair_forward(lands, specs, f"ag{l}{part}_pair_forward")
        full[l].update(zip(names, lands))

    vec = lambda a: a.reshape(1, -1)
    x2d = x.reshape(t_dim, d_model)
    mem2d = mem.reshape(n_batch * m_len, d_model)
    mem_n = _rmsnorm_fwd(mem2d, vec(mem_norm_g), "mem_norm", after=all_started)
    h_first = _rmsnorm_fwd(x2d, vec(mix_norm_g[0]), "mix_norm_l0", after=mem_n)
    ensure(0, "w_in", h_first)
    filt_all = full[0]["filt"].reshape(N_DEV, -1)
    n_cw = conv_dw_w.size
    kc, cs = conv_dw_w.shape[1:]
    kf, fs = ffn_dw_w.shape[1:]
    conv_w_full = jnp.moveaxis(filt_all[:, :n_cw].reshape(N_DEV, depth, kc, cs), 0, 2).reshape(depth, kc, N_DEV * cs)
    ffn_w_full = jnp.moveaxis(filt_all[:, n_cw:].reshape(N_DEV, depth, kf, fs), 0, 2).reshape(depth, kf, N_DEV * fs)

    saved = []
    xc = x2d
    for l in range(depth):
        ensure(l, "w_in", xc)
        wl = full[l]
        s = {"x0": xc}
        s["h"] = h_first if l == 0 else _rmsnorm_fwd(xc, vec(mix_norm_g[l]), f"mix_norm_l{l}")
        s["proj"] = _matmul(s["h"], wl["w_in"], "nt", f"in_proj_l{l}", out_dtype=BF16)
        s["y1"] = _glu_conv_fwd(s["proj"], conv_w_full[l], vec(conv_dw_b[l]), n_batch, seq, f"glu_conv_l{l}")
        s["y3"] = _ln_silu_fwd(s["y1"], vec(conv_ln_g[l]), vec(conv_ln_b[l]), f"ln_silu_l{l}")
        s["yc"] = _matmul(s["y3"], wl["w_conv_out"], "nt", f"conv_out_l{l}", out_dtype=BF16)
        s["zp"] = _pool_fwd(s["proj"], pool_col0, n_groups, n_batch, seq, f"pool_l{l}")
        s["yp"] = _grouped(s["zp"], wl["w_pool_grp"], "nt", f"pool_proj_l{l}", out_dtype=BF16)
        s["merged"] = _merge_fwd(s["proj"], gate_col0, s["yc"], s["yp"], vec(pool_scale[l]), f"merge_l{l}")
        s["x1"] = _matmul(s["merged"], wl["w_out"], "nn", f"mix_out_l{l}", res=xc)
        ensure(l, "w_q", s["x1"])
        half_up = wl["w_up"].shape[0] // 2
        up_gate, up_val = (0, half_up), (half_up, half_up)
        s["hq"] = _rmsnorm_fwd(s["x1"], vec(xattn_norm_g[l]), f"xattn_norm_l{l}")
        s["q"] = _matmul(s["hq"], wl["w_q"], "nn", f"q_proj_l{l}", out_dtype=BF16)
        s["kv"] = _matmul(mem_n, wl["w_kv"], "nt", f"kv_proj_l{l}", out_dtype=BF16)
        s["att"] = _attn_fwd(s["q"], s["kv"], n_batch, seq, m_len, f"attn_l{l}")
        s["x2"] = _matmul(s["att"], wl["w_o"], "nn", f"attn_out_l{l}", res=s["x1"])
        s["hf"] = _rmsnorm_fwd(s["x2"], vec(ffn_norm_g[l]), f"ffn_norm_l{l}")
        s["up_g"] = _matmul(s["hf"], wl["w_up"], "nt", f"up_proj_gate_l{l}", out_dtype=BF16, b_window=up_gate)
        s["up_v"] = _matmul(s["hf"], wl["w_up"], "nt", f"up_proj_val_l{l}", out_dtype=BF16, b_window=up_val)
        s["act"] = _ffn_act_fwd(s["up_g"], s["up_v"], ffn_w_full[l], n_batch, seq, f"ffn_act_l{l}")
        xc = _matmul(s["act"], wl["w_down"], "nn", f"down_proj_l{l}", res=s["x2"])
        saved.append(s)

    dx, dxb, dg_final, loss_part = _loss_head(xc, vec(final_norm_g), loss_target.reshape(t_dim, d_model), "loss_head")

    small = {"final_norm_g": dg_final.reshape(-1)}
    big = [dict() for _ in range(depth)]
    rs_state = {}
    rs_after = loss_part

    def rs_begin(l, part):
        pack = lax.empty((N_DEV, sum(nrows for _, _, nrows in layout[part]), d_model), BF16)
        row0 = 0
        for name, _, nrows in layout[part]:
            pieces = big[l][name] if isinstance(big[l][name], tuple) else (big[l][name],)
            d0 = 0
            for piece in pieces:
                blocks = _scatter_blocks(name, piece, p[name].shape[1:], d_model, N_DEV // len(pieces)).astype(BF16)
                pack = lax.dynamic_update_slice(pack, blocks, (d0, row0, 0))
                d0 += blocks.shape[0]
            row0 += nrows
        rs_state[(l, part)] = _reduce_scatter_begin(pack, f"rs{l}{part}")
        return rs_state[(l, part)][4]

    dmem_n = None
    for l in reversed(range(depth)):
        wl, s = full[l], saved[l]
        sm = {}
        dact = _matmul(dxb, wl["w_down"], "nt", f"d_act_l{l}", out_dtype=BF16, after=rs_after)
        big[l]["w_down"] = _matmul(s["act"], dxb, "tn", f"d_w_down_l{l}", out_dtype=BF16)
        dup_g, dup_v, dwf_g, dwf_v = _ffn_act_bwd(s["up_g"], s["up_v"], ffn_w_full[l], dact, n_batch, seq,
                                                  f"ffn_act_bwd_l{l}")
        sm["ffn_dw_w"] = jnp.concatenate([dwf_g, dwf_v], axis=1)
        dx, dxb, dg = _matmul_rmsnorm_bwd((dup_g, dup_v), wl["w_up"], "nn", s["x2"], vec(ffn_norm_g[l]), dx,
                                          f"d_hf_ffn_norm_bwd_l{l}")
        big[l]["w_up"] = (_matmul(dup_g, s["hf"], "tn", f"d_w_up_gate_l{l}", out_dtype=BF16),
                          _matmul(dup_v, s["hf"], "tn", f"d_w_up_val_l{l}", out_dtype=BF16))
        sm["ffn_norm_g"] = dg
        datt = _matmul(dxb, wl["w_o"], "nt", f"d_att_l{l}", out_dtype=BF16, after=rs_after)
        big[l]["w_o"] = _matmul(s["att"], dxb, "tn", f"d_w_o_l{l}", out_dtype=BF16)
        dq, dk, dv = _attn_bwd(s["q"], s["kv"], datt, n_batch, seq, m_len, f"attn_bwd_l{l}")
        dkv = jnp.concatenate([dk, dv], axis=1)
        big[l]["w_kv"] = _matmul(dkv, mem_n, "tn", f"d_w_kv_l{l}", out_dtype=BF16)
        dmem_n = _matmul(dkv, wl["w_kv"], "nn", f"d_mem_l{l}", res=dmem_n)
        big[l]["w_q"] = _matmul(s["hq"], dq, "tn", f"d_w_q_l{l}", out_dtype=BF16)
        dx, dxb, dg = _matmul_rmsnorm_bwd(dq, wl["w_q"], "nt", s["x1"], vec(xattn_norm_g[l]), dx,
                                          f"d_hq_xattn_norm_bwd_l{l}")
        sm["xattn_norm_g"] = dg
        if part_of[(l, "w_q")] != part_of[(l, "w_in")]:
            rs_after = rs_begin(l, part_of[(l, "w_q")])
        dmerged = _matmul(dxb, wl["w_out"], "nt", f"d_merged_l{l}", out_dtype=BF16, after=rs_after)
        big[l]["w_out"] = _matmul(s["merged"], dxb, "tn", f"d_w_out_l{l}", out_dtype=BF16)
        dgc, dgp, dyc, dyp, dscale = _merge_bwd(s["proj"], gate_col0, s["yc"], s["yp"], vec(pool_scale[l]), dmerged,
                                                f"merge_bwd_l{l}")
        sm["pool_scale"] = dscale
        dzp = _grouped(dyp, wl["w_pool_grp"], "nn", f"d_zp_l{l}", out_dtype=BF16)
        big[l]["w_pool_grp"] = _grouped_tn(dyp, s["zp"], n_groups, f"d_w_pool_l{l}")
        du = _pool_bwd(dzp, n_groups, n_batch, seq, f"pool_bwd_l{l}")
        dy3 = _matmul(dyc, wl["w_conv_out"], "nn", f"d_y3_l{l}", out_dtype=BF16)
        big[l]["w_conv_out"] = _matmul(dyc, s["y3"], "tn", f"d_w_conv_out_l{l}", out_dtype=BF16)
        dy1, dlg, dlb = _ln_silu_bwd(s["y1"], vec(conv_ln_g[l]), vec(conv_ln_b[l]), dy3, f"ln_silu_bwd_l{l}")
        sm["conv_ln_g"], sm["conv_ln_b"] = dlg, dlb
        da, dgl, dcw, dcb = _glu_conv_bwd(s["proj"], conv_w_full[l], dy1, n_batch, seq, f"glu_conv_bwd_l{l}")
        sm["conv_dw_w"], sm["conv_dw_b"] = dcw, dcb
        dproj = jnp.concatenate([da, dgl, du, dgc, dgp], axis=1)
        big[l]["w_in"] = _matmul(dproj, s["h"], "tn", f"d_w_in_l{l}", out_dtype=BF16)
        dx, dxb, dg = _matmul_rmsnorm_bwd(dproj, wl["w_in"], "nn", s["x0"], vec(mix_norm_g[l]), dx,
                                          f"d_h_mix_norm_bwd_l{l}")
        sm["mix_norm_g"] = dg
        for k, val in sm.items():
            small[(l, k)] = val.reshape(-1)
        rs_after = rs_begin(l, part_of[(l, "w_in")])
    _, _, dg_mem = _rmsnorm_bwd(mem2d, vec(mem_norm_g), dmem_n, None, "mem_norm_bwd")
    small["mem_norm_g"] = dg_mem.reshape(-1)
    small["loss"] = loss_part.reshape(-1)

    grads = {}
    per_layer = {name: [None] * depth for name, _ in MATRICES}
    for l in reversed(range(depth)):
        for part, _ in reversed(_parts(l)):
            mat_grads = _reduce_scatter_end(rs_state[(l, part)], rs_after, f"rs{l}{part}")
            row0 = 0
            for name, tr, nrows in layout[part]:
                per_layer[name][l] = _stored_form(name, tr, mat_grads[row0:row0 + nrows], p[name].shape[1:])
                row0 += nrows
    flip = lambda t: jnp.swapaxes(t, -1, -2)
    stored_grads = {name: jnp.stack(per_layer[name]) for name, _ in MATRICES}
    for name, tr in MATRICES:
        grads[name] = flip(stored_grads[name]) if tr else stored_grads[name]

    keys = list(small.keys())
    flat = jnp.concatenate([small[k] for k in keys])
    n_small = flat.shape[0]
    rows_small = -(-n_small // (SUBLANE * d_model)) * SUBLANE
    flat = jnp.pad(flat, (0, rows_small * d_model - n_small)).reshape(rows_small, d_model)
    every = _all_gather(flat, "small_all_gather")
    total = _sum_rows([every[i] for i in range(N_DEV)], F32, "small_sum").reshape(-1)
    off = 0
    red = {}
    for k in keys:
        red[k] = total[off:off + small[k].shape[0]]
        off += small[k].shape[0]
    loss = red["loss"][0]
    for name in ("mix_norm_g", "conv_dw_b", "conv_ln_g", "conv_ln_b", "pool_scale", "xattn_norm_g", "ffn_norm_g"):
        grads[name] = jnp.stack([red[(l, name)] for l in range(depth)])
    grads["conv_dw_w"] = jnp.stack([
        lax.dynamic_slice_in_dim(red[(l, "conv_dw_w")].reshape(kc, N_DEV * cs), dev * cs, cs, axis=1)
        for l in range(depth)])
    grads["ffn_dw_w"] = jnp.stack([
        lax.dynamic_slice_in_dim(red[(l, "ffn_dw_w")].reshape(kf, N_DEV * fs), dev * fs, fs, axis=1)
        for l in range(depth)])
    grads["mem_norm_g"] = red["mem_norm_g"]
    grads["final_norm_g"] = red["final_norm_g"]

    deltas, new_m, new_v = {}, {}, {}
    for name in weight_names:
        if transposed.get(name, False):
            out = _adamw(flip(p[name]), stored_grads[name], flip(p["m_" + name]), flip(p["v_" + name]), f"adamw_{name}")
            deltas[name], new_m[name], new_v[name] = (flip(t) for t in out)
        else:
            deltas[name], new_m[name], new_v[name] = _adamw(p[name], grads[name], p["m_" + name], p["v_" + name],
                                                            f"adamw_{name}")
    grad_x = dx.reshape(n_batch, seq, d_model)
    return (loss, grad_x, *[grads[n] for n in weight_names], *[deltas[n] for n in weight_names],
            *[new_m[n] for n in weight_names], *[new_v[n] for n in weight_names])
```

```python
import functools

import jax
import jax.numpy as jnp
from jax import lax
from jax.experimental import pallas as pl
from jax.experimental.pallas import tpu as pltpu

F32 = jnp.float32
BF16 = jnp.bfloat16
MESH = pl.DeviceIdType.MESH

N_DEV = 8
EPS = 1e-6
V7X_VMEM_BYTES = 64 * 1024 * 1024
VMEM_LIMIT = (V7X_VMEM_BYTES * 3) // 4
LANE = 128
SUBLANE = 8

CONV_HALO = 32
POOL_HALO = 16
FFN_HALO = 8
FFN_LANE_GROUPS = 2
POOL_WINDOW_MAX = 16
XA_HEADS = 4

ADAM_LR = 0.001
ADAM_B1 = 0.9
ADAM_B2 = 0.999
ADAM_EPS = 1e-08
ADAM_WD = 0.01
ADAM_STEP = 10

GELU_C0 = 0.7978845608028654
GELU_C1 = 0.044715


ANY_SPEC = pl.BlockSpec(memory_space=pl.ANY)


def _tile(n, cap, mult=LANE):
    if n <= cap:
        return n
    best = None
    for d in range(mult, cap + 1, mult):
        if n % d == 0:
            best = d
    assert best is not None, (n, cap, mult)
    return best


def _params(*sem):
    return pltpu.CompilerParams(dimension_semantics=sem, vmem_limit_bytes=VMEM_LIMIT)


def _delayed(x, halo, rows, n_shifts):
    for r in range(min(SUBLANE, n_shifts)):
        xr = x if r == 0 else pltpu.roll(x, r, 0)
        for s in range(r, n_shifts, SUBLANE):
            yield s, xr[halo - (s - r):halo - (s - r) + rows]


def _advanced(x, rows, n_shifts):
    for r in range(min(SUBLANE, n_shifts)):
        xr = x if r == 0 else pltpu.roll(x, x.shape[0] - r, 0)
        for s in range(r, n_shifts, SUBLANE):
            yield s, xr[s - r:s - r + rows]


def _sig(x):
    return 1.0 / (1.0 + jnp.exp(-x))


def _bs(shape, imap):
    return pl.BlockSpec(shape, imap)


def _mxu_tile(n, cap):
    if n <= cap:
        return n
    best = {mult: max((d for d in range(mult, cap + 1, mult) if n % d == 0), default=0) for mult in (2 * LANE, LANE)}
    assert best[LANE] > 0, (n, cap)
    return best[2 * LANE] if 2 * best[2 * LANE] >= best[LANE] else best[LANE]


def _matmul(a, b, mode, name, res=None, out_dtype=F32, after=None, b_window=None):
    b_row0, b_rows = b_window if b_window is not None else (0, b.shape[0])
    if mode == "tn":
        k_dim, m_dim = a.shape
        k2, n_dim = b_rows, b.shape[1]
    elif mode == "nn":
        m_dim, k_dim = a.shape
        k2, n_dim = b_rows, b.shape[1]
    else:
        m_dim, k_dim = a.shape
        n_dim, k2 = b_rows, b.shape[1]
    assert k_dim == k2, (name, a.shape, b.shape)
    size = lambda t: jnp.dtype(t).itemsize
    tm = _mxu_tile(m_dim, 2816 if mode == "tn" else 1024)
    tn = _mxu_tile(n_dim, 2816)
    fixed = tm * tn * (2 * size(out_dtype) + (2 * size(res.dtype) if res is not None else 0) + 4)
    for cap in (2816, 2048, 1792, 1024, 512):
        tk = _mxu_tile(k_dim, cap)
        if fixed + 2 * tk * (tm * size(a.dtype) + tn * size(b.dtype)) <= VMEM_LIMIT - 8 * 1024 * 1024:
            break
    nk = k_dim // tk
    use_acc = nk > 1 and out_dtype != F32
    if mode == "tn":
        a_spec, ca = _bs((tk, tm), lambda i, j, k: (k, i)), 0
    else:
        a_spec, ca = _bs((tm, tk), lambda i, j, k: (i, k)), 1
    if mode == "nt":
        assert b_row0 % tn == 0
        b_spec, cb = _bs((tn, tk), lambda i, j, k: (j + b_row0 // tn, k)), 1
    else:
        assert b_row0 % tk == 0
        b_spec, cb = _bs((tk, tn), lambda i, j, k: (k + b_row0 // tk, j)), 0
    dims = (((ca,), (cb,)), ((), ()))
    o_spec = _bs((tm, tn), lambda i, j, k: (i, j))
    has_res = res is not None

    def body(*refs):
        a_ref, b_ref = refs[:2]
        r_ref = refs[2] if has_res else None
        o_ref = refs[n_in]
        k = pl.program_id(2)
        part = lax.dot_general(a_ref[...].astype(BF16), b_ref[...].astype(BF16), dims,
                               preferred_element_type=F32)
        if nk == 1:
            if has_res:
                part = part + r_ref[...].astype(F32)
            o_ref[...] = part.astype(out_dtype)
            return
        acc = refs[-1] if use_acc else o_ref

        @pl.when(k == 0)
        def _():
            acc[...] = part + r_ref[...].astype(F32) if has_res else part

        @pl.when(k > 0)
        def _():
            acc[...] += part

        if use_acc:
            @pl.when(k == nk - 1)
            def _():
                o_ref[...] = acc[...].astype(out_dtype)

    in_specs = [a_spec, b_spec] + ([o_spec] if has_res else [])
    args = (a, b) + ((res,) if has_res else ())
    if after is not None:
        in_specs.append(ANY_SPEC)
        args += (after,)
    n_in = len(args)
    return pl.pallas_call(
        body, out_shape=jax.ShapeDtypeStruct((m_dim, n_dim), out_dtype),
        grid=(m_dim // tm, n_dim // tn, nk), in_specs=in_specs, out_specs=o_spec,
        scratch_shapes=[pltpu.VMEM((tm, tn), F32)] if use_acc else [], name=name,
        compiler_params=_params("parallel", "parallel", "arbitrary"))(*args)


def _grouped(a, w, mode, name, out_dtype=F32):
    t_dim = a.shape[0]
    g_dim, r_dim, c_dim = w.shape
    ka, no = (c_dim, r_dim) if mode == "nt" else (r_dim, c_dim)
    tm = _tile(t_dim, 2048)
    dims = (((1,), (1 if mode == "nt" else 0,)), ((), ()))

    def body(a_ref, w_ref, o_ref):
        o_ref[...] = lax.dot_general(a_ref[...].astype(BF16), w_ref[...].astype(BF16), dims,
                                     preferred_element_type=F32).astype(out_dtype)

    return pl.pallas_call(
        body, out_shape=jax.ShapeDtypeStruct((t_dim, g_dim * no), out_dtype),
        grid=(t_dim // tm, g_dim),
        in_specs=[_bs((tm, ka), lambda i, g: (i, g)), _bs((None, r_dim, c_dim), lambda i, g: (g, 0, 0))],
        out_specs=_bs((tm, no), lambda i, g: (i, g)), name=name,
        compiler_params=_params("parallel", "parallel"))(a, w)


def _grouped_tn(a, b, g_dim, name):
    t_dim = a.shape[0]
    ra = a.shape[1] // g_dim
    cb = b.shape[1] // g_dim
    tm = _tile(t_dim, 2048)
    nt = t_dim // tm

    def body(a_ref, b_ref, o_ref):
        part = lax.dot_general(a_ref[...].astype(BF16), b_ref[...].astype(BF16), (((0,), (0,)), ((), ())),
                               preferred_element_type=F32)

        @pl.when(pl.program_id(1) == 0)
        def _():
            o_ref[...] = part

        @pl.when(pl.program_id(1) > 0)
        def _():
            o_ref[...] += part

    return pl.pallas_call(
        body, out_shape=jax.ShapeDtypeStruct((g_dim, ra, cb), F32), grid=(g_dim, nt),
        in_specs=[_bs((tm, ra), lambda g, i: (i, g)), _bs((tm, cb), lambda g, i: (i, g))],
        out_specs=_bs((None, ra, cb), lambda g, i: (g, 0, 0)), name=name,
        compiler_params=_params("parallel", "arbitrary"))(a, b)


def _rmsnorm_fwd(x, g, name, after=None):
    t_dim, d = x.shape
    tm = _tile(t_dim, 1024)

    def body(x_ref, g_ref, *rest):
        o_ref = rest[-1]
        xv = x_ref[...]
        r = lax.rsqrt(jnp.mean(xv * xv, axis=-1, keepdims=True) + EPS)
        o_ref[...] = (xv * r * g_ref[...]).astype(BF16)

    return pl.pallas_call(
        body, out_shape=jax.ShapeDtypeStruct((t_dim, d), BF16), grid=(t_dim // tm,),
        in_specs=[_bs((tm, d), lambda i: (i, 0)), _bs((1, d), lambda i: (0, 0))] + ([ANY_SPEC] if after is not None else []),
        out_specs=_bs((tm, d), lambda i: (i, 0)), name=name,
        compiler_params=_params("parallel"))(x, g, *([after] if after is not None else []))


def _rmsnorm_bwd(x, g, dh, dx_in, name):
    t_dim, d = x.shape
    tm = _tile(t_dim, 512)
    has_in = dx_in is not None

    def body(*refs):
        if has_in:
            x_ref, g_ref, dh_ref, di_ref, dx_ref, dxb_ref, dg_ref = refs
        else:
            x_ref, g_ref, dh_ref, dx_ref, dxb_ref, dg_ref = refs
        xv = x_ref[...]
        r = lax.rsqrt(jnp.mean(xv * xv, axis=-1, keepdims=True) + EPS)
        xh = xv * r
        dhv = dh_ref[...].astype(F32)
        dxh = dhv * g_ref[...]
        dx = r * (dxh - xh * jnp.mean(dxh * xh, axis=-1, keepdims=True))
        if has_in:
            dx = dx + di_ref[...]
        dx_ref[...] = dx
        dxb_ref[...] = dx.astype(BF16)
        part = jnp.sum(dhv * xh, axis=0, keepdims=True)

        @pl.when(pl.program_id(0) == 0)
        def _():
            dg_ref[...] = part

        @pl.when(pl.program_id(0) > 0)
        def _():
            dg_ref[...] += part

    row = _bs((tm, d), lambda i: (i, 0))
    vec = _bs((1, d), lambda i: (0, 0))
    args = (x, g, dh) + ((dx_in,) if has_in else ())
    return pl.pallas_call(
        body, out_shape=(jax.ShapeDtypeStruct((t_dim, d), F32), jax.ShapeDtypeStruct((t_dim, d), BF16),
                         jax.ShapeDtypeStruct((1, d), F32)),
        grid=(t_dim // tm,), in_specs=[row, vec, row] + ([row] if has_in else []),
        out_specs=(row, row, vec), name=name, compiler_params=_params("arbitrary"))(*args)


def _matmul_rmsnorm_bwd(a, b, mode, x, g, dx_in, name, res=None, b_window=None):
    pieces = a if isinstance(a, tuple) else (a,)
    n_p = len(pieces)
    b_row0, b_rows = b_window if b_window is not None else (0, b.shape[0])
    m_dim, k_piece = pieces[0].shape
    assert all(t.shape == pieces[0].shape for t in pieces)
    k_dim = n_p * k_piece
    d = x.shape[1]
    assert (b_rows, b.shape[1]) == ((k_dim, d) if mode == "nn" else (d, k_dim)), (name, pieces[0].shape, b.shape)
    tm = _mxu_tile(m_dim, 512)
    tk = _mxu_tile(k_piece, 1792)
    nkp = k_piece // tk
    nk = n_p * nkp
    has_res = res is not None
    if mode == "nt":
        assert b_row0 == 0
        b_spec, cb = _bs((d, tk), lambda i, k: (0, k)), 1
    else:
        assert b_row0 % tk == 0
        b_spec, cb = _bs((tk, d), lambda i, k: (k + b_row0 // tk, 0)), 0
    dims = (((1,), (cb,)), ((), ()))

    def body(*refs):
        b_ref = refs[n_p]
        r_ref = refs[n_p + 1] if has_res else None
        x_ref, g_ref, di_ref, dx_ref, dxb_ref, dg_ref = refs[n_p + 1 + has_res:n_p + 7 + has_res]
        i, k = pl.program_id(0), pl.program_id(1)

        def finish(dhv):
            if has_res:
                dhv = dhv + r_ref[...].astype(F32)
            xv = x_ref[...]
            r = lax.rsqrt(jnp.mean(xv * xv, axis=-1, keepdims=True) + EPS)
            xh = xv * r
            dxh = dhv * g_ref[...]
            dx = r * (dxh - xh * jnp.mean(dxh * xh, axis=-1, keepdims=True)) + di_ref[...]
            dx_ref[...] = dx
            dxb_ref[...] = dx.astype(BF16)
            dg_part = jnp.sum(dhv * xh, axis=0, keepdims=True)

            @pl.when(i == 0)
            def _():
                dg_ref[...] = dg_part

            @pl.when(i > 0)
            def _():
                dg_ref[...] += dg_part

        def step(a_ref):
            part = lax.dot_general(a_ref[...].astype(BF16), b_ref[...].astype(BF16), dims, preferred_element_type=F32)
            if nk == 1:
                finish(part)
                return
            acc = refs[-1]

            @pl.when(k == 0)
            def _():
                acc[...] = part

            @pl.when(jnp.logical_and(k > 0, k < nk - 1))
            def _():
                acc[...] += part

            @pl.when(k == nk - 1)
            def _():
                finish(acc[...] + part)

        if n_p == 1:
            step(refs[0])
        else:
            for q in range(n_p):
                pl.when(jnp.logical_and(k >= q * nkp, k < (q + 1) * nkp))(functools.partial(step, refs[q]))

    row = _bs((tm, d), lambda i, k: (i, 0))
    vec = _bs((1, d), lambda i, k: (0, 0))
    a_specs = [_bs((tm, tk), lambda i, k, q=q: (i, jnp.clip(k - q * nkp, 0, nkp - 1))) for q in range(n_p)]
    in_specs = a_specs + [b_spec] + ([row] if has_res else []) + [row, vec, row]
    args = pieces + (b,) + ((res,) if has_res else ()) + (x, g, dx_in)
    return pl.pallas_call(
        body, out_shape=(jax.ShapeDtypeStruct((m_dim, d), F32), jax.ShapeDtypeStruct((m_dim, d), BF16),
                         jax.ShapeDtypeStruct((1, d), F32)),
        grid=(m_dim // tm, nk), in_specs=in_specs, out_specs=(row, row, vec),
        scratch_shapes=[pltpu.VMEM((tm, d), F32)] if nk > 1 else [], name=name,
        compiler_params=_params("arbitrary", "arbitrary"))(*args)


def _loss_head(x, g, tgt, name):
    t_dim, d = x.shape
    tm = _tile(t_dim, 512)

    def body(x_ref, g_ref, t_ref, dx_ref, dxb_ref, dg_ref, loss_ref):
        xv = x_ref[...]
        gv = g_ref[...]
        r = lax.rsqrt(jnp.mean(xv * xv, axis=-1, keepdims=True) + EPS)
        xh = xv * r
        err = xh * gv - t_ref[...]
        dy = err * (1.0 / d)
        dxh = dy * gv
        dx = r * (dxh - xh * jnp.mean(dxh * xh, axis=-1, keepdims=True))
        dx_ref[...] = dx
        dxb_ref[...] = dx.astype(BF16)
        dg_part = jnp.sum(dy * xh, axis=0, keepdims=True)
        loss_part = jnp.full((1, LANE), 0.5 * jnp.sum(jnp.mean(err * err, axis=-1, keepdims=True)), F32)

        @pl.when(pl.program_id(0) == 0)
        def _():
            dg_ref[...] = dg_part
            loss_ref[...] = loss_part

        @pl.when(pl.program_id(0) > 0)
        def _():
            dg_ref[...] += dg_part
            loss_ref[...] += loss_part

    row = _bs((tm, d), lambda i: (i, 0))
    vec = _bs((1, d), lambda i: (0, 0))
    return pl.pallas_call(
        body, out_shape=(jax.ShapeDtypeStruct((t_dim, d), F32), jax.ShapeDtypeStruct((t_dim, d), BF16),
                         jax.ShapeDtypeStruct((1, d), F32), jax.ShapeDtypeStruct((1, LANE), F32)),
        grid=(t_dim // tm,), in_specs=[row, vec, row],
        out_specs=(row, row, vec, _bs((1, LANE), lambda i: (0, 0))), name=name,
        compiler_params=_params("arbitrary"))(x, g, tgt)


def _glu_conv_fwd(proj, dw_w, dw_b, n_batch, seq, name):
    kk, cc = dw_w.shape
    nj = cc // LANE
    ch = min(256, seq)

    def body(a_ref, gl_ref, w_ref, b_ref, o_ref, pad):
        pad[0:CONV_HALO, :] = jnp.zeros((CONV_HALO, LANE), F32)
        pad[CONV_HALO:CONV_HALO + seq, :] = a_ref[...].astype(F32) * _sig(gl_ref[...].astype(F32))
        for c0 in range(0, seq, ch):
            acc = jnp.broadcast_to(b_ref[...], (ch, LANE))
            for k in range(kk):
                acc = acc + w_ref[k:k + 1, :] * pad[pl.ds(c0 + CONV_HALO - (kk - 1) + k, ch), :]
            o_ref[c0:c0 + ch, :] = acc

    return pl.pallas_call(
        body, out_shape=jax.ShapeDtypeStruct((n_batch * seq, cc), F32), grid=(n_batch, nj),
        in_specs=[_bs((seq, LANE), lambda b, j: (b, j)), _bs((seq, LANE), lambda b, j: (b, nj + j)),
                  _bs((kk, LANE), lambda b, j: (0, j)), _bs((1, LANE), lambda b, j: (0, j))],
        out_specs=_bs((seq, LANE), lambda b, j: (b, j)),
        scratch_shapes=[pltpu.VMEM((seq + CONV_HALO, LANE), F32)], name=name,
        compiler_params=_params("parallel", "parallel"))(proj, proj, dw_w, dw_b)


def _glu_conv_bwd(proj, dw_w, dy1, n_batch, seq, name):
    kk, cc = dw_w.shape
    nj = cc // LANE
    ch = min(256, seq)

    def body(a_ref, gl_ref, dy_ref, w_ref, da_ref, dgl_ref, dw_ref, db_ref, padf, padb):
        first = pl.program_id(1) == 0
        padf[0:CONV_HALO, :] = jnp.zeros((CONV_HALO, LANE), F32)
        padf[CONV_HALO:CONV_HALO + seq, :] = a_ref[...].astype(F32) * _sig(gl_ref[...].astype(F32))
        padb[0:seq, :] = dy_ref[...]
        padb[seq:seq + CONV_HALO, :] = jnp.zeros((CONV_HALO, LANE), F32)

        @pl.when(first)
        def _():
            dw_ref[...] = jnp.zeros((kk, LANE), F32)
            db_ref[...] = jnp.zeros((1, LANE), F32)

        dws = [jnp.zeros((1, LANE), F32) for _ in range(kk)]
        for c0 in range(0, seq, ch):
            acc = jnp.zeros((ch, LANE), F32)
            y0 = padf[CONV_HALO + c0:CONV_HALO + c0 + ch, :]
            for k in range(kk):
                win = padb[pl.ds(c0 + (kk - 1) - k, ch), :]
                acc = acc + w_ref[k:k + 1, :] * win
                dws[k] = dws[k] + jnp.sum(win * y0, axis=0, keepdims=True)
            sg = _sig(gl_ref[c0:c0 + ch, :].astype(F32))
            da_ref[c0:c0 + ch, :] = (acc * sg).astype(BF16)
            dgl_ref[c0:c0 + ch, :] = (acc * a_ref[c0:c0 + ch, :].astype(F32) * sg * (1.0 - sg)).astype(BF16)
        for k in range(kk):
            dw_ref[k:k + 1, :] += dws[k]
        db_ref[...] += jnp.sum(dy_ref[...], axis=0, keepdims=True)

    tok = _bs((seq, LANE), lambda j, b: (b, j))
    t_dim = n_batch * seq
    return pl.pallas_call(
        body, out_shape=(jax.ShapeDtypeStruct((t_dim, cc), BF16), jax.ShapeDtypeStruct((t_dim, cc), BF16),
                         jax.ShapeDtypeStruct((kk, cc), F32), jax.ShapeDtypeStruct((1, cc), F32)),
        grid=(nj, n_batch),
        in_specs=[tok, _bs((seq, LANE), lambda j, b: (b, nj + j)), tok, _bs((kk, LANE), lambda j, b: (0, j))],
        out_specs=(tok, tok, _bs((kk, LANE), lambda j, b: (0, j)), _bs((1, LANE), lambda j, b: (0, j))),
        scratch_shapes=[pltpu.VMEM((seq + CONV_HALO, LANE), F32), pltpu.VMEM((seq + CONV_HALO, LANE), F32)],
        name=name, compiler_params=_params("parallel", "arbitrary"))(proj, proj, dy1, dw_w)


def _ln_silu_fwd(y1, g, b, name):
    t_dim, c = y1.shape
    tm = _tile(t_dim, 1024)

    def body(y_ref, g_ref, b_ref, o_ref):
        yv = y_ref[...]
        xc = yv - jnp.mean(yv, axis=-1, keepdims=True)
        rstd = lax.rsqrt(jnp.mean(xc * xc, axis=-1, keepdims=True) + EPS)
        y2 = xc * rstd * g_ref[...] + b_ref[...]
        o_ref[...] = (y2 * _sig(y2)).astype(BF16)

    row = _bs((tm, c), lambda i: (i, 0))
    vec = _bs((1, c), lambda i: (0, 0))
    return pl.pallas_call(
        body, out_shape=jax.ShapeDtypeStruct((t_dim, c), BF16), grid=(t_dim // tm,),
        in_specs=[row, vec, vec], out_specs=row, name=name, compiler_params=_params("parallel"))(y1, g, b)


def _ln_silu_bwd(y1, g, b, dy3, name):
    t_dim, c = y1.shape
    tm = _tile(t_dim, 1024)

    def body(y_ref, g_ref, b_ref, d_ref, dy_ref, dg_ref, db_ref):
        yv = y_ref[...]
        gv = g_ref[...]
        xc = yv - jnp.mean(yv, axis=-1, keepdims=True)
        rstd = lax.rsqrt(jnp.mean(xc * xc, axis=-1, keepdims=True) + EPS)
        yh = xc * rstd
        y2 = yh * gv + b_ref[...]
        s = _sig(y2)
        dy2 = d_ref[...].astype(F32) * (s * (1.0 + y2 * (1.0 - s)))
        dyh = dy2 * gv
        dy_ref[...] = rstd * (dyh - jnp.mean(dyh, axis=-1, keepdims=True)
                              - yh * jnp.mean(dyh * yh, axis=-1, keepdims=True))
        dg_part = jnp.sum(dy2 * yh, axis=0, keepdims=True)
        db_part = jnp.sum(dy2, axis=0, keepdims=True)

        @pl.when(pl.program_id(0) == 0)
        def _():
            dg_ref[...] = dg_part
            db_ref[...] = db_part

        @pl.when(pl.program_id(0) > 0)
        def _():
            dg_ref[...] += dg_part
            db_ref[...] += db_part

    row = _bs((tm, c), lambda i: (i, 0))
    vec = _bs((1, c), lambda i: (0, 0))
    return pl.pallas_call(
        body, out_shape=(jax.ShapeDtypeStruct((t_dim, c), F32), jax.ShapeDtypeStruct((1, c), F32),
                         jax.ShapeDtypeStruct((1, c), F32)),
        grid=(t_dim // tm,), in_specs=[row, vec, vec, row], out_specs=(row, vec, vec), name=name,
        compiler_params=_params("arbitrary"))(y1, g, b, dy3)


def _pool_fwd(proj, col0, n_groups, n_batch, seq, name):
    ch = min(256, seq)

    def body(u_ref, o_ref, pad):
        w = lax.shift_left(jnp.int32(2), pl.program_id(1))
        pad[0:POOL_HALO, :] = jnp.zeros((POOL_HALO, LANE), F32)
        pad[POOL_HALO:POOL_HALO + seq, :] = u_ref[...].astype(F32)
        for c0 in range(0, seq, ch):
            acc = jnp.zeros((ch, LANE), F32)
            for j in range(POOL_WINDOW_MAX):
                acc = acc + jnp.where(j < w, 1.0, 0.0).astype(F32) * pad[pl.ds(c0 + POOL_HALO - j, ch), :]
            t = c0 + lax.broadcasted_iota(jnp.int32, (ch, LANE), 0)
            cnt = jnp.minimum(t + 1, w).astype(F32)
            o_ref[c0:c0 + ch, :] = (acc / cnt - pad[POOL_HALO + c0:POOL_HALO + c0 + ch, :]).astype(BF16)

    return pl.pallas_call(
        body, out_shape=jax.ShapeDtypeStruct((n_batch * seq, n_groups * LANE), BF16), grid=(n_batch, n_groups),
        in_specs=[_bs((seq, LANE), lambda b, g: (b, col0 + g))], out_specs=_bs((seq, LANE), lambda b, g: (b, g)),
        scratch_shapes=[pltpu.VMEM((seq + POOL_HALO, LANE), F32)], name=name,
        compiler_params=_params("parallel", "parallel"))(proj)


def _pool_bwd(dzp, n_groups, n_batch, seq, name):
    ch = min(256, seq)

    def body(d_ref, o_ref, pad):
        w = lax.shift_left(jnp.int32(2), pl.program_id(1))
        for c0 in range(0, seq, ch):
            t = c0 + lax.broadcasted_iota(jnp.int32, (ch, LANE), 0)
            cnt = jnp.minimum(t + 1, w).astype(F32)
            pad[c0:c0 + ch, :] = d_ref[c0:c0 + ch, :].astype(F32) / cnt
        pad[seq:seq + POOL_HALO, :] = jnp.zeros((POOL_HALO, LANE), F32)
        for c0 in range(0, seq, ch):
            acc = jnp.zeros((ch, LANE), F32)
            for j in range(POOL_WINDOW_MAX):
                acc = acc + jnp.where(j < w, 1.0, 0.0).astype(F32) * pad[pl.ds(c0 + j, ch), :]
            o_ref[c0:c0 + ch, :] = (acc - d_ref[c0:c0 + ch, :].astype(F32)).astype(BF16)

    tok = _bs((seq, LANE), lambda b, g: (b, g))
    return pl.pallas_call(
        body, out_shape=jax.ShapeDtypeStruct((n_batch * seq, n_groups * LANE), BF16), grid=(n_batch, n_groups),
        in_specs=[tok], out_specs=tok, scratch_shapes=[pltpu.VMEM((seq + POOL_HALO, LANE), F32)], name=name,
        compiler_params=_params("parallel", "parallel"))(dzp)


def _merge_fwd(proj, col0, yc, yp, scale, name):
    t_dim, d = yc.shape
    half = d // 2
    tm = _tile(t_dim, 1024)
    c0 = col0 // half

    def body(gc_ref, gp_ref, yc_ref, yp_ref, s_ref, o_ref):
        f32 = lambda r: r[...].astype(F32)
        o_ref[...] = (_sig(f32(gc_ref)) * f32(yc_ref) + _sig(f32(gp_ref)) * (f32(yp_ref) * s_ref[...])).astype(BF16)

    blk = _bs((tm, half), lambda i, j: (i, j))
    return pl.pallas_call(
        body, out_shape=jax.ShapeDtypeStruct((t_dim, d), BF16), grid=(t_dim // tm, 2),
        in_specs=[_bs((tm, half), lambda i, j: (i, c0 + j)), _bs((tm, half), lambda i, j: (i, c0 + 2 + j)),
                  blk, blk, _bs((1, half), lambda i, j: (0, j))],
        out_specs=blk, name=name, compiler_params=_params("parallel", "parallel"))(proj, proj, yc, yp, scale)


def _merge_bwd(proj, col0, yc, yp, scale, dm, name):
    t_dim, d = yc.shape
    half = d // 2
    tm = _tile(t_dim, 1024)
    c0 = col0 // half

    def body(gc_ref, gp_ref, yc_ref, yp_ref, s_ref, dm_ref, dgc_ref, dgp_ref, dyc_ref, dyp_ref, ds_ref):
        dmv = dm_ref[...].astype(F32)
        sgc = _sig(gc_ref[...].astype(F32))
        sgp = _sig(gp_ref[...].astype(F32))
        sv = s_ref[...]
        ypre = yp_ref[...].astype(F32)
        dgc_ref[...] = (dmv * yc_ref[...].astype(F32) * sgc * (1.0 - sgc)).astype(BF16)
        dgp_ref[...] = (dmv * (ypre * sv) * sgp * (1.0 - sgp)).astype(BF16)
        dyc_ref[...] = (dmv * sgc).astype(BF16)
        dyp = dmv * sgp
        dyp_ref[...] = (dyp * sv).astype(BF16)
        part = jnp.sum(dyp * ypre, axis=0, keepdims=True)

        @pl.when(pl.program_id(1) == 0)
        def _():
            ds_ref[...] = part

        @pl.when(pl.program_id(1) > 0)
        def _():
            ds_ref[...] += part

    blk = _bs((tm, half), lambda j, i: (i, j))
    big = jax.ShapeDtypeStruct((t_dim, d), BF16)
    return pl.pallas_call(
        body, out_shape=(big, big, big, big, jax.ShapeDtypeStruct((1, d), F32)), grid=(2, t_dim // tm),
        in_specs=[_bs((tm, half), lambda j, i: (i, c0 + j)), _bs((tm, half), lambda j, i: (i, c0 + 2 + j)),
                  blk, blk, _bs((1, half), lambda j, i: (0, j)), blk],
        out_specs=(blk, blk, blk, blk, _bs((1, half), lambda j, i: (0, j))), name=name,
        compiler_params=_params("parallel", "arbitrary"))(proj, proj, yc, yp, scale, dm)


def _attn_fwd(q, kv, n_batch, seq, m_len, name):
    d = q.shape[1]
    hd = d // XA_HEADS
    tq = _tile(seq, 1024)
    nq = seq // tq
    scale = hd ** -0.5

    def body(q_ref, k_ref, v_ref, o_ref):
        sc = lax.dot_general(q_ref[...].astype(BF16), k_ref[...].astype(BF16), (((1,), (1,)), ((), ())),
                             preferred_element_type=F32) * scale
        p = jnp.exp(sc - jnp.max(sc, axis=-1, keepdims=True))
        pr = p / jnp.sum(p, axis=-1, keepdims=True)
        o_ref[...] = jnp.dot(pr.astype(BF16), v_ref[...].astype(BF16), preferred_element_type=F32).astype(BF16)

    return pl.pallas_call(
        body, out_shape=jax.ShapeDtypeStruct((n_batch * seq, d), BF16), grid=(n_batch, XA_HEADS, nq),
        in_specs=[_bs((tq, hd), lambda b, h, i: (b * nq + i, h)), _bs((m_len, hd), lambda b, h, i: (b, h)),
                  _bs((m_len, hd), lambda b, h, i: (b, XA_HEADS + h))],
        out_specs=_bs((tq, hd), lambda b, h, i: (b * nq + i, h)), name=name,
        compiler_params=_params("parallel", "parallel", "parallel"))(q, kv, kv)


def _attn_bwd(q, kv, datt, n_batch, seq, m_len, name):
    d = q.shape[1]
    hd = d // XA_HEADS
    tq = _tile(seq, 1024)
    nq = seq // tq
    scale = hd ** -0.5

    def body(q_ref, k_ref, v_ref, do_ref, dq_ref, dk_ref, dv_ref):
        qb = q_ref[...].astype(BF16)
        kb = k_ref[...].astype(BF16)
        vb = v_ref[...].astype(BF16)
        dob = do_ref[...].astype(BF16)
        sc = lax.dot_general(qb, kb, (((1,), (1,)), ((), ())), preferred_element_type=F32) * scale
        p = jnp.exp(sc - jnp.max(sc, axis=-1, keepdims=True))
        pr = p / jnp.sum(p, axis=-1, keepdims=True)
        dpr = lax.dot_general(dob, vb, (((1,), (1,)), ((), ())), preferred_element_type=F32)
        dsc = pr * (dpr - jnp.sum(dpr * pr, axis=-1, keepdims=True)) * scale
        dsb = dsc.astype(BF16)
        dq_ref[...] = jnp.dot(dsb, kb, preferred_element_type=F32).astype(BF16)
        dv_part = lax.dot_general(pr.astype(BF16), dob, (((0,), (0,)), ((), ())), preferred_element_type=F32)
        dk_part = lax.dot_general(dsb, qb, (((0,), (0,)), ((), ())), preferred_element_type=F32)

        @pl.when(pl.program_id(2) == 0)
        def _():
            dk_ref[...] = dk_part
            dv_ref[...] = dv_part

        @pl.when(pl.program_id(2) > 0)
        def _():
            dk_ref[...] += dk_part
            dv_ref[...] += dv_part

    qs = _bs((tq, hd), lambda b, h, i: (b * nq + i, h))
    ks = _bs((m_len, hd), lambda b, h, i: (b, h))
    return pl.pallas_call(
        body, out_shape=(jax.ShapeDtypeStruct((n_batch * seq, d), BF16), jax.ShapeDtypeStruct((n_batch * m_len, d), F32),
                         jax.ShapeDtypeStruct((n_batch * m_len, d), F32)),
        grid=(n_batch, XA_HEADS, nq),
        in_specs=[qs, ks, _bs((m_len, hd), lambda b, h, i: (b, XA_HEADS + h)), qs],
        out_specs=(qs, ks, ks), name=name,
        compiler_params=_params("parallel", "parallel", "arbitrary"))(q, kv, kv, datt)


def _gelu_parts(g):
    th = jnp.tanh(GELU_C0 * (g + GELU_C1 * g * g * g))
    return th, 0.5 * g * (1.0 + th)


def _ffn_act_fwd(up_g, up_v, dw_w, n_batch, seq, name):
    kk, c2 = dw_w.shape
    f_dim = c2 // 2
    wd = FFN_LANE_GROUPS * LANE
    nj = f_dim // wd
    ch = min(128, seq)

    def body(g_ref, v_ref, wg_ref, wv_ref, o_ref, padg, padv):
        for h in range(FFN_LANE_GROUPS):
            lanes = slice(h * LANE, (h + 1) * LANE)
            for pad, src in ((padg, g_ref), (padv, v_ref)):
                pad[h, 0:FFN_HALO, :] = jnp.zeros((FFN_HALO, LANE), F32)
                pad[h, FFN_HALO:FFN_HALO + seq, :] = src[:, lanes].astype(F32)
            for c0 in range(0, seq, ch):
                gate = jnp.zeros((ch, LANE), F32)
                val = jnp.zeros((ch, LANE), F32)
                for k in range(kk):
                    off = c0 + FFN_HALO - (kk - 1) + k
                    gate = gate + wg_ref[k:k + 1, lanes] * padg[h, pl.ds(off, ch), :]
                    val = val + wv_ref[k:k + 1, lanes] * padv[h, pl.ds(off, ch), :]
                o_ref[c0:c0 + ch, lanes] = (_gelu_parts(gate)[1] * val).astype(BF16)

    pad_shape = pltpu.VMEM((FFN_LANE_GROUPS, seq + FFN_HALO, LANE), F32)
    return pl.pallas_call(
        body, out_shape=jax.ShapeDtypeStruct((n_batch * seq, f_dim), BF16), grid=(n_batch, nj),
        in_specs=[_bs((seq, wd), lambda b, j: (b, j)), _bs((seq, wd), lambda b, j: (b, j)),
                  _bs((kk, wd), lambda b, j: (0, j)), _bs((kk, wd), lambda b, j: (0, nj + j))],
        out_specs=_bs((seq, wd), lambda b, j: (b, j)), scratch_shapes=[pad_shape, pad_shape], name=name,
        compiler_params=_params("parallel", "parallel"))(up_g, up_v, dw_w, dw_w)


def _ffn_act_bwd(up_g, up_v, dw_w, dact, n_batch, seq, name):
    kk, c2 = dw_w.shape
    f_dim = c2 // 2
    wd = FFN_LANE_GROUPS * LANE
    nj = f_dim // wd
    ch = min(128, seq)

    def body(g_ref, v_ref, wg_ref, wv_ref, da_ref, dg_ref, dv_ref, dwg_ref, dwv_ref, padg, padv, pbg, pbv):
        @pl.when(pl.program_id(1) == 0)
        def _():
            dwg_ref[...] = jnp.zeros((kk, wd), F32)
            dwv_ref[...] = jnp.zeros((kk, wd), F32)

        for h in range(FFN_LANE_GROUPS):
            lanes = slice(h * LANE, (h + 1) * LANE)
            for pad, src in ((padg, g_ref), (padv, v_ref)):
                pad[h, 0:FFN_HALO, :] = jnp.zeros((FFN_HALO, LANE), F32)
                pad[h, FFN_HALO:FFN_HALO + seq, :] = src[:, lanes].astype(F32)
            for pb in (pbg, pbv):
                pb[h, seq:seq + FFN_HALO, :] = jnp.zeros((FFN_HALO, LANE), F32)
            for c0 in range(0, seq, ch):
                gate = jnp.zeros((ch, LANE), F32)
                val = jnp.zeros((ch, LANE), F32)
                for k in range(kk):
                    off = c0 + FFN_HALO - (kk - 1) + k
                    gate = gate + wg_ref[k:k + 1, lanes] * padg[h, pl.ds(off, ch), :]
                    val = val + wv_ref[k:k + 1, lanes] * padv[h, pl.ds(off, ch), :]
                sq = gate * gate
                th = jnp.tanh(GELU_C0 * gate * (1.0 + GELU_C1 * sq))
                half = 0.5 * th + 0.5
                dgelu = half * (1.0 + gate * (GELU_C0 + 3.0 * GELU_C0 * GELU_C1 * sq) * (1.0 - th))
                dav = da_ref[c0:c0 + ch, lanes].astype(F32)
                pbg[h, c0:c0 + ch, :] = dav * val * dgelu
                pbv[h, c0:c0 + ch, :] = dav * (gate * half)
            for pb, pad, w_ref, d_ref, dw_ref in ((pbg, padg, wg_ref, dg_ref, dwg_ref), (pbv, padv, wv_ref, dv_ref, dwv_ref)):
                for c0 in range(0, seq, ch):
                    acc = jnp.zeros((ch, LANE), F32)
                    for k in range(kk):
                        acc = acc + w_ref[k:k + 1, lanes] * pb[h, pl.ds(c0 + (kk - 1) - k, ch), :]
                    d_ref[c0:c0 + ch, lanes] = acc.astype(BF16)
                for k in range(kk):
                    s = jnp.zeros((1, LANE), F32)
                    for c0 in range(0, seq, ch):
                        s = s + jnp.sum(pb[h, c0:c0 + ch, :] * pad[h, pl.ds(c0 + FFN_HALO - (kk - 1) + k, ch), :],
                                        axis=0, keepdims=True)
                    dw_ref[k:k + 1, lanes] += s

    t_dim = n_batch * seq
    tok = _bs((seq, wd), lambda j, b: (b, j))
    wblk = _bs((kk, wd), lambda j, b: (0, j))
    pad_shape = pltpu.VMEM((FFN_LANE_GROUPS, seq + FFN_HALO, LANE), F32)
    return pl.pallas_call(
        body, out_shape=(jax.ShapeDtypeStruct((t_dim, f_dim), BF16), jax.ShapeDtypeStruct((t_dim, f_dim), BF16),
                         jax.ShapeDtypeStruct((kk, f_dim), F32), jax.ShapeDtypeStruct((kk, f_dim), F32)),
        grid=(nj, n_batch),
        in_specs=[tok, tok, wblk, _bs((kk, wd), lambda j, b: (0, nj + j)), tok],
        out_specs=(tok, tok, wblk, wblk), scratch_shapes=[pad_shape, pad_shape, pad_shape, pad_shape], name=name,
        compiler_params=_params("parallel", "arbitrary"))(up_g, up_v, dw_w, dw_w, dact)


def _sum_rows(parts, out_dtype, name):
    r_dim, c_dim = parts[0].shape
    tr = _tile(r_dim, 1200, SUBLANE)
    n = len(parts)

    def body(*refs):
        acc = refs[0][...].astype(F32)
        for r in refs[1:n]:
            acc = acc + r[...].astype(F32)
        refs[n][...] = acc.astype(out_dtype)

    blk = _bs((tr, c_dim), lambda i: (i, 0))
    return pl.pallas_call(
        body, out_shape=jax.ShapeDtypeStruct((r_dim, c_dim), out_dtype), grid=(r_dim // tr,),
        in_specs=[blk] * n, out_specs=blk, name=name, compiler_params=_params("parallel"))(*parts)


def _adamw(w, g, m, v, name):
    shape = w.shape
    c_dim = shape[-1]
    r_dim = w.size // c_dim
    two_d = lambda t: t.reshape(r_dim, c_dim)
    tr = _tile(r_dim, max(SUBLANE, (256 * 1024) // max(c_dim, LANE) // SUBLANE * SUBLANE), SUBLANE)
    c1 = 1.0 - ADAM_B1 ** ADAM_STEP
    c2 = 1.0 - ADAM_B2 ** ADAM_STEP

    def body(w_ref, g_ref, m_ref, v_ref, d_ref, mo_ref, vo_ref):
        gv = g_ref[...]
        mn = ADAM_B1 * m_ref[...] + (1.0 - ADAM_B1) * gv
        vn = ADAM_B2 * v_ref[...] + (1.0 - ADAM_B2) * (gv * gv)
        mo_ref[...] = mn
        vo_ref[...] = vn
        d_ref[...] = -ADAM_LR * ((mn / c1) / (jnp.sqrt(vn / c2) + ADAM_EPS) + ADAM_WD * w_ref[...])

    blk = _bs((tr, c_dim), lambda i: (i, 0))
    out = jax.ShapeDtypeStruct((r_dim, c_dim), F32)
    d, mo, vo = pl.pallas_call(
        body, out_shape=(out, out, out), grid=(r_dim // tr,), in_specs=[blk] * 4, out_specs=(blk, blk, blk),
        name=name, compiler_params=_params("parallel"))(two_d(w), two_d(g), two_d(m), two_d(v))
    return d.reshape(shape), mo.reshape(shape), vo.reshape(shape)


HBM_SPEC = pl.BlockSpec(memory_space=pltpu.HBM)


def _position():
    return lax.axis_index("x"), lax.axis_index("y"), lax.axis_index("c")


def _all_gather(shard, name):
    def body(x_ref, out_ref, send_sems, recv_sems, local_sem):
        x, y, c = _position()
        me, sibling = (x, y, c), (x, y, 1 - c)
        chips = [(1 - x, y), (x, 1 - y), (1 - x, 1 - y)]

        def rows(px, py, pc):
            return out_ref.at[4 * px + 2 * py + pc]

        def copy(k, block, to, src=None):
            return pltpu.make_async_remote_copy(
                src_ref=rows(*block) if src is None else src, dst_ref=rows(*block),
                send_sem=send_sems.at[k], recv_sem=recv_sems.at[k], device_id=to, device_id_type=MESH)

        mine = pltpu.make_async_copy(x_ref, rows(*me), local_sem)
        mine.start()
        first = [copy(0, me, sibling, src=x_ref)]
        first += [copy(1 + j, me, (*chip, c), src=x_ref) for j, chip in enumerate(chips)]
        for cp in first:
            cp.start()
        passed = [copy(4 + j, (*chip, c), sibling) for j, chip in enumerate(chips)]
        for j, chip in enumerate(chips):
            copy(1 + j, (*chip, c), me).wait_recv()
            passed[j].start()
        copy(0, sibling, me).wait_recv()
        for j, chip in enumerate(chips):
            copy(4 + j, (*chip, 1 - c), me).wait_recv()
        for cp in first + passed:
            cp.wait_send()
        mine.wait()

    return pl.pallas_call(
        body, out_shape=jax.ShapeDtypeStruct((N_DEV,) + shard.shape, shard.dtype),
        in_specs=[HBM_SPEC], out_specs=HBM_SPEC,
        scratch_shapes=[pltpu.SemaphoreType.DMA((7,)), pltpu.SemaphoreType.DMA((7,)), pltpu.SemaphoreType.DMA(())],
        name=name)(shard)


CHIP_RELATIONS = ((0, 0), (1, 0), (0, 1), (1, 1))


def _rs_pair_exchange(g, name):
    _, r_dim, c_dim = g.shape
    n = len(CHIP_RELATIONS)

    def body(g_ref, recv_ref, send_sems, recv_sems):
        x, y, c = _position()
        sibling = (x, y, 1 - c)
        copies = []
        for k, (rx, ry) in enumerate(CHIP_RELATIONS):
            px = x + rx - 2 * x * rx
            py = y + ry - 2 * y * ry
            copies.append(pltpu.make_async_remote_copy(
                src_ref=g_ref.at[4 * px + 2 * py + 1 - c], dst_ref=recv_ref.at[k], send_sem=send_sems.at[k],
                recv_sem=recv_sems.at[k], device_id=sibling, device_id_type=MESH))
        for cp in copies:
            cp.start()
        for cp in copies:
            cp.wait()

    return pl.pallas_call(
        body, out_shape=jax.ShapeDtypeStruct((n, r_dim, c_dim), g.dtype), in_specs=[HBM_SPEC], out_specs=HBM_SPEC,
        scratch_shapes=[pltpu.SemaphoreType.DMA((n,)), pltpu.SemaphoreType.DMA((n,))], name=name)(g)


def _rs_pair_sum(g, recv, name):
    _, r_dim, c_dim = g.shape
    n = len(CHIP_RELATIONS)
    tr = _tile(r_dim, 1200, SUBLANE)
    x, y, c = _position()
    own = jnp.stack([4 * (x + rx - 2 * x * rx) + 2 * (y + ry - 2 * y * ry) + c for rx, ry in CHIP_RELATIONS])

    def body(own_ref, g_ref, r_ref, o_ref):
        o_ref[...] = (g_ref[...].astype(F32) + r_ref[...].astype(F32)).astype(o_ref.dtype)

    blk = _bs((None, tr, c_dim), lambda k, i, own_ref: (k, i, 0))
    return pl.pallas_call(
        body, out_shape=jax.ShapeDtypeStruct((n, r_dim, c_dim), g.dtype),
        grid_spec=pltpu.PrefetchScalarGridSpec(
            num_scalar_prefetch=1, grid=(n, r_dim // tr),
            in_specs=[_bs((None, tr, c_dim), lambda k, i, own_ref: (own_ref[k], i, 0)), blk], out_specs=blk),
        name=name, compiler_params=_params("parallel", "parallel"))(own.astype(jnp.int32), g, recv)


SEM_SPEC = pl.BlockSpec(memory_space=pltpu.SEMAPHORE)
DATAFLOW = pltpu.SideEffectType.DATAFLOW_SIDE_EFFECTING
CHIP_FLIPS = CHIP_RELATIONS[1:]
TOKEN = jax.ShapeDtypeStruct((SUBLANE, LANE), F32)


def _flip(v, r):
    return v + r - 2 * v * r


def _chip_copies(src_ref, src_of, dst_ref, dst_of, send_sems, recv_sems):
    x, y, c = _position()
    me = 4 * x + 2 * y + c
    out = []
    for k, (rx, ry) in enumerate(CHIP_FLIPS):
        px, py = _flip(x, rx), _flip(y, ry)
        peer = 4 * px + 2 * py + c
        out.append(pltpu.make_async_remote_copy(
            src_ref=src_ref.at[src_of(k, me, peer)], dst_ref=dst_ref.at[dst_of(k, me, peer)],
            send_sem=send_sems.at[k], recv_sem=recv_sems.at[k], device_id=(px, py, c), device_id_type=MESH))
    return out


def _device_block(ref, spec, d):
    rows, axis = spec
    return ref.at[pl.ds(d * rows, rows)] if axis == 0 else ref.at[:, pl.ds(d * rows, rows)]


def _ag_chips_start(lands, specs, after, name):
    n = len(lands)
    nf = len(CHIP_FLIPS)

    def body(*refs):
        send_sems, recv_sems, token = refs[n + 1], refs[n + 2], refs[-1]
        x, y, c = _position()
        me = 4 * x + 2 * y + c
        for i, spec in enumerate(specs):
            blk = _device_block(refs[i], spec, me)
            for k, (rx, ry) in enumerate(CHIP_FLIPS):
                pltpu.make_async_remote_copy(
                    src_ref=blk, dst_ref=blk, send_sem=send_sems.at[nf * i + k], recv_sem=recv_sems.at[nf * i + k],
                    device_id=(_flip(x, rx), _flip(y, ry), c), device_id_type=MESH).start()
        token[...] = jnp.zeros(TOKEN.shape, TOKEN.dtype)

    sems = pltpu.SemaphoreType.DMA((nf * n,))
    return pl.pallas_call(
        body, name=name, out_shape=(sems, sems, *[pltpu.HBM(t.shape, t.dtype) for t in lands], TOKEN),
        in_specs=(HBM_SPEC,) * n + (ANY_SPEC,),
        out_specs=(SEM_SPEC, SEM_SPEC) + (HBM_SPEC,) * n + (pl.BlockSpec(memory_space=pltpu.VMEM),),
        input_output_aliases={i: 2 + i for i in range(n)}, compiler_params=pltpu.CompilerParams(has_side_effects=DATAFLOW),
    )(*[pltpu.with_memory_space_constraint(t, pltpu.HBM) for t in lands], after)


def _ag_chips_wait(send_sems, recv_sems, lands, specs, after, name):
    n = len(lands)
    nf = len(CHIP_FLIPS)

    def body(*refs):
        send_sems, recv_sems = refs[n], refs[n + 1]
        x, y, c = _position()
        me = 4 * x + 2 * y + c
        for i, spec in enumerate(specs):
            for k, (rx, ry) in enumerate(CHIP_FLIPS):
                px, py = _flip(x, rx), _flip(y, ry)
                cp = pltpu.make_async_remote_copy(
                    src_ref=_device_block(refs[i], spec, me), dst_ref=_device_block(refs[i], spec, 4 * px + 2 * py + c),
                    send_sem=send_sems.at[nf * i + k], recv_sem=recv_sems.at[nf * i + k],
                    device_id=(px, py, c), device_id_type=MESH)
                cp.wait_send()
                cp.wait_recv()

    return pl.pallas_call(
        body, name=name, out_shape=tuple(pltpu.HBM(t.shape, t.dtype) for t in lands),
        in_specs=(HBM_SPEC,) * n + (SEM_SPEC, SEM_SPEC, ANY_SPEC), out_specs=(HBM_SPEC,) * n,
        input_output_aliases={i: i for i in range(n)}, compiler_params=pltpu.CompilerParams(has_side_effects=DATAFLOW),
    )(*lands, send_sems, recv_sems, after)


def _ag_pair_forward(lands, specs, name):
    n = len(lands)
    nr = len(CHIP_RELATIONS)

    def body(*refs):
        outs, send_sems, recv_sems = refs[n:2 * n], refs[2 * n], refs[2 * n + 1]
        x, y, c = _position()
        copies = []
        for i, spec in enumerate(specs):
            for k, (rx, ry) in enumerate(CHIP_RELATIONS):
                chip = 4 * _flip(x, rx) + 2 * _flip(y, ry)
                held = _device_block(outs[i], spec, chip + c)
                sems = dict(send_sem=send_sems.at[nr * i + k], recv_sem=recv_sems.at[nr * i + k],
                            device_id=(x, y, 1 - c), device_id_type=MESH)
                mine = pltpu.make_async_remote_copy(src_ref=held, dst_ref=held, **sems)
                theirs = pltpu.make_async_remote_copy(src_ref=held, dst_ref=_device_block(outs[i], spec, chip + 1 - c), **sems)
                copies.append((mine, theirs))
        for mine, _ in copies:
            mine.start()
        for mine, theirs in copies:
            mine.wait_send()
            theirs.wait_recv()

    sems = pltpu.SemaphoreType.DMA((nr * n,))
    return pl.pallas_call(
        body, out_shape=tuple(jax.ShapeDtypeStruct(t.shape, t.dtype) for t in lands), in_specs=[HBM_SPEC] * n,
        out_specs=(HBM_SPEC,) * n, input_output_aliases={i: i for i in range(n)}, scratch_shapes=[sems, sems], name=name)(*lands)


def _rs_chips_start(pair, name):
    _, r_dim, c_dim = pair.shape
    n = len(CHIP_FLIPS)

    def body(pair_ref, far_ref, send_sems, recv_sems, pair_thru, far_thru, token):
        for cp in _chip_copies(pair_ref, lambda k, me, peer: k + 1, far_ref, lambda k, me, peer: k, send_sems, recv_sems):
            cp.start()
        token[...] = jnp.zeros(TOKEN.shape, TOKEN.dtype)

    far = lax.empty((n, r_dim, c_dim), pair.dtype)
    return pl.pallas_call(
        body, name=name,
        out_shape=(pltpu.SemaphoreType.DMA((n,)), pltpu.SemaphoreType.DMA((n,)), pltpu.HBM(pair.shape, pair.dtype),
                   pltpu.HBM(far.shape, far.dtype), TOKEN),
        in_specs=(HBM_SPEC, HBM_SPEC),
        out_specs=(SEM_SPEC, SEM_SPEC, HBM_SPEC, HBM_SPEC, pl.BlockSpec(memory_space=pltpu.VMEM)),
        input_output_aliases={0: 2, 1: 3}, compiler_params=pltpu.CompilerParams(has_side_effects=DATAFLOW),
    )(pltpu.with_memory_space_constraint(pair, pltpu.HBM), pltpu.with_memory_space_constraint(far, pltpu.HBM))


def _rs_chips_wait(send_sems, recv_sems, pair, far, after, name):
    def body(pair_ref, far_ref, send_sems, recv_sems, after_ref, pair_out, far_out):
        for cp in _chip_copies(pair_ref, lambda k, me, peer: k + 1, far_ref, lambda k, me, peer: k, send_sems, recv_sems):
            cp.wait_send()
            cp.wait_recv()

    return pl.pallas_call(
        body, name=name, out_shape=(pltpu.HBM(pair.shape, pair.dtype), pltpu.HBM(far.shape, far.dtype)),
        in_specs=(HBM_SPEC, HBM_SPEC, SEM_SPEC, SEM_SPEC, ANY_SPEC),
        out_specs=(HBM_SPEC, HBM_SPEC), input_output_aliases={0: 0, 1: 1},
        compiler_params=pltpu.CompilerParams(has_side_effects=DATAFLOW),
    )(pair, far, send_sems, recv_sems, after)


def _rs_final_sum(pair, far, name):
    _, r_dim, c_dim = pair.shape
    tr = _tile(r_dim, 1200, SUBLANE)

    def body(p_ref, f0_ref, f1_ref, f2_ref, o_ref):
        o_ref[...] = ((p_ref[...].astype(F32) + f0_ref[...].astype(F32)) + f1_ref[...].astype(F32)) + f2_ref[...].astype(F32)

    def slot(k):
        return _bs((None, tr, c_dim), lambda i: (k, i, 0))

    return pl.pallas_call(
        body, out_shape=jax.ShapeDtypeStruct((r_dim, c_dim), F32), grid=(r_dim // tr,),
        in_specs=[slot(0), slot(0), slot(1), slot(2)], out_specs=_bs((tr, c_dim), lambda i: (i, 0)), name=name,
        compiler_params=_params("parallel"))(pair, far, far, far)


def _reduce_scatter_begin(g, name):
    recv = _rs_pair_exchange(g, name + "_pair")
    pair = _rs_pair_sum(g, recv, name + "_pairsum")
    return _rs_chips_start(pair, name + "_chips_start")


def _reduce_scatter_end(state, after, name):
    send_sems, recv_sems, pair, far, _ = state
    pair, far = _rs_chips_wait(send_sems, recv_sems, pair, far, after, name + "_chips_wait")
    return _rs_final_sum(pair, far, name + "_sum")


MATRICES = (("w_in", True), ("w_out", False), ("w_q", False), ("w_kv", True), ("w_o", False), ("w_up", True),
            ("w_down", False), ("w_conv_out", True), ("w_pool_grp", True))
MIX_NAMES = ("w_in", "w_conv_out", "w_pool_grp", "w_out")
REST_NAMES = ("w_q", "w_kv", "w_o", "w_up", "w_down")


def _parts(layer):
    return (("mix", MIX_NAMES), ("rest", REST_NAMES)) if layer == 0 else (("all", MIX_NAMES + REST_NAMES),)


def _to_rows(name, transposed, w, d_model):
    if name == "w_pool_grp":
        w = jnp.swapaxes(w, 1, 2)
    elif transposed:
        w = w.T
    return w.reshape(-1, d_model)


def _stored_form(name, transposed, rows, shard_shape):
    if name == "w_pool_grp":
        g, i, o = shard_shape
        return rows.reshape(g, o, i)
    if transposed:
        return rows.reshape(shard_shape[1], shard_shape[0])
    return rows.reshape(shard_shape)


def _scatter_blocks(name, full, shard_shape, d_model, n_dev=N_DEV):
    if name == "w_pool_grp":
        g, i, o = shard_shape
        return jnp.swapaxes(full.reshape(g, n_dev, o, i), 0, 1).reshape(n_dev, -1, d_model)
    return full.reshape(n_dev, -1, d_model)


def kernel(x, mem, mix_norm_g, w_in, conv_dw_w, conv_dw_b, conv_ln_g, conv_ln_b, w_conv_out, w_pool_grp, pool_scale, w_out, xattn_norm_g, mem_norm_g, w_q, w_kv, w_o, ffn_norm_g, w_up, ffn_dw_w, w_down, final_norm_g, loss_target, m_mix_norm_g, m_w_in, m_conv_dw_w, m_conv_dw_b, m_conv_ln_g, m_conv_ln_b, m_w_conv_out, m_w_pool_grp, m_pool_scale, m_w_out, m_xattn_norm_g, m_mem_norm_g, m_w_q, m_w_kv, m_w_o, m_ffn_norm_g, m_w_up, m_ffn_dw_w, m_w_down, m_final_norm_g, v_mix_norm_g, v_w_in, v_conv_dw_w, v_conv_dw_b, v_conv_ln_g, v_conv_ln_b, v_w_conv_out, v_w_pool_grp, v_pool_scale, v_w_out, v_xattn_norm_g, v_mem_norm_g, v_w_q, v_w_kv, v_w_o, v_ffn_norm_g, v_w_up, v_ffn_dw_w, v_w_down, v_final_norm_g):
    p = dict(locals())
    weight_names = ["mix_norm_g", "w_in", "conv_dw_w", "conv_dw_b", "conv_ln_g", "conv_ln_b", "w_conv_out",
                    "w_pool_grp", "pool_scale", "w_out", "xattn_norm_g", "mem_norm_g", "w_q", "w_kv", "w_o",
                    "ffn_norm_g", "w_up", "ffn_dw_w", "w_down", "final_norm_g"]
    n_batch, seq, d_model = x.shape
    m_len = mem.shape[1]
    depth = w_in.shape[0]
    assert depth == 2, "the exchange schedule below is written for two layers"
    t_dim = n_batch * seq
    c_conv = conv_dw_b.shape[1]
    n_groups = w_pool_grp.shape[1]
    assert w_pool_grp.shape[2] == LANE and c_conv % LANE == 0 and n_groups * LANE == c_conv
    gate_col0 = 2 * c_conv + n_groups * LANE
    pool_col0 = (2 * c_conv) // LANE

    dev = 4 * lax.axis_index("x") + 2 * lax.axis_index("y") + lax.axis_index("c")
    filt = jnp.concatenate([conv_dw_w.reshape(-1), ffn_dw_w.reshape(-1)])
    filt_rows = filt.reshape(-1, d_model)
    transposed = dict(MATRICES)
    layout = {part: [(name, transposed[name], _to_rows(name, transposed[name], p[name][0], d_model).shape[0])
                     for name in names] for l in range(depth) for part, names in _parts(l)}
    part_of = {(l, name): part for l in range(depth) for part, names in _parts(l) for name in names}

    def landing(name, shard):
        if name == "w_pool_grp":
            block, axis = jnp.swapaxes(shard, 1, 2), 1
        elif name == "filt":
            block, axis = shard, 0
        else:
            block, axis = (shard.T if transposed[name] else shard), 0
        block = block if name == "filt" else block.astype(BF16)
        rows = block.shape[axis]
        shape = block.shape[:axis] + (N_DEV * rows,) + block.shape[axis + 1:]
        start = (0,) * axis + (dev * rows,) + (0,) * (block.ndim - axis - 1)
        return lax.dynamic_update_slice(lax.empty(shape, block.dtype), block, start), (rows, axis)

    ag_state = {}
    after = filt_rows
    for l in range(depth):
        for part, names in _parts(l):
            items = [(name, p[name][l]) for name in names]
            if (l, part) == (0, part_of[(0, "w_in")]):
                items.append(("filt", filt_rows))
            lands, specs = zip(*[landing(name, shard) for name, shard in items])
            out = _ag_chips_start(lands, specs, after, f"ag{l}{part}_chips_start")
            ag_state[(l, part)] = ([name for name, _ in items], specs, out)
            after = out[-1]
    all_started = after

    full = [dict() for _ in range(depth)]

    def ensure(l, name, after):
        if name in full[l]:
            return
        part = part_of[(l, name)]
        names, specs, out = ag_state[(l, part)]
        lands = _ag_chips_wait(out[0], out[1], out[2:-1], specs, after, f"ag{l}{part}_chips_wait")
        lands = _ag_pair_forward(lands, specs, f"ag{l}{part}_pair_forward")
        full[l].update(zip(names, lands))

    vec = lambda a: a.reshape(1, -1)
    x2d = x.reshape(t_dim, d_model)
    mem2d = mem.reshape(n_batch * m_len, d_model)
    mem_n = _rmsnorm_fwd(mem2d, vec(mem_norm_g), "mem_norm", after=all_started)
    h_first = _rmsnorm_fwd(x2d, vec(mix_norm_g[0]), "mix_norm_l0", after=mem_n)
    ensure(0, "w_in", h_first)
    filt_all = full[0]["filt"].reshape(N_DEV, -1)
    n_cw = conv_dw_w.size
    kc, cs = conv_dw_w.shape[1:]
    kf, fs = ffn_dw_w.shape[1:]
    conv_w_full = jnp.moveaxis(filt_all[:, :n_cw].reshape(N_DEV, depth, kc, cs), 0, 2).reshape(depth, kc, N_DEV * cs)
    ffn_w_full = jnp.moveaxis(filt_all[:, n_cw:].reshape(N_DEV, depth, kf, fs), 0, 2).reshape(depth, kf, N_DEV * fs)

    saved = []
    xc = x2d
    for l in range(depth):
        ensure(l, "w_in", xc)
        wl = full[l]
        s = {"x0": xc}
        s["h"] = h_first if l == 0 else _rmsnorm_fwd(xc, vec(mix_norm_g[l]), f"mix_norm_l{l}")
        s["proj"] = _matmul(s["h"], wl["w_in"], "nt", f"in_proj_l{l}", out_dtype=BF16)
        s["y1"] = _glu_conv_fwd(s["proj"], conv_w_full[l], vec(conv_dw_b[l]), n_batch, seq, f"glu_conv_l{l}")
        s["y3"] = _ln_silu_fwd(s["y1"], vec(conv_ln_g[l]), vec(conv_ln_b[l]), f"ln_silu_l{l}")
        s["yc"] = _matmul(s["y3"], wl["w_conv_out"], "nt", f"conv_out_l{l}", out_dtype=BF16)
        s["zp"] = _pool_fwd(s["proj"], pool_col0, n_groups, n_batch, seq, f"pool_l{l}")
        s["yp"] = _grouped(s["zp"], wl["w_pool_grp"], "nt", f"pool_proj_l{l}", out_dtype=BF16)
        s["merged"] = _merge_fwd(s["proj"], gate_col0, s["yc"], s["yp"], vec(pool_scale[l]), f"merge_l{l}")
        s["x1"] = _matmul(s["merged"], wl["w_out"], "nn", f"mix_out_l{l}", res=xc)
        ensure(l, "w_q", s["x1"])
        half_up = wl["w_up"].shape[0] // 2
        up_gate, up_val = (0, half_up), (half_up, half_up)
        s["hq"] = _rmsnorm_fwd(s["x1"], vec(xattn_norm_g[l]), f"xattn_norm_l{l}")
        s["q"] = _matmul(s["hq"], wl["w_q"], "nn", f"q_proj_l{l}", out_dtype=BF16)
        s["kv"] = _matmul(mem_n, wl["w_kv"], "nt", f"kv_proj_l{l}", out_dtype=BF16)
        s["att"] = _attn_fwd(s["q"], s["kv"], n_batch, seq, m_len, f"attn_l{l}")
        s["x2"] = _matmul(s["att"], wl["w_o"], "nn", f"attn_out_l{l}", res=s["x1"])
        s["hf"] = _rmsnorm_fwd(s["x2"], vec(ffn_norm_g[l]), f"ffn_norm_l{l}")
        s["up_g"] = _matmul(s["hf"], wl["w_up"], "nt", f"up_proj_gate_l{l}", out_dtype=BF16, b_window=up_gate)
        s["up_v"] = _matmul(s["hf"], wl["w_up"], "nt", f"up_proj_val_l{l}", out_dtype=BF16, b_window=up_val)
        s["act"] = _ffn_act_fwd(s["up_g"], s["up_v"], ffn_w_full[l], n_batch, seq, f"ffn_act_l{l}")
        xc = _matmul(s["act"], wl["w_down"], "nn", f"down_proj_l{l}", res=s["x2"])
        saved.append(s)

    dx, dxb, dg_final, loss_part = _loss_head(xc, vec(final_norm_g), loss_target.reshape(t_dim, d_model), "loss_head")

    small = {"final_norm_g": dg_final.reshape(-1)}
    big = [dict() for _ in range(depth)]
    rs_state = {}
    rs_after = loss_part

    def rs_begin(l, part):
        pack = lax.empty((N_DEV, sum(nrows for _, _, nrows in layout[part]), d_model), BF16)
        row0 = 0
        for name, _, nrows in layout[part]:
            pieces = big[l][name] if isinstance(big[l][name], tuple) else (big[l][name],)
            d0 = 0
            for piece in pieces:
                blocks = _scatter_blocks(name, piece, p[name].shape[1:], d_model, N_DEV // len(pieces)).astype(BF16)
                pack = lax.dynamic_update_slice(pack, blocks, (d0, row0, 0))
                d0 += blocks.shape[0]
            row0 += nrows
        rs_state[(l, part)] = _reduce_scatter_begin(pack, f"rs{l}{part}")
        return rs_state[(l, part)][4]

    dmem_n = None
    for l in reversed(range(depth)):
        wl, s = full[l], saved[l]
        sm = {}
        dact = _matmul(dxb, wl["w_down"], "nt", f"d_act_l{l}", out_dtype=BF16, after=rs_after)
        big[l]["w_down"] = _matmul(s["act"], dxb, "tn", f"d_w_down_l{l}", out_dtype=BF16)
        dup_g, dup_v, dwf_g, dwf_v = _ffn_act_bwd(s["up_g"], s["up_v"], ffn_w_full[l], dact, n_batch, seq,
                                                  f"ffn_act_bwd_l{l}")
        sm["ffn_dw_w"] = jnp.concatenate([dwf_g, dwf_v], axis=1)
        dx, dxb, dg = _matmul_rmsnorm_bwd((dup_g, dup_v), wl["w_up"], "nn", s["x2"], vec(ffn_norm_g[l]), dx,
                                          f"d_hf_ffn_norm_bwd_l{l}")
        big[l]["w_up"] = (_matmul(dup_g, s["hf"], "tn", f"d_w_up_gate_l{l}", out_dtype=BF16),
                          _matmul(dup_v, s["hf"], "tn", f"d_w_up_val_l{l}", out_dtype=BF16))
        sm["ffn_norm_g"] = dg
        datt = _matmul(dxb, wl["w_o"], "nt", f"d_att_l{l}", out_dtype=BF16, after=rs_after)
        big[l]["w_o"] = _matmul(s["att"], dxb, "tn", f"d_w_o_l{l}", out_dtype=BF16)
        dq, dk, dv = _attn_bwd(s["q"], s["kv"], datt, n_batch, seq, m_len, f"attn_bwd_l{l}")
        dkv = jnp.concatenate([dk, dv], axis=1)
        big[l]["w_kv"] = _matmul(dkv, mem_n, "tn", f"d_w_kv_l{l}", out_dtype=BF16)
        dmem_n = _matmul(dkv, wl["w_kv"], "nn", f"d_mem_l{l}", res=dmem_n)
        big[l]["w_q"] = _matmul(s["hq"], dq, "tn", f"d_w_q_l{l}", out_dtype=BF16)
        dx, dxb, dg = _matmul_rmsnorm_bwd(dq, wl["w_q"], "nt", s["x1"], vec(xattn_norm_g[l]), dx,
                                          f"d_hq_xattn_norm_bwd_l{l}")
        sm["xattn_norm_g"] = dg
        if part_of[(l, "w_q")] != part_of[(l, "w_in")]:
            rs_after = rs_begin(l, part_of[(l, "w_q")])
        dmerged = _matmul(dxb, wl["w_out"], "nt", f"d_merged_l{l}", out_dtype=BF16, after=rs_after)
        big[l]["w_out"] = _matmul(s["merged"], dxb, "tn", f"d_w_out_l{l}", out_dtype=BF16)
        dgc, dgp, dyc, dyp, dscale = _merge_bwd(s["proj"], gate_col0, s["yc"], s["yp"], vec(pool_scale[l]), dmerged,
                                                f"merge_bwd_l{l}")
        sm["pool_scale"] = dscale
        dzp = _grouped(dyp, wl["w_pool_grp"], "nn", f"d_zp_l{l}", out_dtype=BF16)
        big[l]["w_pool_grp"] = _grouped_tn(dyp, s["zp"], n_groups, f"d_w_pool_l{l}")
        du = _pool_bwd(dzp, n_groups, n_batch, seq, f"pool_bwd_l{l}")
        dy3 = _matmul(dyc, wl["w_conv_out"], "nn", f"d_y3_l{l}", out_dtype=BF16)
        big[l]["w_conv_out"] = _matmul(dyc, s["y3"], "tn", f"d_w_conv_out_l{l}", out_dtype=BF16)
        dy1, dlg, dlb = _ln_silu_bwd(s["y1"], vec(conv_ln_g[l]), vec(conv_ln_b[l]), dy3, f"ln_silu_bwd_l{l}")
        sm["conv_ln_g"], sm["conv_ln_b"] = dlg, dlb
        da, dgl, dcw, dcb = _glu_conv_bwd(s["proj"], conv_w_full[l], dy1, n_batch, seq, f"glu_conv_bwd_l{l}")
        sm["conv_dw_w"], sm["conv_dw_b"] = dcw, dcb
        dproj = jnp.concatenate([da, dgl, du, dgc, dgp], axis=1)
        big[l]["w_in"] = _matmul(dproj, s["h"], "tn", f"d_w_in_l{l}", out_dtype=BF16)
        dx, dxb, dg = _matmul_rmsnorm_bwd(dproj, wl["w_in"], "nn", s["x0"], vec(mix_norm_g[l]), dx,
                                          f"d_h_mix_norm_bwd_l{l}")
        sm["mix_norm_g"] = dg
        for k, val in sm.items():
            small[(l, k)] = val.reshape(-1)
        rs_after = rs_begin(l, part_of[(l, "w_in")])
    _, _, dg_mem = _rmsnorm_bwd(mem2d, vec(mem_norm_g), dmem_n, None, "mem_norm_bwd")
    small["mem_norm_g"] = dg_mem.reshape(-1)
    small["loss"] = loss_part.reshape(-1)

    grads = {}
    per_layer = {name: [None] * depth for name, _ in MATRICES}
    for l in reversed(range(depth)):
        for part, _ in reversed(_parts(l)):
            mat_grads = _reduce_scatter_end(rs_state[(l, part)], rs_after, f"rs{l}{part}")
            row0 = 0
            for name, tr, nrows in layout[part]:
                per_layer[name][l] = _stored_form(name, tr, mat_grads[row0:row0 + nrows], p[name].shape[1:])
                row0 += nrows
    flip = lambda t: jnp.swapaxes(t, -1, -2)
    stored_grads = {name: jnp.stack(per_layer[name]) for name, _ in MATRICES}
    for name, tr in MATRICES:
        grads[name] = flip(stored_grads[name]) if tr else stored_grads[name]

    keys = list(small.keys())
    flat = jnp.concatenate([small[k] for k in keys])
    n_small = flat.shape[0]
    rows_small = -(-n_small // (SUBLANE * d_model)) * SUBLANE
    flat = jnp.pad(flat, (0, rows_small * d_model - n_small)).reshape(rows_small, d_model)
    every = _all_gather(flat, "small_all_gather")
    total = _sum_rows([every[i] for i in range(N_DEV)], F32, "small_sum").reshape(-1)
    off = 0
    red = {}
    for k in keys:
        red[k] = total[off:off + small[k].shape[0]]
        off += small[k].shape[0]
    loss = red["loss"][0]
    for name in ("mix_norm_g", "conv_dw_b", "conv_ln_g", "conv_ln_b", "pool_scale", "xattn_norm_g", "ffn_norm_g"):
        grads[name] = jnp.stack([red[(l, name)] for l in range(depth)])
    grads["conv_dw_w"] = jnp.stack([
        lax.dynamic_slice_in_dim(red[(l, "conv_dw_w")].reshape(kc, N_DEV * cs), dev * cs, cs, axis=1)
        for l in range(depth)])
    grads["ffn_dw_w"] = jnp.stack([
        lax.dynamic_slice_in_dim(red[(l, "ffn_dw_w")].reshape(kf, N_DEV * fs), dev * fs, fs, axis=1)
        for l in range(depth)])
    grads["mem_norm_g"] = red["mem_norm_g"]
    grads["final_norm_g"] = red["final_norm_g"]

    deltas, new_m, new_v = {}, {}, {}
    for name in weight_names:
        if transposed.get(name, False):
            out = _adamw(flip(p[name]), stored_grads[name], flip(p["m_" + name]), flip(p["v_" + name]), f"adamw_{name}")
            deltas[name], new_m[name], new_v[name] = (flip(t) for t in out)
        else:
            deltas[name], new_m[name], new_v[name] = _adamw(p[name], grads[name], p["m_" + name], p["v_" + name],
                                                            f"adamw_{name}")
    grad_x = dx.reshape(n_batch, seq, d_model)
    return (loss, grad_x, *[grads[n] for n in weight_names], *[deltas[n] for n in weight_names],
            *[new_m[n] for n in weight_names], *[new_v[n] for n in weight_names])
```

```python
import functools

import jax
import jax.numpy as jnp
from jax import lax
from jax.experimental import pallas as pl
from jax.experimental.pallas import tpu as pltpu

F32 = jnp.float32
BF16 = jnp.bfloat16
MESH = pl.DeviceIdType.MESH

N_DEV = 8
EPS = 1e-6
V7X_VMEM_BYTES = 64 * 1024 * 1024
VMEM_LIMIT = (V7X_VMEM_BYTES * 3) // 4
LANE = 128
SUBLANE = 8

CONV_HALO = 32
POOL_HALO = 16
FFN_HALO = 8
FFN_LANE_GROUPS = 2
POOL_WINDOW_MAX = 16
XA_HEADS = 4

ADAM_LR = 0.001
ADAM_B1 = 0.9
ADAM_B2 = 0.999
ADAM_EPS = 1e-08
ADAM_WD = 0.01
ADAM_STEP = 10

GELU_C0 = 0.7978845608028654
GELU_C1 = 0.044715


ANY_SPEC = pl.BlockSpec(memory_space=pl.ANY)


def _tile(n, cap, mult=LANE):
    if n <= cap:
        return n
    best = None
    for d in range(mult, cap + 1, mult):
        if n % d == 0:
            best = d
    assert best is not None, (n, cap, mult)
    return best


def _params(*sem):
    return pltpu.CompilerParams(dimension_semantics=sem, vmem_limit_bytes=VMEM_LIMIT)


def _delayed(x, halo, rows, n_shifts):
    for r in range(min(SUBLANE, n_shifts)):
        xr = x if r == 0 else pltpu.roll(x, r, 0)
        for s in range(r, n_shifts, SUBLANE):
            yield s, xr[halo - (s - r):halo - (s - r) + rows]


def _advanced(x, rows, n_shifts):
    for r in range(min(SUBLANE, n_shifts)):
        xr = x if r == 0 else pltpu.roll(x, x.shape[0] - r, 0)
        for s in range(r, n_shifts, SUBLANE):
            yield s, xr[s - r:s - r + rows]


def _sig(x):
    return 1.0 / (1.0 + jnp.exp(-x))


def _bs(shape, imap):
    return pl.BlockSpec(shape, imap)


def _mxu_tile(n, cap):
    if n <= cap:
        return n
    best = {mult: max((d for d in range(mult, cap + 1, mult) if n % d == 0), default=0) for mult in (2 * LANE, LANE)}
    assert best[LANE] > 0, (n, cap)
    return best[2 * LANE] if 2 * best[2 * LANE] >= best[LANE] else best[LANE]


def _matmul(a, b, mode, name, res=None, out_dtype=F32, after=None, b_window=None):
    b_row0, b_rows = b_window if b_window is not None else (0, b.shape[0])
    if mode == "tn":
        k_dim, m_dim = a.shape
        k2, n_dim = b_rows, b.shape[1]
    elif mode == "nn":
        m_dim, k_dim = a.shape
        k2, n_dim = b_rows, b.shape[1]
    else:
        m_dim, k_dim = a.shape
        n_dim, k2 = b_rows, b.shape[1]
    assert k_dim == k2, (name, a.shape, b.shape)
    size = lambda t: jnp.dtype(t).itemsize
    tm = _mxu_tile(m_dim, 2816 if mode == "tn" else 1024)
    tn = _mxu_tile(n_dim, 2816)
    fixed = tm * tn * (2 * size(out_dtype) + (2 * size(res.dtype) if res is not None else 0) + 4)
    for cap in (2816, 2048, 1792, 1024, 512):
        tk = _mxu_tile(k_dim, cap)
        if fixed + 2 * tk * (tm * size(a.dtype) + tn * size(b.dtype)) <= VMEM_LIMIT - 8 * 1024 * 1024:
            break
    nk = k_dim // tk
    use_acc = nk > 1 and out_dtype != F32
    if mode == "tn":
        a_spec, ca = _bs((tk, tm), lambda i, j, k: (k, i)), 0
    else:
        a_spec, ca = _bs((tm, tk), lambda i, j, k: (i, k)), 1
    if mode == "nt":
        assert b_row0 % tn == 0
        b_spec, cb = _bs((tn, tk), lambda i, j, k: (j + b_row0 // tn, k)), 1
    else:
        assert b_row0 % tk == 0
        b_spec, cb = _bs((tk, tn), lambda i, j, k: (k + b_row0 // tk, j)), 0
    dims = (((ca,), (cb,)), ((), ()))
    o_spec = _bs((tm, tn), lambda i, j, k: (i, j))
    has_res = res is not None

    def body(*refs):
        a_ref, b_ref = refs[:2]
        r_ref = refs[2] if has_res else None
        o_ref = refs[n_in]
        k = pl.program_id(2)
        part = lax.dot_general(a_ref[...].astype(BF16), b_ref[...].astype(BF16), dims,
                               preferred_element_type=F32)
        if nk == 1:
            if has_res:
                part = part + r_ref[...].astype(F32)
            o_ref[...] = part.astype(out_dtype)
            return
        acc = refs[-1] if use_acc else o_ref

        @pl.when(k == 0)
        def _():
            acc[...] = part + r_ref[...].astype(F32) if has_res else part

        @pl.when(k > 0)
        def _():
            acc[...] += part

        if use_acc:
            @pl.when(k == nk - 1)
            def _():
                o_ref[...] = acc[...].astype(out_dtype)

    in_specs = [a_spec, b_spec] + ([o_spec] if has_res else [])
    args = (a, b) + ((res,) if has_res else ())
    if after is not None:
        in_specs.append(ANY_SPEC)
        args += (after,)
    n_in = len(args)
    return pl.pallas_call(
        body, out_shape=jax.ShapeDtypeStruct((m_dim, n_dim), out_dtype),
        grid=(m_dim // tm, n_dim // tn, nk), in_specs=in_specs, out_specs=o_spec,
        scratch_shapes=[pltpu.VMEM((tm, tn), F32)] if use_acc else [], name=name,
        compiler_params=_params("parallel", "parallel", "arbitrary"))(*args)


def _grouped(a, w, mode, name, out_dtype=F32):
    t_dim = a.shape[0]
    g_dim, r_dim, c_dim = w.shape
    ka, no = (c_dim, r_dim) if mode == "nt" else (r_dim, c_dim)
    tm = _tile(t_dim, 2048)
    dims = (((1,), (1 if mode == "nt" else 0,)), ((), ()))

    def body(a_ref, w_ref, o_ref):
        o_ref[...] = lax.dot_general(a_ref[...].astype(BF16), w_ref[...].astype(BF16), dims,
                                     preferred_element_type=F32).astype(out_dtype)

    return pl.pallas_call(
        body, out_shape=jax.ShapeDtypeStruct((t_dim, g_dim * no), out_dtype),
        grid=(t_dim // tm, g_dim),
        in_specs=[_bs((tm, ka), lambda i, g: (i, g)), _bs((None, r_dim, c_dim), lambda i, g: (g, 0, 0))],
        out_specs=_bs((tm, no), lambda i, g: (i, g)), name=name,
        compiler_params=_params("parallel", "parallel"))(a, w)


def _grouped_tn(a, b, g_dim, name):
    t_dim = a.shape[0]
    ra = a.shape[1] // g_dim
    cb = b.shape[1] // g_dim
    tm = _tile(t_dim, 2048)
    nt = t_dim // tm

    def body(a_ref, b_ref, o_ref):
        part = lax.dot_general(a_ref[...].astype(BF16), b_ref[...].astype(BF16), (((0,), (0,)), ((), ())),
                               preferred_element_type=F32)

        @pl.when(pl.program_id(1) == 0)
        def _():
            o_ref[...] = part

        @pl.when(pl.program_id(1) > 0)
        def _():
            o_ref[...] += part

    return pl.pallas_call(
        body, out_shape=jax.ShapeDtypeStruct((g_dim, ra, cb), F32), grid=(g_dim, nt),
        in_specs=[_bs((tm, ra), lambda g, i: (i, g)), _bs((tm, cb), lambda g, i: (i, g))],
        out_specs=_bs((None, ra, cb), lambda g, i: (g, 0, 0)), name=name,
        compiler_params=_params("parallel", "arbitrary"))(a, b)


def _rmsnorm_fwd(x, g, name, after=None):
    t_dim, d = x.shape
    tm = _tile(t_dim, 1024)

    def body(x_ref, g_ref, *rest):
        o_ref = rest[-1]
        xv = x_ref[...]
        r = lax.rsqrt(jnp.mean(xv * xv, axis=-1, keepdims=True) + EPS)
        o_ref[...] = (xv * r * g_ref[...]).astype(BF16)

    return pl.pallas_call(
        body, out_shape=jax.ShapeDtypeStruct((t_dim, d), BF16), grid=(t_dim // tm,),
        in_specs=[_bs((tm, d), lambda i: (i, 0)), _bs((1, d), lambda i: (0, 0))] + ([ANY_SPEC] if after is not None else []),
        out_specs=_bs((tm, d), lambda i: (i, 0)), name=name,
        compiler_params=_params("parallel"))(x, g, *([after] if after is not None else []))


def _rmsnorm_bwd(x, g, dh, dx_in, name):
    t_dim, d = x.shape
    tm = _tile(t_dim, 512)
    has_in = dx_in is not None

    def body(*refs):
        if has_in:
            x_ref, g_ref, dh_ref, di_ref, dx_ref, dxb_ref, dg_ref = refs
        else:
            x_ref, g_ref, dh_ref, dx_ref, dxb_ref, dg_ref = refs
        xv = x_ref[...]
        r = lax.rsqrt(jnp.mean(xv * xv, axis=-1, keepdims=True) + EPS)
        xh = xv * r
        dhv = dh_ref[...].astype(F32)
        dxh = dhv * g_ref[...]
        dx = r * (dxh - xh * jnp.mean(dxh * xh, axis=-1, keepdims=True))
        if has_in:
            dx = dx + di_ref[...]
        dx_ref[...] = dx
        dxb_ref[...] = dx.astype(BF16)
        part = jnp.sum(dhv * xh, axis=0, keepdims=True)

        @pl.when(pl.program_id(0) == 0)
        def _():
            dg_ref[...] = part

        @pl.when(pl.program_id(0) > 0)
        def _():
            dg_ref[...] += part

    row = _bs((tm, d), lambda i: (i, 0))
    vec = _bs((1, d), lambda i: (0, 0))
    args = (x, g, dh) + ((dx_in,) if has_in else ())
    return pl.pallas_call(
        body, out_shape=(jax.ShapeDtypeStruct((t_dim, d), F32), jax.ShapeDtypeStruct((t_dim, d), BF16),
                         jax.ShapeDtypeStruct((1, d), F32)),
        grid=(t_dim // tm,), in_specs=[row, vec, row] + ([row] if has_in else []),
        out_specs=(row, row, vec), name=name, compiler_params=_params("arbitrary"))(*args)


def _matmul_rmsnorm_bwd(a, b, mode, x, g, dx_in, name, res=None, b_window=None):
    pieces = a if isinstance(a, tuple) else (a,)
    n_p = len(pieces)
    b_row0, b_rows = b_window if b_window is not None else (0, b.shape[0])
    m_dim, k_piece = pieces[0].shape
    assert all(t.shape == pieces[0].shape for t in pieces)
    k_dim = n_p * k_piece
    d = x.shape[1]
    assert (b_rows, b.shape[1]) == ((k_dim, d) if mode == "nn" else (d, k_dim)), (name, pieces[0].shape, b.shape)
    tm = _mxu_tile(m_dim, 512)
    tk = _mxu_tile(k_piece, 1792)
    nkp = k_piece // tk
    nk = n_p * nkp
    has_res = res is not None
    if mode == "nt":
        assert b_row0 == 0
        b_spec, cb = _bs((d, tk), lambda i, k: (0, k)), 1
    else:
        assert b_row0 % tk == 0
        b_spec, cb = _bs((tk, d), lambda i, k: (k + b_row0 // tk, 0)), 0
    dims = (((1,), (cb,)), ((), ()))

    def body(*refs):
        b_ref = refs[n_p]
        r_ref = refs[n_p + 1] if has_res else None
        x_ref, g_ref, di_ref, dx_ref, dxb_ref, dg_ref = refs[n_p + 1 + has_res:n_p + 7 + has_res]
        i, k = pl.program_id(0), pl.program_id(1)

        def finish(dhv):
            if has_res:
                dhv = dhv + r_ref[...].astype(F32)
            xv = x_ref[...]
            r = lax.rsqrt(jnp.mean(xv * xv, axis=-1, keepdims=True) + EPS)
            xh = xv * r
            dxh = dhv * g_ref[...]
            dx = r * (dxh - xh * jnp.mean(dxh * xh, axis=-1, keepdims=True)) + di_ref[...]
            dx_ref[...] = dx
            dxb_ref[...] = dx.astype(BF16)
            dg_part = jnp.sum(dhv * xh, axis=0, keepdims=True)

            @pl.when(i == 0)
            def _():
                dg_ref[...] = dg_part

            @pl.when(i > 0)
            def _():
                dg_ref[...] += dg_part

        def step(a_ref):
            part = lax.dot_general(a_ref[...].astype(BF16), b_ref[...].astype(BF16), dims, preferred_element_type=F32)
            if nk == 1:
                finish(part)
                return
            acc = refs[-1]

            @pl.when(k == 0)
            def _():
                acc[...] = part

            @pl.when(jnp.logical_and(k > 0, k < nk - 1))
            def _():
                acc[...] += part

            @pl.when(k == nk - 1)
            def _():
                finish(acc[...] + part)

        if n_p == 1:
            step(refs[0])
        else:
            for q in range(n_p):
                pl.when(jnp.logical_and(k >= q * nkp, k < (q + 1) * nkp))(functools.partial(step, refs[q]))

    row = _bs((tm, d), lambda i, k: (i, 0))
    vec = _bs((1, d), lambda i, k: (0, 0))
    a_specs = [_bs((tm, tk), lambda i, k, q=q: (i, jnp.clip(k - q * nkp, 0, nkp - 1))) for q in range(n_p)]
    in_specs = a_specs + [b_spec] + ([row] if has_res else []) + [row, vec, row]
    args = pieces + (b,) + ((res,) if has_res else ()) + (x, g, dx_in)
    return pl.pallas_call(
        body, out_shape=(jax.ShapeDtypeStruct((m_dim, d), F32), jax.ShapeDtypeStruct((m_dim, d), BF16),
                         jax.ShapeDtypeStruct((1, d), F32)),
        grid=(m_dim // tm, nk), in_specs=in_specs, out_specs=(row, row, vec),
        scratch_shapes=[pltpu.VMEM((tm, d), F32)] if nk > 1 else [], name=name,
        compiler_params=_params("arbitrary", "arbitrary"))(*args)


def _loss_head(x, g, tgt, name):
    t_dim, d = x.shape
    tm = _tile(t_dim, 512)

    def body(x_ref, g_ref, t_ref, dx_ref, dxb_ref, dg_ref, loss_ref):
        xv = x_ref[...]
        gv = g_ref[...]
        r = lax.rsqrt(jnp.mean(xv * xv, axis=-1, keepdims=True) + EPS)
        xh = xv * r
        err = xh * gv - t_ref[...]
        dy = err * (1.0 / d)
        dxh = dy * gv
        dx = r * (dxh - xh * jnp.mean(dxh * xh, axis=-1, keepdims=True))
        dx_ref[...] = dx
        dxb_ref[...] = dx.astype(BF16)
        dg_part = jnp.sum(dy * xh, axis=0, keepdims=True)
        loss_part = jnp.full((1, LANE), 0.5 * jnp.sum(jnp.mean(err * err, axis=-1, keepdims=True)), F32)

        @pl.when(pl.program_id(0) == 0)
        def _():
            dg_ref[...] = dg_part
            loss_ref[...] = loss_part

        @pl.when(pl.program_id(0) > 0)
        def _():
            dg_ref[...] += dg_part
            loss_ref[...] += loss_part

    row = _bs((tm, d), lambda i: (i, 0))
    vec = _bs((1, d), lambda i: (0, 0))
    return pl.pallas_call(
        body, out_shape=(jax.ShapeDtypeStruct((t_dim, d), F32), jax.ShapeDtypeStruct((t_dim, d), BF16),
                         jax.ShapeDtypeStruct((1, d), F32), jax.ShapeDtypeStruct((1, LANE), F32)),
        grid=(t_dim // tm,), in_specs=[row, vec, row],
        out_specs=(row, row, vec, _bs((1, LANE), lambda i: (0, 0))), name=name,
        compiler_params=_params("arbitrary"))(x, g, tgt)


def _glu_conv_fwd(proj, dw_w, dw_b, n_batch, seq, name):
    kk, cc = dw_w.shape
    nj = cc // LANE
    ch = min(256, seq)

    def body(a_ref, gl_ref, w_ref, b_ref, o_ref, pad):
        pad[0:CONV_HALO, :] = jnp.zeros((CONV_HALO, LANE), F32)
        pad[CONV_HALO:CONV_HALO + seq, :] = a_ref[...].astype(F32) * _sig(gl_ref[...].astype(F32))
        for c0 in range(0, seq, ch):
            acc = jnp.broadcast_to(b_ref[...], (ch, LANE))
            for k in range(kk):
                acc = acc + w_ref[k:k + 1, :] * pad[pl.ds(c0 + CONV_HALO - (kk - 1) + k, ch), :]
            o_ref[c0:c0 + ch, :] = acc

    return pl.pallas_call(
        body, out_shape=jax.ShapeDtypeStruct((n_batch * seq, cc), F32), grid=(n_batch, nj),
        in_specs=[_bs((seq, LANE), lambda b, j: (b, j)), _bs((seq, LANE), lambda b, j: (b, nj + j)),
                  _bs((kk, LANE), lambda b, j: (0, j)), _bs((1, LANE), lambda b, j: (0, j))],
        out_specs=_bs((seq, LANE), lambda b, j: (b, j)),
        scratch_shapes=[pltpu.VMEM((seq + CONV_HALO, LANE), F32)], name=name,
        compiler_params=_params("parallel", "parallel"))(proj, proj, dw_w, dw_b)


def _glu_conv_bwd(proj, dw_w, dy1, n_batch, seq, name):
    kk, cc = dw_w.shape
    nj = cc // LANE
    ch = min(256, seq)

    def body(a_ref, gl_ref, dy_ref, w_ref, da_ref, dgl_ref, dw_ref, db_ref, padf, padb):
        first = pl.program_id(1) == 0
        padf[0:CONV_HALO, :] = jnp.zeros((CONV_HALO, LANE), F32)
        padf[CONV_HALO:CONV_HALO + seq, :] = a_ref[...].astype(F32) * _sig(gl_ref[...].astype(F32))
        padb[0:seq, :] = dy_ref[...]
        padb[seq:seq + CONV_HALO, :] = jnp.zeros((CONV_HALO, LANE), F32)

        @pl.when(first)
        def _():
            dw_ref[...] = jnp.zeros((kk, LANE), F32)
            db_ref[...] = jnp.zeros((1, LANE), F32)

        dws = [jnp.zeros((1, LANE), F32) for _ in range(kk)]
        for c0 in range(0, seq, ch):
            acc = jnp.zeros((ch, LANE), F32)
            y0 = padf[CONV_HALO + c0:CONV_HALO + c0 + ch, :]
            for k in range(kk):
                win = padb[pl.ds(c0 + (kk - 1) - k, ch), :]
                acc = acc + w_ref[k:k + 1, :] * win
                dws[k] = dws[k] + jnp.sum(win * y0, axis=0, keepdims=True)
            sg = _sig(gl_ref[c0:c0 + ch, :].astype(F32))
            da_ref[c0:c0 + ch, :] = (acc * sg).astype(BF16)
            dgl_ref[c0:c0 + ch, :] = (acc * a_ref[c0:c0 + ch, :].astype(F32) * sg * (1.0 - sg)).astype(BF16)
        for k in range(kk):
            dw_ref[k:k + 1, :] += dws[k]
        db_ref[...] += jnp.sum(dy_ref[...], axis=0, keepdims=True)

    tok = _bs((seq, LANE), lambda j, b: (b, j))
    t_dim = n_batch * seq
    return pl.pallas_call(
        body, out_shape=(jax.ShapeDtypeStruct((t_dim, cc), BF16), jax.ShapeDtypeStruct((t_dim, cc), BF16),
                         jax.ShapeDtypeStruct((kk, cc), F32), jax.ShapeDtypeStruct((1, cc), F32)),
        grid=(nj, n_batch),
        in_specs=[tok, _bs((seq, LANE), lambda j, b: (b, nj + j)), tok, _bs((kk, LANE), lambda j, b: (0, j))],
        out_specs=(tok, tok, _bs((kk, LANE), lambda j, b: (0, j)), _bs((1, LANE), lambda j, b: (0, j))),
        scratch_shapes=[pltpu.VMEM((seq + CONV_HALO, LANE), F32), pltpu.VMEM((seq + CONV_HALO, LANE), F32)],
        name=name, compiler_params=_params("parallel", "arbitrary"))(proj, proj, dy1, dw_w)


def _ln_silu_fwd(y1, g, b, name):
    t_dim, c = y1.shape
    tm = _tile(t_dim, 1024)

    def body(y_ref, g_ref, b_ref, o_ref):
        yv = y_ref[...]
        xc = yv - jnp.mean(yv, axis=-1, keepdims=True)
        rstd = lax.rsqrt(jnp.mean(xc * xc, axis=-1, keepdims=True) + EPS)
        y2 = xc * rstd * g_ref[...] + b_ref[...]
        o_ref[...] = (y2 * _sig(y2)).astype(BF16)

    row = _bs((tm, c), lambda i: (i, 0))
    vec = _bs((1, c), lambda i: (0, 0))
    return pl.pallas_call(
        body, out_shape=jax.ShapeDtypeStruct((t_dim, c), BF16), grid=(t_dim // tm,),
        in_specs=[row, vec, vec], out_specs=row, name=name, compiler_params=_params("parallel"))(y1, g, b)


def _ln_silu_bwd(y1, g, b, dy3, name):
    t_dim, c = y1.shape
    tm = _tile(t_dim, 1024)

    def body(y_ref, g_ref, b_ref, d_ref, dy_ref, dg_ref, db_ref):
        yv = y_ref[...]
        gv = g_ref[...]
        xc = yv - jnp.mean(yv, axis=-1, keepdims=True)
        rstd = lax.rsqrt(jnp.mean(xc * xc, axis=-1, keepdims=True) + EPS)
        yh = xc * rstd
        y2 = yh * gv + b_ref[...]
        s = _sig(y2)
        dy2 = d_ref[...].astype(F32) * (s * (1.0 + y2 * (1.0 - s)))
        dyh = dy2 * gv
        dy_ref[...] = rstd * (dyh - jnp.mean(dyh, axis=-1, keepdims=True)
                              - yh * jnp.mean(dyh * yh, axis=-1, keepdims=True))
        dg_part = jnp.sum(dy2 * yh, axis=0, keepdims=True)
        db_part = jnp.sum(dy2, axis=0, keepdims=True)

        @pl.when(pl.program_id(0) == 0)
        def _():
            dg_ref[...] = dg_part
            db_ref[...] = db_part

        @pl.when(pl.program_id(0) > 0)
        def _():
            dg_ref[...] += dg_part
            db_ref[...] += db_part

    row = _bs((tm, c), lambda i: (i, 0))
    vec = _bs((1, c), lambda i: (0, 0))
    return pl.pallas_call(
        body, out_shape=(jax.ShapeDtypeStruct((t_dim, c), F32), jax.ShapeDtypeStruct((1, c), F32),
                         jax.ShapeDtypeStruct((1, c), F32)),
        grid=(t_dim // tm,), in_specs=[row, vec, vec, row], out_specs=(row, vec, vec), name=name,
        compiler_params=_params("arbitrary"))(y1, g, b, dy3)


def _pool_fwd(proj, col0, n_groups, n_batch, seq, name):
    ch = min(256, seq)

    def body(u_ref, o_ref, pad):
        w = lax.shift_left(jnp.int32(2), pl.program_id(1))
        pad[0:POOL_HALO, :] = jnp.zeros((POOL_HALO, LANE), F32)
        pad[POOL_HALO:POOL_HALO + seq, :] = u_ref[...].astype(F32)
        for c0 in range(0, seq, ch):
            acc = jnp.zeros((ch, LANE), F32)
            for j in range(POOL_WINDOW_MAX):
                acc = acc + jnp.where(j < w, 1.0, 0.0).astype(F32) * pad[pl.ds(c0 + POOL_HALO - j, ch), :]
            t = c0 + lax.broadcasted_iota(jnp.int32, (ch, LANE), 0)
            cnt = jnp.minimum(t + 1, w).astype(F32)
            o_ref[c0:c0 + ch, :] = (acc / cnt - pad[POOL_HALO + c0:POOL_HALO + c0 + ch, :]).astype(BF16)

    return pl.pallas_call(
        body, out_shape=jax.ShapeDtypeStruct((n_batch * seq, n_groups * LANE), BF16), grid=(n_batch, n_groups),
        in_specs=[_bs((seq, LANE), lambda b, g: (b, col0 + g))], out_specs=_bs((seq, LANE), lambda b, g: (b, g)),
        scratch_shapes=[pltpu.VMEM((seq + POOL_HALO, LANE), F32)], name=name,
        compiler_params=_params("parallel", "parallel"))(proj)


def _pool_bwd(dzp, n_groups, n_batch, seq, name):
    ch = min(256, seq)

    def body(d_ref, o_ref, pad):
        w = lax.shift_left(jnp.int32(2), pl.program_id(1))
        for c0 in range(0, seq, ch):
            t = c0 + lax.broadcasted_iota(jnp.int32, (ch, LANE), 0)
            cnt = jnp.minimum(t + 1, w).astype(F32)
            pad[c0:c0 + ch, :] = d_ref[c0:c0 + ch, :].astype(F32) / cnt
        pad[seq:seq + POOL_HALO, :] = jnp.zeros((POOL_HALO, LANE), F32)
        for c0 in range(0, seq, ch):
            acc = jnp.zeros((ch, LANE), F32)
            for j in range(POOL_WINDOW_MAX):
                acc = acc + jnp.where(j < w, 1.0, 0.0).astype(F32) * pad[pl.ds(c0 + j, ch), :]
            o_ref[c0:c0 + ch, :] = (acc - d_ref[c0:c0 + ch, :].astype(F32)).astype(BF16)

    tok = _bs((seq, LANE), lambda b, g: (b, g))
    return pl.pallas_call(
        body, out_shape=jax.ShapeDtypeStruct((n_batch * seq, n_groups * LANE), BF16), grid=(n_batch, n_groups),
        in_specs=[tok], out_specs=tok, scratch_shapes=[pltpu.VMEM((seq + POOL_HALO, LANE), F32)], name=name,
        compiler_params=_params("parallel", "parallel"))(dzp)


def _merge_fwd(proj, col0, yc, yp, scale, name):
    t_dim, d = yc.shape
    half = d // 2
    tm = _tile(t_dim, 1024)
    c0 = col0 // half

    def body(gc_ref, gp_ref, yc_ref, yp_ref, s_ref, o_ref):
        f32 = lambda r: r[...].astype(F32)
        o_ref[...] = (_sig(f32(gc_ref)) * f32(yc_ref) + _sig(f32(gp_ref)) * (f32(yp_ref) * s_ref[...])).astype(BF16)

    blk = _bs((tm, half), lambda i, j: (i, j))
    return pl.pallas_call(
        body, out_shape=jax.ShapeDtypeStruct((t_dim, d), BF16), grid=(t_dim // tm, 2),
        in_specs=[_bs((tm, half), lambda i, j: (i, c0 + j)), _bs((tm, half), lambda i, j: (i, c0 + 2 + j)),
                  blk, blk, _bs((1, half), lambda i, j: (0, j))],
        out_specs=blk, name=name, compiler_params=_params("parallel", "parallel"))(proj, proj, yc, yp, scale)


def _merge_bwd(proj, col0, yc, yp, scale, dm, name):
    t_dim, d = yc.shape
    half = d // 2
    tm = _tile(t_dim, 1024)
    c0 = col0 // half

    def body(gc_ref, gp_ref, yc_ref, yp_ref, s_ref, dm_ref, dgc_ref, dgp_ref, dyc_ref, dyp_ref, ds_ref):
        dmv = dm_ref[...].astype(F32)
        sgc = _sig(gc_ref[...].astype(F32))
        sgp = _sig(gp_ref[...].astype(F32))
        sv = s_ref[...]
        ypre = yp_ref[...].astype(F32)
        dgc_ref[...] = (dmv * yc_ref[...].astype(F32) * sgc * (1.0 - sgc)).astype(BF16)
        dgp_ref[...] = (dmv * (ypre * sv) * sgp * (1.0 - sgp)).astype(BF16)
        dyc_ref[...] = (dmv * sgc).astype(BF16)
        dyp = dmv * sgp
        dyp_ref[...] = (dyp * sv).astype(BF16)
        part = jnp.sum(dyp * ypre, axis=0, keepdims=True)

        @pl.when(pl.program_id(1) == 0)
        def _():
            ds_ref[...] = part

        @pl.when(pl.program_id(1) > 0)
        def _():
            ds_ref[...] += part

    blk = _bs((tm, half), lambda j, i: (i, j))
    big = jax.ShapeDtypeStruct((t_dim, d), BF16)
    return pl.pallas_call(
        body, out_shape=(big, big, big, big, jax.ShapeDtypeStruct((1, d), F32)), grid=(2, t_dim // tm),
        in_specs=[_bs((tm, half), lambda j, i: (i, c0 + j)), _bs((tm, half), lambda j, i: (i, c0 + 2 + j)),
                  blk, blk, _bs((1, half), lambda j, i: (0, j)), blk],
        out_specs=(blk, blk, blk, blk, _bs((1, half), lambda j, i: (0, j))), name=name,
        compiler_params=_params("parallel", "arbitrary"))(proj, proj, yc, yp, scale, dm)


def _attn_fwd(q, kv, n_batch, seq, m_len, name):
    d = q.shape[1]
    hd = d // XA_HEADS
    tq = _tile(seq, 2048)
    nq = seq // tq
    scale = hd ** -0.5

    def body(q_ref, k_ref, v_ref, o_ref):
        sc = lax.dot_general(q_ref[...].astype(BF16), k_ref[...].astype(BF16), (((1,), (1,)), ((), ())),
                             preferred_element_type=F32) * scale
        p = jnp.exp(sc - jnp.max(sc, axis=-1, keepdims=True))
        pr = p / jnp.sum(p, axis=-1, keepdims=True)
        o_ref[...] = jnp.dot(pr.astype(BF16), v_ref[...].astype(BF16), preferred_element_type=F32).astype(BF16)

    return pl.pallas_call(
        body, out_shape=jax.ShapeDtypeStruct((n_batch * seq, d), BF16), grid=(n_batch, XA_HEADS, nq),
        in_specs=[_bs((tq, hd), lambda b, h, i: (b * nq + i, h)), _bs((m_len, hd), lambda b, h, i: (b, h)),
                  _bs((m_len, hd), lambda b, h, i: (b, XA_HEADS + h))],
        out_specs=_bs((tq, hd), lambda b, h, i: (b * nq + i, h)), name=name,
        compiler_params=_params("parallel", "parallel", "parallel"))(q, kv, kv)


def _attn_bwd(q, kv, datt, n_batch, seq, m_len, name):
    d = q.shape[1]
    hd = d // XA_HEADS
    tq = _tile(seq, 2048)
    nq = seq // tq
    scale = hd ** -0.5

    def body(q_ref, k_ref, v_ref, do_ref, dq_ref, dk_ref, dv_ref):
        qb = q_ref[...].astype(BF16)
        kb = k_ref[...].astype(BF16)
        vb = v_ref[...].astype(BF16)
        dob = do_ref[...].astype(BF16)
        sc = lax.dot_general(qb, kb, (((1,), (1,)), ((), ())), preferred_element_type=F32) * scale
        p = jnp.exp(sc - jnp.max(sc, axis=-1, keepdims=True))
        pr = p / jnp.sum(p, axis=-1, keepdims=True)
        dpr = lax.dot_general(dob, vb, (((1,), (1,)), ((), ())), preferred_element_type=F32)
        dsc = pr * (dpr - jnp.sum(dpr * pr, axis=-1, keepdims=True)) * scale
        dsb = dsc.astype(BF16)
        dq_ref[...] = jnp.dot(dsb, kb, preferred_element_type=F32).astype(BF16)
        dv_part = lax.dot_general(pr.astype(BF16), dob, (((0,), (0,)), ((), ())), preferred_element_type=F32)
        dk_part = lax.dot_general(dsb, qb, (((0,), (0,)), ((), ())), preferred_element_type=F32)

        @pl.when(pl.program_id(2) == 0)
        def _():
            dk_ref[...] = dk_part
            dv_ref[...] = dv_part

        @pl.when(pl.program_id(2) > 0)
        def _():
            dk_ref[...] += dk_part
            dv_ref[...] += dv_part

    qs = _bs((tq, hd), lambda b, h, i: (b * nq + i, h))
    ks = _bs((m_len, hd), lambda b, h, i: (b, h))
    return pl.pallas_call(
        body, out_shape=(jax.ShapeDtypeStruct((n_batch * seq, d), BF16), jax.ShapeDtypeStruct((n_batch * m_len, d), F32),
                         jax.ShapeDtypeStruct((n_batch * m_len, d), F32)),
        grid=(n_batch, XA_HEADS, nq),
        in_specs=[qs, ks, _bs((m_len, hd), lambda b, h, i: (b, XA_HEADS + h)), qs],
        out_specs=(qs, ks, ks), name=name,
        compiler_params=_params("parallel", "parallel", "arbitrary"))(q, kv, kv, datt)


def _gelu_parts(g):
    th = jnp.tanh(GELU_C0 * (g + GELU_C1 * g * g * g))
    return th, 0.5 * g * (1.0 + th)


def _ffn_act_fwd(up_g, up_v, dw_w, n_batch, seq, name):
    kk, c2 = dw_w.shape
    f_dim = c2 // 2
    wd = FFN_LANE_GROUPS * LANE
    nj = f_dim // wd
    ch = min(128, seq)

    def body(g_ref, v_ref, wg_ref, wv_ref, o_ref, padg, padv):
        for h in range(FFN_LANE_GROUPS):
            lanes = slice(h * LANE, (h + 1) * LANE)
            for pad, src in ((padg, g_ref), (padv, v_ref)):
                pad[h, 0:FFN_HALO, :] = jnp.zeros((FFN_HALO, LANE), F32)
                pad[h, FFN_HALO:FFN_HALO + seq, :] = src[:, lanes].astype(F32)
            for c0 in range(0, seq, ch):
                gate = jnp.zeros((ch, LANE), F32)
                val = jnp.zeros((ch, LANE), F32)
                for k in range(kk):
                    off = c0 + FFN_HALO - (kk - 1) + k
                    gate = gate + wg_ref[k:k + 1, lanes] * padg[h, pl.ds(off, ch), :]
                    val = val + wv_ref[k:k + 1, lanes] * padv[h, pl.ds(off, ch), :]
                o_ref[c0:c0 + ch, lanes] = (_gelu_parts(gate)[1] * val).astype(BF16)

    pad_shape = pltpu.VMEM((FFN_LANE_GROUPS, seq + FFN_HALO, LANE), F32)
    return pl.pallas_call(
        body, out_shape=jax.ShapeDtypeStruct((n_batch * seq, f_dim), BF16), grid=(n_batch, nj),
        in_specs=[_bs((seq, wd), lambda b, j: (b, j)), _bs((seq, wd), lambda b, j: (b, j)),
                  _bs((kk, wd), lambda b, j: (0, j)), _bs((kk, wd), lambda b, j: (0, nj + j))],
        out_specs=_bs((seq, wd), lambda b, j: (b, j)), scratch_shapes=[pad_shape, pad_shape], name=name,
        compiler_params=_params("parallel", "parallel"))(up_g, up_v, dw_w, dw_w)


def _ffn_act_bwd(up_g, up_v, dw_w, dact, n_batch, seq, name):
    kk, c2 = dw_w.shape
    f_dim = c2 // 2
    wd = FFN_LANE_GROUPS * LANE
    nj = f_dim // wd
    ch = min(128, seq)

    def body(g_ref, v_ref, wg_ref, wv_ref, da_ref, dg_ref, dv_ref, dwg_ref, dwv_ref, padg, padv, pbg, pbv):
        @pl.when(pl.program_id(1) == 0)
        def _():
            dwg_ref[...] = jnp.zeros((kk, wd), F32)
            dwv_ref[...] = jnp.zeros((kk, wd), F32)

        for h in range(FFN_LANE_GROUPS):
            lanes = slice(h * LANE, (h + 1) * LANE)
            for pad, src in ((padg, g_ref), (padv, v_ref)):
                pad[h, 0:FFN_HALO, :] = jnp.zeros((FFN_HALO, LANE), F32)
                pad[h, FFN_HALO:FFN_HALO + seq, :] = src[:, lanes].astype(F32)
            for pb in (pbg, pbv):
                pb[h, seq:seq + FFN_HALO, :] = jnp.zeros((FFN_HALO, LANE), F32)
            for c0 in range(0, seq, ch):
                gate = jnp.zeros((ch, LANE), F32)
                val = jnp.zeros((ch, LANE), F32)
                for k in range(kk):
                    off = c0 + FFN_HALO - (kk - 1) + k
                    gate = gate + wg_ref[k:k + 1, lanes] * padg[h, pl.ds(off, ch), :]
                    val = val + wv_ref[k:k + 1, lanes] * padv[h, pl.ds(off, ch), :]
                sq = gate * gate
                th = jnp.tanh(GELU_C0 * gate * (1.0 + GELU_C1 * sq))
                half = 0.5 * th + 0.5
                dgelu = half * (1.0 + gate * (GELU_C0 + 3.0 * GELU_C0 * GELU_C1 * sq) * (1.0 - th))
                dav = da_ref[c0:c0 + ch, lanes].astype(F32)
                pbg[h, c0:c0 + ch, :] = dav * val * dgelu
                pbv[h, c0:c0 + ch, :] = dav * (gate * half)
            for pb, pad, w_ref, d_ref, dw_ref in ((pbg, padg, wg_ref, dg_ref, dwg_ref), (pbv, padv, wv_ref, dv_ref, dwv_ref)):
                for c0 in range(0, seq, ch):
                    acc = jnp.zeros((ch, LANE), F32)
                    for k in range(kk):
                        acc = acc + w_ref[k:k + 1, lanes] * pb[h, pl.ds(c0 + (kk - 1) - k, ch), :]
                    d_ref[c0:c0 + ch, lanes] = acc.astype(BF16)
                for k in range(kk):
                    s = jnp.zeros((1, LANE), F32)
                    for c0 in range(0, seq, ch):
                        s = s + jnp.sum(pb[h, c0:c0 + ch, :] * pad[h, pl.ds(c0 + FFN_HALO - (kk - 1) + k, ch), :],
                                        axis=0, keepdims=True)
                    dw_ref[k:k + 1, lanes] += s

    t_dim = n_batch * seq
    tok = _bs((seq, wd), lambda j, b: (b, j))
    wblk = _bs((kk, wd), lambda j, b: (0, j))
    pad_shape = pltpu.VMEM((FFN_LANE_GROUPS, seq + FFN_HALO, LANE), F32)
    return pl.pallas_call(
        body, out_shape=(jax.ShapeDtypeStruct((t_dim, f_dim), BF16), jax.ShapeDtypeStruct((t_dim, f_dim), BF16),
                         jax.ShapeDtypeStruct((kk, f_dim), F32), jax.ShapeDtypeStruct((kk, f_dim), F32)),
        grid=(nj, n_batch),
        in_specs=[tok, tok, wblk, _bs((kk, wd), lambda j, b: (0, nj + j)), tok],
        out_specs=(tok, tok, wblk, wblk), scratch_shapes=[pad_shape, pad_shape, pad_shape, pad_shape], name=name,
        compiler_params=_params("parallel", "arbitrary"))(up_g, up_v, dw_w, dw_w, dact)


def _sum_rows(parts, out_dtype, name):
    r_dim, c_dim = parts[0].shape
    tr = _tile(r_dim, 1200, SUBLANE)
    n = len(parts)

    def body(*refs):
        acc = refs[0][...].astype(F32)
        for r in refs[1:n]:
            acc = acc + r[...].astype(F32)
        refs[n][...] = acc.astype(out_dtype)

    blk = _bs((tr, c_dim), lambda i: (i, 0))
    return pl.pallas_call(
        body, out_shape=jax.ShapeDtypeStruct((r_dim, c_dim), out_dtype), grid=(r_dim // tr,),
        in_specs=[blk] * n, out_specs=blk, name=name, compiler_params=_params("parallel"))(*parts)


def _adamw(w, g, m, v, name):
    shape = w.shape
    c_dim = shape[-1]
    r_dim = w.size // c_dim
    two_d = lambda t: t.reshape(r_dim, c_dim)
    tr = _tile(r_dim, max(SUBLANE, (512 * 1024) // max(c_dim, LANE) // SUBLANE * SUBLANE), SUBLANE)
    c1 = 1.0 - ADAM_B1 ** ADAM_STEP
    c2 = 1.0 - ADAM_B2 ** ADAM_STEP

    def body(w_ref, g_ref, m_ref, v_ref, d_ref, mo_ref, vo_ref):
        gv = g_ref[...]
        mn = ADAM_B1 * m_ref[...] + (1.0 - ADAM_B1) * gv
        vn = ADAM_B2 * v_ref[...] + (1.0 - ADAM_B2) * (gv * gv)
        mo_ref[...] = mn
        vo_ref[...] = vn
        d_ref[...] = -ADAM_LR * ((mn / c1) / (jnp.sqrt(vn / c2) + ADAM_EPS) + ADAM_WD * w_ref[...])

    blk = _bs((tr, c_dim), lambda i: (i, 0))
    out = jax.ShapeDtypeStruct((r_dim, c_dim), F32)
    d, mo, vo = pl.pallas_call(
        body, out_shape=(out, out, out), grid=(r_dim // tr,), in_specs=[blk] * 4, out_specs=(blk, blk, blk),
        name=name, compiler_params=_params("parallel"))(two_d(w), two_d(g), two_d(m), two_d(v))
    return d.reshape(shape), mo.reshape(shape), vo.reshape(shape)


HBM_SPEC = pl.BlockSpec(memory_space=pltpu.HBM)


def _position():
    return lax.axis_index("x"), lax.axis_index("y"), lax.axis_index("c")


def _all_gather(shard, name):
    def body(x_ref, out_ref, send_sems, recv_sems, local_sem):
        x, y, c = _position()
        me, sibling = (x, y, c), (x, y, 1 - c)
        chips = [(1 - x, y), (x, 1 - y), (1 - x, 1 - y)]

        def rows(px, py, pc):
            return out_ref.at[4 * px + 2 * py + pc]

        def copy(k, block, to, src=None):
            return pltpu.make_async_remote_copy(
                src_ref=rows(*block) if src is None else src, dst_ref=rows(*block),
                send_sem=send_sems.at[k], recv_sem=recv_sems.at[k], device_id=to, device_id_type=MESH)

        mine = pltpu.make_async_copy(x_ref, rows(*me), local_sem)
        mine.start()
        first = [copy(0, me, sibling, src=x_ref)]
        first += [copy(1 + j, me, (*chip, c), src=x_ref) for j, chip in enumerate(chips)]
        for cp in first:
            cp.start()
        passed = [copy(4 + j, (*chip, c), sibling) for j, chip in enumerate(chips)]
        for j, chip in enumerate(chips):
            copy(1 + j, (*chip, c), me).wait_recv()
            passed[j].start()
        copy(0, sibling, me).wait_recv()
        for j, chip in enumerate(chips):
            copy(4 + j, (*chip, 1 - c), me).wait_recv()
        for cp in first + passed:
            cp.wait_send()
        mine.wait()

    return pl.pallas_call(
        body, out_shape=jax.ShapeDtypeStruct((N_DEV,) + shard.shape, shard.dtype),
        in_specs=[HBM_SPEC], out_specs=HBM_SPEC,
        scratch_shapes=[pltpu.SemaphoreType.DMA((7,)), pltpu.SemaphoreType.DMA((7,)), pltpu.SemaphoreType.DMA(())],
        name=name)(shard)


CHIP_RELATIONS = ((0, 0), (1, 0), (0, 1), (1, 1))


def _rs_pair_exchange(g, name):
    _, r_dim, c_dim = g.shape
    n = len(CHIP_RELATIONS)

    def body(g_ref, recv_ref, send_sems, recv_sems):
        x, y, c = _position()
        sibling = (x, y, 1 - c)
        copies = []
        for k, (rx, ry) in enumerate(CHIP_RELATIONS):
            px = x + rx - 2 * x * rx
            py = y + ry - 2 * y * ry
            copies.append(pltpu.make_async_remote_copy(
                src_ref=g_ref.at[4 * px + 2 * py + 1 - c], dst_ref=recv_ref.at[k], send_sem=send_sems.at[k],
                recv_sem=recv_sems.at[k], device_id=sibling, device_id_type=MESH))
        for cp in copies:
            cp.start()
        for cp in copies:
            cp.wait()

    return pl.pallas_call(
        body, out_shape=jax.ShapeDtypeStruct((n, r_dim, c_dim), g.dtype), in_specs=[HBM_SPEC], out_specs=HBM_SPEC,
        scratch_shapes=[pltpu.SemaphoreType.DMA((n,)), pltpu.SemaphoreType.DMA((n,))], name=name)(g)


def _rs_pair_sum(g, recv, name):
    _, r_dim, c_dim = g.shape
    n = len(CHIP_RELATIONS)
    tr = _tile(r_dim, 1200, SUBLANE)
    x, y, c = _position()
    own = jnp.stack([4 * (x + rx - 2 * x * rx) + 2 * (y + ry - 2 * y * ry) + c for rx, ry in CHIP_RELATIONS])

    def body(own_ref, g_ref, r_ref, o_ref):
        o_ref[...] = (g_ref[...].astype(F32) + r_ref[...].astype(F32)).astype(o_ref.dtype)

    blk = _bs((None, tr, c_dim), lambda k, i, own_ref: (k, i, 0))
    return pl.pallas_call(
        body, out_shape=jax.ShapeDtypeStruct((n, r_dim, c_dim), g.dtype),
        grid_spec=pltpu.PrefetchScalarGridSpec(
            num_scalar_prefetch=1, grid=(n, r_dim // tr),
            in_specs=[_bs((None, tr, c_dim), lambda k, i, own_ref: (own_ref[k], i, 0)), blk], out_specs=blk),
        name=name, compiler_params=_params("parallel", "parallel"))(own.astype(jnp.int32), g, recv)


SEM_SPEC = pl.BlockSpec(memory_space=pltpu.SEMAPHORE)
DATAFLOW = pltpu.SideEffectType.DATAFLOW_SIDE_EFFECTING
CHIP_FLIPS = CHIP_RELATIONS[1:]
TOKEN = jax.ShapeDtypeStruct((SUBLANE, LANE), F32)


def _flip(v, r):
    return v + r - 2 * v * r


def _chip_copies(src_ref, src_of, dst_ref, dst_of, send_sems, recv_sems):
    x, y, c = _position()
    me = 4 * x + 2 * y + c
    out = []
    for k, (rx, ry) in enumerate(CHIP_FLIPS):
        px, py = _flip(x, rx), _flip(y, ry)
        peer = 4 * px + 2 * py + c
        out.append(pltpu.make_async_remote_copy(
            src_ref=src_ref.at[src_of(k, me, peer)], dst_ref=dst_ref.at[dst_of(k, me, peer)],
            send_sem=send_sems.at[k], recv_sem=recv_sems.at[k], device_id=(px, py, c), device_id_type=MESH))
    return out


def _device_block(ref, spec, d):
    rows, axis = spec
    return ref.at[pl.ds(d * rows, rows)] if axis == 0 else ref.at[:, pl.ds(d * rows, rows)]


def _ag_chips_start(lands, specs, after, name):
    n = len(lands)
    nf = len(CHIP_FLIPS)

    def body(*refs):
        send_sems, recv_sems, token = refs[n + 1], refs[n + 2], refs[-1]
        x, y, c = _position()
        me = 4 * x + 2 * y + c
        for i, spec in enumerate(specs):
            blk = _device_block(refs[i], spec, me)
            for k, (rx, ry) in enumerate(CHIP_FLIPS):
                pltpu.make_async_remote_copy(
                    src_ref=blk, dst_ref=blk, send_sem=send_sems.at[nf * i + k], recv_sem=recv_sems.at[nf * i + k],
                    device_id=(_flip(x, rx), _flip(y, ry), c), device_id_type=MESH).start()
        token[...] = jnp.zeros(TOKEN.shape, TOKEN.dtype)

    sems = pltpu.SemaphoreType.DMA((nf * n,))
    return pl.pallas_call(
        body, name=name, out_shape=(sems, sems, *[pltpu.HBM(t.shape, t.dtype) for t in lands], TOKEN),
        in_specs=(HBM_SPEC,) * n + (ANY_SPEC,),
        out_specs=(SEM_SPEC, SEM_SPEC) + (HBM_SPEC,) * n + (pl.BlockSpec(memory_space=pltpu.VMEM),),
        input_output_aliases={i: 2 + i for i in range(n)}, compiler_params=pltpu.CompilerParams(has_side_effects=DATAFLOW),
    )(*[pltpu.with_memory_space_constraint(t, pltpu.HBM) for t in lands], after)


def _ag_chips_wait(send_sems, recv_sems, lands, specs, after, name):
    n = len(lands)
    nf = len(CHIP_FLIPS)

    def body(*refs):
        send_sems, recv_sems = refs[n], refs[n + 1]
        x, y, c = _position()
        me = 4 * x + 2 * y + c
        for i, spec in enumerate(specs):
            for k, (rx, ry) in enumerate(CHIP_FLIPS):
                px, py = _flip(x, rx), _flip(y, ry)
                cp = pltpu.make_async_remote_copy(
                    src_ref=_device_block(refs[i], spec, me), dst_ref=_device_block(refs[i], spec, 4 * px + 2 * py + c),
                    send_sem=send_sems.at[nf * i + k], recv_sem=recv_sems.at[nf * i + k],
                    device_id=(px, py, c), device_id_type=MESH)
                cp.wait_send()
                cp.wait_recv()

    return pl.pallas_call(
        body, name=name, out_shape=tuple(pltpu.HBM(t.shape, t.dtype) for t in lands),
        in_specs=(HBM_SPEC,) * n + (SEM_SPEC, SEM_SPEC, ANY_SPEC), out_specs=(HBM_SPEC,) * n,
        input_output_aliases={i: i for i in range(n)}, compiler_params=pltpu.CompilerParams(has_side_effects=DATAFLOW),
    )(*lands, send_sems, recv_sems, after)


def _ag_pair_forward(lands, specs, name):
    n = len(lands)
    nr = len(CHIP_RELATIONS)

    def body(*refs):
        outs, send_sems, recv_sems = refs[n:2 * n], refs[2 * n], refs[2 * n + 1]
        x, y, c = _position()
        copies = []
        for i, spec in enumerate(specs):
            for k, (rx, ry) in enumerate(CHIP_RELATIONS):
                chip = 4 * _flip(x, rx) + 2 * _flip(y, ry)
                held = _device_block(outs[i], spec, chip + c)
                sems = dict(send_sem=send_sems.at[nr * i + k], recv_sem=recv_sems.at[nr * i + k],
                            device_id=(x, y, 1 - c), device_id_type=MESH)
                mine = pltpu.make_async_remote_copy(src_ref=held, dst_ref=held, **sems)
                theirs = pltpu.make_async_remote_copy(src_ref=held, dst_ref=_device_block(outs[i], spec, chip + 1 - c), **sems)
                copies.append((mine, theirs))
        for mine, _ in copies:
            mine.start()
        for mine, theirs in copies:
            mine.wait_send()
            theirs.wait_recv()

    sems = pltpu.SemaphoreType.DMA((nr * n,))
    return pl.pallas_call(
        body, out_shape=tuple(jax.ShapeDtypeStruct(t.shape, t.dtype) for t in lands), in_specs=[HBM_SPEC] * n,
        out_specs=(HBM_SPEC,) * n, input_output_aliases={i: i for i in range(n)}, scratch_shapes=[sems, sems], name=name)(*lands)


def _rs_chips_start(pair, name):
    _, r_dim, c_dim = pair.shape
    n = len(CHIP_FLIPS)

    def body(pair_ref, far_ref, send_sems, recv_sems, pair_thru, far_thru, token):
        for cp in _chip_copies(pair_ref, lambda k, me, peer: k + 1, far_ref, lambda k, me, peer: k, send_sems, recv_sems):
            cp.start()
        token[...] = jnp.zeros(TOKEN.shape, TOKEN.dtype)

    far = lax.empty((n, r_dim, c_dim), pair.dtype)
    return pl.pallas_call(
        body, name=name,
        out_shape=(pltpu.SemaphoreType.DMA((n,)), pltpu.SemaphoreType.DMA((n,)), pltpu.HBM(pair.shape, pair.dtype),
                   pltpu.HBM(far.shape, far.dtype), TOKEN),
        in_specs=(HBM_SPEC, HBM_SPEC),
        out_specs=(SEM_SPEC, SEM_SPEC, HBM_SPEC, HBM_SPEC, pl.BlockSpec(memory_space=pltpu.VMEM)),
        input_output_aliases={0: 2, 1: 3}, compiler_params=pltpu.CompilerParams(has_side_effects=DATAFLOW),
    )(pltpu.with_memory_space_constraint(pair, pltpu.HBM), pltpu.with_memory_space_constraint(far, pltpu.HBM))


def _rs_chips_wait(send_sems, recv_sems, pair, far, after, name):
    def body(pair_ref, far_ref, send_sems, recv_sems, after_ref, pair_out, far_out):
        for cp in _chip_copies(pair_ref, lambda k, me, peer: k + 1, far_ref, lambda k, me, peer: k, send_sems, recv_sems):
            cp.wait_send()
            cp.wait_recv()

    return pl.pallas_call(
        body, name=name, out_shape=(pltpu.HBM(pair.shape, pair.dtype), pltpu.HBM(far.shape, far.dtype)),
        in_specs=(HBM_SPEC, HBM_SPEC, SEM_SPEC, SEM_SPEC, ANY_SPEC),
        out_specs=(HBM_SPEC, HBM_SPEC), input_output_aliases={0: 0, 1: 1},
        compiler_params=pltpu.CompilerParams(has_side_effects=DATAFLOW),
    )(pair, far, send_sems, recv_sems, after)


def _rs_final_sum(pair, far, name):
    _, r_dim, c_dim = pair.shape
    tr = _tile(r_dim, 1200, SUBLANE)

    def body(p_ref, f0_ref, f1_ref, f2_ref, o_ref):
        o_ref[...] = ((p_ref[...].astype(F32) + f0_ref[...].astype(F32)) + f1_ref[...].astype(F32)) + f2_ref[...].astype(F32)

    def slot(k):
        return _bs((None, tr, c_dim), lambda i: (k, i, 0))

    return pl.pallas_call(
        body, out_shape=jax.ShapeDtypeStruct((r_dim, c_dim), F32), grid=(r_dim // tr,),
        in_specs=[slot(0), slot(0), slot(1), slot(2)], out_specs=_bs((tr, c_dim), lambda i: (i, 0)), name=name,
        compiler_params=_params("parallel"))(pair, far, far, far)


def _reduce_scatter_begin(g, name):
    recv = _rs_pair_exchange(g, name + "_pair")
    pair = _rs_pair_sum(g, recv, name + "_pairsum")
    return _rs_chips_start(pair, name + "_chips_start")


def _reduce_scatter_end(state, after, name):
    send_sems, recv_sems, pair, far, _ = state
    pair, far = _rs_chips_wait(send_sems, recv_sems, pair, far, after, name + "_chips_wait")
    return _rs_final_sum(pair, far, name + "_sum")


MATRICES = (("w_in", True), ("w_out", False), ("w_q", False), ("w_kv", True), ("w_o", False), ("w_up", True),
            ("w_down", False), ("w_conv_out", True), ("w_pool_grp", True))
MIX_NAMES = ("w_in", "w_conv_out", "w_pool_grp", "w_out")
REST_NAMES = ("w_q", "w_kv", "w_o", "w_up", "w_down")


def _parts(layer):
    return (("mix", MIX_NAMES), ("rest", REST_NAMES)) if layer == 0 else (("all", MIX_NAMES + REST_NAMES),)


def _to_rows(name, transposed, w, d_model):
    if name == "w_pool_grp":
        w = jnp.swapaxes(w, 1, 2)
    elif transposed:
        w = w.T
    return w.reshape(-1, d_model)


def _stored_form(name, transposed, rows, shard_shape):
    if name == "w_pool_grp":
        g, i, o = shard_shape
        return rows.reshape(g, o, i)
    if transposed:
        return rows.reshape(shard_shape[1], shard_shape[0])
    return rows.reshape(shard_shape)


def _scatter_blocks(name, full, shard_shape, d_model, n_dev=N_DEV):
    if name == "w_pool_grp":
        g, i, o = shard_shape
        return jnp.swapaxes(full.reshape(g, n_dev, o, i), 0, 1).reshape(n_dev, -1, d_model)
    return full.reshape(n_dev, -1, d_model)


def kernel(x, mem, mix_norm_g, w_in, conv_dw_w, conv_dw_b, conv_ln_g, conv_ln_b, w_conv_out, w_pool_grp, pool_scale, w_out, xattn_norm_g, mem_norm_g, w_q, w_kv, w_o, ffn_norm_g, w_up, ffn_dw_w, w_down, final_norm_g, loss_target, m_mix_norm_g, m_w_in, m_conv_dw_w, m_conv_dw_b, m_conv_ln_g, m_conv_ln_b, m_w_conv_out, m_w_pool_grp, m_pool_scale, m_w_out, m_xattn_norm_g, m_mem_norm_g, m_w_q, m_w_kv, m_w_o, m_ffn_norm_g, m_w_up, m_ffn_dw_w, m_w_down, m_final_norm_g, v_mix_norm_g, v_w_in, v_conv_dw_w, v_conv_dw_b, v_conv_ln_g, v_conv_ln_b, v_w_conv_out, v_w_pool_grp, v_pool_scale, v_w_out, v_xattn_norm_g, v_mem_norm_g, v_w_q, v_w_kv, v_w_o, v_ffn_norm_g, v_w_up, v_ffn_dw_w, v_w_down, v_final_norm_g):
    p = dict(locals())
    weight_names = ["mix_norm_g", "w_in", "conv_dw_w", "conv_dw_b", "conv_ln_g", "conv_ln_b", "w_conv_out",
                    "w_pool_grp", "pool_scale", "w_out", "xattn_norm_g", "mem_norm_g", "w_q", "w_kv", "w_o",
                    "ffn_norm_g", "w_up", "ffn_dw_w", "w_down", "final_norm_g"]
    n_batch, seq, d_model = x.shape
    m_len = mem.shape[1]
    depth = w_in.shape[0]
    assert depth == 2, "the exchange schedule below is written for two layers"
    t_dim = n_batch * seq
    c_conv = conv_dw_b.shape[1]
    n_groups = w_pool_grp.shape[1]
    assert w_pool_grp.shape[2] == LANE and c_conv % LANE == 0 and n_groups * LANE == c_conv
    gate_col0 = 2 * c_conv + n_groups * LANE
    pool_col0 = (2 * c_conv) // LANE

    dev = 4 * lax.axis_index("x") + 2 * lax.axis_index("y") + lax.axis_index("c")
    filt = jnp.concatenate([conv_dw_w.reshape(-1), ffn_dw_w.reshape(-1)])
    filt_rows = filt.reshape(-1, d_model)
    transposed = dict(MATRICES)
    layout = {part: [(name, transposed[name], _to_rows(name, transposed[name], p[name][0], d_model).shape[0])
                     for name in names] for l in range(depth) for part, names in _parts(l)}
    part_of = {(l, name): part for l in range(depth) for part, names in _parts(l) for name in names}

    def landing(name, shard):
        if name == "w_pool_grp":
            block, axis = jnp.swapaxes(shard, 1, 2), 1
        elif name == "filt":
            block, axis = shard, 0
        else:
            block, axis = (shard.T if transposed[name] else shard), 0
        block = block if name == "filt" else block.astype(BF16)
        rows = block.shape[axis]
        shape = block.shape[:axis] + (N_DEV * rows,) + block.shape[axis + 1:]
        start = (0,) * axis + (dev * rows,) + (0,) * (block.ndim - axis - 1)
        return lax.dynamic_update_slice(lax.empty(shape, block.dtype), block, start), (rows, axis)

    ag_state = {}
    after = filt_rows
    for l in range(depth):
        for part, names in _parts(l):
            items = [(name, p[name][l]) for name in names]
            if (l, part) == (0, part_of[(0, "w_in")]):
                items.append(("filt", filt_rows))
            lands, specs = zip(*[landing(name, shard) for name, shard in items])
            out = _ag_chips_start(lands, specs, after, f"ag{l}{part}_chips_start")
            ag_state[(l, part)] = ([name for name, _ in items], specs, out)
            after = out[-1]
    all_started = after

    full = [dict() for _ in range(depth)]

    def ensure(l, name, after):
        if name in full[l]:
            return
        part = part_of[(l, name)]
        names, specs, out = ag_state[(l, part)]
        lands = _ag_chips_wait(out[0], out[1], out[2:-1], specs, after, f"ag{l}{part}_chips_wait")
        lands = _ag_pair_forward(lands, specs, f"ag{l}{part}_pair_forward")
        full[l].update(zip(names, lands))

    vec = lambda a: a.reshape(1, -1)
    x2d = x.reshape(t_dim, d_model)
    mem2d = mem.reshape(n_batch * m_len, d_model)
    mem_n = _rmsnorm_fwd(mem2d, vec(mem_norm_g), "mem_norm", after=all_started)
    h_first = _rmsnorm_fwd(x2d, vec(mix_norm_g[0]), "mix_norm_l0", after=mem_n)
    ensure(0, "w_in", h_first)
    filt_all = full[0]["filt"].reshape(N_DEV, -1)
    n_cw = conv_dw_w.size
    kc, cs = conv_dw_w.shape[1:]
    kf, fs = ffn_dw_w.shape[1:]
    conv_w_full = jnp.moveaxis(filt_all[:, :n_cw].reshape(N_DEV, depth, kc, cs), 0, 2).reshape(depth, kc, N_DEV * cs)
    ffn_w_full = jnp.moveaxis(filt_all[:, n_cw:].reshape(N_DEV, depth, kf, fs), 0, 2).reshape(depth, kf, N_DEV * fs)

    saved = []
    xc = x2d
    for l in range(depth):
        ensure(l, "w_in", xc)
        wl = full[l]
        s = {"x0": xc}
        s["h"] = h_first if l == 0 else _rmsnorm_fwd(xc, vec(mix_norm_g[l]), f"mix_norm_l{l}")
        s["proj"] = _matmul(s["h"], wl["w_in"], "nt", f"in_proj_l{l}", out_dtype=BF16)
        s["y1"] = _glu_conv_fwd(s["proj"], conv_w_full[l], vec(conv_dw_b[l]), n_batch, seq, f"glu_conv_l{l}")
        s["y3"] = _ln_silu_fwd(s["y1"], vec(conv_ln_g[l]), vec(conv_ln_b[l]), f"ln_silu_l{l}")
        s["yc"] = _matmul(s["y3"], wl["w_conv_out"], "nt", f"conv_out_l{l}", out_dtype=BF16)
        s["zp"] = _pool_fwd(s["proj"], pool_col0, n_groups, n_batch, seq, f"pool_l{l}")
        s["yp"] = _grouped(s["zp"], wl["w_pool_grp"], "nt", f"pool_proj_l{l}", out_dtype=BF16)
        s["merged"] = _merge_fwd(s["proj"], gate_col0, s["yc"], s["yp"], vec(pool_scale[l]), f"merge_l{l}")
        s["x1"] = _matmul(s["merged"], wl["w_out"], "nn", f"mix_out_l{l}", res=xc)
        ensure(l, "w_q", s["x1"])
        half_up = wl["w_up"].shape[0] // 2
        up_gate, up_val = (0, half_up), (half_up, half_up)
        s["hq"] = _rmsnorm_fwd(s["x1"], vec(xattn_norm_g[l]), f"xattn_norm_l{l}")
        s["q"] = _matmul(s["hq"], wl["w_q"], "nn", f"q_proj_l{l}", out_dtype=BF16)
        s["kv"] = _matmul(mem_n, wl["w_kv"], "nt", f"kv_proj_l{l}", out_dtype=BF16)
        s["att"] = _attn_fwd(s["q"], s["kv"], n_batch, seq, m_len, f"attn_l{l}")
        s["x2"] = _matmul(s["att"], wl["w_o"], "nn", f"attn_out_l{l}", res=s["x1"])
        s["hf"] = _rmsnorm_fwd(s["x2"], vec(ffn_norm_g[l]), f"ffn_norm_l{l}")
        s["up_g"] = _matmul(s["hf"], wl["w_up"], "nt", f"up_proj_gate_l{l}", out_dtype=BF16, b_window=up_gate)
        s["up_v"] = _matmul(s["hf"], wl["w_up"], "nt", f"up_proj_val_l{l}", out_dtype=BF16, b_window=up_val)
        s["act"] = _ffn_act_fwd(s["up_g"], s["up_v"], ffn_w_full[l], n_batch, seq, f"ffn_act_l{l}")
        xc = _matmul(s["act"], wl["w_down"], "nn", f"down_proj_l{l}", res=s["x2"])
        saved.append(s)

    dx, dxb, dg_final, loss_part = _loss_head(xc, vec(final_norm_g), loss_target.reshape(t_dim, d_model), "loss_head")

    small = {"final_norm_g": dg_final.reshape(-1)}
    big = [dict() for _ in range(depth)]
    rs_state = {}
    rs_after = loss_part

    def rs_begin(l, part):
        pack = lax.empty((N_DEV, sum(nrows for _, _, nrows in layout[part]), d_model), BF16)
        row0 = 0
        for name, _, nrows in layout[part]:
            pieces = big[l][name] if isinstance(big[l][name], tuple) else (big[l][name],)
            d0 = 0
            for piece in pieces:
                blocks = _scatter_blocks(name, piece, p[name].shape[1:], d_model, N_DEV // len(pieces)).astype(BF16)
                pack = lax.dynamic_update_slice(pack, blocks, (d0, row0, 0))
                d0 += blocks.shape[0]
            row0 += nrows
        rs_state[(l, part)] = _reduce_scatter_begin(pack, f"rs{l}{part}")
        return rs_state[(l, part)][4]

    dmem_n = None
    for l in reversed(range(depth)):
        wl, s = full[l], saved[l]
        sm = {}
        dact = _matmul(dxb, wl["w_down"], "nt", f"d_act_l{l}", out_dtype=BF16, after=rs_after)
        big[l]["w_down"] = _matmul(s["act"], dxb, "tn", f"d_w_down_l{l}", out_dtype=BF16)
        dup_g, dup_v, dwf_g, dwf_v = _ffn_act_bwd(s["up_g"], s["up_v"], ffn_w_full[l], dact, n_batch, seq,
                                                  f"ffn_act_bwd_l{l}")
        sm["ffn_dw_w"] = jnp.concatenate([dwf_g, dwf_v], axis=1)
        dx, dxb, dg = _matmul_rmsnorm_bwd((dup_g, dup_v), wl["w_up"], "nn", s["x2"], vec(ffn_norm_g[l]), dx,
                                          f"d_hf_ffn_norm_bwd_l{l}")
        big[l]["w_up"] = (_matmul(dup_g, s["hf"], "tn", f"d_w_up_gate_l{l}", out_dtype=BF16),
                          _matmul(dup_v, s["hf"], "tn", f"d_w_up_val_l{l}", out_dtype=BF16))
        sm["ffn_norm_g"] = dg
        datt = _matmul(dxb, wl["w_o"], "nt", f"d_att_l{l}", out_dtype=BF16, after=rs_after)
        big[l]["w_o"] = _matmul(s["att"], dxb, "tn", f"d_w_o_l{l}", out_dtype=BF16)
        dq, dk, dv = _attn_bwd(s["q"], s["kv"], datt, n_batch, seq, m_len, f"attn_bwd_l{l}")
        dkv = jnp.concatenate([dk, dv], axis=1)
        big[l]["w_kv"] = _matmul(dkv, mem_n, "tn", f"d_w_kv_l{l}", out_dtype=BF16)
        dmem_n = _matmul(dkv, wl["w_kv"], "nn", f"d_mem_l{l}", res=dmem_n)
        big[l]["w_q"] = _matmul(s["hq"], dq, "tn", f"d_w_q_l{l}", out_dtype=BF16)
        dx, dxb, dg = _matmul_rmsnorm_bwd(dq, wl["w_q"], "nt", s["x1"], vec(xattn_norm_g[l]), dx,
                                          f"d_hq_xattn_norm_bwd_l{l}")
        sm["xattn_norm_g"] = dg
        if part_of[(l, "w_q")] != part_of[(l, "w_in")]:
            rs_after = rs_begin(l, part_of[(l, "w_q")])
        dmerged = _matmul(dxb, wl["w_out"], "nt", f"d_merged_l{l}", out_dtype=BF16, after=rs_after)
        big[l]["w_out"] = _matmul(s["merged"], dxb, "tn", f"d_w_out_l{l}", out_dtype=BF16)
        dgc, dgp, dyc, dyp, dscale = _merge_bwd(s["proj"], gate_col0, s["yc"], s["yp"], vec(pool_scale[l]), dmerged,
                                                f"merge_bwd_l{l}")
        sm["pool_scale"] = dscale
        dzp = _grouped(dyp, wl["w_pool_grp"], "nn", f"d_zp_l{l}", out_dtype=BF16)
        big[l]["w_pool_grp"] = _grouped_tn(dyp, s["zp"], n_groups, f"d_w_pool_l{l}")
        du = _pool_bwd(dzp, n_groups, n_batch, seq, f"pool_bwd_l{l}")
        dy3 = _matmul(dyc, wl["w_conv_out"], "nn", f"d_y3_l{l}", out_dtype=BF16)
        big[l]["w_conv_out"] = _matmul(dyc, s["y3"], "tn", f"d_w_conv_out_l{l}", out_dtype=BF16)
        dy1, dlg, dlb = _ln_silu_bwd(s["y1"], vec(conv_ln_g[l]), vec(conv_ln_b[l]), dy3, f"ln_silu_bwd_l{l}")
        sm["conv_ln_g"], sm["conv_ln_b"] = dlg, dlb
        da, dgl, dcw, dcb = _glu_conv_bwd(s["proj"], conv_w_full[l], dy1, n_batch, seq, f"glu_conv_bwd_l{l}")
        sm["conv_dw_w"], sm["conv_dw_b"] = dcw, dcb
        dproj = jnp.concatenate([da, dgl, du, dgc, dgp], axis=1)
        big[l]["w_in"] = _matmul(dproj, s["h"], "tn", f"d_w_in_l{l}", out_dtype=BF16)
        dx, dxb, dg = _matmul_rmsnorm_bwd(dproj, wl["w_in"], "nn", s["x0"], vec(mix_norm_g[l]), dx,
                                          f"d_h_mix_norm_bwd_l{l}")
        sm["mix_norm_g"] = dg
        for k, val in sm.items():
            small[(l, k)] = val.reshape(-1)
        rs_after = rs_begin(l, part_of[(l, "w_in")])
    _, _, dg_mem = _rmsnorm_bwd(mem2d, vec(mem_norm_g), dmem_n, None, "mem_norm_bwd")
    small["mem_norm_g"] = dg_mem.reshape(-1)
    small["loss"] = loss_part.reshape(-1)

    grads = {}
    per_layer = {name: [None] * depth for name, _ in MATRICES}
    for l in reversed(range(depth)):
        for part, _ in reversed(_parts(l)):
            mat_grads = _reduce_scatter_end(rs_state[(l, part)], rs_after, f"rs{l}{part}")
            row0 = 0
            for name, tr, nrows in layout[part]:
                per_layer[name][l] = _stored_form(name, tr, mat_grads[row0:row0 + nrows], p[name].shape[1:])
                row0 += nrows
    flip = lambda t: jnp.swapaxes(t, -1, -2)
    stored_grads = {name: jnp.stack(per_layer[name]) for name, _ in MATRICES}
    for name, tr in MATRICES:
        grads[name] = flip(stored_grads[name]) if tr else stored_grads[name]

    keys = list(small.keys())
    flat = jnp.concatenate([small[k] for k in keys])
    n_small = flat.shape[0]
    rows_small = -(-n_small // (SUBLANE * d_model)) * SUBLANE
    flat = jnp.pad(flat, (0, rows_small * d_model - n_small)).reshape(rows_small, d_model)
    every = _all_gather(flat, "small_all_gather")
    total = _sum_rows([every[i] for i in range(N_DEV)], F32, "small_sum").reshape(-1)
    off = 0
    red = {}
    for k in keys:
        red[k] = total[off:off + small[k].shape[0]]
        off += small[k].shape[0]
    loss = red["loss"][0]
    for name in ("mix_norm_g", "conv_dw_b", "conv_ln_g", "conv_ln_b", "pool_scale", "xattn_norm_g", "ffn_norm_g"):
        grads[name] = jnp.stack([red[(l, name)] for l in range(depth)])
    grads["conv_dw_w"] = jnp.stack([
        lax.dynamic_slice_in_dim(red[(l, "conv_dw_w")].reshape(kc, N_DEV * cs), dev * cs, cs, axis=1)
        for l in range(depth)])
    grads["ffn_dw_w"] = jnp.stack([
        lax.dynamic_slice_in_dim(red[(l, "ffn_dw_w")].reshape(kf, N_DEV * fs), dev * fs, fs, axis=1)
        for l in range(depth)])
    grads["mem_norm_g"] = red["mem_norm_g"]
    grads["final_norm_g"] = red["final_norm_g"]

    deltas, new_m, new_v = {}, {}, {}
    for name in weight_names:
        if transposed.get(name, False):
            out = _adamw(flip(p[name]), stored_grads[name], flip(p["m_" + name]), flip(p["v_" + name]), f"adamw_{name}")
            deltas[name], new_m[name], new_v[name] = (flip(t) for t in out)
        else:
            deltas[name], new_m[name], new_v[name] = _adamw(p[name], grads[name], p["m_" + name], p["v_" + name],
                                                            f"adamw_{name}")
    grad_x = dx.reshape(n_batch, seq, d_model)
    return (loss, grad_x, *[grads[n] for n in weight_names], *[deltas[n] for n in weight_names],
            *[new_m[n] for n in weight_names], *[new_v[n] for n in weight_names])
```

```python
import functools

import jax
import jax.numpy as jnp
from jax import lax
from jax.experimental import pallas as pl
from jax.experimental.pallas import tpu as pltpu

F32 = jnp.float32
BF16 = jnp.bfloat16
MESH = pl.DeviceIdType.MESH

N_DEV = 8
EPS = 1e-6
V7X_VMEM_BYTES = 64 * 1024 * 1024
VMEM_LIMIT = (V7X_VMEM_BYTES * 3) // 4
LANE = 128
SUBLANE = 8

CONV_HALO = 32
POOL_HALO = 16
FFN_HALO = 8
FFN_LANE_GROUPS = 2
POOL_WINDOW_MAX = 16
XA_HEADS = 4

ADAM_LR = 0.001
ADAM_B1 = 0.9
ADAM_B2 = 0.999
ADAM_EPS = 1e-08
ADAM_WD = 0.01
ADAM_STEP = 10

GELU_C0 = 0.7978845608028654
GELU_C1 = 0.044715


ANY_SPEC = pl.BlockSpec(memory_space=pl.ANY)


def _tile(n, cap, mult=LANE):
    if n <= cap:
        return n
    best = None
    for d in range(mult, cap + 1, mult):
        if n % d == 0:
            best = d
    assert best is not None, (n, cap, mult)
    return best


def _params(*sem):
    return pltpu.CompilerParams(dimension_semantics=sem, vmem_limit_bytes=VMEM_LIMIT)


def _delayed(x, halo, rows, n_shifts):
    for r in range(min(SUBLANE, n_shifts)):
        xr = x if r == 0 else pltpu.roll(x, r, 0)
        for s in range(r, n_shifts, SUBLANE):
            yield s, xr[halo - (s - r):halo - (s - r) + rows]


def _advanced(x, rows, n_shifts):
    for r in range(min(SUBLANE, n_shifts)):
        xr = x if r == 0 else pltpu.roll(x, x.shape[0] - r, 0)
        for s in range(r, n_shifts, SUBLANE):
            yield s, xr[s - r:s - r + rows]


def _sig(x):
    return 1.0 / (1.0 + jnp.exp(-x))


def _bs(shape, imap):
    return pl.BlockSpec(shape, imap)


def _mxu_tile(n, cap):
    if n <= cap:
        return n
    best = {mult: max((d for d in range(mult, cap + 1, mult) if n % d == 0), default=0) for mult in (2 * LANE, LANE)}
    assert best[LANE] > 0, (n, cap)
    return best[2 * LANE] if 2 * best[2 * LANE] >= best[LANE] else best[LANE]


def _matmul(a, b, mode, name, res=None, out_dtype=F32, after=None, b_window=None):
    b_row0, b_rows = b_window if b_window is not None else (0, b.shape[0])
    if mode == "tn":
        k_dim, m_dim = a.shape
        k2, n_dim = b_rows, b.shape[1]
    elif mode == "nn":
        m_dim, k_dim = a.shape
        k2, n_dim = b_rows, b.shape[1]
    else:
        m_dim, k_dim = a.shape
        n_dim, k2 = b_rows, b.shape[1]
    assert k_dim == k2, (name, a.shape, b.shape)
    size = lambda t: jnp.dtype(t).itemsize
    tm = _mxu_tile(m_dim, 1792 if mode == "tn" else 1024)
    tn = _mxu_tile(n_dim, 2816)
    fixed = tm * tn * (2 * size(out_dtype) + (2 * size(res.dtype) if res is not None else 0) + 4)
    for cap in (2816, 2048, 1792, 1024, 512):
        tk = _mxu_tile(k_dim, cap)
        if fixed + 2 * tk * (tm * size(a.dtype) + tn * size(b.dtype)) <= VMEM_LIMIT - 8 * 1024 * 1024:
            break
    nk = k_dim // tk
    use_acc = nk > 1 and out_dtype != F32
    if mode == "tn":
        a_spec, ca = _bs((tk, tm), lambda i, j, k: (k, i)), 0
    else:
        a_spec, ca = _bs((tm, tk), lambda i, j, k: (i, k)), 1
    if mode == "nt":
        assert b_row0 % tn == 0
        b_spec, cb = _bs((tn, tk), lambda i, j, k: (j + b_row0 // tn, k)), 1
    else:
        assert b_row0 % tk == 0
        b_spec, cb = _bs((tk, tn), lambda i, j, k: (k + b_row0 // tk, j)), 0
    dims = (((ca,), (cb,)), ((), ()))
    o_spec = _bs((tm, tn), lambda i, j, k: (i, j))
    has_res = res is not None

    def body(*refs):
        a_ref, b_ref = refs[:2]
        r_ref = refs[2] if has_res else None
        o_ref = refs[n_in]
        k = pl.program_id(2)
        part = lax.dot_general(a_ref[...].astype(BF16), b_ref[...].astype(BF16), dims,
                               preferred_element_type=F32)
        if nk == 1:
            if has_res:
                part = part + r_ref[...].astype(F32)
            o_ref[...] = part.astype(out_dtype)
            return
        acc = refs[-1] if use_acc else o_ref

        @pl.when(k == 0)
        def _():
            acc[...] = part + r_ref[...].astype(F32) if has_res else part

        @pl.when(k > 0)
        def _():
            acc[...] += part

        if use_acc:
            @pl.when(k == nk - 1)
            def _():
                o_ref[...] = acc[...].astype(out_dtype)

    in_specs = [a_spec, b_spec] + ([o_spec] if has_res else [])
    args = (a, b) + ((res,) if has_res else ())
    if after is not None:
        in_specs.append(ANY_SPEC)
        args += (after,)
    n_in = len(args)
    return pl.pallas_call(
        body, out_shape=jax.ShapeDtypeStruct((m_dim, n_dim), out_dtype),
        grid=(m_dim // tm, n_dim // tn, nk), in_specs=in_specs, out_specs=o_spec,
        scratch_shapes=[pltpu.VMEM((tm, tn), F32)] if use_acc else [], name=name,
        compiler_params=_params("parallel", "parallel", "arbitrary"))(*args)


def _grouped(a, w, mode, name, out_dtype=F32):
    t_dim = a.shape[0]
    g_dim, r_dim, c_dim = w.shape
    ka, no = (c_dim, r_dim) if mode == "nt" else (r_dim, c_dim)
    tm = _tile(t_dim, 2048)
    dims = (((1,), (1 if mode == "nt" else 0,)), ((), ()))

    def body(a_ref, w_ref, o_ref):
        o_ref[...] = lax.dot_general(a_ref[...].astype(BF16), w_ref[...].astype(BF16), dims,
                                     preferred_element_type=F32).astype(out_dtype)

    return pl.pallas_call(
        body, out_shape=jax.ShapeDtypeStruct((t_dim, g_dim * no), out_dtype),
        grid=(t_dim // tm, g_dim),
        in_specs=[_bs((tm, ka), lambda i, g: (i, g)), _bs((None, r_dim, c_dim), lambda i, g: (g, 0, 0))],
        out_specs=_bs((tm, no), lambda i, g: (i, g)), name=name,
        compiler_params=_params("parallel", "parallel"))(a, w)


def _grouped_tn(a, b, g_dim, name):
    t_dim = a.shape[0]
    ra = a.shape[1] // g_dim
    cb = b.shape[1] // g_dim
    tm = _tile(t_dim, 2048)
    nt = t_dim // tm

    def body(a_ref, b_ref, o_ref):
        part = lax.dot_general(a_ref[...].astype(BF16), b_ref[...].astype(BF16), (((0,), (0,)), ((), ())),
                               preferred_element_type=F32)

        @pl.when(pl.program_id(1) == 0)
        def _():
            o_ref[...] = part

        @pl.when(pl.program_id(1) > 0)
        def _():
            o_ref[...] += part

    return pl.pallas_call(
        body, out_shape=jax.ShapeDtypeStruct((g_dim, ra, cb), F32), grid=(g_dim, nt),
        in_specs=[_bs((tm, ra), lambda g, i: (i, g)), _bs((tm, cb), lambda g, i: (i, g))],
        out_specs=_bs((None, ra, cb), lambda g, i: (g, 0, 0)), name=name,
        compiler_params=_params("parallel", "arbitrary"))(a, b)


def _rmsnorm_fwd(x, g, name, after=None):
    t_dim, d = x.shape
    tm = _tile(t_dim, 1024)

    def body(x_ref, g_ref, *rest):
        o_ref = rest[-1]
        xv = x_ref[...]
        r = lax.rsqrt(jnp.mean(xv * xv, axis=-1, keepdims=True) + EPS)
        o_ref[...] = (xv * r * g_ref[...]).astype(BF16)

    return pl.pallas_call(
        body, out_shape=jax.ShapeDtypeStruct((t_dim, d), BF16), grid=(t_dim // tm,),
        in_specs=[_bs((tm, d), lambda i: (i, 0)), _bs((1, d), lambda i: (0, 0))] + ([ANY_SPEC] if after is not None else []),
        out_specs=_bs((tm, d), lambda i: (i, 0)), name=name,
        compiler_params=_params("parallel"))(x, g, *([after] if after is not None else []))


def _rmsnorm_bwd(x, g, dh, dx_in, name):
    t_dim, d = x.shape
    tm = _tile(t_dim, 512)
    has_in = dx_in is not None

    def body(*refs):
        if has_in:
            x_ref, g_ref, dh_ref, di_ref, dx_ref, dxb_ref, dg_ref = refs
        else:
            x_ref, g_ref, dh_ref, dx_ref, dxb_ref, dg_ref = refs
        xv = x_ref[...]
        r = lax.rsqrt(jnp.mean(xv * xv, axis=-1, keepdims=True) + EPS)
        xh = xv * r
        dhv = dh_ref[...].astype(F32)
        dxh = dhv * g_ref[...]
        dx = r * (dxh - xh * jnp.mean(dxh * xh, axis=-1, keepdims=True))
        if has_in:
            dx = dx + di_ref[...]
        dx_ref[...] = dx
        dxb_ref[...] = dx.astype(BF16)
        part = jnp.sum(dhv * xh, axis=0, keepdims=True)

        @pl.when(pl.program_id(0) == 0)
        def _():
            dg_ref[...] = part

        @pl.when(pl.program_id(0) > 0)
        def _():
            dg_ref[...] += part

    row = _bs((tm, d), lambda i: (i, 0))
    vec = _bs((1, d), lambda i: (0, 0))
    args = (x, g, dh) + ((dx_in,) if has_in else ())
    return pl.pallas_call(
        body, out_shape=(jax.ShapeDtypeStruct((t_dim, d), F32), jax.ShapeDtypeStruct((t_dim, d), BF16),
                         jax.ShapeDtypeStruct((1, d), F32)),
        grid=(t_dim // tm,), in_specs=[row, vec, row] + ([row] if has_in else []),
        out_specs=(row, row, vec), name=name, compiler_params=_params("arbitrary"))(*args)


def _matmul_rmsnorm_bwd(a, b, mode, x, g, dx_in, name, res=None, b_window=None):
    pieces = a if isinstance(a, tuple) else (a,)
    n_p = len(pieces)
    b_row0, b_rows = b_window if b_window is not None else (0, b.shape[0])
    m_dim, k_piece = pieces[0].shape
    assert all(t.shape == pieces[0].shape for t in pieces)
    k_dim = n_p * k_piece
    d = x.shape[1]
    assert (b_rows, b.shape[1]) == ((k_dim, d) if mode == "nn" else (d, k_dim)), (name, pieces[0].shape, b.shape)
    tm = _mxu_tile(m_dim, 512)
    tk = _mxu_tile(k_piece, 1792)
    nkp = k_piece // tk
    nk = n_p * nkp
    has_res = res is not None
    if mode == "nt":
        assert b_row0 == 0
        b_spec, cb = _bs((d, tk), lambda i, k: (0, k)), 1
    else:
        assert b_row0 % tk == 0
        b_spec, cb = _bs((tk, d), lambda i, k: (k + b_row0 // tk, 0)), 0
    dims = (((1,), (cb,)), ((), ()))

    def body(*refs):
        b_ref = refs[n_p]
        r_ref = refs[n_p + 1] if has_res else None
        x_ref, g_ref, di_ref, dx_ref, dxb_ref, dg_ref = refs[n_p + 1 + has_res:n_p + 7 + has_res]
        i, k = pl.program_id(0), pl.program_id(1)

        def finish(dhv):
            if has_res:
                dhv = dhv + r_ref[...].astype(F32)
            xv = x_ref[...]
            r = lax.rsqrt(jnp.mean(xv * xv, axis=-1, keepdims=True) + EPS)
            xh = xv * r
            dxh = dhv * g_ref[...]
            dx = r * (dxh - xh * jnp.mean(dxh * xh, axis=-1, keepdims=True)) + di_ref[...]
            dx_ref[...] = dx
            dxb_ref[...] = dx.astype(BF16)
            dg_part = jnp.sum(dhv * xh, axis=0, keepdims=True)

            @pl.when(i == 0)
            def _():
                dg_ref[...] = dg_part

            @pl.when(i > 0)
            def _():
                dg_ref[...] += dg_part

        def step(a_ref):
            part = lax.dot_general(a_ref[...].astype(BF16), b_ref[...].astype(BF16), dims, preferred_element_type=F32)
            if nk == 1:
                finish(part)
                return
            acc = refs[-1]

            @pl.when(k == 0)
            def _():
                acc[...] = part

            @pl.when(jnp.logical_and(k > 0, k < nk - 1))
            def _():
                acc[...] += part

            @pl.when(k == nk - 1)
            def _():
                finish(acc[...] + part)

        if n_p == 1:
            step(refs[0])
        else:
            for q in range(n_p):
                pl.when(jnp.logical_and(k >= q * nkp, k < (q + 1) * nkp))(functools.partial(step, refs[q]))

    row = _bs((tm, d), lambda i, k: (i, 0))
    vec = _bs((1, d), lambda i, k: (0, 0))
    a_specs = [_bs((tm, tk), lambda i, k, q=q: (i, jnp.clip(k - q * nkp, 0, nkp - 1))) for q in range(n_p)]
    in_specs = a_specs + [b_spec] + ([row] if has_res else []) + [row, vec, row]
    args = pieces + (b,) + ((res,) if has_res else ()) + (x, g, dx_in)
    return pl.pallas_call(
        body, out_shape=(jax.ShapeDtypeStruct((m_dim, d), F32), jax.ShapeDtypeStruct((m_dim, d), BF16),
                         jax.ShapeDtypeStruct((1, d), F32)),
        grid=(m_dim // tm, nk), in_specs=in_specs, out_specs=(row, row, vec),
        scratch_shapes=[pltpu.VMEM((tm, d), F32)] if nk > 1 else [], name=name,
        compiler_params=_params("arbitrary", "arbitrary"))(*args)


def _loss_head(x, g, tgt, name):
    t_dim, d = x.shape
    tm = _tile(t_dim, 512)

    def body(x_ref, g_ref, t_ref, dx_ref, dxb_ref, dg_ref, loss_ref):
        xv = x_ref[...]
        gv = g_ref[...]
        r = lax.rsqrt(jnp.mean(xv * xv, axis=-1, keepdims=True) + EPS)
        xh = xv * r
        err = xh * gv - t_ref[...]
        dy = err * (1.0 / d)
        dxh = dy * gv
        dx = r * (dxh - xh * jnp.mean(dxh * xh, axis=-1, keepdims=True))
        dx_ref[...] = dx
        dxb_ref[...] = dx.astype(BF16)
        dg_part = jnp.sum(dy * xh, axis=0, keepdims=True)
        loss_part = jnp.full((1, LANE), 0.5 * jnp.sum(jnp.mean(err * err, axis=-1, keepdims=True)), F32)

        @pl.when(pl.program_id(0) == 0)
        def _():
            dg_ref[...] = dg_part
            loss_ref[...] = loss_part

        @pl.when(pl.program_id(0) > 0)
        def _():
            dg_ref[...] += dg_part
            loss_ref[...] += loss_part

    row = _bs((tm, d), lambda i: (i, 0))
    vec = _bs((1, d), lambda i: (0, 0))
    return pl.pallas_call(
        body, out_shape=(jax.ShapeDtypeStruct((t_dim, d), F32), jax.ShapeDtypeStruct((t_dim, d), BF16),
                         jax.ShapeDtypeStruct((1, d), F32), jax.ShapeDtypeStruct((1, LANE), F32)),
        grid=(t_dim // tm,), in_specs=[row, vec, row],
        out_specs=(row, row, vec, _bs((1, LANE), lambda i: (0, 0))), name=name,
        compiler_params=_params("arbitrary"))(x, g, tgt)


def _glu_conv_fwd(proj, dw_w, dw_b, n_batch, seq, name):
    kk, cc = dw_w.shape
    nj = cc // LANE
    ch = min(256, seq)

    def body(a_ref, gl_ref, w_ref, b_ref, o_ref, pad):
        pad[0:CONV_HALO, :] = jnp.zeros((CONV_HALO, LANE), F32)
        pad[CONV_HALO:CONV_HALO + seq, :] = a_ref[...].astype(F32) * _sig(gl_ref[...].astype(F32))
        for c0 in range(0, seq, ch):
            acc = jnp.broadcast_to(b_ref[...], (ch, LANE))
            for k in range(kk):
                acc = acc + w_ref[k:k + 1, :] * pad[pl.ds(c0 + CONV_HALO - (kk - 1) + k, ch), :]
            o_ref[c0:c0 + ch, :] = acc

    return pl.pallas_call(
        body, out_shape=jax.ShapeDtypeStruct((n_batch * seq, cc), F32), grid=(n_batch, nj),
        in_specs=[_bs((seq, LANE), lambda b, j: (b, j)), _bs((seq, LANE), lambda b, j: (b, nj + j)),
                  _bs((kk, LANE), lambda b, j: (0, j)), _bs((1, LANE), lambda b, j: (0, j))],
        out_specs=_bs((seq, LANE), lambda b, j: (b, j)),
        scratch_shapes=[pltpu.VMEM((seq + CONV_HALO, LANE), F32)], name=name,
        compiler_params=_params("parallel", "parallel"))(proj, proj, dw_w, dw_b)


def _glu_conv_bwd(proj, dw_w, dy1, n_batch, seq, name):
    kk, cc = dw_w.shape
    nj = cc // LANE
    ch = min(256, seq)

    def body(a_ref, gl_ref, dy_ref, w_ref, da_ref, dgl_ref, dw_ref, db_ref, padf, padb):
        first = pl.program_id(1) == 0
        padf[0:CONV_HALO, :] = jnp.zeros((CONV_HALO, LANE), F32)
        padf[CONV_HALO:CONV_HALO + seq, :] = a_ref[...].astype(F32) * _sig(gl_ref[...].astype(F32))
        padb[0:seq, :] = dy_ref[...]
        padb[seq:seq + CONV_HALO, :] = jnp.zeros((CONV_HALO, LANE), F32)

        @pl.when(first)
        def _():
            dw_ref[...] = jnp.zeros((kk, LANE), F32)
            db_ref[...] = jnp.zeros((1, LANE), F32)

        dws = [jnp.zeros((1, LANE), F32) for _ in range(kk)]
        for c0 in range(0, seq, ch):
            acc = jnp.zeros((ch, LANE), F32)
            y0 = padf[CONV_HALO + c0:CONV_HALO + c0 + ch, :]
            for k in range(kk):
                win = padb[pl.ds(c0 + (kk - 1) - k, ch), :]
                acc = acc + w_ref[k:k + 1, :] * win
                dws[k] = dws[k] + jnp.sum(win * y0, axis=0, keepdims=True)
            sg = _sig(gl_ref[c0:c0 + ch, :].astype(F32))
            da_ref[c0:c0 + ch, :] = (acc * sg).astype(BF16)
            dgl_ref[c0:c0 + ch, :] = (acc * a_ref[c0:c0 + ch, :].astype(F32) * sg * (1.0 - sg)).astype(BF16)
        for k in range(kk):
            dw_ref[k:k + 1, :] += dws[k]
        db_ref[...] += jnp.sum(dy_ref[...], axis=0, keepdims=True)

    tok = _bs((seq, LANE), lambda j, b: (b, j))
    t_dim = n_batch * seq
    return pl.pallas_call(
        body, out_shape=(jax.ShapeDtypeStruct((t_dim, cc), BF16), jax.ShapeDtypeStruct((t_dim, cc), BF16),
                         jax.ShapeDtypeStruct((kk, cc), F32), jax.ShapeDtypeStruct((1, cc), F32)),
        grid=(nj, n_batch),
        in_specs=[tok, _bs((seq, LANE), lambda j, b: (b, nj + j)), tok, _bs((kk, LANE), lambda j, b: (0, j))],
        out_specs=(tok, tok, _bs((kk, LANE), lambda j, b: (0, j)), _bs((1, LANE), lambda j, b: (0, j))),
        scratch_shapes=[pltpu.VMEM((seq + CONV_HALO, LANE), F32), pltpu.VMEM((seq + CONV_HALO, LANE), F32)],
        name=name, compiler_params=_params("parallel", "arbitrary"))(proj, proj, dy1, dw_w)


def _ln_silu_fwd(y1, g, b, name):
    t_dim, c = y1.shape
    tm = _tile(t_dim, 1024)

    def body(y_ref, g_ref, b_ref, o_ref):
        yv = y_ref[...]
        xc = yv - jnp.mean(yv, axis=-1, keepdims=True)
        rstd = lax.rsqrt(jnp.mean(xc * xc, axis=-1, keepdims=True) + EPS)
        y2 = xc * rstd * g_ref[...] + b_ref[...]
        o_ref[...] = (y2 * _sig(y2)).astype(BF16)

    row = _bs((tm, c), lambda i: (i, 0))
    vec = _bs((1, c), lambda i: (0, 0))
    return pl.pallas_call(
        body, out_shape=jax.ShapeDtypeStruct((t_dim, c), BF16), grid=(t_dim // tm,),
        in_specs=[row, vec, vec], out_specs=row, name=name, compiler_params=_params("parallel"))(y1, g, b)


def _ln_silu_bwd(y1, g, b, dy3, name):
    t_dim, c = y1.shape
    tm = _tile(t_dim, 1024)

    def body(y_ref, g_ref, b_ref, d_ref, dy_ref, dg_ref, db_ref):
        yv = y_ref[...]
        gv = g_ref[...]
        xc = yv - jnp.mean(yv, axis=-1, keepdims=True)
        rstd = lax.rsqrt(jnp.mean(xc * xc, axis=-1, keepdims=True) + EPS)
        yh = xc * rstd
        y2 = yh * gv + b_ref[...]
        s = _sig(y2)
        dy2 = d_ref[...].astype(F32) * (s * (1.0 + y2 * (1.0 - s)))
        dyh = dy2 * gv
        dy_ref[...] = rstd * (dyh - jnp.mean(dyh, axis=-1, keepdims=True)
                              - yh * jnp.mean(dyh * yh, axis=-1, keepdims=True))
        dg_part = jnp.sum(dy2 * yh, axis=0, keepdims=True)
        db_part = jnp.sum(dy2, axis=0, keepdims=True)

        @pl.when(pl.program_id(0) == 0)
        def _():
            dg_ref[...] = dg_part
            db_ref[...] = db_part

        @pl.when(pl.program_id(0) > 0)
        def _():
            dg_ref[...] += dg_part
            db_ref[...] += db_part

    row = _bs((tm, c), lambda i: (i, 0))
    vec = _bs((1, c), lambda i: (0, 0))
    return pl.pallas_call(
        body, out_shape=(jax.ShapeDtypeStruct((t_dim, c), F32), jax.ShapeDtypeStruct((1, c), F32),
                         jax.ShapeDtypeStruct((1, c), F32)),
        grid=(t_dim // tm,), in_specs=[row, vec, vec, row], out_specs=(row, vec, vec), name=name,
        compiler_params=_params("arbitrary"))(y1, g, b, dy3)


def _pool_fwd(proj, col0, n_groups, n_batch, seq, name):
    ch = min(256, seq)

    def body(u_ref, o_ref, pad):
        w = lax.shift_left(jnp.int32(2), pl.program_id(1))
        pad[0:POOL_HALO, :] = jnp.zeros((POOL_HALO, LANE), F32)
        pad[POOL_HALO:POOL_HALO + seq, :] = u_ref[...].astype(F32)
        for c0 in range(0, seq, ch):
            acc = jnp.zeros((ch, LANE), F32)
            for j in range(POOL_WINDOW_MAX):
                acc = acc + jnp.where(j < w, 1.0, 0.0).astype(F32) * pad[pl.ds(c0 + POOL_HALO - j, ch), :]
            t = c0 + lax.broadcasted_iota(jnp.int32, (ch, LANE), 0)
            cnt = jnp.minimum(t + 1, w).astype(F32)
            o_ref[c0:c0 + ch, :] = (acc / cnt - pad[POOL_HALO + c0:POOL_HALO + c0 + ch, :]).astype(BF16)

    return pl.pallas_call(
        body, out_shape=jax.ShapeDtypeStruct((n_batch * seq, n_groups * LANE), BF16), grid=(n_batch, n_groups),
        in_specs=[_bs((seq, LANE), lambda b, g: (b, col0 + g))], out_specs=_bs((seq, LANE), lambda b, g: (b, g)),
        scratch_shapes=[pltpu.VMEM((seq + POOL_HALO, LANE), F32)], name=name,
        compiler_params=_params("parallel", "parallel"))(proj)


def _pool_bwd(dzp, n_groups, n_batch, seq, name):
    ch = min(256, seq)

    def body(d_ref, o_ref, pad):
        w = lax.shift_left(jnp.int32(2), pl.program_id(1))
        for c0 in range(0, seq, ch):
            t = c0 + lax.broadcasted_iota(jnp.int32, (ch, LANE), 0)
            cnt = jnp.minimum(t + 1, w).astype(F32)
            pad[c0:c0 + ch, :] = d_ref[c0:c0 + ch, :].astype(F32) / cnt
        pad[seq:seq + POOL_HALO, :] = jnp.zeros((POOL_HALO, LANE), F32)
        for c0 in range(0, seq, ch):
            acc = jnp.zeros((ch, LANE), F32)
            for j in range(POOL_WINDOW_MAX):
                acc = acc + jnp.where(j < w, 1.0, 0.0).astype(F32) * pad[pl.ds(c0 + j, ch), :]
            o_ref[c0:c0 + ch, :] = (acc - d_ref[c0:c0 + ch, :].astype(F32)).astype(BF16)

    tok = _bs((seq, LANE), lambda b, g: (b, g))
    return pl.pallas_call(
        body, out_shape=jax.ShapeDtypeStruct((n_batch * seq, n_groups * LANE), BF16), grid=(n_batch, n_groups),
        in_specs=[tok], out_specs=tok, scratch_shapes=[pltpu.VMEM((seq + POOL_HALO, LANE), F32)], name=name,
        compiler_params=_params("parallel", "parallel"))(dzp)


def _merge_fwd(proj, col0, yc, yp, scale, name):
    t_dim, d = yc.shape
    half = d // 2
    tm = _tile(t_dim, 1024)
    c0 = col0 // half

    def body(gc_ref, gp_ref, yc_ref, yp_ref, s_ref, o_ref):
        f32 = lambda r: r[...].astype(F32)
        o_ref[...] = (_sig(f32(gc_ref)) * f32(yc_ref) + _sig(f32(gp_ref)) * (f32(yp_ref) * s_ref[...])).astype(BF16)

    blk = _bs((tm, half), lambda i, j: (i, j))
    return pl.pallas_call(
        body, out_shape=jax.ShapeDtypeStruct((t_dim, d), BF16), grid=(t_dim // tm, 2),
        in_specs=[_bs((tm, half), lambda i, j: (i, c0 + j)), _bs((tm, half), lambda i, j: (i, c0 + 2 + j)),
                  blk, blk, _bs((1, half), lambda i, j: (0, j))],
        out_specs=blk, name=name, compiler_params=_params("parallel", "parallel"))(proj, proj, yc, yp, scale)


def _merge_bwd(proj, col0, yc, yp, scale, dm, name):
    t_dim, d = yc.shape
    half = d // 2
    tm = _tile(t_dim, 1024)
    c0 = col0 // half

    def body(gc_ref, gp_ref, yc_ref, yp_ref, s_ref, dm_ref, dgc_ref, dgp_ref, dyc_ref, dyp_ref, ds_ref):
        dmv = dm_ref[...].astype(F32)
        sgc = _sig(gc_ref[...].astype(F32))
        sgp = _sig(gp_ref[...].astype(F32))
        sv = s_ref[...]
        ypre = yp_ref[...].astype(F32)
        dgc_ref[...] = (dmv * yc_ref[...].astype(F32) * sgc * (1.0 - sgc)).astype(BF16)
        dgp_ref[...] = (dmv * (ypre * sv) * sgp * (1.0 - sgp)).astype(BF16)
        dyc_ref[...] = (dmv * sgc).astype(BF16)
        dyp = dmv * sgp
        dyp_ref[...] = (dyp * sv).astype(BF16)
        part = jnp.sum(dyp * ypre, axis=0, keepdims=True)

        @pl.when(pl.program_id(1) == 0)
        def _():
            ds_ref[...] = part

        @pl.when(pl.program_id(1) > 0)
        def _():
            ds_ref[...] += part

    blk = _bs((tm, half), lambda j, i: (i, j))
    big = jax.ShapeDtypeStruct((t_dim, d), BF16)
    return pl.pallas_call(
        body, out_shape=(big, big, big, big, jax.ShapeDtypeStruct((1, d), F32)), grid=(2, t_dim // tm),
        in_specs=[_bs((tm, half), lambda j, i: (i, c0 + j)), _bs((tm, half), lambda j, i: (i, c0 + 2 + j)),
                  blk, blk, _bs((1, half), lambda j, i: (0, j)), blk],
        out_specs=(blk, blk, blk, blk, _bs((1, half), lambda j, i: (0, j))), name=name,
        compiler_params=_params("parallel", "arbitrary"))(proj, proj, yc, yp, scale, dm)


def _attn_fwd(q, kv, n_batch, seq, m_len, name):
    d = q.shape[1]
    hd = d // XA_HEADS
    tq = _tile(seq, 2048)
    nq = seq // tq
    scale = hd ** -0.5

    def body(q_ref, k_ref, v_ref, o_ref):
        sc = lax.dot_general(q_ref[...].astype(BF16), k_ref[...].astype(BF16), (((1,), (1,)), ((), ())),
                             preferred_element_type=F32) * scale
        p = jnp.exp(sc - jnp.max(sc, axis=-1, keepdims=True))
        pr = p / jnp.sum(p, axis=-1, keepdims=True)
        o_ref[...] = jnp.dot(pr.astype(BF16), v_ref[...].astype(BF16), preferred_element_type=F32).astype(BF16)

    return pl.pallas_call(
        body, out_shape=jax.ShapeDtypeStruct((n_batch * seq, d), BF16), grid=(n_batch, XA_HEADS, nq),
        in_specs=[_bs((tq, hd), lambda b, h, i: (b * nq + i, h)), _bs((m_len, hd), lambda b, h, i: (b, h)),
                  _bs((m_len, hd), lambda b, h, i: (b, XA_HEADS + h))],
        out_specs=_bs((tq, hd), lambda b, h, i: (b * nq + i, h)), name=name,
        compiler_params=_params("parallel", "parallel", "parallel"))(q, kv, kv)


def _attn_bwd(q, kv, datt, n_batch, seq, m_len, name):
    d = q.shape[1]
    hd = d // XA_HEADS
    tq = _tile(seq, 2048)
    nq = seq // tq
    scale = hd ** -0.5

    def body(q_ref, k_ref, v_ref, do_ref, dq_ref, dk_ref, dv_ref):
        qb = q_ref[...].astype(BF16)
        kb = k_ref[...].astype(BF16)
        vb = v_ref[...].astype(BF16)
        dob = do_ref[...].astype(BF16)
        sc = lax.dot_general(qb, kb, (((1,), (1,)), ((), ())), preferred_element_type=F32) * scale
        p = jnp.exp(sc - jnp.max(sc, axis=-1, keepdims=True))
        pr = p / jnp.sum(p, axis=-1, keepdims=True)
        dpr = lax.dot_general(dob, vb, (((1,), (1,)), ((), ())), preferred_element_type=F32)
        dsc = pr * (dpr - jnp.sum(dpr * pr, axis=-1, keepdims=True)) * scale
        dsb = dsc.astype(BF16)
        dq_ref[...] = jnp.dot(dsb, kb, preferred_element_type=F32).astype(BF16)
        dv_part = lax.dot_general(pr.astype(BF16), dob, (((0,), (0,)), ((), ())), preferred_element_type=F32)
        dk_part = lax.dot_general(dsb, qb, (((0,), (0,)), ((), ())), preferred_element_type=F32)

        @pl.when(pl.program_id(2) == 0)
        def _():
            dk_ref[...] = dk_part
            dv_ref[...] = dv_part

        @pl.when(pl.program_id(2) > 0)
        def _():
            dk_ref[...] += dk_part
            dv_ref[...] += dv_part

    qs = _bs((tq, hd), lambda b, h, i: (b * nq + i, h))
    ks = _bs((m_len, hd), lambda b, h, i: (b, h))
    return pl.pallas_call(
        body, out_shape=(jax.ShapeDtypeStruct((n_batch * seq, d), BF16), jax.ShapeDtypeStruct((n_batch * m_len, d), F32),
                         jax.ShapeDtypeStruct((n_batch * m_len, d), F32)),
        grid=(n_batch, XA_HEADS, nq),
        in_specs=[qs, ks, _bs((m_len, hd), lambda b, h, i: (b, XA_HEADS + h)), qs],
        out_specs=(qs, ks, ks), name=name,
        compiler_params=_params("parallel", "parallel", "arbitrary"))(q, kv, kv, datt)


def _gelu_parts(g):
    th = jnp.tanh(GELU_C0 * (g + GELU_C1 * g * g * g))
    return th, 0.5 * g * (1.0 + th)


def _ffn_act_fwd(up_g, up_v, dw_w, n_batch, seq, name):
    kk, c2 = dw_w.shape
    f_dim = c2 // 2
    wd = FFN_LANE_GROUPS * LANE
    nj = f_dim // wd
    ch = min(128, seq)

    def body(g_ref, v_ref, wg_ref, wv_ref, o_ref, padg, padv):
        for h in range(FFN_LANE_GROUPS):
            lanes = slice(h * LANE, (h + 1) * LANE)
            for pad, src in ((padg, g_ref), (padv, v_ref)):
                pad[h, 0:FFN_HALO, :] = jnp.zeros((FFN_HALO, LANE), F32)
                pad[h, FFN_HALO:FFN_HALO + seq, :] = src[:, lanes].astype(F32)
            for c0 in range(0, seq, ch):
                gate = jnp.zeros((ch, LANE), F32)
                val = jnp.zeros((ch, LANE), F32)
                for k in range(kk):
                    off = c0 + FFN_HALO - (kk - 1) + k
                    gate = gate + wg_ref[k:k + 1, lanes] * padg[h, pl.ds(off, ch), :]
                    val = val + wv_ref[k:k + 1, lanes] * padv[h, pl.ds(off, ch), :]
                o_ref[c0:c0 + ch, lanes] = (_gelu_parts(gate)[1] * val).astype(BF16)

    pad_shape = pltpu.VMEM((FFN_LANE_GROUPS, seq + FFN_HALO, LANE), F32)
    return pl.pallas_call(
        body, out_shape=jax.ShapeDtypeStruct((n_batch * seq, f_dim), BF16), grid=(n_batch, nj),
        in_specs=[_bs((seq, wd), lambda b, j: (b, j)), _bs((seq, wd), lambda b, j: (b, j)),
                  _bs((kk, wd), lambda b, j: (0, j)), _bs((kk, wd), lambda b, j: (0, nj + j))],
        out_specs=_bs((seq, wd), lambda b, j: (b, j)), scratch_shapes=[pad_shape, pad_shape], name=name,
        compiler_params=_params("parallel", "parallel"))(up_g, up_v, dw_w, dw_w)


def _ffn_act_bwd(up_g, up_v, dw_w, dact, n_batch, seq, name):
    kk, c2 = dw_w.shape
    f_dim = c2 // 2
    wd = FFN_LANE_GROUPS * LANE
    nj = f_dim // wd
    ch = min(128, seq)

    def body(g_ref, v_ref, wg_ref, wv_ref, da_ref, dg_ref, dv_ref, dwg_ref, dwv_ref, padg, padv, pbg, pbv):
        @pl.when(pl.program_id(1) == 0)
        def _():
            dwg_ref[...] = jnp.zeros((kk, wd), F32)
            dwv_ref[...] = jnp.zeros((kk, wd), F32)

        for h in range(FFN_LANE_GROUPS):
            lanes = slice(h * LANE, (h + 1) * LANE)
            for pad, src in ((padg, g_ref), (padv, v_ref)):
                pad[h, 0:FFN_HALO, :] = jnp.zeros((FFN_HALO, LANE), F32)
                pad[h, FFN_HALO:FFN_HALO + seq, :] = src[:, lanes].astype(F32)
            for pb in (pbg, pbv):
                pb[h, seq:seq + FFN_HALO, :] = jnp.zeros((FFN_HALO, LANE), F32)
            for c0 in range(0, seq, ch):
                gate = jnp.zeros((ch, LANE), F32)
                val = jnp.zeros((ch, LANE), F32)
                for k in range(kk):
                    off = c0 + FFN_HALO - (kk - 1) + k
                    gate = gate + wg_ref[k:k + 1, lanes] * padg[h, pl.ds(off, ch), :]
                    val = val + wv_ref[k:k + 1, lanes] * padv[h, pl.ds(off, ch), :]
                sq = gate * gate
                th = jnp.tanh(GELU_C0 * gate * (1.0 + GELU_C1 * sq))
                half = 0.5 * th + 0.5
                dgelu = half * (1.0 + gate * (GELU_C0 + 3.0 * GELU_C0 * GELU_C1 * sq) * (1.0 - th))
                dav = da_ref[c0:c0 + ch, lanes].astype(F32)
                pbg[h, c0:c0 + ch, :] = dav * val * dgelu
                pbv[h, c0:c0 + ch, :] = dav * (gate * half)
            for pb, pad, w_ref, d_ref, dw_ref in ((pbg, padg, wg_ref, dg_ref, dwg_ref), (pbv, padv, wv_ref, dv_ref, dwv_ref)):
                for c0 in range(0, seq, ch):
                    acc = jnp.zeros((ch, LANE), F32)
                    for k in range(kk):
                        acc = acc + w_ref[k:k + 1, lanes] * pb[h, pl.ds(c0 + (kk - 1) - k, ch), :]
                    d_ref[c0:c0 + ch, lanes] = acc.astype(BF16)
                for k in range(kk):
                    s = jnp.zeros((1, LANE), F32)
                    for c0 in range(0, seq, ch):
                        s = s + jnp.sum(pb[h, c0:c0 + ch, :] * pad[h, pl.ds(c0 + FFN_HALO - (kk - 1) + k, ch), :],
                                        axis=0, keepdims=True)
                    dw_ref[k:k + 1, lanes] += s

    t_dim = n_batch * seq
    tok = _bs((seq, wd), lambda j, b: (b, j))
    wblk = _bs((kk, wd), lambda j, b: (0, j))
    pad_shape = pltpu.VMEM((FFN_LANE_GROUPS, seq + FFN_HALO, LANE), F32)
    return pl.pallas_call(
        body, out_shape=(jax.ShapeDtypeStruct((t_dim, f_dim), BF16), jax.ShapeDtypeStruct((t_dim, f_dim), BF16),
                         jax.ShapeDtypeStruct((kk, f_dim), F32), jax.ShapeDtypeStruct((kk, f_dim), F32)),
        grid=(nj, n_batch),
        in_specs=[tok, tok, wblk, _bs((kk, wd), lambda j, b: (0, nj + j)), tok],
        out_specs=(tok, tok, wblk, wblk), scratch_shapes=[pad_shape, pad_shape, pad_shape, pad_shape], name=name,
        compiler_params=_params("parallel", "arbitrary"))(up_g, up_v, dw_w, dw_w, dact)


def _sum_rows(parts, out_dtype, name):
    r_dim, c_dim = parts[0].shape
    tr = _tile(r_dim, 1200, SUBLANE)
    n = len(parts)

    def body(*refs):
        acc = refs[0][...].astype(F32)
        for r in refs[1:n]:
            acc = acc + r[...].astype(F32)
        refs[n][...] = acc.astype(out_dtype)

    blk = _bs((tr, c_dim), lambda i: (i, 0))
    return pl.pallas_call(
        body, out_shape=jax.ShapeDtypeStruct((r_dim, c_dim), out_dtype), grid=(r_dim // tr,),
        in_specs=[blk] * n, out_specs=blk, name=name, compiler_params=_params("parallel"))(*parts)


def _adamw(w, g, m, v, name):
    shape = w.shape
    c_dim = shape[-1]
    r_dim = w.size // c_dim
    two_d = lambda t: t.reshape(r_dim, c_dim)
    tr = _tile(r_dim, max(SUBLANE, (512 * 1024) // max(c_dim, LANE) // SUBLANE * SUBLANE), SUBLANE)
    c1 = 1.0 - ADAM_B1 ** ADAM_STEP
    c2 = 1.0 - ADAM_B2 ** ADAM_STEP

    def body(w_ref, g_ref, m_ref, v_ref, d_ref, mo_ref, vo_ref):
        gv = g_ref[...]
        mn = ADAM_B1 * m_ref[...] + (1.0 - ADAM_B1) * gv
        vn = ADAM_B2 * v_ref[...] + (1.0 - ADAM_B2) * (gv * gv)
        mo_ref[...] = mn
        vo_ref[...] = vn
        d_ref[...] = -ADAM_LR * ((mn / c1) / (jnp.sqrt(vn / c2) + ADAM_EPS) + ADAM_WD * w_ref[...])

    blk = _bs((tr, c_dim), lambda i: (i, 0))
    out = jax.ShapeDtypeStruct((r_dim, c_dim), F32)
    d, mo, vo = pl.pallas_call(
        body, out_shape=(out, out, out), grid=(r_dim // tr,), in_specs=[blk] * 4, out_specs=(blk, blk, blk),
        name=name, compiler_params=_params("parallel"))(two_d(w), two_d(g), two_d(m), two_d(v))
    return d.reshape(shape), mo.reshape(shape), vo.reshape(shape)


HBM_SPEC = pl.BlockSpec(memory_space=pltpu.HBM)


def _position():
    return lax.axis_index("x"), lax.axis_index("y"), lax.axis_index("c")


def _all_gather(shard, name):
    def body(x_ref, out_ref, send_sems, recv_sems, local_sem):
        x, y, c = _position()
        me, sibling = (x, y, c), (x, y, 1 - c)
        chips = [(1 - x, y), (x, 1 - y), (1 - x, 1 - y)]

        def rows(px, py, pc):
            return out_ref.at[4 * px + 2 * py + pc]

        def copy(k, block, to, src=None):
            return pltpu.make_async_remote_copy(
                src_ref=rows(*block) if src is None else src, dst_ref=rows(*block),
                send_sem=send_sems.at[k], recv_sem=recv_sems.at[k], device_id=to, device_id_type=MESH)

        mine = pltpu.make_async_copy(x_ref, rows(*me), local_sem)
        mine.start()
        first = [copy(0, me, sibling, src=x_ref)]
        first += [copy(1 + j, me, (*chip, c), src=x_ref) for j, chip in enumerate(chips)]
        for cp in first:
            cp.start()
        passed = [copy(4 + j, (*chip, c), sibling) for j, chip in enumerate(chips)]
        for j, chip in enumerate(chips):
            copy(1 + j, (*chip, c), me).wait_recv()
            passed[j].start()
        copy(0, sibling, me).wait_recv()
        for j, chip in enumerate(chips):
            copy(4 + j, (*chip, 1 - c), me).wait_recv()
        for cp in first + passed:
            cp.wait_send()
        mine.wait()

    return pl.pallas_call(
        body, out_shape=jax.ShapeDtypeStruct((N_DEV,) + shard.shape, shard.dtype),
        in_specs=[HBM_SPEC], out_specs=HBM_SPEC,
        scratch_shapes=[pltpu.SemaphoreType.DMA((7,)), pltpu.SemaphoreType.DMA((7,)), pltpu.SemaphoreType.DMA(())],
        name=name)(shard)


CHIP_RELATIONS = ((0, 0), (1, 0), (0, 1), (1, 1))


def _rs_pair_exchange(g, name):
    _, r_dim, c_dim = g.shape
    n = len(CHIP_RELATIONS)

    def body(g_ref, recv_ref, send_sems, recv_sems):
        x, y, c = _position()
        sibling = (x, y, 1 - c)
        copies = []
        for k, (rx, ry) in enumerate(CHIP_RELATIONS):
            px = x + rx - 2 * x * rx
            py = y + ry - 2 * y * ry
            copies.append(pltpu.make_async_remote_copy(
                src_ref=g_ref.at[4 * px + 2 * py + 1 - c], dst_ref=recv_ref.at[k], send_sem=send_sems.at[k],
                recv_sem=recv_sems.at[k], device_id=sibling, device_id_type=MESH))
        for cp in copies:
            cp.start()
        for cp in copies:
            cp.wait()

    return pl.pallas_call(
        body, out_shape=jax.ShapeDtypeStruct((n, r_dim, c_dim), g.dtype), in_specs=[HBM_SPEC], out_specs=HBM_SPEC,
        scratch_shapes=[pltpu.SemaphoreType.DMA((n,)), pltpu.SemaphoreType.DMA((n,))], name=name)(g)


def _rs_pair_sum(g, recv, name):
    _, r_dim, c_dim = g.shape
    n = len(CHIP_RELATIONS)
    tr = _tile(r_dim, 1200, SUBLANE)
    x, y, c = _position()
    own = jnp.stack([4 * (x + rx - 2 * x * rx) + 2 * (y + ry - 2 * y * ry) + c for rx, ry in CHIP_RELATIONS])

    def body(own_ref, g_ref, r_ref, o_ref):
        o_ref[...] = (g_ref[...].astype(F32) + r_ref[...].astype(F32)).astype(o_ref.dtype)

    blk = _bs((None, tr, c_dim), lambda k, i, own_ref: (k, i, 0))
    return pl.pallas_call(
        body, out_shape=jax.ShapeDtypeStruct((n, r_dim, c_dim), g.dtype),
        grid_spec=pltpu.PrefetchScalarGridSpec(
            num_scalar_prefetch=1, grid=(n, r_dim // tr),
            in_specs=[_bs((None, tr, c_dim), lambda k, i, own_ref: (own_ref[k], i, 0)), blk], out_specs=blk),
        name=name, compiler_params=_params("parallel", "parallel"))(own.astype(jnp.int32), g, recv)


SEM_SPEC = pl.BlockSpec(memory_space=pltpu.SEMAPHORE)
DATAFLOW = pltpu.SideEffectType.DATAFLOW_SIDE_EFFECTING
CHIP_FLIPS = CHIP_RELATIONS[1:]
TOKEN = jax.ShapeDtypeStruct((SUBLANE, LANE), F32)


def _flip(v, r):
    return v + r - 2 * v * r


def _chip_copies(src_ref, src_of, dst_ref, dst_of, send_sems, recv_sems):
    x, y, c = _position()
    me = 4 * x + 2 * y + c
    out = []
    for k, (rx, ry) in enumerate(CHIP_FLIPS):
        px, py = _flip(x, rx), _flip(y, ry)
        peer = 4 * px + 2 * py + c
        out.append(pltpu.make_async_remote_copy(
            src_ref=src_ref.at[src_of(k, me, peer)], dst_ref=dst_ref.at[dst_of(k, me, peer)],
            send_sem=send_sems.at[k], recv_sem=recv_sems.at[k], device_id=(px, py, c), device_id_type=MESH))
    return out


def _device_block(ref, spec, d):
    rows, axis = spec
    return ref.at[pl.ds(d * rows, rows)] if axis == 0 else ref.at[:, pl.ds(d * rows, rows)]


def _ag_chips_start(lands, specs, after, name):
    n = len(lands)
    nf = len(CHIP_FLIPS)

    def body(*refs):
        send_sems, recv_sems, token = refs[n + 1], refs[n + 2], refs[-1]
        x, y, c = _position()
        me = 4 * x + 2 * y + c
        for i, spec in enumerate(specs):
            blk = _device_block(refs[i], spec, me)
            for k, (rx, ry) in enumerate(CHIP_FLIPS):
                pltpu.make_async_remote_copy(
                    src_ref=blk, dst_ref=blk, send_sem=send_sems.at[nf * i + k], recv_sem=recv_sems.at[nf * i + k],
                    device_id=(_flip(x, rx), _flip(y, ry), c), device_id_type=MESH).start()
        token[...] = jnp.zeros(TOKEN.shape, TOKEN.dtype)

    sems = pltpu.SemaphoreType.DMA((nf * n,))
    return pl.pallas_call(
        body, name=name, out_shape=(sems, sems, *[pltpu.HBM(t.shape, t.dtype) for t in lands], TOKEN),
        in_specs=(HBM_SPEC,) * n + (ANY_SPEC,),
        out_specs=(SEM_SPEC, SEM_SPEC) + (HBM_SPEC,) * n + (pl.BlockSpec(memory_space=pltpu.VMEM),),
        input_output_aliases={i: 2 + i for i in range(n)}, compiler_params=pltpu.CompilerParams(has_side_effects=DATAFLOW),
    )(*[pltpu.with_memory_space_constraint(t, pltpu.HBM) for t in lands], after)


def _ag_chips_wait(send_sems, recv_sems, lands, specs, after, name):
    n = len(lands)
    nf = len(CHIP_FLIPS)

    def body(*refs):
        send_sems, recv_sems = refs[n], refs[n + 1]
        x, y, c = _position()
        me = 4 * x + 2 * y + c
        for i, spec in enumerate(specs):
            for k, (rx, ry) in enumerate(CHIP_FLIPS):
                px, py = _flip(x, rx), _flip(y, ry)
                cp = pltpu.make_async_remote_copy(
                    src_ref=_device_block(refs[i], spec, me), dst_ref=_device_block(refs[i], spec, 4 * px + 2 * py + c),
                    send_sem=send_sems.at[nf * i + k], recv_sem=recv_sems.at[nf * i + k],
                    device_id=(px, py, c), device_id_type=MESH)
                cp.wait_send()
                cp.wait_recv()

    return pl.pallas_call(
        body, name=name, out_shape=tuple(pltpu.HBM(t.shape, t.dtype) for t in lands),
        in_specs=(HBM_SPEC,) * n + (SEM_SPEC, SEM_SPEC, ANY_SPEC), out_specs=(HBM_SPEC,) * n,
        input_output_aliases={i: i for i in range(n)}, compiler_params=pltpu.CompilerParams(has_side_effects=DATAFLOW),
    )(*lands, send_sems, recv_sems, after)


def _ag_pair_forward(lands, specs, name):
    n = len(lands)
    nr = len(CHIP_RELATIONS)

    def body(*refs):
        outs, send_sems, recv_sems = refs[n:2 * n], refs[2 * n], refs[2 * n + 1]
        x, y, c = _position()
        copies = []
        for i, spec in enumerate(specs):
            for k, (rx, ry) in enumerate(CHIP_RELATIONS):
                chip = 4 * _flip(x, rx) + 2 * _flip(y, ry)
                held = _device_block(outs[i], spec, chip + c)
                sems = dict(send_sem=send_sems.at[nr * i + k], recv_sem=recv_sems.at[nr * i + k],
                            device_id=(x, y, 1 - c), device_id_type=MESH)
                mine = pltpu.make_async_remote_copy(src_ref=held, dst_ref=held, **sems)
                theirs = pltpu.make_async_remote_copy(src_ref=held, dst_ref=_device_block(outs[i], spec, chip + 1 - c), **sems)
                copies.append((mine, theirs))
        for mine, _ in copies:
            mine.start()
        for mine, theirs in copies:
            mine.wait_send()
            theirs.wait_recv()

    sems = pltpu.SemaphoreType.DMA((nr * n,))
    return pl.pallas_call(
        body, out_shape=tuple(jax.ShapeDtypeStruct(t.shape, t.dtype) for t in lands), in_specs=[HBM_SPEC] * n,
        out_specs=(HBM_SPEC,) * n, input_output_aliases={i: i for i in range(n)}, scratch_shapes=[sems, sems], name=name)(*lands)


def _rs_chips_start(pair, name):
    _, r_dim, c_dim = pair.shape
    n = len(CHIP_FLIPS)

    def body(pair_ref, far_ref, send_sems, recv_sems, pair_thru, far_thru, token):
        for cp in _chip_copies(pair_ref, lambda k, me, peer: k + 1, far_ref, lambda k, me, peer: k, send_sems, recv_sems):
            cp.start()
        token[...] = jnp.zeros(TOKEN.shape, TOKEN.dtype)

    far = lax.empty((n, r_dim, c_dim), pair.dtype)
    return pl.pallas_call(
        body, name=name,
        out_shape=(pltpu.SemaphoreType.DMA((n,)), pltpu.SemaphoreType.DMA((n,)), pltpu.HBM(pair.shape, pair.dtype),
                   pltpu.HBM(far.shape, far.dtype), TOKEN),
        in_specs=(HBM_SPEC, HBM_SPEC),
        out_specs=(SEM_SPEC, SEM_SPEC, HBM_SPEC, HBM_SPEC, pl.BlockSpec(memory_space=pltpu.VMEM)),
        input_output_aliases={0: 2, 1: 3}, compiler_params=pltpu.CompilerParams(has_side_effects=DATAFLOW),
    )(pltpu.with_memory_space_constraint(pair, pltpu.HBM), pltpu.with_memory_space_constraint(far, pltpu.HBM))


def _rs_chips_wait(send_sems, recv_sems, pair, far, after, name):
    def body(pair_ref, far_ref, send_sems, recv_sems, after_ref, pair_out, far_out):
        for cp in _chip_copies(pair_ref, lambda k, me, peer: k + 1, far_ref, lambda k, me, peer: k, send_sems, recv_sems):
            cp.wait_send()
            cp.wait_recv()

    return pl.pallas_call(
        body, name=name, out_shape=(pltpu.HBM(pair.shape, pair.dtype), pltpu.HBM(far.shape, far.dtype)),
        in_specs=(HBM_SPEC, HBM_SPEC, SEM_SPEC, SEM_SPEC, ANY_SPEC),
        out_specs=(HBM_SPEC, HBM_SPEC), input_output_aliases={0: 0, 1: 1},
        compiler_params=pltpu.CompilerParams(has_side_effects=DATAFLOW),
    )(pair, far, send_sems, recv_sems, after)


def _rs_final_sum(pair, far, name):
    _, r_dim, c_dim = pair.shape
    tr = _tile(r_dim, 1200, SUBLANE)

    def body(p_ref, f0_ref, f1_ref, f2_ref, o_ref):
        o_ref[...] = ((p_ref[...].astype(F32) + f0_ref[...].astype(F32)) + f1_ref[...].astype(F32)) + f2_ref[...].astype(F32)

    def slot(k):
        return _bs((None, tr, c_dim), lambda i: (k, i, 0))

    return pl.pallas_call(
        body, out_shape=jax.ShapeDtypeStruct((r_dim, c_dim), F32), grid=(r_dim // tr,),
        in_specs=[slot(0), slot(0), slot(1), slot(2)], out_specs=_bs((tr, c_dim), lambda i: (i, 0)), name=name,
        compiler_params=_params("parallel"))(pair, far, far, far)


def _reduce_scatter_begin(g, name):
    recv = _rs_pair_exchange(g, name + "_pair")
    pair = _rs_pair_sum(g, recv, name + "_pairsum")
    return _rs_chips_start(pair, name + "_chips_start")


def _reduce_scatter_end(state, after, name):
    send_sems, recv_sems, pair, far, _ = state
    pair, far = _rs_chips_wait(send_sems, recv_sems, pair, far, after, name + "_chips_wait")
    return _rs_final_sum(pair, far, name + "_sum")


MATRICES = (("w_in", True), ("w_out", False), ("w_q", False), ("w_kv", True), ("w_o", False), ("w_up", True),
            ("w_down", False), ("w_conv_out", True), ("w_pool_grp", True))
MIX_NAMES = ("w_in", "w_conv_out", "w_pool_grp", "w_out")
REST_NAMES = ("w_q", "w_kv", "w_o", "w_up", "w_down")


def _parts(layer):
    return (("mix", MIX_NAMES), ("rest", REST_NAMES)) if layer == 0 else (("all", MIX_NAMES + REST_NAMES),)


def _to_rows(name, transposed, w, d_model):
    if name == "w_pool_grp":
        w = jnp.swapaxes(w, 1, 2)
    elif transposed:
        w = w.T
    return w.reshape(-1, d_model)


def _stored_form(name, transposed, rows, shard_shape):
    if name == "w_pool_grp":
        g, i, o = shard_shape
        return rows.reshape(g, o, i)
    if transposed:
        return rows.reshape(shard_shape[1], shard_shape[0])
    return rows.reshape(shard_shape)


def _scatter_blocks(name, full, shard_shape, d_model, n_dev=N_DEV):
    if name == "w_pool_grp":
        g, i, o = shard_shape
        return jnp.swapaxes(full.reshape(g, n_dev, o, i), 0, 1).reshape(n_dev, -1, d_model)
    return full.reshape(n_dev, -1, d_model)


def kernel(x, mem, mix_norm_g, w_in, conv_dw_w, conv_dw_b, conv_ln_g, conv_ln_b, w_conv_out, w_pool_grp, pool_scale, w_out, xattn_norm_g, mem_norm_g, w_q, w_kv, w_o, ffn_norm_g, w_up, ffn_dw_w, w_down, final_norm_g, loss_target, m_mix_norm_g, m_w_in, m_conv_dw_w, m_conv_dw_b, m_conv_ln_g, m_conv_ln_b, m_w_conv_out, m_w_pool_grp, m_pool_scale, m_w_out, m_xattn_norm_g, m_mem_norm_g, m_w_q, m_w_kv, m_w_o, m_ffn_norm_g, m_w_up, m_ffn_dw_w, m_w_down, m_final_norm_g, v_mix_norm_g, v_w_in, v_conv_dw_w, v_conv_dw_b, v_conv_ln_g, v_conv_ln_b, v_w_conv_out, v_w_pool_grp, v_pool_scale, v_w_out, v_xattn_norm_g, v_mem_norm_g, v_w_q, v_w_kv, v_w_o, v_ffn_norm_g, v_w_up, v_ffn_dw_w, v_w_down, v_final_norm_g):
    p = dict(locals())
    weight_names = ["mix_norm_g", "w_in", "conv_dw_w", "conv_dw_b", "conv_ln_g", "conv_ln_b", "w_conv_out",
                    "w_pool_grp", "pool_scale", "w_out", "xattn_norm_g", "mem_norm_g", "w_q", "w_kv", "w_o",
                    "ffn_norm_g", "w_up", "ffn_dw_w", "w_down", "final_norm_g"]
    n_batch, seq, d_model = x.shape
    m_len = mem.shape[1]
    depth = w_in.shape[0]
    assert depth == 2, "the exchange schedule below is written for two layers"
    t_dim = n_batch * seq
    c_conv = conv_dw_b.shape[1]
    n_groups = w_pool_grp.shape[1]
    assert w_pool_grp.shape[2] == LANE and c_conv % LANE == 0 and n_groups * LANE == c_conv
    gate_col0 = 2 * c_conv + n_groups * LANE
    pool_col0 = (2 * c_conv) // LANE

    dev = 4 * lax.axis_index("x") + 2 * lax.axis_index("y") + lax.axis_index("c")
    filt = jnp.concatenate([conv_dw_w.reshape(-1), ffn_dw_w.reshape(-1)])
    filt_rows = filt.reshape(-1, d_model)
    transposed = dict(MATRICES)
    layout = {part: [(name, transposed[name], _to_rows(name, transposed[name], p[name][0], d_model).shape[0])
                     for name in names] for l in range(depth) for part, names in _parts(l)}
    part_of = {(l, name): part for l in range(depth) for part, names in _parts(l) for name in names}

    def landing(name, shard):
        if name == "w_pool_grp":
            block, axis = jnp.swapaxes(shard, 1, 2), 1
        elif name == "filt":
            block, axis = shard, 0
        else:
            block, axis = (shard.T if transposed[name] else shard), 0
        block = block if name == "filt" else block.astype(BF16)
        rows = block.shape[axis]
        shape = block.shape[:axis] + (N_DEV * rows,) + block.shape[axis + 1:]
        start = (0,) * axis + (dev * rows,) + (0,) * (block.ndim - axis - 1)
        return lax.dynamic_update_slice(lax.empty(shape, block.dtype), block, start), (rows, axis)

    ag_state = {}
    after = filt_rows
    for l in range(depth):
        for part, names in _parts(l):
            items = [(name, p[name][l]) for name in names]
            if (l, part) == (0, part_of[(0, "w_in")]):
                items.append(("filt", filt_rows))
            lands, specs = zip(*[landing(name, shard) for name, shard in items])
            out = _ag_chips_start(lands, specs, after, f"ag{l}{part}_chips_start")
            ag_state[(l, part)] = ([name for name, _ in items], specs, out)
            after = out[-1]
    all_started = after

    full = [dict() for _ in range(depth)]

    def ensure(l, name, after):
        if name in full[l]:
            return
        part = part_of[(l, name)]
        names, specs, out = ag_state[(l, part)]
        lands = _ag_chips_wait(out[0], out[1], out[2:-1], specs, after, f"ag{l}{part}_chips_wait")
        lands = _ag_pair_forward(lands, specs, f"ag{l}{part}_pair_forward")
        full[l].update(zip(names, lands))

    vec = lambda a: a.reshape(1, -1)
    x2d = x.reshape(t_dim, d_model)
    mem2d = mem.reshape(n_batch * m_len, d_model)
    mem_n = _rmsnorm_fwd(mem2d, vec(mem_norm_g), "mem_norm", after=all_started)
    h_first = _rmsnorm_fwd(x2d, vec(mix_norm_g[0]), "mix_norm_l0", after=mem_n)
    ensure(0, "w_in", h_first)
    filt_all = full[0]["filt"].reshape(N_DEV, -1)
    n_cw = conv_dw_w.size
    kc, cs = conv_dw_w.shape[1:]
    kf, fs = ffn_dw_w.shape[1:]
    conv_w_full = jnp.moveaxis(filt_all[:, :n_cw].reshape(N_DEV, depth, kc, cs), 0, 2).reshape(depth, kc, N_DEV * cs)
    ffn_w_full = jnp.moveaxis(filt_all[:, n_cw:].reshape(N_DEV, depth, kf, fs), 0, 2).reshape(depth, kf, N_DEV * fs)

    saved = []
    xc = x2d
    for l in range(depth):
        ensure(l, "w_in", xc)
        wl = full[l]
        s = {"x0": xc}
        s["h"] = h_first if l == 0 else _rmsnorm_fwd(xc, vec(mix_norm_g[l]), f"mix_norm_l{l}")
        s["proj"] = _matmul(s["h"], wl["w_in"], "nt", f"in_proj_l{l}", out_dtype=BF16)
        s["y1"] = _glu_conv_fwd(s["proj"], conv_w_full[l], vec(conv_dw_b[l]), n_batch, seq, f"glu_conv_l{l}")
        s["y3"] = _ln_silu_fwd(s["y1"], vec(conv_ln_g[l]), vec(conv_ln_b[l]), f"ln_silu_l{l}")
        s["yc"] = _matmul(s["y3"], wl["w_conv_out"], "nt", f"conv_out_l{l}", out_dtype=BF16)
        s["zp"] = _pool_fwd(s["proj"], pool_col0, n_groups, n_batch, seq, f"pool_l{l}")
        s["yp"] = _grouped(s["zp"], wl["w_pool_grp"], "nt", f"pool_proj_l{l}", out_dtype=BF16)
        s["merged"] = _merge_fwd(s["proj"], gate_col0, s["yc"], s["yp"], vec(pool_scale[l]), f"merge_l{l}")
        s["x1"] = _matmul(s["merged"], wl["w_out"], "nn", f"mix_out_l{l}", res=xc)
        ensure(l, "w_q", s["x1"])
        half_up = wl["w_up"].shape[0] // 2
        up_gate, up_val = (0, half_up), (half_up, half_up)
        s["hq"] = _rmsnorm_fwd(s["x1"], vec(xattn_norm_g[l]), f"xattn_norm_l{l}")
        s["q"] = _matmul(s["hq"], wl["w_q"], "nn", f"q_proj_l{l}", out_dtype=BF16)
        s["kv"] = _matmul(mem_n, wl["w_kv"], "nt", f"kv_proj_l{l}", out_dtype=BF16)
        s["att"] = _attn_fwd(s["q"], s["kv"], n_batch, seq, m_len, f"attn_l{l}")
        s["x2"] = _matmul(s["att"], wl["w_o"], "nn", f"attn_out_l{l}", res=s["x1"])
        s["hf"] = _rmsnorm_fwd(s["x2"], vec(ffn_norm_g[l]), f"ffn_norm_l{l}")
        s["up_g"] = _matmul(s["hf"], wl["w_up"], "nt", f"up_proj_gate_l{l}", out_dtype=BF16, b_window=up_gate)
        s["up_v"] = _matmul(s["hf"], wl["w_up"], "nt", f"up_proj_val_l{l}", out_dtype=BF16, b_window=up_val)
        s["act"] = _ffn_act_fwd(s["up_g"], s["up_v"], ffn_w_full[l], n_batch, seq, f"ffn_act_l{l}")
        xc = _matmul(s["act"], wl["w_down"], "nn", f"down_proj_l{l}", res=s["x2"])
        saved.append(s)

    dx, dxb, dg_final, loss_part = _loss_head(xc, vec(final_norm_g), loss_target.reshape(t_dim, d_model), "loss_head")

    small = {"final_norm_g": dg_final.reshape(-1)}
    big = [dict() for _ in range(depth)]
    rs_state = {}
    rs_after = loss_part

    def rs_begin(l, part):
        pack = lax.empty((N_DEV, sum(nrows for _, _, nrows in layout[part]), d_model), BF16)
        row0 = 0
        for name, _, nrows in layout[part]:
            pieces = big[l][name] if isinstance(big[l][name], tuple) else (big[l][name],)
            d0 = 0
            for piece in pieces:
                blocks = _scatter_blocks(name, piece, p[name].shape[1:], d_model, N_DEV // len(pieces)).astype(BF16)
                pack = lax.dynamic_update_slice(pack, blocks, (d0, row0, 0))
                d0 += blocks.shape[0]
            row0 += nrows
        rs_state[(l, part)] = _reduce_scatter_begin(pack, f"rs{l}{part}")
        return rs_state[(l, part)][4]

    dmem_n = None
    for l in reversed(range(depth)):
        wl, s = full[l], saved[l]
        sm = {}
        dact = _matmul(dxb, wl["w_down"], "nt", f"d_act_l{l}", out_dtype=BF16, after=rs_after)
        big[l]["w_down"] = _matmul(s["act"], dxb, "tn", f"d_w_down_l{l}", out_dtype=BF16)
        dup_g, dup_v, dwf_g, dwf_v = _ffn_act_bwd(s["up_g"], s["up_v"], ffn_w_full[l], dact, n_batch, seq,
                                                  f"ffn_act_bwd_l{l}")
        sm["ffn_dw_w"] = jnp.concatenate([dwf_g, dwf_v], axis=1)
        dx, dxb, dg = _matmul_rmsnorm_bwd((dup_g, dup_v), wl["w_up"], "nn", s["x2"], vec(ffn_norm_g[l]), dx,
                                          f"d_hf_ffn_norm_bwd_l{l}")
        big[l]["w_up"] = (_matmul(dup_g, s["hf"], "tn", f"d_w_up_gate_l{l}", out_dtype=BF16),
                          _matmul(dup_v, s["hf"], "tn", f"d_w_up_val_l{l}", out_dtype=BF16))
        sm["ffn_norm_g"] = dg
        datt = _matmul(dxb, wl["w_o"], "nt", f"d_att_l{l}", out_dtype=BF16, after=rs_after)
        big[l]["w_o"] = _matmul(s["att"], dxb, "tn", f"d_w_o_l{l}", out_dtype=BF16)
        dq, dk, dv = _attn_bwd(s["q"], s["kv"], datt, n_batch, seq, m_len, f"attn_bwd_l{l}")
        dkv = jnp.concatenate([dk, dv], axis=1)
        big[l]["w_kv"] = _matmul(dkv, mem_n, "tn", f"d_w_kv_l{l}", out_dtype=BF16)
        dmem_n = _matmul(dkv, wl["w_kv"], "nn", f"d_mem_l{l}", res=dmem_n)
        big[l]["w_q"] = _matmul(s["hq"], dq, "tn", f"d_w_q_l{l}", out_dtype=BF16)
        dx, dxb, dg = _matmul_rmsnorm_bwd(dq, wl["w_q"], "nt", s["x1"], vec(xattn_norm_g[l]), dx,
                                          f"d_hq_xattn_norm_bwd_l{l}")
        sm["xattn_norm_g"] = dg
        if part_of[(l, "w_q")] != part_of[(l, "w_in")]:
            rs_after = rs_begin(l, part_of[(l, "w_q")])
        dmerged = _matmul(dxb, wl["w_out"], "nt", f"d_merged_l{l}", out_dtype=BF16, after=rs_after)
        big[l]["w_out"] = _matmul(s["merged"], dxb, "tn", f"d_w_out_l{l}", out_dtype=BF16)
        dgc, dgp, dyc, dyp, dscale = _merge_bwd(s["proj"], gate_col0, s["yc"], s["yp"], vec(pool_scale[l]), dmerged,
                                                f"merge_bwd_l{l}")
        sm["pool_scale"] = dscale
        dzp = _grouped(dyp, wl["w_pool_grp"], "nn", f"d_zp_l{l}", out_dtype=BF16)
        big[l]["w_pool_grp"] = _grouped_tn(dyp, s["zp"], n_groups, f"d_w_pool_l{l}")
        du = _pool_bwd(dzp, n_groups, n_batch, seq, f"pool_bwd_l{l}")
        dy3 = _matmul(dyc, wl["w_conv_out"], "nn", f"d_y3_l{l}", out_dtype=BF16)
        big[l]["w_conv_out"] = _matmul(dyc, s["y3"], "tn", f"d_w_conv_out_l{l}", out_dtype=BF16)
        dy1, dlg, dlb = _ln_silu_bwd(s["y1"], vec(conv_ln_g[l]), vec(conv_ln_b[l]), dy3, f"ln_silu_bwd_l{l}")
        sm["conv_ln_g"], sm["conv_ln_b"] = dlg, dlb
        da, dgl, dcw, dcb = _glu_conv_bwd(s["proj"], conv_w_full[l], dy1, n_batch, seq, f"glu_conv_bwd_l{l}")
        sm["conv_dw_w"], sm["conv_dw_b"] = dcw, dcb
        dproj = jnp.concatenate([da, dgl, du, dgc, dgp], axis=1)
        big[l]["w_in"] = _matmul(dproj, s["h"], "tn", f"d_w_in_l{l}", out_dtype=BF16)
        dx, dxb, dg = _matmul_rmsnorm_bwd(dproj, wl["w_in"], "nn", s["x0"], vec(mix_norm_g[l]), dx,
                                          f"d_h_mix_norm_bwd_l{l}")
        sm["mix_norm_g"] = dg
        for k, val in sm.items():
            small[(l, k)] = val.reshape(-1)
        rs_after = rs_begin(l, part_of[(l, "w_in")])
    _, _, dg_mem = _rmsnorm_bwd(mem2d, vec(mem_norm_g), dmem_n, None, "mem_norm_bwd")
    small["mem_norm_g"] = dg_mem.reshape(-1)
    small["loss"] = loss_part.reshape(-1)

    grads = {}
    per_layer = {name: [None] * depth for name, _ in MATRICES}
    for l in reversed(range(depth)):
        for part, _ in reversed(_parts(l)):
            mat_grads = _reduce_scatter_end(rs_state[(l, part)], rs_after, f"rs{l}{part}")
            row0 = 0
            for name, tr, nrows in layout[part]:
                per_layer[name][l] = _stored_form(name, tr, mat_grads[row0:row0 + nrows], p[name].shape[1:])
                row0 += nrows
    flip = lambda t: jnp.swapaxes(t, -1, -2)
    stored_grads = {name: jnp.stack(per_layer[name]) for name, _ in MATRICES}
    for name, tr in MATRICES:
        grads[name] = flip(stored_grads[name]) if tr else stored_grads[name]

    keys = list(small.keys())
    flat = jnp.concatenate([small[k] for k in keys])
    n_small = flat.shape[0]
    rows_small = -(-n_small // (SUBLANE * d_model)) * SUBLANE
    flat = jnp.pad(flat, (0, rows_small * d_model - n_small)).reshape(rows_small, d_model)
    every = _all_gather(flat, "small_all_gather")
    total = _sum_rows([every[i] for i in range(N_DEV)], F32, "small_sum").reshape(-1)
    off = 0
    red = {}
    for k in keys:
        red[k] = total[off:off + small[k].shape[0]]
        off += small[k].shape[0]
    loss = red["loss"][0]
    for name in ("mix_norm_g", "conv_dw_b", "conv_ln_g", "conv_ln_b", "pool_scale", "xattn_norm_g", "ffn_norm_g"):
        grads[name] = jnp.stack([red[(l, name)] for l in range(depth)])
    grads["conv_dw_w"] = jnp.stack([
        lax.dynamic_slice_in_dim(red[(l, "conv_dw_w")].reshape(kc, N_DEV * cs), dev * cs, cs, axis=1)
        for l in range(depth)])
    grads["ffn_dw_w"] = jnp.stack([
        lax.dynamic_slice_in_dim(red[(l, "ffn_dw_w")].reshape(kf, N_DEV * fs), dev * fs, fs, axis=1)
        for l in range(depth)])
    grads["mem_norm_g"] = red["mem_norm_g"]
    grads["final_norm_g"] = red["final_norm_g"]

    deltas, new_m, new_v = {}, {}, {}
    for name in weight_names:
        if transposed.get(name, False):
            out = _adamw(flip(p[name]), stored_grads[name], flip(p["m_" + name]), flip(p["v_" + name]), f"adamw_{name}")
            deltas[name], new_m[name], new_v[name] = (flip(t) for t in out)
        else:
            deltas[name], new_m[name], new_v[name] = _adamw(p[name], grads[name], p["m_" + name], p["v_" + name],
                                                            f"adamw_{name}")
    grad_x = dx.reshape(n_batch, seq, d_model)
    return (loss, grad_x, *[grads[n] for n in weight_names], *[deltas[n] for n in weight_names],
            *[new_m[n] for n in weight_names], *[new_v[n] for n in weight_names])
```

```python
import functools

import jax
import jax.numpy as jnp
from jax import lax
from jax.experimental import pallas as pl
from jax.experimental.pallas import tpu as pltpu

F32 = jnp.float32
BF16 = jnp.bfloat16
MESH = pl.DeviceIdType.MESH

N_DEV = 8
EPS = 1e-6
V7X_VMEM_BYTES = 64 * 1024 * 1024
VMEM_LIMIT = (V7X_VMEM_BYTES * 3) // 4
LANE = 128
SUBLANE = 8

CONV_HALO = 32
POOL_HALO = 16
FFN_HALO = 8
FFN_LANE_GROUPS = 2
POOL_WINDOW_MAX = 16
XA_HEADS = 4

ADAM_LR = 0.001
ADAM_B1 = 0.9
ADAM_B2 = 0.999
ADAM_EPS = 1e-08
ADAM_WD = 0.01
ADAM_STEP = 10

GELU_C0 = 0.7978845608028654
GELU_C1 = 0.044715


ANY_SPEC = pl.BlockSpec(memory_space=pl.ANY)


def _tile(n, cap, mult=LANE):
    if n <= cap:
        return n
    best = None
    for d in range(mult, cap + 1, mult):
        if n % d == 0:
            best = d
    assert best is not None, (n, cap, mult)
    return best


def _params(*sem):
    return pltpu.CompilerParams(dimension_semantics=sem, vmem_limit_bytes=VMEM_LIMIT)


def _delayed(x, halo, rows, n_shifts):
    for r in range(min(SUBLANE, n_shifts)):
        xr = x if r == 0 else pltpu.roll(x, r, 0)
        for s in range(r, n_shifts, SUBLANE):
            yield s, xr[halo - (s - r):halo - (s - r) + rows]


def _advanced(x, rows, n_shifts):
    for r in range(min(SUBLANE, n_shifts)):
        xr = x if r == 0 else pltpu.roll(x, x.shape[0] - r, 0)
        for s in range(r, n_shifts, SUBLANE):
            yield s, xr[s - r:s - r + rows]


def _sig(x):
    return 1.0 / (1.0 + jnp.exp(-x))


def _bs(shape, imap):
    return pl.BlockSpec(shape, imap)


def _mxu_tile(n, cap):
    if n <= cap:
        return n
    best = {mult: max((d for d in range(mult, cap + 1, mult) if n % d == 0), default=0) for mult in (2 * LANE, LANE)}
    assert best[LANE] > 0, (n, cap)
    return best[2 * LANE] if 2 * best[2 * LANE] >= best[LANE] else best[LANE]


def _matmul(a, b, mode, name, res=None, out_dtype=F32, after=None, b_window=None):
    b_row0, b_rows = b_window if b_window is not None else (0, b.shape[0])
    if mode == "tn":
        k_dim, m_dim = a.shape
        k2, n_dim = b_rows, b.shape[1]
    elif mode == "nn":
        m_dim, k_dim = a.shape
        k2, n_dim = b_rows, b.shape[1]
    else:
        m_dim, k_dim = a.shape
        n_dim, k2 = b_rows, b.shape[1]
    assert k_dim == k2, (name, a.shape, b.shape)
    size = lambda t: jnp.dtype(t).itemsize
    tn = _mxu_tile(n_dim, 2816)
    budget = VMEM_LIMIT - 8 * 1024 * 1024
    for tm_cap in ((1792,) if mode == "tn" else (2048, 1024)):
        tm = _mxu_tile(m_dim, tm_cap)
        fixed = tm * tn * (2 * size(out_dtype) + (2 * size(res.dtype) if res is not None else 0) + 4)
        if fixed + 2 * min(k_dim, 1024) * (tm * size(a.dtype) + tn * size(b.dtype)) <= budget:
            break
    for cap in (2816, 2048, 1792, 1024, 512):
        tk = _mxu_tile(k_dim, cap)
        if fixed + 2 * tk * (tm * size(a.dtype) + tn * size(b.dtype)) <= VMEM_LIMIT - 8 * 1024 * 1024:
            break
    nk = k_dim // tk
    use_acc = nk > 1 and out_dtype != F32
    if mode == "tn":
        a_spec, ca = _bs((tk, tm), lambda i, j, k: (k, i)), 0
    else:
        a_spec, ca = _bs((tm, tk), lambda i, j, k: (i, k)), 1
    if mode == "nt":
        assert b_row0 % tn == 0
        b_spec, cb = _bs((tn, tk), lambda i, j, k: (j + b_row0 // tn, k)), 1
    else:
        assert b_row0 % tk == 0
        b_spec, cb = _bs((tk, tn), lambda i, j, k: (k + b_row0 // tk, j)), 0
    dims = (((ca,), (cb,)), ((), ()))
    o_spec = _bs((tm, tn), lambda i, j, k: (i, j))
    has_res = res is not None

    def body(*refs):
        a_ref, b_ref = refs[:2]
        r_ref = refs[2] if has_res else None
        o_ref = refs[n_in]
        k = pl.program_id(2)
        part = lax.dot_general(a_ref[...].astype(BF16), b_ref[...].astype(BF16), dims,
                               preferred_element_type=F32)
        if nk == 1:
            if has_res:
                part = part + r_ref[...].astype(F32)
            o_ref[...] = part.astype(out_dtype)
            return
        acc = refs[-1] if use_acc else o_ref

        @pl.when(k == 0)
        def _():
            acc[...] = part + r_ref[...].astype(F32) if has_res else part

        @pl.when(k > 0)
        def _():
            acc[...] += part

        if use_acc:
            @pl.when(k == nk - 1)
            def _():
                o_ref[...] = acc[...].astype(out_dtype)

    in_specs = [a_spec, b_spec] + ([o_spec] if has_res else [])
    args = (a, b) + ((res,) if has_res else ())
    if after is not None:
        in_specs.append(ANY_SPEC)
        args += (after,)
    n_in = len(args)
    return pl.pallas_call(
        body, out_shape=jax.ShapeDtypeStruct((m_dim, n_dim), out_dtype),
        grid=(m_dim // tm, n_dim // tn, nk), in_specs=in_specs, out_specs=o_spec,
        scratch_shapes=[pltpu.VMEM((tm, tn), F32)] if use_acc else [], name=name,
        compiler_params=_params("parallel", "parallel", "arbitrary"))(*args)


def _grouped(a, w, mode, name, out_dtype=F32):
    t_dim = a.shape[0]
    g_dim, r_dim, c_dim = w.shape
    ka, no = (c_dim, r_dim) if mode == "nt" else (r_dim, c_dim)
    tm = _tile(t_dim, 4096)
    dims = (((1,), (1 if mode == "nt" else 0,)), ((), ()))

    def body(a_ref, w_ref, o_ref):
        o_ref[...] = lax.dot_general(a_ref[...].astype(BF16), w_ref[...].astype(BF16), dims,
                                     preferred_element_type=F32).astype(out_dtype)

    return pl.pallas_call(
        body, out_shape=jax.ShapeDtypeStruct((t_dim, g_dim * no), out_dtype),
        grid=(t_dim // tm, g_dim),
        in_specs=[_bs((tm, ka), lambda i, g: (i, g)), _bs((None, r_dim, c_dim), lambda i, g: (g, 0, 0))],
        out_specs=_bs((tm, no), lambda i, g: (i, g)), name=name,
        compiler_params=_params("parallel", "parallel"))(a, w)


def _grouped_tn(a, b, g_dim, name):
    t_dim = a.shape[0]
    ra = a.shape[1] // g_dim
    cb = b.shape[1] // g_dim
    tm = _tile(t_dim, 4096)
    nt = t_dim // tm

    def body(a_ref, b_ref, o_ref):
        part = lax.dot_general(a_ref[...].astype(BF16), b_ref[...].astype(BF16), (((0,), (0,)), ((), ())),
                               preferred_element_type=F32)

        @pl.when(pl.program_id(1) == 0)
        def _():
            o_ref[...] = part

        @pl.when(pl.program_id(1) > 0)
        def _():
            o_ref[...] += part

    return pl.pallas_call(
        body, out_shape=jax.ShapeDtypeStruct((g_dim, ra, cb), F32), grid=(g_dim, nt),
        in_specs=[_bs((tm, ra), lambda g, i: (i, g)), _bs((tm, cb), lambda g, i: (i, g))],
        out_specs=_bs((None, ra, cb), lambda g, i: (g, 0, 0)), name=name,
        compiler_params=_params("parallel", "arbitrary"))(a, b)


def _rmsnorm_fwd(x, g, name, after=None):
    t_dim, d = x.shape
    tm = _tile(t_dim, 1024)

    def body(x_ref, g_ref, *rest):
        o_ref = rest[-1]
        xv = x_ref[...]
        r = lax.rsqrt(jnp.mean(xv * xv, axis=-1, keepdims=True) + EPS)
        o_ref[...] = (xv * r * g_ref[...]).astype(BF16)

    return pl.pallas_call(
        body, out_shape=jax.ShapeDtypeStruct((t_dim, d), BF16), grid=(t_dim // tm,),
        in_specs=[_bs((tm, d), lambda i: (i, 0)), _bs((1, d), lambda i: (0, 0))] + ([ANY_SPEC] if after is not None else []),
        out_specs=_bs((tm, d), lambda i: (i, 0)), name=name,
        compiler_params=_params("parallel"))(x, g, *([after] if after is not None else []))


def _rmsnorm_bwd(x, g, dh, dx_in, name):
    t_dim, d = x.shape
    tm = _tile(t_dim, 512)
    has_in = dx_in is not None

    def body(*refs):
        if has_in:
            x_ref, g_ref, dh_ref, di_ref, dx_ref, dxb_ref, dg_ref = refs
        else:
            x_ref, g_ref, dh_ref, dx_ref, dxb_ref, dg_ref = refs
        xv = x_ref[...]
        r = lax.rsqrt(jnp.mean(xv * xv, axis=-1, keepdims=True) + EPS)
        xh = xv * r
        dhv = dh_ref[...].astype(F32)
        dxh = dhv * g_ref[...]
        dx = r * (dxh - xh * jnp.mean(dxh * xh, axis=-1, keepdims=True))
        if has_in:
            dx = dx + di_ref[...]
        dx_ref[...] = dx
        dxb_ref[...] = dx.astype(BF16)
        part = jnp.sum(dhv * xh, axis=0, keepdims=True)

        @pl.when(pl.program_id(0) == 0)
        def _():
            dg_ref[...] = part

        @pl.when(pl.program_id(0) > 0)
        def _():
            dg_ref[...] += part

    row = _bs((tm, d), lambda i: (i, 0))
    vec = _bs((1, d), lambda i: (0, 0))
    args = (x, g, dh) + ((dx_in,) if has_in else ())
    return pl.pallas_call(
        body, out_shape=(jax.ShapeDtypeStruct((t_dim, d), F32), jax.ShapeDtypeStruct((t_dim, d), BF16),
                         jax.ShapeDtypeStruct((1, d), F32)),
        grid=(t_dim // tm,), in_specs=[row, vec, row] + ([row] if has_in else []),
        out_specs=(row, row, vec), name=name, compiler_params=_params("arbitrary"))(*args)


def _matmul_rmsnorm_bwd(a, b, mode, x, g, dx_in, name, res=None, b_window=None):
    pieces = a if isinstance(a, tuple) else (a,)
    n_p = len(pieces)
    b_row0, b_rows = b_window if b_window is not None else (0, b.shape[0])
    m_dim, k_piece = pieces[0].shape
    assert all(t.shape == pieces[0].shape for t in pieces)
    k_dim = n_p * k_piece
    d = x.shape[1]
    assert (b_rows, b.shape[1]) == ((k_dim, d) if mode == "nn" else (d, k_dim)), (name, pieces[0].shape, b.shape)
    tm = _mxu_tile(m_dim, 512)
    tk = _mxu_tile(k_piece, 1792)
    nkp = k_piece // tk
    nk = n_p * nkp
    has_res = res is not None
    if mode == "nt":
        assert b_row0 == 0
        b_spec, cb = _bs((d, tk), lambda i, k: (0, k)), 1
    else:
        assert b_row0 % tk == 0
        b_spec, cb = _bs((tk, d), lambda i, k: (k + b_row0 // tk, 0)), 0
    dims = (((1,), (cb,)), ((), ()))

    def body(*refs):
        b_ref = refs[n_p]
        r_ref = refs[n_p + 1] if has_res else None
        x_ref, g_ref, di_ref, dx_ref, dxb_ref, dg_ref = refs[n_p + 1 + has_res:n_p + 7 + has_res]
        i, k = pl.program_id(0), pl.program_id(1)

        def finish(dhv):
            if has_res:
                dhv = dhv + r_ref[...].astype(F32)
            xv = x_ref[...]
            r = lax.rsqrt(jnp.mean(xv * xv, axis=-1, keepdims=True) + EPS)
            xh = xv * r
            dxh = dhv * g_ref[...]
            dx = r * (dxh - xh * jnp.mean(dxh * xh, axis=-1, keepdims=True)) + di_ref[...]
            dx_ref[...] = dx
            dxb_ref[...] = dx.astype(BF16)
            dg_part = jnp.sum(dhv * xh, axis=0, keepdims=True)

            @pl.when(i == 0)
            def _():
                dg_ref[...] = dg_part

            @pl.when(i > 0)
            def _():
                dg_ref[...] += dg_part

        def step(a_ref):
            part = lax.dot_general(a_ref[...].astype(BF16), b_ref[...].astype(BF16), dims, preferred_element_type=F32)
            if nk == 1:
                finish(part)
                return
            acc = refs[-1]

            @pl.when(k == 0)
            def _():
                acc[...] = part

            @pl.when(jnp.logical_and(k > 0, k < nk - 1))
            def _():
                acc[...] += part

            @pl.when(k == nk - 1)
            def _():
                finish(acc[...] + part)

        if n_p == 1:
            step(refs[0])
        else:
            for q in range(n_p):
                pl.when(jnp.logical_and(k >= q * nkp, k < (q + 1) * nkp))(functools.partial(step, refs[q]))

    row = _bs((tm, d), lambda i, k: (i, 0))
    vec = _bs((1, d), lambda i, k: (0, 0))
    a_specs = [_bs((tm, tk), lambda i, k, q=q: (i, jnp.clip(k - q * nkp, 0, nkp - 1))) for q in range(n_p)]
    in_specs = a_specs + [b_spec] + ([row] if has_res else []) + [row, vec, row]
    args = pieces + (b,) + ((res,) if has_res else ()) + (x, g, dx_in)
    return pl.pallas_call(
        body, out_shape=(jax.ShapeDtypeStruct((m_dim, d), F32), jax.ShapeDtypeStruct((m_dim, d), BF16),
                         jax.ShapeDtypeStruct((1, d), F32)),
        grid=(m_dim // tm, nk), in_specs=in_specs, out_specs=(row, row, vec),
        scratch_shapes=[pltpu.VMEM((tm, d), F32)] if nk > 1 else [], name=name,
        compiler_params=_params("arbitrary", "arbitrary"))(*args)


def _loss_head(x, g, tgt, name):
    t_dim, d = x.shape
    tm = _tile(t_dim, 512)

    def body(x_ref, g_ref, t_ref, dx_ref, dxb_ref, dg_ref, loss_ref):
        xv = x_ref[...]
        gv = g_ref[...]
        r = lax.rsqrt(jnp.mean(xv * xv, axis=-1, keepdims=True) + EPS)
        xh = xv * r
        err = xh * gv - t_ref[...]
        dy = err * (1.0 / d)
        dxh = dy * gv
        dx = r * (dxh - xh * jnp.mean(dxh * xh, axis=-1, keepdims=True))
        dx_ref[...] = dx
        dxb_ref[...] = dx.astype(BF16)
        dg_part = jnp.sum(dy * xh, axis=0, keepdims=True)
        loss_part = jnp.full((1, LANE), 0.5 * jnp.sum(jnp.mean(err * err, axis=-1, keepdims=True)), F32)

        @pl.when(pl.program_id(0) == 0)
        def _():
            dg_ref[...] = dg_part
            loss_ref[...] = loss_part

        @pl.when(pl.program_id(0) > 0)
        def _():
            dg_ref[...] += dg_part
            loss_ref[...] += loss_part

    row = _bs((tm, d), lambda i: (i, 0))
    vec = _bs((1, d), lambda i: (0, 0))
    return pl.pallas_call(
        body, out_shape=(jax.ShapeDtypeStruct((t_dim, d), F32), jax.ShapeDtypeStruct((t_dim, d), BF16),
                         jax.ShapeDtypeStruct((1, d), F32), jax.ShapeDtypeStruct((1, LANE), F32)),
        grid=(t_dim // tm,), in_specs=[row, vec, row],
        out_specs=(row, row, vec, _bs((1, LANE), lambda i: (0, 0))), name=name,
        compiler_params=_params("arbitrary"))(x, g, tgt)


def _glu_conv_fwd(proj, dw_w, dw_b, n_batch, seq, name):
    kk, cc = dw_w.shape
    nj = cc // LANE
    ch = min(256, seq)

    def body(a_ref, gl_ref, w_ref, b_ref, o_ref, pad):
        pad[0:CONV_HALO, :] = jnp.zeros((CONV_HALO, LANE), F32)
        pad[CONV_HALO:CONV_HALO + seq, :] = a_ref[...].astype(F32) * _sig(gl_ref[...].astype(F32))
        for c0 in range(0, seq, ch):
            acc = jnp.broadcast_to(b_ref[...], (ch, LANE))
            for k in range(kk):
                acc = acc + w_ref[k:k + 1, :] * pad[pl.ds(c0 + CONV_HALO - (kk - 1) + k, ch), :]
            o_ref[c0:c0 + ch, :] = acc

    return pl.pallas_call(
        body, out_shape=jax.ShapeDtypeStruct((n_batch * seq, cc), F32), grid=(n_batch, nj),
        in_specs=[_bs((seq, LANE), lambda b, j: (b, j)), _bs((seq, LANE), lambda b, j: (b, nj + j)),
                  _bs((kk, LANE), lambda b, j: (0, j)), _bs((1, LANE), lambda b, j: (0, j))],
        out_specs=_bs((seq, LANE), lambda b, j: (b, j)),
        scratch_shapes=[pltpu.VMEM((seq + CONV_HALO, LANE), F32)], name=name,
        compiler_params=_params("parallel", "parallel"))(proj, proj, dw_w, dw_b)


def _glu_conv_bwd(proj, dw_w, dy1, n_batch, seq, name):
    kk, cc = dw_w.shape
    nj = cc // LANE
    ch = min(256, seq)

    def body(a_ref, gl_ref, dy_ref, w_ref, da_ref, dgl_ref, dw_ref, db_ref, padf, padb):
        first = pl.program_id(1) == 0
        padf[0:CONV_HALO, :] = jnp.zeros((CONV_HALO, LANE), F32)
        padf[CONV_HALO:CONV_HALO + seq, :] = a_ref[...].astype(F32) * _sig(gl_ref[...].astype(F32))
        padb[0:seq, :] = dy_ref[...]
        padb[seq:seq + CONV_HALO, :] = jnp.zeros((CONV_HALO, LANE), F32)

        @pl.when(first)
        def _():
            dw_ref[...] = jnp.zeros((kk, LANE), F32)
            db_ref[...] = jnp.zeros((1, LANE), F32)

        dws = [jnp.zeros((1, LANE), F32) for _ in range(kk)]
        for c0 in range(0, seq, ch):
            acc = jnp.zeros((ch, LANE), F32)
            y0 = padf[CONV_HALO + c0:CONV_HALO + c0 + ch, :]
            for k in range(kk):
                win = padb[pl.ds(c0 + (kk - 1) - k, ch), :]
                acc = acc + w_ref[k:k + 1, :] * win
                dws[k] = dws[k] + jnp.sum(win * y0, axis=0, keepdims=True)
            sg = _sig(gl_ref[c0:c0 + ch, :].astype(F32))
            da_ref[c0:c0 + ch, :] = (acc * sg).astype(BF16)
            dgl_ref[c0:c0 + ch, :] = (acc * a_ref[c0:c0 + ch, :].astype(F32) * sg * (1.0 - sg)).astype(BF16)
        for k in range(kk):
            dw_ref[k:k + 1, :] += dws[k]
        db_ref[...] += jnp.sum(dy_ref[...], axis=0, keepdims=True)

    tok = _bs((seq, LANE), lambda j, b: (b, j))
    t_dim = n_batch * seq
    return pl.pallas_call(
        body, out_shape=(jax.ShapeDtypeStruct((t_dim, cc), BF16), jax.ShapeDtypeStruct((t_dim, cc), BF16),
                         jax.ShapeDtypeStruct((kk, cc), F32), jax.ShapeDtypeStruct((1, cc), F32)),
        grid=(nj, n_batch),
        in_specs=[tok, _bs((seq, LANE), lambda j, b: (b, nj + j)), tok, _bs((kk, LANE), lambda j, b: (0, j))],
        out_specs=(tok, tok, _bs((kk, LANE), lambda j, b: (0, j)), _bs((1, LANE), lambda j, b: (0, j))),
        scratch_shapes=[pltpu.VMEM((seq + CONV_HALO, LANE), F32), pltpu.VMEM((seq + CONV_HALO, LANE), F32)],
        name=name, compiler_params=_params("parallel", "arbitrary"))(proj, proj, dy1, dw_w)


def _ln_silu_fwd(y1, g, b, name):
    t_dim, c = y1.shape
    tm = _tile(t_dim, 1024)

    def body(y_ref, g_ref, b_ref, o_ref):
        yv = y_ref[...]
        xc = yv - jnp.mean(yv, axis=-1, keepdims=True)
        rstd = lax.rsqrt(jnp.mean(xc * xc, axis=-1, keepdims=True) + EPS)
        y2 = xc * rstd * g_ref[...] + b_ref[...]
        o_ref[...] = (y2 * _sig(y2)).astype(BF16)

    row = _bs((tm, c), lambda i: (i, 0))
    vec = _bs((1, c), lambda i: (0, 0))
    return pl.pallas_call(
        body, out_shape=jax.ShapeDtypeStruct((t_dim, c), BF16), grid=(t_dim // tm,),
        in_specs=[row, vec, vec], out_specs=row, name=name, compiler_params=_params("parallel"))(y1, g, b)


def _ln_silu_bwd(y1, g, b, dy3, name):
    t_dim, c = y1.shape
    tm = _tile(t_dim, 1024)

    def body(y_ref, g_ref, b_ref, d_ref, dy_ref, dg_ref, db_ref):
        yv = y_ref[...]
        gv = g_ref[...]
        xc = yv - jnp.mean(yv, axis=-1, keepdims=True)
        rstd = lax.rsqrt(jnp.mean(xc * xc, axis=-1, keepdims=True) + EPS)
        yh = xc * rstd
        y2 = yh * gv + b_ref[...]
        s = _sig(y2)
        dy2 = d_ref[...].astype(F32) * (s * (1.0 + y2 * (1.0 - s)))
        dyh = dy2 * gv
        dy_ref[...] = rstd * (dyh - jnp.mean(dyh, axis=-1, keepdims=True)
                              - yh * jnp.mean(dyh * yh, axis=-1, keepdims=True))
        dg_part = jnp.sum(dy2 * yh, axis=0, keepdims=True)
        db_part = jnp.sum(dy2, axis=0, keepdims=True)

        @pl.when(pl.program_id(0) == 0)
        def _():
            dg_ref[...] = dg_part
            db_ref[...] = db_part

        @pl.when(pl.program_id(0) > 0)
        def _():
            dg_ref[...] += dg_part
            db_ref[...] += db_part

    row = _bs((tm, c), lambda i: (i, 0))
    vec = _bs((1, c), lambda i: (0, 0))
    return pl.pallas_call(
        body, out_shape=(jax.ShapeDtypeStruct((t_dim, c), F32), jax.ShapeDtypeStruct((1, c), F32),
                         jax.ShapeDtypeStruct((1, c), F32)),
        grid=(t_dim // tm,), in_specs=[row, vec, vec, row], out_specs=(row, vec, vec), name=name,
        compiler_params=_params("arbitrary"))(y1, g, b, dy3)


def _pool_fwd(proj, col0, n_groups, n_batch, seq, name):
    ch = min(256, seq)

    def body(u_ref, o_ref, pad):
        w = lax.shift_left(jnp.int32(2), pl.program_id(1))
        pad[0:POOL_HALO, :] = jnp.zeros((POOL_HALO, LANE), F32)
        pad[POOL_HALO:POOL_HALO + seq, :] = u_ref[...].astype(F32)
        for c0 in range(0, seq, ch):
            acc = jnp.zeros((ch, LANE), F32)
            for j in range(POOL_WINDOW_MAX):
                acc = acc + jnp.where(j < w, 1.0, 0.0).astype(F32) * pad[pl.ds(c0 + POOL_HALO - j, ch), :]
            t = c0 + lax.broadcasted_iota(jnp.int32, (ch, LANE), 0)
            cnt = jnp.minimum(t + 1, w).astype(F32)
            o_ref[c0:c0 + ch, :] = (acc / cnt - pad[POOL_HALO + c0:POOL_HALO + c0 + ch, :]).astype(BF16)

    return pl.pallas_call(
        body, out_shape=jax.ShapeDtypeStruct((n_batch * seq, n_groups * LANE), BF16), grid=(n_batch, n_groups),
        in_specs=[_bs((seq, LANE), lambda b, g: (b, col0 + g))], out_specs=_bs((seq, LANE), lambda b, g: (b, g)),
        scratch_shapes=[pltpu.VMEM((seq + POOL_HALO, LANE), F32)], name=name,
        compiler_params=_params("parallel", "parallel"))(proj)


def _pool_bwd(dzp, n_groups, n_batch, seq, name):
    ch = min(256, seq)

    def body(d_ref, o_ref, pad):
        w = lax.shift_left(jnp.int32(2), pl.program_id(1))
        for c0 in range(0, seq, ch):
            t = c0 + lax.broadcasted_iota(jnp.int32, (ch, LANE), 0)
            cnt = jnp.minimum(t + 1, w).astype(F32)
            pad[c0:c0 + ch, :] = d_ref[c0:c0 + ch, :].astype(F32) / cnt
        pad[seq:seq + POOL_HALO, :] = jnp.zeros((POOL_HALO, LANE), F32)
        for c0 in range(0, seq, ch):
            acc = jnp.zeros((ch, LANE), F32)
            for j in range(POOL_WINDOW_MAX):
                acc = acc + jnp.where(j < w, 1.0, 0.0).astype(F32) * pad[pl.ds(c0 + j, ch), :]
            o_ref[c0:c0 + ch, :] = (acc - d_ref[c0:c0 + ch, :].astype(F32)).astype(BF16)

    tok = _bs((seq, LANE), lambda b, g: (b, g))
    return pl.pallas_call(
        body, out_shape=jax.ShapeDtypeStruct((n_batch * seq, n_groups * LANE), BF16), grid=(n_batch, n_groups),
        in_specs=[tok], out_specs=tok, scratch_shapes=[pltpu.VMEM((seq + POOL_HALO, LANE), F32)], name=name,
        compiler_params=_params("parallel", "parallel"))(dzp)


def _merge_fwd(proj, col0, yc, yp, scale, name):
    t_dim, d = yc.shape
    half = d // 2
    tm = _tile(t_dim, 1024)
    c0 = col0 // half

    def body(gc_ref, gp_ref, yc_ref, yp_ref, s_ref, o_ref):
        f32 = lambda r: r[...].astype(F32)
        o_ref[...] = (_sig(f32(gc_ref)) * f32(yc_ref) + _sig(f32(gp_ref)) * (f32(yp_ref) * s_ref[...])).astype(BF16)

    blk = _bs((tm, half), lambda i, j: (i, j))
    return pl.pallas_call(
        body, out_shape=jax.ShapeDtypeStruct((t_dim, d), BF16), grid=(t_dim // tm, 2),
        in_specs=[_bs((tm, half), lambda i, j: (i, c0 + j)), _bs((tm, half), lambda i, j: (i, c0 + 2 + j)),
                  blk, blk, _bs((1, half), lambda i, j: (0, j))],
        out_specs=blk, name=name, compiler_params=_params("parallel", "parallel"))(proj, proj, yc, yp, scale)


def _merge_bwd(proj, col0, yc, yp, scale, dm, name):
    t_dim, d = yc.shape
    half = d // 2
    tm = _tile(t_dim, 1024)
    c0 = col0 // half

    def body(gc_ref, gp_ref, yc_ref, yp_ref, s_ref, dm_ref, dgc_ref, dgp_ref, dyc_ref, dyp_ref, ds_ref):
        dmv = dm_ref[...].astype(F32)
        sgc = _sig(gc_ref[...].astype(F32))
        sgp = _sig(gp_ref[...].astype(F32))
        sv = s_ref[...]
        ypre = yp_ref[...].astype(F32)
        dgc_ref[...] = (dmv * yc_ref[...].astype(F32) * sgc * (1.0 - sgc)).astype(BF16)
        dgp_ref[...] = (dmv * (ypre * sv) * sgp * (1.0 - sgp)).astype(BF16)
        dyc_ref[...] = (dmv * sgc).astype(BF16)
        dyp = dmv * sgp
        dyp_ref[...] = (dyp * sv).astype(BF16)
        part = jnp.sum(dyp * ypre, axis=0, keepdims=True)

        @pl.when(pl.program_id(1) == 0)
        def _():
            ds_ref[...] = part

        @pl.when(pl.program_id(1) > 0)
        def _():
            ds_ref[...] += part

    blk = _bs((tm, half), lambda j, i: (i, j))
    big = jax.ShapeDtypeStruct((t_dim, d), BF16)
    return pl.pallas_call(
        body, out_shape=(big, big, big, big, jax.ShapeDtypeStruct((1, d), F32)), grid=(2, t_dim // tm),
        in_specs=[_bs((tm, half), lambda j, i: (i, c0 + j)), _bs((tm, half), lambda j, i: (i, c0 + 2 + j)),
                  blk, blk, _bs((1, half), lambda j, i: (0, j)), blk],
        out_specs=(blk, blk, blk, blk, _bs((1, half), lambda j, i: (0, j))), name=name,
        compiler_params=_params("parallel", "arbitrary"))(proj, proj, yc, yp, scale, dm)


def _attn_fwd(q, kv, n_batch, seq, m_len, name):
    d = q.shape[1]
    hd = d // XA_HEADS
    tq = _tile(seq, 2048)
    nq = seq // tq
    scale = hd ** -0.5

    def body(q_ref, k_ref, v_ref, o_ref):
        sc = lax.dot_general(q_ref[...].astype(BF16), k_ref[...].astype(BF16), (((1,), (1,)), ((), ())),
                             preferred_element_type=F32) * scale
        p = jnp.exp(sc - jnp.max(sc, axis=-1, keepdims=True))
        pr = p / jnp.sum(p, axis=-1, keepdims=True)
        o_ref[...] = jnp.dot(pr.astype(BF16), v_ref[...].astype(BF16), preferred_element_type=F32).astype(BF16)

    return pl.pallas_call(
        body, out_shape=jax.ShapeDtypeStruct((n_batch * seq, d), BF16), grid=(n_batch, XA_HEADS, nq),
        in_specs=[_bs((tq, hd), lambda b, h, i: (b * nq + i, h)), _bs((m_len, hd), lambda b, h, i: (b, h)),
                  _bs((m_len, hd), lambda b, h, i: (b, XA_HEADS + h))],
        out_specs=_bs((tq, hd), lambda b, h, i: (b * nq + i, h)), name=name,
        compiler_params=_params("parallel", "parallel", "parallel"))(q, kv, kv)


def _attn_bwd(q, kv, datt, n_batch, seq, m_len, name):
    d = q.shape[1]
    hd = d // XA_HEADS
    tq = _tile(seq, 2048)
    nq = seq // tq
    scale = hd ** -0.5

    def body(q_ref, k_ref, v_ref, do_ref, dq_ref, dk_ref, dv_ref):
        qb = q_ref[...].astype(BF16)
        kb = k_ref[...].astype(BF16)
        vb = v_ref[...].astype(BF16)
        dob = do_ref[...].astype(BF16)
        sc = lax.dot_general(qb, kb, (((1,), (1,)), ((), ())), preferred_element_type=F32) * scale
        p = jnp.exp(sc - jnp.max(sc, axis=-1, keepdims=True))
        pr = p / jnp.sum(p, axis=-1, keepdims=True)
        dpr = lax.dot_general(dob, vb, (((1,), (1,)), ((), ())), preferred_element_type=F32)
        dsc = pr * (dpr - jnp.sum(dpr * pr, axis=-1, keepdims=True)) * scale
        dsb = dsc.astype(BF16)
        dq_ref[...] = jnp.dot(dsb, kb, preferred_element_type=F32).astype(BF16)
        dv_part = lax.dot_general(pr.astype(BF16), dob, (((0,), (0,)), ((), ())), preferred_element_type=F32)
        dk_part = lax.dot_general(dsb, qb, (((0,), (0,)), ((), ())), preferred_element_type=F32)

        @pl.when(pl.program_id(2) == 0)
        def _():
            dk_ref[...] = dk_part
            dv_ref[...] = dv_part

        @pl.when(pl.program_id(2) > 0)
        def _():
            dk_ref[...] += dk_part
            dv_ref[...] += dv_part

    qs = _bs((tq, hd), lambda b, h, i: (b * nq + i, h))
    ks = _bs((m_len, hd), lambda b, h, i: (b, h))
    return pl.pallas_call(
        body, out_shape=(jax.ShapeDtypeStruct((n_batch * seq, d), BF16), jax.ShapeDtypeStruct((n_batch * m_len, d), F32),
                         jax.ShapeDtypeStruct((n_batch * m_len, d), F32)),
        grid=(n_batch, XA_HEADS, nq),
        in_specs=[qs, ks, _bs((m_len, hd), lambda b, h, i: (b, XA_HEADS + h)), qs],
        out_specs=(qs, ks, ks), name=name,
        compiler_params=_params("parallel", "parallel", "arbitrary"))(q, kv, kv, datt)


def _gelu_parts(g):
    th = jnp.tanh(GELU_C0 * (g + GELU_C1 * g * g * g))
    return th, 0.5 * g * (1.0 + th)


def _ffn_act_fwd(up_g, up_v, dw_w, n_batch, seq, name):
    kk, c2 = dw_w.shape
    f_dim = c2 // 2
    wd = FFN_LANE_GROUPS * LANE
    nj = f_dim // wd
    ch = min(128, seq)

    def body(g_ref, v_ref, wg_ref, wv_ref, o_ref, padg, padv):
        for h in range(FFN_LANE_GROUPS):
            lanes = slice(h * LANE, (h + 1) * LANE)
            for pad, src in ((padg, g_ref), (padv, v_ref)):
                pad[h, 0:FFN_HALO, :] = jnp.zeros((FFN_HALO, LANE), F32)
                pad[h, FFN_HALO:FFN_HALO + seq, :] = src[:, lanes].astype(F32)
            for c0 in range(0, seq, ch):
                gate = jnp.zeros((ch, LANE), F32)
                val = jnp.zeros((ch, LANE), F32)
                for k in range(kk):
                    off = c0 + FFN_HALO - (kk - 1) + k
                    gate = gate + wg_ref[k:k + 1, lanes] * padg[h, pl.ds(off, ch), :]
                    val = val + wv_ref[k:k + 1, lanes] * padv[h, pl.ds(off, ch), :]
                o_ref[c0:c0 + ch, lanes] = (_gelu_parts(gate)[1] * val).astype(BF16)

    pad_shape = pltpu.VMEM((FFN_LANE_GROUPS, seq + FFN_HALO, LANE), F32)
    return pl.pallas_call(
        body, out_shape=jax.ShapeDtypeStruct((n_batch * seq, f_dim), BF16), grid=(n_batch, nj),
        in_specs=[_bs((seq, wd), lambda b, j: (b, j)), _bs((seq, wd), lambda b, j: (b, j)),
                  _bs((kk, wd), lambda b, j: (0, j)), _bs((kk, wd), lambda b, j: (0, nj + j))],
        out_specs=_bs((seq, wd), lambda b, j: (b, j)), scratch_shapes=[pad_shape, pad_shape], name=name,
        compiler_params=_params("parallel", "parallel"))(up_g, up_v, dw_w, dw_w)


def _ffn_act_bwd(up_g, up_v, dw_w, dact, n_batch, seq, name):
    kk, c2 = dw_w.shape
    f_dim = c2 // 2
    wd = FFN_LANE_GROUPS * LANE
    nj = f_dim // wd
    ch = min(128, seq)

    def body(g_ref, v_ref, wg_ref, wv_ref, da_ref, dg_ref, dv_ref, dwg_ref, dwv_ref, padg, padv, pbg, pbv):
        @pl.when(pl.program_id(1) == 0)
        def _():
            dwg_ref[...] = jnp.zeros((kk, wd), F32)
            dwv_ref[...] = jnp.zeros((kk, wd), F32)

        for h in range(FFN_LANE_GROUPS):
            lanes = slice(h * LANE, (h + 1) * LANE)
            for pad, src in ((padg, g_ref), (padv, v_ref)):
                pad[h, 0:FFN_HALO, :] = jnp.zeros((FFN_HALO, LANE), F32)
                pad[h, FFN_HALO:FFN_HALO + seq, :] = src[:, lanes].astype(F32)
            for pb in (pbg, pbv):
                pb[h, seq:seq + FFN_HALO, :] = jnp.zeros((FFN_HALO, LANE), F32)
            for c0 in range(0, seq, ch):
                gate = jnp.zeros((ch, LANE), F32)
                val = jnp.zeros((ch, LANE), F32)
                for k in range(kk):
                    off = c0 + FFN_HALO - (kk - 1) + k
                    gate = gate + wg_ref[k:k + 1, lanes] * padg[h, pl.ds(off, ch), :]
                    val = val + wv_ref[k:k + 1, lanes] * padv[h, pl.ds(off, ch), :]
                sq = gate * gate
                th = jnp.tanh(GELU_C0 * gate * (1.0 + GELU_C1 * sq))
                half = 0.5 * th + 0.5
                dgelu = half * (1.0 + gate * (GELU_C0 + 3.0 * GELU_C0 * GELU_C1 * sq) * (1.0 - th))
                dav = da_ref[c0:c0 + ch, lanes].astype(F32)
                pbg[h, c0:c0 + ch, :] = dav * val * dgelu
                pbv[h, c0:c0 + ch, :] = dav * (gate * half)
            for pb, pad, w_ref, d_ref, dw_ref in ((pbg, padg, wg_ref, dg_ref, dwg_ref), (pbv, padv, wv_ref, dv_ref, dwv_ref)):
                for c0 in range(0, seq, ch):
                    acc = jnp.zeros((ch, LANE), F32)
                    for k in range(kk):
                        acc = acc + w_ref[k:k + 1, lanes] * pb[h, pl.ds(c0 + (kk - 1) - k, ch), :]
                    d_ref[c0:c0 + ch, lanes] = acc.astype(BF16)
                for k in range(kk):
                    s = jnp.zeros((1, LANE), F32)
                    for c0 in range(0, seq, ch):
                        s = s + jnp.sum(pb[h, c0:c0 + ch, :] * pad[h, pl.ds(c0 + FFN_HALO - (kk - 1) + k, ch), :],
                                        axis=0, keepdims=True)
                    dw_ref[k:k + 1, lanes] += s

    t_dim = n_batch * seq
    tok = _bs((seq, wd), lambda j, b: (b, j))
    wblk = _bs((kk, wd), lambda j, b: (0, j))
    pad_shape = pltpu.VMEM((FFN_LANE_GROUPS, seq + FFN_HALO, LANE), F32)
    return pl.pallas_call(
        body, out_shape=(jax.ShapeDtypeStruct((t_dim, f_dim), BF16), jax.ShapeDtypeStruct((t_dim, f_dim), BF16),
                         jax.ShapeDtypeStruct((kk, f_dim), F32), jax.ShapeDtypeStruct((kk, f_dim), F32)),
        grid=(nj, n_batch),
        in_specs=[tok, tok, wblk, _bs((kk, wd), lambda j, b: (0, nj + j)), tok],
        out_specs=(tok, tok, wblk, wblk), scratch_shapes=[pad_shape, pad_shape, pad_shape, pad_shape], name=name,
        compiler_params=_params("parallel", "arbitrary"))(up_g, up_v, dw_w, dw_w, dact)


def _sum_rows(parts, out_dtype, name):
    r_dim, c_dim = parts[0].shape
    tr = _tile(r_dim, 1200, SUBLANE)
    n = len(parts)

    def body(*refs):
        acc = refs[0][...].astype(F32)
        for r in refs[1:n]:
            acc = acc + r[...].astype(F32)
        refs[n][...] = acc.astype(out_dtype)

    blk = _bs((tr, c_dim), lambda i: (i, 0))
    return pl.pallas_call(
        body, out_shape=jax.ShapeDtypeStruct((r_dim, c_dim), out_dtype), grid=(r_dim // tr,),
        in_specs=[blk] * n, out_specs=blk, name=name, compiler_params=_params("parallel"))(*parts)


def _adamw(w, g, m, v, name):
    shape = w.shape
    c_dim = shape[-1]
    r_dim = w.size // c_dim
    two_d = lambda t: t.reshape(r_dim, c_dim)
    tr = _tile(r_dim, max(SUBLANE, (512 * 1024) // max(c_dim, LANE) // SUBLANE * SUBLANE), SUBLANE)
    c1 = 1.0 - ADAM_B1 ** ADAM_STEP
    c2 = 1.0 - ADAM_B2 ** ADAM_STEP

    def body(w_ref, g_ref, m_ref, v_ref, d_ref, mo_ref, vo_ref):
        gv = g_ref[...]
        mn = ADAM_B1 * m_ref[...] + (1.0 - ADAM_B1) * gv
        vn = ADAM_B2 * v_ref[...] + (1.0 - ADAM_B2) * (gv * gv)
        mo_ref[...] = mn
        vo_ref[...] = vn
        d_ref[...] = -ADAM_LR * ((mn / c1) / (jnp.sqrt(vn / c2) + ADAM_EPS) + ADAM_WD * w_ref[...])

    blk = _bs((tr, c_dim), lambda i: (i, 0))
    out = jax.ShapeDtypeStruct((r_dim, c_dim), F32)
    d, mo, vo = pl.pallas_call(
        body, out_shape=(out, out, out), grid=(r_dim // tr,), in_specs=[blk] * 4, out_specs=(blk, blk, blk),
        name=name, compiler_params=_params("parallel"))(two_d(w), two_d(g), two_d(m), two_d(v))
    return d.reshape(shape), mo.reshape(shape), vo.reshape(shape)


HBM_SPEC = pl.BlockSpec(memory_space=pltpu.HBM)


def _position():
    return lax.axis_index("x"), lax.axis_index("y"), lax.axis_index("c")


def _all_gather(shard, name):
    def body(x_ref, out_ref, send_sems, recv_sems, local_sem):
        x, y, c = _position()
        me, sibling = (x, y, c), (x, y, 1 - c)
        chips = [(1 - x, y), (x, 1 - y), (1 - x, 1 - y)]

        def rows(px, py, pc):
            return out_ref.at[4 * px + 2 * py + pc]

        def copy(k, block, to, src=None):
            return pltpu.make_async_remote_copy(
                src_ref=rows(*block) if src is None else src, dst_ref=rows(*block),
                send_sem=send_sems.at[k], recv_sem=recv_sems.at[k], device_id=to, device_id_type=MESH)

        mine = pltpu.make_async_copy(x_ref, rows(*me), local_sem)
        mine.start()
        first = [copy(0, me, sibling, src=x_ref)]
        first += [copy(1 + j, me, (*chip, c), src=x_ref) for j, chip in enumerate(chips)]
        for cp in first:
            cp.start()
        passed = [copy(4 + j, (*chip, c), sibling) for j, chip in enumerate(chips)]
        for j, chip in enumerate(chips):
            copy(1 + j, (*chip, c), me).wait_recv()
            passed[j].start()
        copy(0, sibling, me).wait_recv()
        for j, chip in enumerate(chips):
            copy(4 + j, (*chip, 1 - c), me).wait_recv()
        for cp in first + passed:
            cp.wait_send()
        mine.wait()

    return pl.pallas_call(
        body, out_shape=jax.ShapeDtypeStruct((N_DEV,) + shard.shape, shard.dtype),
        in_specs=[HBM_SPEC], out_specs=HBM_SPEC,
        scratch_shapes=[pltpu.SemaphoreType.DMA((7,)), pltpu.SemaphoreType.DMA((7,)), pltpu.SemaphoreType.DMA(())],
        name=name)(shard)


CHIP_RELATIONS = ((0, 0), (1, 0), (0, 1), (1, 1))


def _rs_pair_exchange(g, name):
    _, r_dim, c_dim = g.shape
    n = len(CHIP_RELATIONS)

    def body(g_ref, recv_ref, send_sems, recv_sems):
        x, y, c = _position()
        sibling = (x, y, 1 - c)
        copies = []
        for k, (rx, ry) in enumerate(CHIP_RELATIONS):
            px = x + rx - 2 * x * rx
            py = y + ry - 2 * y * ry
            copies.append(pltpu.make_async_remote_copy(
                src_ref=g_ref.at[4 * px + 2 * py + 1 - c], dst_ref=recv_ref.at[k], send_sem=send_sems.at[k],
                recv_sem=recv_sems.at[k], device_id=sibling, device_id_type=MESH))
        for cp in copies:
            cp.start()
        for cp in copies:
            cp.wait()

    return pl.pallas_call(
        body, out_shape=jax.ShapeDtypeStruct((n, r_dim, c_dim), g.dtype), in_specs=[HBM_SPEC], out_specs=HBM_SPEC,
        scratch_shapes=[pltpu.SemaphoreType.DMA((n,)), pltpu.SemaphoreType.DMA((n,))], name=name)(g)


def _rs_pair_sum(g, recv, name):
    _, r_dim, c_dim = g.shape
    n = len(CHIP_RELATIONS)
    tr = _tile(r_dim, 1200, SUBLANE)
    x, y, c = _position()
    own = jnp.stack([4 * (x + rx - 2 * x * rx) + 2 * (y + ry - 2 * y * ry) + c for rx, ry in CHIP_RELATIONS])

    def body(own_ref, g_ref, r_ref, o_ref):
        o_ref[...] = (g_ref[...].astype(F32) + r_ref[...].astype(F32)).astype(o_ref.dtype)

    blk = _bs((None, tr, c_dim), lambda k, i, own_ref: (k, i, 0))
    return pl.pallas_call(
        body, out_shape=jax.ShapeDtypeStruct((n, r_dim, c_dim), g.dtype),
        grid_spec=pltpu.PrefetchScalarGridSpec(
            num_scalar_prefetch=1, grid=(n, r_dim // tr),
            in_specs=[_bs((None, tr, c_dim), lambda k, i, own_ref: (own_ref[k], i, 0)), blk], out_specs=blk),
        name=name, compiler_params=_params("parallel", "parallel"))(own.astype(jnp.int32), g, recv)


SEM_SPEC = pl.BlockSpec(memory_space=pltpu.SEMAPHORE)
DATAFLOW = pltpu.SideEffectType.DATAFLOW_SIDE_EFFECTING
CHIP_FLIPS = CHIP_RELATIONS[1:]
TOKEN = jax.ShapeDtypeStruct((SUBLANE, LANE), F32)


def _flip(v, r):
    return v + r - 2 * v * r


def _chip_copies(src_ref, src_of, dst_ref, dst_of, send_sems, recv_sems):
    x, y, c = _position()
    me = 4 * x + 2 * y + c
    out = []
    for k, (rx, ry) in enumerate(CHIP_FLIPS):
        px, py = _flip(x, rx), _flip(y, ry)
        peer = 4 * px + 2 * py + c
        out.append(pltpu.make_async_remote_copy(
            src_ref=src_ref.at[src_of(k, me, peer)], dst_ref=dst_ref.at[dst_of(k, me, peer)],
            send_sem=send_sems.at[k], recv_sem=recv_sems.at[k], device_id=(px, py, c), device_id_type=MESH))
    return out


def _device_block(ref, spec, d):
    rows, axis = spec
    return ref.at[pl.ds(d * rows, rows)] if axis == 0 else ref.at[:, pl.ds(d * rows, rows)]


def _ag_chips_start(lands, specs, after, name):
    n = len(lands)
    nf = len(CHIP_FLIPS)

    def body(*refs):
        send_sems, recv_sems, token = refs[n + 1], refs[n + 2], refs[-1]
        x, y, c = _position()
        me = 4 * x + 2 * y + c
        for i, spec in enumerate(specs):
            blk = _device_block(refs[i], spec, me)
            for k, (rx, ry) in enumerate(CHIP_FLIPS):
                pltpu.make_async_remote_copy(
                    src_ref=blk, dst_ref=blk, send_sem=send_sems.at[nf * i + k], recv_sem=recv_sems.at[nf * i + k],
                    device_id=(_flip(x, rx), _flip(y, ry), c), device_id_type=MESH).start()
        token[...] = jnp.zeros(TOKEN.shape, TOKEN.dtype)

    sems = pltpu.SemaphoreType.DMA((nf * n,))
    return pl.pallas_call(
        body, name=name, out_shape=(sems, sems, *[pltpu.HBM(t.shape, t.dtype) for t in lands], TOKEN),
        in_specs=(HBM_SPEC,) * n + (ANY_SPEC,),
        out_specs=(SEM_SPEC, SEM_SPEC) + (HBM_SPEC,) * n + (pl.BlockSpec(memory_space=pltpu.VMEM),),
        input_output_aliases={i: 2 + i for i in range(n)}, compiler_params=pltpu.CompilerParams(has_side_effects=DATAFLOW),
    )(*[pltpu.with_memory_space_constraint(t, pltpu.HBM) for t in lands], after)


def _ag_chips_wait(send_sems, recv_sems, lands, specs, after, name):
    n = len(lands)
    nf = len(CHIP_FLIPS)

    def body(*refs):
        send_sems, recv_sems = refs[n], refs[n + 1]
        x, y, c = _position()
        me = 4 * x + 2 * y + c
        for i, spec in enumerate(specs):
            for k, (rx, ry) in enumerate(CHIP_FLIPS):
                px, py = _flip(x, rx), _flip(y, ry)
                cp = pltpu.make_async_remote_copy(
                    src_ref=_device_block(refs[i], spec, me), dst_ref=_device_block(refs[i], spec, 4 * px + 2 * py + c),
                    send_sem=send_sems.at[nf * i + k], recv_sem=recv_sems.at[nf * i + k],
                    device_id=(px, py, c), device_id_type=MESH)
                cp.wait_send()
                cp.wait_recv()

    return pl.pallas_call(
        body, name=name, out_shape=tuple(pltpu.HBM(t.shape, t.dtype) for t in lands),
        in_specs=(HBM_SPEC,) * n + (SEM_SPEC, SEM_SPEC, ANY_SPEC), out_specs=(HBM_SPEC,) * n,
        input_output_aliases={i: i for i in range(n)}, compiler_params=pltpu.CompilerParams(has_side_effects=DATAFLOW),
    )(*lands, send_sems, recv_sems, after)


def _ag_pair_forward(lands, specs, name):
    n = len(lands)
    nr = len(CHIP_RELATIONS)

    def body(*refs):
        outs, send_sems, recv_sems = refs[n:2 * n], refs[2 * n], refs[2 * n + 1]
        x, y, c = _position()
        copies = []
        for i, spec in enumerate(specs):
            for k, (rx, ry) in enumerate(CHIP_RELATIONS):
                chip = 4 * _flip(x, rx) + 2 * _flip(y, ry)
                held = _device_block(outs[i], spec, chip + c)
                sems = dict(send_sem=send_sems.at[nr * i + k], recv_sem=recv_sems.at[nr * i + k],
                            device_id=(x, y, 1 - c), device_id_type=MESH)
                mine = pltpu.make_async_remote_copy(src_ref=held, dst_ref=held, **sems)
                theirs = pltpu.make_async_remote_copy(src_ref=held, dst_ref=_device_block(outs[i], spec, chip + 1 - c), **sems)
                copies.append((mine, theirs))
        for mine, _ in copies:
            mine.start()
        for mine, theirs in copies:
            mine.wait_send()
            theirs.wait_recv()

    sems = pltpu.SemaphoreType.DMA((nr * n,))
    return pl.pallas_call(
        body, out_shape=tuple(jax.ShapeDtypeStruct(t.shape, t.dtype) for t in lands), in_specs=[HBM_SPEC] * n,
        out_specs=(HBM_SPEC,) * n, input_output_aliases={i: i for i in range(n)}, scratch_shapes=[sems, sems], name=name)(*lands)


def _rs_chips_start(pair, name):
    _, r_dim, c_dim = pair.shape
    n = len(CHIP_FLIPS)

    def body(pair_ref, far_ref, send_sems, recv_sems, pair_thru, far_thru, token):
        for cp in _chip_copies(pair_ref, lambda k, me, peer: k + 1, far_ref, lambda k, me, peer: k, send_sems, recv_sems):
            cp.start()
        token[...] = jnp.zeros(TOKEN.shape, TOKEN.dtype)

    far = lax.empty((n, r_dim, c_dim), pair.dtype)
    return pl.pallas_call(
        body, name=name,
        out_shape=(pltpu.SemaphoreType.DMA((n,)), pltpu.SemaphoreType.DMA((n,)), pltpu.HBM(pair.shape, pair.dtype),
                   pltpu.HBM(far.shape, far.dtype), TOKEN),
        in_specs=(HBM_SPEC, HBM_SPEC),
        out_specs=(SEM_SPEC, SEM_SPEC, HBM_SPEC, HBM_SPEC, pl.BlockSpec(memory_space=pltpu.VMEM)),
        input_output_aliases={0: 2, 1: 3}, compiler_params=pltpu.CompilerParams(has_side_effects=DATAFLOW),
    )(pltpu.with_memory_space_constraint(pair, pltpu.HBM), pltpu.with_memory_space_constraint(far, pltpu.HBM))


def _rs_chips_wait(send_sems, recv_sems, pair, far, after, name):
    def body(pair_ref, far_ref, send_sems, recv_sems, after_ref, pair_out, far_out):
        for cp in _chip_copies(pair_ref, lambda k, me, peer: k + 1, far_ref, lambda k, me, peer: k, send_sems, recv_sems):
            cp.wait_send()
            cp.wait_recv()

    return pl.pallas_call(
        body, name=name, out_shape=(pltpu.HBM(pair.shape, pair.dtype), pltpu.HBM(far.shape, far.dtype)),
        in_specs=(HBM_SPEC, HBM_SPEC, SEM_SPEC, SEM_SPEC, ANY_SPEC),
        out_specs=(HBM_SPEC, HBM_SPEC), input_output_aliases={0: 0, 1: 1},
        compiler_params=pltpu.CompilerParams(has_side_effects=DATAFLOW),
    )(pair, far, send_sems, recv_sems, after)


def _rs_final_sum(pair, far, name):
    _, r_dim, c_dim = pair.shape
    tr = _tile(r_dim, 1200, SUBLANE)

    def body(p_ref, f0_ref, f1_ref, f2_ref, o_ref):
        o_ref[...] = ((p_ref[...].astype(F32) + f0_ref[...].astype(F32)) + f1_ref[...].astype(F32)) + f2_ref[...].astype(F32)

    def slot(k):
        return _bs((None, tr, c_dim), lambda i: (k, i, 0))

    return pl.pallas_call(
        body, out_shape=jax.ShapeDtypeStruct((r_dim, c_dim), F32), grid=(r_dim // tr,),
        in_specs=[slot(0), slot(0), slot(1), slot(2)], out_specs=_bs((tr, c_dim), lambda i: (i, 0)), name=name,
        compiler_params=_params("parallel"))(pair, far, far, far)


def _reduce_scatter_begin(g, name):
    recv = _rs_pair_exchange(g, name + "_pair")
    pair = _rs_pair_sum(g, recv, name + "_pairsum")
    return _rs_chips_start(pair, name + "_chips_start")


def _reduce_scatter_end(state, after, name):
    send_sems, recv_sems, pair, far, _ = state
    pair, far = _rs_chips_wait(send_sems, recv_sems, pair, far, after, name + "_chips_wait")
    return _rs_final_sum(pair, far, name + "_sum")


MATRICES = (("w_in", True), ("w_out", False), ("w_q", False), ("w_kv", True), ("w_o", False), ("w_up", True),
            ("w_down", False), ("w_conv_out", True), ("w_pool_grp", True))
MIX_NAMES = ("w_in", "w_conv_out", "w_pool_grp", "w_out")
REST_NAMES = ("w_q", "w_kv", "w_o", "w_up", "w_down")


def _parts(layer):
    return (("mix", MIX_NAMES), ("rest", REST_NAMES)) if layer == 0 else (("all", MIX_NAMES + REST_NAMES),)


def _to_rows(name, transposed, w, d_model):
    if name == "w_pool_grp":
        w = jnp.swapaxes(w, 1, 2)
    elif transposed:
        w = w.T
    return w.reshape(-1, d_model)


def _stored_form(name, transposed, rows, shard_shape):
    if name == "w_pool_grp":
        g, i, o = shard_shape
        return rows.reshape(g, o, i)
    if transposed:
        return rows.reshape(shard_shape[1], shard_shape[0])
    return rows.reshape(shard_shape)


def _scatter_blocks(name, full, shard_shape, d_model, n_dev=N_DEV):
    if name == "w_pool_grp":
        g, i, o = shard_shape
        return jnp.swapaxes(full.reshape(g, n_dev, o, i), 0, 1).reshape(n_dev, -1, d_model)
    return full.reshape(n_dev, -1, d_model)


def kernel(x, mem, mix_norm_g, w_in, conv_dw_w, conv_dw_b, conv_ln_g, conv_ln_b, w_conv_out, w_pool_grp, pool_scale, w_out, xattn_norm_g, mem_norm_g, w_q, w_kv, w_o, ffn_norm_g, w_up, ffn_dw_w, w_down, final_norm_g, loss_target, m_mix_norm_g, m_w_in, m_conv_dw_w, m_conv_dw_b, m_conv_ln_g, m_conv_ln_b, m_w_conv_out, m_w_pool_grp, m_pool_scale, m_w_out, m_xattn_norm_g, m_mem_norm_g, m_w_q, m_w_kv, m_w_o, m_ffn_norm_g, m_w_up, m_ffn_dw_w, m_w_down, m_final_norm_g, v_mix_norm_g, v_w_in, v_conv_dw_w, v_conv_dw_b, v_conv_ln_g, v_conv_ln_b, v_w_conv_out, v_w_pool_grp, v_pool_scale, v_w_out, v_xattn_norm_g, v_mem_norm_g, v_w_q, v_w_kv, v_w_o, v_ffn_norm_g, v_w_up, v_ffn_dw_w, v_w_down, v_final_norm_g):
    p = dict(locals())
    weight_names = ["mix_norm_g", "w_in", "conv_dw_w", "conv_dw_b", "conv_ln_g", "conv_ln_b", "w_conv_out",
                    "w_pool_grp", "pool_scale", "w_out", "xattn_norm_g", "mem_norm_g", "w_q", "w_kv", "w_o",
                    "ffn_norm_g", "w_up", "ffn_dw_w", "w_down", "final_norm_g"]
    n_batch, seq, d_model = x.shape
    m_len = mem.shape[1]
    depth = w_in.shape[0]
    assert depth == 2, "the exchange schedule below is written for two layers"
    t_dim = n_batch * seq
    c_conv = conv_dw_b.shape[1]
    n_groups = w_pool_grp.shape[1]
    assert w_pool_grp.shape[2] == LANE and c_conv % LANE == 0 and n_groups * LANE == c_conv
    gate_col0 = 2 * c_conv + n_groups * LANE
    pool_col0 = (2 * c_conv) // LANE

    dev = 4 * lax.axis_index("x") + 2 * lax.axis_index("y") + lax.axis_index("c")
    filt = jnp.concatenate([conv_dw_w.reshape(-1), ffn_dw_w.reshape(-1)])
    filt_rows = filt.reshape(-1, d_model)
    transposed = dict(MATRICES)
    layout = {part: [(name, transposed[name], _to_rows(name, transposed[name], p[name][0], d_model).shape[0])
                     for name in names] for l in range(depth) for part, names in _parts(l)}
    part_of = {(l, name): part for l in range(depth) for part, names in _parts(l) for name in names}

    def landing(name, shard):
        if name == "w_pool_grp":
            block, axis = jnp.swapaxes(shard, 1, 2), 1
        elif name == "filt":
            block, axis = shard, 0
        else:
            block, axis = (shard.T if transposed[name] else shard), 0
        block = block if name == "filt" else block.astype(BF16)
        rows = block.shape[axis]
        shape = block.shape[:axis] + (N_DEV * rows,) + block.shape[axis + 1:]
        start = (0,) * axis + (dev * rows,) + (0,) * (block.ndim - axis - 1)
        return lax.dynamic_update_slice(lax.empty(shape, block.dtype), block, start), (rows, axis)

    ag_state = {}
    after = filt_rows
    for l in range(depth):
        for part, names in _parts(l):
            items = [(name, p[name][l]) for name in names]
            if (l, part) == (0, part_of[(0, "w_in")]):
                items.append(("filt", filt_rows))
            lands, specs = zip(*[landing(name, shard) for name, shard in items])
            out = _ag_chips_start(lands, specs, after, f"ag{l}{part}_chips_start")
            ag_state[(l, part)] = ([name for name, _ in items], specs, out)
            after = out[-1]
    all_started = after

    full = [dict() for _ in range(depth)]

    def ensure(l, name, after):
        if name in full[l]:
            return
        part = part_of[(l, name)]
        names, specs, out = ag_state[(l, part)]
        lands = _ag_chips_wait(out[0], out[1], out[2:-1], specs, after, f"ag{l}{part}_chips_wait")
        lands = _ag_pair_forward(lands, specs, f"ag{l}{part}_pair_forward")
        full[l].update(zip(names, lands))

    vec = lambda a: a.reshape(1, -1)
    x2d = x.reshape(t_dim, d_model)
    mem2d = mem.reshape(n_batch * m_len, d_model)
    mem_n = _rmsnorm_fwd(mem2d, vec(mem_norm_g), "mem_norm", after=all_started)
    h_first = _rmsnorm_fwd(x2d, vec(mix_norm_g[0]), "mix_norm_l0", after=mem_n)
    ensure(0, "w_in", h_first)
    filt_all = full[0]["filt"].reshape(N_DEV, -1)
    n_cw = conv_dw_w.size
    kc, cs = conv_dw_w.shape[1:]
    kf, fs = ffn_dw_w.shape[1:]
    conv_w_full = jnp.moveaxis(filt_all[:, :n_cw].reshape(N_DEV, depth, kc, cs), 0, 2).reshape(depth, kc, N_DEV * cs)
    ffn_w_full = jnp.moveaxis(filt_all[:, n_cw:].reshape(N_DEV, depth, kf, fs), 0, 2).reshape(depth, kf, N_DEV * fs)

    saved = []
    xc = x2d
    for l in range(depth):
        ensure(l, "w_in", xc)
        wl = full[l]
        s = {"x0": xc}
        s["h"] = h_first if l == 0 else _rmsnorm_fwd(xc, vec(mix_norm_g[l]), f"mix_norm_l{l}")
        s["proj"] = _matmul(s["h"], wl["w_in"], "nt", f"in_proj_l{l}", out_dtype=BF16)
        s["y1"] = _glu_conv_fwd(s["proj"], conv_w_full[l], vec(conv_dw_b[l]), n_batch, seq, f"glu_conv_l{l}")
        s["y3"] = _ln_silu_fwd(s["y1"], vec(conv_ln_g[l]), vec(conv_ln_b[l]), f"ln_silu_l{l}")
        s["yc"] = _matmul(s["y3"], wl["w_conv_out"], "nt", f"conv_out_l{l}", out_dtype=BF16)
        s["zp"] = _pool_fwd(s["proj"], pool_col0, n_groups, n_batch, seq, f"pool_l{l}")
        s["yp"] = _grouped(s["zp"], wl["w_pool_grp"], "nt", f"pool_proj_l{l}", out_dtype=BF16)
        s["merged"] = _merge_fwd(s["proj"], gate_col0, s["yc"], s["yp"], vec(pool_scale[l]), f"merge_l{l}")
        s["x1"] = _matmul(s["merged"], wl["w_out"], "nn", f"mix_out_l{l}", res=xc)
        ensure(l, "w_q", s["x1"])
        half_up = wl["w_up"].shape[0] // 2
        up_gate, up_val = (0, half_up), (half_up, half_up)
        s["hq"] = _rmsnorm_fwd(s["x1"], vec(xattn_norm_g[l]), f"xattn_norm_l{l}")
        s["q"] = _matmul(s["hq"], wl["w_q"], "nn", f"q_proj_l{l}", out_dtype=BF16)
        s["kv"] = _matmul(mem_n, wl["w_kv"], "nt", f"kv_proj_l{l}", out_dtype=BF16)
        s["att"] = _attn_fwd(s["q"], s["kv"], n_batch, seq, m_len, f"attn_l{l}")
        s["x2"] = _matmul(s["att"], wl["w_o"], "nn", f"attn_out_l{l}", res=s["x1"])
        s["hf"] = _rmsnorm_fwd(s["x2"], vec(ffn_norm_g[l]), f"ffn_norm_l{l}")
        s["up_g"] = _matmul(s["hf"], wl["w_up"], "nt", f"up_proj_gate_l{l}", out_dtype=BF16, b_window=up_gate)
        s["up_v"] = _matmul(s["hf"], wl["w_up"], "nt", f"up_proj_val_l{l}", out_dtype=BF16, b_window=up_val)
        s["act"] = _ffn_act_fwd(s["up_g"], s["up_v"], ffn_w_full[l], n_batch, seq, f"ffn_act_l{l}")
        xc = _matmul(s["act"], wl["w_down"], "nn", f"down_proj_l{l}", res=s["x2"])
        saved.append(s)

    dx, dxb, dg_final, loss_part = _loss_head(xc, vec(final_norm_g), loss_target.reshape(t_dim, d_model), "loss_head")

    small = {"final_norm_g": dg_final.reshape(-1)}
    big = [dict() for _ in range(depth)]
    rs_state = {}
    rs_after = loss_part

    def rs_begin(l, part):
        pack = lax.empty((N_DEV, sum(nrows for _, _, nrows in layout[part]), d_model), BF16)
        row0 = 0
        for name, _, nrows in layout[part]:
            pieces = big[l][name] if isinstance(big[l][name], tuple) else (big[l][name],)
            d0 = 0
            for piece in pieces:
                blocks = _scatter_blocks(name, piece, p[name].shape[1:], d_model, N_DEV // len(pieces)).astype(BF16)
                pack = lax.dynamic_update_slice(pack, blocks, (d0, row0, 0))
                d0 += blocks.shape[0]
            row0 += nrows
        rs_state[(l, part)] = _reduce_scatter_begin(pack, f"rs{l}{part}")
        return rs_state[(l, part)][4]

    dmem_n = None
    for l in reversed(range(depth)):
        wl, s = full[l], saved[l]
        sm = {}
        dact = _matmul(dxb, wl["w_down"], "nt", f"d_act_l{l}", out_dtype=BF16, after=rs_after)
        big[l]["w_down"] = _matmul(s["act"], dxb, "tn", f"d_w_down_l{l}", out_dtype=BF16)
        dup_g, dup_v, dwf_g, dwf_v = _ffn_act_bwd(s["up_g"], s["up_v"], ffn_w_full[l], dact, n_batch, seq,
                                                  f"ffn_act_bwd_l{l}")
        sm["ffn_dw_w"] = jnp.concatenate([dwf_g, dwf_v], axis=1)
        dx, dxb, dg = _matmul_rmsnorm_bwd((dup_g, dup_v), wl["w_up"], "nn", s["x2"], vec(ffn_norm_g[l]), dx,
                                          f"d_hf_ffn_norm_bwd_l{l}")
        big[l]["w_up"] = (_matmul(dup_g, s["hf"], "tn", f"d_w_up_gate_l{l}", out_dtype=BF16),
                          _matmul(dup_v, s["hf"], "tn", f"d_w_up_val_l{l}", out_dtype=BF16))
        sm["ffn_norm_g"] = dg
        datt = _matmul(dxb, wl["w_o"], "nt", f"d_att_l{l}", out_dtype=BF16, after=rs_after)
        big[l]["w_o"] = _matmul(s["att"], dxb, "tn", f"d_w_o_l{l}", out_dtype=BF16)
        dq, dk, dv = _attn_bwd(s["q"], s["kv"], datt, n_batch, seq, m_len, f"attn_bwd_l{l}")
        dkv = jnp.concatenate([dk, dv], axis=1)
        big[l]["w_kv"] = _matmul(dkv, mem_n, "tn", f"d_w_kv_l{l}", out_dtype=BF16)
        dmem_n = _matmul(dkv, wl["w_kv"], "nn", f"d_mem_l{l}", res=dmem_n)
        big[l]["w_q"] = _matmul(s["hq"], dq, "tn", f"d_w_q_l{l}", out_dtype=BF16)
        dx, dxb, dg = _matmul_rmsnorm_bwd(dq, wl["w_q"], "nt", s["x1"], vec(xattn_norm_g[l]), dx,
                                          f"d_hq_xattn_norm_bwd_l{l}")
        sm["xattn_norm_g"] = dg
        if part_of[(l, "w_q")] != part_of[(l, "w_in")]:
            rs_after = rs_begin(l, part_of[(l, "w_q")])
        dmerged = _matmul(dxb, wl["w_out"], "nt", f"d_merged_l{l}", out_dtype=BF16, after=rs_after)
        big[l]["w_out"] = _matmul(s["merged"], dxb, "tn", f"d_w_out_l{l}", out_dtype=BF16)
        dgc, dgp, dyc, dyp, dscale = _merge_bwd(s["proj"], gate_col0, s["yc"], s["yp"], vec(pool_scale[l]), dmerged,
                                                f"merge_bwd_l{l}")
        sm["pool_scale"] = dscale
        dzp = _grouped(dyp, wl["w_pool_grp"], "nn", f"d_zp_l{l}", out_dtype=BF16)
        big[l]["w_pool_grp"] = _grouped_tn(dyp, s["zp"], n_groups, f"d_w_pool_l{l}")
        du = _pool_bwd(dzp, n_groups, n_batch, seq, f"pool_bwd_l{l}")
        dy3 = _matmul(dyc, wl["w_conv_out"], "nn", f"d_y3_l{l}", out_dtype=BF16)
        big[l]["w_conv_out"] = _matmul(dyc, s["y3"], "tn", f"d_w_conv_out_l{l}", out_dtype=BF16)
        dy1, dlg, dlb = _ln_silu_bwd(s["y1"], vec(conv_ln_g[l]), vec(conv_ln_b[l]), dy3, f"ln_silu_bwd_l{l}")
        sm["conv_ln_g"], sm["conv_ln_b"] = dlg, dlb
        da, dgl, dcw, dcb = _glu_conv_bwd(s["proj"], conv_w_full[l], dy1, n_batch, seq, f"glu_conv_bwd_l{l}")
        sm["conv_dw_w"], sm["conv_dw_b"] = dcw, dcb
        dproj = jnp.concatenate([da, dgl, du, dgc, dgp], axis=1)
        big[l]["w_in"] = _matmul(dproj, s["h"], "tn", f"d_w_in_l{l}", out_dtype=BF16)
        dx, dxb, dg = _matmul_rmsnorm_bwd(dproj, wl["w_in"], "nn", s["x0"], vec(mix_norm_g[l]), dx,
                                          f"d_h_mix_norm_bwd_l{l}")
        sm["mix_norm_g"] = dg
        for k, val in sm.items():
            small[(l, k)] = val.reshape(-1)
        rs_after = rs_begin(l, part_of[(l, "w_in")])
    _, _, dg_mem = _rmsnorm_bwd(mem2d, vec(mem_norm_g), dmem_n, None, "mem_norm_bwd")
    small["mem_norm_g"] = dg_mem.reshape(-1)
    small["loss"] = loss_part.reshape(-1)

    grads = {}
    per_layer = {name: [None] * depth for name, _ in MATRICES}
    for l in reversed(range(depth)):
        for part, _ in reversed(_parts(l)):
            mat_grads = _reduce_scatter_end(rs_state[(l, part)], rs_after, f"rs{l}{part}")
            row0 = 0
            for name, tr, nrows in layout[part]:
                per_layer[name][l] = _stored_form(name, tr, mat_grads[row0:row0 + nrows], p[name].shape[1:])
                row0 += nrows
    flip = lambda t: jnp.swapaxes(t, -1, -2)
    stored_grads = {name: jnp.stack(per_layer[name]) for name, _ in MATRICES}
    for name, tr in MATRICES:
        grads[name] = flip(stored_grads[name]) if tr else stored_grads[name]

    keys = list(small.keys())
    flat = jnp.concatenate([small[k] for k in keys])
    n_small = flat.shape[0]
    rows_small = -(-n_small // (SUBLANE * d_model)) * SUBLANE
    flat = jnp.pad(flat, (0, rows_small * d_model - n_small)).reshape(rows_small, d_model)
    every = _all_gather(flat, "small_all_gather")
    total = _sum_rows([every[i] for i in range(N_DEV)], F32, "small_sum").reshape(-1)
    off = 0
    red = {}
    for k in keys:
        red[k] = total[off:off + small[k].shape[0]]
        off += small[k].shape[0]
    loss = red["loss"][0]
    for name in ("mix_norm_g", "conv_dw_b", "conv_ln_g", "conv_ln_b", "pool_scale", "xattn_norm_g", "ffn_norm_g"):
        grads[name] = jnp.stack([red[(l, name)] for l in range(depth)])
    grads["conv_dw_w"] = jnp.stack([
        lax.dynamic_slice_in_dim(red[(l, "conv_dw_w")].reshape(kc, N_DEV * cs), dev * cs, cs, axis=1)
        for l in range(depth)])
    grads["ffn_dw_w"] = jnp.stack([
        lax.dynamic_slice_in_dim(red[(l, "ffn_dw_w")].reshape(kf, N_DEV * fs), dev * fs, fs, axis=1)
        for l in range(depth)])
    grads["mem_norm_g"] = red["mem_norm_g"]
    grads["final_norm_g"] = red["final_norm_g"]

    deltas, new_m, new_v = {}, {}, {}
    for name in weight_names:
        if transposed.get(name, False):
            out = _adamw(flip(p[name]), stored_grads[name], flip(p["m_" + name]), flip(p["v_" + name]), f"adamw_{name}")
            deltas[name], new_m[name], new_v[name] = (flip(t) for t in out)
        else:
            deltas[name], new_m[name], new_v[name] = _adamw(p[name], grads[name], p["m_" + name], p["v_" + name],
                                                            f"adamw_{name}")
    grad_x = dx.reshape(n_batch, seq, d_model)
    return (loss, grad_x, *[grads[n] for n in weight_names], *[deltas[n] for n in weight_names],
            *[new_m[n] for n in weight_names], *[new_v[n] for n in weight_names])
```

```python
import functools

import jax
import jax.numpy as jnp
from jax import lax
from jax.experimental import pallas as pl
from jax.experimental.pallas import tpu as pltpu

F32 = jnp.float32
BF16 = jnp.bfloat16
MESH = pl.DeviceIdType.MESH

N_DEV = 8
EPS = 1e-6
V7X_VMEM_BYTES = 64 * 1024 * 1024
VMEM_LIMIT = (V7X_VMEM_BYTES * 3) // 4
LANE = 128
SUBLANE = 8

CONV_HALO = 32
POOL_HALO = 16
FFN_HALO = 8
FFN_LANE_GROUPS = 2
POOL_WINDOW_MAX = 16
XA_HEADS = 4

ADAM_LR = 0.001
ADAM_B1 = 0.9
ADAM_B2 = 0.999
ADAM_EPS = 1e-08
ADAM_WD = 0.01
ADAM_STEP = 10

GELU_C0 = 0.7978845608028654
GELU_C1 = 0.044715


ANY_SPEC = pl.BlockSpec(memory_space=pl.ANY)


def _tile(n, cap, mult=LANE):
    if n <= cap:
        return n
    best = None
    for d in range(mult, cap + 1, mult):
        if n % d == 0:
            best = d
    assert best is not None, (n, cap, mult)
    return best


def _params(*sem):
    return pltpu.CompilerParams(dimension_semantics=sem, vmem_limit_bytes=VMEM_LIMIT)


def _delayed(x, halo, rows, n_shifts):
    for r in range(min(SUBLANE, n_shifts)):
        xr = x if r == 0 else pltpu.roll(x, r, 0)
        for s in range(r, n_shifts, SUBLANE):
            yield s, xr[halo - (s - r):halo - (s - r) + rows]


def _advanced(x, rows, n_shifts):
    for r in range(min(SUBLANE, n_shifts)):
        xr = x if r == 0 else pltpu.roll(x, x.shape[0] - r, 0)
        for s in range(r, n_shifts, SUBLANE):
            yield s, xr[s - r:s - r + rows]


def _sig(x):
    return 1.0 / (1.0 + jnp.exp(-x))


def _bs(shape, imap):
    return pl.BlockSpec(shape, imap)


def _mxu_tile(n, cap):
    if n <= cap:
        return n
    best = {mult: max((d for d in range(mult, cap + 1, mult) if n % d == 0), default=0) for mult in (2 * LANE, LANE)}
    assert best[LANE] > 0, (n, cap)
    return best[2 * LANE] if 2 * best[2 * LANE] >= best[LANE] else best[LANE]


def _matmul(a, b, mode, name, res=None, out_dtype=F32, after=None, b_window=None):
    b_row0, b_rows = b_window if b_window is not None else (0, b.shape[0])
    if mode == "tn":
        k_dim, m_dim = a.shape
        k2, n_dim = b_rows, b.shape[1]
    elif mode == "nn":
        m_dim, k_dim = a.shape
        k2, n_dim = b_rows, b.shape[1]
    else:
        m_dim, k_dim = a.shape
        n_dim, k2 = b_rows, b.shape[1]
    assert k_dim == k2, (name, a.shape, b.shape)
    size = lambda t: jnp.dtype(t).itemsize
    tn = _mxu_tile(n_dim, 2816)
    budget = VMEM_LIMIT - 8 * 1024 * 1024
    for tm_cap in ((1792,) if mode == "tn" else (2048, 1024)):
        tm = _mxu_tile(m_dim, tm_cap)
        fixed = tm * tn * (2 * size(out_dtype) + (2 * size(res.dtype) if res is not None else 0) + 4)
        if fixed + 2 * min(k_dim, 1024) * (tm * size(a.dtype) + tn * size(b.dtype)) <= budget:
            break
    for cap in (2816, 2048, 1792, 1024, 512):
        tk = _mxu_tile(k_dim, cap)
        if fixed + 2 * tk * (tm * size(a.dtype) + tn * size(b.dtype)) <= VMEM_LIMIT - 8 * 1024 * 1024:
            break
    nk = k_dim // tk
    use_acc = nk > 1 and out_dtype != F32
    if mode == "tn":
        a_spec, ca = _bs((tk, tm), lambda i, j, k: (k, i)), 0
    else:
        a_spec, ca = _bs((tm, tk), lambda i, j, k: (i, k)), 1
    if mode == "nt":
        assert b_row0 % tn == 0
        b_spec, cb = _bs((tn, tk), lambda i, j, k: (j + b_row0 // tn, k)), 1
    else:
        assert b_row0 % tk == 0
        b_spec, cb = _bs((tk, tn), lambda i, j, k: (k + b_row0 // tk, j)), 0
    dims = (((ca,), (cb,)), ((), ()))
    o_spec = _bs((tm, tn), lambda i, j, k: (i, j))
    has_res = res is not None

    def body(*refs):
        a_ref, b_ref = refs[:2]
        r_ref = refs[2] if has_res else None
        o_ref = refs[n_in]
        k = pl.program_id(2)
        part = lax.dot_general(a_ref[...].astype(BF16), b_ref[...].astype(BF16), dims,
                               preferred_element_type=F32)
        if nk == 1:
            if has_res:
                part = part + r_ref[...].astype(F32)
            o_ref[...] = part.astype(out_dtype)
            return
        acc = refs[-1] if use_acc else o_ref

        @pl.when(k == 0)
        def _():
            acc[...] = part + r_ref[...].astype(F32) if has_res else part

        @pl.when(k > 0)
        def _():
            acc[...] += part

        if use_acc:
            @pl.when(k == nk - 1)
            def _():
                o_ref[...] = acc[...].astype(out_dtype)

    in_specs = [a_spec, b_spec] + ([o_spec] if has_res else [])
    args = (a, b) + ((res,) if has_res else ())
    if after is not None:
        in_specs.append(ANY_SPEC)
        args += (after,)
    n_in = len(args)
    return pl.pallas_call(
        body, out_shape=jax.ShapeDtypeStruct((m_dim, n_dim), out_dtype),
        grid=(m_dim // tm, n_dim // tn, nk), in_specs=in_specs, out_specs=o_spec,
        scratch_shapes=[pltpu.VMEM((tm, tn), F32)] if use_acc else [], name=name,
        compiler_params=_params("parallel", "parallel", "arbitrary"))(*args)


def _grouped(a, w, mode, name, out_dtype=F32):
    t_dim = a.shape[0]
    g_dim, r_dim, c_dim = w.shape
    ka, no = (c_dim, r_dim) if mode == "nt" else (r_dim, c_dim)
    tm = _tile(t_dim, 4096)
    dims = (((1,), (1 if mode == "nt" else 0,)), ((), ()))

    def body(a_ref, w_ref, o_ref):
        o_ref[...] = lax.dot_general(a_ref[...].astype(BF16), w_ref[...].astype(BF16), dims,
                                     preferred_element_type=F32).astype(out_dtype)

    return pl.pallas_call(
        body, out_shape=jax.ShapeDtypeStruct((t_dim, g_dim * no), out_dtype),
        grid=(t_dim // tm, g_dim),
        in_specs=[_bs((tm, ka), lambda i, g: (i, g)), _bs((None, r_dim, c_dim), lambda i, g: (g, 0, 0))],
        out_specs=_bs((tm, no), lambda i, g: (i, g)), name=name,
        compiler_params=_params("parallel", "parallel"))(a, w)


def _grouped_tn(a, b, g_dim, name):
    t_dim = a.shape[0]
    ra = a.shape[1] // g_dim
    cb = b.shape[1] // g_dim
    tm = _tile(t_dim, 4096)
    nt = t_dim // tm

    def body(a_ref, b_ref, o_ref):
        part = lax.dot_general(a_ref[...].astype(BF16), b_ref[...].astype(BF16), (((0,), (0,)), ((), ())),
                               preferred_element_type=F32)

        @pl.when(pl.program_id(1) == 0)
        def _():
            o_ref[...] = part

        @pl.when(pl.program_id(1) > 0)
        def _():
            o_ref[...] += part

    return pl.pallas_call(
        body, out_shape=jax.ShapeDtypeStruct((g_dim, ra, cb), F32), grid=(g_dim, nt),
        in_specs=[_bs((tm, ra), lambda g, i: (i, g)), _bs((tm, cb), lambda g, i: (i, g))],
        out_specs=_bs((None, ra, cb), lambda g, i: (g, 0, 0)), name=name,
        compiler_params=_params("parallel", "arbitrary"))(a, b)


def _rmsnorm_fwd(x, g, name, after=None):
    t_dim, d = x.shape
    tm = _tile(t_dim, 1024)

    def body(x_ref, g_ref, *rest):
        o_ref = rest[-1]
        xv = x_ref[...]
        r = lax.rsqrt(jnp.mean(xv * xv, axis=-1, keepdims=True) + EPS)
        o_ref[...] = (xv * r * g_ref[...]).astype(BF16)

    return pl.pallas_call(
        body, out_shape=jax.ShapeDtypeStruct((t_dim, d), BF16), grid=(t_dim // tm,),
        in_specs=[_bs((tm, d), lambda i: (i, 0)), _bs((1, d), lambda i: (0, 0))] + ([ANY_SPEC] if after is not None else []),
        out_specs=_bs((tm, d), lambda i: (i, 0)), name=name,
        compiler_params=_params("parallel"))(x, g, *([after] if after is not None else []))


def _rmsnorm_bwd(x, g, dh, dx_in, name):
    t_dim, d = x.shape
    tm = _tile(t_dim, 512)
    has_in = dx_in is not None

    def body(*refs):
        if has_in:
            x_ref, g_ref, dh_ref, di_ref, dx_ref, dxb_ref, dg_ref = refs
        else:
            x_ref, g_ref, dh_ref, dx_ref, dxb_ref, dg_ref = refs
        xv = x_ref[...]
        r = lax.rsqrt(jnp.mean(xv * xv, axis=-1, keepdims=True) + EPS)
        xh = xv * r
        dhv = dh_ref[...].astype(F32)
        dxh = dhv * g_ref[...]
        dx = r * (dxh - xh * jnp.mean(dxh * xh, axis=-1, keepdims=True))
        if has_in:
            dx = dx + di_ref[...]
        dx_ref[...] = dx
        dxb_ref[...] = dx.astype(BF16)
        part = jnp.sum(dhv * xh, axis=0, keepdims=True)

        @pl.when(pl.program_id(0) == 0)
        def _():
            dg_ref[...] = part

        @pl.when(pl.program_id(0) > 0)
        def _():
            dg_ref[...] += part

    row = _bs((tm, d), lambda i: (i, 0))
    vec = _bs((1, d), lambda i: (0, 0))
    args = (x, g, dh) + ((dx_in,) if has_in else ())
    return pl.pallas_call(
        body, out_shape=(jax.ShapeDtypeStruct((t_dim, d), F32), jax.ShapeDtypeStruct((t_dim, d), BF16),
                         jax.ShapeDtypeStruct((1, d), F32)),
        grid=(t_dim // tm,), in_specs=[row, vec, row] + ([row] if has_in else []),
        out_specs=(row, row, vec), name=name, compiler_params=_params("arbitrary"))(*args)


def _matmul_rmsnorm_bwd(a, b, mode, x, g, dx_in, name, res=None, b_window=None):
    pieces = a if isinstance(a, tuple) else (a,)
    n_p = len(pieces)
    b_row0, b_rows = b_window if b_window is not None else (0, b.shape[0])
    m_dim, k_piece = pieces[0].shape
    assert all(t.shape == pieces[0].shape for t in pieces)
    k_dim = n_p * k_piece
    d = x.shape[1]
    assert (b_rows, b.shape[1]) == ((k_dim, d) if mode == "nn" else (d, k_dim)), (name, pieces[0].shape, b.shape)
    tm = _mxu_tile(m_dim, 512)
    tk = _mxu_tile(k_piece, 1792)
    nkp = k_piece // tk
    nk = n_p * nkp
    has_res = res is not None
    if mode == "nt":
        assert b_row0 == 0
        b_spec, cb = _bs((d, tk), lambda i, k: (0, k)), 1
    else:
        assert b_row0 % tk == 0
        b_spec, cb = _bs((tk, d), lambda i, k: (k + b_row0 // tk, 0)), 0
    dims = (((1,), (cb,)), ((), ()))

    def body(*refs):
        b_ref = refs[n_p]
        r_ref = refs[n_p + 1] if has_res else None
        x_ref, g_ref, di_ref, dx_ref, dxb_ref, dg_ref = refs[n_p + 1 + has_res:n_p + 7 + has_res]
        i, k = pl.program_id(0), pl.program_id(1)

        def finish(dhv):
            if has_res:
                dhv = dhv + r_ref[...].astype(F32)
            xv = x_ref[...]
            r = lax.rsqrt(jnp.mean(xv * xv, axis=-1, keepdims=True) + EPS)
            xh = xv * r
            dxh = dhv * g_ref[...]
            dx = r * (dxh - xh * jnp.mean(dxh * xh, axis=-1, keepdims=True)) + di_ref[...]
            dx_ref[...] = dx
            dxb_ref[...] = dx.astype(BF16)
            dg_part = jnp.sum(dhv * xh, axis=0, keepdims=True)

            @pl.when(i == 0)
            def _():
                dg_ref[...] = dg_part

            @pl.when(i > 0)
            def _():
                dg_ref[...] += dg_part

        def step(a_ref):
            part = lax.dot_general(a_ref[...].astype(BF16), b_ref[...].astype(BF16), dims, preferred_element_type=F32)
            if nk == 1:
                finish(part)
                return
            acc = refs[-1]

            @pl.when(k == 0)
            def _():
                acc[...] = part

            @pl.when(jnp.logical_and(k > 0, k < nk - 1))
            def _():
                acc[...] += part

            @pl.when(k == nk - 1)
            def _():
                finish(acc[...] + part)

        if n_p == 1:
            step(refs[0])
        else:
            for q in range(n_p):
                pl.when(jnp.logical_and(k >= q * nkp, k < (q + 1) * nkp))(functools.partial(step, refs[q]))

    row = _bs((tm, d), lambda i, k: (i, 0))
    vec = _bs((1, d), lambda i, k: (0, 0))
    a_specs = [_bs((tm, tk), lambda i, k, q=q: (i, jnp.clip(k - q * nkp, 0, nkp - 1))) for q in range(n_p)]
    in_specs = a_specs + [b_spec] + ([row] if has_res else []) + [row, vec, row]
    args = pieces + (b,) + ((res,) if has_res else ()) + (x, g, dx_in)
    return pl.pallas_call(
        body, out_shape=(jax.ShapeDtypeStruct((m_dim, d), F32), jax.ShapeDtypeStruct((m_dim, d), BF16),
                         jax.ShapeDtypeStruct((1, d), F32)),
        grid=(m_dim // tm, nk), in_specs=in_specs, out_specs=(row, row, vec),
        scratch_shapes=[pltpu.VMEM((tm, d), F32)] if nk > 1 else [], name=name,
        compiler_params=_params("arbitrary", "arbitrary"))(*args)


def _loss_head(x, g, tgt, name):
    t_dim, d = x.shape
    tm = _tile(t_dim, 512)

    def body(x_ref, g_ref, t_ref, dx_ref, dxb_ref, dg_ref, loss_ref):
        xv = x_ref[...]
        gv = g_ref[...]
        r = lax.rsqrt(jnp.mean(xv * xv, axis=-1, keepdims=True) + EPS)
        xh = xv * r
        err = xh * gv - t_ref[...]
        dy = err * (1.0 / d)
        dxh = dy * gv
        dx = r * (dxh - xh * jnp.mean(dxh * xh, axis=-1, keepdims=True))
        dx_ref[...] = dx
        dxb_ref[...] = dx.astype(BF16)
        dg_part = jnp.sum(dy * xh, axis=0, keepdims=True)
        loss_part = jnp.full((1, LANE), 0.5 * jnp.sum(jnp.mean(err * err, axis=-1, keepdims=True)), F32)

        @pl.when(pl.program_id(0) == 0)
        def _():
            dg_ref[...] = dg_part
            loss_ref[...] = loss_part

        @pl.when(pl.program_id(0) > 0)
        def _():
            dg_ref[...] += dg_part
            loss_ref[...] += loss_part

    row = _bs((tm, d), lambda i: (i, 0))
    vec = _bs((1, d), lambda i: (0, 0))
    return pl.pallas_call(
        body, out_shape=(jax.ShapeDtypeStruct((t_dim, d), F32), jax.ShapeDtypeStruct((t_dim, d), BF16),
                         jax.ShapeDtypeStruct((1, d), F32), jax.ShapeDtypeStruct((1, LANE), F32)),
        grid=(t_dim // tm,), in_specs=[row, vec, row],
        out_specs=(row, row, vec, _bs((1, LANE), lambda i: (0, 0))), name=name,
        compiler_params=_params("arbitrary"))(x, g, tgt)


def _glu_conv_fwd(proj, dw_w, dw_b, n_batch, seq, name):
    kk, cc = dw_w.shape
    nj = cc // LANE
    ch = min(256, seq)

    def body(a_ref, gl_ref, w_ref, b_ref, o_ref, pad):
        pad[0:CONV_HALO, :] = jnp.zeros((CONV_HALO, LANE), F32)
        pad[CONV_HALO:CONV_HALO + seq, :] = a_ref[...].astype(F32) * _sig(gl_ref[...].astype(F32))
        for c0 in range(0, seq, ch):
            acc = jnp.broadcast_to(b_ref[...], (ch, LANE))
            for k in range(kk):
                acc = acc + w_ref[k:k + 1, :] * pad[pl.ds(c0 + CONV_HALO - (kk - 1) + k, ch), :]
            o_ref[c0:c0 + ch, :] = acc

    return pl.pallas_call(
        body, out_shape=jax.ShapeDtypeStruct((n_batch * seq, cc), F32), grid=(n_batch, nj),
        in_specs=[_bs((seq, LANE), lambda b, j: (b, j)), _bs((seq, LANE), lambda b, j: (b, nj + j)),
                  _bs((kk, LANE), lambda b, j: (0, j)), _bs((1, LANE), lambda b, j: (0, j))],
        out_specs=_bs((seq, LANE), lambda b, j: (b, j)),
        scratch_shapes=[pltpu.VMEM((seq + CONV_HALO, LANE), F32)], name=name,
        compiler_params=_params("parallel", "parallel"))(proj, proj, dw_w, dw_b)


def _glu_conv_bwd(proj, dw_w, dy1, n_batch, seq, name):
    kk, cc = dw_w.shape
    nj = cc // LANE
    ch = min(256, seq)

    def body(a_ref, gl_ref, dy_ref, w_ref, da_ref, dgl_ref, dw_ref, db_ref, padf, padb):
        first = pl.program_id(1) == 0
        padf[0:CONV_HALO, :] = jnp.zeros((CONV_HALO, LANE), F32)
        padf[CONV_HALO:CONV_HALO + seq, :] = a_ref[...].astype(F32) * _sig(gl_ref[...].astype(F32))
        padb[0:seq, :] = dy_ref[...]
        padb[seq:seq + CONV_HALO, :] = jnp.zeros((CONV_HALO, LANE), F32)

        @pl.when(first)
        def _():
            dw_ref[...] = jnp.zeros((kk, LANE), F32)
            db_ref[...] = jnp.zeros((1, LANE), F32)

        dws = [jnp.zeros((1, LANE), F32) for _ in range(kk)]
        for c0 in range(0, seq, ch):
            acc = jnp.zeros((ch, LANE), F32)
            y0 = padf[CONV_HALO + c0:CONV_HALO + c0 + ch, :]
            for k in range(kk):
                win = padb[pl.ds(c0 + (kk - 1) - k, ch), :]
                acc = acc + w_ref[k:k + 1, :] * win
                dws[k] = dws[k] + jnp.sum(win * y0, axis=0, keepdims=True)
            sg = _sig(gl_ref[c0:c0 + ch, :].astype(F32))
            da_ref[c0:c0 + ch, :] = (acc * sg).astype(BF16)
            dgl_ref[c0:c0 + ch, :] = (acc * a_ref[c0:c0 + ch, :].astype(F32) * sg * (1.0 - sg)).astype(BF16)
        for k in range(kk):
            dw_ref[k:k + 1, :] += dws[k]
        db_ref[...] += jnp.sum(dy_ref[...], axis=0, keepdims=True)

    tok = _bs((seq, LANE), lambda j, b: (b, j))
    t_dim = n_batch * seq
    return pl.pallas_call(
        body, out_shape=(jax.ShapeDtypeStruct((t_dim, cc), BF16), jax.ShapeDtypeStruct((t_dim, cc), BF16),
                         jax.ShapeDtypeStruct((kk, cc), F32), jax.ShapeDtypeStruct((1, cc), F32)),
        grid=(nj, n_batch),
        in_specs=[tok, _bs((seq, LANE), lambda j, b: (b, nj + j)), tok, _bs((kk, LANE), lambda j, b: (0, j))],
        out_specs=(tok, tok, _bs((kk, LANE), lambda j, b: (0, j)), _bs((1, LANE), lambda j, b: (0, j))),
        scratch_shapes=[pltpu.VMEM((seq + CONV_HALO, LANE), F32), pltpu.VMEM((seq + CONV_HALO, LANE), F32)],
        name=name, compiler_params=_params("parallel", "arbitrary"))(proj, proj, dy1, dw_w)


def _ln_silu_matmul(y1, g, b, w, name):
    t_dim, c = y1.shape
    n_dim = w.shape[0]
    tm = _tile(t_dim, 1024)

    def body(y_ref, g_ref, b_ref, w_ref, y3_ref, yc_ref):
        yv = y_ref[...]
        xc = yv - jnp.mean(yv, axis=-1, keepdims=True)
        rstd = lax.rsqrt(jnp.mean(xc * xc, axis=-1, keepdims=True) + EPS)
        y2 = xc * rstd * g_ref[...] + b_ref[...]
        y3 = (y2 * _sig(y2)).astype(BF16)
        y3_ref[...] = y3
        yc_ref[...] = lax.dot_general(y3, w_ref[...].astype(BF16), (((1,), (1,)), ((), ())),
                                      preferred_element_type=F32).astype(BF16)

    row = _bs((tm, c), lambda i: (i, 0))
    vec = _bs((1, c), lambda i: (0, 0))
    return pl.pallas_call(
        body, out_shape=(jax.ShapeDtypeStruct((t_dim, c), BF16), jax.ShapeDtypeStruct((t_dim, n_dim), BF16)),
        grid=(t_dim // tm,), in_specs=[row, vec, vec, _bs((n_dim, c), lambda i: (0, 0))],
        out_specs=(row, _bs((tm, n_dim), lambda i: (i, 0))), name=name, compiler_params=_params("parallel"))(y1, g, b, w)


def _ln_silu_bwd(y1, g, b, dy3, name):
    t_dim, c = y1.shape
    tm = _tile(t_dim, 1024)

    def body(y_ref, g_ref, b_ref, d_ref, dy_ref, dg_ref, db_ref):
        yv = y_ref[...]
        gv = g_ref[...]
        xc = yv - jnp.mean(yv, axis=-1, keepdims=True)
        rstd = lax.rsqrt(jnp.mean(xc * xc, axis=-1, keepdims=True) + EPS)
        yh = xc * rstd
        y2 = yh * gv + b_ref[...]
        s = _sig(y2)
        dy2 = d_ref[...].astype(F32) * (s * (1.0 + y2 * (1.0 - s)))
        dyh = dy2 * gv
        dy_ref[...] = rstd * (dyh - jnp.mean(dyh, axis=-1, keepdims=True)
                              - yh * jnp.mean(dyh * yh, axis=-1, keepdims=True))
        dg_part = jnp.sum(dy2 * yh, axis=0, keepdims=True)
        db_part = jnp.sum(dy2, axis=0, keepdims=True)

        @pl.when(pl.program_id(0) == 0)
        def _():
            dg_ref[...] = dg_part
            db_ref[...] = db_part

        @pl.when(pl.program_id(0) > 0)
        def _():
            dg_ref[...] += dg_part
            db_ref[...] += db_part

    row = _bs((tm, c), lambda i: (i, 0))
    vec = _bs((1, c), lambda i: (0, 0))
    return pl.pallas_call(
        body, out_shape=(jax.ShapeDtypeStruct((t_dim, c), F32), jax.ShapeDtypeStruct((1, c), F32),
                         jax.ShapeDtypeStruct((1, c), F32)),
        grid=(t_dim // tm,), in_specs=[row, vec, vec, row], out_specs=(row, vec, vec), name=name,
        compiler_params=_params("arbitrary"))(y1, g, b, dy3)


def _pool_fwd(proj, col0, n_groups, n_batch, seq, name):
    ch = min(256, seq)

    def body(u_ref, o_ref, pad):
        w = lax.shift_left(jnp.int32(2), pl.program_id(1))
        pad[0:POOL_HALO, :] = jnp.zeros((POOL_HALO, LANE), F32)
        pad[POOL_HALO:POOL_HALO + seq, :] = u_ref[...].astype(F32)
        for c0 in range(0, seq, ch):
            acc = jnp.zeros((ch, LANE), F32)
            for j in range(POOL_WINDOW_MAX):
                acc = acc + jnp.where(j < w, 1.0, 0.0).astype(F32) * pad[pl.ds(c0 + POOL_HALO - j, ch), :]
            t = c0 + lax.broadcasted_iota(jnp.int32, (ch, LANE), 0)
            cnt = jnp.minimum(t + 1, w).astype(F32)
            o_ref[c0:c0 + ch, :] = (acc / cnt - pad[POOL_HALO + c0:POOL_HALO + c0 + ch, :]).astype(BF16)

    return pl.pallas_call(
        body, out_shape=jax.ShapeDtypeStruct((n_batch * seq, n_groups * LANE), BF16), grid=(n_batch, n_groups),
        in_specs=[_bs((seq, LANE), lambda b, g: (b, col0 + g))], out_specs=_bs((seq, LANE), lambda b, g: (b, g)),
        scratch_shapes=[pltpu.VMEM((seq + POOL_HALO, LANE), F32)], name=name,
        compiler_params=_params("parallel", "parallel"))(proj)


def _pool_bwd(dzp, n_groups, n_batch, seq, name):
    ch = min(256, seq)

    def body(d_ref, o_ref, pad):
        w = lax.shift_left(jnp.int32(2), pl.program_id(1))
        for c0 in range(0, seq, ch):
            t = c0 + lax.broadcasted_iota(jnp.int32, (ch, LANE), 0)
            cnt = jnp.minimum(t + 1, w).astype(F32)
            pad[c0:c0 + ch, :] = d_ref[c0:c0 + ch, :].astype(F32) / cnt
        pad[seq:seq + POOL_HALO, :] = jnp.zeros((POOL_HALO, LANE), F32)
        for c0 in range(0, seq, ch):
            acc = jnp.zeros((ch, LANE), F32)
            for j in range(POOL_WINDOW_MAX):
                acc = acc + jnp.where(j < w, 1.0, 0.0).astype(F32) * pad[pl.ds(c0 + j, ch), :]
            o_ref[c0:c0 + ch, :] = (acc - d_ref[c0:c0 + ch, :].astype(F32)).astype(BF16)

    tok = _bs((seq, LANE), lambda b, g: (b, g))
    return pl.pallas_call(
        body, out_shape=jax.ShapeDtypeStruct((n_batch * seq, n_groups * LANE), BF16), grid=(n_batch, n_groups),
        in_specs=[tok], out_specs=tok, scratch_shapes=[pltpu.VMEM((seq + POOL_HALO, LANE), F32)], name=name,
        compiler_params=_params("parallel", "parallel"))(dzp)


def _merge_fwd(proj, col0, yc, yp, scale, name):
    t_dim, d = yc.shape
    half = d // 2
    tm = _tile(t_dim, 1024)
    c0 = col0 // half

    def body(gc_ref, gp_ref, yc_ref, yp_ref, s_ref, o_ref):
        f32 = lambda r: r[...].astype(F32)
        o_ref[...] = (_sig(f32(gc_ref)) * f32(yc_ref) + _sig(f32(gp_ref)) * (f32(yp_ref) * s_ref[...])).astype(BF16)

    blk = _bs((tm, half), lambda i, j: (i, j))
    return pl.pallas_call(
        body, out_shape=jax.ShapeDtypeStruct((t_dim, d), BF16), grid=(t_dim // tm, 2),
        in_specs=[_bs((tm, half), lambda i, j: (i, c0 + j)), _bs((tm, half), lambda i, j: (i, c0 + 2 + j)),
                  blk, blk, _bs((1, half), lambda i, j: (0, j))],
        out_specs=blk, name=name, compiler_params=_params("parallel", "parallel"))(proj, proj, yc, yp, scale)


def _merge_bwd(proj, col0, yc, yp, scale, dm, name):
    t_dim, d = yc.shape
    half = d // 2
    tm = _tile(t_dim, 1024)
    c0 = col0 // half

    def body(gc_ref, gp_ref, yc_ref, yp_ref, s_ref, dm_ref, dgc_ref, dgp_ref, dyc_ref, dyp_ref, ds_ref):
        dmv = dm_ref[...].astype(F32)
        sgc = _sig(gc_ref[...].astype(F32))
        sgp = _sig(gp_ref[...].astype(F32))
        sv = s_ref[...]
        ypre = yp_ref[...].astype(F32)
        dgc_ref[...] = (dmv * yc_ref[...].astype(F32) * sgc * (1.0 - sgc)).astype(BF16)
        dgp_ref[...] = (dmv * (ypre * sv) * sgp * (1.0 - sgp)).astype(BF16)
        dyc_ref[...] = (dmv * sgc).astype(BF16)
        dyp = dmv * sgp
        dyp_ref[...] = (dyp * sv).astype(BF16)
        part = jnp.sum(dyp * ypre, axis=0, keepdims=True)

        @pl.when(pl.program_id(1) == 0)
        def _():
            ds_ref[...] = part

        @pl.when(pl.program_id(1) > 0)
        def _():
            ds_ref[...] += part

    blk = _bs((tm, half), lambda j, i: (i, j))
    big = jax.ShapeDtypeStruct((t_dim, d), BF16)
    return pl.pallas_call(
        body, out_shape=(big, big, big, big, jax.ShapeDtypeStruct((1, d), F32)), grid=(2, t_dim // tm),
        in_specs=[_bs((tm, half), lambda j, i: (i, c0 + j)), _bs((tm, half), lambda j, i: (i, c0 + 2 + j)),
                  blk, blk, _bs((1, half), lambda j, i: (0, j)), blk],
        out_specs=(blk, blk, blk, blk, _bs((1, half), lambda j, i: (0, j))), name=name,
        compiler_params=_params("parallel", "arbitrary"))(proj, proj, yc, yp, scale, dm)


def _attn_fwd(q, kv, n_batch, seq, m_len, name):
    d = q.shape[1]
    hd = d // XA_HEADS
    tq = _tile(seq, 2048)
    nq = seq // tq
    scale = hd ** -0.5

    def body(q_ref, k_ref, v_ref, o_ref):
        sc = lax.dot_general(q_ref[...].astype(BF16), k_ref[...].astype(BF16), (((1,), (1,)), ((), ())),
                             preferred_element_type=F32) * scale
        p = jnp.exp(sc - jnp.max(sc, axis=-1, keepdims=True))
        pr = p / jnp.sum(p, axis=-1, keepdims=True)
        o_ref[...] = jnp.dot(pr.astype(BF16), v_ref[...].astype(BF16), preferred_element_type=F32).astype(BF16)

    return pl.pallas_call(
        body, out_shape=jax.ShapeDtypeStruct((n_batch * seq, d), BF16), grid=(n_batch, XA_HEADS, nq),
        in_specs=[_bs((tq, hd), lambda b, h, i: (b * nq + i, h)), _bs((m_len, hd), lambda b, h, i: (b, h)),
                  _bs((m_len, hd), lambda b, h, i: (b, XA_HEADS + h))],
        out_specs=_bs((tq, hd), lambda b, h, i: (b * nq + i, h)), name=name,
        compiler_params=_params("parallel", "parallel", "parallel"))(q, kv, kv)


def _attn_bwd(q, kv, datt, n_batch, seq, m_len, name):
    d = q.shape[1]
    hd = d // XA_HEADS
    tq = _tile(seq, 2048)
    nq = seq // tq
    scale = hd ** -0.5

    def body(q_ref, k_ref, v_ref, do_ref, dq_ref, dk_ref, dv_ref):
        qb = q_ref[...].astype(BF16)
        kb = k_ref[...].astype(BF16)
        vb = v_ref[...].astype(BF16)
        dob = do_ref[...].astype(BF16)
        sc = lax.dot_general(qb, kb, (((1,), (1,)), ((), ())), preferred_element_type=F32) * scale
        p = jnp.exp(sc - jnp.max(sc, axis=-1, keepdims=True))
        pr = p / jnp.sum(p, axis=-1, keepdims=True)
        dpr = lax.dot_general(dob, vb, (((1,), (1,)), ((), ())), preferred_element_type=F32)
        dsc = pr * (dpr - jnp.sum(dpr * pr, axis=-1, keepdims=True)) * scale
        dsb = dsc.astype(BF16)
        dq_ref[...] = jnp.dot(dsb, kb, preferred_element_type=F32).astype(BF16)
        dv_part = lax.dot_general(pr.astype(BF16), dob, (((0,), (0,)), ((), ())), preferred_element_type=F32)
        dk_part = lax.dot_general(dsb, qb, (((0,), (0,)), ((), ())), preferred_element_type=F32)

        @pl.when(pl.program_id(2) == 0)
        def _():
            dk_ref[...] = dk_part
            dv_ref[...] = dv_part

        @pl.when(pl.program_id(2) > 0)
        def _():
            dk_ref[...] += dk_part
            dv_ref[...] += dv_part

    qs = _bs((tq, hd), lambda b, h, i: (b * nq + i, h))
    ks = _bs((m_len, hd), lambda b, h, i: (b, h))
    return pl.pallas_call(
        body, out_shape=(jax.ShapeDtypeStruct((n_batch * seq, d), BF16), jax.ShapeDtypeStruct((n_batch * m_len, d), F32),
                         jax.ShapeDtypeStruct((n_batch * m_len, d), F32)),
        grid=(n_batch, XA_HEADS, nq),
        in_specs=[qs, ks, _bs((m_len, hd), lambda b, h, i: (b, XA_HEADS + h)), qs],
        out_specs=(qs, ks, ks), name=name,
        compiler_params=_params("parallel", "parallel", "arbitrary"))(q, kv, kv, datt)


def _gelu_parts(g):
    th = jnp.tanh(GELU_C0 * (g + GELU_C1 * g * g * g))
    return th, 0.5 * g * (1.0 + th)


def _ffn_act_fwd(up_g, up_v, dw_w, n_batch, seq, name):
    kk, c2 = dw_w.shape
    f_dim = c2 // 2
    wd = FFN_LANE_GROUPS * LANE
    nj = f_dim // wd
    ch = min(128, seq)

    def body(g_ref, v_ref, wg_ref, wv_ref, o_ref, padg, padv):
        for h in range(FFN_LANE_GROUPS):
            lanes = slice(h * LANE, (h + 1) * LANE)
            for pad, src in ((padg, g_ref), (padv, v_ref)):
                pad[h, 0:FFN_HALO, :] = jnp.zeros((FFN_HALO, LANE), F32)
                pad[h, FFN_HALO:FFN_HALO + seq, :] = src[:, lanes].astype(F32)
            for c0 in range(0, seq, ch):
                gate = jnp.zeros((ch, LANE), F32)
                val = jnp.zeros((ch, LANE), F32)
                for k in range(kk):
                    off = c0 + FFN_HALO - (kk - 1) + k
                    gate = gate + wg_ref[k:k + 1, lanes] * padg[h, pl.ds(off, ch), :]
                    val = val + wv_ref[k:k + 1, lanes] * padv[h, pl.ds(off, ch), :]
                o_ref[c0:c0 + ch, lanes] = (_gelu_parts(gate)[1] * val).astype(BF16)

    pad_shape = pltpu.VMEM((FFN_LANE_GROUPS, seq + FFN_HALO, LANE), F32)
    return pl.pallas_call(
        body, out_shape=jax.ShapeDtypeStruct((n_batch * seq, f_dim), BF16), grid=(n_batch, nj),
        in_specs=[_bs((seq, wd), lambda b, j: (b, j)), _bs((seq, wd), lambda b, j: (b, j)),
                  _bs((kk, wd), lambda b, j: (0, j)), _bs((kk, wd), lambda b, j: (0, nj + j))],
        out_specs=_bs((seq, wd), lambda b, j: (b, j)), scratch_shapes=[pad_shape, pad_shape], name=name,
        compiler_params=_params("parallel", "parallel"))(up_g, up_v, dw_w, dw_w)


def _ffn_act_bwd(up_g, up_v, dw_w, dact, n_batch, seq, name):
    kk, c2 = dw_w.shape
    f_dim = c2 // 2
    wd = FFN_LANE_GROUPS * LANE
    nj = f_dim // wd
    ch = min(128, seq)

    def body(g_ref, v_ref, wg_ref, wv_ref, da_ref, dg_ref, dv_ref, dwg_ref, dwv_ref, padg, padv, pbg, pbv):
        @pl.when(pl.program_id(1) == 0)
        def _():
            dwg_ref[...] = jnp.zeros((kk, wd), F32)
            dwv_ref[...] = jnp.zeros((kk, wd), F32)

        for h in range(FFN_LANE_GROUPS):
            lanes = slice(h * LANE, (h + 1) * LANE)
            for pad, src in ((padg, g_ref), (padv, v_ref)):
                pad[h, 0:FFN_HALO, :] = jnp.zeros((FFN_HALO, LANE), F32)
                pad[h, FFN_HALO:FFN_HALO + seq, :] = src[:, lanes].astype(F32)
            for pb in (pbg, pbv):
                pb[h, seq:seq + FFN_HALO, :] = jnp.zeros((FFN_HALO, LANE), F32)
            for c0 in range(0, seq, ch):
                gate = jnp.zeros((ch, LANE), F32)
                val = jnp.zeros((ch, LANE), F32)
                for k in range(kk):
                    off = c0 + FFN_HALO - (kk - 1) + k
                    gate = gate + wg_ref[k:k + 1, lanes] * padg[h, pl.ds(off, ch), :]
                    val = val + wv_ref[k:k + 1, lanes] * padv[h, pl.ds(off, ch), :]
                sq = gate * gate
                th = jnp.tanh(GELU_C0 * gate * (1.0 + GELU_C1 * sq))
                half = 0.5 * th + 0.5
                dgelu = half * (1.0 + gate * (GELU_C0 + 3.0 * GELU_C0 * GELU_C1 * sq) * (1.0 - th))
                dav = da_ref[c0:c0 + ch, lanes].astype(F32)
                pbg[h, c0:c0 + ch, :] = dav * val * dgelu
                pbv[h, c0:c0 + ch, :] = dav * (gate * half)
            for pb, pad, w_ref, d_ref, dw_ref in ((pbg, padg, wg_ref, dg_ref, dwg_ref), (pbv, padv, wv_ref, dv_ref, dwv_ref)):
                for c0 in range(0, seq, ch):
                    acc = jnp.zeros((ch, LANE), F32)
                    for k in range(kk):
                        acc = acc + w_ref[k:k + 1, lanes] * pb[h, pl.ds(c0 + (kk - 1) - k, ch), :]
                    d_ref[c0:c0 + ch, lanes] = acc.astype(BF16)
                for k in range(kk):
                    s = jnp.zeros((1, LANE), F32)
                    for c0 in range(0, seq, ch):
                        s = s + jnp.sum(pb[h, c0:c0 + ch, :] * pad[h, pl.ds(c0 + FFN_HALO - (kk - 1) + k, ch), :],
                                        axis=0, keepdims=True)
                    dw_ref[k:k + 1, lanes] += s

    t_dim = n_batch * seq
    tok = _bs((seq, wd), lambda j, b: (b, j))
    wblk = _bs((kk, wd), lambda j, b: (0, j))
    pad_shape = pltpu.VMEM((FFN_LANE_GROUPS, seq + FFN_HALO, LANE), F32)
    return pl.pallas_call(
        body, out_shape=(jax.ShapeDtypeStruct((t_dim, f_dim), BF16), jax.ShapeDtypeStruct((t_dim, f_dim), BF16),
                         jax.ShapeDtypeStruct((kk, f_dim), F32), jax.ShapeDtypeStruct((kk, f_dim), F32)),
        grid=(nj, n_batch),
        in_specs=[tok, tok, wblk, _bs((kk, wd), lambda j, b: (0, nj + j)), tok],
        out_specs=(tok, tok, wblk, wblk), scratch_shapes=[pad_shape, pad_shape, pad_shape, pad_shape], name=name,
        compiler_params=_params("parallel", "arbitrary"))(up_g, up_v, dw_w, dw_w, dact)


def _sum_rows(parts, out_dtype, name):
    r_dim, c_dim = parts[0].shape
    tr = _tile(r_dim, 1200, SUBLANE)
    n = len(parts)

    def body(*refs):
        acc = refs[0][...].astype(F32)
        for r in refs[1:n]:
            acc = acc + r[...].astype(F32)
        refs[n][...] = acc.astype(out_dtype)

    blk = _bs((tr, c_dim), lambda i: (i, 0))
    return pl.pallas_call(
        body, out_shape=jax.ShapeDtypeStruct((r_dim, c_dim), out_dtype), grid=(r_dim // tr,),
        in_specs=[blk] * n, out_specs=blk, name=name, compiler_params=_params("parallel"))(*parts)


def _adamw(w, g, m, v, name):
    shape = w.shape
    c_dim = shape[-1]
    r_dim = w.size // c_dim
    two_d = lambda t: t.reshape(r_dim, c_dim)
    tr = _tile(r_dim, max(SUBLANE, (512 * 1024) // max(c_dim, LANE) // SUBLANE * SUBLANE), SUBLANE)
    c1 = 1.0 - ADAM_B1 ** ADAM_STEP
    c2 = 1.0 - ADAM_B2 ** ADAM_STEP

    def body(w_ref, g_ref, m_ref, v_ref, d_ref, mo_ref, vo_ref):
        gv = g_ref[...]
        mn = ADAM_B1 * m_ref[...] + (1.0 - ADAM_B1) * gv
        vn = ADAM_B2 * v_ref[...] + (1.0 - ADAM_B2) * (gv * gv)
        mo_ref[...] = mn
        vo_ref[...] = vn
        d_ref[...] = -ADAM_LR * ((mn / c1) / (jnp.sqrt(vn / c2) + ADAM_EPS) + ADAM_WD * w_ref[...])

    blk = _bs((tr, c_dim), lambda i: (i, 0))
    out = jax.ShapeDtypeStruct((r_dim, c_dim), F32)
    d, mo, vo = pl.pallas_call(
        body, out_shape=(out, out, out), grid=(r_dim // tr,), in_specs=[blk] * 4, out_specs=(blk, blk, blk),
        name=name, compiler_params=_params("parallel"))(two_d(w), two_d(g), two_d(m), two_d(v))
    return d.reshape(shape), mo.reshape(shape), vo.reshape(shape)


HBM_SPEC = pl.BlockSpec(memory_space=pltpu.HBM)


def _position():
    return lax.axis_index("x"), lax.axis_index("y"), lax.axis_index("c")


def _all_gather(shard, name):
    def body(x_ref, out_ref, send_sems, recv_sems, local_sem):
        x, y, c = _position()
        me, sibling = (x, y, c), (x, y, 1 - c)
        chips = [(1 - x, y), (x, 1 - y), (1 - x, 1 - y)]

        def rows(px, py, pc):
            return out_ref.at[4 * px + 2 * py + pc]

        def copy(k, block, to, src=None):
            return pltpu.make_async_remote_copy(
                src_ref=rows(*block) if src is None else src, dst_ref=rows(*block),
                send_sem=send_sems.at[k], recv_sem=recv_sems.at[k], device_id=to, device_id_type=MESH)

        mine = pltpu.make_async_copy(x_ref, rows(*me), local_sem)
        mine.start()
        first = [copy(0, me, sibling, src=x_ref)]
        first += [copy(1 + j, me, (*chip, c), src=x_ref) for j, chip in enumerate(chips)]
        for cp in first:
            cp.start()
        passed = [copy(4 + j, (*chip, c), sibling) for j, chip in enumerate(chips)]
        for j, chip in enumerate(chips):
            copy(1 + j, (*chip, c), me).wait_recv()
            passed[j].start()
        copy(0, sibling, me).wait_recv()
        for j, chip in enumerate(chips):
            copy(4 + j, (*chip, 1 - c), me).wait_recv()
        for cp in first + passed:
            cp.wait_send()
        mine.wait()

    return pl.pallas_call(
        body, out_shape=jax.ShapeDtypeStruct((N_DEV,) + shard.shape, shard.dtype),
        in_specs=[HBM_SPEC], out_specs=HBM_SPEC,
        scratch_shapes=[pltpu.SemaphoreType.DMA((7,)), pltpu.SemaphoreType.DMA((7,)), pltpu.SemaphoreType.DMA(())],
        name=name)(shard)


CHIP_RELATIONS = ((0, 0), (1, 0), (0, 1), (1, 1))


def _rs_pair_exchange(g, name):
    _, r_dim, c_dim = g.shape
    n = len(CHIP_RELATIONS)

    def body(g_ref, recv_ref, send_sems, recv_sems):
        x, y, c = _position()
        sibling = (x, y, 1 - c)
        copies = []
        for k, (rx, ry) in enumerate(CHIP_RELATIONS):
            px = x + rx - 2 * x * rx
            py = y + ry - 2 * y * ry
            copies.append(pltpu.make_async_remote_copy(
                src_ref=g_ref.at[4 * px + 2 * py + 1 - c], dst_ref=recv_ref.at[k], send_sem=send_sems.at[k],
                recv_sem=recv_sems.at[k], device_id=sibling, device_id_type=MESH))
        for cp in copies:
            cp.start()
        for cp in copies:
            cp.wait()

    return pl.pallas_call(
        body, out_shape=jax.ShapeDtypeStruct((n, r_dim, c_dim), g.dtype), in_specs=[HBM_SPEC], out_specs=HBM_SPEC,
        scratch_shapes=[pltpu.SemaphoreType.DMA((n,)), pltpu.SemaphoreType.DMA((n,))], name=name)(g)


def _rs_pair_sum(g, recv, name):
    _, r_dim, c_dim = g.shape
    n = len(CHIP_RELATIONS)
    tr = _tile(r_dim, 1200, SUBLANE)
    x, y, c = _position()
    own = jnp.stack([4 * (x + rx - 2 * x * rx) + 2 * (y + ry - 2 * y * ry) + c for rx, ry in CHIP_RELATIONS])

    def body(own_ref, g_ref, r_ref, o_ref):
        o_ref[...] = (g_ref[...].astype(F32) + r_ref[...].astype(F32)).astype(o_ref.dtype)

    blk = _bs((None, tr, c_dim), lambda k, i, own_ref: (k, i, 0))
    return pl.pallas_call(
        body, out_shape=jax.ShapeDtypeStruct((n, r_dim, c_dim), g.dtype),
        grid_spec=pltpu.PrefetchScalarGridSpec(
            num_scalar_prefetch=1, grid=(n, r_dim // tr),
            in_specs=[_bs((None, tr, c_dim), lambda k, i, own_ref: (own_ref[k], i, 0)), blk], out_specs=blk),
        name=name, compiler_params=_params("parallel", "parallel"))(own.astype(jnp.int32), g, recv)


SEM_SPEC = pl.BlockSpec(memory_space=pltpu.SEMAPHORE)
DATAFLOW = pltpu.SideEffectType.DATAFLOW_SIDE_EFFECTING
CHIP_FLIPS = CHIP_RELATIONS[1:]
TOKEN = jax.ShapeDtypeStruct((SUBLANE, LANE), F32)


def _flip(v, r):
    return v + r - 2 * v * r


def _chip_copies(src_ref, src_of, dst_ref, dst_of, send_sems, recv_sems):
    x, y, c = _position()
    me = 4 * x + 2 * y + c
    out = []
    for k, (rx, ry) in enumerate(CHIP_FLIPS):
        px, py = _flip(x, rx), _flip(y, ry)
        peer = 4 * px + 2 * py + c
        out.append(pltpu.make_async_remote_copy(
            src_ref=src_ref.at[src_of(k, me, peer)], dst_ref=dst_ref.at[dst_of(k, me, peer)],
            send_sem=send_sems.at[k], recv_sem=recv_sems.at[k], device_id=(px, py, c), device_id_type=MESH))
    return out


def _device_block(ref, spec, d):
    rows, axis = spec
    return ref.at[pl.ds(d * rows, rows)] if axis == 0 else ref.at[:, pl.ds(d * rows, rows)]


def _ag_chips_start(lands, specs, after, name):
    n = len(lands)
    nf = len(CHIP_FLIPS)

    def body(*refs):
        send_sems, recv_sems, token = refs[n + 1], refs[n + 2], refs[-1]
        x, y, c = _position()
        me = 4 * x + 2 * y + c
        for i, spec in enumerate(specs):
            blk = _device_block(refs[i], spec, me)
            for k, (rx, ry) in enumerate(CHIP_FLIPS):
                pltpu.make_async_remote_copy(
                    src_ref=blk, dst_ref=blk, send_sem=send_sems.at[nf * i + k], recv_sem=recv_sems.at[nf * i + k],
                    device_id=(_flip(x, rx), _flip(y, ry), c), device_id_type=MESH).start()
        token[...] = jnp.zeros(TOKEN.shape, TOKEN.dtype)

    sems = pltpu.SemaphoreType.DMA((nf * n,))
    return pl.pallas_call(
        body, name=name, out_shape=(sems, sems, *[pltpu.HBM(t.shape, t.dtype) for t in lands], TOKEN),
        in_specs=(HBM_SPEC,) * n + (ANY_SPEC,),
        out_specs=(SEM_SPEC, SEM_SPEC) + (HBM_SPEC,) * n + (pl.BlockSpec(memory_space=pltpu.VMEM),),
        input_output_aliases={i: 2 + i for i in range(n)}, compiler_params=pltpu.CompilerParams(has_side_effects=DATAFLOW),
    )(*[pltpu.with_memory_space_constraint(t, pltpu.HBM) for t in lands], after)


def _ag_chips_wait(send_sems, recv_sems, lands, specs, after, name):
    n = len(lands)
    nf = len(CHIP_FLIPS)

    def body(*refs):
        send_sems, recv_sems = refs[n], refs[n + 1]
        x, y, c = _position()
        me = 4 * x + 2 * y + c
        for i, spec in enumerate(specs):
            for k, (rx, ry) in enumerate(CHIP_FLIPS):
                px, py = _flip(x, rx), _flip(y, ry)
                cp = pltpu.make_async_remote_copy(
                    src_ref=_device_block(refs[i], spec, me), dst_ref=_device_block(refs[i], spec, 4 * px + 2 * py + c),
                    send_sem=send_sems.at[nf * i + k], recv_sem=recv_sems.at[nf * i + k],
                    device_id=(px, py, c), device_id_type=MESH)
                cp.wait_send()
                cp.wait_recv()

    return pl.pallas_call(
        body, name=name, out_shape=tuple(pltpu.HBM(t.shape, t.dtype) for t in lands),
        in_specs=(HBM_SPEC,) * n + (SEM_SPEC, SEM_SPEC, ANY_SPEC), out_specs=(HBM_SPEC,) * n,
        input_output_aliases={i: i for i in range(n)}, compiler_params=pltpu.CompilerParams(has_side_effects=DATAFLOW),
    )(*lands, send_sems, recv_sems, after)


def _ag_pair_forward(lands, specs, name):
    n = len(lands)
    nr = len(CHIP_RELATIONS)

    def body(*refs):
        outs, send_sems, recv_sems = refs[n:2 * n], refs[2 * n], refs[2 * n + 1]
        x, y, c = _position()
        copies = []
        for i, spec in enumerate(specs):
            for k, (rx, ry) in enumerate(CHIP_RELATIONS):
                chip = 4 * _flip(x, rx) + 2 * _flip(y, ry)
                held = _device_block(outs[i], spec, chip + c)
                sems = dict(send_sem=send_sems.at[nr * i + k], recv_sem=recv_sems.at[nr * i + k],
                            device_id=(x, y, 1 - c), device_id_type=MESH)
                mine = pltpu.make_async_remote_copy(src_ref=held, dst_ref=held, **sems)
                theirs = pltpu.make_async_remote_copy(src_ref=held, dst_ref=_device_block(outs[i], spec, chip + 1 - c), **sems)
                copies.append((mine, theirs))
        for mine, _ in copies:
            mine.start()
        for mine, theirs in copies:
            mine.wait_send()
            theirs.wait_recv()

    sems = pltpu.SemaphoreType.DMA((nr * n,))
    return pl.pallas_call(
        body, out_shape=tuple(jax.ShapeDtypeStruct(t.shape, t.dtype) for t in lands), in_specs=[HBM_SPEC] * n,
        out_specs=(HBM_SPEC,) * n, input_output_aliases={i: i for i in range(n)}, scratch_shapes=[sems, sems], name=name)(*lands)


def _rs_chips_start(pair, name):
    _, r_dim, c_dim = pair.shape
    n = len(CHIP_FLIPS)

    def body(pair_ref, far_ref, send_sems, recv_sems, pair_thru, far_thru, token):
        for cp in _chip_copies(pair_ref, lambda k, me, peer: k + 1, far_ref, lambda k, me, peer: k, send_sems, recv_sems):
            cp.start()
        token[...] = jnp.zeros(TOKEN.shape, TOKEN.dtype)

    far = lax.empty((n, r_dim, c_dim), pair.dtype)
    return pl.pallas_call(
        body, name=name,
        out_shape=(pltpu.SemaphoreType.DMA((n,)), pltpu.SemaphoreType.DMA((n,)), pltpu.HBM(pair.shape, pair.dtype),
                   pltpu.HBM(far.shape, far.dtype), TOKEN),
        in_specs=(HBM_SPEC, HBM_SPEC),
        out_specs=(SEM_SPEC, SEM_SPEC, HBM_SPEC, HBM_SPEC, pl.BlockSpec(memory_space=pltpu.VMEM)),
        input_output_aliases={0: 2, 1: 3}, compiler_params=pltpu.CompilerParams(has_side_effects=DATAFLOW),
    )(pltpu.with_memory_space_constraint(pair, pltpu.HBM), pltpu.with_memory_space_constraint(far, pltpu.HBM))


def _rs_chips_wait(send_sems, recv_sems, pair, far, after, name):
    def body(pair_ref, far_ref, send_sems, recv_sems, after_ref, pair_out, far_out):
        for cp in _chip_copies(pair_ref, lambda k, me, peer: k + 1, far_ref, lambda k, me, peer: k, send_sems, recv_sems):
            cp.wait_send()
            cp.wait_recv()

    return pl.pallas_call(
        body, name=name, out_shape=(pltpu.HBM(pair.shape, pair.dtype), pltpu.HBM(far.shape, far.dtype)),
        in_specs=(HBM_SPEC, HBM_SPEC, SEM_SPEC, SEM_SPEC, ANY_SPEC),
        out_specs=(HBM_SPEC, HBM_SPEC), input_output_aliases={0: 0, 1: 1},
        compiler_params=pltpu.CompilerParams(has_side_effects=DATAFLOW),
    )(pair, far, send_sems, recv_sems, after)


def _rs_final_sum(pair, far, name):
    _, r_dim, c_dim = pair.shape
    tr = _tile(r_dim, 1200, SUBLANE)

    def body(p_ref, f0_ref, f1_ref, f2_ref, o_ref):
        o_ref[...] = ((p_ref[...].astype(F32) + f0_ref[...].astype(F32)) + f1_ref[...].astype(F32)) + f2_ref[...].astype(F32)

    def slot(k):
        return _bs((None, tr, c_dim), lambda i: (k, i, 0))

    return pl.pallas_call(
        body, out_shape=jax.ShapeDtypeStruct((r_dim, c_dim), F32), grid=(r_dim // tr,),
        in_specs=[slot(0), slot(0), slot(1), slot(2)], out_specs=_bs((tr, c_dim), lambda i: (i, 0)), name=name,
        compiler_params=_params("parallel"))(pair, far, far, far)


def _reduce_scatter_begin(g, name):
    recv = _rs_pair_exchange(g, name + "_pair")
    pair = _rs_pair_sum(g, recv, name + "_pairsum")
    return _rs_chips_start(pair, name + "_chips_start")


def _reduce_scatter_end(state, after, name):
    send_sems, recv_sems, pair, far, _ = state
    pair, far = _rs_chips_wait(send_sems, recv_sems, pair, far, after, name + "_chips_wait")
    return _rs_final_sum(pair, far, name + "_sum")


MATRICES = (("w_in", True), ("w_out", False), ("w_q", False), ("w_kv", True), ("w_o", False), ("w_up", True),
            ("w_down", False), ("w_conv_out", True), ("w_pool_grp", True))
MIX_NAMES = ("w_in", "w_conv_out", "w_pool_grp", "w_out")
REST_NAMES = ("w_q", "w_kv", "w_o", "w_up", "w_down")


def _parts(layer):
    return (("mix", MIX_NAMES), ("rest", REST_NAMES)) if layer == 0 else (("all", MIX_NAMES + REST_NAMES),)


def _to_rows(name, transposed, w, d_model):
    if name == "w_pool_grp":
        w = jnp.swapaxes(w, 1, 2)
    elif transposed:
        w = w.T
    return w.reshape(-1, d_model)


def _stored_form(name, transposed, rows, shard_shape):
    if name == "w_pool_grp":
        g, i, o = shard_shape
        return rows.reshape(g, o, i)
    if transposed:
        return rows.reshape(shard_shape[1], shard_shape[0])
    return rows.reshape(shard_shape)


def _scatter_blocks(name, full, shard_shape, d_model, n_dev=N_DEV):
    if name == "w_pool_grp":
        g, i, o = shard_shape
        return jnp.swapaxes(full.reshape(g, n_dev, o, i), 0, 1).reshape(n_dev, -1, d_model)
    return full.reshape(n_dev, -1, d_model)


def kernel(x, mem, mix_norm_g, w_in, conv_dw_w, conv_dw_b, conv_ln_g, conv_ln_b, w_conv_out, w_pool_grp, pool_scale, w_out, xattn_norm_g, mem_norm_g, w_q, w_kv, w_o, ffn_norm_g, w_up, ffn_dw_w, w_down, final_norm_g, loss_target, m_mix_norm_g, m_w_in, m_conv_dw_w, m_conv_dw_b, m_conv_ln_g, m_conv_ln_b, m_w_conv_out, m_w_pool_grp, m_pool_scale, m_w_out, m_xattn_norm_g, m_mem_norm_g, m_w_q, m_w_kv, m_w_o, m_ffn_norm_g, m_w_up, m_ffn_dw_w, m_w_down, m_final_norm_g, v_mix_norm_g, v_w_in, v_conv_dw_w, v_conv_dw_b, v_conv_ln_g, v_conv_ln_b, v_w_conv_out, v_w_pool_grp, v_pool_scale, v_w_out, v_xattn_norm_g, v_mem_norm_g, v_w_q, v_w_kv, v_w_o, v_ffn_norm_g, v_w_up, v_ffn_dw_w, v_w_down, v_final_norm_g):
    p = dict(locals())
    weight_names = ["mix_norm_g", "w_in", "conv_dw_w", "conv_dw_b", "conv_ln_g", "conv_ln_b", "w_conv_out",
                    "w_pool_grp", "pool_scale", "w_out", "xattn_norm_g", "mem_norm_g", "w_q", "w_kv", "w_o",
                    "ffn_norm_g", "w_up", "ffn_dw_w", "w_down", "final_norm_g"]
    n_batch, seq, d_model = x.shape
    m_len = mem.shape[1]
    depth = w_in.shape[0]
    assert depth == 2, "the exchange schedule below is written for two layers"
    t_dim = n_batch * seq
    c_conv = conv_dw_b.shape[1]
    n_groups = w_pool_grp.shape[1]
    assert w_pool_grp.shape[2] == LANE and c_conv % LANE == 0 and n_groups * LANE == c_conv
    gate_col0 = 2 * c_conv + n_groups * LANE
    pool_col0 = (2 * c_conv) // LANE

    dev = 4 * lax.axis_index("x") + 2 * lax.axis_index("y") + lax.axis_index("c")
    filt = jnp.concatenate([conv_dw_w.reshape(-1), ffn_dw_w.reshape(-1)])
    filt_rows = filt.reshape(-1, d_model)
    transposed = dict(MATRICES)
    layout = {part: [(name, transposed[name], _to_rows(name, transposed[name], p[name][0], d_model).shape[0])
                     for name in names] for l in range(depth) for part, names in _parts(l)}
    part_of = {(l, name): part for l in range(depth) for part, names in _parts(l) for name in names}

    def landing(name, shard):
        if name == "w_pool_grp":
            block, axis = jnp.swapaxes(shard, 1, 2), 1
        elif name == "filt":
            block, axis = shard, 0
        else:
            block, axis = (shard.T if transposed[name] else shard), 0
        block = block if name == "filt" else block.astype(BF16)
        rows = block.shape[axis]
        shape = block.shape[:axis] + (N_DEV * rows,) + block.shape[axis + 1:]
        start = (0,) * axis + (dev * rows,) + (0,) * (block.ndim - axis - 1)
        return lax.dynamic_update_slice(lax.empty(shape, block.dtype), block, start), (rows, axis)

    ag_state = {}
    after = filt_rows
    for l in range(depth):
        for part, names in _parts(l):
            items = [(name, p[name][l]) for name in names]
            if (l, part) == (0, part_of[(0, "w_in")]):
                items.append(("filt", filt_rows))
            lands, specs = zip(*[landing(name, shard) for name, shard in items])
            out = _ag_chips_start(lands, specs, after, f"ag{l}{part}_chips_start")
            ag_state[(l, part)] = ([name for name, _ in items], specs, out)
            after = out[-1]
    all_started = after

    full = [dict() for _ in range(depth)]

    def ensure(l, name, after):
        if name in full[l]:
            return
        part = part_of[(l, name)]
        names, specs, out = ag_state[(l, part)]
        lands = _ag_chips_wait(out[0], out[1], out[2:-1], specs, after, f"ag{l}{part}_chips_wait")
        lands = _ag_pair_forward(lands, specs, f"ag{l}{part}_pair_forward")
        full[l].update(zip(names, lands))

    vec = lambda a: a.reshape(1, -1)
    x2d = x.reshape(t_dim, d_model)
    mem2d = mem.reshape(n_batch * m_len, d_model)
    mem_n = _rmsnorm_fwd(mem2d, vec(mem_norm_g), "mem_norm", after=all_started)
    h_first = _rmsnorm_fwd(x2d, vec(mix_norm_g[0]), "mix_norm_l0", after=mem_n)
    ensure(0, "w_in", h_first)
    filt_all = full[0]["filt"].reshape(N_DEV, -1)
    n_cw = conv_dw_w.size
    kc, cs = conv_dw_w.shape[1:]
    kf, fs = ffn_dw_w.shape[1:]
    conv_w_full = jnp.moveaxis(filt_all[:, :n_cw].reshape(N_DEV, depth, kc, cs), 0, 2).reshape(depth, kc, N_DEV * cs)
    ffn_w_full = jnp.moveaxis(filt_all[:, n_cw:].reshape(N_DEV, depth, kf, fs), 0, 2).reshape(depth, kf, N_DEV * fs)

    saved = []
    xc = x2d
    for l in range(depth):
        ensure(l, "w_in", xc)
        wl = full[l]
        s = {"x0": xc}
        s["h"] = h_first if l == 0 else _rmsnorm_fwd(xc, vec(mix_norm_g[l]), f"mix_norm_l{l}")
        s["proj"] = _matmul(s["h"], wl["w_in"], "nt", f"in_proj_l{l}", out_dtype=BF16)
        s["y1"] = _glu_conv_fwd(s["proj"], conv_w_full[l], vec(conv_dw_b[l]), n_batch, seq, f"glu_conv_l{l}")
        s["y3"], s["yc"] = _ln_silu_matmul(s["y1"], vec(conv_ln_g[l]), vec(conv_ln_b[l]), wl["w_conv_out"],
                                           f"ln_silu_conv_out_l{l}")
        s["zp"] = _pool_fwd(s["proj"], pool_col0, n_groups, n_batch, seq, f"pool_l{l}")
        s["yp"] = _grouped(s["zp"], wl["w_pool_grp"], "nt", f"pool_proj_l{l}", out_dtype=BF16)
        s["merged"] = _merge_fwd(s["proj"], gate_col0, s["yc"], s["yp"], vec(pool_scale[l]), f"merge_l{l}")
        s["x1"] = _matmul(s["merged"], wl["w_out"], "nn", f"mix_out_l{l}", res=xc)
        ensure(l, "w_q", s["x1"])
        half_up = wl["w_up"].shape[0] // 2
        up_gate, up_val = (0, half_up), (half_up, half_up)
        s["hq"] = _rmsnorm_fwd(s["x1"], vec(xattn_norm_g[l]), f"xattn_norm_l{l}")
        s["q"] = _matmul(s["hq"], wl["w_q"], "nn", f"q_proj_l{l}", out_dtype=BF16)
        s["kv"] = _matmul(mem_n, wl["w_kv"], "nt", f"kv_proj_l{l}", out_dtype=BF16)
        s["att"] = _attn_fwd(s["q"], s["kv"], n_batch, seq, m_len, f"attn_l{l}")
        s["x2"] = _matmul(s["att"], wl["w_o"], "nn", f"attn_out_l{l}", res=s["x1"])
        s["hf"] = _rmsnorm_fwd(s["x2"], vec(ffn_norm_g[l]), f"ffn_norm_l{l}")
        s["up_g"] = _matmul(s["hf"], wl["w_up"], "nt", f"up_proj_gate_l{l}", out_dtype=BF16, b_window=up_gate)
        s["up_v"] = _matmul(s["hf"], wl["w_up"], "nt", f"up_proj_val_l{l}", out_dtype=BF16, b_window=up_val)
        s["act"] = _ffn_act_fwd(s["up_g"], s["up_v"], ffn_w_full[l], n_batch, seq, f"ffn_act_l{l}")
        xc = _matmul(s["act"], wl["w_down"], "nn", f"down_proj_l{l}", res=s["x2"])
        saved.append(s)

    dx, dxb, dg_final, loss_part = _loss_head(xc, vec(final_norm_g), loss_target.reshape(t_dim, d_model), "loss_head")

    small = {"final_norm_g": dg_final.reshape(-1)}
    big = [dict() for _ in range(depth)]
    rs_state = {}
    rs_after = loss_part

    def rs_begin(l, part):
        pack = lax.empty((N_DEV, sum(nrows for _, _, nrows in layout[part]), d_model), BF16)
        row0 = 0
        for name, _, nrows in layout[part]:
            pieces = big[l][name] if isinstance(big[l][name], tuple) else (big[l][name],)
            d0 = 0
            for piece in pieces:
                blocks = _scatter_blocks(name, piece, p[name].shape[1:], d_model, N_DEV // len(pieces)).astype(BF16)
                pack = lax.dynamic_update_slice(pack, blocks, (d0, row0, 0))
                d0 += blocks.shape[0]
            row0 += nrows
        rs_state[(l, part)] = _reduce_scatter_begin(pack, f"rs{l}{part}")
        return rs_state[(l, part)][4]

    dmem_n = None
    for l in reversed(range(depth)):
        wl, s = full[l], saved[l]
        sm = {}
        dact = _matmul(dxb, wl["w_down"], "nt", f"d_act_l{l}", out_dtype=BF16, after=rs_after)
        big[l]["w_down"] = _matmul(s["act"], dxb, "tn", f"d_w_down_l{l}", out_dtype=BF16)
        dup_g, dup_v, dwf_g, dwf_v = _ffn_act_bwd(s["up_g"], s["up_v"], ffn_w_full[l], dact, n_batch, seq,
                                                  f"ffn_act_bwd_l{l}")
        sm["ffn_dw_w"] = jnp.concatenate([dwf_g, dwf_v], axis=1)
        dx, dxb, dg = _matmul_rmsnorm_bwd((dup_g, dup_v), wl["w_up"], "nn", s["x2"], vec(ffn_norm_g[l]), dx,
                                          f"d_hf_ffn_norm_bwd_l{l}")
        big[l]["w_up"] = (_matmul(dup_g, s["hf"], "tn", f"d_w_up_gate_l{l}", out_dtype=BF16),
                          _matmul(dup_v, s["hf"], "tn", f"d_w_up_val_l{l}", out_dtype=BF16))
        sm["ffn_norm_g"] = dg
        datt = _matmul(dxb, wl["w_o"], "nt", f"d_att_l{l}", out_dtype=BF16, after=rs_after)
        big[l]["w_o"] = _matmul(s["att"], dxb, "tn", f"d_w_o_l{l}", out_dtype=BF16)
        dq, dk, dv = _attn_bwd(s["q"], s["kv"], datt, n_batch, seq, m_len, f"attn_bwd_l{l}")
        dkv = jnp.concatenate([dk, dv], axis=1)
        big[l]["w_kv"] = _matmul(dkv, mem_n, "tn", f"d_w_kv_l{l}", out_dtype=BF16)
        dmem_n = _matmul(dkv, wl["w_kv"], "nn", f"d_mem_l{l}", res=dmem_n)
        big[l]["w_q"] = _matmul(s["hq"], dq, "tn", f"d_w_q_l{l}", out_dtype=BF16)
        dx, dxb, dg = _matmul_rmsnorm_bwd(dq, wl["w_q"], "nt", s["x1"], vec(xattn_norm_g[l]), dx,
                                          f"d_hq_xattn_norm_bwd_l{l}")
        sm["xattn_norm_g"] = dg
        if part_of[(l, "w_q")] != part_of[(l, "w_in")]:
            rs_after = rs_begin(l, part_of[(l, "w_q")])
        dmerged = _matmul(dxb, wl["w_out"], "nt", f"d_merged_l{l}", out_dtype=BF16, after=rs_after)
        big[l]["w_out"] = _matmul(s["merged"], dxb, "tn", f"d_w_out_l{l}", out_dtype=BF16)
        dgc, dgp, dyc, dyp, dscale = _merge_bwd(s["proj"], gate_col0, s["yc"], s["yp"], vec(pool_scale[l]), dmerged,
                                                f"merge_bwd_l{l}")
        sm["pool_scale"] = dscale
        dzp = _grouped(dyp, wl["w_pool_grp"], "nn", f"d_zp_l{l}", out_dtype=BF16)
        big[l]["w_pool_grp"] = _grouped_tn(dyp, s["zp"], n_groups, f"d_w_pool_l{l}")
        du = _pool_bwd(dzp, n_groups, n_batch, seq, f"pool_bwd_l{l}")
        dy3 = _matmul(dyc, wl["w_conv_out"], "nn", f"d_y3_l{l}", out_dtype=BF16)
        big[l]["w_conv_out"] = _matmul(dyc, s["y3"], "tn", f"d_w_conv_out_l{l}", out_dtype=BF16)
        dy1, dlg, dlb = _ln_silu_bwd(s["y1"], vec(conv_ln_g[l]), vec(conv_ln_b[l]), dy3, f"ln_silu_bwd_l{l}")
        sm["conv_ln_g"], sm["conv_ln_b"] = dlg, dlb
        da, dgl, dcw, dcb = _glu_conv_bwd(s["proj"], conv_w_full[l], dy1, n_batch, seq, f"glu_conv_bwd_l{l}")
        sm["conv_dw_w"], sm["conv_dw_b"] = dcw, dcb
        dproj = jnp.concatenate([da, dgl, du, dgc, dgp], axis=1)
        big[l]["w_in"] = _matmul(dproj, s["h"], "tn", f"d_w_in_l{l}", out_dtype=BF16)
        dx, dxb, dg = _matmul_rmsnorm_bwd(dproj, wl["w_in"], "nn", s["x0"], vec(mix_norm_g[l]), dx,
                                          f"d_h_mix_norm_bwd_l{l}")
        sm["mix_norm_g"] = dg
        for k, val in sm.items():
            small[(l, k)] = val.reshape(-1)
        rs_after = rs_begin(l, part_of[(l, "w_in")])
    _, _, dg_mem = _rmsnorm_bwd(mem2d, vec(mem_norm_g), dmem_n, None, "mem_norm_bwd")
    small["mem_norm_g"] = dg_mem.reshape(-1)
    small["loss"] = loss_part.reshape(-1)

    grads = {}
    per_layer = {name: [None] * depth for name, _ in MATRICES}
    for l in reversed(range(depth)):
        for part, _ in reversed(_parts(l)):
            mat_grads = _reduce_scatter_end(rs_state[(l, part)], rs_after, f"rs{l}{part}")
            row0 = 0
            for name, tr, nrows in layout[part]:
                per_layer[name][l] = _stored_form(name, tr, mat_grads[row0:row0 + nrows], p[name].shape[1:])
                row0 += nrows
    flip = lambda t: jnp.swapaxes(t, -1, -2)
    stored_grads = {name: jnp.stack(per_layer[name]) for name, _ in MATRICES}
    for name, tr in MATRICES:
        grads[name] = flip(stored_grads[name]) if tr else stored_grads[name]

    keys = list(small.keys())
    flat = jnp.concatenate([small[k] for k in keys])
    n_small = flat.shape[0]
    rows_small = -(-n_small // (SUBLANE * d_model)) * SUBLANE
    flat = jnp.pad(flat, (0, rows_small * d_model - n_small)).reshape(rows_small, d_model)
    every = _all_gather(flat, "small_all_gather")
    total = _sum_rows([every[i] for i in range(N_DEV)], F32, "small_sum").reshape(-1)
    off = 0
    red = {}
    for k in keys:
        red[k] = total[off:off + small[k].shape[0]]
        off += small[k].shape[0]
    loss = red["loss"][0]
    for name in ("mix_norm_g", "conv_dw_b", "conv_ln_g", "conv_ln_b", "pool_scale", "xattn_norm_g", "ffn_norm_g"):
        grads[name] = jnp.stack([red[(l, name)] for l in range(depth)])
    grads["conv_dw_w"] = jnp.stack([
        lax.dynamic_slice_in_dim(red[(l, "conv_dw_w")].reshape(kc, N_DEV * cs), dev * cs, cs, axis=1)
        for l in range(depth)])
    grads["ffn_dw_w"] = jnp.stack([
        lax.dynamic_slice_in_dim(red[(l, "ffn_dw_w")].reshape(kf, N_DEV * fs), dev * fs, fs, axis=1)
        for l in range(depth)])
    grads["mem_norm_g"] = red["mem_norm_g"]
    grads["final_norm_g"] = red["final_norm_g"]

    deltas, new_m, new_v = {}, {}, {}
    for name in weight_names:
        if transposed.get(name, False):
            out = _adamw(flip(p[name]), stored_grads[name], flip(p["m_" + name]), flip(p["v_" + name]), f"adamw_{name}")
            deltas[name], new_m[name], new_v[name] = (flip(t) for t in out)
        else:
            deltas[name], new_m[name], new_v[name] = _adamw(p[name], grads[name], p["m_" + name], p["v_" + name],
                                                            f"adamw_{name}")
    grad_x = dx.reshape(n_batch, seq, d_model)
    return (loss, grad_x, *[grads[n] for n in weight_names], *[deltas[n] for n in weight_names],
            *[new_m[n] for n in weight_names], *[new_v[n] for n in weight_names])
```
